```python
import math
import jax
import jax.numpy as jnp
from jax import lax
import numpy as np

D_MODEL = 2048
BATCH = 8
SEQ = 4096
DEPTH = 1

HEAD_DIM = 64
N_Q_HEADS = 16
N_KV_HEADS = 4
GQA_GROUP = N_Q_HEADS // N_KV_HEADS
ATTN_WIDTH = N_Q_HEADS * HEAD_DIM
KV_WIDTH = N_KV_HEADS * HEAD_DIM
WINDOW = 128
BLOCK = 128
NUM_BUCKETS = 32
MAX_DISTANCE = 128
NEG_INF = -1e30
SSM_WIDTH = D_MODEL // 4
SSM_GROUP_CH = 16
SSM_GROUPS = SSM_WIDTH // SSM_GROUP_CH
SSM_STATE = 64
D_FF = 4 * D_MODEL
N_BRANCHES = 2
IN_WIDTH = ATTN_WIDTH + 2 * KV_WIDTH + SSM_WIDTH + N_BRANCHES * D_MODEL
N_MOD = 6
EPS = 1e-6

kernel_name = "hybrid_swa_sink_s5_gated_adaln_block"


def _rmsnorm(x, g):
    xf = x.astype(jnp.float32)
    y = xf * lax.rsqrt(jnp.mean(xf * xf, axis=-1, keepdims=True) + EPS)
    return (y * g.astype(jnp.float32)).astype(x.dtype)


def _modulate(h, shift, scale):
    return h * (1 + scale[:, None, :]) + shift[:, None, :]


def _t5_buckets_block():
    qi = np.arange(BLOCK)[:, None]
    ki = np.arange(2 * BLOCK)[None, :]
    n = np.maximum(qi + BLOCK - ki, 0)
    max_exact = NUM_BUCKETS // 2
    large = max_exact + (np.log(np.maximum(n, 1) / max_exact)
                         / np.log(MAX_DISTANCE / max_exact)
                         * (NUM_BUCKETS - max_exact)).astype(np.int32)
    large = np.minimum(large, NUM_BUCKETS - 1)
    return np.where(n < max_exact, n, large).astype(np.int32)


def _sliding_window_attention(q, k, v, sinks, bias):
    b, s, _ = q.shape
    nb = s // BLOCK
    q = q.reshape(b, nb, BLOCK, N_KV_HEADS, GQA_GROUP, HEAD_DIM)
    pad = ((0, 0), (BLOCK, 0), (0, 0))
    kp = jnp.pad(k, pad).reshape(b, nb + 1, BLOCK, N_KV_HEADS, HEAD_DIM)
    vp = jnp.pad(v, pad).reshape(b, nb + 1, BLOCK, N_KV_HEADS, HEAD_DIM)
    kk = jnp.concatenate([kp[:, :-1], kp[:, 1:]], axis=2)
    vv = jnp.concatenate([vp[:, :-1], vp[:, 1:]], axis=2)
    scores = jnp.einsum('bnqhgd,bnkhd->bnhgqk', q, kk).astype(jnp.float32)
    scores = scores * (HEAD_DIM ** -0.5) + bias
    qi = jnp.arange(BLOCK)[:, None]
    ki = jnp.arange(2 * BLOCK)[None, :]
    dist = qi + BLOCK - ki
    band = (dist >= 0) & (dist < WINDOW)
    blk = jnp.arange(nb)[:, None, None]
    valid = band[None] & (blk * BLOCK + ki[None] - BLOCK >= 0)
    scores = jnp.where(valid[None, :, None, None], scores, NEG_INF)
    sink = sinks.astype(jnp.float32).reshape(N_KV_HEADS, GQA_GROUP, 1)
    m = jnp.maximum(jnp.max(scores, axis=-1), sink)
    p = jnp.exp(scores - m[..., None])
    denom = jnp.sum(p, axis=-1) + jnp.exp(sink - m)
    p = (p / denom[..., None]).astype(vv.dtype)
    o = jnp.einsum('bnhgqk,bnkhd->bnqhgd', p, vv)
    return o.reshape(b, s, ATTN_WIDTH)


def _ssm_combine(e1, e2):
    (a1r, a1i), (b1r, b1i) = e1
    (a2r, a2i), (b2r, b2i) = e2
    a_new = (a1r * a2r - a1i * a2i, a1r * a2i + a1i * a2r)
    b_new = (a2r * b1r - a2i * b1i + b2r, a2r * b1i + a2i * b1r + b2i)
    return (a_new, b_new)


def _s5_ssm(u, lambda_re, lambda_im, log_step, b_re, b_im, c_re, c_im, d_skip):
    bsz, s, _ = u.shape
    f32 = jnp.float32
    uf = u.astype(f32).reshape(bsz, s, SSM_GROUPS, SSM_GROUP_CH)
    lam_re = jnp.minimum(lambda_re.astype(f32), -1e-4)
    lam_im = lambda_im.astype(f32)
    delta = jnp.exp(log_step.astype(f32))[:, None]
    mag = jnp.exp(lam_re * delta)
    ang = lam_im * delta
    abar_re, abar_im = mag * jnp.cos(ang), mag * jnp.sin(ang)
    num_re, num_im = abar_re - 1.0, abar_im
    den = lam_re * lam_re + lam_im * lam_im
    f_re = (num_re * lam_re + num_im * lam_im) / den
    f_im = (num_im * lam_re - num_re * lam_im) / den
    br, bi = b_re.astype(f32), b_im.astype(f32)
    bbar_re = f_re[..., None] * br - f_im[..., None] * bi
    bbar_im = f_re[..., None] * bi + f_im[..., None] * br
    bu_re = jnp.einsum('bsgp,gnp->bsgn', uf, bbar_re)
    bu_im = jnp.einsum('bsgp,gnp->bsgn', uf, bbar_im)
    shape_a = (1, s, SSM_GROUPS, SSM_STATE)
    a_re = jnp.broadcast_to(abar_re, shape_a)
    a_im = jnp.broadcast_to(abar_im, shape_a)
    _, (x_re, x_im) = lax.associative_scan(
        _ssm_combine, ((a_re, a_im), (bu_re, bu_im)), axis=1)
    y = (jnp.einsum('bsgn,gpn->bsgp', x_re, c_re.astype(f32))
         - jnp.einsum('bsgn,gpn->bsgp', x_im, c_im.astype(f32))
         + d_skip.astype(f32).reshape(SSM_GROUPS, SSM_GROUP_CH) * uf)
    return y.reshape(bsz, s, SSM_WIDTH).astype(u.dtype)


def _fwd_setup_inputs(seed: int = 0) -> dict:
    key = jax.random.key(seed)
    ks = jax.random.split(key, 32)

    def nrm(k, shape, scale):
        return jax.random.normal(k, shape, jnp.float32) * scale

    G, N, P = SSM_GROUPS, SSM_STATE, SSM_GROUP_CH
    lam_im0 = jnp.pi * jnp.arange(N, dtype=jnp.float32)
    return {
        "x": nrm(ks[0], (BATCH, SEQ, D_MODEL), 1.0),
        "c": nrm(ks[1], (BATCH, D_MODEL), 1.0),
        "w_ada": nrm(ks[2], (DEPTH, D_MODEL, N_MOD * D_MODEL), 0.5 * D_MODEL ** -0.5),
        "b_ada": nrm(ks[3], (DEPTH, N_MOD * D_MODEL), 0.02),
        "norm1_g": 1.0 + nrm(ks[4], (DEPTH, D_MODEL), 0.02),
        "w_in": nrm(ks[5], (DEPTH, D_MODEL, IN_WIDTH), D_MODEL ** -0.5),
        "b_in": nrm(ks[6], (DEPTH, IN_WIDTH), 0.02),
        "attn_sinks": nrm(ks[7], (DEPTH, N_Q_HEADS), 0.5),
        "rel_bias": nrm(ks[8], (NUM_BUCKETS, N_Q_HEADS), 0.1),
        "lambda_re": -0.5 + nrm(ks[9], (DEPTH, G, N), 0.01),
        "lambda_im": lam_im0 + nrm(ks[10], (DEPTH, G, N), 0.01),
        "log_step": jax.random.uniform(ks[11], (DEPTH, G), jnp.float32,
                                       minval=math.log(1e-3), maxval=math.log(1e-1)),
        "ssm_b_re": nrm(ks[12], (DEPTH, G, N, P), (2 * P) ** -0.5),
        "ssm_b_im": nrm(ks[13], (DEPTH, G, N, P), (2 * P) ** -0.5),
        "ssm_c_re": nrm(ks[14], (DEPTH, G, P, N), (2 * N) ** -0.5),
        "ssm_c_im": nrm(ks[15], (DEPTH, G, P, N), (2 * N) ** -0.5),
        "ssm_d": nrm(ks[16], (DEPTH, SSM_WIDTH), 1.0),
        "w_glu": nrm(ks[17], (DEPTH, SSM_WIDTH, SSM_WIDTH), SSM_WIDTH ** -0.5),
        "b_glu": nrm(ks[18], (DEPTH, SSM_WIDTH), 0.02),
        "w_attn_proj": nrm(ks[19], (DEPTH, ATTN_WIDTH, D_MODEL), ATTN_WIDTH ** -0.5),
        "w_ssm_proj": nrm(ks[20], (DEPTH, SSM_WIDTH, D_MODEL), SSM_WIDTH ** -0.5),
        "w_out": nrm(ks[21], (DEPTH, D_MODEL, D_MODEL), D_MODEL ** -0.5),
        "norm2_g": 1.0 + nrm(ks[22], (DEPTH, D_MODEL), 0.02),
        "w_ff1": nrm(ks[23], (DEPTH, D_MODEL, D_FF), D_MODEL ** -0.5),
        "w_ff2": nrm(ks[24], (DEPTH, D_FF, D_MODEL), D_FF ** -0.5),
        "final_g": 1.0 + nrm(ks[25], (D_MODEL,), 0.02),
    }


def _fwd_reference(x, c, w_ada, b_ada, norm1_g, w_in, b_in, attn_sinks, rel_bias,
              lambda_re, lambda_im, log_step, ssm_b_re, ssm_b_im, ssm_c_re,
              ssm_c_im, ssm_d, w_glu, b_glu, w_attn_proj, w_ssm_proj, w_out,
              norm2_g, w_ff1, w_ff2, final_g):
    bsz, s, _ = x.shape
    buckets = jnp.asarray(_t5_buckets_block())
    bias = rel_bias.astype(jnp.float32)[buckets]
    bias = jnp.transpose(bias, (2, 0, 1)).reshape(N_KV_HEADS, GQA_GROUP, BLOCK, 2 * BLOCK)
    splits = [ATTN_WIDTH, ATTN_WIDTH + KV_WIDTH, ATTN_WIDTH + 2 * KV_WIDTH,
              ATTN_WIDTH + 2 * KV_WIDTH + SSM_WIDTH,
              ATTN_WIDTH + 2 * KV_WIDTH + SSM_WIDTH + D_MODEL]
    cs = jax.nn.silu(c)
    for l in range(DEPTH):
        mod = cs @ w_ada[l] + b_ada[l]
        sh1, sc1, g1, sh2, sc2, g2 = jnp.split(mod, N_MOD, axis=-1)
        h = _modulate(_rmsnorm(x, norm1_g[l]), sh1, sc1)
        proj = h @ w_in[l] + b_in[l]
        q, k, v, u, gate_a, gate_s = jnp.split(proj, splits, axis=-1)
        attn = _sliding_window_attention(q, k, v, attn_sinks[l], bias)
        y_attn = attn @ w_attn_proj[l]
        y = _s5_ssm(u, lambda_re[l], lambda_im[l], log_step[l], ssm_b_re[l],
                    ssm_b_im[l], ssm_c_re[l], ssm_c_im[l], ssm_d[l])
        z = jax.nn.gelu(y)
        z = z * jax.nn.sigmoid(z @ w_glu[l] + b_glu[l])
        y_ssm = z @ w_ssm_proj[l]
        merged = jax.nn.sigmoid(gate_a) * y_attn + jax.nn.sigmoid(gate_s) * y_ssm
        x = x + g1[:, None, :] * (merged @ w_out[l])
        h2 = _modulate(_rmsnorm(x, norm2_g[l]), sh2, sc2)
        ff = jnp.square(jax.nn.relu(h2 @ w_ff1[l])) @ w_ff2[l]
        x = x + g2[:, None, :] * ff
    return _rmsnorm(x, final_g)


import jax as _jax
import jax.numpy as _jnp

TWIN_FORMAT = 'train_step'
FWD_PARAMS = ['x', 'c', 'w_ada', 'b_ada', 'norm1_g', 'w_in', 'b_in', 'attn_sinks', 'rel_bias', 'lambda_re', 'lambda_im', 'log_step', 'ssm_b_re', 'ssm_b_im', 'ssm_c_re', 'ssm_c_im', 'ssm_d', 'w_glu', 'b_glu', 'w_attn_proj', 'w_ssm_proj', 'w_out', 'norm2_g', 'w_ff1', 'w_ff2', 'final_g']
TWIN_WEIGHTS = ['w_ada', 'b_ada', 'norm1_g', 'w_in', 'b_in', 'attn_sinks', 'rel_bias', 'lambda_re', 'lambda_im', 'log_step', 'ssm_b_re', 'ssm_b_im', 'ssm_c_re', 'ssm_c_im', 'ssm_d', 'w_glu', 'b_glu', 'w_attn_proj', 'w_ssm_proj', 'w_out', 'norm2_g', 'w_ff1', 'w_ff2', 'final_g']
TWIN_DIFF_INPUT = 'x'
TWIN_INPUTS = ['x', 'c', 'w_ada', 'b_ada', 'norm1_g', 'w_in', 'b_in', 'attn_sinks', 'rel_bias', 'lambda_re', 'lambda_im', 'log_step', 'ssm_b_re', 'ssm_b_im', 'ssm_c_re', 'ssm_c_im', 'ssm_d', 'w_glu', 'b_glu', 'w_attn_proj', 'w_ssm_proj', 'w_out', 'norm2_g', 'w_ff1', 'w_ff2', 'final_g', 'loss_target', 'm_w_ada', 'm_b_ada', 'm_norm1_g', 'm_w_in', 'm_b_in', 'm_attn_sinks', 'm_rel_bias', 'm_lambda_re', 'm_lambda_im', 'm_log_step', 'm_ssm_b_re', 'm_ssm_b_im', 'm_ssm_c_re', 'm_ssm_c_im', 'm_ssm_d', 'm_w_glu', 'm_b_glu', 'm_w_attn_proj', 'm_w_ssm_proj', 'm_w_out', 'm_norm2_g', 'm_w_ff1', 'm_w_ff2', 'm_final_g', 'v_w_ada', 'v_b_ada', 'v_norm1_g', 'v_w_in', 'v_b_in', 'v_attn_sinks', 'v_rel_bias', 'v_lambda_re', 'v_lambda_im', 'v_log_step', 'v_ssm_b_re', 'v_ssm_b_im', 'v_ssm_c_re', 'v_ssm_c_im', 'v_ssm_d', 'v_w_glu', 'v_b_glu', 'v_w_attn_proj', 'v_w_ssm_proj', 'v_w_out', 'v_norm2_g', 'v_w_ff1', 'v_w_ff2', 'v_final_g']
TWIN_OUTPUTS = ['loss', 'grad_x', 'grad_w_ada', 'grad_b_ada', 'grad_norm1_g', 'grad_w_in', 'grad_b_in', 'grad_attn_sinks', 'grad_rel_bias', 'grad_lambda_re', 'grad_lambda_im', 'grad_log_step', 'grad_ssm_b_re', 'grad_ssm_b_im', 'grad_ssm_c_re', 'grad_ssm_c_im', 'grad_ssm_d', 'grad_w_glu', 'grad_b_glu', 'grad_w_attn_proj', 'grad_w_ssm_proj', 'grad_w_out', 'grad_norm2_g', 'grad_w_ff1', 'grad_w_ff2', 'grad_final_g', 'delta_w_ada', 'delta_b_ada', 'delta_norm1_g', 'delta_w_in', 'delta_b_in', 'delta_attn_sinks', 'delta_rel_bias', 'delta_lambda_re', 'delta_lambda_im', 'delta_log_step', 'delta_ssm_b_re', 'delta_ssm_b_im', 'delta_ssm_c_re', 'delta_ssm_c_im', 'delta_ssm_d', 'delta_w_glu', 'delta_b_glu', 'delta_w_attn_proj', 'delta_w_ssm_proj', 'delta_w_out', 'delta_norm2_g', 'delta_w_ff1', 'delta_w_ff2', 'delta_final_g', 'new_m_w_ada', 'new_m_b_ada', 'new_m_norm1_g', 'new_m_w_in', 'new_m_b_in', 'new_m_attn_sinks', 'new_m_rel_bias', 'new_m_lambda_re', 'new_m_lambda_im', 'new_m_log_step', 'new_m_ssm_b_re', 'new_m_ssm_b_im', 'new_m_ssm_c_re', 'new_m_ssm_c_im', 'new_m_ssm_d', 'new_m_w_glu', 'new_m_b_glu', 'new_m_w_attn_proj', 'new_m_w_ssm_proj', 'new_m_w_out', 'new_m_norm2_g', 'new_m_w_ff1', 'new_m_w_ff2', 'new_m_final_g', 'new_v_w_ada', 'new_v_b_ada', 'new_v_norm1_g', 'new_v_w_in', 'new_v_b_in', 'new_v_attn_sinks', 'new_v_rel_bias', 'new_v_lambda_re', 'new_v_lambda_im', 'new_v_log_step', 'new_v_ssm_b_re', 'new_v_ssm_b_im', 'new_v_ssm_c_re', 'new_v_ssm_c_im', 'new_v_ssm_d', 'new_v_w_glu', 'new_v_b_glu', 'new_v_w_attn_proj', 'new_v_w_ssm_proj', 'new_v_w_out', 'new_v_norm2_g', 'new_v_w_ff1', 'new_v_w_ff2', 'new_v_final_g']
TWIN_LEAF_KINDS = {'loss': 'loss', 'grad_x': 'grad_x', 'grad_w_ada': 'grad_w', 'grad_b_ada': 'grad_w', 'grad_norm1_g': 'grad_w', 'grad_w_in': 'grad_w', 'grad_b_in': 'grad_w', 'grad_attn_sinks': 'grad_w', 'grad_rel_bias': 'grad_w', 'grad_lambda_re': 'grad_w', 'grad_lambda_im': 'grad_w', 'grad_log_step': 'grad_w', 'grad_ssm_b_re': 'grad_w', 'grad_ssm_b_im': 'grad_w', 'grad_ssm_c_re': 'grad_w', 'grad_ssm_c_im': 'grad_w', 'grad_ssm_d': 'grad_w', 'grad_w_glu': 'grad_w', 'grad_b_glu': 'grad_w', 'grad_w_attn_proj': 'grad_w', 'grad_w_ssm_proj': 'grad_w', 'grad_w_out': 'grad_w', 'grad_norm2_g': 'grad_w', 'grad_w_ff1': 'grad_w', 'grad_w_ff2': 'grad_w', 'grad_final_g': 'grad_w', 'delta_w_ada': 'delta_w', 'delta_b_ada': 'delta_w', 'delta_norm1_g': 'delta_w', 'delta_w_in': 'delta_w', 'delta_b_in': 'delta_w', 'delta_attn_sinks': 'delta_w', 'delta_rel_bias': 'delta_w', 'delta_lambda_re': 'delta_w', 'delta_lambda_im': 'delta_w', 'delta_log_step': 'delta_w', 'delta_ssm_b_re': 'delta_w', 'delta_ssm_b_im': 'delta_w', 'delta_ssm_c_re': 'delta_w', 'delta_ssm_c_im': 'delta_w', 'delta_ssm_d': 'delta_w', 'delta_w_glu': 'delta_w', 'delta_b_glu': 'delta_w', 'delta_w_attn_proj': 'delta_w', 'delta_w_ssm_proj': 'delta_w', 'delta_w_out': 'delta_w', 'delta_norm2_g': 'delta_w', 'delta_w_ff1': 'delta_w', 'delta_w_ff2': 'delta_w', 'delta_final_g': 'delta_w', 'new_m_w_ada': 'new_m', 'new_m_b_ada': 'new_m', 'new_m_norm1_g': 'new_m', 'new_m_w_in': 'new_m', 'new_m_b_in': 'new_m', 'new_m_attn_sinks': 'new_m', 'new_m_rel_bias': 'new_m', 'new_m_lambda_re': 'new_m', 'new_m_lambda_im': 'new_m', 'new_m_log_step': 'new_m', 'new_m_ssm_b_re': 'new_m', 'new_m_ssm_b_im': 'new_m', 'new_m_ssm_c_re': 'new_m', 'new_m_ssm_c_im': 'new_m', 'new_m_ssm_d': 'new_m', 'new_m_w_glu': 'new_m', 'new_m_b_glu': 'new_m', 'new_m_w_attn_proj': 'new_m', 'new_m_w_ssm_proj': 'new_m', 'new_m_w_out': 'new_m', 'new_m_norm2_g': 'new_m', 'new_m_w_ff1': 'new_m', 'new_m_w_ff2': 'new_m', 'new_m_final_g': 'new_m', 'new_v_w_ada': 'new_v', 'new_v_b_ada': 'new_v', 'new_v_norm1_g': 'new_v', 'new_v_w_in': 'new_v', 'new_v_b_in': 'new_v', 'new_v_attn_sinks': 'new_v', 'new_v_rel_bias': 'new_v', 'new_v_lambda_re': 'new_v', 'new_v_lambda_im': 'new_v', 'new_v_log_step': 'new_v', 'new_v_ssm_b_re': 'new_v', 'new_v_ssm_b_im': 'new_v', 'new_v_ssm_c_re': 'new_v', 'new_v_ssm_c_im': 'new_v', 'new_v_ssm_d': 'new_v', 'new_v_w_glu': 'new_v', 'new_v_b_glu': 'new_v', 'new_v_w_attn_proj': 'new_v', 'new_v_w_ssm_proj': 'new_v', 'new_v_w_out': 'new_v', 'new_v_norm2_g': 'new_v', 'new_v_w_ff1': 'new_v', 'new_v_w_ff2': 'new_v', 'new_v_final_g': 'new_v'}


def _forward(args):
    return _fwd_reference(*[args[k] for k in FWD_PARAMS])


def _output_shape():
    def fwd():
        inp = _fwd_setup_inputs(0)
        return _fwd_reference(*[inp[k] for k in FWD_PARAMS])
    out = _jax.eval_shape(fwd)
    return out.shape, out.dtype

N_MICROBATCH = 1
ADAM_LR = 0.001
ADAM_B1 = 0.9
ADAM_B2 = 0.999
ADAM_EPS = 1e-08
ADAM_WD = 0.01
ADAM_STEP = 10
PER_EXAMPLE_BATCH_AXIS = {'x': 0, 'c': 0, 'loss_target': 0}
SHARED_INPUTS = []
_WEIGHT_DTYPES = {'w_ada': _jnp.float32, 'b_ada': _jnp.float32, 'norm1_g': _jnp.float32, 'w_in': _jnp.float32, 'b_in': _jnp.float32, 'attn_sinks': _jnp.float32, 'rel_bias': _jnp.float32, 'lambda_re': _jnp.float32, 'lambda_im': _jnp.float32, 'log_step': _jnp.float32, 'ssm_b_re': _jnp.float32, 'ssm_b_im': _jnp.float32, 'ssm_c_re': _jnp.float32, 'ssm_c_im': _jnp.float32, 'ssm_d': _jnp.float32, 'w_glu': _jnp.float32, 'b_glu': _jnp.float32, 'w_attn_proj': _jnp.float32, 'w_ssm_proj': _jnp.float32, 'w_out': _jnp.float32, 'norm2_g': _jnp.float32, 'w_ff1': _jnp.float32, 'w_ff2': _jnp.float32, 'final_g': _jnp.float32}
MOMENT_SCALE = {'w_ada': 3.979926e-02, 'b_ada': 7.524203e-02, 'norm1_g': 8.848333e-03, 'w_in': 5.788354e-03, 'b_in': 1.064791e-02, 'attn_sinks': 3.813383e-03, 'rel_bias': 5.726433e-03, 'lambda_re': 9.538010e-04, 'lambda_im': 9.647807e-04, 'log_step': 3.067836e-01, 'ssm_b_re': 4.734846e-04, 'ssm_b_im': 4.977967e-04, 'ssm_c_re': 9.680542e-04, 'ssm_c_im': 9.859709e-04, 'ssm_d': 1.274397e-02, 'w_glu': 3.585461e-03, 'b_glu': 5.269719e-03, 'w_attn_proj': 5.474126e-03, 'w_ssm_proj': 5.678033e-03, 'w_out': 7.831789e-03, 'norm2_g': 4.006288e-02, 'w_ff1': 2.057052e-02, 'w_ff2': 3.863387e-02, 'final_g': 1.605209e+01}


def _to_microbatches(a, axis):
    t = _jnp.moveaxis(a, axis, 0)
    t = t.reshape((N_MICROBATCH, t.shape[0] // N_MICROBATCH) + t.shape[1:])
    return _jnp.moveaxis(t, 1, axis + 1)


def setup_inputs(seed: int = 0) -> dict:
    inp = _fwd_setup_inputs(seed)
    key = _jax.random.fold_in(_jax.random.key(seed), 7919)
    shape, _ = _output_shape()
    out = dict(inp)
    out["loss_target"] = _jax.random.normal(_jax.random.fold_in(key, 0), shape, _jnp.float32)
    for i, name in enumerate(TWIN_WEIGHTS):
        w = inp[name].astype(_jnp.float32)
        if MOMENT_SCALE is None:
            s = _jnp.sqrt(_jnp.mean(_jnp.square(w)) + 1e-30)
        else:
            s = MOMENT_SCALE[name]
        km, kv = _jax.random.split(_jax.random.fold_in(key, i + 1))
        out[name] = w
        out["m_" + name] = s * _jax.random.normal(km, w.shape, _jnp.float32)
        out["v_" + name] = (s * s) * _jax.random.uniform(kv, w.shape, _jnp.float32, 0.5, 1.5)
    if N_MICROBATCH > 1:
        for name, axis in PER_EXAMPLE_BATCH_AXIS.items():
            out[name] = _to_microbatches(out[name], axis)
    return {'x': out['x'], 'c': out['c'], 'w_ada': out['w_ada'], 'b_ada': out['b_ada'], 'norm1_g': out['norm1_g'], 'w_in': out['w_in'], 'b_in': out['b_in'], 'attn_sinks': out['attn_sinks'], 'rel_bias': out['rel_bias'], 'lambda_re': out['lambda_re'], 'lambda_im': out['lambda_im'], 'log_step': out['log_step'], 'ssm_b_re': out['ssm_b_re'], 'ssm_b_im': out['ssm_b_im'], 'ssm_c_re': out['ssm_c_re'], 'ssm_c_im': out['ssm_c_im'], 'ssm_d': out['ssm_d'], 'w_glu': out['w_glu'], 'b_glu': out['b_glu'], 'w_attn_proj': out['w_attn_proj'], 'w_ssm_proj': out['w_ssm_proj'], 'w_out': out['w_out'], 'norm2_g': out['norm2_g'], 'w_ff1': out['w_ff1'], 'w_ff2': out['w_ff2'], 'final_g': out['final_g'], 'loss_target': out['loss_target'], 'm_w_ada': out['m_w_ada'], 'm_b_ada': out['m_b_ada'], 'm_norm1_g': out['m_norm1_g'], 'm_w_in': out['m_w_in'], 'm_b_in': out['m_b_in'], 'm_attn_sinks': out['m_attn_sinks'], 'm_rel_bias': out['m_rel_bias'], 'm_lambda_re': out['m_lambda_re'], 'm_lambda_im': out['m_lambda_im'], 'm_log_step': out['m_log_step'], 'm_ssm_b_re': out['m_ssm_b_re'], 'm_ssm_b_im': out['m_ssm_b_im'], 'm_ssm_c_re': out['m_ssm_c_re'], 'm_ssm_c_im': out['m_ssm_c_im'], 'm_ssm_d': out['m_ssm_d'], 'm_w_glu': out['m_w_glu'], 'm_b_glu': out['m_b_glu'], 'm_w_attn_proj': out['m_w_attn_proj'], 'm_w_ssm_proj': out['m_w_ssm_proj'], 'm_w_out': out['m_w_out'], 'm_norm2_g': out['m_norm2_g'], 'm_w_ff1': out['m_w_ff1'], 'm_w_ff2': out['m_w_ff2'], 'm_final_g': out['m_final_g'], 'v_w_ada': out['v_w_ada'], 'v_b_ada': out['v_b_ada'], 'v_norm1_g': out['v_norm1_g'], 'v_w_in': out['v_w_in'], 'v_b_in': out['v_b_in'], 'v_attn_sinks': out['v_attn_sinks'], 'v_rel_bias': out['v_rel_bias'], 'v_lambda_re': out['v_lambda_re'], 'v_lambda_im': out['v_lambda_im'], 'v_log_step': out['v_log_step'], 'v_ssm_b_re': out['v_ssm_b_re'], 'v_ssm_b_im': out['v_ssm_b_im'], 'v_ssm_c_re': out['v_ssm_c_re'], 'v_ssm_c_im': out['v_ssm_c_im'], 'v_ssm_d': out['v_ssm_d'], 'v_w_glu': out['v_w_glu'], 'v_b_glu': out['v_b_glu'], 'v_w_attn_proj': out['v_w_attn_proj'], 'v_w_ssm_proj': out['v_w_ssm_proj'], 'v_w_out': out['v_w_out'], 'v_norm2_g': out['v_norm2_g'], 'v_w_ff1': out['v_w_ff1'], 'v_w_ff2': out['v_w_ff2'], 'v_final_g': out['v_final_g']}


def _loss(weights, diff, rest, loss_target):
    with _jax.named_scope("forward"):
        args = {**rest, TWIN_DIFF_INPUT: diff, **{k: w.astype(_WEIGHT_DTYPES[k]) for k, w in weights.items()}}
        y = _forward(args)
    with _jax.named_scope("loss_head"):
        err = _jnp.square(y.astype(_jnp.float32) - loss_target)
        return 0.5 * _jnp.sum(_jnp.mean(err, axis=-1)) if err.ndim else 0.5 * err


def _adamw(w, g, m, v):
    m = ADAM_B1 * m + (1.0 - ADAM_B1) * g
    v = ADAM_B2 * v + (1.0 - ADAM_B2) * _jnp.square(g)
    m_hat = m / (1.0 - ADAM_B1 ** ADAM_STEP)
    v_hat = v / (1.0 - ADAM_B2 ** ADAM_STEP)
    delta = -ADAM_LR * (m_hat / (_jnp.sqrt(v_hat) + ADAM_EPS) + ADAM_WD * w)
    return delta, m, v


def reference(x, c, w_ada, b_ada, norm1_g, w_in, b_in, attn_sinks, rel_bias, lambda_re, lambda_im, log_step, ssm_b_re, ssm_b_im, ssm_c_re, ssm_c_im, ssm_d, w_glu, b_glu, w_attn_proj, w_ssm_proj, w_out, norm2_g, w_ff1, w_ff2, final_g, loss_target, m_w_ada, m_b_ada, m_norm1_g, m_w_in, m_b_in, m_attn_sinks, m_rel_bias, m_lambda_re, m_lambda_im, m_log_step, m_ssm_b_re, m_ssm_b_im, m_ssm_c_re, m_ssm_c_im, m_ssm_d, m_w_glu, m_b_glu, m_w_attn_proj, m_w_ssm_proj, m_w_out, m_norm2_g, m_w_ff1, m_w_ff2, m_final_g, v_w_ada, v_b_ada, v_norm1_g, v_w_in, v_b_in, v_attn_sinks, v_rel_bias, v_lambda_re, v_lambda_im, v_log_step, v_ssm_b_re, v_ssm_b_im, v_ssm_c_re, v_ssm_c_im, v_ssm_d, v_w_glu, v_b_glu, v_w_attn_proj, v_w_ssm_proj, v_w_out, v_norm2_g, v_w_ff1, v_w_ff2, v_final_g):
    given = dict(x=x, c=c, w_ada=w_ada, b_ada=b_ada, norm1_g=norm1_g, w_in=w_in, b_in=b_in, attn_sinks=attn_sinks, rel_bias=rel_bias, lambda_re=lambda_re, lambda_im=lambda_im, log_step=log_step, ssm_b_re=ssm_b_re, ssm_b_im=ssm_b_im, ssm_c_re=ssm_c_re, ssm_c_im=ssm_c_im, ssm_d=ssm_d, w_glu=w_glu, b_glu=b_glu, w_attn_proj=w_attn_proj, w_ssm_proj=w_ssm_proj, w_out=w_out, norm2_g=norm2_g, w_ff1=w_ff1, w_ff2=w_ff2, final_g=final_g, loss_target=loss_target, m_w_ada=m_w_ada, m_b_ada=m_b_ada, m_norm1_g=m_norm1_g, m_w_in=m_w_in, m_b_in=m_b_in, m_attn_sinks=m_attn_sinks, m_rel_bias=m_rel_bias, m_lambda_re=m_lambda_re, m_lambda_im=m_lambda_im, m_log_step=m_log_step, m_ssm_b_re=m_ssm_b_re, m_ssm_b_im=m_ssm_b_im, m_ssm_c_re=m_ssm_c_re, m_ssm_c_im=m_ssm_c_im, m_ssm_d=m_ssm_d, m_w_glu=m_w_glu, m_b_glu=m_b_glu, m_w_attn_proj=m_w_attn_proj, m_w_ssm_proj=m_w_ssm_proj, m_w_out=m_w_out, m_norm2_g=m_norm2_g, m_w_ff1=m_w_ff1, m_w_ff2=m_w_ff2, m_final_g=m_final_g, v_w_ada=v_w_ada, v_b_ada=v_b_ada, v_norm1_g=v_norm1_g, v_w_in=v_w_in, v_b_in=v_b_in, v_attn_sinks=v_attn_sinks, v_rel_bias=v_rel_bias, v_lambda_re=v_lambda_re, v_lambda_im=v_lambda_im, v_log_step=v_log_step, v_ssm_b_re=v_ssm_b_re, v_ssm_b_im=v_ssm_b_im, v_ssm_c_re=v_ssm_c_re, v_ssm_c_im=v_ssm_c_im, v_ssm_d=v_ssm_d, v_w_glu=v_w_glu, v_b_glu=v_b_glu, v_w_attn_proj=v_w_attn_proj, v_w_ssm_proj=v_w_ssm_proj, v_w_out=v_w_out, v_norm2_g=v_norm2_g, v_w_ff1=v_w_ff1, v_w_ff2=v_w_ff2, v_final_g=v_final_g)
    weights = {n: given[n] for n in TWIN_WEIGHTS}
    shared = {n: given[n] for n in SHARED_INPUTS}
    per_example = {n: given[n] for n in ['x', 'c']}
    grad_fn = _jax.value_and_grad(_loss, argnums=(0, 1))

    def one_microbatch(ex, loss_target):
        ex = dict(ex)
        diff = ex.pop(TWIN_DIFF_INPUT)
        return grad_fn(weights, diff, {**shared, **ex}, loss_target)

    if N_MICROBATCH == 1:
        loss, (grad_w, grad_x) = one_microbatch(per_example, given["loss_target"])
    else:
        def body(carry, xs):
            loss_sum, grad_sum = carry
            l_k, (gw_k, gx_k) = one_microbatch(xs[0], xs[1])
            with _jax.named_scope("update"):
                return (loss_sum + l_k, _jax.tree.map(_jnp.add, grad_sum, gw_k)), gx_k

        init = (_jnp.zeros((), _jnp.float32), _jax.tree.map(_jnp.zeros_like, weights))
        (loss, grad_w), grad_x = _jax.lax.scan(body, init, (per_example, given["loss_target"]))
    with _jax.named_scope("update"):
        delta_w, new_m, new_v = {}, {}, {}
        for n in TWIN_WEIGHTS:
            delta_w[n], new_m[n], new_v[n] = _adamw(weights[n], grad_w[n], given["m_" + n], given["v_" + n])
    return (loss, grad_x, *[grad_w[n] for n in TWIN_WEIGHTS], *[delta_w[n] for n in TWIN_WEIGHTS],
            *[new_m[n] for n in TWIN_WEIGHTS], *[new_v[n] for n in TWIN_WEIGHTS])
```

```python
import functools
import math

import numpy as np
import jax
import jax.numpy as jnp
from jax import lax
from jax.experimental import pallas as pl
from jax.experimental.pallas import tpu as pltpu

F32 = jnp.float32
BF16 = jnp.bfloat16
MESH = pl.DeviceIdType.MESH

N_DEV = 8
D = 2048
HEAD_DIM = 64
N_Q_HEADS = 16
N_KV_HEADS = 4
GROUP = N_Q_HEADS // N_KV_HEADS
ATTN_W = N_Q_HEADS * HEAD_DIM
KV_W = N_KV_HEADS * HEAD_DIM
BLK = 128
NUM_BUCKETS = 32
MAX_DISTANCE = 128
NEG_INF = -1e30
SSM_W = 512
SSM_P = 16
SSM_G = 32
SSM_N = 64
SSM_H = SSM_G * SSM_N
D_FF = 4 * D
IN_W = ATTN_W + 2 * KV_W + SSM_W + 2 * D
N_MOD = 6
EPS = 1e-6

ADAM_LR = 0.001
ADAM_B1 = 0.9
ADAM_B2 = 0.999
ADAM_EPS = 1e-08
ADAM_WD = 0.01
ADAM_STEP = 10

VMEM_LIMIT = 56 * 1024 * 1024
PACK_W = 2048


def _cparams(sem):
    return pltpu.CompilerParams(dimension_semantics=sem, vmem_limit_bytes=VMEM_LIMIT)


def _matmul(a, b, *, mode, dims, tiles, out_dtypes, name, a_off=0, b3=False,
            out3=False, bias=None, extras=(), epilogue=None):
    M, N, K = dims
    tm, tn, tk = tiles
    assert M % tm == 0 and N % tn == 0 and K % tk == 0, (name, dims, tiles)
    gm, gn, gk = M // tm, N // tn, K // tk
    n_extra = len(extras)
    has_bias = bias is not None
    n_out = len(out_dtypes)

    if mode == "nn":
        a_spec = pl.BlockSpec((tm, tk), lambda i, j, k: (i, a_off + k))
        if b3:
            nb = (N // N_DEV) // tn
            assert nb * tn * N_DEV == N
            b_spec = pl.BlockSpec((None, tk, tn), lambda i, j, k: (j // nb, k, j % nb))
        else:
            b_spec = pl.BlockSpec((tk, tn), lambda i, j, k: (k, j))
        dn = (((1,), (0,)), ((), ()))
    elif mode == "nt":
        a_spec = pl.BlockSpec((tm, tk), lambda i, j, k: (i, a_off + k))
        if b3:
            nb = (K // N_DEV) // tk
            assert nb * tk * N_DEV == K
            b_spec = pl.BlockSpec((None, tn, tk), lambda i, j, k: (k // nb, j, k % nb))
        else:
            b_spec = pl.BlockSpec((tn, tk), lambda i, j, k: (j, k))
        dn = (((1,), (1,)), ((), ()))
    else:
        a_spec = pl.BlockSpec((tk, tm), lambda i, j, k: (k, a_off + i))
        b_spec = pl.BlockSpec((tk, tn), lambda i, j, k: (k, j))
        dn = (((0,), (0,)), ((), ()))

    if out3:
        nbo = (N // N_DEV) // tn
        assert nbo * tn * N_DEV == N
        o_spec = pl.BlockSpec((None, tm, tn), lambda i, j, k: (j // nbo, i, j % nbo))
        o_shape = (N_DEV, M, N // N_DEV)
    else:
        o_spec = pl.BlockSpec((tm, tn), lambda i, j, k: (i, j))
        o_shape = (M, N)

    in_specs = [a_spec, b_spec]
    args = [a, b]
    if has_bias:
        in_specs.append(pl.BlockSpec((1, tn), lambda i, j, k: (0, j)))
        args.append(bias)
    for e in extras:
        in_specs.append(pl.BlockSpec((tm, tn), lambda i, j, k: (i, j)))
        args.append(e)

    def body(*refs):
        a_ref, b_ref = refs[0], refs[1]
        pos = 2
        bias_ref = None
        if has_bias:
            bias_ref = refs[pos]
            pos += 1
        extra_refs = refs[pos:pos + n_extra]
        pos += n_extra
        out_refs = refs[pos:pos + n_out]
        acc_ref = refs[pos + n_out] if gk > 1 else None

        part = lax.dot_general(a_ref[...].astype(BF16), b_ref[...].astype(BF16), dn,
                               preferred_element_type=F32)

        def finish(acc):
            if has_bias:
                acc = acc + bias_ref[...]
            if epilogue is None:
                vals = (acc,)
            else:
                vals = epilogue(acc, *[e[...] for e in extra_refs])
            for o_ref, val in zip(out_refs, vals):
                o_ref[...] = val.astype(o_ref.dtype)

        if gk == 1:
            finish(part)
        else:
            k = pl.program_id(2)

            @pl.when(k == 0)
            def _():
                acc_ref[...] = part

            @pl.when(k > 0)
            def _():
                acc_ref[...] += part

            @pl.when(k == gk - 1)
            def _():
                finish(acc_ref[...])

    outs = pl.pallas_call(
        body,
        grid=(gm, gn, gk),
        in_specs=in_specs,
        out_specs=[o_spec] * n_out,
        out_shape=[jax.ShapeDtypeStruct(o_shape, dt) for dt in out_dtypes],
        scratch_shapes=([pltpu.VMEM((tm, tn), F32)] if gk > 1 else []),
        compiler_params=_cparams(("parallel", "parallel", "arbitrary")),
        name=name,
    )(*args)
    return outs[0] if n_out == 1 else outs


def _rowwise(fn, rows, vecs, row_outs, sum_outs, *, n_rows, tr, ch, name):
    assert n_rows % tr == 0 and tr % ch == 0
    nt = n_rows // tr
    nr, nv, nro, nso = len(rows), len(vecs), len(row_outs), len(sum_outs)
    in_specs, args = [], []
    for (arr, lead, cblk, w, shift) in rows:
        if shift:
            ridx = lambda i, shift=shift: jnp.minimum(i + shift, nt - 1)
        else:
            ridx = lambda i: i
        if arr.ndim == 3:
            in_specs.append(pl.BlockSpec(
                (None, tr, w), lambda i, lead=lead, cblk=cblk, ridx=ridx: (lead, ridx(i), cblk)))
        else:
            in_specs.append(pl.BlockSpec(
                (tr, w), lambda i, cblk=cblk, ridx=ridx: (ridx(i), cblk)))
        args.append(arr)
    for v in vecs:
        in_specs.append(pl.BlockSpec(v.shape, lambda i, nd=v.ndim: (0,) * nd))
        args.append(v)
    out_specs = [pl.BlockSpec((tr, w), lambda i: (i, 0)) for (w, _) in row_outs]
    out_shape = [jax.ShapeDtypeStruct((n_rows, w), dt) for (w, dt) in row_outs]
    for (r, w) in sum_outs:
        out_specs.append(pl.BlockSpec((r, w), lambda i: (0, 0)))
        out_shape.append(jax.ShapeDtypeStruct((r, w), F32))

    def body(*refs):
        i = pl.program_id(0)
        r_in = refs[:nr]
        v_in = refs[nr:nr + nv]
        r_out = refs[nr + nv:nr + nv + nro]
        s_out = refs[nr + nv + nro:]
        if nso:
            @pl.when(i == 0)
            def _():
                for s in s_out:
                    s[...] = jnp.zeros(s.shape, F32)
        vvals = [v[...] for v in v_in]

        def chunk(ci, carry):
            r0 = pl.multiple_of(ci * ch, ch)
            rv = [r[pl.ds(r0, ch), :] for r in r_in]
            ro, so = fn(rv, vvals, i, nt)
            for ref, val in zip(r_out, ro):
                ref[pl.ds(r0, ch), :] = val.astype(ref.dtype)
            for ref, val in zip(s_out, so):
                ref[...] += val
            return carry

        lax.fori_loop(0, tr // ch, chunk, 0)

    outs = pl.pallas_call(
        body,
        grid=(nt,),
        in_specs=in_specs,
        out_specs=out_specs,
        out_shape=out_shape,
        compiler_params=_cparams(("arbitrary",)),
        name=name,
    )(*args)
    return outs


def _row(arr, cblk=0, w=None, lead=0, shift=0):
    return (arr, lead, cblk, arr.shape[-1] if w is None else w, shift)


def _colsum(v):
    return jnp.sum(v, axis=0, keepdims=True)


def _rms(x):
    return lax.rsqrt(jnp.mean(x * x, axis=-1, keepdims=True) + EPS)


def _sigmoid(x):
    return 1.0 / (1.0 + jnp.exp(-x))


_GELU_C = math.sqrt(2.0 / math.pi)


def _gelu(x):
    return 0.5 * x * (1.0 + jnp.tanh(_GELU_C * (x + 0.044715 * (x * x * x))))


def _gelu_grad(x):
    t = jnp.tanh(_GELU_C * (x + 0.044715 * (x * x * x)))
    return 0.5 * (1.0 + t) + 0.5 * x * (1.0 - t * t) * (_GELU_C * (1.0 + 3.0 * 0.044715 * (x * x)))


def _my_pos():
    return lax.axis_index("x"), lax.axis_index("y"), lax.axis_index("c")


def _flip(pos, k):
    x, y, c = pos
    return (1 - x if k & 4 else x, 1 - y if k & 2 else y, 1 - c if k & 1 else c)


def _dev_id(pos):
    return 4 * pos[0] + 2 * pos[1] + pos[2]


def _small_allgather(x, name):
    r, c = x.shape

    def body(x_ref, out_ref, send_sems, recv_sems):
        me = _my_pos()
        out_ref[_dev_id(me)] = x_ref[...]
        copies = []
        for k in range(1, N_DEV):
            cp = pltpu.make_async_remote_copy(
                src_ref=x_ref, dst_ref=out_ref.at[_dev_id(me)],
                send_sem=send_sems.at[k - 1], recv_sem=recv_sems.at[k - 1],
                device_id=_flip(me, k), device_id_type=MESH)
            cp.start()
            copies.append(cp)
        for k in range(1, N_DEV):
            peer = _flip(me, k)
            pltpu.make_async_remote_copy(
                src_ref=x_ref, dst_ref=out_ref.at[_dev_id(peer)],
                send_sem=send_sems.at[k - 1], recv_sem=recv_sems.at[k - 1],
                device_id=peer, device_id_type=MESH).wait_recv()
        for cp in copies:
            cp.wait_send()

    return pl.pallas_call(
        body,
        out_shape=jax.ShapeDtypeStruct((N_DEV, r, c), x.dtype),
        in_specs=[pl.BlockSpec(memory_space=pltpu.VMEM)],
        out_specs=pl.BlockSpec(memory_space=pltpu.VMEM),
        scratch_shapes=[pltpu.SemaphoreType.DMA((N_DEV - 1,)),
                        pltpu.SemaphoreType.DMA((N_DEV - 1,))],
        compiler_params=pltpu.CompilerParams(vmem_limit_bytes=VMEM_LIMIT),
        name=name,
    )(x)


def _allgather_weights(shards, name):
    n = len(shards)

    def body(*refs):
        xs = refs[:n]
        outs = refs[n:2 * n]
        send_sems, recv_sems, local_sems = refs[2 * n:]
        x, y, c = _my_pos()
        me, sib = (x, y, c), (x, y, 1 - c)
        chips = [(1 - x, y), (x, 1 - y), (1 - x, 1 - y)]

        def copy(a, k, block, to, src=None):
            slot = outs[a].at[_dev_id(block)]
            return pltpu.make_async_remote_copy(
                src_ref=slot if src is None else src, dst_ref=slot,
                send_sem=send_sems.at[a, k], recv_sem=recv_sems.at[a, k],
                device_id=to, device_id_type=MESH)

        mine = [pltpu.make_async_copy(xs[a], outs[a].at[_dev_id(me)], local_sems.at[a])
                for a in range(n)]
        for cp in mine:
            cp.start()
        first = []
        for a in range(n):
            first.append(copy(a, 0, me, sib, src=xs[a]))
            for j, chip in enumerate(chips):
                first.append(copy(a, 1 + j, me, (*chip, c), src=xs[a]))
        for cp in first:
            cp.start()
        passed = []
        for a in range(n):
            for j, chip in enumerate(chips):
                copy(a, 1 + j, (*chip, c), me).wait_recv()
                cp = copy(a, 4 + j, (*chip, c), sib)
                cp.start()
                passed.append(cp)
        for a in range(n):
            copy(a, 0, sib, me).wait_recv()
            for j, chip in enumerate(chips):
                copy(a, 4 + j, (*chip, 1 - c), me).wait_recv()
        for cp in first + passed:
            cp.wait_send()
        for cp in mine:
            cp.wait()

    hbm = pl.BlockSpec(memory_space=pl.ANY)
    return pl.pallas_call(
        body,
        out_shape=[jax.ShapeDtypeStruct((N_DEV,) + s.shape, s.dtype) for s in shards],
        in_specs=[hbm] * n,
        out_specs=[hbm] * n,
        scratch_shapes=[pltpu.SemaphoreType.DMA((n, 7)),
                        pltpu.SemaphoreType.DMA((n, 7)),
                        pltpu.SemaphoreType.DMA((n,))],
        name=name,
    )(*shards)


def _exchange_grads(grads, name):
    n = len(grads)

    def body(*refs):
        gs = refs[:n]
        outs = refs[n:2 * n]
        send_sems, recv_sems, local_sems = refs[2 * n:]
        me = _my_pos()
        my_id = _dev_id(me)
        mine = [pltpu.make_async_copy(gs[a].at[my_id], outs[a].at[my_id], local_sems.at[a])
                for a in range(n)]
        for cp in mine:
            cp.start()
        sent = []
        for a in range(n):
            for k in range(1, N_DEV):
                peer = _flip(me, k)
                cp = pltpu.make_async_remote_copy(
                    src_ref=gs[a].at[_dev_id(peer)], dst_ref=outs[a].at[my_id],
                    send_sem=send_sems.at[a, k - 1], recv_sem=recv_sems.at[a, k - 1],
                    device_id=peer, device_id_type=MESH)
                cp.start()
                sent.append(cp)
        for a in range(n):
            for k in range(1, N_DEV):
                peer = _flip(me, k)
                pltpu.make_async_remote_copy(
                    src_ref=gs[a].at[my_id], dst_ref=outs[a].at[_dev_id(peer)],
                    send_sem=send_sems.at[a, k - 1], recv_sem=recv_sems.at[a, k - 1],
                    device_id=peer, device_id_type=MESH).wait_recv()
        for cp in sent:
            cp.wait_send()
        for cp in mine:
            cp.wait()

    hbm = pl.BlockSpec(memory_space=pl.ANY)
    return pl.pallas_call(
        body,
        out_shape=[jax.ShapeDtypeStruct(g.shape, g.dtype) for g in grads],
        in_specs=[hbm] * n,
        out_specs=[hbm] * n,
        scratch_shapes=[pltpu.SemaphoreType.DMA((n, 7)),
                        pltpu.SemaphoreType.DMA((n, 7)),
                        pltpu.SemaphoreType.DMA((n,))],
        name=name,
    )(*grads)


def _t5_buckets_block():
    qi = np.arange(BLK)[:, None]
    ki = np.arange(2 * BLK)[None, :]
    n = np.maximum(qi + BLK - ki, 0)
    max_exact = NUM_BUCKETS // 2
    large = max_exact + (np.log(np.maximum(n, 1) / max_exact)
                         / np.log(MAX_DISTANCE / max_exact)
                         * (NUM_BUCKETS - max_exact)).astype(np.int32)
    large = np.minimum(large, NUM_BUCKETS - 1)
    return np.where(n < max_exact, n, large).astype(np.int32)


def _band_mask():
    qi = np.arange(BLK)[:, None]
    ki = np.arange(2 * BLK)[None, :]
    dist = qi + BLK - ki
    return (dist >= 0) & (dist < BLK)


def _attn_probs(q_ref, kp_ref, kc_ref, bias_ref, sink_ref, hkv, first_block):
    c0 = hkv * HEAD_DIM
    kk = jnp.concatenate([kp_ref[:, c0:c0 + HEAD_DIM], kc_ref[:, c0:c0 + HEAD_DIM]],
                         axis=0).astype(BF16)
    qg = jnp.concatenate(
        [q_ref[:, (hkv * GROUP + g) * HEAD_DIM:(hkv * GROUP + g + 1) * HEAD_DIM]
         for g in range(GROUP)], axis=0).astype(BF16)
    s = lax.dot_general(qg, kk, (((1,), (1,)), ((), ())), preferred_element_type=F32)
    s = s * (HEAD_DIM ** -0.5) + bias_ref[hkv * GROUP * BLK:(hkv + 1) * GROUP * BLK, :]
    col = lax.broadcasted_iota(jnp.int32, s.shape, 1)
    s = jnp.where(jnp.logical_and(first_block, col < BLK), NEG_INF, s)
    sink = sink_ref[hkv * GROUP * BLK:(hkv + 1) * GROUP * BLK, :]
    m = jnp.maximum(jnp.max(s, axis=-1, keepdims=True), sink)
    p = jnp.exp(s - m)
    e_sink = jnp.exp(sink - m)
    inv = 1.0 / (jnp.sum(p, axis=-1, keepdims=True) + e_sink)
    return qg, kk, p * inv, e_sink * inv


def _attention_fwd(proj, biasm, sinkcol, n_rows):
    nb = n_rows // BLK

    def body(q_ref, kp_ref, kc_ref, vp_ref, vc_ref, bias_ref, sink_ref, o_ref):
        first = pl.program_id(0) == 0
        for hkv in range(N_KV_HEADS):
            c0 = hkv * HEAD_DIM
            _, _, p, _ = _attn_probs(q_ref, kp_ref, kc_ref, bias_ref, sink_ref, hkv, first)
            vv = jnp.concatenate([vp_ref[:, c0:c0 + HEAD_DIM], vc_ref[:, c0:c0 + HEAD_DIM]],
                                 axis=0).astype(BF16)
            o = jnp.dot(p.astype(BF16), vv, preferred_element_type=F32)
            for g in range(GROUP):
                h = hkv * GROUP + g
                o_ref[:, h * HEAD_DIM:(h + 1) * HEAD_DIM] = (
                    o[g * BLK:(g + 1) * BLK, :].astype(o_ref.dtype))

    prev = lambda n: jnp.maximum(n - 1, 0)
    return pl.pallas_call(
        body,
        grid=(nb,),
        in_specs=[
            pl.BlockSpec((BLK, ATTN_W), lambda n: (n, 0)),
            pl.BlockSpec((BLK, KV_W), lambda n: (prev(n), ATTN_W // KV_W)),
            pl.BlockSpec((BLK, KV_W), lambda n: (n, ATTN_W // KV_W)),
            pl.BlockSpec((BLK, KV_W), lambda n: (prev(n), ATTN_W // KV_W + 1)),
            pl.BlockSpec((BLK, KV_W), lambda n: (n, ATTN_W // KV_W + 1)),
            pl.BlockSpec(biasm.shape, lambda n: (0, 0)),
            pl.BlockSpec(sinkcol.shape, lambda n: (0, 0)),
        ],
        out_specs=pl.BlockSpec((BLK, ATTN_W), lambda n: (n, 0)),
        out_shape=jax.ShapeDtypeStruct((n_rows, ATTN_W), BF16),
        compiler_params=_cparams(("parallel",)),
        name="attn_fwd",
    )(proj, proj, proj, proj, proj, biasm, sinkcol)


def _attention_bwd(proj, attn, dattn, biasm, sinkcol, n_rows):
    nb = n_rows // BLK
    scale = HEAD_DIM ** -0.5

    def body(q_ref, kp_ref, kc_ref, vp_ref, vc_ref, o_ref, do_ref, bias_ref, sink_ref,
             dq_ref, dkc_ref, dkp_ref, dvc_ref, dvp_ref, dbias_ref, dsink_ref):
        n = pl.program_id(0)
        first = n == 0

        @pl.when(first)
        def _():
            dbias_ref[...] = jnp.zeros(dbias_ref.shape, F32)
            dsink_ref[...] = jnp.zeros(dsink_ref.shape, F32)

        for hkv in range(N_KV_HEADS):
            c0 = hkv * HEAD_DIM
            r0, r1 = hkv * GROUP * BLK, (hkv + 1) * GROUP * BLK
            qg, kk, p, p_sink = _attn_probs(q_ref, kp_ref, kc_ref, bias_ref, sink_ref, hkv, first)
            vv = jnp.concatenate([vp_ref[:, c0:c0 + HEAD_DIM], vc_ref[:, c0:c0 + HEAD_DIM]],
                                 axis=0).astype(BF16)
            heads = [hkv * GROUP + g for g in range(GROUP)]
            dog = jnp.concatenate([do_ref[:, h * HEAD_DIM:(h + 1) * HEAD_DIM] for h in heads], axis=0)
            og = jnp.concatenate([o_ref[:, h * HEAD_DIM:(h + 1) * HEAD_DIM] for h in heads], axis=0)
            delta = jnp.sum(dog.astype(F32) * og.astype(F32), axis=-1, keepdims=True)
            dp = lax.dot_general(dog.astype(BF16), vv, (((1,), (1,)), ((), ())),
                                 preferred_element_type=F32)
            ds = p * (dp - delta)
            dbias_ref[r0:r1, :] += ds
            dsink_ref[r0:r1, :] += -(p_sink * delta)
            ds16 = ds.astype(BF16)
            dqg = jnp.dot(ds16, kk, preferred_element_type=F32) * scale
            dkk = lax.dot_general(ds16, qg, (((0,), (0,)), ((), ())),
                                  preferred_element_type=F32) * scale
            dvv = lax.dot_general(p.astype(BF16), dog.astype(BF16), (((0,), (0,)), ((), ())),
                                  preferred_element_type=F32)
            for g, h in enumerate(heads):
                dq_ref[:, h * HEAD_DIM:(h + 1) * HEAD_DIM] = (
                    dqg[g * BLK:(g + 1) * BLK, :].astype(dq_ref.dtype))
            dkp_ref[:, c0:c0 + HEAD_DIM] = dkk[:BLK].astype(dkp_ref.dtype)
            dkc_ref[:, c0:c0 + HEAD_DIM] = dkk[BLK:].astype(dkc_ref.dtype)
            dvp_ref[:, c0:c0 + HEAD_DIM] = dvv[:BLK].astype(dvp_ref.dtype)
            dvc_ref[:, c0:c0 + HEAD_DIM] = dvv[BLK:].astype(dvc_ref.dtype)

    prev = lambda n: jnp.maximum(n - 1, 0)
    kv_out = pl.BlockSpec((BLK, KV_W), lambda n: (n, 0))
    kv_shape = jax.ShapeDtypeStruct((n_rows, KV_W), F32)
    return pl.pallas_call(
        body,
        grid=(nb,),
        in_specs=[
            pl.BlockSpec((BLK, ATTN_W), lambda n: (n, 0)),
            pl.BlockSpec((BLK, KV_W), lambda n: (prev(n), ATTN_W // KV_W)),
            pl.BlockSpec((BLK, KV_W), lambda n: (n, ATTN_W // KV_W)),
            pl.BlockSpec((BLK, KV_W), lambda n: (prev(n), ATTN_W // KV_W + 1)),
            pl.BlockSpec((BLK, KV_W), lambda n: (n, ATTN_W // KV_W + 1)),
            pl.BlockSpec((BLK, ATTN_W), lambda n: (n, 0)),
            pl.BlockSpec((BLK, ATTN_W), lambda n: (n, 0)),
            pl.BlockSpec(biasm.shape, lambda n: (0, 0)),
            pl.BlockSpec(sinkcol.shape, lambda n: (0, 0)),
        ],
        out_specs=[
            pl.BlockSpec((BLK, ATTN_W), lambda n: (n, 0)),
            kv_out, kv_out, kv_out, kv_out,
            pl.BlockSpec(biasm.shape, lambda n: (0, 0)),
            pl.BlockSpec(sinkcol.shape, lambda n: (0, 0)),
        ],
        out_shape=[
            jax.ShapeDtypeStruct((n_rows, ATTN_W), BF16),
            kv_shape, kv_shape, kv_shape, kv_shape,
            jax.ShapeDtypeStruct(biasm.shape, F32),
            jax.ShapeDtypeStruct(sinkcol.shape, F32),
        ],
        compiler_params=_cparams(("arbitrary",)),
        name="attn_bwd",
    )(proj, proj, proj, proj, proj, attn, dattn, biasm, sinkcol)


def _split3(a):
    hi = a.astype(BF16)
    r1 = a - hi.astype(F32)
    mid = r1.astype(BF16)
    lo = (r1 - mid.astype(F32)).astype(BF16)
    return hi, mid, lo


def _bucket_reduce(dbias, dsink, onehot_t):
    def body(db_ref, ds_ref, oh_ref, ob_ref, os_ref):
        acc = jnp.zeros((N_Q_HEADS, 128), F32)
        for part in _split3(db_ref[...]):
            acc = acc + lax.dot_general(part, oh_ref[...], (((1,), (1,)), ((), ())),
                                        preferred_element_type=F32)
        ob_ref[...] = acc
        os_ref[...] = jnp.broadcast_to(jnp.sum(ds_ref[...], axis=-1, keepdims=True),
                                       os_ref.shape)

    return pl.pallas_call(
        body,
        out_shape=[jax.ShapeDtypeStruct((N_Q_HEADS, 128), F32),
                   jax.ShapeDtypeStruct((N_Q_HEADS, 128), F32)],
        compiler_params=pltpu.CompilerParams(vmem_limit_bytes=VMEM_LIMIT),
        name="bias_bucket_reduce",
    )(dbias, dsink, onehot_t)


def _disc(lr, li, ls, btr, bti):
    lam_re = jnp.minimum(lr, -1e-4)
    delta = jnp.exp(ls)
    mag = jnp.exp(lam_re * delta)
    ang = li * delta
    ar, ai = mag * jnp.cos(ang), mag * jnp.sin(ang)
    nr, ni = ar - 1.0, ai
    den = lam_re * lam_re + li * li
    fr = (nr * lam_re + ni * li) / den
    fi = (ni * lam_re - nr * li) / den
    bbr = fr * btr - fi * bti
    bbi = fr * bti + fi * btr
    return ar, ai, bbr, bbi


def _block_mask():
    row = lax.broadcasted_iota(jnp.int32, (SSM_W, SSM_H), 0)
    col = lax.broadcasted_iota(jnp.int32, (SSM_W, SSM_H), 1)
    return (row // SSM_P) == (col // SSM_N)


def _ssm_setup(lr, li, ls, btr, bti, ctr, cti):
    def body(lr_ref, li_ref, ls_ref, btr_ref, bti_ref, ctr_ref, cti_ref, a_ref, b_ref, c_ref):
        ar, ai, bbr, bbi = _disc(lr_ref[...], li_ref[...], ls_ref[...], btr_ref[...], bti_ref[...])
        a_ref[:, :SSM_H] = ar
        a_ref[:, SSM_H:] = ai
        mask = _block_mask()
        blk = lambda t: jnp.where(mask, jnp.tile(t, (SSM_G, 1)), 0.0)
        b_ref[:, :SSM_H] = blk(bbr).astype(BF16)
        b_ref[:, SSM_H:] = blk(bbi).astype(BF16)
        c_ref[:, :SSM_H] = blk(ctr_ref[...]).astype(BF16)
        c_ref[:, SSM_H:] = blk(-cti_ref[...]).astype(BF16)

    return pl.pallas_call(
        body,
        out_shape=[jax.ShapeDtypeStruct((1, 2 * SSM_H), F32),
                   jax.ShapeDtypeStruct((SSM_W, 2 * SSM_H), BF16),
                   jax.ShapeDtypeStruct((SSM_W, 2 * SSM_H), BF16)],
        compiler_params=pltpu.CompilerParams(vmem_limit_bytes=VMEM_LIMIT),
        name="ssm_setup",
    )(lr, li, ls, btr, bti, ctr, cti)


def _ssm_param_bwd(lr, li, ls, btr, bti, dacc, dbcat, dccat, gind):
    def body(lr_ref, li_ref, ls_ref, btr_ref, bti_ref, dacc_ref, db_ref, dc_ref, g_ref,
             dlr_ref, dli_ref, dls_ref, dbtr_ref, dbti_ref, dctr_ref, dcti_ref):
        dar = jnp.sum(dacc_ref[:, :SSM_H], axis=0, keepdims=True)
        dai = jnp.sum(dacc_ref[:, SSM_H:], axis=0, keepdims=True)
        col = lax.broadcasted_iota(jnp.int32, (SSM_P, 2 * SSM_H), 1)
        grp = (col % SSM_H) // SSM_N
        db = jnp.zeros((SSM_P, 2 * SSM_H), F32)
        dc = jnp.zeros((SSM_P, 2 * SSM_H), F32)
        for g in range(SSM_G):
            sel = grp == g
            db = db + jnp.where(sel, db_ref[g * SSM_P:(g + 1) * SSM_P, :], 0.0)
            dc = dc + jnp.where(sel, dc_ref[g * SSM_P:(g + 1) * SSM_P, :], 0.0)
        dctr_ref[...] = dc[:, :SSM_H]
        dcti_ref[...] = -dc[:, SSM_H:]
        prim = (lr_ref[...], li_ref[...], ls_ref[...], btr_ref[...], bti_ref[...])
        _, vjp = jax.vjp(_disc, *prim)
        dlr, dli, dls, dbtr, dbti = vjp((dar, dai, db[:, :SSM_H], db[:, SSM_H:]))
        dlr_ref[...] = dlr
        dli_ref[...] = dli
        dbtr_ref[...] = dbtr
        dbti_ref[...] = dbti
        acc = jnp.zeros((8, 128), F32)
        for part in _split3(jnp.broadcast_to(dls, (8, SSM_H))):
            acc = acc + jnp.dot(part, g_ref[...], preferred_element_type=F32)
        dls_ref[...] = acc

    vec = jax.ShapeDtypeStruct((1, SSM_H), F32)
    mat = jax.ShapeDtypeStruct((SSM_P, SSM_H), F32)
    return pl.pallas_call(
        body,
        out_shape=[vec, vec, jax.ShapeDtypeStruct((8, 128), F32), mat, mat, mat, mat],
        compiler_params=pltpu.CompilerParams(vmem_limit_bytes=VMEM_LIMIT),
        name="ssm_param_bwd",
    )(lr, li, ls, btr, bti, dacc, dbcat, dccat, gind)


SCAN_TR = 256


def _cmul_add(vr, vi, pr, pi, sr, si):
    return vr + pr * sr - pi * si, vi + pr * si + pi * sr


def _bcast_row(v, row, which):
    b = jnp.where(row == which, v, 0.0)
    b = b + pltpu.roll(b, 4, 0)
    b = b + pltpu.roll(b, 2, 0)
    return b + pltpu.roll(b, 1, 0)


def _scan_tables(a_ref, tab_ref, reverse):
    H = SSM_H
    ar = jnp.broadcast_to(a_ref[:, :H], (8, H))
    ai = jnp.broadcast_to(a_ref[:, H:], (8, H))
    if reverse:
        ai = -ai
    row = lax.broadcasted_iota(jnp.int32, (8, H), 0)
    pw = [(ar, ai)]
    for _ in range(7):
        cr, ci = pw[-1]
        pw.append((cr * ar - ci * ai, cr * ai + ci * ar))
    pcr = jnp.zeros((8, H), F32)
    pci = jnp.zeros((8, H), F32)
    for e in range(8):
        sel = (row == (7 - e)) if reverse else (row == e)
        pcr = jnp.where(sel, pw[e][0], pcr)
        pci = jnp.where(sel, pw[e][1], pci)
    tab_ref[0, :, :H] = pcr
    tab_ref[0, :, H:] = pci
    for t, k in enumerate((1, 2, 4)):
        keep = (row < 8 - k) if reverse else (row >= k)
        tab_ref[1 + t, :, :H] = jnp.where(keep, pw[k - 1][0], 0.0)
        tab_ref[1 + t, :, H:] = jnp.where(keep, pw[k - 1][1], 0.0)


def _scan_group(vr, vi, cr, ci, tab_ref, reverse):
    H = SSM_H
    for t, k in enumerate((1, 2, 4)):
        sh = 8 - k if reverse else k
        vr, vi = _cmul_add(vr, vi, tab_ref[1 + t, :, :H], tab_ref[1 + t, :, H:],
                           pltpu.roll(vr, sh, 0), pltpu.roll(vi, sh, 0))
    return _cmul_add(vr, vi, tab_ref[0, :, :H], tab_ref[0, :, H:], cr, ci)


def _scan_fwd(bu, abar, n_rows):
    H = SSM_H
    nt = n_rows // SCAN_TR

    def body(bu_ref, a_ref, xs_ref, xp_ref, tab_ref, carry_ref):
        @pl.when(pl.program_id(0) == 0)
        def _():
            _scan_tables(a_ref, tab_ref, False)
            carry_ref[...] = jnp.zeros(carry_ref.shape, F32)

        row = lax.broadcasted_iota(jnp.int32, (8, H), 0)

        def group(j, carry):
            cr, ci = carry
            r0 = pl.multiple_of(j * 16, 16)
            xr, xi = [], []
            for half in range(2):
                rr = pl.multiple_of(r0 + 8 * half, 8)
                vr, vi = _scan_group(bu_ref[pl.ds(rr, 8), :H], bu_ref[pl.ds(rr, 8), H:],
                                     cr, ci, tab_ref, False)
                xp_ref[pl.ds(rr, 8), :H] = jnp.where(row == 0, cr, pltpu.roll(vr, 1, 0))
                xp_ref[pl.ds(rr, 8), H:] = jnp.where(row == 0, ci, pltpu.roll(vi, 1, 0))
                cr, ci = _bcast_row(vr, row, 7), _bcast_row(vi, row, 7)
                xr.append(vr)
                xi.append(vi)
            xs_ref[pl.ds(r0, 16), :H] = jnp.concatenate(xr, axis=0).astype(BF16)
            xs_ref[pl.ds(r0, 16), H:] = jnp.concatenate(xi, axis=0).astype(BF16)
            return cr, ci

        cr, ci = lax.fori_loop(0, SCAN_TR // 16, group,
                               (carry_ref[:, :H], carry_ref[:, H:]))
        carry_ref[:, :H] = cr
        carry_ref[:, H:] = ci

    return pl.pallas_call(
        body,
        grid=(nt,),
        in_specs=[pl.BlockSpec((SCAN_TR, 2 * H), lambda i: (i, 0)),
                  pl.BlockSpec((1, 2 * H), lambda i: (0, 0))],
        out_specs=[pl.BlockSpec((SCAN_TR, 2 * H), lambda i: (i, 0)),
                   pl.BlockSpec((SCAN_TR, 2 * H), lambda i: (i, 0))],
        out_shape=[jax.ShapeDtypeStruct((n_rows, 2 * H), BF16),
                   jax.ShapeDtypeStruct((n_rows, 2 * H), F32)],
        scratch_shapes=[pltpu.VMEM((4, 8, 2 * H), F32), pltpu.VMEM((8, 2 * H), F32)],
        compiler_params=_cparams(("arbitrary",)),
        name="ssm_scan_fwd",
    )(bu, abar)


def _scan_bwd(gx, xprev, abar, n_rows):
    H = SSM_H
    nt = n_rows // SCAN_TR

    def body(g_ref, xp_ref, a_ref, h_ref, da_ref, tab_ref, carry_ref):
        @pl.when(pl.program_id(0) == 0)
        def _():
            _scan_tables(a_ref, tab_ref, True)
            carry_ref[...] = jnp.zeros(carry_ref.shape, F32)
            da_ref[...] = jnp.zeros(da_ref.shape, F32)

        row = lax.broadcasted_iota(jnp.int32, (8, H), 0)
        n16 = SCAN_TR // 16

        def group(jj, carry):
            cr, ci = carry
            r0 = pl.multiple_of((n16 - 1 - jj) * 16, 16)
            hr, hi = [None, None], [None, None]
            for half in (1, 0):
                rr = pl.multiple_of(r0 + 8 * half, 8)
                vr, vi = _scan_group(g_ref[pl.ds(rr, 8), :H], g_ref[pl.ds(rr, 8), H:],
                                     cr, ci, tab_ref, True)
                pr, pi = xp_ref[pl.ds(rr, 8), :H], xp_ref[pl.ds(rr, 8), H:]
                da_ref[:, :H] += vr * pr + vi * pi
                da_ref[:, H:] += vi * pr - vr * pi
                cr, ci = _bcast_row(vr, row, 0), _bcast_row(vi, row, 0)
                hr[half], hi[half] = vr, vi
            h_ref[pl.ds(r0, 16), :H] = jnp.concatenate(hr, axis=0).astype(BF16)
            h_ref[pl.ds(r0, 16), H:] = jnp.concatenate(hi, axis=0).astype(BF16)
            return cr, ci

        cr, ci = lax.fori_loop(0, n16, group, (carry_ref[:, :H], carry_ref[:, H:]))
        carry_ref[:, :H] = cr
        carry_ref[:, H:] = ci

    rev = lambda i: (nt - 1 - i, 0)
    return pl.pallas_call(
        body,
        grid=(nt,),
        in_specs=[pl.BlockSpec((SCAN_TR, 2 * H), rev),
                  pl.BlockSpec((SCAN_TR, 2 * H), rev),
                  pl.BlockSpec((1, 2 * H), lambda i: (0, 0))],
        out_specs=[pl.BlockSpec((SCAN_TR, 2 * H), rev),
                   pl.BlockSpec((8, 2 * H), lambda i: (0, 0))],
        out_shape=[jax.ShapeDtypeStruct((n_rows, 2 * H), BF16),
                   jax.ShapeDtypeStruct((8, 2 * H), F32)],
        scratch_shapes=[pltpu.VMEM((4, 8, 2 * H), F32), pltpu.VMEM((8, 2 * H), F32)],
        compiler_params=_cparams(("arbitrary",)),
        name="ssm_scan_bwd",
    )(gx, xprev, abar)


def _adamw(parts, w, m, v, *, tr, ch, name):
    n_rows, cols = w.shape
    n_parts = len(parts)
    c1 = 1.0 - ADAM_B1 ** ADAM_STEP
    c2 = 1.0 - ADAM_B2 ** ADAM_STEP

    def fn(rv, vv, i, nt):
        g = rv[0].astype(F32)
        for p in rv[1:n_parts]:
            g = g + p.astype(F32)
        wv, mv, vval = rv[n_parts:]
        nm = ADAM_B1 * mv + (1.0 - ADAM_B1) * g
        nv = ADAM_B2 * vval + (1.0 - ADAM_B2) * (g * g)
        delta = -ADAM_LR * ((nm / c1) / (jnp.sqrt(nv / c2) + ADAM_EPS) + ADAM_WD * wv)
        return [g, delta, nm, nv], []

    rows = [_row(arr, lead=lead) for (arr, lead) in parts] + [_row(w), _row(m), _row(v)]
    return _rowwise(fn, rows, [], [(cols, F32)] * 4, [], n_rows=n_rows, tr=tr, ch=ch, name=name)


_PACK = [
    ("b_ada", 6), ("norm1_g", 1), ("b_in", 3), ("norm2_g", 1), ("final_g", 1),
    ("lambda_re", 1), ("lambda_im", 1), ("log_step", 1), ("attn_sinks", 1),
    ("rel_bias", 1), ("b_glu", 1), ("ssm_d", 1), ("loss", 1),
    ("ssm_b_re", 16), ("ssm_b_im", 16), ("ssm_c_re", 16), ("ssm_c_im", 16),
]
_PACK_OFF = {}
_off = 0
for _n, _r in _PACK:
    _PACK_OFF[_n] = (_off, _r)
    _off += _r
PACK_ROWS = -(-_off // 8) * 8


def _to_rows(a, rows):
    flat = a.reshape(-1).astype(F32)
    pad = rows * PACK_W - flat.shape[0]
    if pad:
        flat = jnp.concatenate([flat, jnp.zeros((pad,), F32)])
    return flat.reshape(rows, PACK_W)


def _b_to_rows(b):
    return jnp.transpose(b, (2, 0, 1)).reshape(SSM_P, SSM_H)


def _rows_to_b(r):
    return jnp.transpose(r.reshape(SSM_P, SSM_G, SSM_N), (1, 2, 0))


def _c_to_rows(cm):
    return jnp.transpose(cm, (1, 0, 2)).reshape(SSM_P, SSM_H)


def _rows_to_c(r):
    return jnp.transpose(r.reshape(SSM_P, SSM_G, SSM_N), (1, 0, 2))


def _pack(vals):
    pieces = []
    for n, r in _PACK:
        if n in vals:
            pieces.append(_to_rows(vals[n], r))
        else:
            pieces.append(jnp.zeros((r, PACK_W), F32))
    pad = PACK_ROWS - _off
    if pad:
        pieces.append(jnp.zeros((pad, PACK_W), F32))
    return jnp.concatenate(pieces, axis=0)


def _unpack(packed, name, shape):
    o, r = _PACK_OFF[name]
    n = int(np.prod(shape))
    return packed[o:o + r].reshape(-1)[:n].reshape(shape)


def _small_params_packed(p):
    return {
        "b_ada": p["b_ada"], "norm1_g": p["norm1_g"], "b_in": p["b_in"],
        "norm2_g": p["norm2_g"], "final_g": p["final_g"],
        "lambda_re": p["lambda_re"], "lambda_im": p["lambda_im"],
        "log_step": p["log_step"], "attn_sinks": p["attn_sinks"],
        "rel_bias": p["rel_bias"], "b_glu": p["b_glu"], "ssm_d": p["ssm_d"],
        "ssm_b_re": _b_to_rows(p["ssm_b_re"][0]), "ssm_b_im": _b_to_rows(p["ssm_b_im"][0]),
        "ssm_c_re": _c_to_rows(p["ssm_c_re"][0]), "ssm_c_im": _c_to_rows(p["ssm_c_im"][0]),
    }


_SMALL_SHAPES = {
    "b_ada": (1, N_MOD * D), "norm1_g": (1, D), "b_in": (1, IN_W), "norm2_g": (1, D),
    "final_g": (D,), "lambda_re": (1, SSM_G, SSM_N), "lambda_im": (1, SSM_G, SSM_N),
    "log_step": (1, SSM_G), "attn_sinks": (1, N_Q_HEADS), "rel_bias": (NUM_BUCKETS, N_Q_HEADS),
    "b_glu": (1, SSM_W), "ssm_d": (1, SSM_W),
}


def _unpack_small(packed, name):
    if name in ("ssm_b_re", "ssm_b_im"):
        o, r = _PACK_OFF[name]
        return _rows_to_b(packed[o:o + r])[None]
    if name in ("ssm_c_re", "ssm_c_im"):
        o, r = _PACK_OFF[name]
        return _rows_to_c(packed[o:o + r])[None]
    return _unpack(packed, name, _SMALL_SHAPES[name])


WEIGHT_ORDER = ['w_ada', 'b_ada', 'norm1_g', 'w_in', 'b_in', 'attn_sinks', 'rel_bias', 'lambda_re',
                'lambda_im', 'log_step', 'ssm_b_re', 'ssm_b_im', 'ssm_c_re', 'ssm_c_im', 'ssm_d',
                'w_glu', 'b_glu', 'w_attn_proj', 'w_ssm_proj', 'w_out', 'norm2_g', 'w_ff1', 'w_ff2',
                'final_g']
BIG = ['w_in', 'w_glu', 'w_attn_proj', 'w_ssm_proj', 'w_out', 'w_ff1', 'w_ff2']


ADAMW_TILE_ELEMS = 1 << 18


def _adamw_rows(rows, cols):
    tr = rows
    while tr * cols > ADAMW_TILE_ELEMS and tr % 32 == 0:
        tr //= 2
    return tr


def _cast_bf16(w, name):
    rows, cols = w.shape
    tr = min(rows, 256)
    return _rowwise(lambda rv, vv, i, nt: ([rv[0]], []), [_row(w)], [], [(cols, BF16)], [],
                    n_rows=rows, tr=tr, ch=min(tr, 32), name=name)[0]


def kernel(x, c, w_ada, b_ada, norm1_g, w_in, b_in, attn_sinks, rel_bias, lambda_re, lambda_im, log_step, ssm_b_re, ssm_b_im, ssm_c_re, ssm_c_im, ssm_d, w_glu, b_glu, w_attn_proj, w_ssm_proj, w_out, norm2_g, w_ff1, w_ff2, final_g, loss_target, m_w_ada, m_b_ada, m_norm1_g, m_w_in, m_b_in, m_attn_sinks, m_rel_bias, m_lambda_re, m_lambda_im, m_log_step, m_ssm_b_re, m_ssm_b_im, m_ssm_c_re, m_ssm_c_im, m_ssm_d, m_w_glu, m_b_glu, m_w_attn_proj, m_w_ssm_proj, m_w_out, m_norm2_g, m_w_ff1, m_w_ff2, m_final_g, v_w_ada, v_b_ada, v_norm1_g, v_w_in, v_b_in, v_attn_sinks, v_rel_bias, v_lambda_re, v_lambda_im, v_log_step, v_ssm_b_re, v_ssm_b_im, v_ssm_c_re, v_ssm_c_im, v_ssm_d, v_w_glu, v_b_glu, v_w_attn_proj, v_w_ssm_proj, v_w_out, v_norm2_g, v_w_ff1, v_w_ff2, v_final_g):
    loc = dict(locals())
    W = {n: loc[n] for n in WEIGHT_ORDER}
    Mo = {n: loc["m_" + n] for n in WEIGHT_ORDER}
    Vo = {n: loc["v_" + n] for n in WEIGHT_ORDER}
    S = x.shape[1]
    TM = min(512, S)
    TS = min(1024, S)
    TR = min(256, S)
    me = 4 * lax.axis_index("x") + 2 * lax.axis_index("y") + lax.axis_index("c")
    x2d = x.reshape(S, D)
    tgt = loss_target.reshape(S, D)

    shard = {n: W[n][0] for n in BIG}
    gathered = _allgather_weights([_cast_bf16(shard[n], "cast_" + n) for n in BIG], "allgather_weights")
    G = dict(zip(BIG, gathered))
    w_glu_f = G["w_glu"].reshape(SSM_W, SSM_W)
    w_out_f = G["w_out"].reshape(D, D)
    w_ff2_f = G["w_ff2"].reshape(D_FF, D)

    c_all = _small_allgather(c, "allgather_c").reshape(N_DEV, D)
    cs = _rowwise(lambda rv, vv, i, nt: ([rv[0] * _sigmoid(rv[0])], []), [_row(c_all)], [],
                  [(D, F32)], [], n_rows=N_DEV, tr=8, ch=8, name="silu_c")[0]
    n_ada = N_MOD * D // N_DEV
    b_ada_cols = lax.dynamic_slice(b_ada, (0, me * n_ada), (1, n_ada))
    mod_piece = _matmul(cs, w_ada[0], mode="nn", dims=(N_DEV, n_ada, D), tiles=(N_DEV, 512, D),
                        out_dtypes=[F32], name="ada_fwd", bias=b_ada_cols)
    mod_all = _small_allgather(mod_piece, "allgather_mod")
    mod_b = lax.dynamic_index_in_dim(mod_all, me, axis=1, keepdims=False).reshape(N_MOD, D)
    sh1, sc1, g1, sh2, sc2, g2 = [mod_b[i:i + 1] for i in range(N_MOD)]

    def f_norm1(rv, vv, i, nt):
        xv, (g, sc, sh) = rv[0], vv
        return [(xv * _rms(xv) * g) * (1.0 + sc) + sh], []

    h = _rowwise(f_norm1, [_row(x2d)], [norm1_g, sc1, sh1], [(D, BF16)], [],
                 n_rows=S, tr=TR, ch=32, name="norm1_fwd")[0]
    proj = _matmul(h, G["w_in"], mode="nn", dims=(S, IN_W, D), tiles=(TM, 768, D),
                   out_dtypes=[F32], name="in_proj", b3=True, bias=b_in)

    buckets = _t5_buckets_block()
    band = _band_mask()
    bias_tab = jnp.transpose(rel_bias[buckets], (2, 0, 1))
    biasm = jnp.where(band[None], bias_tab, NEG_INF).reshape(N_Q_HEADS * BLK, 2 * BLK)
    sinkcol = jnp.repeat(attn_sinks.reshape(N_Q_HEADS), BLK).reshape(N_Q_HEADS * BLK, 1)
    attn = _attention_fwd(proj, biasm, sinkcol, S)
    y_attn = _matmul(attn, G["w_attn_proj"], mode="nn", dims=(S, D, ATTN_W), tiles=(TM, 256, ATTN_W),
                     out_dtypes=[F32], name="attn_proj", b3=True)

    lam_re = lambda_re.reshape(1, SSM_H)
    lam_im = lambda_im.reshape(1, SSM_H)
    ls_x = jnp.repeat(log_step.reshape(SSM_G), SSM_N).reshape(1, SSM_H)
    btr, bti = _b_to_rows(ssm_b_re[0]), _b_to_rows(ssm_b_im[0])
    ctr, cti = _c_to_rows(ssm_c_re[0]), _c_to_rows(ssm_c_im[0])
    abar, bcat, ccat = _ssm_setup(lam_re, lam_im, ls_x, btr, bti, ctr, cti)
    u_blk = (ATTN_W + 2 * KV_W) // SSM_W
    bu = _matmul(proj, bcat, mode="nn", dims=(S, 2 * SSM_H, SSM_W), tiles=(TM, 2048, SSM_W),
                 out_dtypes=[F32], name="ssm_bu", a_off=u_blk)
    xs, xprev = _scan_fwd(bu, abar, S)
    yc = _matmul(xs, ccat, mode="nt", dims=(S, SSM_W, 2 * SSM_H), tiles=(TM, SSM_W, 2 * SSM_H),
                 out_dtypes=[F32], name="ssm_cx")

    def f_ssm_out(rv, vv, i, nt):
        y = rv[0] + vv[0] * rv[1]
        return [y, _gelu(y)], []

    y_ssm_pre, z = _rowwise(f_ssm_out, [_row(yc), _row(proj, u_blk, SSM_W)], [ssm_d],
                            [(SSM_W, F32), (SSM_W, BF16)], [], n_rows=S, tr=TM, ch=32, name="ssm_out")
    zg = _matmul(z, w_glu_f, mode="nn", dims=(S, SSM_W, SSM_W), tiles=(TM, SSM_W, SSM_W),
                 out_dtypes=[F32], name="glu_proj", bias=b_glu)
    z2 = _rowwise(lambda rv, vv, i, nt: ([rv[0].astype(F32) * _sigmoid(rv[1])], []),
                  [_row(z), _row(zg)], [], [(SSM_W, BF16)], [], n_rows=S, tr=TM, ch=32, name="glu_gate")[0]
    y_ssm = _matmul(z2, G["w_ssm_proj"], mode="nn", dims=(S, D, SSM_W), tiles=(TM, 256, SSM_W),
                    out_dtypes=[F32], name="ssm_proj", b3=True)

    ga_row = _row(proj, 1, D)
    gs_row = _row(proj, 2, D)

    def f_merge(rv, vv, i, nt):
        ga, gs, ya, ys = rv
        return [_sigmoid(ga) * ya + _sigmoid(gs) * ys], []

    merged = _rowwise(f_merge, [ga_row, gs_row, _row(y_attn), _row(y_ssm)], [], [(D, BF16)], [],
                      n_rows=S, tr=TR, ch=32, name="merge")[0]
    mo = _matmul(merged, w_out_f, mode="nn", dims=(S, D, D), tiles=(TM, 1024, D),
                 out_dtypes=[F32], name="out_proj")

    def f_norm2(rv, vv, i, nt):
        xv, mv = rv
        g1v, g, sc, sh = vv
        x1v = xv + g1v * mv
        return [x1v, (x1v * _rms(x1v) * g) * (1.0 + sc) + sh], []

    x1, h2 = _rowwise(f_norm2, [_row(x2d), _row(mo)], [g1, norm2_g, sc2, sh2],
                      [(D, F32), (D, BF16)], [], n_rows=S, tr=TR, ch=32, name="norm2_fwd")

    def relu_sq(acc):
        r = jnp.maximum(acc, 0.0)
        return r * r, r

    act, relu = _matmul(h2, G["w_ff1"], mode="nn", dims=(S, D_FF, D), tiles=(TM, 1024, D),
                        out_dtypes=[BF16, BF16], name="ff1", b3=True, epilogue=relu_sq)
    ff = _matmul(act, w_ff2_f, mode="nn", dims=(S, D, D_FF), tiles=(TM, 1024, 2048),
                 out_dtypes=[F32], name="ff2")

    def f_loss(rv, vv, i, nt):
        x1v, ffv, tv = rv
        g2v, gf = vv
        x2v = x1v + g2v * ffv
        r = _rms(x2v)
        xh = x2v * r
        diff = xh * gf - tv
        dy = diff * (1.0 / D)
        dxh = dy * gf
        dx2 = r * (dxh - xh * jnp.mean(dxh * xh, axis=-1, keepdims=True))
        return [dx2, dx2 * g2v], [_colsum(0.5 * diff * diff * (1.0 / D)), _colsum(dy * xh),
                                  _colsum(dx2 * ffv)]

    dx2, dff, loss_cols, d_final_g, dg2 = _rowwise(
        f_loss, [_row(x1), _row(ff), _row(tgt)], [g2, final_g.reshape(1, D)],
        [(D, F32), (D, BF16)], [(1, D)] * 3, n_rows=S, tr=TR, ch=32, name="loss_bwd")

    df1 = _matmul(dff, w_ff2_f, mode="nt", dims=(S, D_FF, D), tiles=(TM, 1024, D),
                  out_dtypes=[BF16], name="ff2_dgrad", extras=(relu,),
                  epilogue=lambda acc, r: (acc * (2.0 * r.astype(F32)),))
    gw_ff2 = _matmul(act, dff, mode="tn", dims=(D_FF, D, S), tiles=(1024, 1024, TS),
                     out_dtypes=[BF16], name="ff2_wgrad").reshape(N_DEV, D_FF // N_DEV, D)
    dh2 = _matmul(df1, G["w_ff1"], mode="nt", dims=(S, D, D_FF), tiles=(TM, D, 1024),
                  out_dtypes=[F32], name="ff1_dgrad", b3=True)
    gw_ff1 = _matmul(h2, df1, mode="tn", dims=(D, D_FF, S), tiles=(1024, 1024, TS),
                     out_dtypes=[BF16], name="ff1_wgrad", out3=True)

    def f_norm2_bwd(rv, vv, i, nt):
        x1v, dh, dx2v, mv = rv
        g, sc, g1v = vv
        r = _rms(x1v)
        xh = x1v * r
        t = xh * g
        dt = dh * (1.0 + sc)
        dxh = dt * g
        dx1 = dx2v + r * (dxh - xh * jnp.mean(dxh * xh, axis=-1, keepdims=True))
        return [dx1, dx1 * g1v], [_colsum(dh), _colsum(dh * t), _colsum(dt * xh), _colsum(dx1 * mv)]

    dx1, dmo, dsh2, dsc2, d_norm2_g, dg1 = _rowwise(
        f_norm2_bwd, [_row(x1), _row(dh2), _row(dx2), _row(mo)], [norm2_g, sc2, g1],
        [(D, F32), (D, BF16)], [(1, D)] * 4, n_rows=S, tr=TR, ch=16, name="norm2_bwd")

    dmerged = _matmul(dmo, w_out_f, mode="nt", dims=(S, D, D), tiles=(TM, 1024, D),
                      out_dtypes=[F32], name="out_dgrad")
    gw_out = _matmul(merged, dmo, mode="tn", dims=(D, D, S), tiles=(1024, 1024, TS),
                     out_dtypes=[BF16], name="out_wgrad").reshape(N_DEV, D // N_DEV, D)

    def f_merge_bwd(rv, vv, i, nt):
        dm, ga, gs, ya, ys = rv
        sa, ss = _sigmoid(ga), _sigmoid(gs)
        return [dm * sa, dm * ss, dm * ya * sa * (1.0 - sa), dm * ys * ss * (1.0 - ss)], []

    dy_attn, dy_ssm, dga, dgs = _rowwise(
        f_merge_bwd, [_row(dmerged), ga_row, gs_row, _row(y_attn), _row(y_ssm)], [],
        [(D, BF16)] * 4, [], n_rows=S, tr=TR, ch=16, name="merge_bwd")

    dz2 = _matmul(dy_ssm, G["w_ssm_proj"], mode="nt", dims=(S, SSM_W, D), tiles=(TM, SSM_W, 256),
                  out_dtypes=[F32], name="ssm_proj_dgrad", b3=True)
    gw_ssm_proj = _matmul(z2, dy_ssm, mode="tn", dims=(SSM_W, D, S), tiles=(SSM_W, 256, TS),
                          out_dtypes=[BF16], name="ssm_proj_wgrad", out3=True)

    def f_glu_bwd(rv, vv, i, nt):
        dz2v, zv, zgv = rv
        sg = _sigmoid(zgv)
        dzg = dz2v * zv.astype(F32) * sg * (1.0 - sg)
        return [dzg, dz2v * sg], [_colsum(dzg)]

    dzg, dz_a, d_b_glu = _rowwise(f_glu_bwd, [_row(dz2), _row(z), _row(zg)], [],
                                  [(SSM_W, BF16), (SSM_W, F32)], [(1, SSM_W)],
                                  n_rows=S, tr=TM, ch=32, name="glu_bwd")
    dz_b = _matmul(dzg, w_glu_f, mode="nt", dims=(S, SSM_W, SSM_W), tiles=(TM, SSM_W, SSM_W),
                   out_dtypes=[F32], name="glu_dgrad")
    gw_glu = _matmul(z, dzg, mode="tn", dims=(SSM_W, SSM_W, S), tiles=(SSM_W, SSM_W, TS),
                     out_dtypes=[BF16], name="glu_wgrad").reshape(N_DEV, SSM_W // N_DEV, SSM_W)

    def f_ssm_out_bwd(rv, vv, i, nt):
        dza, dzb, yv, uv = rv
        dy = (dza + dzb) * _gelu_grad(yv)
        return [dy, dy * vv[0]], [_colsum(dy * uv)]

    dy_s, du_a, d_ssm_d = _rowwise(
        f_ssm_out_bwd, [_row(dz_a), _row(dz_b), _row(y_ssm_pre), _row(proj, u_blk, SSM_W)], [ssm_d],
        [(SSM_W, BF16), (SSM_W, F32)], [(1, SSM_W)], n_rows=S, tr=TM, ch=32, name="ssm_out_bwd")
    gx = _matmul(dy_s, ccat, mode="nn", dims=(S, 2 * SSM_H, SSM_W), tiles=(TM, 2048, SSM_W),
                 out_dtypes=[F32], name="ssm_cx_dgrad")
    dccat = _matmul(dy_s, xs, mode="tn", dims=(SSM_W, 2 * SSM_H, S), tiles=(SSM_W, 2048, TS),
                    out_dtypes=[F32], name="ssm_c_wgrad")
    hs, dacc = _scan_bwd(gx, xprev, abar, S)
    du_b = _matmul(hs, bcat, mode="nt", dims=(S, SSM_W, 2 * SSM_H), tiles=(TM, SSM_W, 2 * SSM_H),
                   out_dtypes=[F32], name="ssm_bu_dgrad")
    dbcat = _matmul(proj, hs, mode="tn", dims=(SSM_W, 2 * SSM_H, S), tiles=(SSM_W, 2048, TS),
                    out_dtypes=[F32], name="ssm_b_wgrad", a_off=u_blk)
    grp = np.arange(SSM_H) // SSM_N
    gind = jnp.asarray((grp[:, None] == np.arange(128)[None, :]).astype(np.float32), BF16)
    d_lam_re, d_lam_im, d_ls, d_btr, d_bti, d_ctr, d_cti = _ssm_param_bwd(
        lam_re, lam_im, ls_x, btr, bti, dacc, dbcat, dccat, gind)

    dattn = _matmul(dy_attn, G["w_attn_proj"], mode="nt", dims=(S, ATTN_W, D), tiles=(TM, ATTN_W, 256),
                    out_dtypes=[BF16], name="attn_proj_dgrad", b3=True)
    gw_attn_proj = _matmul(attn, dy_attn, mode="tn", dims=(ATTN_W, D, S), tiles=(ATTN_W, 256, TS),
                           out_dtypes=[BF16], name="attn_proj_wgrad", out3=True)
    dq, dkc, dkp, dvc, dvp, dbias, dsink = _attention_bwd(proj, attn, dattn, biasm, sinkcol, S)
    onehot_t = jnp.asarray(
        (np.arange(128)[:, None] == buckets.reshape(-1)[None, :]).astype(np.float32), BF16)
    d_bias_b, d_sinks = _bucket_reduce(dbias.reshape(N_Q_HEADS, BLK * 2 * BLK),
                                       dsink.reshape(N_Q_HEADS, BLK), onehot_t)

    def f_dproj(rv, vv, i, nt):
        dqv, kc, kp, vc, vp, dua, dub, gav, gsv = rv
        keep = (i < nt - 1).astype(F32)
        dp = jnp.concatenate([dqv.astype(F32), kc + keep * kp, vc + keep * vp, dua + dub,
                              gav.astype(F32), gsv.astype(F32)], axis=-1)
        return [dp], [_colsum(dp)]

    dproj, d_b_in = _rowwise(
        f_dproj, [_row(dq), _row(dkc), _row(dkp, shift=1), _row(dvc), _row(dvp, shift=1),
                  _row(du_a), _row(du_b), _row(dga), _row(dgs)], [],
        [(IN_W, BF16)], [(1, IN_W)], n_rows=S, tr=BLK, ch=16, name="dproj")
    dh = _matmul(dproj, G["w_in"], mode="nt", dims=(S, D, IN_W), tiles=(TM, D, 768),
                 out_dtypes=[F32], name="in_dgrad", b3=True)
    gw_in = _matmul(h, dproj, mode="tn", dims=(D, IN_W, S), tiles=(1024, 768, TS),
                    out_dtypes=[BF16], name="in_wgrad", out3=True)

    def f_norm1_bwd(rv, vv, i, nt):
        xv, dhv, dx1v = rv
        g, sc = vv
        r = _rms(xv)
        xh = xv * r
        t = xh * g
        dt = dhv * (1.0 + sc)
        dxh = dt * g
        dxv = dx1v + r * (dxh - xh * jnp.mean(dxh * xh, axis=-1, keepdims=True))
        return [dxv], [_colsum(dhv), _colsum(dhv * t), _colsum(dt * xh)]

    grad_x, dsh1, dsc1, d_norm1_g = _rowwise(
        f_norm1_bwd, [_row(x2d), _row(dh), _row(dx1)], [norm1_g, sc1],
        [(D, F32)], [(1, D)] * 3, n_rows=S, tr=TR, ch=32, name="norm1_bwd")

    dmod_b = jnp.concatenate([dsh1, dsc1, dg1, dsh2, dsc2, dg2], axis=0)
    part = _pack({
        "b_ada": dmod_b, "norm1_g": d_norm1_g, "b_in": d_b_in, "norm2_g": d_norm2_g,
        "final_g": d_final_g, "lambda_re": d_lam_re, "lambda_im": d_lam_im,
        "log_step": d_ls[0, :SSM_G], "attn_sinks": d_sinks[:, 0],
        "rel_bias": jnp.transpose(d_bias_b[:, :NUM_BUCKETS]), "b_glu": d_b_glu, "ssm_d": d_ssm_d,
        "loss": loss_cols, "ssm_b_re": d_btr, "ssm_b_im": d_bti, "ssm_c_re": d_ctr, "ssm_c_im": d_cti,
    })
    part_all = _small_allgather(part, "allgather_small_grads")
    wp, mp, vp = [_pack(_small_params_packed(p)) for p in (W, Mo, Vo)]
    sg, sdelta, sm, sv = _adamw([(part_all, d) for d in range(N_DEV)], wp, mp, vp,
                                tr=PACK_ROWS, ch=8, name="adamw_small")
    lo, _ = _PACK_OFF["loss"]
    loss = jnp.sum(sg[lo])

    o_ada, _ = _PACK_OFF["b_ada"]
    dmod_all = part_all[:, o_ada:o_ada + N_MOD, :].reshape(N_DEV, N_MOD * D)
    dmod_cols = lax.dynamic_slice(dmod_all, (0, me * n_ada), (N_DEV, n_ada))
    gw_ada = _matmul(cs, dmod_cols, mode="tn", dims=(D, n_ada, N_DEV), tiles=(D, 512, N_DEV),
                     out_dtypes=[F32], name="ada_wgrad")

    gbig = {"w_in": gw_in, "w_glu": gw_glu, "w_attn_proj": gw_attn_proj, "w_ssm_proj": gw_ssm_proj,
            "w_out": gw_out, "w_ff1": gw_ff1, "w_ff2": gw_ff2}
    recv = dict(zip(BIG, _exchange_grads([gbig[n] for n in BIG], "exchange_grads")))
    big_out = {}
    for n in BIG:
        rows, cols = shard[n].shape
        big_out[n] = _adamw([(recv[n], d) for d in range(N_DEV)], shard[n], Mo[n][0], Vo[n][0],
                            tr=_adamw_rows(rows, cols), ch=16, name="adamw_" + n)
    big_out["w_ada"] = _adamw([(gw_ada, 0)], w_ada[0], m_w_ada[0], v_w_ada[0],
                              tr=_adamw_rows(D, n_ada), ch=16, name="adamw_w_ada")

    def leaf(kind, n):
        if n in big_out:
            return big_out[n][kind][None]
        return _unpack_small((sg, sdelta, sm, sv)[kind], n)

    outs = [loss, grad_x.reshape(1, S, D)]
    for kind in range(4):
        outs.extend(leaf(kind, n) for n in WEIGHT_ORDER)
    return tuple(outs)
```

```python
import functools
import math

import numpy as np
import jax
import jax.numpy as jnp
from jax import lax
from jax.experimental import pallas as pl
from jax.experimental.pallas import tpu as pltpu

F32 = jnp.float32
BF16 = jnp.bfloat16
MESH = pl.DeviceIdType.MESH

N_DEV = 8
D = 2048
HEAD_DIM = 64
N_Q_HEADS = 16
N_KV_HEADS = 4
GROUP = N_Q_HEADS // N_KV_HEADS
ATTN_W = N_Q_HEADS * HEAD_DIM
KV_W = N_KV_HEADS * HEAD_DIM
BLK = 128
NUM_BUCKETS = 32
MAX_DISTANCE = 128
NEG_INF = -1e30
SSM_W = 512
SSM_P = 16
SSM_G = 32
SSM_N = 64
SSM_H = SSM_G * SSM_N
D_FF = 4 * D
IN_W = ATTN_W + 2 * KV_W + SSM_W + 2 * D
N_MOD = 6
EPS = 1e-6

ADAM_LR = 0.001
ADAM_B1 = 0.9
ADAM_B2 = 0.999
ADAM_EPS = 1e-08
ADAM_WD = 0.01
ADAM_STEP = 10

VMEM_LIMIT = 56 * 1024 * 1024
PACK_W = 2048


def _cparams(sem):
    return pltpu.CompilerParams(dimension_semantics=sem, vmem_limit_bytes=VMEM_LIMIT)


def _matmul(a, b, *, mode, dims, tiles, out_dtypes, name, a_off=0, b3=False,
            out3=False, bias=None, extras=(), epilogue=None):
    M, N, K = dims
    tm, tn, tk = tiles
    assert M % tm == 0 and N % tn == 0 and K % tk == 0, (name, dims, tiles)
    gm, gn, gk = M // tm, N // tn, K // tk
    n_extra = len(extras)
    has_bias = bias is not None
    n_out = len(out_dtypes)

    if mode == "nn":
        a_spec = pl.BlockSpec((tm, tk), lambda i, j, k: (i, a_off + k))
        if b3:
            nb = (N // N_DEV) // tn
            assert nb * tn * N_DEV == N
            b_spec = pl.BlockSpec((None, tk, tn), lambda i, j, k: (j // nb, k, j % nb))
        else:
            b_spec = pl.BlockSpec((tk, tn), lambda i, j, k: (k, j))
        dn = (((1,), (0,)), ((), ()))
    elif mode == "nt":
        a_spec = pl.BlockSpec((tm, tk), lambda i, j, k: (i, a_off + k))
        if b3:
            nb = (K // N_DEV) // tk
            assert nb * tk * N_DEV == K
            b_spec = pl.BlockSpec((None, tn, tk), lambda i, j, k: (k // nb, j, k % nb))
        else:
            b_spec = pl.BlockSpec((tn, tk), lambda i, j, k: (j, k))
        dn = (((1,), (1,)), ((), ()))
    else:
        a_spec = pl.BlockSpec((tk, tm), lambda i, j, k: (k, a_off + i))
        b_spec = pl.BlockSpec((tk, tn), lambda i, j, k: (k, j))
        dn = (((0,), (0,)), ((), ()))

    if out3:
        nbo = (N // N_DEV) // tn
        assert nbo * tn * N_DEV == N
        o_spec = pl.BlockSpec((None, tm, tn), lambda i, j, k: (j // nbo, i, j % nbo))
        o_shape = (N_DEV, M, N // N_DEV)
    else:
        o_spec = pl.BlockSpec((tm, tn), lambda i, j, k: (i, j))
        o_shape = (M, N)

    in_specs = [a_spec, b_spec]
    args = [a, b]
    if has_bias:
        in_specs.append(pl.BlockSpec((1, tn), lambda i, j, k: (0, j)))
        args.append(bias)
    for e in extras:
        in_specs.append(pl.BlockSpec((tm, tn), lambda i, j, k: (i, j)))
        args.append(e)

    def body(*refs):
        a_ref, b_ref = refs[0], refs[1]
        pos = 2
        bias_ref = None
        if has_bias:
            bias_ref = refs[pos]
            pos += 1
        extra_refs = refs[pos:pos + n_extra]
        pos += n_extra
        out_refs = refs[pos:pos + n_out]
        acc_ref = refs[pos + n_out] if gk > 1 else None

        part = lax.dot_general(a_ref[...].astype(BF16), b_ref[...].astype(BF16), dn,
                               preferred_element_type=F32)

        def finish(acc):
            if has_bias:
                acc = acc + bias_ref[...]
            if epilogue is None:
                vals = (acc,)
            else:
                vals = epilogue(acc, *[e[...] for e in extra_refs])
            for o_ref, val in zip(out_refs, vals):
                o_ref[...] = val.astype(o_ref.dtype)

        if gk == 1:
            finish(part)
        else:
            k = pl.program_id(2)

            @pl.when(k == 0)
            def _():
                acc_ref[...] = part

            @pl.when(k > 0)
            def _():
                acc_ref[...] += part

            @pl.when(k == gk - 1)
            def _():
                finish(acc_ref[...])

    outs = pl.pallas_call(
        body,
        grid=(gm, gn, gk),
        in_specs=in_specs,
        out_specs=[o_spec] * n_out,
        out_shape=[jax.ShapeDtypeStruct(o_shape, dt) for dt in out_dtypes],
        scratch_shapes=([pltpu.VMEM((tm, tn), F32)] if gk > 1 else []),
        compiler_params=_cparams(("parallel", "parallel", "arbitrary")),
        name=name,
    )(*args)
    return outs[0] if n_out == 1 else outs


def _rowwise(fn, rows, vecs, row_outs, sum_outs, *, n_rows, tr, ch, name):
    assert n_rows % tr == 0 and tr % ch == 0
    nt = n_rows // tr
    nr, nv, nro, nso = len(rows), len(vecs), len(row_outs), len(sum_outs)
    in_specs, args = [], []
    for (arr, lead, cblk, w, shift) in rows:
        if shift:
            ridx = lambda i, shift=shift: jnp.minimum(i + shift, nt - 1)
        else:
            ridx = lambda i: i
        if arr.ndim == 3:
            in_specs.append(pl.BlockSpec(
                (None, tr, w), lambda i, lead=lead, cblk=cblk, ridx=ridx: (lead, ridx(i), cblk)))
        else:
            in_specs.append(pl.BlockSpec(
                (tr, w), lambda i, cblk=cblk, ridx=ridx: (ridx(i), cblk)))
        args.append(arr)
    for v in vecs:
        in_specs.append(pl.BlockSpec(v.shape, lambda i, nd=v.ndim: (0,) * nd))
        args.append(v)
    out_specs = [pl.BlockSpec((tr, w), lambda i: (i, 0)) for (w, _) in row_outs]
    out_shape = [jax.ShapeDtypeStruct((n_rows, w), dt) for (w, dt) in row_outs]
    for (r, w) in sum_outs:
        out_specs.append(pl.BlockSpec((r, w), lambda i: (0, 0)))
        out_shape.append(jax.ShapeDtypeStruct((r, w), F32))

    def body(*refs):
        i = pl.program_id(0)
        r_in = refs[:nr]
        v_in = refs[nr:nr + nv]
        r_out = refs[nr + nv:nr + nv + nro]
        s_out = refs[nr + nv + nro:]
        if nso:
            @pl.when(i == 0)
            def _():
                for s in s_out:
                    s[...] = jnp.zeros(s.shape, F32)
        vvals = [v[...] for v in v_in]

        def chunk(ci, carry):
            r0 = pl.multiple_of(ci * ch, ch)
            rv = [r[pl.ds(r0, ch), :] for r in r_in]
            ro, so = fn(rv, vvals, i, nt)
            for ref, val in zip(r_out, ro):
                ref[pl.ds(r0, ch), :] = val.astype(ref.dtype)
            for ref, val in zip(s_out, so):
                ref[...] += val
            return carry

        lax.fori_loop(0, tr // ch, chunk, 0)

    outs = pl.pallas_call(
        body,
        grid=(nt,),
        in_specs=in_specs,
        out_specs=out_specs,
        out_shape=out_shape,
        compiler_params=_cparams(("arbitrary",)),
        name=name,
    )(*args)
    return outs


def _row(arr, cblk=0, w=None, lead=0, shift=0):
    return (arr, lead, cblk, arr.shape[-1] if w is None else w, shift)


def _colsum(v):
    return jnp.sum(v, axis=0, keepdims=True)


def _rms(x):
    return lax.rsqrt(jnp.mean(x * x, axis=-1, keepdims=True) + EPS)


def _sigmoid(x):
    return 1.0 / (1.0 + jnp.exp(-x))


_GELU_C = math.sqrt(2.0 / math.pi)


def _gelu(x):
    return 0.5 * x * (1.0 + jnp.tanh(_GELU_C * (x + 0.044715 * (x * x * x))))


def _gelu_grad(x):
    t = jnp.tanh(_GELU_C * (x + 0.044715 * (x * x * x)))
    return 0.5 * (1.0 + t) + 0.5 * x * (1.0 - t * t) * (_GELU_C * (1.0 + 3.0 * 0.044715 * (x * x)))


def _my_pos():
    return lax.axis_index("x"), lax.axis_index("y"), lax.axis_index("c")


def _flip(pos, k):
    x, y, c = pos
    return (1 - x if k & 4 else x, 1 - y if k & 2 else y, 1 - c if k & 1 else c)


def _dev_id(pos):
    return 4 * pos[0] + 2 * pos[1] + pos[2]


def _small_allgather(x, name):
    r, c = x.shape

    def body(x_ref, out_ref, send_sems, recv_sems):
        me = _my_pos()
        out_ref[_dev_id(me)] = x_ref[...]
        copies = []
        for k in range(1, N_DEV):
            cp = pltpu.make_async_remote_copy(
                src_ref=x_ref, dst_ref=out_ref.at[_dev_id(me)],
                send_sem=send_sems.at[k - 1], recv_sem=recv_sems.at[k - 1],
                device_id=_flip(me, k), device_id_type=MESH)
            cp.start()
            copies.append(cp)
        for k in range(1, N_DEV):
            peer = _flip(me, k)
            pltpu.make_async_remote_copy(
                src_ref=x_ref, dst_ref=out_ref.at[_dev_id(peer)],
                send_sem=send_sems.at[k - 1], recv_sem=recv_sems.at[k - 1],
                device_id=peer, device_id_type=MESH).wait_recv()
        for cp in copies:
            cp.wait_send()

    return pl.pallas_call(
        body,
        out_shape=jax.ShapeDtypeStruct((N_DEV, r, c), x.dtype),
        in_specs=[pl.BlockSpec(memory_space=pltpu.VMEM)],
        out_specs=pl.BlockSpec(memory_space=pltpu.VMEM),
        scratch_shapes=[pltpu.SemaphoreType.DMA((N_DEV - 1,)),
                        pltpu.SemaphoreType.DMA((N_DEV - 1,))],
        compiler_params=pltpu.CompilerParams(vmem_limit_bytes=VMEM_LIMIT),
        name=name,
    )(x)


def _allgather_weights(shards, name):
    n = len(shards)

    def body(*refs):
        xs = refs[:n]
        outs = refs[n:2 * n]
        send_sems, recv_sems, local_sems = refs[2 * n:]
        x, y, c = _my_pos()
        me, sib = (x, y, c), (x, y, 1 - c)
        chips = [(1 - x, y), (x, 1 - y), (1 - x, 1 - y)]

        def copy(a, k, block, to, src=None):
            slot = outs[a].at[_dev_id(block)]
            return pltpu.make_async_remote_copy(
                src_ref=slot if src is None else src, dst_ref=slot,
                send_sem=send_sems.at[a, k], recv_sem=recv_sems.at[a, k],
                device_id=to, device_id_type=MESH)

        mine = [pltpu.make_async_copy(xs[a], outs[a].at[_dev_id(me)], local_sems.at[a])
                for a in range(n)]
        for cp in mine:
            cp.start()
        first = []
        for a in range(n):
            first.append(copy(a, 0, me, sib, src=xs[a]))
            for j, chip in enumerate(chips):
                first.append(copy(a, 1 + j, me, (*chip, c), src=xs[a]))
        for cp in first:
            cp.start()
        passed = []
        for a in range(n):
            for j, chip in enumerate(chips):
                copy(a, 1 + j, (*chip, c), me).wait_recv()
                cp = copy(a, 4 + j, (*chip, c), sib)
                cp.start()
                passed.append(cp)
        for a in range(n):
            copy(a, 0, sib, me).wait_recv()
            for j, chip in enumerate(chips):
                copy(a, 4 + j, (*chip, 1 - c), me).wait_recv()
        for cp in first + passed:
            cp.wait_send()
        for cp in mine:
            cp.wait()

    hbm = pl.BlockSpec(memory_space=pl.ANY)
    return pl.pallas_call(
        body,
        out_shape=[jax.ShapeDtypeStruct((N_DEV,) + s.shape, s.dtype) for s in shards],
        in_specs=[hbm] * n,
        out_specs=[hbm] * n,
        scratch_shapes=[pltpu.SemaphoreType.DMA((n, 7)),
                        pltpu.SemaphoreType.DMA((n, 7)),
                        pltpu.SemaphoreType.DMA((n,))],
        name=name,
    )(*shards)


def _exchange_grads(grads, name):
    n = len(grads)

    def body(*refs):
        gs = refs[:n]
        outs = refs[n:2 * n]
        send_sems, recv_sems, local_sems = refs[2 * n:]
        me = _my_pos()
        my_id = _dev_id(me)
        mine = [pltpu.make_async_copy(gs[a].at[my_id], outs[a].at[my_id], local_sems.at[a])
                for a in range(n)]
        for cp in mine:
            cp.start()
        sent = []
        for a in range(n):
            for k in range(1, N_DEV):
                peer = _flip(me, k)
                cp = pltpu.make_async_remote_copy(
                    src_ref=gs[a].at[_dev_id(peer)], dst_ref=outs[a].at[my_id],
                    send_sem=send_sems.at[a, k - 1], recv_sem=recv_sems.at[a, k - 1],
                    device_id=peer, device_id_type=MESH)
                cp.start()
                sent.append(cp)
        for a in range(n):
            for k in range(1, N_DEV):
                peer = _flip(me, k)
                pltpu.make_async_remote_copy(
                    src_ref=gs[a].at[my_id], dst_ref=outs[a].at[_dev_id(peer)],
                    send_sem=send_sems.at[a, k - 1], recv_sem=recv_sems.at[a, k - 1],
                    device_id=peer, device_id_type=MESH).wait_recv()
        for cp in sent:
            cp.wait_send()
        for cp in mine:
            cp.wait()

    hbm = pl.BlockSpec(memory_space=pl.ANY)
    return pl.pallas_call(
        body,
        out_shape=[jax.ShapeDtypeStruct(g.shape, g.dtype) for g in grads],
        in_specs=[hbm] * n,
        out_specs=[hbm] * n,
        scratch_shapes=[pltpu.SemaphoreType.DMA((n, 7)),
                        pltpu.SemaphoreType.DMA((n, 7)),
                        pltpu.SemaphoreType.DMA((n,))],
        name=name,
    )(*grads)


def _t5_buckets_block():
    qi = np.arange(BLK)[:, None]
    ki = np.arange(2 * BLK)[None, :]
    n = np.maximum(qi + BLK - ki, 0)
    max_exact = NUM_BUCKETS // 2
    large = max_exact + (np.log(np.maximum(n, 1) / max_exact)
                         / np.log(MAX_DISTANCE / max_exact)
                         * (NUM_BUCKETS - max_exact)).astype(np.int32)
    large = np.minimum(large, NUM_BUCKETS - 1)
    return np.where(n < max_exact, n, large).astype(np.int32)


def _band_mask():
    qi = np.arange(BLK)[:, None]
    ki = np.arange(2 * BLK)[None, :]
    dist = qi + BLK - ki
    return (dist >= 0) & (dist < BLK)


def _attn_scores(q_ref, kp_ref, kc_ref, hkv):
    c0 = hkv * HEAD_DIM
    kk = jnp.concatenate([kp_ref[:, c0:c0 + HEAD_DIM], kc_ref[:, c0:c0 + HEAD_DIM]],
                         axis=0).astype(BF16)
    qg = jnp.concatenate(
        [q_ref[:, (hkv * GROUP + g) * HEAD_DIM:(hkv * GROUP + g + 1) * HEAD_DIM]
         for g in range(GROUP)], axis=0).astype(BF16)
    s = lax.dot_general(qg, kk, (((1,), (1,)), ((), ())), preferred_element_type=F32)
    return qg, kk, s


def _attn_softmax(s, bias_ref, sink_ref, hkv):
    r0, r1 = hkv * GROUP * BLK, (hkv + 1) * GROUP * BLK
    s = s * (HEAD_DIM ** -0.5) + bias_ref[r0:r1, :]
    sink = sink_ref[r0:r1, :]
    m = jnp.maximum(jnp.max(s, axis=-1, keepdims=True), sink)
    p = jnp.exp(s - m)
    e_sink = jnp.exp(sink - m)
    inv = 1.0 / (jnp.sum(p, axis=-1, keepdims=True) + e_sink)
    return p * inv, e_sink * inv


def _kv_rows(p_ref, c_ref, hkv):
    c0 = hkv * HEAD_DIM
    return jnp.concatenate([p_ref[:, c0:c0 + HEAD_DIM], c_ref[:, c0:c0 + HEAD_DIM]],
                           axis=0).astype(BF16)


def _attn_in_specs(bias2):
    prev = lambda n: jnp.maximum(n - 1, 0)
    return [
        pl.BlockSpec((BLK, ATTN_W), lambda n: (n, 0)),
        pl.BlockSpec((BLK, KV_W), lambda n: (prev(n), ATTN_W // KV_W)),
        pl.BlockSpec((BLK, KV_W), lambda n: (n, ATTN_W // KV_W)),
        pl.BlockSpec((BLK, KV_W), lambda n: (prev(n), ATTN_W // KV_W + 1)),
        pl.BlockSpec((BLK, KV_W), lambda n: (n, ATTN_W // KV_W + 1)),
        pl.BlockSpec((None,) + bias2.shape[1:], lambda n: (jnp.minimum(n, 1), 0, 0)),
    ]


def _attention_fwd(proj, bias2, sinkcol, n_rows):
    nb = n_rows // BLK

    def body(q_ref, kp_ref, kc_ref, vp_ref, vc_ref, bias_ref, sink_ref, o_ref):
        heads = range(N_KV_HEADS)
        scores = [_attn_scores(q_ref, kp_ref, kc_ref, hkv)[2] for hkv in heads]
        probs = [_attn_softmax(scores[hkv], bias_ref, sink_ref, hkv)[0] for hkv in heads]
        outs = [jnp.dot(probs[hkv].astype(BF16), _kv_rows(vp_ref, vc_ref, hkv),
                        preferred_element_type=F32) for hkv in heads]
        for hkv in heads:
            for g in range(GROUP):
                h = hkv * GROUP + g
                o_ref[:, h * HEAD_DIM:(h + 1) * HEAD_DIM] = (
                    outs[hkv][g * BLK:(g + 1) * BLK, :].astype(o_ref.dtype))

    return pl.pallas_call(
        body,
        grid=(nb,),
        in_specs=_attn_in_specs(bias2) + [pl.BlockSpec(sinkcol.shape, lambda n: (0, 0))],
        out_specs=pl.BlockSpec((BLK, ATTN_W), lambda n: (n, 0)),
        out_shape=jax.ShapeDtypeStruct((n_rows, ATTN_W), BF16),
        compiler_params=_cparams(("parallel",)),
        name="attn_fwd",
    )(proj, proj, proj, proj, proj, bias2, sinkcol)


def _attention_bwd(proj, attn, dattn, bias2, sinkcol, n_rows):
    nb = n_rows // BLK
    scale = HEAD_DIM ** -0.5
    dn_t = (((0,), (0,)), ((), ()))

    def body(q_ref, kp_ref, kc_ref, vp_ref, vc_ref, bias_ref, o_ref, do_ref, sink_ref,
             dq_ref, dkc_ref, dkp_ref, dvc_ref, dvp_ref, dbias_ref, dsink_ref):
        @pl.when(pl.program_id(0) == 0)
        def _():
            dbias_ref[...] = jnp.zeros(dbias_ref.shape, F32)
            dsink_ref[...] = jnp.zeros(dsink_ref.shape, F32)

        heads = range(N_KV_HEADS)
        qk = [_attn_scores(q_ref, kp_ref, kc_ref, hkv) for hkv in heads]
        dog, dps, deltas = [], [], []
        for hkv in heads:
            hs = [hkv * GROUP + g for g in range(GROUP)]
            d_o = jnp.concatenate([do_ref[:, h * HEAD_DIM:(h + 1) * HEAD_DIM] for h in hs], axis=0)
            o = jnp.concatenate([o_ref[:, h * HEAD_DIM:(h + 1) * HEAD_DIM] for h in hs], axis=0)
            deltas.append(jnp.sum(d_o.astype(F32) * o.astype(F32), axis=-1, keepdims=True))
            dog.append(d_o.astype(BF16))
            dps.append(lax.dot_general(dog[hkv], _kv_rows(vp_ref, vc_ref, hkv),
                                       (((1,), (1,)), ((), ())), preferred_element_type=F32))
        p16, ds16 = [], []
        for hkv in heads:
            r0, r1 = hkv * GROUP * BLK, (hkv + 1) * GROUP * BLK
            p, p_sink = _attn_softmax(qk[hkv][2], bias_ref, sink_ref, hkv)
            ds = p * (dps[hkv] - deltas[hkv])
            dbias_ref[r0:r1, :] += ds
            dsink_ref[r0:r1, :] += -(p_sink * deltas[hkv])
            p16.append(p.astype(BF16))
            ds16.append(ds.astype(BF16))
        for hkv in heads:
            c0 = hkv * HEAD_DIM
            qg, kk, _ = qk[hkv]
            dqg = jnp.dot(ds16[hkv], kk, preferred_element_type=F32) * scale
            dkk = lax.dot_general(ds16[hkv], qg, dn_t, preferred_element_type=F32) * scale
            dvv = lax.dot_general(p16[hkv], dog[hkv], dn_t, preferred_element_type=F32)
            for g in range(GROUP):
                h = hkv * GROUP + g
                dq_ref[:, h * HEAD_DIM:(h + 1) * HEAD_DIM] = (
                    dqg[g * BLK:(g + 1) * BLK, :].astype(dq_ref.dtype))
            dkp_ref[:, c0:c0 + HEAD_DIM] = dkk[:BLK].astype(dkp_ref.dtype)
            dkc_ref[:, c0:c0 + HEAD_DIM] = dkk[BLK:].astype(dkc_ref.dtype)
            dvp_ref[:, c0:c0 + HEAD_DIM] = dvv[:BLK].astype(dvp_ref.dtype)
            dvc_ref[:, c0:c0 + HEAD_DIM] = dvv[BLK:].astype(dvc_ref.dtype)

    kv_out = pl.BlockSpec((BLK, KV_W), lambda n: (n, 0))
    kv_shape = jax.ShapeDtypeStruct((n_rows, KV_W), F32)
    acc_shape = bias2.shape[1:]
    return pl.pallas_call(
        body,
        grid=(nb,),
        in_specs=_attn_in_specs(bias2) + [
            pl.BlockSpec((BLK, ATTN_W), lambda n: (n, 0)),
            pl.BlockSpec((BLK, ATTN_W), lambda n: (n, 0)),
            pl.BlockSpec(sinkcol.shape, lambda n: (0, 0)),
        ],
        out_specs=[
            pl.BlockSpec((BLK, ATTN_W), lambda n: (n, 0)),
            kv_out, kv_out, kv_out, kv_out,
            pl.BlockSpec(acc_shape, lambda n: (0, 0)),
            pl.BlockSpec(sinkcol.shape, lambda n: (0, 0)),
        ],
        out_shape=[
            jax.ShapeDtypeStruct((n_rows, ATTN_W), BF16),
            kv_shape, kv_shape, kv_shape, kv_shape,
            jax.ShapeDtypeStruct(acc_shape, F32),
            jax.ShapeDtypeStruct(sinkcol.shape, F32),
        ],
        compiler_params=_cparams(("arbitrary",)),
        name="attn_bwd",
    )(proj, proj, proj, proj, proj, bias2, attn, dattn, sinkcol)


def _bias_tables(rel_bias_t, onehot_t, band_first, band_rest):
    def body(rb_ref, oh_ref, mf_ref, mr_ref, out_ref):
        acc = jnp.zeros((N_Q_HEADS, BLK * 2 * BLK), F32)
        for part in _split3(rb_ref[...]):
            acc = acc + jnp.dot(part, oh_ref[...], preferred_element_type=F32)
        out_ref[0] = jnp.where(mf_ref[...] > 0.0, acc, NEG_INF)
        out_ref[1] = jnp.where(mr_ref[...] > 0.0, acc, NEG_INF)

    return pl.pallas_call(
        body,
        out_shape=jax.ShapeDtypeStruct((2, N_Q_HEADS, BLK * 2 * BLK), F32),
        compiler_params=pltpu.CompilerParams(vmem_limit_bytes=VMEM_LIMIT),
        name="bias_tables",
    )(rel_bias_t, onehot_t, band_first, band_rest)


def _split3(a):
    hi = a.astype(BF16)
    r1 = a - hi.astype(F32)
    mid = r1.astype(BF16)
    lo = (r1 - mid.astype(F32)).astype(BF16)
    return hi, mid, lo


def _bucket_reduce(dbias, dsink, onehot_t):
    def body(db_ref, ds_ref, oh_ref, ob_ref, os_ref):
        acc = jnp.zeros((N_Q_HEADS, 128), F32)
        for part in _split3(db_ref[...]):
            acc = acc + lax.dot_general(part, oh_ref[...], (((1,), (1,)), ((), ())),
                                        preferred_element_type=F32)
        ob_ref[...] = acc
        os_ref[...] = jnp.broadcast_to(jnp.sum(ds_ref[...], axis=-1, keepdims=True),
                                       os_ref.shape)

    return pl.pallas_call(
        body,
        out_shape=[jax.ShapeDtypeStruct((N_Q_HEADS, 128), F32),
                   jax.ShapeDtypeStruct((N_Q_HEADS, 128), F32)],
        compiler_params=pltpu.CompilerParams(vmem_limit_bytes=VMEM_LIMIT),
        name="bias_bucket_reduce",
    )(dbias, dsink, onehot_t)


def _disc(lr, li, ls, btr, bti):
    lam_re = jnp.minimum(lr, -1e-4)
    delta = jnp.exp(ls)
    mag = jnp.exp(lam_re * delta)
    ang = li * delta
    ar, ai = mag * jnp.cos(ang), mag * jnp.sin(ang)
    nr, ni = ar - 1.0, ai
    den = lam_re * lam_re + li * li
    fr = (nr * lam_re + ni * li) / den
    fi = (ni * lam_re - nr * li) / den
    bbr = fr * btr - fi * bti
    bbi = fr * bti + fi * btr
    return ar, ai, bbr, bbi


def _block_mask():
    row = lax.broadcasted_iota(jnp.int32, (SSM_W, SSM_H), 0)
    col = lax.broadcasted_iota(jnp.int32, (SSM_W, SSM_H), 1)
    return (row // SSM_P) == (col // SSM_N)


def _ssm_setup(lr, li, ls, btr, bti, ctr, cti):
    def body(lr_ref, li_ref, ls_ref, btr_ref, bti_ref, ctr_ref, cti_ref, a_ref, b_ref, c_ref):
        ar, ai, bbr, bbi = _disc(lr_ref[...], li_ref[...], ls_ref[...], btr_ref[...], bti_ref[...])
        a_ref[:, :SSM_H] = ar
        a_ref[:, SSM_H:] = ai
        mask = _block_mask()
        blk = lambda t: jnp.where(mask, jnp.tile(t, (SSM_G, 1)), 0.0)
        b_ref[:, :SSM_H] = blk(bbr).astype(BF16)
        b_ref[:, SSM_H:] = blk(bbi).astype(BF16)
        c_ref[:, :SSM_H] = blk(ctr_ref[...]).astype(BF16)
        c_ref[:, SSM_H:] = blk(-cti_ref[...]).astype(BF16)

    return pl.pallas_call(
        body,
        out_shape=[jax.ShapeDtypeStruct((1, 2 * SSM_H), F32),
                   jax.ShapeDtypeStruct((SSM_W, 2 * SSM_H), BF16),
                   jax.ShapeDtypeStruct((SSM_W, 2 * SSM_H), BF16)],
        compiler_params=pltpu.CompilerParams(vmem_limit_bytes=VMEM_LIMIT),
        name="ssm_setup",
    )(lr, li, ls, btr, bti, ctr, cti)


def _ssm_param_bwd(lr, li, ls, btr, bti, dacc, dbcat, dccat, gind):
    def body(lr_ref, li_ref, ls_ref, btr_ref, bti_ref, dacc_ref, db_ref, dc_ref, g_ref,
             dlr_ref, dli_ref, dls_ref, dbtr_ref, dbti_ref, dctr_ref, dcti_ref):
        dar = jnp.sum(dacc_ref[:, :SSM_H], axis=0, keepdims=True)
        dai = jnp.sum(dacc_ref[:, SSM_H:], axis=0, keepdims=True)
        col = lax.broadcasted_iota(jnp.int32, (SSM_P, 2 * SSM_H), 1)
        grp = (col % SSM_H) // SSM_N
        db = jnp.zeros((SSM_P, 2 * SSM_H), F32)
        dc = jnp.zeros((SSM_P, 2 * SSM_H), F32)
        for g in range(SSM_G):
            sel = grp == g
            db = db + jnp.where(sel, db_ref[g * SSM_P:(g + 1) * SSM_P, :], 0.0)
            dc = dc + jnp.where(sel, dc_ref[g * SSM_P:(g + 1) * SSM_P, :], 0.0)
        dctr_ref[...] = dc[:, :SSM_H]
        dcti_ref[...] = -dc[:, SSM_H:]
        prim = (lr_ref[...], li_ref[...], ls_ref[...], btr_ref[...], bti_ref[...])
        _, vjp = jax.vjp(_disc, *prim)
        dlr, dli, dls, dbtr, dbti = vjp((dar, dai, db[:, :SSM_H], db[:, SSM_H:]))
        dlr_ref[...] = dlr
        dli_ref[...] = dli
        dbtr_ref[...] = dbtr
        dbti_ref[...] = dbti
        acc = jnp.zeros((8, 128), F32)
        for part in _split3(jnp.broadcast_to(dls, (8, SSM_H))):
            acc = acc + jnp.dot(part, g_ref[...], preferred_element_type=F32)
        dls_ref[...] = acc

    vec = jax.ShapeDtypeStruct((1, SSM_H), F32)
    mat = jax.ShapeDtypeStruct((SSM_P, SSM_H), F32)
    return pl.pallas_call(
        body,
        out_shape=[vec, vec, jax.ShapeDtypeStruct((8, 128), F32), mat, mat, mat, mat],
        compiler_params=pltpu.CompilerParams(vmem_limit_bytes=VMEM_LIMIT),
        name="ssm_param_bwd",
    )(lr, li, ls, btr, bti, dacc, dbcat, dccat, gind)


SCAN_TR = 256


def _cmul_add(vr, vi, pr, pi, sr, si):
    return vr + pr * sr - pi * si, vi + pr * si + pi * sr


def _bcast_row(v, row, which):
    b = jnp.where(row == which, v, 0.0)
    b = b + pltpu.roll(b, 4, 0)
    b = b + pltpu.roll(b, 2, 0)
    return b + pltpu.roll(b, 1, 0)


def _scan_tables(a_ref, tab_ref, reverse):
    H = SSM_H
    ar = jnp.broadcast_to(a_ref[:, :H], (8, H))
    ai = jnp.broadcast_to(a_ref[:, H:], (8, H))
    if reverse:
        ai = -ai
    row = lax.broadcasted_iota(jnp.int32, (8, H), 0)
    pw = [(ar, ai)]
    for _ in range(7):
        cr, ci = pw[-1]
        pw.append((cr * ar - ci * ai, cr * ai + ci * ar))
    pcr = jnp.zeros((8, H), F32)
    pci = jnp.zeros((8, H), F32)
    for e in range(8):
        sel = (row == (7 - e)) if reverse else (row == e)
        pcr = jnp.where(sel, pw[e][0], pcr)
        pci = jnp.where(sel, pw[e][1], pci)
    tab_ref[0, :, :H] = pcr
    tab_ref[0, :, H:] = pci
    for t, k in enumerate((1, 2, 4)):
        keep = (row < 8 - k) if reverse else (row >= k)
        tab_ref[1 + t, :, :H] = jnp.where(keep, pw[k - 1][0], 0.0)
        tab_ref[1 + t, :, H:] = jnp.where(keep, pw[k - 1][1], 0.0)


def _scan_group(vr, vi, cr, ci, tab_ref, reverse):
    H = SSM_H
    for t, k in enumerate((1, 2, 4)):
        sh = 8 - k if reverse else k
        vr, vi = _cmul_add(vr, vi, tab_ref[1 + t, :, :H], tab_ref[1 + t, :, H:],
                           pltpu.roll(vr, sh, 0), pltpu.roll(vi, sh, 0))
    return _cmul_add(vr, vi, tab_ref[0, :, :H], tab_ref[0, :, H:], cr, ci)


def _scan_fwd(bu, abar, n_rows):
    H = SSM_H
    nt = n_rows // SCAN_TR

    def body(bu_ref, a_ref, xs_ref, xp_ref, tab_ref, carry_ref):
        @pl.when(pl.program_id(0) == 0)
        def _():
            _scan_tables(a_ref, tab_ref, False)
            carry_ref[...] = jnp.zeros(carry_ref.shape, F32)

        row = lax.broadcasted_iota(jnp.int32, (8, H), 0)

        def group(j, carry):
            cr, ci = carry
            r0 = pl.multiple_of(j * 16, 16)
            xr, xi = [], []
            for half in range(2):
                rr = pl.multiple_of(r0 + 8 * half, 8)
                vr, vi = _scan_group(bu_ref[pl.ds(rr, 8), :H], bu_ref[pl.ds(rr, 8), H:],
                                     cr, ci, tab_ref, False)
                xp_ref[pl.ds(rr, 8), :H] = jnp.where(row == 0, cr, pltpu.roll(vr, 1, 0))
                xp_ref[pl.ds(rr, 8), H:] = jnp.where(row == 0, ci, pltpu.roll(vi, 1, 0))
                cr, ci = _bcast_row(vr, row, 7), _bcast_row(vi, row, 7)
                xr.append(vr)
                xi.append(vi)
            xs_ref[pl.ds(r0, 16), :H] = jnp.concatenate(xr, axis=0).astype(BF16)
            xs_ref[pl.ds(r0, 16), H:] = jnp.concatenate(xi, axis=0).astype(BF16)
            return cr, ci

        cr, ci = lax.fori_loop(0, SCAN_TR // 16, group,
                               (carry_ref[:, :H], carry_ref[:, H:]))
        carry_ref[:, :H] = cr
        carry_ref[:, H:] = ci

    return pl.pallas_call(
        body,
        grid=(nt,),
        in_specs=[pl.BlockSpec((SCAN_TR, 2 * H), lambda i: (i, 0)),
                  pl.BlockSpec((1, 2 * H), lambda i: (0, 0))],
        out_specs=[pl.BlockSpec((SCAN_TR, 2 * H), lambda i: (i, 0)),
                   pl.BlockSpec((SCAN_TR, 2 * H), lambda i: (i, 0))],
        out_shape=[jax.ShapeDtypeStruct((n_rows, 2 * H), BF16),
                   jax.ShapeDtypeStruct((n_rows, 2 * H), F32)],
        scratch_shapes=[pltpu.VMEM((4, 8, 2 * H), F32), pltpu.VMEM((8, 2 * H), F32)],
        compiler_params=_cparams(("arbitrary",)),
        name="ssm_scan_fwd",
    )(bu, abar)


def _scan_bwd(gx, xprev, abar, n_rows):
    H = SSM_H
    nt = n_rows // SCAN_TR

    def body(g_ref, xp_ref, a_ref, h_ref, da_ref, tab_ref, carry_ref):
        @pl.when(pl.program_id(0) == 0)
        def _():
            _scan_tables(a_ref, tab_ref, True)
            carry_ref[...] = jnp.zeros(carry_ref.shape, F32)
            da_ref[...] = jnp.zeros(da_ref.shape, F32)

        row = lax.broadcasted_iota(jnp.int32, (8, H), 0)
        n16 = SCAN_TR // 16

        def group(jj, carry):
            cr, ci = carry
            r0 = pl.multiple_of((n16 - 1 - jj) * 16, 16)
            hr, hi = [None, None], [None, None]
            for half in (1, 0):
                rr = pl.multiple_of(r0 + 8 * half, 8)
                vr, vi = _scan_group(g_ref[pl.ds(rr, 8), :H], g_ref[pl.ds(rr, 8), H:],
                                     cr, ci, tab_ref, True)
                pr, pi = xp_ref[pl.ds(rr, 8), :H], xp_ref[pl.ds(rr, 8), H:]
                da_ref[:, :H] += vr * pr + vi * pi
                da_ref[:, H:] += vi * pr - vr * pi
                cr, ci = _bcast_row(vr, row, 0), _bcast_row(vi, row, 0)
                hr[half], hi[half] = vr, vi
            h_ref[pl.ds(r0, 16), :H] = jnp.concatenate(hr, axis=0).astype(BF16)
            h_ref[pl.ds(r0, 16), H:] = jnp.concatenate(hi, axis=0).astype(BF16)
            return cr, ci

        cr, ci = lax.fori_loop(0, n16, group, (carry_ref[:, :H], carry_ref[:, H:]))
        carry_ref[:, :H] = cr
        carry_ref[:, H:] = ci

    rev = lambda i: (nt - 1 - i, 0)
    return pl.pallas_call(
        body,
        grid=(nt,),
        in_specs=[pl.BlockSpec((SCAN_TR, 2 * H), rev),
                  pl.BlockSpec((SCAN_TR, 2 * H), rev),
                  pl.BlockSpec((1, 2 * H), lambda i: (0, 0))],
        out_specs=[pl.BlockSpec((SCAN_TR, 2 * H), rev),
                   pl.BlockSpec((8, 2 * H), lambda i: (0, 0))],
        out_shape=[jax.ShapeDtypeStruct((n_rows, 2 * H), BF16),
                   jax.ShapeDtypeStruct((8, 2 * H), F32)],
        scratch_shapes=[pltpu.VMEM((4, 8, 2 * H), F32), pltpu.VMEM((8, 2 * H), F32)],
        compiler_params=_cparams(("arbitrary",)),
        name="ssm_scan_bwd",
    )(gx, xprev, abar)


def _adamw(parts, w, m, v, *, tr, ch, name):
    n_rows, cols = w.shape
    n_parts = len(parts)
    c1 = 1.0 - ADAM_B1 ** ADAM_STEP
    c2 = 1.0 - ADAM_B2 ** ADAM_STEP

    def fn(rv, vv, i, nt):
        g = rv[0].astype(F32)
        for p in rv[1:n_parts]:
            g = g + p.astype(F32)
        wv, mv, vval = rv[n_parts:]
        nm = ADAM_B1 * mv + (1.0 - ADAM_B1) * g
        nv = ADAM_B2 * vval + (1.0 - ADAM_B2) * (g * g)
        delta = -ADAM_LR * ((nm / c1) / (jnp.sqrt(nv / c2) + ADAM_EPS) + ADAM_WD * wv)
        return [g, delta, nm, nv], []

    rows = [_row(arr, lead=lead) for (arr, lead) in parts] + [_row(w), _row(m), _row(v)]
    return _rowwise(fn, rows, [], [(cols, F32)] * 4, [], n_rows=n_rows, tr=tr, ch=ch, name=name)


_PACK = [
    ("b_ada", 6), ("norm1_g", 1), ("b_in", 3), ("norm2_g", 1), ("final_g", 1),
    ("lambda_re", 1), ("lambda_im", 1), ("log_step", 1), ("attn_sinks", 1),
    ("rel_bias", 1), ("b_glu", 1), ("ssm_d", 1), ("loss", 1),
    ("ssm_b_re", 16), ("ssm_b_im", 16), ("ssm_c_re", 16), ("ssm_c_im", 16),
]
_PACK_OFF = {}
_off = 0
for _n, _r in _PACK:
    _PACK_OFF[_n] = (_off, _r)
    _off += _r
PACK_ROWS = -(-_off // 8) * 8


def _to_rows(a, rows):
    flat = a.reshape(-1).astype(F32)
    pad = rows * PACK_W - flat.shape[0]
    if pad:
        flat = jnp.pad(flat, (0, pad))
    return flat.reshape(rows, PACK_W)


def _b_to_rows(b):
    return jnp.transpose(b, (2, 0, 1)).reshape(SSM_P, SSM_H)


def _rows_to_b(r):
    return jnp.transpose(r.reshape(SSM_P, SSM_G, SSM_N), (1, 2, 0))


def _c_to_rows(cm):
    return jnp.transpose(cm, (1, 0, 2)).reshape(SSM_P, SSM_H)


def _rows_to_c(r):
    return jnp.transpose(r.reshape(SSM_P, SSM_G, SSM_N), (1, 0, 2))


def _pack(vals):
    out = jnp.zeros((PACK_ROWS, PACK_W), F32)
    for n, r in _PACK:
        if n in vals:
            pieces = vals[n] if isinstance(vals[n], list) else [vals[n]]
            rows_each = r // len(pieces)
            for i, piece in enumerate(pieces):
                out = lax.dynamic_update_slice(out, _to_rows(piece, rows_each),
                                               (_PACK_OFF[n][0] + i * rows_each, 0))
    return out


def _unpack(packed, name, shape):
    o, r = _PACK_OFF[name]
    n = int(np.prod(shape))
    return packed[o:o + r].reshape(-1)[:n].reshape(shape)


def _small_params_packed(p):
    return {
        "b_ada": p["b_ada"], "norm1_g": p["norm1_g"], "b_in": p["b_in"],
        "norm2_g": p["norm2_g"], "final_g": p["final_g"],
        "lambda_re": p["lambda_re"], "lambda_im": p["lambda_im"],
        "log_step": p["log_step"], "attn_sinks": p["attn_sinks"],
        "rel_bias": p["rel_bias"], "b_glu": p["b_glu"], "ssm_d": p["ssm_d"],
        "ssm_b_re": _b_to_rows(p["ssm_b_re"][0]), "ssm_b_im": _b_to_rows(p["ssm_b_im"][0]),
        "ssm_c_re": _c_to_rows(p["ssm_c_re"][0]), "ssm_c_im": _c_to_rows(p["ssm_c_im"][0]),
    }


_SMALL_SHAPES = {
    "b_ada": (1, N_MOD * D), "norm1_g": (1, D), "b_in": (1, IN_W), "norm2_g": (1, D),
    "final_g": (D,), "lambda_re": (1, SSM_G, SSM_N), "lambda_im": (1, SSM_G, SSM_N),
    "log_step": (1, SSM_G), "attn_sinks": (1, N_Q_HEADS), "rel_bias": (NUM_BUCKETS, N_Q_HEADS),
    "b_glu": (1, SSM_W), "ssm_d": (1, SSM_W),
}


def _unpack_small(packed, name):
    if name in ("ssm_b_re", "ssm_b_im"):
        o, r = _PACK_OFF[name]
        return _rows_to_b(packed[o:o + r])[None]
    if name in ("ssm_c_re", "ssm_c_im"):
        o, r = _PACK_OFF[name]
        return _rows_to_c(packed[o:o + r])[None]
    return _unpack(packed, name, _SMALL_SHAPES[name])


WEIGHT_ORDER = ['w_ada', 'b_ada', 'norm1_g', 'w_in', 'b_in', 'attn_sinks', 'rel_bias', 'lambda_re',
                'lambda_im', 'log_step', 'ssm_b_re', 'ssm_b_im', 'ssm_c_re', 'ssm_c_im', 'ssm_d',
                'w_glu', 'b_glu', 'w_attn_proj', 'w_ssm_proj', 'w_out', 'norm2_g', 'w_ff1', 'w_ff2',
                'final_g']
BIG = ['w_in', 'w_glu', 'w_attn_proj', 'w_ssm_proj', 'w_out', 'w_ff1', 'w_ff2']


ADAMW_TILE_ELEMS = 1 << 18


def _adamw_rows(rows, cols):
    tr = rows
    while tr * cols > ADAMW_TILE_ELEMS and tr % 32 == 0:
        tr //= 2
    return tr


def _cast_bf16(w, name):
    rows, cols = w.shape
    tr = min(rows, 256)
    return _rowwise(lambda rv, vv, i, nt: ([rv[0]], []), [_row(w)], [], [(cols, BF16)], [],
                    n_rows=rows, tr=tr, ch=min(tr, 32), name=name)[0]


def kernel(x, c, w_ada, b_ada, norm1_g, w_in, b_in, attn_sinks, rel_bias, lambda_re, lambda_im, log_step, ssm_b_re, ssm_b_im, ssm_c_re, ssm_c_im, ssm_d, w_glu, b_glu, w_attn_proj, w_ssm_proj, w_out, norm2_g, w_ff1, w_ff2, final_g, loss_target, m_w_ada, m_b_ada, m_norm1_g, m_w_in, m_b_in, m_attn_sinks, m_rel_bias, m_lambda_re, m_lambda_im, m_log_step, m_ssm_b_re, m_ssm_b_im, m_ssm_c_re, m_ssm_c_im, m_ssm_d, m_w_glu, m_b_glu, m_w_attn_proj, m_w_ssm_proj, m_w_out, m_norm2_g, m_w_ff1, m_w_ff2, m_final_g, v_w_ada, v_b_ada, v_norm1_g, v_w_in, v_b_in, v_attn_sinks, v_rel_bias, v_lambda_re, v_lambda_im, v_log_step, v_ssm_b_re, v_ssm_b_im, v_ssm_c_re, v_ssm_c_im, v_ssm_d, v_w_glu, v_b_glu, v_w_attn_proj, v_w_ssm_proj, v_w_out, v_norm2_g, v_w_ff1, v_w_ff2, v_final_g):
    loc = dict(locals())
    W = {n: loc[n] for n in WEIGHT_ORDER}
    Mo = {n: loc["m_" + n] for n in WEIGHT_ORDER}
    Vo = {n: loc["v_" + n] for n in WEIGHT_ORDER}
    S = x.shape[1]
    TM = min(512, S)
    TS = min(1024, S)
    TR = min(256, S)
    me = 4 * lax.axis_index("x") + 2 * lax.axis_index("y") + lax.axis_index("c")
    x2d = x.reshape(S, D)
    tgt = loss_target.reshape(S, D)

    shard = {n: W[n][0] for n in BIG}
    gathered = _allgather_weights([_cast_bf16(shard[n], "cast_" + n) for n in BIG], "allgather_weights")
    G = dict(zip(BIG, gathered))
    w_glu_f = G["w_glu"].reshape(SSM_W, SSM_W)
    w_out_f = G["w_out"].reshape(D, D)
    w_ff2_f = G["w_ff2"].reshape(D_FF, D)

    c_all = _small_allgather(c, "allgather_c").reshape(N_DEV, D)
    cs = _rowwise(lambda rv, vv, i, nt: ([rv[0] * _sigmoid(rv[0])], []), [_row(c_all)], [],
                  [(D, F32)], [], n_rows=N_DEV, tr=8, ch=8, name="silu_c")[0]
    n_ada = N_MOD * D // N_DEV
    b_ada_cols = lax.dynamic_slice(b_ada, (0, me * n_ada), (1, n_ada))
    mod_piece = _matmul(cs, w_ada[0], mode="nn", dims=(N_DEV, n_ada, D), tiles=(N_DEV, 512, D),
                        out_dtypes=[F32], name="ada_fwd", bias=b_ada_cols)
    mod_all = _small_allgather(mod_piece, "allgather_mod")
    mod_b = lax.dynamic_index_in_dim(mod_all, me, axis=1, keepdims=False).reshape(N_MOD, D)
    sh1, sc1, g1, sh2, sc2, g2 = [mod_b[i:i + 1] for i in range(N_MOD)]

    def f_norm1(rv, vv, i, nt):
        xv, (g, sc, sh) = rv[0], vv
        return [(xv * _rms(xv) * g) * (1.0 + sc) + sh], []

    h = _rowwise(f_norm1, [_row(x2d)], [norm1_g, sc1, sh1], [(D, BF16)], [],
                 n_rows=S, tr=TR, ch=32, name="norm1_fwd")[0]
    proj = _matmul(h, G["w_in"], mode="nn", dims=(S, IN_W, D), tiles=(TM, 768, D),
                   out_dtypes=[F32], name="in_proj", b3=True, bias=b_in)

    buckets = _t5_buckets_block()
    band = _band_mask()
    onehot_t = jnp.asarray(
        (np.arange(128)[:, None] == buckets.reshape(-1)[None, :]).astype(np.float32), BF16)
    band_first = band & (np.arange(2 * BLK)[None, :] >= BLK)
    rel_bias_t = jnp.pad(jnp.transpose(rel_bias), ((0, 0), (0, 128 - NUM_BUCKETS)))
    bias2 = _bias_tables(rel_bias_t, onehot_t,
                         jnp.asarray(band_first.reshape(1, -1).astype(np.float32)),
                         jnp.asarray(band.reshape(1, -1).astype(np.float32))
                         ).reshape(2, N_Q_HEADS * BLK, 2 * BLK)
    sinkcol = jnp.repeat(attn_sinks.reshape(N_Q_HEADS), BLK).reshape(N_Q_HEADS * BLK, 1)
    attn = _attention_fwd(proj, bias2, sinkcol, S)
    y_attn = _matmul(attn, G["w_attn_proj"], mode="nn", dims=(S, D, ATTN_W), tiles=(TM, 256, ATTN_W),
                     out_dtypes=[F32], name="attn_proj", b3=True)

    lam_re = lambda_re.reshape(1, SSM_H)
    lam_im = lambda_im.reshape(1, SSM_H)
    ls_x = jnp.repeat(log_step.reshape(SSM_G), SSM_N).reshape(1, SSM_H)
    btr, bti = _b_to_rows(ssm_b_re[0]), _b_to_rows(ssm_b_im[0])
    ctr, cti = _c_to_rows(ssm_c_re[0]), _c_to_rows(ssm_c_im[0])
    abar, bcat, ccat = _ssm_setup(lam_re, lam_im, ls_x, btr, bti, ctr, cti)
    u_blk = (ATTN_W + 2 * KV_W) // SSM_W
    bu = _matmul(proj, bcat, mode="nn", dims=(S, 2 * SSM_H, SSM_W), tiles=(TM, 2048, SSM_W),
                 out_dtypes=[F32], name="ssm_bu", a_off=u_blk)
    xs, xprev = _scan_fwd(bu, abar, S)
    yc = _matmul(xs, ccat, mode="nt", dims=(S, SSM_W, 2 * SSM_H), tiles=(TM, SSM_W, 2 * SSM_H),
                 out_dtypes=[F32], name="ssm_cx")

    def f_ssm_out(rv, vv, i, nt):
        y = rv[0] + vv[0] * rv[1]
        return [y, _gelu(y)], []

    y_ssm_pre, z = _rowwise(f_ssm_out, [_row(yc), _row(proj, u_blk, SSM_W)], [ssm_d],
                            [(SSM_W, F32), (SSM_W, BF16)], [], n_rows=S, tr=TM, ch=32, name="ssm_out")
    zg = _matmul(z, w_glu_f, mode="nn", dims=(S, SSM_W, SSM_W), tiles=(TM, SSM_W, SSM_W),
                 out_dtypes=[F32], name="glu_proj", bias=b_glu)
    z2 = _rowwise(lambda rv, vv, i, nt: ([rv[0].astype(F32) * _sigmoid(rv[1])], []),
                  [_row(z), _row(zg)], [], [(SSM_W, BF16)], [], n_rows=S, tr=TM, ch=32, name="glu_gate")[0]
    y_ssm = _matmul(z2, G["w_ssm_proj"], mode="nn", dims=(S, D, SSM_W), tiles=(TM, 256, SSM_W),
                    out_dtypes=[F32], name="ssm_proj", b3=True)

    ga_row = _row(proj, 1, D)
    gs_row = _row(proj, 2, D)

    def f_merge(rv, vv, i, nt):
        ga, gs, ya, ys = rv
        return [_sigmoid(ga) * ya + _sigmoid(gs) * ys], []

    merged = _rowwise(f_merge, [ga_row, gs_row, _row(y_attn), _row(y_ssm)], [], [(D, BF16)], [],
                      n_rows=S, tr=TR, ch=32, name="merge")[0]
    mo = _matmul(merged, w_out_f, mode="nn", dims=(S, D, D), tiles=(TM, 1024, D),
                 out_dtypes=[F32], name="out_proj")

    def f_norm2(rv, vv, i, nt):
        xv, mv = rv
        g1v, g, sc, sh = vv
        x1v = xv + g1v * mv
        return [x1v, (x1v * _rms(x1v) * g) * (1.0 + sc) + sh], []

    x1, h2 = _rowwise(f_norm2, [_row(x2d), _row(mo)], [g1, norm2_g, sc2, sh2],
                      [(D, F32), (D, BF16)], [], n_rows=S, tr=TR, ch=32, name="norm2_fwd")

    def relu_sq(acc):
        r = jnp.maximum(acc, 0.0)
        return r * r, r

    act, relu = _matmul(h2, G["w_ff1"], mode="nn", dims=(S, D_FF, D), tiles=(TM, 1024, D),
                        out_dtypes=[BF16, BF16], name="ff1", b3=True, epilogue=relu_sq)
    ff = _matmul(act, w_ff2_f, mode="nn", dims=(S, D, D_FF), tiles=(TM, 1024, 2048),
                 out_dtypes=[F32], name="ff2")

    def f_loss(rv, vv, i, nt):
        x1v, ffv, tv = rv
        g2v, gf = vv
        x2v = x1v + g2v * ffv
        r = _rms(x2v)
        xh = x2v * r
        diff = xh * gf - tv
        dy = diff * (1.0 / D)
        dxh = dy * gf
        dx2 = r * (dxh - xh * jnp.mean(dxh * xh, axis=-1, keepdims=True))
        return [dx2, dx2 * g2v], [_colsum(0.5 * diff * diff * (1.0 / D)), _colsum(dy * xh),
                                  _colsum(dx2 * ffv)]

    dx2, dff, loss_cols, d_final_g, dg2 = _rowwise(
        f_loss, [_row(x1), _row(ff), _row(tgt)], [g2, final_g.reshape(1, D)],
        [(D, F32), (D, BF16)], [(1, D)] * 3, n_rows=S, tr=TR, ch=32, name="loss_bwd")

    df1 = _matmul(dff, w_ff2_f, mode="nt", dims=(S, D_FF, D), tiles=(TM, 1024, D),
                  out_dtypes=[BF16], name="ff2_dgrad", extras=(relu,),
                  epilogue=lambda acc, r: (acc * (2.0 * r.astype(F32)),))
    gw_ff2 = _matmul(act, dff, mode="tn", dims=(D_FF, D, S), tiles=(1024, 1024, TS),
                     out_dtypes=[BF16], name="ff2_wgrad").reshape(N_DEV, D_FF // N_DEV, D)
    dh2 = _matmul(df1, G["w_ff1"], mode="nt", dims=(S, D, D_FF), tiles=(TM, D, 1024),
                  out_dtypes=[F32], name="ff1_dgrad", b3=True)
    gw_ff1 = _matmul(h2, df1, mode="tn", dims=(D, D_FF, S), tiles=(1024, 1024, TS),
                     out_dtypes=[BF16], name="ff1_wgrad", out3=True)

    def f_norm2_bwd(rv, vv, i, nt):
        x1v, dh, dx2v, mv = rv
        g, sc, g1v = vv
        r = _rms(x1v)
        xh = x1v * r
        t = xh * g
        dt = dh * (1.0 + sc)
        dxh = dt * g
        dx1 = dx2v + r * (dxh - xh * jnp.mean(dxh * xh, axis=-1, keepdims=True))
        return [dx1, dx1 * g1v], [_colsum(dh), _colsum(dh * t), _colsum(dt * xh), _colsum(dx1 * mv)]

    dx1, dmo, dsh2, dsc2, d_norm2_g, dg1 = _rowwise(
        f_norm2_bwd, [_row(x1), _row(dh2), _row(dx2), _row(mo)], [norm2_g, sc2, g1],
        [(D, F32), (D, BF16)], [(1, D)] * 4, n_rows=S, tr=TR, ch=16, name="norm2_bwd")

    dmerged = _matmul(dmo, w_out_f, mode="nt", dims=(S, D, D), tiles=(TM, 1024, D),
                      out_dtypes=[F32], name="out_dgrad")
    gw_out = _matmul(merged, dmo, mode="tn", dims=(D, D, S), tiles=(1024, 1024, TS),
                     out_dtypes=[BF16], name="out_wgrad").reshape(N_DEV, D // N_DEV, D)

    def f_merge_bwd(rv, vv, i, nt):
        dm, ga, gs, ya, ys = rv
        sa, ss = _sigmoid(ga), _sigmoid(gs)
        return [dm * sa, dm * ss, dm * ya * sa * (1.0 - sa), dm * ys * ss * (1.0 - ss)], []

    dy_attn, dy_ssm, dga, dgs = _rowwise(
        f_merge_bwd, [_row(dmerged), ga_row, gs_row, _row(y_attn), _row(y_ssm)], [],
        [(D, BF16)] * 4, [], n_rows=S, tr=TR, ch=16, name="merge_bwd")

    dz2 = _matmul(dy_ssm, G["w_ssm_proj"], mode="nt", dims=(S, SSM_W, D), tiles=(TM, SSM_W, 256),
                  out_dtypes=[F32], name="ssm_proj_dgrad", b3=True)
    gw_ssm_proj = _matmul(z2, dy_ssm, mode="tn", dims=(SSM_W, D, S), tiles=(SSM_W, 256, TS),
                          out_dtypes=[BF16], name="ssm_proj_wgrad", out3=True)

    def f_glu_bwd(rv, vv, i, nt):
        dz2v, zv, zgv = rv
        sg = _sigmoid(zgv)
        dzg = dz2v * zv.astype(F32) * sg * (1.0 - sg)
        return [dzg, dz2v * sg], [_colsum(dzg)]

    dzg, dz_a, d_b_glu = _rowwise(f_glu_bwd, [_row(dz2), _row(z), _row(zg)], [],
                                  [(SSM_W, BF16), (SSM_W, F32)], [(1, SSM_W)],
                                  n_rows=S, tr=TM, ch=32, name="glu_bwd")
    dz_b = _matmul(dzg, w_glu_f, mode="nt", dims=(S, SSM_W, SSM_W), tiles=(TM, SSM_W, SSM_W),
                   out_dtypes=[F32], name="glu_dgrad")
    gw_glu = _matmul(z, dzg, mode="tn", dims=(SSM_W, SSM_W, S), tiles=(SSM_W, SSM_W, TS),
                     out_dtypes=[BF16], name="glu_wgrad").reshape(N_DEV, SSM_W // N_DEV, SSM_W)

    def f_ssm_out_bwd(rv, vv, i, nt):
        dza, dzb, yv, uv = rv
        dy = (dza + dzb) * _gelu_grad(yv)
        return [dy, dy * vv[0]], [_colsum(dy * uv)]

    dy_s, du_a, d_ssm_d = _rowwise(
        f_ssm_out_bwd, [_row(dz_a), _row(dz_b), _row(y_ssm_pre), _row(proj, u_blk, SSM_W)], [ssm_d],
        [(SSM_W, BF16), (SSM_W, F32)], [(1, SSM_W)], n_rows=S, tr=TM, ch=32, name="ssm_out_bwd")
    gx = _matmul(dy_s, ccat, mode="nn", dims=(S, 2 * SSM_H, SSM_W), tiles=(TM, 2048, SSM_W),
                 out_dtypes=[F32], name="ssm_cx_dgrad")
    dccat = _matmul(dy_s, xs, mode="tn", dims=(SSM_W, 2 * SSM_H, S), tiles=(SSM_W, 2048, TS),
                    out_dtypes=[F32], name="ssm_c_wgrad")
    hs, dacc = _scan_bwd(gx, xprev, abar, S)
    du_b = _matmul(hs, bcat, mode="nt", dims=(S, SSM_W, 2 * SSM_H), tiles=(TM, SSM_W, 2 * SSM_H),
                   out_dtypes=[F32], name="ssm_bu_dgrad")
    dbcat = _matmul(proj, hs, mode="tn", dims=(SSM_W, 2 * SSM_H, S), tiles=(SSM_W, 2048, TS),
                    out_dtypes=[F32], name="ssm_b_wgrad", a_off=u_blk)
    grp = np.arange(SSM_H) // SSM_N
    gind = jnp.asarray((grp[:, None] == np.arange(128)[None, :]).astype(np.float32), BF16)
    d_lam_re, d_lam_im, d_ls, d_btr, d_bti, d_ctr, d_cti = _ssm_param_bwd(
        lam_re, lam_im, ls_x, btr, bti, dacc, dbcat, dccat, gind)

    dattn = _matmul(dy_attn, G["w_attn_proj"], mode="nt", dims=(S, ATTN_W, D), tiles=(TM, ATTN_W, 256),
                    out_dtypes=[BF16], name="attn_proj_dgrad", b3=True)
    gw_attn_proj = _matmul(attn, dy_attn, mode="tn", dims=(ATTN_W, D, S), tiles=(ATTN_W, 256, TS),
                           out_dtypes=[BF16], name="attn_proj_wgrad", out3=True)
    dq, dkc, dkp, dvc, dvp, dbias, dsink = _attention_bwd(proj, attn, dattn, bias2, sinkcol, S)
    d_bias_b, d_sinks = _bucket_reduce(dbias.reshape(N_Q_HEADS, BLK * 2 * BLK),
                                       dsink.reshape(N_Q_HEADS, BLK), onehot_t)

    def f_dproj(rv, vv, i, nt):
        dqv, kc, kp, vc, vp, dua, dub, gav, gsv = rv
        keep = (i < nt - 1).astype(F32)
        dp = jnp.concatenate([dqv.astype(F32), kc + keep * kp, vc + keep * vp, dua + dub,
                              gav.astype(F32), gsv.astype(F32)], axis=-1)
        return [dp], [_colsum(dp)]

    dproj, d_b_in = _rowwise(
        f_dproj, [_row(dq), _row(dkc), _row(dkp, shift=1), _row(dvc), _row(dvp, shift=1),
                  _row(du_a), _row(du_b), _row(dga), _row(dgs)], [],
        [(IN_W, BF16)], [(1, IN_W)], n_rows=S, tr=BLK, ch=16, name="dproj")
    dh = _matmul(dproj, G["w_in"], mode="nt", dims=(S, D, IN_W), tiles=(TM, D, 768),
                 out_dtypes=[F32], name="in_dgrad", b3=True)
    gw_in = _matmul(h, dproj, mode="tn", dims=(D, IN_W, S), tiles=(1024, 768, TS),
                    out_dtypes=[BF16], name="in_wgrad", out3=True)

    def f_norm1_bwd(rv, vv, i, nt):
        xv, dhv, dx1v = rv
        g, sc = vv
        r = _rms(xv)
        xh = xv * r
        t = xh * g
        dt = dhv * (1.0 + sc)
        dxh = dt * g
        dxv = dx1v + r * (dxh - xh * jnp.mean(dxh * xh, axis=-1, keepdims=True))
        return [dxv], [_colsum(dhv), _colsum(dhv * t), _colsum(dt * xh)]

    grad_x, dsh1, dsc1, d_norm1_g = _rowwise(
        f_norm1_bwd, [_row(x2d), _row(dh), _row(dx1)], [norm1_g, sc1],
        [(D, F32)], [(1, D)] * 3, n_rows=S, tr=TR, ch=32, name="norm1_bwd")

    part = _pack({
        "b_ada": [dsh1, dsc1, dg1, dsh2, dsc2, dg2], "norm1_g": d_norm1_g, "b_in": d_b_in, "norm2_g": d_norm2_g,
        "final_g": d_final_g, "lambda_re": d_lam_re, "lambda_im": d_lam_im,
        "log_step": d_ls[0, :SSM_G], "attn_sinks": d_sinks[:, 0],
        "rel_bias": jnp.transpose(d_bias_b[:, :NUM_BUCKETS]), "b_glu": d_b_glu, "ssm_d": d_ssm_d,
        "loss": loss_cols, "ssm_b_re": d_btr, "ssm_b_im": d_bti, "ssm_c_re": d_ctr, "ssm_c_im": d_cti,
    })
    part_all = _small_allgather(part, "allgather_small_grads")
    wp, mp, vp = [_pack(_small_params_packed(p)) for p in (W, Mo, Vo)]
    sg, sdelta, sm, sv = _adamw([(part_all, d) for d in range(N_DEV)], wp, mp, vp,
                                tr=PACK_ROWS, ch=8, name="adamw_small")
    lo, _ = _PACK_OFF["loss"]
    loss = jnp.sum(sg[lo])

    o_ada, _ = _PACK_OFF["b_ada"]
    dmod_all = part_all[:, o_ada:o_ada + N_MOD, :].reshape(N_DEV, N_MOD * D)
    dmod_cols = lax.dynamic_slice(dmod_all, (0, me * n_ada), (N_DEV, n_ada))
    gw_ada = _matmul(cs, dmod_cols, mode="tn", dims=(D, n_ada, N_DEV), tiles=(D, 512, N_DEV),
                     out_dtypes=[F32], name="ada_wgrad")

    gbig = {"w_in": gw_in, "w_glu": gw_glu, "w_attn_proj": gw_attn_proj, "w_ssm_proj": gw_ssm_proj,
            "w_out": gw_out, "w_ff1": gw_ff1, "w_ff2": gw_ff2}
    recv = dict(zip(BIG, _exchange_grads([gbig[n] for n in BIG], "exchange_grads")))
    big_out = {}
    for n in BIG:
        rows, cols = shard[n].shape
        big_out[n] = _adamw([(recv[n], d) for d in range(N_DEV)], shard[n], Mo[n][0], Vo[n][0],
                            tr=_adamw_rows(rows, cols), ch=16, name="adamw_" + n)
    big_out["w_ada"] = _adamw([(gw_ada, 0)], w_ada[0], m_w_ada[0], v_w_ada[0],
                              tr=_adamw_rows(D, n_ada), ch=16, name="adamw_w_ada")

    def leaf(kind, n):
        if n in big_out:
            return big_out[n][kind][None]
        return _unpack_small((sg, sdelta, sm, sv)[kind], n)

    outs = [loss, grad_x.reshape(1, S, D)]
    for kind in range(4):
        outs.extend(leaf(kind, n) for n in WEIGHT_ORDER)
    return tuple(outs)
```

```python
import functools
import math

import numpy as np
import jax
import jax.numpy as jnp
from jax import lax
from jax.experimental import pallas as pl
from jax.experimental.pallas import tpu as pltpu

F32 = jnp.float32
BF16 = jnp.bfloat16
MESH = pl.DeviceIdType.MESH

N_DEV = 8
D = 2048
HEAD_DIM = 64
N_Q_HEADS = 16
N_KV_HEADS = 4
GROUP = N_Q_HEADS // N_KV_HEADS
ATTN_W = N_Q_HEADS * HEAD_DIM
KV_W = N_KV_HEADS * HEAD_DIM
BLK = 128
NUM_BUCKETS = 32
MAX_DISTANCE = 128
NEG_INF = -1e30
SSM_W = 512
SSM_P = 16
SSM_G = 32
SSM_N = 64
SSM_H = SSM_G * SSM_N
D_FF = 4 * D
IN_W = ATTN_W + 2 * KV_W + SSM_W + 2 * D
N_MOD = 6
EPS = 1e-6

ADAM_LR = 0.001
ADAM_B1 = 0.9
ADAM_B2 = 0.999
ADAM_EPS = 1e-08
ADAM_WD = 0.01
ADAM_STEP = 10

VMEM_LIMIT = 56 * 1024 * 1024
PACK_W = 2048


def _cparams(sem):
    return pltpu.CompilerParams(dimension_semantics=sem, vmem_limit_bytes=VMEM_LIMIT)


def _matmul(a, b, *, mode, dims, tiles, out_dtypes, name, a_off=0, b3=False,
            out3=False, bias=None, extras=(), epilogue=None, dep=None):
    M, N, K = dims
    tm, tn, tk = tiles
    assert M % tm == 0 and N % tn == 0 and K % tk == 0, (name, dims, tiles)
    gm, gn, gk = M // tm, N // tn, K // tk
    n_extra = len(extras)
    has_bias = bias is not None
    n_out = len(out_dtypes)

    if mode == "nn":
        a_spec = pl.BlockSpec((tm, tk), lambda i, j, k: (i, a_off + k))
        if b3:
            nb = (N // N_DEV) // tn
            assert nb * tn * N_DEV == N
            b_spec = pl.BlockSpec((None, tk, tn), lambda i, j, k: (j // nb, k, j % nb))
        else:
            b_spec = pl.BlockSpec((tk, tn), lambda i, j, k: (k, j))
        dn = (((1,), (0,)), ((), ()))
    elif mode == "nt":
        a_spec = pl.BlockSpec((tm, tk), lambda i, j, k: (i, a_off + k))
        if b3:
            nb = (K // N_DEV) // tk
            assert nb * tk * N_DEV == K
            b_spec = pl.BlockSpec((None, tn, tk), lambda i, j, k: (k // nb, j, k % nb))
        else:
            b_spec = pl.BlockSpec((tn, tk), lambda i, j, k: (j, k))
        dn = (((1,), (1,)), ((), ()))
    else:
        a_spec = pl.BlockSpec((tk, tm), lambda i, j, k: (k, a_off + i))
        b_spec = pl.BlockSpec((tk, tn), lambda i, j, k: (k, j))
        dn = (((0,), (0,)), ((), ()))

    if out3:
        nbo = (N // N_DEV) // tn
        assert nbo * tn * N_DEV == N
        o_spec = pl.BlockSpec((None, tm, tn), lambda i, j, k: (j // nbo, i, j % nbo))
        o_shape = (N_DEV, M, N // N_DEV)
    else:
        o_spec = pl.BlockSpec((tm, tn), lambda i, j, k: (i, j))
        o_shape = (M, N)

    in_specs = [a_spec, b_spec]
    args = [a, b]
    if has_bias:
        in_specs.append(pl.BlockSpec((1, tn), lambda i, j, k: (0, j)))
        args.append(bias)
    for e in extras:
        in_specs.append(pl.BlockSpec((tm, tn), lambda i, j, k: (i, j)))
        args.append(e)
    n_dep = 0 if dep is None else 1
    if n_dep:
        in_specs.append(pl.BlockSpec(memory_space=pl.ANY))
        args.append(dep)

    def body(*refs):
        a_ref, b_ref = refs[0], refs[1]
        pos = 2
        bias_ref = None
        if has_bias:
            bias_ref = refs[pos]
            pos += 1
        extra_refs = refs[pos:pos + n_extra]
        pos += n_extra + n_dep
        out_refs = refs[pos:pos + n_out]
        acc_ref = refs[pos + n_out] if gk > 1 else None

        part = lax.dot_general(a_ref[...].astype(BF16), b_ref[...].astype(BF16), dn,
                               preferred_element_type=F32)

        def finish(acc):
            if has_bias:
                acc = acc + bias_ref[...]
            if epilogue is None:
                vals = (acc,)
            else:
                vals = epilogue(acc, *[e[...] for e in extra_refs])
            for o_ref, val in zip(out_refs, vals):
                o_ref[...] = val.astype(o_ref.dtype)

        if gk == 1:
            finish(part)
        else:
            k = pl.program_id(2)

            @pl.when(k == 0)
            def _():
                acc_ref[...] = part

            @pl.when(k > 0)
            def _():
                acc_ref[...] += part

            @pl.when(k == gk - 1)
            def _():
                finish(acc_ref[...])

    outs = pl.pallas_call(
        body,
        grid=(gm, gn, gk),
        in_specs=in_specs,
        out_specs=[o_spec] * n_out,
        out_shape=[jax.ShapeDtypeStruct(o_shape, dt) for dt in out_dtypes],
        scratch_shapes=([pltpu.VMEM((tm, tn), F32)] if gk > 1 else []),
        compiler_params=_cparams(("parallel", "parallel", "arbitrary")),
        name=name,
    )(*args)
    return outs[0] if n_out == 1 else outs


def _rowwise(fn, rows, vecs, row_outs, sum_outs, *, n_rows, tr, ch, name, dep=None):
    assert n_rows % tr == 0 and tr % ch == 0
    nt = n_rows // tr
    nr, nv, nro, nso = len(rows), len(vecs), len(row_outs), len(sum_outs)
    in_specs, args = [], []
    for (arr, lead, cblk, w, shift) in rows:
        if shift:
            ridx = lambda i, shift=shift: jnp.minimum(i + shift, nt - 1)
        else:
            ridx = lambda i: i
        if arr.ndim == 3:
            in_specs.append(pl.BlockSpec(
                (None, tr, w), lambda i, lead=lead, cblk=cblk, ridx=ridx: (lead, ridx(i), cblk)))
        else:
            in_specs.append(pl.BlockSpec(
                (tr, w), lambda i, cblk=cblk, ridx=ridx: (ridx(i), cblk)))
        args.append(arr)
    for v in vecs:
        in_specs.append(pl.BlockSpec(v.shape, lambda i, nd=v.ndim: (0,) * nd))
        args.append(v)
    n_dep = 0 if dep is None else 1
    if n_dep:
        in_specs.append(pl.BlockSpec(memory_space=pl.ANY))
        args.append(dep)
    out_specs = [pl.BlockSpec((tr, w), lambda i: (i, 0)) for (w, _) in row_outs]
    out_shape = [jax.ShapeDtypeStruct((n_rows, w), dt) for (w, dt) in row_outs]
    for (r, w) in sum_outs:
        out_specs.append(pl.BlockSpec((r, w), lambda i: (0, 0)))
        out_shape.append(jax.ShapeDtypeStruct((r, w), F32))

    def body(*refs):
        i = pl.program_id(0)
        r_in = refs[:nr]
        v_in = refs[nr:nr + nv]
        r_out = refs[nr + nv + n_dep:nr + nv + n_dep + nro]
        s_out = refs[nr + nv + n_dep + nro:]
        if nso:
            @pl.when(i == 0)
            def _():
                for s in s_out:
                    s[...] = jnp.zeros(s.shape, F32)
        vvals = [v[...] for v in v_in]

        def chunk(ci, carry):
            r0 = pl.multiple_of(ci * ch, ch)
            rv = [r[pl.ds(r0, ch), :] for r in r_in]
            ro, so = fn(rv, vvals, i, nt)
            for ref, val in zip(r_out, ro):
                ref[pl.ds(r0, ch), :] = val.astype(ref.dtype)
            for ref, val in zip(s_out, so):
                ref[...] += val
            return carry

        lax.fori_loop(0, tr // ch, chunk, 0)

    outs = pl.pallas_call(
        body,
        grid=(nt,),
        in_specs=in_specs,
        out_specs=out_specs,
        out_shape=out_shape,
        compiler_params=_cparams(("arbitrary",)),
        name=name,
    )(*args)
    return outs


def _row(arr, cblk=0, w=None, lead=0, shift=0):
    return (arr, lead, cblk, arr.shape[-1] if w is None else w, shift)


def _colsum(v):
    return jnp.sum(v, axis=0, keepdims=True)


def _rms(x):
    return lax.rsqrt(jnp.mean(x * x, axis=-1, keepdims=True) + EPS)


def _sigmoid(x):
    return 1.0 / (1.0 + jnp.exp(-x))


_GELU_C = math.sqrt(2.0 / math.pi)


def _gelu(x):
    return 0.5 * x * (1.0 + jnp.tanh(_GELU_C * (x + 0.044715 * (x * x * x))))


def _gelu_grad(x):
    t = jnp.tanh(_GELU_C * (x + 0.044715 * (x * x * x)))
    return 0.5 * (1.0 + t) + 0.5 * x * (1.0 - t * t) * (_GELU_C * (1.0 + 3.0 * 0.044715 * (x * x)))


def _my_pos():
    return lax.axis_index("x"), lax.axis_index("y"), lax.axis_index("c")


def _flip(pos, k):
    x, y, c = pos
    return (1 - x if k & 4 else x, 1 - y if k & 2 else y, 1 - c if k & 1 else c)


def _dev_id(pos):
    return 4 * pos[0] + 2 * pos[1] + pos[2]


def _small_allgather(x, name):
    r, c = x.shape

    def body(x_ref, out_ref, send_sems, recv_sems):
        me = _my_pos()
        out_ref[_dev_id(me)] = x_ref[...]
        copies = []
        for k in range(1, N_DEV):
            cp = pltpu.make_async_remote_copy(
                src_ref=x_ref, dst_ref=out_ref.at[_dev_id(me)],
                send_sem=send_sems.at[k - 1], recv_sem=recv_sems.at[k - 1],
                device_id=_flip(me, k), device_id_type=MESH)
            cp.start()
            copies.append(cp)
        for k in range(1, N_DEV):
            peer = _flip(me, k)
            pltpu.make_async_remote_copy(
                src_ref=x_ref, dst_ref=out_ref.at[_dev_id(peer)],
                send_sem=send_sems.at[k - 1], recv_sem=recv_sems.at[k - 1],
                device_id=peer, device_id_type=MESH).wait_recv()
        for cp in copies:
            cp.wait_send()

    return pl.pallas_call(
        body,
        out_shape=jax.ShapeDtypeStruct((N_DEV, r, c), x.dtype),
        in_specs=[pl.BlockSpec(memory_space=pltpu.VMEM)],
        out_specs=pl.BlockSpec(memory_space=pltpu.VMEM),
        scratch_shapes=[pltpu.SemaphoreType.DMA((N_DEV - 1,)),
                        pltpu.SemaphoreType.DMA((N_DEV - 1,))],
        compiler_params=pltpu.CompilerParams(vmem_limit_bytes=VMEM_LIMIT),
        name=name,
    )(x)


def _allgather_weights(shards, name):
    n = len(shards)

    def body(*refs):
        xs = refs[:n]
        outs = refs[n:2 * n]
        send_sems, recv_sems, local_sems = refs[2 * n:]
        x, y, c = _my_pos()
        me, sib = (x, y, c), (x, y, 1 - c)
        chips = [(1 - x, y), (x, 1 - y), (1 - x, 1 - y)]

        def copy(a, k, block, to, src=None):
            slot = outs[a].at[_dev_id(block)]
            return pltpu.make_async_remote_copy(
                src_ref=slot if src is None else src, dst_ref=slot,
                send_sem=send_sems.at[a, k], recv_sem=recv_sems.at[a, k],
                device_id=to, device_id_type=MESH)

        mine = [pltpu.make_async_copy(xs[a], outs[a].at[_dev_id(me)], local_sems.at[a])
                for a in range(n)]
        for cp in mine:
            cp.start()
        first = []
        for a in range(n):
            first.append(copy(a, 0, me, sib, src=xs[a]))
            for j, chip in enumerate(chips):
                first.append(copy(a, 1 + j, me, (*chip, c), src=xs[a]))
        for cp in first:
            cp.start()
        passed = []
        for a in range(n):
            for j, chip in enumerate(chips):
                copy(a, 1 + j, (*chip, c), me).wait_recv()
                cp = copy(a, 4 + j, (*chip, c), sib)
                cp.start()
                passed.append(cp)
        for a in range(n):
            copy(a, 0, sib, me).wait_recv()
            for j, chip in enumerate(chips):
                copy(a, 4 + j, (*chip, 1 - c), me).wait_recv()
        for cp in first + passed:
            cp.wait_send()
        for cp in mine:
            cp.wait()

    hbm = pl.BlockSpec(memory_space=pl.ANY)
    return pl.pallas_call(
        body,
        out_shape=[jax.ShapeDtypeStruct((N_DEV,) + s.shape, s.dtype) for s in shards],
        in_specs=[hbm] * n,
        out_specs=[hbm] * n,
        scratch_shapes=[pltpu.SemaphoreType.DMA((n, 7)),
                        pltpu.SemaphoreType.DMA((n, 7)),
                        pltpu.SemaphoreType.DMA((n,))],
        name=name,
    )(*shards)


_HBM = pl.BlockSpec(memory_space=pltpu.HBM)
_SEM = pl.BlockSpec(memory_space=pltpu.SEMAPHORE)
_EFFECT = pltpu.SideEffectType.DATAFLOW_SIDE_EFFECTING


def _exchange_copy(kind, src_ref, land_ref, send_sems, recv_sems, me, k, arriving):
    peer = _flip(me, k)
    my_id, peer_id = _dev_id(me), _dev_id(peer)
    if kind == "gather":
        src = src_ref
    else:
        src = src_ref.at[my_id if arriving else peer_id]
    return pltpu.make_async_remote_copy(
        src_ref=src, dst_ref=land_ref.at[peer_id if arriving else my_id],
        send_sem=send_sems.at[k - 1], recv_sem=recv_sems.at[k - 1],
        device_id=peer, device_id_type=MESH)


def _exchange_start(srcs, kind, name, after=None):
    n = len(srcs)
    n_after = 0 if after is None else 1
    shapes = [((N_DEV,) + s.shape) if kind == "gather" else s.shape for s in srcs]
    lands = [lax.empty(sh, s.dtype) for sh, s in zip(shapes, srcs)]

    def body(*refs):
        src_refs = refs[:n]
        land_refs = refs[n:2 * n]
        outs_at = 2 * n + n_after
        send = refs[outs_at:outs_at + n]
        recv = refs[outs_at + n:outs_at + 2 * n]
        token = refs[outs_at + 4 * n]
        local_sems = refs[outs_at + 4 * n + 1]
        me = _my_pos()
        my_id = _dev_id(me)
        mine = []
        for a in range(n):
            own = src_refs[a] if kind == "gather" else src_refs[a].at[my_id]
            cp = pltpu.make_async_copy(own, land_refs[a].at[my_id], local_sems.at[a])
            cp.start()
            mine.append(cp)
        for a in range(n):
            for k in range(1, N_DEV):
                _exchange_copy(kind, src_refs[a], land_refs[a], send[a], recv[a], me, k, False).start()
        for cp in mine:
            cp.wait()
        token[...] = jnp.zeros(token.shape, token.dtype)

    sem = pltpu.SemaphoreType.DMA((N_DEV - 1,))
    outs = pl.pallas_call(
        body,
        name=name,
        out_shape=([sem] * (2 * n)
                   + [pltpu.HBM(s.shape, s.dtype) for s in srcs]
                   + [pltpu.HBM(sh, s.dtype) for sh, s in zip(shapes, srcs)]
                   + [jax.ShapeDtypeStruct((8, 128), F32)]),
        in_specs=[_HBM] * (2 * n) + [pl.BlockSpec(memory_space=pl.ANY)] * n_after,
        out_specs=[_SEM] * (2 * n) + [_HBM] * (2 * n) + [pl.BlockSpec(memory_space=pltpu.VMEM)],
        input_output_aliases={i: 2 * n + i for i in range(2 * n)},
        scratch_shapes=[pltpu.SemaphoreType.DMA((n,))],
        compiler_params=pltpu.CompilerParams(has_side_effects=_EFFECT),
    )(*[pltpu.with_memory_space_constraint(s, pltpu.HBM) for s in srcs],
      *[pltpu.with_memory_space_constraint(l, pltpu.HBM) for l in lands],
      *([after] if n_after else []))
    flights = [(outs[a], outs[n + a], outs[2 * n + a], outs[3 * n + a]) for a in range(n)]
    return flights, outs[4 * n]


def _exchange_wait(flights, kind, after, name):
    n = len(flights)

    def body(*refs):
        src_refs = refs[:n]
        land_refs = refs[n:2 * n]
        send = refs[2 * n:3 * n]
        recv = refs[3 * n:4 * n]
        me = _my_pos()
        for a in range(n):
            for k in range(1, N_DEV):
                _exchange_copy(kind, src_refs[a], land_refs[a], send[a], recv[a], me, k, False).wait_send()
                _exchange_copy(kind, src_refs[a], land_refs[a], send[a], recv[a], me, k, True).wait_recv()

    srcs = [f[2] for f in flights]
    lands = [f[3] for f in flights]
    outs = pl.pallas_call(
        body,
        name=name,
        out_shape=[pltpu.HBM(s.shape, s.dtype) for s in srcs]
        + [pltpu.HBM(l.shape, l.dtype) for l in lands],
        in_specs=[_HBM] * (2 * n) + [_SEM] * (2 * n) + [pl.BlockSpec(memory_space=pl.ANY)],
        out_specs=[_HBM] * (2 * n),
        input_output_aliases={i: i for i in range(2 * n)},
        compiler_params=pltpu.CompilerParams(has_side_effects=_EFFECT),
    )(*srcs, *lands, *[f[0] for f in flights], *[f[1] for f in flights], after)
    return outs[n:]


def _t5_buckets_block():
    qi = np.arange(BLK)[:, None]
    ki = np.arange(2 * BLK)[None, :]
    n = np.maximum(qi + BLK - ki, 0)
    max_exact = NUM_BUCKETS // 2
    large = max_exact + (np.log(np.maximum(n, 1) / max_exact)
                         / np.log(MAX_DISTANCE / max_exact)
                         * (NUM_BUCKETS - max_exact)).astype(np.int32)
    large = np.minimum(large, NUM_BUCKETS - 1)
    return np.where(n < max_exact, n, large).astype(np.int32)


def _band_mask():
    qi = np.arange(BLK)[:, None]
    ki = np.arange(2 * BLK)[None, :]
    dist = qi + BLK - ki
    return (dist >= 0) & (dist < BLK)


def _attn_scores(q_ref, kp_ref, kc_ref, hkv):
    c0 = hkv * HEAD_DIM
    kk = jnp.concatenate([kp_ref[:, c0:c0 + HEAD_DIM], kc_ref[:, c0:c0 + HEAD_DIM]],
                         axis=0).astype(BF16)
    qg = jnp.concatenate(
        [q_ref[:, (hkv * GROUP + g) * HEAD_DIM:(hkv * GROUP + g + 1) * HEAD_DIM]
         for g in range(GROUP)], axis=0).astype(BF16)
    s = lax.dot_general(qg, kk, (((1,), (1,)), ((), ())), preferred_element_type=F32)
    return qg, kk, s


def _attn_softmax(s, bias_ref, sink_ref, hkv):
    r0, r1 = hkv * GROUP * BLK, (hkv + 1) * GROUP * BLK
    s = s * (HEAD_DIM ** -0.5) + bias_ref[r0:r1, :]
    sink = sink_ref[r0:r1, :]
    m = jnp.maximum(jnp.max(s, axis=-1, keepdims=True), sink)
    p = jnp.exp(s - m)
    e_sink = jnp.exp(sink - m)
    inv = 1.0 / (jnp.sum(p, axis=-1, keepdims=True) + e_sink)
    return p * inv, e_sink * inv


def _kv_rows(p_ref, c_ref, hkv):
    c0 = hkv * HEAD_DIM
    return jnp.concatenate([p_ref[:, c0:c0 + HEAD_DIM], c_ref[:, c0:c0 + HEAD_DIM]],
                           axis=0).astype(BF16)


def _attn_in_specs(bias2):
    prev = lambda n: jnp.maximum(n - 1, 0)
    return [
        pl.BlockSpec((BLK, ATTN_W), lambda n: (n, 0)),
        pl.BlockSpec((BLK, KV_W), lambda n: (prev(n), ATTN_W // KV_W)),
        pl.BlockSpec((BLK, KV_W), lambda n: (n, ATTN_W // KV_W)),
        pl.BlockSpec((BLK, KV_W), lambda n: (prev(n), ATTN_W // KV_W + 1)),
        pl.BlockSpec((BLK, KV_W), lambda n: (n, ATTN_W // KV_W + 1)),
        pl.BlockSpec((None,) + bias2.shape[1:], lambda n: (jnp.minimum(n, 1), 0, 0)),
    ]


def _attention_fwd(proj, bias2, sinkcol, n_rows):
    nb = n_rows // BLK

    def body(q_ref, kp_ref, kc_ref, vp_ref, vc_ref, bias_ref, sink_ref, o_ref):
        heads = range(N_KV_HEADS)
        scores = [_attn_scores(q_ref, kp_ref, kc_ref, hkv)[2] for hkv in heads]
        probs = [_attn_softmax(scores[hkv], bias_ref, sink_ref, hkv)[0] for hkv in heads]
        outs = [jnp.dot(probs[hkv].astype(BF16), _kv_rows(vp_ref, vc_ref, hkv),
                        preferred_element_type=F32) for hkv in heads]
        for hkv in heads:
            for g in range(GROUP):
                h = hkv * GROUP + g
                o_ref[:, h * HEAD_DIM:(h + 1) * HEAD_DIM] = (
                    outs[hkv][g * BLK:(g + 1) * BLK, :].astype(o_ref.dtype))

    return pl.pallas_call(
        body,
        grid=(nb,),
        in_specs=_attn_in_specs(bias2) + [pl.BlockSpec(sinkcol.shape, lambda n: (0, 0))],
        out_specs=pl.BlockSpec((BLK, ATTN_W), lambda n: (n, 0)),
        out_shape=jax.ShapeDtypeStruct((n_rows, ATTN_W), BF16),
        compiler_params=_cparams(("parallel",)),
        name="attn_fwd",
    )(proj, proj, proj, proj, proj, bias2, sinkcol)


def _attention_bwd(proj, attn, dattn, bias2, sinkcol, n_rows):
    nb = n_rows // BLK
    scale = HEAD_DIM ** -0.5
    dn_t = (((0,), (0,)), ((), ()))

    def body(q_ref, kp_ref, kc_ref, vp_ref, vc_ref, bias_ref, o_ref, do_ref, sink_ref,
             dq_ref, dkc_ref, dkp_ref, dvc_ref, dvp_ref, dbias_ref, dsink_ref):
        @pl.when(pl.program_id(0) == 0)
        def _():
            dbias_ref[...] = jnp.zeros(dbias_ref.shape, F32)
            dsink_ref[...] = jnp.zeros(dsink_ref.shape, F32)

        heads = range(N_KV_HEADS)
        qk = [_attn_scores(q_ref, kp_ref, kc_ref, hkv) for hkv in heads]
        dog, dps, deltas = [], [], []
        for hkv in heads:
            hs = [hkv * GROUP + g for g in range(GROUP)]
            d_o = jnp.concatenate([do_ref[:, h * HEAD_DIM:(h + 1) * HEAD_DIM] for h in hs], axis=0)
            o = jnp.concatenate([o_ref[:, h * HEAD_DIM:(h + 1) * HEAD_DIM] for h in hs], axis=0)
            deltas.append(jnp.sum(d_o.astype(F32) * o.astype(F32), axis=-1, keepdims=True))
            dog.append(d_o.astype(BF16))
            dps.append(lax.dot_general(dog[hkv], _kv_rows(vp_ref, vc_ref, hkv),
                                       (((1,), (1,)), ((), ())), preferred_element_type=F32))
        p16, ds16 = [], []
        for hkv in heads:
            r0, r1 = hkv * GROUP * BLK, (hkv + 1) * GROUP * BLK
            p, p_sink = _attn_softmax(qk[hkv][2], bias_ref, sink_ref, hkv)
            ds = p * (dps[hkv] - deltas[hkv])
            dbias_ref[r0:r1, :] += ds
            dsink_ref[r0:r1, :] += -(p_sink * deltas[hkv])
            p16.append(p.astype(BF16))
            ds16.append(ds.astype(BF16))
        for hkv in heads:
            c0 = hkv * HEAD_DIM
            qg, kk, _ = qk[hkv]
            dqg = jnp.dot(ds16[hkv], kk, preferred_element_type=F32) * scale
            dkk = lax.dot_general(ds16[hkv], qg, dn_t, preferred_element_type=F32) * scale
            dvv = lax.dot_general(p16[hkv], dog[hkv], dn_t, preferred_element_type=F32)
            for g in range(GROUP):
                h = hkv * GROUP + g
                dq_ref[:, h * HEAD_DIM:(h + 1) * HEAD_DIM] = (
                    dqg[g * BLK:(g + 1) * BLK, :].astype(dq_ref.dtype))
            dkp_ref[:, c0:c0 + HEAD_DIM] = dkk[:BLK].astype(dkp_ref.dtype)
            dkc_ref[:, c0:c0 + HEAD_DIM] = dkk[BLK:].astype(dkc_ref.dtype)
            dvp_ref[:, c0:c0 + HEAD_DIM] = dvv[:BLK].astype(dvp_ref.dtype)
            dvc_ref[:, c0:c0 + HEAD_DIM] = dvv[BLK:].astype(dvc_ref.dtype)

    kv_out = pl.BlockSpec((BLK, KV_W), lambda n: (n, 0))
    kv_shape = jax.ShapeDtypeStruct((n_rows, KV_W), F32)
    acc_shape = bias2.shape[1:]
    return pl.pallas_call(
        body,
        grid=(nb,),
        in_specs=_attn_in_specs(bias2) + [
            pl.BlockSpec((BLK, ATTN_W), lambda n: (n, 0)),
            pl.BlockSpec((BLK, ATTN_W), lambda n: (n, 0)),
            pl.BlockSpec(sinkcol.shape, lambda n: (0, 0)),
        ],
        out_specs=[
            pl.BlockSpec((BLK, ATTN_W), lambda n: (n, 0)),
            kv_out, kv_out, kv_out, kv_out,
            pl.BlockSpec(acc_shape, lambda n: (0, 0)),
            pl.BlockSpec(sinkcol.shape, lambda n: (0, 0)),
        ],
        out_shape=[
            jax.ShapeDtypeStruct((n_rows, ATTN_W), BF16),
            kv_shape, kv_shape, kv_shape, kv_shape,
            jax.ShapeDtypeStruct(acc_shape, F32),
            jax.ShapeDtypeStruct(sinkcol.shape, F32),
        ],
        compiler_params=_cparams(("arbitrary",)),
        name="attn_bwd",
    )(proj, proj, proj, proj, proj, bias2, attn, dattn, sinkcol)


def _bias_tables(rel_bias_t, onehot_t, band_first, band_rest):
    def body(rb_ref, oh_ref, mf_ref, mr_ref, out_ref):
        acc = jnp.zeros((N_Q_HEADS, BLK * 2 * BLK), F32)
        for part in _split3(rb_ref[...]):
            acc = acc + jnp.dot(part, oh_ref[...], preferred_element_type=F32)
        out_ref[0] = jnp.where(mf_ref[...] > 0.0, acc, NEG_INF)
        out_ref[1] = jnp.where(mr_ref[...] > 0.0, acc, NEG_INF)

    return pl.pallas_call(
        body,
        out_shape=jax.ShapeDtypeStruct((2, N_Q_HEADS, BLK * 2 * BLK), F32),
        compiler_params=pltpu.CompilerParams(vmem_limit_bytes=VMEM_LIMIT),
        name="bias_tables",
    )(rel_bias_t, onehot_t, band_first, band_rest)


def _split3(a):
    hi = a.astype(BF16)
    r1 = a - hi.astype(F32)
    mid = r1.astype(BF16)
    lo = (r1 - mid.astype(F32)).astype(BF16)
    return hi, mid, lo


def _bucket_reduce(dbias, dsink, onehot_t):
    def body(db_ref, ds_ref, oh_ref, ob_ref, os_ref):
        acc = jnp.zeros((N_Q_HEADS, 128), F32)
        for part in _split3(db_ref[...]):
            acc = acc + lax.dot_general(part, oh_ref[...], (((1,), (1,)), ((), ())),
                                        preferred_element_type=F32)
        ob_ref[...] = acc
        os_ref[...] = jnp.broadcast_to(jnp.sum(ds_ref[...], axis=-1, keepdims=True),
                                       os_ref.shape)

    return pl.pallas_call(
        body,
        out_shape=[jax.ShapeDtypeStruct((N_Q_HEADS, 128), F32),
                   jax.ShapeDtypeStruct((N_Q_HEADS, 128), F32)],
        compiler_params=pltpu.CompilerParams(vmem_limit_bytes=VMEM_LIMIT),
        name="bias_bucket_reduce",
    )(dbias, dsink, onehot_t)


def _disc(lr, li, ls, btr, bti):
    lam_re = jnp.minimum(lr, -1e-4)
    delta = jnp.exp(ls)
    mag = jnp.exp(lam_re * delta)
    ang = li * delta
    ar, ai = mag * jnp.cos(ang), mag * jnp.sin(ang)
    nr, ni = ar - 1.0, ai
    den = lam_re * lam_re + li * li
    fr = (nr * lam_re + ni * li) / den
    fi = (ni * lam_re - nr * li) / den
    bbr = fr * btr - fi * bti
    bbi = fr * bti + fi * btr
    return ar, ai, bbr, bbi


def _block_mask():
    row = lax.broadcasted_iota(jnp.int32, (SSM_W, SSM_H), 0)
    col = lax.broadcasted_iota(jnp.int32, (SSM_W, SSM_H), 1)
    return (row // SSM_P) == (col // SSM_N)


def _ssm_setup(lr, li, ls, btr, bti, ctr, cti):
    def body(lr_ref, li_ref, ls_ref, btr_ref, bti_ref, ctr_ref, cti_ref, a_ref, b_ref, c_ref):
        ar, ai, bbr, bbi = _disc(lr_ref[...], li_ref[...], ls_ref[...], btr_ref[...], bti_ref[...])
        a_ref[:, :SSM_H] = ar
        a_ref[:, SSM_H:] = ai
        mask = _block_mask()
        blk = lambda t: jnp.where(mask, jnp.tile(t, (SSM_G, 1)), 0.0)
        b_ref[:, :SSM_H] = blk(bbr).astype(BF16)
        b_ref[:, SSM_H:] = blk(bbi).astype(BF16)
        c_ref[:, :SSM_H] = blk(ctr_ref[...]).astype(BF16)
        c_ref[:, SSM_H:] = blk(-cti_ref[...]).astype(BF16)

    return pl.pallas_call(
        body,
        out_shape=[jax.ShapeDtypeStruct((1, 2 * SSM_H), F32),
                   jax.ShapeDtypeStruct((SSM_W, 2 * SSM_H), BF16),
                   jax.ShapeDtypeStruct((SSM_W, 2 * SSM_H), BF16)],
        compiler_params=pltpu.CompilerParams(vmem_limit_bytes=VMEM_LIMIT),
        name="ssm_setup",
    )(lr, li, ls, btr, bti, ctr, cti)


def _ssm_param_bwd(lr, li, ls, btr, bti, dacc, dbcat, dccat, gind):
    def body(lr_ref, li_ref, ls_ref, btr_ref, bti_ref, dacc_ref, db_ref, dc_ref, g_ref,
             dlr_ref, dli_ref, dls_ref, dbtr_ref, dbti_ref, dctr_ref, dcti_ref):
        dar = jnp.sum(dacc_ref[:, :SSM_H], axis=0, keepdims=True)
        dai = jnp.sum(dacc_ref[:, SSM_H:], axis=0, keepdims=True)
        col = lax.broadcasted_iota(jnp.int32, (SSM_P, 2 * SSM_H), 1)
        grp = (col % SSM_H) // SSM_N
        db = jnp.zeros((SSM_P, 2 * SSM_H), F32)
        dc = jnp.zeros((SSM_P, 2 * SSM_H), F32)
        for g in range(SSM_G):
            sel = grp == g
            db = db + jnp.where(sel, db_ref[g * SSM_P:(g + 1) * SSM_P, :], 0.0)
            dc = dc + jnp.where(sel, dc_ref[g * SSM_P:(g + 1) * SSM_P, :], 0.0)
        dctr_ref[...] = dc[:, :SSM_H]
        dcti_ref[...] = -dc[:, SSM_H:]
        prim = (lr_ref[...], li_ref[...], ls_ref[...], btr_ref[...], bti_ref[...])
        _, vjp = jax.vjp(_disc, *prim)
        dlr, dli, dls, dbtr, dbti = vjp((dar, dai, db[:, :SSM_H], db[:, SSM_H:]))
        dlr_ref[...] = dlr
        dli_ref[...] = dli
        dbtr_ref[...] = dbtr
        dbti_ref[...] = dbti
        acc = jnp.zeros((8, 128), F32)
        for part in _split3(jnp.broadcast_to(dls, (8, SSM_H))):
            acc = acc + jnp.dot(part, g_ref[...], preferred_element_type=F32)
        dls_ref[...] = acc

    vec = jax.ShapeDtypeStruct((1, SSM_H), F32)
    mat = jax.ShapeDtypeStruct((SSM_P, SSM_H), F32)
    return pl.pallas_call(
        body,
        out_shape=[vec, vec, jax.ShapeDtypeStruct((8, 128), F32), mat, mat, mat, mat],
        compiler_params=pltpu.CompilerParams(vmem_limit_bytes=VMEM_LIMIT),
        name="ssm_param_bwd",
    )(lr, li, ls, btr, bti, dacc, dbcat, dccat, gind)


SCAN_TR = 256


def _cmul_add(vr, vi, pr, pi, sr, si):
    return vr + pr * sr - pi * si, vi + pr * si + pi * sr


def _bcast_row(v, row, which):
    b = jnp.where(row == which, v, 0.0)
    b = b + pltpu.roll(b, 4, 0)
    b = b + pltpu.roll(b, 2, 0)
    return b + pltpu.roll(b, 1, 0)


def _scan_tables(a_ref, tab_ref, reverse):
    H = SSM_H
    ar = jnp.broadcast_to(a_ref[:, :H], (8, H))
    ai = jnp.broadcast_to(a_ref[:, H:], (8, H))
    if reverse:
        ai = -ai
    row = lax.broadcasted_iota(jnp.int32, (8, H), 0)
    pw = [(ar, ai)]
    for _ in range(7):
        cr, ci = pw[-1]
        pw.append((cr * ar - ci * ai, cr * ai + ci * ar))
    pcr = jnp.zeros((8, H), F32)
    pci = jnp.zeros((8, H), F32)
    for e in range(8):
        sel = (row == (7 - e)) if reverse else (row == e)
        pcr = jnp.where(sel, pw[e][0], pcr)
        pci = jnp.where(sel, pw[e][1], pci)
    tab_ref[0, :, :H] = pcr
    tab_ref[0, :, H:] = pci
    for t, k in enumerate((1, 2, 4)):
        keep = (row < 8 - k) if reverse else (row >= k)
        tab_ref[1 + t, :, :H] = jnp.where(keep, pw[k - 1][0], 0.0)
        tab_ref[1 + t, :, H:] = jnp.where(keep, pw[k - 1][1], 0.0)


def _scan_group(vr, vi, cr, ci, tab_ref, reverse):
    H = SSM_H
    for t, k in enumerate((1, 2, 4)):
        sh = 8 - k if reverse else k
        vr, vi = _cmul_add(vr, vi, tab_ref[1 + t, :, :H], tab_ref[1 + t, :, H:],
                           pltpu.roll(vr, sh, 0), pltpu.roll(vi, sh, 0))
    return _cmul_add(vr, vi, tab_ref[0, :, :H], tab_ref[0, :, H:], cr, ci)


def _scan_fwd(bu, abar, n_rows):
    H = SSM_H
    nt = n_rows // SCAN_TR

    def body(bu_ref, a_ref, xs_ref, xp_ref, tab_ref, carry_ref):
        @pl.when(pl.program_id(0) == 0)
        def _():
            _scan_tables(a_ref, tab_ref, False)
            carry_ref[...] = jnp.zeros(carry_ref.shape, F32)

        row = lax.broadcasted_iota(jnp.int32, (8, H), 0)

        def group(j, carry):
            cr, ci = carry
            r0 = pl.multiple_of(j * 16, 16)
            xr, xi = [], []
            for half in range(2):
                rr = pl.multiple_of(r0 + 8 * half, 8)
                vr, vi = _scan_group(bu_ref[pl.ds(rr, 8), :H], bu_ref[pl.ds(rr, 8), H:],
                                     cr, ci, tab_ref, False)
                xp_ref[pl.ds(rr, 8), :H] = jnp.where(row == 0, cr, pltpu.roll(vr, 1, 0))
                xp_ref[pl.ds(rr, 8), H:] = jnp.where(row == 0, ci, pltpu.roll(vi, 1, 0))
                cr, ci = _bcast_row(vr, row, 7), _bcast_row(vi, row, 7)
                xr.append(vr)
                xi.append(vi)
            xs_ref[pl.ds(r0, 16), :H] = jnp.concatenate(xr, axis=0).astype(BF16)
            xs_ref[pl.ds(r0, 16), H:] = jnp.concatenate(xi, axis=0).astype(BF16)
            return cr, ci

        cr, ci = lax.fori_loop(0, SCAN_TR // 16, group,
                               (carry_ref[:, :H], carry_ref[:, H:]))
        carry_ref[:, :H] = cr
        carry_ref[:, H:] = ci

    return pl.pallas_call(
        body,
        grid=(nt,),
        in_specs=[pl.BlockSpec((SCAN_TR, 2 * H), lambda i: (i, 0)),
                  pl.BlockSpec((1, 2 * H), lambda i: (0, 0))],
        out_specs=[pl.BlockSpec((SCAN_TR, 2 * H), lambda i: (i, 0)),
                   pl.BlockSpec((SCAN_TR, 2 * H), lambda i: (i, 0))],
        out_shape=[jax.ShapeDtypeStruct((n_rows, 2 * H), BF16),
                   jax.ShapeDtypeStruct((n_rows, 2 * H), F32)],
        scratch_shapes=[pltpu.VMEM((4, 8, 2 * H), F32), pltpu.VMEM((8, 2 * H), F32)],
        compiler_params=_cparams(("arbitrary",)),
        name="ssm_scan_fwd",
    )(bu, abar)


def _scan_bwd(gx, xprev, abar, n_rows):
    H = SSM_H
    nt = n_rows // SCAN_TR

    def body(g_ref, xp_ref, a_ref, h_ref, da_ref, tab_ref, carry_ref):
        @pl.when(pl.program_id(0) == 0)
        def _():
            _scan_tables(a_ref, tab_ref, True)
            carry_ref[...] = jnp.zeros(carry_ref.shape, F32)
            da_ref[...] = jnp.zeros(da_ref.shape, F32)

        row = lax.broadcasted_iota(jnp.int32, (8, H), 0)
        n16 = SCAN_TR // 16

        def group(jj, carry):
            cr, ci = carry
            r0 = pl.multiple_of((n16 - 1 - jj) * 16, 16)
            hr, hi = [None, None], [None, None]
            for half in (1, 0):
                rr = pl.multiple_of(r0 + 8 * half, 8)
                vr, vi = _scan_group(g_ref[pl.ds(rr, 8), :H], g_ref[pl.ds(rr, 8), H:],
                                     cr, ci, tab_ref, True)
                pr, pi = xp_ref[pl.ds(rr, 8), :H], xp_ref[pl.ds(rr, 8), H:]
                da_ref[:, :H] += vr * pr + vi * pi
                da_ref[:, H:] += vi * pr - vr * pi
                cr, ci = _bcast_row(vr, row, 0), _bcast_row(vi, row, 0)
                hr[half], hi[half] = vr, vi
            h_ref[pl.ds(r0, 16), :H] = jnp.concatenate(hr, axis=0).astype(BF16)
            h_ref[pl.ds(r0, 16), H:] = jnp.concatenate(hi, axis=0).astype(BF16)
            return cr, ci

        cr, ci = lax.fori_loop(0, n16, group, (carry_ref[:, :H], carry_ref[:, H:]))
        carry_ref[:, :H] = cr
        carry_ref[:, H:] = ci

    rev = lambda i: (nt - 1 - i, 0)
    return pl.pallas_call(
        body,
        grid=(nt,),
        in_specs=[pl.BlockSpec((SCAN_TR, 2 * H), rev),
                  pl.BlockSpec((SCAN_TR, 2 * H), rev),
                  pl.BlockSpec((1, 2 * H), lambda i: (0, 0))],
        out_specs=[pl.BlockSpec((SCAN_TR, 2 * H), rev),
                   pl.BlockSpec((8, 2 * H), lambda i: (0, 0))],
        out_shape=[jax.ShapeDtypeStruct((n_rows, 2 * H), BF16),
                   jax.ShapeDtypeStruct((8, 2 * H), F32)],
        scratch_shapes=[pltpu.VMEM((4, 8, 2 * H), F32), pltpu.VMEM((8, 2 * H), F32)],
        compiler_params=_cparams(("arbitrary",)),
        name="ssm_scan_bwd",
    )(gx, xprev, abar)


def _adamw(parts, w, m, v, *, tr, ch, name):
    n_rows, cols = w.shape
    n_parts = len(parts)
    c1 = 1.0 - ADAM_B1 ** ADAM_STEP
    c2 = 1.0 - ADAM_B2 ** ADAM_STEP

    def fn(rv, vv, i, nt):
        g = rv[0].astype(F32)
        for p in rv[1:n_parts]:
            g = g + p.astype(F32)
        wv, mv, vval = rv[n_parts:]
        nm = ADAM_B1 * mv + (1.0 - ADAM_B1) * g
        nv = ADAM_B2 * vval + (1.0 - ADAM_B2) * (g * g)
        delta = -ADAM_LR * ((nm / c1) / (jnp.sqrt(nv / c2) + ADAM_EPS) + ADAM_WD * wv)
        return [g, delta, nm, nv], []

    rows = [_row(arr, lead=lead) for (arr, lead) in parts] + [_row(w), _row(m), _row(v)]
    return _rowwise(fn, rows, [], [(cols, F32)] * 4, [], n_rows=n_rows, tr=tr, ch=ch, name=name)


_PACK = [
    ("b_ada", 6), ("norm1_g", 1), ("b_in", 3), ("norm2_g", 1), ("final_g", 1),
    ("lambda_re", 1), ("lambda_im", 1), ("log_step", 1), ("attn_sinks", 1),
    ("rel_bias", 1), ("b_glu", 1), ("ssm_d", 1), ("loss", 1),
    ("ssm_b_re", 16), ("ssm_b_im", 16), ("ssm_c_re", 16), ("ssm_c_im", 16),
]
_PACK_OFF = {}
_off = 0
for _n, _r in _PACK:
    _PACK_OFF[_n] = (_off, _r)
    _off += _r
PACK_ROWS = -(-_off // 8) * 8


def _to_rows(a, rows):
    flat = a.reshape(-1).astype(F32)
    pad = rows * PACK_W - flat.shape[0]
    if pad:
        flat = jnp.pad(flat, (0, pad))
    return flat.reshape(rows, PACK_W)


def _b_to_rows(b):
    return jnp.transpose(b, (2, 0, 1)).reshape(SSM_P, SSM_H)


def _rows_to_b(r):
    return jnp.transpose(r.reshape(SSM_P, SSM_G, SSM_N), (1, 2, 0))


def _c_to_rows(cm):
    return jnp.transpose(cm, (1, 0, 2)).reshape(SSM_P, SSM_H)


def _rows_to_c(r):
    return jnp.transpose(r.reshape(SSM_P, SSM_G, SSM_N), (1, 0, 2))


def _pack(vals):
    out = jnp.zeros((PACK_ROWS, PACK_W), F32)
    for n, r in _PACK:
        if n in vals:
            pieces = vals[n] if isinstance(vals[n], list) else [vals[n]]
            rows_each = r // len(pieces)
            for i, piece in enumerate(pieces):
                out = lax.dynamic_update_slice(out, _to_rows(piece, rows_each),
                                               (_PACK_OFF[n][0] + i * rows_each, 0))
    return out


def _unpack(packed, name, shape):
    o, r = _PACK_OFF[name]
    n = int(np.prod(shape))
    return packed[o:o + r].reshape(-1)[:n].reshape(shape)


def _small_params_packed(p):
    return {
        "b_ada": p["b_ada"], "norm1_g": p["norm1_g"], "b_in": p["b_in"],
        "norm2_g": p["norm2_g"], "final_g": p["final_g"],
        "lambda_re": p["lambda_re"], "lambda_im": p["lambda_im"],
        "log_step": p["log_step"], "attn_sinks": p["attn_sinks"],
        "rel_bias": p["rel_bias"], "b_glu": p["b_glu"], "ssm_d": p["ssm_d"],
        "ssm_b_re": _b_to_rows(p["ssm_b_re"][0]), "ssm_b_im": _b_to_rows(p["ssm_b_im"][0]),
        "ssm_c_re": _c_to_rows(p["ssm_c_re"][0]), "ssm_c_im": _c_to_rows(p["ssm_c_im"][0]),
    }


_SMALL_SHAPES = {
    "b_ada": (1, N_MOD * D), "norm1_g": (1, D), "b_in": (1, IN_W), "norm2_g": (1, D),
    "final_g": (D,), "lambda_re": (1, SSM_G, SSM_N), "lambda_im": (1, SSM_G, SSM_N),
    "log_step": (1, SSM_G), "attn_sinks": (1, N_Q_HEADS), "rel_bias": (NUM_BUCKETS, N_Q_HEADS),
    "b_glu": (1, SSM_W), "ssm_d": (1, SSM_W),
}


def _unpack_small(packed, name):
    if name in ("ssm_b_re", "ssm_b_im"):
        o, r = _PACK_OFF[name]
        return _rows_to_b(packed[o:o + r])[None]
    if name in ("ssm_c_re", "ssm_c_im"):
        o, r = _PACK_OFF[name]
        return _rows_to_c(packed[o:o + r])[None]
    return _unpack(packed, name, _SMALL_SHAPES[name])


WEIGHT_ORDER = ['w_ada', 'b_ada', 'norm1_g', 'w_in', 'b_in', 'attn_sinks', 'rel_bias', 'lambda_re',
                'lambda_im', 'log_step', 'ssm_b_re', 'ssm_b_im', 'ssm_c_re', 'ssm_c_im', 'ssm_d',
                'w_glu', 'b_glu', 'w_attn_proj', 'w_ssm_proj', 'w_out', 'norm2_g', 'w_ff1', 'w_ff2',
                'final_g']
BIG = ['w_in', 'w_glu', 'w_attn_proj', 'w_ssm_proj', 'w_out', 'w_ff1', 'w_ff2']


ADAMW_TILE_ELEMS = 1 << 18


def _adamw_rows(rows, cols):
    tr = rows
    while tr * cols > ADAMW_TILE_ELEMS and tr % 32 == 0:
        tr //= 2
    return tr


def _cast_bf16(w, name):
    rows, cols = w.shape
    tr = min(rows, 256)
    return _rowwise(lambda rv, vv, i, nt: ([rv[0]], []), [_row(w)], [], [(cols, BF16)], [],
                    n_rows=rows, tr=tr, ch=min(tr, 32), name=name)[0]


def kernel(x, c, w_ada, b_ada, norm1_g, w_in, b_in, attn_sinks, rel_bias, lambda_re, lambda_im, log_step, ssm_b_re, ssm_b_im, ssm_c_re, ssm_c_im, ssm_d, w_glu, b_glu, w_attn_proj, w_ssm_proj, w_out, norm2_g, w_ff1, w_ff2, final_g, loss_target, m_w_ada, m_b_ada, m_norm1_g, m_w_in, m_b_in, m_attn_sinks, m_rel_bias, m_lambda_re, m_lambda_im, m_log_step, m_ssm_b_re, m_ssm_b_im, m_ssm_c_re, m_ssm_c_im, m_ssm_d, m_w_glu, m_b_glu, m_w_attn_proj, m_w_ssm_proj, m_w_out, m_norm2_g, m_w_ff1, m_w_ff2, m_final_g, v_w_ada, v_b_ada, v_norm1_g, v_w_in, v_b_in, v_attn_sinks, v_rel_bias, v_lambda_re, v_lambda_im, v_log_step, v_ssm_b_re, v_ssm_b_im, v_ssm_c_re, v_ssm_c_im, v_ssm_d, v_w_glu, v_b_glu, v_w_attn_proj, v_w_ssm_proj, v_w_out, v_norm2_g, v_w_ff1, v_w_ff2, v_final_g):
    loc = dict(locals())
    W = {n: loc[n] for n in WEIGHT_ORDER}
    Mo = {n: loc["m_" + n] for n in WEIGHT_ORDER}
    Vo = {n: loc["v_" + n] for n in WEIGHT_ORDER}
    S = x.shape[1]
    TM = min(512, S)
    TS = min(1024, S)
    TR = min(256, S)
    me = 4 * lax.axis_index("x") + 2 * lax.axis_index("y") + lax.axis_index("c")
    x2d = x.reshape(S, D)
    tgt = loss_target.reshape(S, D)

    shard = {n: W[n][0] for n in BIG}
    w16 = {n: _cast_bf16(shard[n], "cast_" + n) for n in BIG}
    G = {"w_in": _allgather_weights([w16["w_in"]], "allgather_w_in")[0]}
    later = [n for n in BIG if n != "w_in"]
    flights, tok_w = _exchange_start([w16[n] for n in later], "gather", "weights_start", G["w_in"])
    w_flight = dict(zip(later, flights))

    c_all = _small_allgather(c, "allgather_c").reshape(N_DEV, D)
    cs = _rowwise(lambda rv, vv, i, nt: ([rv[0] * _sigmoid(rv[0])], []), [_row(c_all)], [],
                  [(D, F32)], [], n_rows=N_DEV, tr=8, ch=8, name="silu_c")[0]
    n_ada = N_MOD * D // N_DEV
    b_ada_cols = lax.dynamic_slice(b_ada, (0, me * n_ada), (1, n_ada))
    mod_piece = _matmul(cs, w_ada[0], mode="nn", dims=(N_DEV, n_ada, D), tiles=(N_DEV, 512, D),
                        out_dtypes=[F32], name="ada_fwd", bias=b_ada_cols)
    mod_all = _small_allgather(mod_piece, "allgather_mod")
    mod_b = lax.dynamic_index_in_dim(mod_all, me, axis=1, keepdims=False).reshape(N_MOD, D)
    sh1, sc1, g1, sh2, sc2, g2 = [mod_b[i:i + 1] for i in range(N_MOD)]

    def f_norm1(rv, vv, i, nt):
        xv, (g, sc, sh) = rv[0], vv
        return [(xv * _rms(xv) * g) * (1.0 + sc) + sh], []

    h = _rowwise(f_norm1, [_row(x2d)], [norm1_g, sc1, sh1], [(D, BF16)], [],
                 n_rows=S, tr=TR, ch=32, name="norm1_fwd")[0]
    proj = _matmul(h, G["w_in"], mode="nn", dims=(S, IN_W, D), tiles=(TM, 768, D),
                   out_dtypes=[F32], name="in_proj", b3=True, bias=b_in, dep=tok_w)

    buckets = _t5_buckets_block()
    band = _band_mask()
    onehot_t = jnp.asarray(
        (np.arange(128)[:, None] == buckets.reshape(-1)[None, :]).astype(np.float32), BF16)
    band_first = band & (np.arange(2 * BLK)[None, :] >= BLK)
    rel_bias_t = jnp.pad(jnp.transpose(rel_bias), ((0, 0), (0, 128 - NUM_BUCKETS)))
    bias2 = _bias_tables(rel_bias_t, onehot_t,
                         jnp.asarray(band_first.reshape(1, -1).astype(np.float32)),
                         jnp.asarray(band.reshape(1, -1).astype(np.float32))
                         ).reshape(2, N_Q_HEADS * BLK, 2 * BLK)
    sinkcol = jnp.repeat(attn_sinks.reshape(N_Q_HEADS), BLK).reshape(N_Q_HEADS * BLK, 1)
    attn = _attention_fwd(proj, bias2, sinkcol, S)
    mixer = ["w_attn_proj", "w_glu", "w_ssm_proj", "w_out"]
    G.update(zip(mixer, _exchange_wait([w_flight[n] for n in mixer], "gather", attn, "weights_wait_mixer")))
    w_glu_f = G["w_glu"].reshape(SSM_W, SSM_W)
    w_out_f = G["w_out"].reshape(D, D)
    y_attn = _matmul(attn, G["w_attn_proj"], mode="nn", dims=(S, D, ATTN_W), tiles=(TM, 256, ATTN_W),
                     out_dtypes=[F32], name="attn_proj", b3=True)

    lam_re = lambda_re.reshape(1, SSM_H)
    lam_im = lambda_im.reshape(1, SSM_H)
    ls_x = jnp.repeat(log_step.reshape(SSM_G), SSM_N).reshape(1, SSM_H)
    btr, bti = _b_to_rows(ssm_b_re[0]), _b_to_rows(ssm_b_im[0])
    ctr, cti = _c_to_rows(ssm_c_re[0]), _c_to_rows(ssm_c_im[0])
    abar, bcat, ccat = _ssm_setup(lam_re, lam_im, ls_x, btr, bti, ctr, cti)
    u_blk = (ATTN_W + 2 * KV_W) // SSM_W
    bu = _matmul(proj, bcat, mode="nn", dims=(S, 2 * SSM_H, SSM_W), tiles=(TM, 2048, SSM_W),
                 out_dtypes=[F32], name="ssm_bu", a_off=u_blk)
    xs, xprev = _scan_fwd(bu, abar, S)
    yc = _matmul(xs, ccat, mode="nt", dims=(S, SSM_W, 2 * SSM_H), tiles=(TM, SSM_W, 2 * SSM_H),
                 out_dtypes=[F32], name="ssm_cx")

    def f_ssm_out(rv, vv, i, nt):
        y = rv[0] + vv[0] * rv[1]
        return [y, _gelu(y)], []

    y_ssm_pre, z = _rowwise(f_ssm_out, [_row(yc), _row(proj, u_blk, SSM_W)], [ssm_d],
                            [(SSM_W, F32), (SSM_W, BF16)], [], n_rows=S, tr=TM, ch=32, name="ssm_out")
    zg = _matmul(z, w_glu_f, mode="nn", dims=(S, SSM_W, SSM_W), tiles=(TM, SSM_W, SSM_W),
                 out_dtypes=[F32], name="glu_proj", bias=b_glu)
    z2 = _rowwise(lambda rv, vv, i, nt: ([rv[0].astype(F32) * _sigmoid(rv[1])], []),
                  [_row(z), _row(zg)], [], [(SSM_W, BF16)], [], n_rows=S, tr=TM, ch=32, name="glu_gate")[0]
    y_ssm = _matmul(z2, G["w_ssm_proj"], mode="nn", dims=(S, D, SSM_W), tiles=(TM, 256, SSM_W),
                    out_dtypes=[F32], name="ssm_proj", b3=True)

    ga_row = _row(proj, 1, D)
    gs_row = _row(proj, 2, D)

    def f_merge(rv, vv, i, nt):
        ga, gs, ya, ys = rv
        return [_sigmoid(ga) * ya + _sigmoid(gs) * ys], []

    merged = _rowwise(f_merge, [ga_row, gs_row, _row(y_attn), _row(y_ssm)], [], [(D, BF16)], [],
                      n_rows=S, tr=TR, ch=32, name="merge")[0]
    mo = _matmul(merged, w_out_f, mode="nn", dims=(S, D, D), tiles=(TM, 1024, D),
                 out_dtypes=[F32], name="out_proj")

    def f_norm2(rv, vv, i, nt):
        xv, mv = rv
        g1v, g, sc, sh = vv
        x1v = xv + g1v * mv
        return [x1v, (x1v * _rms(x1v) * g) * (1.0 + sc) + sh], []

    x1, h2 = _rowwise(f_norm2, [_row(x2d), _row(mo)], [g1, norm2_g, sc2, sh2],
                      [(D, F32), (D, BF16)], [], n_rows=S, tr=TR, ch=32, name="norm2_fwd")

    def relu_sq(acc):
        r = jnp.maximum(acc, 0.0)
        return r * r, r

    G["w_ff1"] = _exchange_wait([w_flight["w_ff1"]], "gather", h2, "weights_wait_ff1")[0]
    act, relu = _matmul(h2, G["w_ff1"], mode="nn", dims=(S, D_FF, D), tiles=(TM, 1024, D),
                        out_dtypes=[BF16, BF16], name="ff1", b3=True, epilogue=relu_sq)
    w_ff2_f = _exchange_wait([w_flight["w_ff2"]], "gather", act, "weights_wait_ff2")[0].reshape(D_FF, D)
    ff = _matmul(act, w_ff2_f, mode="nn", dims=(S, D, D_FF), tiles=(TM, 1024, 2048),
                 out_dtypes=[F32], name="ff2")

    def f_loss(rv, vv, i, nt):
        x1v, ffv, tv = rv
        g2v, gf = vv
        x2v = x1v + g2v * ffv
        r = _rms(x2v)
        xh = x2v * r
        diff = xh * gf - tv
        dy = diff * (1.0 / D)
        dxh = dy * gf
        dx2 = r * (dxh - xh * jnp.mean(dxh * xh, axis=-1, keepdims=True))
        return [dx2, dx2 * g2v], [_colsum(0.5 * diff * diff * (1.0 / D)), _colsum(dy * xh),
                                  _colsum(dx2 * ffv)]

    dx2, dff, loss_cols, d_final_g, dg2 = _rowwise(
        f_loss, [_row(x1), _row(ff), _row(tgt)], [g2, final_g.reshape(1, D)],
        [(D, F32), (D, BF16)], [(1, D)] * 3, n_rows=S, tr=TR, ch=32, name="loss_bwd")

    df1 = _matmul(dff, w_ff2_f, mode="nt", dims=(S, D_FF, D), tiles=(TM, 1024, D),
                  out_dtypes=[BF16], name="ff2_dgrad", extras=(relu,),
                  epilogue=lambda acc, r: (acc * (2.0 * r.astype(F32)),))
    gw_ff2 = _matmul(act, dff, mode="tn", dims=(D_FF, D, S), tiles=(1024, 1024, TS),
                     out_dtypes=[BF16], name="ff2_wgrad").reshape(N_DEV, D_FF // N_DEV, D)
    g_flight = {}
    (g_flight["w_ff2"],), tok = _exchange_start([gw_ff2], "scatter", "grads_start_ff2")
    dh2 = _matmul(df1, G["w_ff1"], mode="nt", dims=(S, D, D_FF), tiles=(TM, D, 1024),
                  out_dtypes=[F32], name="ff1_dgrad", b3=True, dep=tok)
    gw_ff1 = _matmul(h2, df1, mode="tn", dims=(D, D_FF, S), tiles=(1024, 1024, TS),
                     out_dtypes=[BF16], name="ff1_wgrad", out3=True)
    (g_flight["w_ff1"],), tok = _exchange_start([gw_ff1], "scatter", "grads_start_ff1")

    def f_norm2_bwd(rv, vv, i, nt):
        x1v, dh, dx2v, mv = rv
        g, sc, g1v = vv
        r = _rms(x1v)
        xh = x1v * r
        t = xh * g
        dt = dh * (1.0 + sc)
        dxh = dt * g
        dx1 = dx2v + r * (dxh - xh * jnp.mean(dxh * xh, axis=-1, keepdims=True))
        return [dx1, dx1 * g1v], [_colsum(dh), _colsum(dh * t), _colsum(dt * xh), _colsum(dx1 * mv)]

    dx1, dmo, dsh2, dsc2, d_norm2_g, dg1 = _rowwise(
        f_norm2_bwd, [_row(x1), _row(dh2), _row(dx2), _row(mo)], [norm2_g, sc2, g1],
        [(D, F32), (D, BF16)], [(1, D)] * 4, n_rows=S, tr=TR, ch=16, name="norm2_bwd", dep=tok)

    dmerged = _matmul(dmo, w_out_f, mode="nt", dims=(S, D, D), tiles=(TM, 1024, D),
                      out_dtypes=[F32], name="out_dgrad")
    gw_out = _matmul(merged, dmo, mode="tn", dims=(D, D, S), tiles=(1024, 1024, TS),
                     out_dtypes=[BF16], name="out_wgrad").reshape(N_DEV, D // N_DEV, D)
    (g_flight["w_out"],), tok = _exchange_start([gw_out], "scatter", "grads_start_out")

    def f_merge_bwd(rv, vv, i, nt):
        dm, ga, gs, ya, ys = rv
        sa, ss = _sigmoid(ga), _sigmoid(gs)
        return [dm * sa, dm * ss, dm * ya * sa * (1.0 - sa), dm * ys * ss * (1.0 - ss)], []

    dy_attn, dy_ssm, dga, dgs = _rowwise(
        f_merge_bwd, [_row(dmerged), ga_row, gs_row, _row(y_attn), _row(y_ssm)], [],
        [(D, BF16)] * 4, [], n_rows=S, tr=TR, ch=16, name="merge_bwd", dep=tok)

    dz2 = _matmul(dy_ssm, G["w_ssm_proj"], mode="nt", dims=(S, SSM_W, D), tiles=(TM, SSM_W, 256),
                  out_dtypes=[F32], name="ssm_proj_dgrad", b3=True)
    gw_ssm_proj = _matmul(z2, dy_ssm, mode="tn", dims=(SSM_W, D, S), tiles=(SSM_W, 256, TS),
                          out_dtypes=[BF16], name="ssm_proj_wgrad", out3=True)

    def f_glu_bwd(rv, vv, i, nt):
        dz2v, zv, zgv = rv
        sg = _sigmoid(zgv)
        dzg = dz2v * zv.astype(F32) * sg * (1.0 - sg)
        return [dzg, dz2v * sg], [_colsum(dzg)]

    dzg, dz_a, d_b_glu = _rowwise(f_glu_bwd, [_row(dz2), _row(z), _row(zg)], [],
                                  [(SSM_W, BF16), (SSM_W, F32)], [(1, SSM_W)],
                                  n_rows=S, tr=TM, ch=32, name="glu_bwd")
    dz_b = _matmul(dzg, w_glu_f, mode="nt", dims=(S, SSM_W, SSM_W), tiles=(TM, SSM_W, SSM_W),
                   out_dtypes=[F32], name="glu_dgrad")
    gw_glu = _matmul(z, dzg, mode="tn", dims=(SSM_W, SSM_W, S), tiles=(SSM_W, SSM_W, TS),
                     out_dtypes=[BF16], name="glu_wgrad").reshape(N_DEV, SSM_W // N_DEV, SSM_W)
    (g_flight["w_ssm_proj"], g_flight["w_glu"]), tok = _exchange_start(
        [gw_ssm_proj, gw_glu], "scatter", "grads_start_ssm")

    def f_ssm_out_bwd(rv, vv, i, nt):
        dza, dzb, yv, uv = rv
        dy = (dza + dzb) * _gelu_grad(yv)
        return [dy, dy * vv[0]], [_colsum(dy * uv)]

    dy_s, du_a, d_ssm_d = _rowwise(
        f_ssm_out_bwd, [_row(dz_a), _row(dz_b), _row(y_ssm_pre), _row(proj, u_blk, SSM_W)], [ssm_d],
        [(SSM_W, BF16), (SSM_W, F32)], [(1, SSM_W)], n_rows=S, tr=TM, ch=32, name="ssm_out_bwd", dep=tok)
    gx = _matmul(dy_s, ccat, mode="nn", dims=(S, 2 * SSM_H, SSM_W), tiles=(TM, 2048, SSM_W),
                 out_dtypes=[F32], name="ssm_cx_dgrad")
    dccat = _matmul(dy_s, xs, mode="tn", dims=(SSM_W, 2 * SSM_H, S), tiles=(SSM_W, 2048, TS),
                    out_dtypes=[F32], name="ssm_c_wgrad")
    hs, dacc = _scan_bwd(gx, xprev, abar, S)
    du_b = _matmul(hs, bcat, mode="nt", dims=(S, SSM_W, 2 * SSM_H), tiles=(TM, SSM_W, 2 * SSM_H),
                   out_dtypes=[F32], name="ssm_bu_dgrad")
    dbcat = _matmul(proj, hs, mode="tn", dims=(SSM_W, 2 * SSM_H, S), tiles=(SSM_W, 2048, TS),
                    out_dtypes=[F32], name="ssm_b_wgrad", a_off=u_blk)
    grp = np.arange(SSM_H) // SSM_N
    gind = jnp.asarray((grp[:, None] == np.arange(128)[None, :]).astype(np.float32), BF16)
    d_lam_re, d_lam_im, d_ls, d_btr, d_bti, d_ctr, d_cti = _ssm_param_bwd(
        lam_re, lam_im, ls_x, btr, bti, dacc, dbcat, dccat, gind)

    dattn = _matmul(dy_attn, G["w_attn_proj"], mode="nt", dims=(S, ATTN_W, D), tiles=(TM, ATTN_W, 256),
                    out_dtypes=[BF16], name="attn_proj_dgrad", b3=True)
    gw_attn_proj = _matmul(attn, dy_attn, mode="tn", dims=(ATTN_W, D, S), tiles=(ATTN_W, 256, TS),
                           out_dtypes=[BF16], name="attn_proj_wgrad", out3=True)
    (g_flight["w_attn_proj"],), tok = _exchange_start(
        [gw_attn_proj], "scatter", "grads_start_attn")
    dq, dkc, dkp, dvc, dvp, dbias, dsink = _attention_bwd(proj, attn, dattn, bias2, sinkcol, S)
    d_bias_b, d_sinks = _bucket_reduce(dbias.reshape(N_Q_HEADS, BLK * 2 * BLK),
                                       dsink.reshape(N_Q_HEADS, BLK), onehot_t)

    def f_dproj(rv, vv, i, nt):
        dqv, kc, kp, vc, vp, dua, dub, gav, gsv = rv
        keep = (i < nt - 1).astype(F32)
        dp = jnp.concatenate([dqv.astype(F32), kc + keep * kp, vc + keep * vp, dua + dub,
                              gav.astype(F32), gsv.astype(F32)], axis=-1)
        return [dp], [_colsum(dp)]

    dproj, d_b_in = _rowwise(
        f_dproj, [_row(dq), _row(dkc), _row(dkp, shift=1), _row(dvc), _row(dvp, shift=1),
                  _row(du_a), _row(du_b), _row(dga), _row(dgs)], [],
        [(IN_W, BF16)], [(1, IN_W)], n_rows=S, tr=BLK, ch=16, name="dproj", dep=tok)
    gw_in = _matmul(h, dproj, mode="tn", dims=(D, IN_W, S), tiles=(1024, 768, TS),
                    out_dtypes=[BF16], name="in_wgrad", out3=True)
    (g_flight["w_in"],), tok = _exchange_start([gw_in], "scatter", "grads_start_in")
    dh = _matmul(dproj, G["w_in"], mode="nt", dims=(S, D, IN_W), tiles=(TM, D, 768),
                 out_dtypes=[F32], name="in_dgrad", b3=True, dep=tok)

    def f_norm1_bwd(rv, vv, i, nt):
        xv, dhv, dx1v = rv
        g, sc = vv
        r = _rms(xv)
        xh = xv * r
        t = xh * g
        dt = dhv * (1.0 + sc)
        dxh = dt * g
        dxv = dx1v + r * (dxh - xh * jnp.mean(dxh * xh, axis=-1, keepdims=True))
        return [dxv], [_colsum(dhv), _colsum(dhv * t), _colsum(dt * xh)]

    grad_x, dsh1, dsc1, d_norm1_g = _rowwise(
        f_norm1_bwd, [_row(x2d), _row(dh), _row(dx1)], [norm1_g, sc1],
        [(D, F32)], [(1, D)] * 3, n_rows=S, tr=TR, ch=32, name="norm1_bwd")

    part = _pack({
        "b_ada": [dsh1, dsc1, dg1, dsh2, dsc2, dg2], "norm1_g": d_norm1_g, "b_in": d_b_in, "norm2_g": d_norm2_g,
        "final_g": d_final_g, "lambda_re": d_lam_re, "lambda_im": d_lam_im,
        "log_step": d_ls[0, :SSM_G], "attn_sinks": d_sinks[:, 0],
        "rel_bias": jnp.transpose(d_bias_b[:, :NUM_BUCKETS]), "b_glu": d_b_glu, "ssm_d": d_ssm_d,
        "loss": loss_cols, "ssm_b_re": d_btr, "ssm_b_im": d_bti, "ssm_c_re": d_ctr, "ssm_c_im": d_cti,
    })
    part_all = _small_allgather(part, "allgather_small_grads")
    wp, mp, vp = [_pack(_small_params_packed(p)) for p in (W, Mo, Vo)]
    sg, sdelta, sm, sv = _adamw([(part_all, d) for d in range(N_DEV)], wp, mp, vp,
                                tr=PACK_ROWS, ch=8, name="adamw_small")
    lo, _ = _PACK_OFF["loss"]
    loss = jnp.sum(sg[lo])

    o_ada, _ = _PACK_OFF["b_ada"]
    dmod_all = part_all[:, o_ada:o_ada + N_MOD, :].reshape(N_DEV, N_MOD * D)
    dmod_cols = lax.dynamic_slice(dmod_all, (0, me * n_ada), (N_DEV, n_ada))
    gw_ada = _matmul(cs, dmod_cols, mode="tn", dims=(D, n_ada, N_DEV), tiles=(D, 512, N_DEV),
                     out_dtypes=[F32], name="ada_wgrad")

    big_out = {"w_ada": _adamw([(gw_ada, 0)], w_ada[0], m_w_ada[0], v_w_ada[0],
                               tr=_adamw_rows(D, n_ada), ch=16, name="adamw_w_ada")}
    after = big_out["w_ada"][0]
    for n in ["w_ff2", "w_ff1", "w_out", "w_ssm_proj", "w_glu", "w_attn_proj", "w_in"]:
        recv = _exchange_wait([g_flight[n]], "scatter", after, "grads_wait_" + n[2:])[0]
        rows, cols = shard[n].shape
        big_out[n] = _adamw([(recv, d) for d in range(N_DEV)], shard[n], Mo[n][0], Vo[n][0],
                            tr=_adamw_rows(rows, cols), ch=16, name="adamw_" + n)
        after = big_out[n][0]

    def leaf(kind, n):
        if n in big_out:
            return big_out[n][kind][None]
        return _unpack_small((sg, sdelta, sm, sv)[kind], n)

    outs = [loss, grad_x.reshape(1, S, D)]
    for kind in range(4):
        outs.extend(leaf(kind, n) for n in WEIGHT_ORDER)
    return tuple(outs)
```

```python
import functools
import math

import numpy as np
import jax
import jax.numpy as jnp
from jax import lax
from jax.experimental import pallas as pl
from jax.experimental.pallas import tpu as pltpu

F32 = jnp.float32
BF16 = jnp.bfloat16
MESH = pl.DeviceIdType.MESH

N_DEV = 8
D = 2048
HEAD_DIM = 64
N_Q_HEADS = 16
N_KV_HEADS = 4
GROUP = N_Q_HEADS // N_KV_HEADS
ATTN_W = N_Q_HEADS * HEAD_DIM
KV_W = N_KV_HEADS * HEAD_DIM
BLK = 128
NUM_BUCKETS = 32
MAX_DISTANCE = 128
NEG_INF = -1e30
SSM_W = 512
SSM_P = 16
SSM_G = 32
SSM_N = 64
SSM_H = SSM_G * SSM_N
D_FF = 4 * D
IN_W = ATTN_W + 2 * KV_W + SSM_W + 2 * D
N_MOD = 6
EPS = 1e-6

ADAM_LR = 0.001
ADAM_B1 = 0.9
ADAM_B2 = 0.999
ADAM_EPS = 1e-08
ADAM_WD = 0.01
ADAM_STEP = 10

VMEM_LIMIT = 56 * 1024 * 1024
PACK_W = 2048


def _cparams(sem):
    return pltpu.CompilerParams(dimension_semantics=sem, vmem_limit_bytes=VMEM_LIMIT)


def _matmul(a, b, *, mode, dims, tiles, out_dtypes, name, a_off=0, b3=False,
            out3=False, bias=None, extras=(), epilogue=None, dep=None):
    M, N, K = dims
    tm, tn, tk = tiles
    assert M % tm == 0 and N % tn == 0 and K % tk == 0, (name, dims, tiles)
    gm, gn, gk = M // tm, N // tn, K // tk
    n_extra = len(extras)
    has_bias = bias is not None
    n_out = len(out_dtypes)

    if mode == "nn":
        a_spec = pl.BlockSpec((tm, tk), lambda i, j, k: (i, a_off + k))
        if b3:
            nb = (N // N_DEV) // tn
            assert nb * tn * N_DEV == N
            b_spec = pl.BlockSpec((None, tk, tn), lambda i, j, k: (j // nb, k, j % nb))
        else:
            b_spec = pl.BlockSpec((tk, tn), lambda i, j, k: (k, j))
        dn = (((1,), (0,)), ((), ()))
    elif mode == "nt":
        a_spec = pl.BlockSpec((tm, tk), lambda i, j, k: (i, a_off + k))
        if b3:
            nb = (K // N_DEV) // tk
            assert nb * tk * N_DEV == K
            b_spec = pl.BlockSpec((None, tn, tk), lambda i, j, k: (k // nb, j, k % nb))
        else:
            b_spec = pl.BlockSpec((tn, tk), lambda i, j, k: (j, k))
        dn = (((1,), (1,)), ((), ()))
    else:
        a_spec = pl.BlockSpec((tk, tm), lambda i, j, k: (k, a_off + i))
        b_spec = pl.BlockSpec((tk, tn), lambda i, j, k: (k, j))
        dn = (((0,), (0,)), ((), ()))

    if out3:
        nbo = (N // N_DEV) // tn
        assert nbo * tn * N_DEV == N
        o_spec = pl.BlockSpec((None, tm, tn), lambda i, j, k: (j // nbo, i, j % nbo))
        o_shape = (N_DEV, M, N // N_DEV)
    else:
        o_spec = pl.BlockSpec((tm, tn), lambda i, j, k: (i, j))
        o_shape = (M, N)

    in_specs = [a_spec, b_spec]
    args = [a, b]
    if has_bias:
        in_specs.append(pl.BlockSpec((1, tn), lambda i, j, k: (0, j)))
        args.append(bias)
    for e in extras:
        in_specs.append(pl.BlockSpec((tm, tn), lambda i, j, k: (i, j)))
        args.append(e)
    n_dep = 0 if dep is None else 1
    if n_dep:
        in_specs.append(pl.BlockSpec(memory_space=pl.ANY))
        args.append(dep)

    def body(*refs):
        a_ref, b_ref = refs[0], refs[1]
        pos = 2
        bias_ref = None
        if has_bias:
            bias_ref = refs[pos]
            pos += 1
        extra_refs = refs[pos:pos + n_extra]
        pos += n_extra + n_dep
        out_refs = refs[pos:pos + n_out]
        acc_ref = refs[pos + n_out] if gk > 1 else None

        part = lax.dot_general(a_ref[...].astype(BF16), b_ref[...].astype(BF16), dn,
                               preferred_element_type=F32)

        def finish(acc):
            if has_bias:
                acc = acc + bias_ref[...]
            if epilogue is None:
                vals = (acc,)
            else:
                vals = epilogue(acc, *[e[...] for e in extra_refs])
            for o_ref, val in zip(out_refs, vals):
                o_ref[...] = val.astype(o_ref.dtype)

        if gk == 1:
            finish(part)
        else:
            k = pl.program_id(2)

            @pl.when(k == 0)
            def _():
                acc_ref[...] = part

            @pl.when(k > 0)
            def _():
                acc_ref[...] += part

            @pl.when(k == gk - 1)
            def _():
                finish(acc_ref[...])

    outs = pl.pallas_call(
        body,
        grid=(gm, gn, gk),
        in_specs=in_specs,
        out_specs=[o_spec] * n_out,
        out_shape=[jax.ShapeDtypeStruct(o_shape, dt) for dt in out_dtypes],
        scratch_shapes=([pltpu.VMEM((tm, tn), F32)] if gk > 1 else []),
        compiler_params=_cparams(("parallel", "parallel", "arbitrary")),
        name=name,
    )(*args)
    return outs[0] if n_out == 1 else outs


def _rowwise(fn, rows, vecs, row_outs, sum_outs, *, n_rows, tr, ch, name, dep=None):
    assert n_rows % tr == 0 and tr % ch == 0
    nt = n_rows // tr
    nr, nv, nro, nso = len(rows), len(vecs), len(row_outs), len(sum_outs)
    in_specs, args = [], []
    for (arr, lead, cblk, w, shift) in rows:
        if shift:
            ridx = lambda i, shift=shift: jnp.minimum(i + shift, nt - 1)
        else:
            ridx = lambda i: i
        if arr.ndim == 3:
            in_specs.append(pl.BlockSpec(
                (None, tr, w), lambda i, lead=lead, cblk=cblk, ridx=ridx: (lead, ridx(i), cblk)))
        else:
            in_specs.append(pl.BlockSpec(
                (tr, w), lambda i, cblk=cblk, ridx=ridx: (ridx(i), cblk)))
        args.append(arr)
    for v in vecs:
        in_specs.append(pl.BlockSpec(v.shape, lambda i, nd=v.ndim: (0,) * nd))
        args.append(v)
    n_dep = 0 if dep is None else 1
    if n_dep:
        in_specs.append(pl.BlockSpec(memory_space=pl.ANY))
        args.append(dep)
    out_specs = [pl.BlockSpec((tr, w), lambda i: (i, 0)) for (w, _) in row_outs]
    out_shape = [jax.ShapeDtypeStruct((n_rows, w), dt) for (w, dt) in row_outs]
    for (r, w) in sum_outs:
        out_specs.append(pl.BlockSpec((r, w), lambda i: (0, 0)))
        out_shape.append(jax.ShapeDtypeStruct((r, w), F32))

    def body(*refs):
        i = pl.program_id(0)
        r_in = refs[:nr]
        v_in = refs[nr:nr + nv]
        r_out = refs[nr + nv + n_dep:nr + nv + n_dep + nro]
        s_out = refs[nr + nv + n_dep + nro:]
        if nso:
            @pl.when(i == 0)
            def _():
                for s in s_out:
                    s[...] = jnp.zeros(s.shape, F32)
        vvals = [v[...] for v in v_in]

        def chunk(ci, carry):
            r0 = pl.multiple_of(ci * ch, ch)
            rv = [r[pl.ds(r0, ch), :] for r in r_in]
            ro, so = fn(rv, vvals, i, nt)
            for ref, val in zip(r_out, ro):
                ref[pl.ds(r0, ch), :] = val.astype(ref.dtype)
            for ref, val in zip(s_out, so):
                ref[...] += val
            return carry

        lax.fori_loop(0, tr // ch, chunk, 0)

    outs = pl.pallas_call(
        body,
        grid=(nt,),
        in_specs=in_specs,
        out_specs=out_specs,
        out_shape=out_shape,
        compiler_params=_cparams(("arbitrary",)),
        name=name,
    )(*args)
    return outs


def _row(arr, cblk=0, w=None, lead=0, shift=0):
    return (arr, lead, cblk, arr.shape[-1] if w is None else w, shift)


def _colsum(v):
    return jnp.sum(v, axis=0, keepdims=True)


def _rms(x):
    return lax.rsqrt(jnp.mean(x * x, axis=-1, keepdims=True) + EPS)


def _sigmoid(x):
    return 1.0 / (1.0 + jnp.exp(-x))


_GELU_C = math.sqrt(2.0 / math.pi)


def _gelu(x):
    return 0.5 * x * (1.0 + jnp.tanh(_GELU_C * (x + 0.044715 * (x * x * x))))


def _gelu_grad(x):
    t = jnp.tanh(_GELU_C * (x + 0.044715 * (x * x * x)))
    return 0.5 * (1.0 + t) + 0.5 * x * (1.0 - t * t) * (_GELU_C * (1.0 + 3.0 * 0.044715 * (x * x)))


def _my_pos():
    return lax.axis_index("x"), lax.axis_index("y"), lax.axis_index("c")


def _flip(pos, k):
    x, y, c = pos
    return (1 - x if k & 4 else x, 1 - y if k & 2 else y, 1 - c if k & 1 else c)


def _dev_id(pos):
    return 4 * pos[0] + 2 * pos[1] + pos[2]


def _small_allgather(x, name):
    r, c = x.shape

    def body(x_ref, out_ref, send_sems, recv_sems):
        me = _my_pos()
        out_ref[_dev_id(me)] = x_ref[...]
        copies = []
        for k in range(1, N_DEV):
            cp = pltpu.make_async_remote_copy(
                src_ref=x_ref, dst_ref=out_ref.at[_dev_id(me)],
                send_sem=send_sems.at[k - 1], recv_sem=recv_sems.at[k - 1],
                device_id=_flip(me, k), device_id_type=MESH)
            cp.start()
            copies.append(cp)
        for k in range(1, N_DEV):
            peer = _flip(me, k)
            pltpu.make_async_remote_copy(
                src_ref=x_ref, dst_ref=out_ref.at[_dev_id(peer)],
                send_sem=send_sems.at[k - 1], recv_sem=recv_sems.at[k - 1],
                device_id=peer, device_id_type=MESH).wait_recv()
        for cp in copies:
            cp.wait_send()

    return pl.pallas_call(
        body,
        out_shape=jax.ShapeDtypeStruct((N_DEV, r, c), x.dtype),
        in_specs=[pl.BlockSpec(memory_space=pltpu.VMEM)],
        out_specs=pl.BlockSpec(memory_space=pltpu.VMEM),
        scratch_shapes=[pltpu.SemaphoreType.DMA((N_DEV - 1,)),
                        pltpu.SemaphoreType.DMA((N_DEV - 1,))],
        compiler_params=pltpu.CompilerParams(vmem_limit_bytes=VMEM_LIMIT),
        name=name,
    )(x)


def _allgather_weights(shards, name):
    n = len(shards)

    def body(*refs):
        xs = refs[:n]
        outs = refs[n:2 * n]
        send_sems, recv_sems, local_sems = refs[2 * n:]
        x, y, c = _my_pos()
        me, sib = (x, y, c), (x, y, 1 - c)
        chips = [(1 - x, y), (x, 1 - y), (1 - x, 1 - y)]

        def copy(a, k, block, to, src=None):
            slot = outs[a].at[_dev_id(block)]
            return pltpu.make_async_remote_copy(
                src_ref=slot if src is None else src, dst_ref=slot,
                send_sem=send_sems.at[a, k], recv_sem=recv_sems.at[a, k],
                device_id=to, device_id_type=MESH)

        mine = [pltpu.make_async_copy(xs[a], outs[a].at[_dev_id(me)], local_sems.at[a])
                for a in range(n)]
        for cp in mine:
            cp.start()
        first = []
        for a in range(n):
            first.append(copy(a, 0, me, sib, src=xs[a]))
            for j, chip in enumerate(chips):
                first.append(copy(a, 1 + j, me, (*chip, c), src=xs[a]))
        for cp in first:
            cp.start()
        passed = []
        for a in range(n):
            for j, chip in enumerate(chips):
                copy(a, 1 + j, (*chip, c), me).wait_recv()
                cp = copy(a, 4 + j, (*chip, c), sib)
                cp.start()
                passed.append(cp)
        for a in range(n):
            copy(a, 0, sib, me).wait_recv()
            for j, chip in enumerate(chips):
                copy(a, 4 + j, (*chip, 1 - c), me).wait_recv()
        for cp in first + passed:
            cp.wait_send()
        for cp in mine:
            cp.wait()

    hbm = pl.BlockSpec(memory_space=pl.ANY)
    return pl.pallas_call(
        body,
        out_shape=[jax.ShapeDtypeStruct((N_DEV,) + s.shape, s.dtype) for s in shards],
        in_specs=[hbm] * n,
        out_specs=[hbm] * n,
        scratch_shapes=[pltpu.SemaphoreType.DMA((n, 7)),
                        pltpu.SemaphoreType.DMA((n, 7)),
                        pltpu.SemaphoreType.DMA((n,))],
        name=name,
    )(*shards)


_HBM = pl.BlockSpec(memory_space=pltpu.HBM)
_SEM = pl.BlockSpec(memory_space=pltpu.SEMAPHORE)
_EFFECT = pltpu.SideEffectType.DATAFLOW_SIDE_EFFECTING


def _exchange_copy(kind, src_ref, land_ref, send_sems, recv_sems, me, k, arriving):
    peer = _flip(me, k)
    my_id, peer_id = _dev_id(me), _dev_id(peer)
    if kind == "gather":
        src = src_ref
    else:
        src = src_ref.at[my_id if arriving else peer_id]
    return pltpu.make_async_remote_copy(
        src_ref=src, dst_ref=land_ref.at[peer_id if arriving else my_id],
        send_sem=send_sems.at[k - 1], recv_sem=recv_sems.at[k - 1],
        device_id=peer, device_id_type=MESH)


def _exchange_start(srcs, kind, name, after=None):
    n = len(srcs)
    n_after = 0 if after is None else 1
    shapes = [((N_DEV,) + s.shape) if kind == "gather" else s.shape for s in srcs]
    lands = [lax.empty(sh, s.dtype) for sh, s in zip(shapes, srcs)]

    def body(*refs):
        src_refs = refs[:n]
        land_refs = refs[n:2 * n]
        outs_at = 2 * n + n_after
        send = refs[outs_at:outs_at + n]
        recv = refs[outs_at + n:outs_at + 2 * n]
        token = refs[outs_at + 4 * n]
        me = _my_pos()
        for a in range(n):
            for k in range(1, N_DEV):
                _exchange_copy(kind, src_refs[a], land_refs[a], send[a], recv[a], me, k, False).start()
        token[...] = jnp.zeros(token.shape, token.dtype)

    sem = pltpu.SemaphoreType.DMA((N_DEV - 1,))
    outs = pl.pallas_call(
        body,
        name=name,
        out_shape=([sem] * (2 * n)
                   + [pltpu.HBM(s.shape, s.dtype) for s in srcs]
                   + [pltpu.HBM(sh, s.dtype) for sh, s in zip(shapes, srcs)]
                   + [jax.ShapeDtypeStruct((8, 128), F32)]),
        in_specs=[_HBM] * (2 * n) + [pl.BlockSpec(memory_space=pl.ANY)] * n_after,
        out_specs=[_SEM] * (2 * n) + [_HBM] * (2 * n) + [pl.BlockSpec(memory_space=pltpu.VMEM)],
        input_output_aliases={i: 2 * n + i for i in range(2 * n)},
        compiler_params=pltpu.CompilerParams(has_side_effects=_EFFECT),
    )(*[pltpu.with_memory_space_constraint(s, pltpu.HBM) for s in srcs],
      *[pltpu.with_memory_space_constraint(l, pltpu.HBM) for l in lands],
      *([after] if n_after else []))
    flights = [(outs[a], outs[n + a], outs[2 * n + a], outs[3 * n + a]) for a in range(n)]
    return flights, outs[4 * n]


def _exchange_wait(flights, kind, after, name):
    n = len(flights)

    def body(*refs):
        src_refs = refs[:n]
        land_refs = refs[n:2 * n]
        send = refs[2 * n:3 * n]
        recv = refs[3 * n:4 * n]
        local_sems = refs[4 * n + 1 + 2 * n]
        me = _my_pos()
        my_id = _dev_id(me)
        for a in range(n):
            for k in range(1, N_DEV):
                _exchange_copy(kind, src_refs[a], land_refs[a], send[a], recv[a], me, k, False).wait_send()
                _exchange_copy(kind, src_refs[a], land_refs[a], send[a], recv[a], me, k, True).wait_recv()
        mine = []
        for a in range(n):
            own = src_refs[a] if kind == "gather" else src_refs[a].at[my_id]
            cp = pltpu.make_async_copy(own, land_refs[a].at[my_id], local_sems.at[a])
            cp.start()
            mine.append(cp)
        for cp in mine:
            cp.wait()

    srcs = [f[2] for f in flights]
    lands = [f[3] for f in flights]
    outs = pl.pallas_call(
        body,
        name=name,
        out_shape=[pltpu.HBM(s.shape, s.dtype) for s in srcs]
        + [pltpu.HBM(l.shape, l.dtype) for l in lands],
        in_specs=[_HBM] * (2 * n) + [_SEM] * (2 * n) + [pl.BlockSpec(memory_space=pl.ANY)],
        out_specs=[_HBM] * (2 * n),
        input_output_aliases={i: i for i in range(2 * n)},
        scratch_shapes=[pltpu.SemaphoreType.DMA((n,))],
        compiler_params=pltpu.CompilerParams(has_side_effects=_EFFECT),
    )(*srcs, *lands, *[f[0] for f in flights], *[f[1] for f in flights], after)
    return outs[n:]


def _t5_buckets_block():
    qi = np.arange(BLK)[:, None]
    ki = np.arange(2 * BLK)[None, :]
    n = np.maximum(qi + BLK - ki, 0)
    max_exact = NUM_BUCKETS // 2
    large = max_exact + (np.log(np.maximum(n, 1) / max_exact)
                         / np.log(MAX_DISTANCE / max_exact)
                         * (NUM_BUCKETS - max_exact)).astype(np.int32)
    large = np.minimum(large, NUM_BUCKETS - 1)
    return np.where(n < max_exact, n, large).astype(np.int32)


def _band_mask():
    qi = np.arange(BLK)[:, None]
    ki = np.arange(2 * BLK)[None, :]
    dist = qi + BLK - ki
    return (dist >= 0) & (dist < BLK)


def _attn_scores(q_ref, kp_ref, kc_ref, hkv):
    c0 = hkv * HEAD_DIM
    kk = jnp.concatenate([kp_ref[:, c0:c0 + HEAD_DIM], kc_ref[:, c0:c0 + HEAD_DIM]],
                         axis=0).astype(BF16)
    qg = jnp.concatenate(
        [q_ref[:, (hkv * GROUP + g) * HEAD_DIM:(hkv * GROUP + g + 1) * HEAD_DIM]
         for g in range(GROUP)], axis=0).astype(BF16)
    s = lax.dot_general(qg, kk, (((1,), (1,)), ((), ())), preferred_element_type=F32)
    return qg, kk, s


def _attn_softmax(s, bias_ref, sink_ref, hkv):
    r0, r1 = hkv * GROUP * BLK, (hkv + 1) * GROUP * BLK
    s = s * (HEAD_DIM ** -0.5) + bias_ref[r0:r1, :]
    sink = sink_ref[r0:r1, :]
    m = jnp.maximum(jnp.max(s, axis=-1, keepdims=True), sink)
    p = jnp.exp(s - m)
    e_sink = jnp.exp(sink - m)
    inv = 1.0 / (jnp.sum(p, axis=-1, keepdims=True) + e_sink)
    return p * inv, e_sink * inv


def _kv_rows(p_ref, c_ref, hkv):
    c0 = hkv * HEAD_DIM
    return jnp.concatenate([p_ref[:, c0:c0 + HEAD_DIM], c_ref[:, c0:c0 + HEAD_DIM]],
                           axis=0).astype(BF16)


def _attn_in_specs(bias2):
    prev = lambda n: jnp.maximum(n - 1, 0)
    return [
        pl.BlockSpec((BLK, ATTN_W), lambda n: (n, 0)),
        pl.BlockSpec((BLK, KV_W), lambda n: (prev(n), ATTN_W // KV_W)),
        pl.BlockSpec((BLK, KV_W), lambda n: (n, ATTN_W // KV_W)),
        pl.BlockSpec((BLK, KV_W), lambda n: (prev(n), ATTN_W // KV_W + 1)),
        pl.BlockSpec((BLK, KV_W), lambda n: (n, ATTN_W // KV_W + 1)),
        pl.BlockSpec((None,) + bias2.shape[1:], lambda n: (jnp.minimum(n, 1), 0, 0)),
    ]


def _attention_fwd(proj, bias2, sinkcol, n_rows):
    nb = n_rows // BLK

    def body(q_ref, kp_ref, kc_ref, vp_ref, vc_ref, bias_ref, sink_ref, o_ref):
        heads = range(N_KV_HEADS)
        scores = [_attn_scores(q_ref, kp_ref, kc_ref, hkv)[2] for hkv in heads]
        probs = [_attn_softmax(scores[hkv], bias_ref, sink_ref, hkv)[0] for hkv in heads]
        outs = [jnp.dot(probs[hkv].astype(BF16), _kv_rows(vp_ref, vc_ref, hkv),
                        preferred_element_type=F32) for hkv in heads]
        for hkv in heads:
            for g in range(GROUP):
                h = hkv * GROUP + g
                o_ref[:, h * HEAD_DIM:(h + 1) * HEAD_DIM] = (
                    outs[hkv][g * BLK:(g + 1) * BLK, :].astype(o_ref.dtype))

    return pl.pallas_call(
        body,
        grid=(nb,),
        in_specs=_attn_in_specs(bias2) + [pl.BlockSpec(sinkcol.shape, lambda n: (0, 0))],
        out_specs=pl.BlockSpec((BLK, ATTN_W), lambda n: (n, 0)),
        out_shape=jax.ShapeDtypeStruct((n_rows, ATTN_W), BF16),
        compiler_params=_cparams(("parallel",)),
        name="attn_fwd",
    )(proj, proj, proj, proj, proj, bias2, sinkcol)


def _attention_bwd(proj, attn, dattn, bias2, sinkcol, n_rows):
    nb = n_rows // BLK
    scale = HEAD_DIM ** -0.5
    dn_t = (((0,), (0,)), ((), ()))

    def body(q_ref, kp_ref, kc_ref, vp_ref, vc_ref, bias_ref, o_ref, do_ref, sink_ref,
             dq_ref, dkc_ref, dkp_ref, dvc_ref, dvp_ref, dbias_ref, dsink_ref):
        @pl.when(pl.program_id(0) == 0)
        def _():
            dbias_ref[...] = jnp.zeros(dbias_ref.shape, F32)
            dsink_ref[...] = jnp.zeros(dsink_ref.shape, F32)

        heads = range(N_KV_HEADS)
        qk = [_attn_scores(q_ref, kp_ref, kc_ref, hkv) for hkv in heads]
        dog, dps, deltas = [], [], []
        for hkv in heads:
            hs = [hkv * GROUP + g for g in range(GROUP)]
            d_o = jnp.concatenate([do_ref[:, h * HEAD_DIM:(h + 1) * HEAD_DIM] for h in hs], axis=0)
            o = jnp.concatenate([o_ref[:, h * HEAD_DIM:(h + 1) * HEAD_DIM] for h in hs], axis=0)
            deltas.append(jnp.sum(d_o.astype(F32) * o.astype(F32), axis=-1, keepdims=True))
            dog.append(d_o.astype(BF16))
            dps.append(lax.dot_general(dog[hkv], _kv_rows(vp_ref, vc_ref, hkv),
                                       (((1,), (1,)), ((), ())), preferred_element_type=F32))
        p16, ds16 = [], []
        for hkv in heads:
            r0, r1 = hkv * GROUP * BLK, (hkv + 1) * GROUP * BLK
            p, p_sink = _attn_softmax(qk[hkv][2], bias_ref, sink_ref, hkv)
            ds = p * (dps[hkv] - deltas[hkv])
            dbias_ref[r0:r1, :] += ds
            dsink_ref[r0:r1, :] += -(p_sink * deltas[hkv])
            p16.append(p.astype(BF16))
            ds16.append(ds.astype(BF16))
        for hkv in heads:
            c0 = hkv * HEAD_DIM
            qg, kk, _ = qk[hkv]
            dqg = jnp.dot(ds16[hkv], kk, preferred_element_type=F32) * scale
            dkk = lax.dot_general(ds16[hkv], qg, dn_t, preferred_element_type=F32) * scale
            dvv = lax.dot_general(p16[hkv], dog[hkv], dn_t, preferred_element_type=F32)
            for g in range(GROUP):
                h = hkv * GROUP + g
                dq_ref[:, h * HEAD_DIM:(h + 1) * HEAD_DIM] = (
                    dqg[g * BLK:(g + 1) * BLK, :].astype(dq_ref.dtype))
            dkp_ref[:, c0:c0 + HEAD_DIM] = dkk[:BLK].astype(dkp_ref.dtype)
            dkc_ref[:, c0:c0 + HEAD_DIM] = dkk[BLK:].astype(dkc_ref.dtype)
            dvp_ref[:, c0:c0 + HEAD_DIM] = dvv[:BLK].astype(dvp_ref.dtype)
            dvc_ref[:, c0:c0 + HEAD_DIM] = dvv[BLK:].astype(dvc_ref.dtype)

    kv_out = pl.BlockSpec((BLK, KV_W), lambda n: (n, 0))
    kv_shape = jax.ShapeDtypeStruct((n_rows, KV_W), F32)
    acc_shape = bias2.shape[1:]
    return pl.pallas_call(
        body,
        grid=(nb,),
        in_specs=_attn_in_specs(bias2) + [
            pl.BlockSpec((BLK, ATTN_W), lambda n: (n, 0)),
            pl.BlockSpec((BLK, ATTN_W), lambda n: (n, 0)),
            pl.BlockSpec(sinkcol.shape, lambda n: (0, 0)),
        ],
        out_specs=[
            pl.BlockSpec((BLK, ATTN_W), lambda n: (n, 0)),
            kv_out, kv_out, kv_out, kv_out,
            pl.BlockSpec(acc_shape, lambda n: (0, 0)),
            pl.BlockSpec(sinkcol.shape, lambda n: (0, 0)),
        ],
        out_shape=[
            jax.ShapeDtypeStruct((n_rows, ATTN_W), BF16),
            kv_shape, kv_shape, kv_shape, kv_shape,
            jax.ShapeDtypeStruct(acc_shape, F32),
            jax.ShapeDtypeStruct(sinkcol.shape, F32),
        ],
        compiler_params=_cparams(("arbitrary",)),
        name="attn_bwd",
    )(proj, proj, proj, proj, proj, bias2, attn, dattn, sinkcol)


def _bias_tables(rel_bias_t, onehot_t, band_first, band_rest):
    def body(rb_ref, oh_ref, mf_ref, mr_ref, out_ref):
        acc = jnp.zeros((N_Q_HEADS, BLK * 2 * BLK), F32)
        for part in _split3(rb_ref[...]):
            acc = acc + jnp.dot(part, oh_ref[...], preferred_element_type=F32)
        out_ref[0] = jnp.where(mf_ref[...] > 0.0, acc, NEG_INF)
        out_ref[1] = jnp.where(mr_ref[...] > 0.0, acc, NEG_INF)

    return pl.pallas_call(
        body,
        out_shape=jax.ShapeDtypeStruct((2, N_Q_HEADS, BLK * 2 * BLK), F32),
        compiler_params=pltpu.CompilerParams(vmem_limit_bytes=VMEM_LIMIT),
        name="bias_tables",
    )(rel_bias_t, onehot_t, band_first, band_rest)


def _split3(a):
    hi = a.astype(BF16)
    r1 = a - hi.astype(F32)
    mid = r1.astype(BF16)
    lo = (r1 - mid.astype(F32)).astype(BF16)
    return hi, mid, lo


def _bucket_reduce(dbias, dsink, onehot_t):
    def body(db_ref, ds_ref, oh_ref, ob_ref, os_ref):
        acc = jnp.zeros((N_Q_HEADS, 128), F32)
        for part in _split3(db_ref[...]):
            acc = acc + lax.dot_general(part, oh_ref[...], (((1,), (1,)), ((), ())),
                                        preferred_element_type=F32)
        ob_ref[...] = acc
        os_ref[...] = jnp.broadcast_to(jnp.sum(ds_ref[...], axis=-1, keepdims=True),
                                       os_ref.shape)

    return pl.pallas_call(
        body,
        out_shape=[jax.ShapeDtypeStruct((N_Q_HEADS, 128), F32),
                   jax.ShapeDtypeStruct((N_Q_HEADS, 128), F32)],
        compiler_params=pltpu.CompilerParams(vmem_limit_bytes=VMEM_LIMIT),
        name="bias_bucket_reduce",
    )(dbias, dsink, onehot_t)


def _disc(lr, li, ls, btr, bti):
    lam_re = jnp.minimum(lr, -1e-4)
    delta = jnp.exp(ls)
    mag = jnp.exp(lam_re * delta)
    ang = li * delta
    ar, ai = mag * jnp.cos(ang), mag * jnp.sin(ang)
    nr, ni = ar - 1.0, ai
    den = lam_re * lam_re + li * li
    fr = (nr * lam_re + ni * li) / den
    fi = (ni * lam_re - nr * li) / den
    bbr = fr * btr - fi * bti
    bbi = fr * bti + fi * btr
    return ar, ai, bbr, bbi


def _block_mask():
    row = lax.broadcasted_iota(jnp.int32, (SSM_W, SSM_H), 0)
    col = lax.broadcasted_iota(jnp.int32, (SSM_W, SSM_H), 1)
    return (row // SSM_P) == (col // SSM_N)


def _ssm_setup(lr, li, ls, btr, bti, ctr, cti):
    def body(lr_ref, li_ref, ls_ref, btr_ref, bti_ref, ctr_ref, cti_ref, a_ref, b_ref, c_ref):
        ar, ai, bbr, bbi = _disc(lr_ref[...], li_ref[...], ls_ref[...], btr_ref[...], bti_ref[...])
        a_ref[:, :SSM_H] = ar
        a_ref[:, SSM_H:] = ai
        mask = _block_mask()
        blk = lambda t: jnp.where(mask, jnp.tile(t, (SSM_G, 1)), 0.0)
        b_ref[:, :SSM_H] = blk(bbr).astype(BF16)
        b_ref[:, SSM_H:] = blk(bbi).astype(BF16)
        c_ref[:, :SSM_H] = blk(ctr_ref[...]).astype(BF16)
        c_ref[:, SSM_H:] = blk(-cti_ref[...]).astype(BF16)

    return pl.pallas_call(
        body,
        out_shape=[jax.ShapeDtypeStruct((1, 2 * SSM_H), F32),
                   jax.ShapeDtypeStruct((SSM_W, 2 * SSM_H), BF16),
                   jax.ShapeDtypeStruct((SSM_W, 2 * SSM_H), BF16)],
        compiler_params=pltpu.CompilerParams(vmem_limit_bytes=VMEM_LIMIT),
        name="ssm_setup",
    )(lr, li, ls, btr, bti, ctr, cti)


def _ssm_param_bwd(lr, li, ls, btr, bti, dacc, dbcat, dccat, gind):
    def body(lr_ref, li_ref, ls_ref, btr_ref, bti_ref, dacc_ref, db_ref, dc_ref, g_ref,
             dlr_ref, dli_ref, dls_ref, dbtr_ref, dbti_ref, dctr_ref, dcti_ref):
        dar = jnp.sum(dacc_ref[:, :SSM_H], axis=0, keepdims=True)
        dai = jnp.sum(dacc_ref[:, SSM_H:], axis=0, keepdims=True)
        col = lax.broadcasted_iota(jnp.int32, (SSM_P, 2 * SSM_H), 1)
        grp = (col % SSM_H) // SSM_N
        db = jnp.zeros((SSM_P, 2 * SSM_H), F32)
        dc = jnp.zeros((SSM_P, 2 * SSM_H), F32)
        for g in range(SSM_G):
            sel = grp == g
            db = db + jnp.where(sel, db_ref[g * SSM_P:(g + 1) * SSM_P, :], 0.0)
            dc = dc + jnp.where(sel, dc_ref[g * SSM_P:(g + 1) * SSM_P, :], 0.0)
        dctr_ref[...] = dc[:, :SSM_H]
        dcti_ref[...] = -dc[:, SSM_H:]
        prim = (lr_ref[...], li_ref[...], ls_ref[...], btr_ref[...], bti_ref[...])
        _, vjp = jax.vjp(_disc, *prim)
        dlr, dli, dls, dbtr, dbti = vjp((dar, dai, db[:, :SSM_H], db[:, SSM_H:]))
        dlr_ref[...] = dlr
        dli_ref[...] = dli
        dbtr_ref[...] = dbtr
        dbti_ref[...] = dbti
        acc = jnp.zeros((8, 128), F32)
        for part in _split3(jnp.broadcast_to(dls, (8, SSM_H))):
            acc = acc + jnp.dot(part, g_ref[...], preferred_element_type=F32)
        dls_ref[...] = acc

    vec = jax.ShapeDtypeStruct((1, SSM_H), F32)
    mat = jax.ShapeDtypeStruct((SSM_P, SSM_H), F32)
    return pl.pallas_call(
        body,
        out_shape=[vec, vec, jax.ShapeDtypeStruct((8, 128), F32), mat, mat, mat, mat],
        compiler_params=pltpu.CompilerParams(vmem_limit_bytes=VMEM_LIMIT),
        name="ssm_param_bwd",
    )(lr, li, ls, btr, bti, dacc, dbcat, dccat, gind)


SCAN_TR = 256


def _cmul_add(vr, vi, pr, pi, sr, si):
    return vr + pr * sr - pi * si, vi + pr * si + pi * sr


def _bcast_row(v, row, which):
    b = jnp.where(row == which, v, 0.0)
    b = b + pltpu.roll(b, 4, 0)
    b = b + pltpu.roll(b, 2, 0)
    return b + pltpu.roll(b, 1, 0)


def _scan_tables(a_ref, tab_ref, reverse):
    H = SSM_H
    ar = jnp.broadcast_to(a_ref[:, :H], (8, H))
    ai = jnp.broadcast_to(a_ref[:, H:], (8, H))
    if reverse:
        ai = -ai
    row = lax.broadcasted_iota(jnp.int32, (8, H), 0)
    pw = [(ar, ai)]
    for _ in range(7):
        cr, ci = pw[-1]
        pw.append((cr * ar - ci * ai, cr * ai + ci * ar))
    pcr = jnp.zeros((8, H), F32)
    pci = jnp.zeros((8, H), F32)
    for e in range(8):
        sel = (row == (7 - e)) if reverse else (row == e)
        pcr = jnp.where(sel, pw[e][0], pcr)
        pci = jnp.where(sel, pw[e][1], pci)
    tab_ref[0, :, :H] = pcr
    tab_ref[0, :, H:] = pci
    for t, k in enumerate((1, 2, 4)):
        keep = (row < 8 - k) if reverse else (row >= k)
        tab_ref[1 + t, :, :H] = jnp.where(keep, pw[k - 1][0], 0.0)
        tab_ref[1 + t, :, H:] = jnp.where(keep, pw[k - 1][1], 0.0)


def _scan_group(vr, vi, cr, ci, tab_ref, reverse):
    H = SSM_H
    for t, k in enumerate((1, 2, 4)):
        sh = 8 - k if reverse else k
        vr, vi = _cmul_add(vr, vi, tab_ref[1 + t, :, :H], tab_ref[1 + t, :, H:],
                           pltpu.roll(vr, sh, 0), pltpu.roll(vi, sh, 0))
    return _cmul_add(vr, vi, tab_ref[0, :, :H], tab_ref[0, :, H:], cr, ci)


def _scan_fwd(bu, abar, n_rows):
    H = SSM_H
    nt = n_rows // SCAN_TR

    def body(bu_ref, a_ref, xs_ref, xp_ref, tab_ref, carry_ref):
        @pl.when(pl.program_id(0) == 0)
        def _():
            _scan_tables(a_ref, tab_ref, False)
            carry_ref[...] = jnp.zeros(carry_ref.shape, F32)

        row = lax.broadcasted_iota(jnp.int32, (8, H), 0)

        def group(j, carry):
            cr, ci = carry
            r0 = pl.multiple_of(j * 16, 16)
            xr, xi = [], []
            for half in range(2):
                rr = pl.multiple_of(r0 + 8 * half, 8)
                vr, vi = _scan_group(bu_ref[pl.ds(rr, 8), :H], bu_ref[pl.ds(rr, 8), H:],
                                     cr, ci, tab_ref, False)
                xp_ref[pl.ds(rr, 8), :H] = jnp.where(row == 0, cr, pltpu.roll(vr, 1, 0))
                xp_ref[pl.ds(rr, 8), H:] = jnp.where(row == 0, ci, pltpu.roll(vi, 1, 0))
                cr, ci = _bcast_row(vr, row, 7), _bcast_row(vi, row, 7)
                xr.append(vr)
                xi.append(vi)
            xs_ref[pl.ds(r0, 16), :H] = jnp.concatenate(xr, axis=0).astype(BF16)
            xs_ref[pl.ds(r0, 16), H:] = jnp.concatenate(xi, axis=0).astype(BF16)
            return cr, ci

        cr, ci = lax.fori_loop(0, SCAN_TR // 16, group,
                               (carry_ref[:, :H], carry_ref[:, H:]))
        carry_ref[:, :H] = cr
        carry_ref[:, H:] = ci

    return pl.pallas_call(
        body,
        grid=(nt,),
        in_specs=[pl.BlockSpec((SCAN_TR, 2 * H), lambda i: (i, 0)),
                  pl.BlockSpec((1, 2 * H), lambda i: (0, 0))],
        out_specs=[pl.BlockSpec((SCAN_TR, 2 * H), lambda i: (i, 0)),
                   pl.BlockSpec((SCAN_TR, 2 * H), lambda i: (i, 0))],
        out_shape=[jax.ShapeDtypeStruct((n_rows, 2 * H), BF16),
                   jax.ShapeDtypeStruct((n_rows, 2 * H), F32)],
        scratch_shapes=[pltpu.VMEM((4, 8, 2 * H), F32), pltpu.VMEM((8, 2 * H), F32)],
        compiler_params=_cparams(("arbitrary",)),
        name="ssm_scan_fwd",
    )(bu, abar)


def _scan_bwd(gx, xprev, abar, n_rows):
    H = SSM_H
    nt = n_rows // SCAN_TR

    def body(g_ref, xp_ref, a_ref, h_ref, da_ref, tab_ref, carry_ref):
        @pl.when(pl.program_id(0) == 0)
        def _():
            _scan_tables(a_ref, tab_ref, True)
            carry_ref[...] = jnp.zeros(carry_ref.shape, F32)
            da_ref[...] = jnp.zeros(da_ref.shape, F32)

        row = lax.broadcasted_iota(jnp.int32, (8, H), 0)
        n16 = SCAN_TR // 16

        def group(jj, carry):
            cr, ci = carry
            r0 = pl.multiple_of((n16 - 1 - jj) * 16, 16)
            hr, hi = [None, None], [None, None]
            for half in (1, 0):
                rr = pl.multiple_of(r0 + 8 * half, 8)
                vr, vi = _scan_group(g_ref[pl.ds(rr, 8), :H], g_ref[pl.ds(rr, 8), H:],
                                     cr, ci, tab_ref, True)
                pr, pi = xp_ref[pl.ds(rr, 8), :H], xp_ref[pl.ds(rr, 8), H:]
                da_ref[:, :H] += vr * pr + vi * pi
                da_ref[:, H:] += vi * pr - vr * pi
                cr, ci = _bcast_row(vr, row, 0), _bcast_row(vi, row, 0)
                hr[half], hi[half] = vr, vi
            h_ref[pl.ds(r0, 16), :H] = jnp.concatenate(hr, axis=0).astype(BF16)
            h_ref[pl.ds(r0, 16), H:] = jnp.concatenate(hi, axis=0).astype(BF16)
            return cr, ci

        cr, ci = lax.fori_loop(0, n16, group, (carry_ref[:, :H], carry_ref[:, H:]))
        carry_ref[:, :H] = cr
        carry_ref[:, H:] = ci

    rev = lambda i: (nt - 1 - i, 0)
    return pl.pallas_call(
        body,
        grid=(nt,),
        in_specs=[pl.BlockSpec((SCAN_TR, 2 * H), rev),
                  pl.BlockSpec((SCAN_TR, 2 * H), rev),
                  pl.BlockSpec((1, 2 * H), lambda i: (0, 0))],
        out_specs=[pl.BlockSpec((SCAN_TR, 2 * H), rev),
                   pl.BlockSpec((8, 2 * H), lambda i: (0, 0))],
        out_shape=[jax.ShapeDtypeStruct((n_rows, 2 * H), BF16),
                   jax.ShapeDtypeStruct((8, 2 * H), F32)],
        scratch_shapes=[pltpu.VMEM((4, 8, 2 * H), F32), pltpu.VMEM((8, 2 * H), F32)],
        compiler_params=_cparams(("arbitrary",)),
        name="ssm_scan_bwd",
    )(gx, xprev, abar)


def _adamw(parts, w, m, v, *, tr, ch, name):
    n_rows, cols = w.shape
    n_parts = len(parts)
    c1 = 1.0 - ADAM_B1 ** ADAM_STEP
    c2 = 1.0 - ADAM_B2 ** ADAM_STEP

    def fn(rv, vv, i, nt):
        g = rv[0].astype(F32)
        for p in rv[1:n_parts]:
            g = g + p.astype(F32)
        wv, mv, vval = rv[n_parts:]
        nm = ADAM_B1 * mv + (1.0 - ADAM_B1) * g
        nv = ADAM_B2 * vval + (1.0 - ADAM_B2) * (g * g)
        delta = -ADAM_LR * ((nm / c1) / (jnp.sqrt(nv / c2) + ADAM_EPS) + ADAM_WD * wv)
        return [g, delta, nm, nv], []

    rows = [_row(arr, lead=lead) for (arr, lead) in parts] + [_row(w), _row(m), _row(v)]
    return _rowwise(fn, rows, [], [(cols, F32)] * 4, [], n_rows=n_rows, tr=tr, ch=ch, name=name)


_PACK = [
    ("b_ada", 6), ("norm1_g", 1), ("b_in", 3), ("norm2_g", 1), ("final_g", 1),
    ("lambda_re", 1), ("lambda_im", 1), ("log_step", 1), ("attn_sinks", 1),
    ("rel_bias", 1), ("b_glu", 1), ("ssm_d", 1), ("loss", 1),
    ("ssm_b_re", 16), ("ssm_b_im", 16), ("ssm_c_re", 16), ("ssm_c_im", 16),
]
_PACK_OFF = {}
_off = 0
for _n, _r in _PACK:
    _PACK_OFF[_n] = (_off, _r)
    _off += _r
PACK_ROWS = -(-_off // 8) * 8


def _to_rows(a, rows):
    flat = a.reshape(-1).astype(F32)
    pad = rows * PACK_W - flat.shape[0]
    if pad:
        flat = jnp.pad(flat, (0, pad))
    return flat.reshape(rows, PACK_W)


def _b_to_rows(b):
    return jnp.transpose(b, (2, 0, 1)).reshape(SSM_P, SSM_H)


def _rows_to_b(r):
    return jnp.transpose(r.reshape(SSM_P, SSM_G, SSM_N), (1, 2, 0))


def _c_to_rows(cm):
    return jnp.transpose(cm, (1, 0, 2)).reshape(SSM_P, SSM_H)


def _rows_to_c(r):
    return jnp.transpose(r.reshape(SSM_P, SSM_G, SSM_N), (1, 0, 2))


def _pack(vals):
    out = jnp.zeros((PACK_ROWS, PACK_W), F32)
    for n, r in _PACK:
        if n in vals:
            pieces = vals[n] if isinstance(vals[n], list) else [vals[n]]
            rows_each = r // len(pieces)
            for i, piece in enumerate(pieces):
                out = lax.dynamic_update_slice(out, _to_rows(piece, rows_each),
                                               (_PACK_OFF[n][0] + i * rows_each, 0))
    return out


def _unpack(packed, name, shape):
    o, r = _PACK_OFF[name]
    n = int(np.prod(shape))
    return packed[o:o + r].reshape(-1)[:n].reshape(shape)


def _small_params_packed(p):
    return {
        "b_ada": p["b_ada"], "norm1_g": p["norm1_g"], "b_in": p["b_in"],
        "norm2_g": p["norm2_g"], "final_g": p["final_g"],
        "lambda_re": p["lambda_re"], "lambda_im": p["lambda_im"],
        "log_step": p["log_step"], "attn_sinks": p["attn_sinks"],
        "rel_bias": p["rel_bias"], "b_glu": p["b_glu"], "ssm_d": p["ssm_d"],
        "ssm_b_re": _b_to_rows(p["ssm_b_re"][0]), "ssm_b_im": _b_to_rows(p["ssm_b_im"][0]),
        "ssm_c_re": _c_to_rows(p["ssm_c_re"][0]), "ssm_c_im": _c_to_rows(p["ssm_c_im"][0]),
    }


_SMALL_SHAPES = {
    "b_ada": (1, N_MOD * D), "norm1_g": (1, D), "b_in": (1, IN_W), "norm2_g": (1, D),
    "final_g": (D,), "lambda_re": (1, SSM_G, SSM_N), "lambda_im": (1, SSM_G, SSM_N),
    "log_step": (1, SSM_G), "attn_sinks": (1, N_Q_HEADS), "rel_bias": (NUM_BUCKETS, N_Q_HEADS),
    "b_glu": (1, SSM_W), "ssm_d": (1, SSM_W),
}


def _unpack_small(packed, name):
    if name in ("ssm_b_re", "ssm_b_im"):
        o, r = _PACK_OFF[name]
        return _rows_to_b(packed[o:o + r])[None]
    if name in ("ssm_c_re", "ssm_c_im"):
        o, r = _PACK_OFF[name]
        return _rows_to_c(packed[o:o + r])[None]
    return _unpack(packed, name, _SMALL_SHAPES[name])


WEIGHT_ORDER = ['w_ada', 'b_ada', 'norm1_g', 'w_in', 'b_in', 'attn_sinks', 'rel_bias', 'lambda_re',
                'lambda_im', 'log_step', 'ssm_b_re', 'ssm_b_im', 'ssm_c_re', 'ssm_c_im', 'ssm_d',
                'w_glu', 'b_glu', 'w_attn_proj', 'w_ssm_proj', 'w_out', 'norm2_g', 'w_ff1', 'w_ff2',
                'final_g']
BIG = ['w_in', 'w_glu', 'w_attn_proj', 'w_ssm_proj', 'w_out', 'w_ff1', 'w_ff2']


ADAMW_TILE_ELEMS = 1 << 18


def _adamw_rows(rows, cols):
    tr = rows
    while tr * cols > ADAMW_TILE_ELEMS and tr % 32 == 0:
        tr //= 2
    return tr


def _cast_bf16(w, name):
    rows, cols = w.shape
    tr = min(rows, 256)
    return _rowwise(lambda rv, vv, i, nt: ([rv[0]], []), [_row(w)], [], [(cols, BF16)], [],
                    n_rows=rows, tr=tr, ch=min(tr, 32), name=name)[0]


def kernel(x, c, w_ada, b_ada, norm1_g, w_in, b_in, attn_sinks, rel_bias, lambda_re, lambda_im, log_step, ssm_b_re, ssm_b_im, ssm_c_re, ssm_c_im, ssm_d, w_glu, b_glu, w_attn_proj, w_ssm_proj, w_out, norm2_g, w_ff1, w_ff2, final_g, loss_target, m_w_ada, m_b_ada, m_norm1_g, m_w_in, m_b_in, m_attn_sinks, m_rel_bias, m_lambda_re, m_lambda_im, m_log_step, m_ssm_b_re, m_ssm_b_im, m_ssm_c_re, m_ssm_c_im, m_ssm_d, m_w_glu, m_b_glu, m_w_attn_proj, m_w_ssm_proj, m_w_out, m_norm2_g, m_w_ff1, m_w_ff2, m_final_g, v_w_ada, v_b_ada, v_norm1_g, v_w_in, v_b_in, v_attn_sinks, v_rel_bias, v_lambda_re, v_lambda_im, v_log_step, v_ssm_b_re, v_ssm_b_im, v_ssm_c_re, v_ssm_c_im, v_ssm_d, v_w_glu, v_b_glu, v_w_attn_proj, v_w_ssm_proj, v_w_out, v_norm2_g, v_w_ff1, v_w_ff2, v_final_g):
    loc = dict(locals())
    W = {n: loc[n] for n in WEIGHT_ORDER}
    Mo = {n: loc["m_" + n] for n in WEIGHT_ORDER}
    Vo = {n: loc["v_" + n] for n in WEIGHT_ORDER}
    S = x.shape[1]
    TM = min(512, S)
    TS = min(1024, S)
    TR = min(256, S)
    me = 4 * lax.axis_index("x") + 2 * lax.axis_index("y") + lax.axis_index("c")
    x2d = x.reshape(S, D)
    tgt = loss_target.reshape(S, D)

    shard = {n: W[n][0] for n in BIG}
    w16 = {n: _cast_bf16(shard[n], "cast_" + n) for n in BIG}
    G = {"w_in": _allgather_weights([w16["w_in"]], "allgather_w_in")[0]}
    later = [n for n in BIG if n != "w_in"]
    flights, tok_w = _exchange_start([w16[n] for n in later], "gather", "weights_start", G["w_in"])
    w_flight = dict(zip(later, flights))

    c_all = _small_allgather(c, "allgather_c").reshape(N_DEV, D)
    cs = _rowwise(lambda rv, vv, i, nt: ([rv[0] * _sigmoid(rv[0])], []), [_row(c_all)], [],
                  [(D, F32)], [], n_rows=N_DEV, tr=8, ch=8, name="silu_c")[0]
    n_ada = N_MOD * D // N_DEV
    b_ada_cols = lax.dynamic_slice(b_ada, (0, me * n_ada), (1, n_ada))
    mod_piece = _matmul(cs, w_ada[0], mode="nn", dims=(N_DEV, n_ada, D), tiles=(N_DEV, 512, D),
                        out_dtypes=[F32], name="ada_fwd", bias=b_ada_cols)
    mod_all = _small_allgather(mod_piece, "allgather_mod")
    mod_b = lax.dynamic_index_in_dim(mod_all, me, axis=1, keepdims=False).reshape(N_MOD, D)
    sh1, sc1, g1, sh2, sc2, g2 = [mod_b[i:i + 1] for i in range(N_MOD)]

    def f_norm1(rv, vv, i, nt):
        xv, (g, sc, sh) = rv[0], vv
        return [(xv * _rms(xv) * g) * (1.0 + sc) + sh], []

    h = _rowwise(f_norm1, [_row(x2d)], [norm1_g, sc1, sh1], [(D, BF16)], [],
                 n_rows=S, tr=TR, ch=32, name="norm1_fwd")[0]
    proj = _matmul(h, G["w_in"], mode="nn", dims=(S, IN_W, D), tiles=(TM, 768, D),
                   out_dtypes=[F32], name="in_proj", b3=True, bias=b_in, dep=tok_w)

    buckets = _t5_buckets_block()
    band = _band_mask()
    onehot_t = jnp.asarray(
        (np.arange(128)[:, None] == buckets.reshape(-1)[None, :]).astype(np.float32), BF16)
    band_first = band & (np.arange(2 * BLK)[None, :] >= BLK)
    rel_bias_t = jnp.pad(jnp.transpose(rel_bias), ((0, 0), (0, 128 - NUM_BUCKETS)))
    bias2 = _bias_tables(rel_bias_t, onehot_t,
                         jnp.asarray(band_first.reshape(1, -1).astype(np.float32)),
                         jnp.asarray(band.reshape(1, -1).astype(np.float32))
                         ).reshape(2, N_Q_HEADS * BLK, 2 * BLK)
    sinkcol = jnp.repeat(attn_sinks.reshape(N_Q_HEADS), BLK).reshape(N_Q_HEADS * BLK, 1)
    attn = _attention_fwd(proj, bias2, sinkcol, S)
    mixer = ["w_attn_proj", "w_glu", "w_ssm_proj", "w_out"]
    G.update(zip(mixer, _exchange_wait([w_flight[n] for n in mixer], "gather", attn, "weights_wait_mixer")))
    w_glu_f = G["w_glu"].reshape(SSM_W, SSM_W)
    w_out_f = G["w_out"].reshape(D, D)
    y_attn = _matmul(attn, G["w_attn_proj"], mode="nn", dims=(S, D, ATTN_W), tiles=(TM, 256, ATTN_W),
                     out_dtypes=[F32], name="attn_proj", b3=True)

    lam_re = lambda_re.reshape(1, SSM_H)
    lam_im = lambda_im.reshape(1, SSM_H)
    ls_x = jnp.repeat(log_step.reshape(SSM_G), SSM_N).reshape(1, SSM_H)
    btr, bti = _b_to_rows(ssm_b_re[0]), _b_to_rows(ssm_b_im[0])
    ctr, cti = _c_to_rows(ssm_c_re[0]), _c_to_rows(ssm_c_im[0])
    abar, bcat, ccat = _ssm_setup(lam_re, lam_im, ls_x, btr, bti, ctr, cti)
    u_blk = (ATTN_W + 2 * KV_W) // SSM_W
    bu = _matmul(proj, bcat, mode="nn", dims=(S, 2 * SSM_H, SSM_W), tiles=(TM, 2048, SSM_W),
                 out_dtypes=[F32], name="ssm_bu", a_off=u_blk)
    xs, xprev = _scan_fwd(bu, abar, S)
    yc = _matmul(xs, ccat, mode="nt", dims=(S, SSM_W, 2 * SSM_H), tiles=(TM, SSM_W, 2 * SSM_H),
                 out_dtypes=[F32], name="ssm_cx")

    def f_ssm_out(rv, vv, i, nt):
        y = rv[0] + vv[0] * rv[1]
        return [y, _gelu(y)], []

    y_ssm_pre, z = _rowwise(f_ssm_out, [_row(yc), _row(proj, u_blk, SSM_W)], [ssm_d],
                            [(SSM_W, F32), (SSM_W, BF16)], [], n_rows=S, tr=TM, ch=32, name="ssm_out")
    zg = _matmul(z, w_glu_f, mode="nn", dims=(S, SSM_W, SSM_W), tiles=(TM, SSM_W, SSM_W),
                 out_dtypes=[F32], name="glu_proj", bias=b_glu)
    z2 = _rowwise(lambda rv, vv, i, nt: ([rv[0].astype(F32) * _sigmoid(rv[1])], []),
                  [_row(z), _row(zg)], [], [(SSM_W, BF16)], [], n_rows=S, tr=TM, ch=32, name="glu_gate")[0]
    y_ssm = _matmul(z2, G["w_ssm_proj"], mode="nn", dims=(S, D, SSM_W), tiles=(TM, 256, SSM_W),
                    out_dtypes=[F32], name="ssm_proj", b3=True)

    ga_row = _row(proj, 1, D)
    gs_row = _row(proj, 2, D)

    def f_merge(rv, vv, i, nt):
        ga, gs, ya, ys = rv
        return [_sigmoid(ga) * ya + _sigmoid(gs) * ys], []

    merged = _rowwise(f_merge, [ga_row, gs_row, _row(y_attn), _row(y_ssm)], [], [(D, BF16)], [],
                      n_rows=S, tr=TR, ch=32, name="merge")[0]
    mo = _matmul(merged, w_out_f, mode="nn", dims=(S, D, D), tiles=(TM, 1024, D),
                 out_dtypes=[F32], name="out_proj")

    def f_norm2(rv, vv, i, nt):
        xv, mv = rv
        g1v, g, sc, sh = vv
        x1v = xv + g1v * mv
        return [x1v, (x1v * _rms(x1v) * g) * (1.0 + sc) + sh], []

    x1, h2 = _rowwise(f_norm2, [_row(x2d), _row(mo)], [g1, norm2_g, sc2, sh2],
                      [(D, F32), (D, BF16)], [], n_rows=S, tr=TR, ch=32, name="norm2_fwd")

    def relu_sq(acc):
        r = jnp.maximum(acc, 0.0)
        return r * r, r

    G["w_ff1"] = _exchange_wait([w_flight["w_ff1"]], "gather", h2, "weights_wait_ff1")[0]
    act, relu = _matmul(h2, G["w_ff1"], mode="nn", dims=(S, D_FF, D), tiles=(TM, 1024, D),
                        out_dtypes=[BF16, BF16], name="ff1", b3=True, epilogue=relu_sq)
    w_ff2_f = _exchange_wait([w_flight["w_ff2"]], "gather", act, "weights_wait_ff2")[0].reshape(D_FF, D)
    ff = _matmul(act, w_ff2_f, mode="nn", dims=(S, D, D_FF), tiles=(TM, 1024, 2048),
                 out_dtypes=[F32], name="ff2")

    def f_loss(rv, vv, i, nt):
        x1v, ffv, tv = rv
        g2v, gf = vv
        x2v = x1v + g2v * ffv
        r = _rms(x2v)
        xh = x2v * r
        diff = xh * gf - tv
        dy = diff * (1.0 / D)
        dxh = dy * gf
        dx2 = r * (dxh - xh * jnp.mean(dxh * xh, axis=-1, keepdims=True))
        return [dx2, dx2 * g2v], [_colsum(0.5 * diff * diff * (1.0 / D)), _colsum(dy * xh),
                                  _colsum(dx2 * ffv)]

    dx2, dff, loss_cols, d_final_g, dg2 = _rowwise(
        f_loss, [_row(x1), _row(ff), _row(tgt)], [g2, final_g.reshape(1, D)],
        [(D, F32), (D, BF16)], [(1, D)] * 3, n_rows=S, tr=TR, ch=32, name="loss_bwd")

    df1 = _matmul(dff, w_ff2_f, mode="nt", dims=(S, D_FF, D), tiles=(TM, 1024, D),
                  out_dtypes=[BF16], name="ff2_dgrad", extras=(relu,),
                  epilogue=lambda acc, r: (acc * (2.0 * r.astype(F32)),))
    gw_ff2 = _matmul(act, dff, mode="tn", dims=(D_FF, D, S), tiles=(1024, 1024, TS),
                     out_dtypes=[BF16], name="ff2_wgrad").reshape(N_DEV, D_FF // N_DEV, D)
    g_flight = {}
    (g_flight["w_ff2"],), tok = _exchange_start([gw_ff2], "scatter", "grads_start_ff2")
    dh2 = _matmul(df1, G["w_ff1"], mode="nt", dims=(S, D, D_FF), tiles=(TM, D, 1024),
                  out_dtypes=[F32], name="ff1_dgrad", b3=True, dep=tok)
    gw_ff1 = _matmul(h2, df1, mode="tn", dims=(D, D_FF, S), tiles=(1024, 1024, TS),
                     out_dtypes=[BF16], name="ff1_wgrad", out3=True)
    (g_flight["w_ff1"],), tok = _exchange_start([gw_ff1], "scatter", "grads_start_ff1")

    def f_norm2_bwd(rv, vv, i, nt):
        x1v, dh, dx2v, mv = rv
        g, sc, g1v = vv
        r = _rms(x1v)
        xh = x1v * r
        t = xh * g
        dt = dh * (1.0 + sc)
        dxh = dt * g
        dx1 = dx2v + r * (dxh - xh * jnp.mean(dxh * xh, axis=-1, keepdims=True))
        return [dx1, dx1 * g1v], [_colsum(dh), _colsum(dh * t), _colsum(dt * xh), _colsum(dx1 * mv)]

    dx1, dmo, dsh2, dsc2, d_norm2_g, dg1 = _rowwise(
        f_norm2_bwd, [_row(x1), _row(dh2), _row(dx2), _row(mo)], [norm2_g, sc2, g1],
        [(D, F32), (D, BF16)], [(1, D)] * 4, n_rows=S, tr=TR, ch=16, name="norm2_bwd", dep=tok)

    dmerged = _matmul(dmo, w_out_f, mode="nt", dims=(S, D, D), tiles=(TM, 1024, D),
                      out_dtypes=[F32], name="out_dgrad")
    gw_out = _matmul(merged, dmo, mode="tn", dims=(D, D, S), tiles=(1024, 1024, TS),
                     out_dtypes=[BF16], name="out_wgrad").reshape(N_DEV, D // N_DEV, D)
    (g_flight["w_out"],), tok = _exchange_start([gw_out], "scatter", "grads_start_out")

    def f_merge_bwd(rv, vv, i, nt):
        dm, ga, gs, ya, ys = rv
        sa, ss = _sigmoid(ga), _sigmoid(gs)
        return [dm * sa, dm * ss, dm * ya * sa * (1.0 - sa), dm * ys * ss * (1.0 - ss)], []

    dy_attn, dy_ssm, dga, dgs = _rowwise(
        f_merge_bwd, [_row(dmerged), ga_row, gs_row, _row(y_attn), _row(y_ssm)], [],
        [(D, BF16)] * 4, [], n_rows=S, tr=TR, ch=16, name="merge_bwd", dep=tok)

    dz2 = _matmul(dy_ssm, G["w_ssm_proj"], mode="nt", dims=(S, SSM_W, D), tiles=(TM, SSM_W, 256),
                  out_dtypes=[F32], name="ssm_proj_dgrad", b3=True)
    gw_ssm_proj = _matmul(z2, dy_ssm, mode="tn", dims=(SSM_W, D, S), tiles=(SSM_W, 256, TS),
                          out_dtypes=[BF16], name="ssm_proj_wgrad", out3=True)

    def f_glu_bwd(rv, vv, i, nt):
        dz2v, zv, zgv = rv
        sg = _sigmoid(zgv)
        dzg = dz2v * zv.astype(F32) * sg * (1.0 - sg)
        return [dzg, dz2v * sg], [_colsum(dzg)]

    dzg, dz_a, d_b_glu = _rowwise(f_glu_bwd, [_row(dz2), _row(z), _row(zg)], [],
                                  [(SSM_W, BF16), (SSM_W, F32)], [(1, SSM_W)],
                                  n_rows=S, tr=TM, ch=32, name="glu_bwd")
    dz_b = _matmul(dzg, w_glu_f, mode="nt", dims=(S, SSM_W, SSM_W), tiles=(TM, SSM_W, SSM_W),
                   out_dtypes=[F32], name="glu_dgrad")
    gw_glu = _matmul(z, dzg, mode="tn", dims=(SSM_W, SSM_W, S), tiles=(SSM_W, SSM_W, TS),
                     out_dtypes=[BF16], name="glu_wgrad").reshape(N_DEV, SSM_W // N_DEV, SSM_W)
    (g_flight["w_ssm_proj"], g_flight["w_glu"]), tok = _exchange_start(
        [gw_ssm_proj, gw_glu], "scatter", "grads_start_ssm")

    def f_ssm_out_bwd(rv, vv, i, nt):
        dza, dzb, yv, uv = rv
        dy = (dza + dzb) * _gelu_grad(yv)
        return [dy, dy * vv[0]], [_colsum(dy * uv)]

    dy_s, du_a, d_ssm_d = _rowwise(
        f_ssm_out_bwd, [_row(dz_a), _row(dz_b), _row(y_ssm_pre), _row(proj, u_blk, SSM_W)], [ssm_d],
        [(SSM_W, BF16), (SSM_W, F32)], [(1, SSM_W)], n_rows=S, tr=TM, ch=32, name="ssm_out_bwd", dep=tok)
    gx = _matmul(dy_s, ccat, mode="nn", dims=(S, 2 * SSM_H, SSM_W), tiles=(TM, 2048, SSM_W),
                 out_dtypes=[F32], name="ssm_cx_dgrad")
    dccat = _matmul(dy_s, xs, mode="tn", dims=(SSM_W, 2 * SSM_H, S), tiles=(SSM_W, 2048, TS),
                    out_dtypes=[F32], name="ssm_c_wgrad")
    hs, dacc = _scan_bwd(gx, xprev, abar, S)
    du_b = _matmul(hs, bcat, mode="nt", dims=(S, SSM_W, 2 * SSM_H), tiles=(TM, SSM_W, 2 * SSM_H),
                   out_dtypes=[F32], name="ssm_bu_dgrad")
    dbcat = _matmul(proj, hs, mode="tn", dims=(SSM_W, 2 * SSM_H, S), tiles=(SSM_W, 2048, TS),
                    out_dtypes=[F32], name="ssm_b_wgrad", a_off=u_blk)
    grp = np.arange(SSM_H) // SSM_N
    gind = jnp.asarray((grp[:, None] == np.arange(128)[None, :]).astype(np.float32), BF16)
    d_lam_re, d_lam_im, d_ls, d_btr, d_bti, d_ctr, d_cti = _ssm_param_bwd(
        lam_re, lam_im, ls_x, btr, bti, dacc, dbcat, dccat, gind)

    dattn = _matmul(dy_attn, G["w_attn_proj"], mode="nt", dims=(S, ATTN_W, D), tiles=(TM, ATTN_W, 256),
                    out_dtypes=[BF16], name="attn_proj_dgrad", b3=True)
    gw_attn_proj = _matmul(attn, dy_attn, mode="tn", dims=(ATTN_W, D, S), tiles=(ATTN_W, 256, TS),
                           out_dtypes=[BF16], name="attn_proj_wgrad", out3=True)
    (g_flight["w_attn_proj"],), tok = _exchange_start(
        [gw_attn_proj], "scatter", "grads_start_attn")
    dq, dkc, dkp, dvc, dvp, dbias, dsink = _attention_bwd(proj, attn, dattn, bias2, sinkcol, S)
    d_bias_b, d_sinks = _bucket_reduce(dbias.reshape(N_Q_HEADS, BLK * 2 * BLK),
                                       dsink.reshape(N_Q_HEADS, BLK), onehot_t)

    def f_dproj(rv, vv, i, nt):
        dqv, kc, kp, vc, vp, dua, dub, gav, gsv = rv
        keep = (i < nt - 1).astype(F32)
        dp = jnp.concatenate([dqv.astype(F32), kc + keep * kp, vc + keep * vp, dua + dub,
                              gav.astype(F32), gsv.astype(F32)], axis=-1)
        return [dp], [_colsum(dp)]

    dproj, d_b_in = _rowwise(
        f_dproj, [_row(dq), _row(dkc), _row(dkp, shift=1), _row(dvc), _row(dvp, shift=1),
                  _row(du_a), _row(du_b), _row(dga), _row(dgs)], [],
        [(IN_W, BF16)], [(1, IN_W)], n_rows=S, tr=BLK, ch=16, name="dproj", dep=tok)
    gw_in = _matmul(h, dproj, mode="tn", dims=(D, IN_W, S), tiles=(1024, 768, TS),
                    out_dtypes=[BF16], name="in_wgrad", out3=True)
    (g_flight["w_in"],), tok = _exchange_start([gw_in], "scatter", "grads_start_in")
    dh = _matmul(dproj, G["w_in"], mode="nt", dims=(S, D, IN_W), tiles=(TM, D, 768),
                 out_dtypes=[F32], name="in_dgrad", b3=True, dep=tok)

    def f_norm1_bwd(rv, vv, i, nt):
        xv, dhv, dx1v = rv
        g, sc = vv
        r = _rms(xv)
        xh = xv * r
        t = xh * g
        dt = dhv * (1.0 + sc)
        dxh = dt * g
        dxv = dx1v + r * (dxh - xh * jnp.mean(dxh * xh, axis=-1, keepdims=True))
        return [dxv], [_colsum(dhv), _colsum(dhv * t), _colsum(dt * xh)]

    grad_x, dsh1, dsc1, d_norm1_g = _rowwise(
        f_norm1_bwd, [_row(x2d), _row(dh), _row(dx1)], [norm1_g, sc1],
        [(D, F32)], [(1, D)] * 3, n_rows=S, tr=TR, ch=32, name="norm1_bwd")

    part = _pack({
        "b_ada": [dsh1, dsc1, dg1, dsh2, dsc2, dg2], "norm1_g": d_norm1_g, "b_in": d_b_in, "norm2_g": d_norm2_g,
        "final_g": d_final_g, "lambda_re": d_lam_re, "lambda_im": d_lam_im,
        "log_step": d_ls[0, :SSM_G], "attn_sinks": d_sinks[:, 0],
        "rel_bias": jnp.transpose(d_bias_b[:, :NUM_BUCKETS]), "b_glu": d_b_glu, "ssm_d": d_ssm_d,
        "loss": loss_cols, "ssm_b_re": d_btr, "ssm_b_im": d_bti, "ssm_c_re": d_ctr, "ssm_c_im": d_cti,
    })
    part_all = _small_allgather(part, "allgather_small_grads")
    wp, mp, vp = [_pack(_small_params_packed(p)) for p in (W, Mo, Vo)]
    sg, sdelta, sm, sv = _adamw([(part_all, d) for d in range(N_DEV)], wp, mp, vp,
                                tr=PACK_ROWS, ch=8, name="adamw_small")
    lo, _ = _PACK_OFF["loss"]
    loss = jnp.sum(sg[lo])

    o_ada, _ = _PACK_OFF["b_ada"]
    dmod_all = part_all[:, o_ada:o_ada + N_MOD, :].reshape(N_DEV, N_MOD * D)
    dmod_cols = lax.dynamic_slice(dmod_all, (0, me * n_ada), (N_DEV, n_ada))
    gw_ada = _matmul(cs, dmod_cols, mode="tn", dims=(D, n_ada, N_DEV), tiles=(D, 512, N_DEV),
                     out_dtypes=[F32], name="ada_wgrad")

    big_out = {"w_ada": _adamw([(gw_ada, 0)], w_ada[0], m_w_ada[0], v_w_ada[0],
                               tr=_adamw_rows(D, n_ada), ch=16, name="adamw_w_ada")}
    after = big_out["w_ada"][0]
    for n in ["w_ff2", "w_ff1", "w_out", "w_ssm_proj", "w_glu", "w_attn_proj", "w_in"]:
        recv = _exchange_wait([g_flight[n]], "scatter", after, "grads_wait_" + n[2:])[0]
        rows, cols = shard[n].shape
        big_out[n] = _adamw([(recv, d) for d in range(N_DEV)], shard[n], Mo[n][0], Vo[n][0],
                            tr=_adamw_rows(rows, cols), ch=16, name="adamw_" + n)
        after = big_out[n][0]

    def leaf(kind, n):
        if n in big_out:
            return big_out[n][kind][None]
        return _unpack_small((sg, sdelta, sm, sv)[kind], n)

    outs = [loss, grad_x.reshape(1, S, D)]
    for kind in range(4):
        outs.extend(leaf(kind, n) for n in WEIGHT_ORDER)
    return tuple(outs)
```

```python
import functools
import math

import numpy as np
import jax
import jax.numpy as jnp
from jax import lax
from jax.experimental import pallas as pl
from jax.experimental.pallas import tpu as pltpu

F32 = jnp.float32
BF16 = jnp.bfloat16
MESH = pl.DeviceIdType.MESH

N_DEV = 8
D = 2048
HEAD_DIM = 64
N_Q_HEADS = 16
N_KV_HEADS = 4
GROUP = N_Q_HEADS // N_KV_HEADS
ATTN_W = N_Q_HEADS * HEAD_DIM
KV_W = N_KV_HEADS * HEAD_DIM
BLK = 128
NUM_BUCKETS = 32
MAX_DISTANCE = 128
NEG_INF = -1e30
SSM_W = 512
SSM_P = 16
SSM_G = 32
SSM_N = 64
SSM_H = SSM_G * SSM_N
D_FF = 4 * D
IN_W = ATTN_W + 2 * KV_W + SSM_W + 2 * D
N_MOD = 6
EPS = 1e-6

ADAM_LR = 0.001
ADAM_B1 = 0.9
ADAM_B2 = 0.999
ADAM_EPS = 1e-08
ADAM_WD = 0.01
ADAM_STEP = 10

VMEM_LIMIT = 56 * 1024 * 1024
PACK_W = 2048


def _cparams(sem):
    return pltpu.CompilerParams(dimension_semantics=sem, vmem_limit_bytes=VMEM_LIMIT)


def _matmul(a, b, *, mode, dims, tiles, out_dtypes, name, a_off=0, b3=False,
            out3=False, bias=None, extras=(), epilogue=None, dep=None):
    M, N, K = dims
    tm, tn, tk = tiles
    assert M % tm == 0 and N % tn == 0 and K % tk == 0, (name, dims, tiles)
    gm, gn, gk = M // tm, N // tn, K // tk
    n_extra = len(extras)
    has_bias = bias is not None
    n_out = len(out_dtypes)

    if mode == "nn":
        a_spec = pl.BlockSpec((tm, tk), lambda i, j, k: (i, a_off + k))
        if b3:
            nb = (N // N_DEV) // tn
            assert nb * tn * N_DEV == N
            b_spec = pl.BlockSpec((None, tk, tn), lambda i, j, k: (j // nb, k, j % nb))
        else:
            b_spec = pl.BlockSpec((tk, tn), lambda i, j, k: (k, j))
        dn = (((1,), (0,)), ((), ()))
    elif mode == "nt":
        a_spec = pl.BlockSpec((tm, tk), lambda i, j, k: (i, a_off + k))
        if b3:
            nb = (K // N_DEV) // tk
            assert nb * tk * N_DEV == K
            b_spec = pl.BlockSpec((None, tn, tk), lambda i, j, k: (k // nb, j, k % nb))
        else:
            b_spec = pl.BlockSpec((tn, tk), lambda i, j, k: (j, k))
        dn = (((1,), (1,)), ((), ()))
    else:
        a_spec = pl.BlockSpec((tk, tm), lambda i, j, k: (k, a_off + i))
        b_spec = pl.BlockSpec((tk, tn), lambda i, j, k: (k, j))
        dn = (((0,), (0,)), ((), ()))

    if out3:
        nbo = (N // N_DEV) // tn
        assert nbo * tn * N_DEV == N
        o_spec = pl.BlockSpec((None, tm, tn), lambda i, j, k: (j // nbo, i, j % nbo))
        o_shape = (N_DEV, M, N // N_DEV)
    else:
        o_spec = pl.BlockSpec((tm, tn), lambda i, j, k: (i, j))
        o_shape = (M, N)

    in_specs = [a_spec, b_spec]
    args = [a, b]
    if has_bias:
        in_specs.append(pl.BlockSpec((1, tn), lambda i, j, k: (0, j)))
        args.append(bias)
    for e in extras:
        in_specs.append(pl.BlockSpec((tm, tn), lambda i, j, k: (i, j)))
        args.append(e)
    n_dep = 0 if dep is None else 1
    if n_dep:
        in_specs.append(pl.BlockSpec(memory_space=pl.ANY))
        args.append(dep)

    def body(*refs):
        a_ref, b_ref = refs[0], refs[1]
        pos = 2
        bias_ref = None
        if has_bias:
            bias_ref = refs[pos]
            pos += 1
        extra_refs = refs[pos:pos + n_extra]
        pos += n_extra + n_dep
        out_refs = refs[pos:pos + n_out]
        acc_ref = refs[pos + n_out] if gk > 1 else None

        part = lax.dot_general(a_ref[...].astype(BF16), b_ref[...].astype(BF16), dn,
                               preferred_element_type=F32)

        def finish(acc):
            if has_bias:
                acc = acc + bias_ref[...]
            if epilogue is None:
                vals = (acc,)
            else:
                vals = epilogue(acc, *[e[...] for e in extra_refs])
            for o_ref, val in zip(out_refs, vals):
                o_ref[...] = val.astype(o_ref.dtype)

        if gk == 1:
            finish(part)
        else:
            k = pl.program_id(2)

            @pl.when(k == 0)
            def _():
                acc_ref[...] = part

            @pl.when(k > 0)
            def _():
                acc_ref[...] += part

            @pl.when(k == gk - 1)
            def _():
                finish(acc_ref[...])

    outs = pl.pallas_call(
        body,
        grid=(gm, gn, gk),
        in_specs=in_specs,
        out_specs=[o_spec] * n_out,
        out_shape=[jax.ShapeDtypeStruct(o_shape, dt) for dt in out_dtypes],
        scratch_shapes=([pltpu.VMEM((tm, tn), F32)] if gk > 1 else []),
        compiler_params=_cparams(("parallel", "parallel", "arbitrary")),
        name=name,
    )(*args)
    return outs[0] if n_out == 1 else outs


def _rowwise(fn, rows, vecs, row_outs, sum_outs, *, n_rows, tr, ch, name, dep=None, prefetch=None):
    assert n_rows % tr == 0 and tr % ch == 0
    nt = n_rows // tr
    nr, nv, nro, nso = len(rows), len(vecs), len(row_outs), len(sum_outs)
    in_specs, args = [], []
    n_pf = 0 if prefetch is None else 1
    for (arr, lead, cblk, w, shift) in rows:
        if shift:
            ridx = lambda i, shift=shift: jnp.minimum(i + shift, nt - 1)
        else:
            ridx = lambda i: i
        if arr.ndim == 3:
            def imap(i, *pf, lead=lead, cblk=cblk, ridx=ridx):
                return (lead(pf[0]) if callable(lead) else lead, ridx(i), cblk)
            in_specs.append(pl.BlockSpec((None, tr, w), imap))
        else:
            in_specs.append(pl.BlockSpec(
                (tr, w), lambda i, *pf, cblk=cblk, ridx=ridx: (ridx(i), cblk)))
        args.append(arr)
    for v in vecs:
        in_specs.append(pl.BlockSpec(v.shape, lambda i, *pf, nd=v.ndim: (0,) * nd))
        args.append(v)
    n_dep = 0 if dep is None else 1
    if n_dep:
        in_specs.append(pl.BlockSpec(memory_space=pl.ANY))
        args.append(dep)
    out_specs = [pl.BlockSpec((tr, w), lambda i, *pf: (i, 0)) for (w, _) in row_outs]
    out_shape = [jax.ShapeDtypeStruct((n_rows, w), dt) for (w, dt) in row_outs]
    for (r, w) in sum_outs:
        out_specs.append(pl.BlockSpec((r, w), lambda i, *pf: (0, 0)))
        out_shape.append(jax.ShapeDtypeStruct((r, w), F32))

    def body(*refs):
        refs = refs[n_pf:]
        i = pl.program_id(0)
        r_in = refs[:nr]
        v_in = refs[nr:nr + nv]
        r_out = refs[nr + nv + n_dep:nr + nv + n_dep + nro]
        s_out = refs[nr + nv + n_dep + nro:]
        if nso:
            @pl.when(i == 0)
            def _():
                for s in s_out:
                    s[...] = jnp.zeros(s.shape, F32)
        vvals = [v[...] for v in v_in]

        def chunk(ci, carry):
            r0 = pl.multiple_of(ci * ch, ch)
            rv = [r[pl.ds(r0, ch), :] for r in r_in]
            ro, so = fn(rv, vvals, i, nt)
            for ref, val in zip(r_out, ro):
                ref[pl.ds(r0, ch), :] = val.astype(ref.dtype)
            for ref, val in zip(s_out, so):
                ref[...] += val
            return carry

        lax.fori_loop(0, tr // ch, chunk, 0)

    outs = pl.pallas_call(
        body,
        grid_spec=pltpu.PrefetchScalarGridSpec(
            num_scalar_prefetch=n_pf, grid=(nt,), in_specs=in_specs, out_specs=out_specs),
        out_shape=out_shape,
        compiler_params=_cparams(("arbitrary",)),
        name=name,
    )(*([prefetch] if n_pf else []), *args)
    return outs


def _row(arr, cblk=0, w=None, lead=0, shift=0):
    return (arr, lead, cblk, arr.shape[-1] if w is None else w, shift)


def _colsum(v):
    return jnp.sum(v, axis=0, keepdims=True)


def _rms(x):
    return lax.rsqrt(jnp.mean(x * x, axis=-1, keepdims=True) + EPS)


def _sigmoid(x):
    return 1.0 / (1.0 + jnp.exp(-x))


_GELU_C = math.sqrt(2.0 / math.pi)


def _gelu(x):
    return 0.5 * x * (1.0 + jnp.tanh(_GELU_C * (x + 0.044715 * (x * x * x))))


def _gelu_grad(x):
    t = jnp.tanh(_GELU_C * (x + 0.044715 * (x * x * x)))
    return 0.5 * (1.0 + t) + 0.5 * x * (1.0 - t * t) * (_GELU_C * (1.0 + 3.0 * 0.044715 * (x * x)))


def _my_pos():
    return lax.axis_index("x"), lax.axis_index("y"), lax.axis_index("c")


def _flip(pos, k):
    x, y, c = pos
    return (1 - x if k & 4 else x, 1 - y if k & 2 else y, 1 - c if k & 1 else c)


def _dev_id(pos):
    return 4 * pos[0] + 2 * pos[1] + pos[2]


def _small_allgather(x, name):
    r, c = x.shape

    def body(x_ref, out_ref, send_sems, recv_sems):
        me = _my_pos()
        out_ref[_dev_id(me)] = x_ref[...]
        copies = []
        for k in range(1, N_DEV):
            cp = pltpu.make_async_remote_copy(
                src_ref=x_ref, dst_ref=out_ref.at[_dev_id(me)],
                send_sem=send_sems.at[k - 1], recv_sem=recv_sems.at[k - 1],
                device_id=_flip(me, k), device_id_type=MESH)
            cp.start()
            copies.append(cp)
        for k in range(1, N_DEV):
            peer = _flip(me, k)
            pltpu.make_async_remote_copy(
                src_ref=x_ref, dst_ref=out_ref.at[_dev_id(peer)],
                send_sem=send_sems.at[k - 1], recv_sem=recv_sems.at[k - 1],
                device_id=peer, device_id_type=MESH).wait_recv()
        for cp in copies:
            cp.wait_send()

    return pl.pallas_call(
        body,
        out_shape=jax.ShapeDtypeStruct((N_DEV, r, c), x.dtype),
        in_specs=[pl.BlockSpec(memory_space=pltpu.VMEM)],
        out_specs=pl.BlockSpec(memory_space=pltpu.VMEM),
        scratch_shapes=[pltpu.SemaphoreType.DMA((N_DEV - 1,)),
                        pltpu.SemaphoreType.DMA((N_DEV - 1,))],
        compiler_params=pltpu.CompilerParams(vmem_limit_bytes=VMEM_LIMIT),
        name=name,
    )(x)


def _allgather_inplace(buf, name):
    def body(_, out, send_sems, recv_sems):
        x, y, c = _my_pos()
        me, sib = (x, y, c), (x, y, 1 - c)
        chips = [(1 - x, y), (x, 1 - y), (1 - x, 1 - y)]

        def copy(k, block, to):
            slot = out.at[_dev_id(block)]
            return pltpu.make_async_remote_copy(
                src_ref=slot, dst_ref=slot, send_sem=send_sems.at[k], recv_sem=recv_sems.at[k],
                device_id=to, device_id_type=MESH)

        first = [copy(0, me, sib)] + [copy(1 + j, me, (*chip, c)) for j, chip in enumerate(chips)]
        for cp in first:
            cp.start()
        passed = []
        for j, chip in enumerate(chips):
            copy(1 + j, (*chip, c), me).wait_recv()
            cp = copy(4 + j, (*chip, c), sib)
            cp.start()
            passed.append(cp)
        copy(0, sib, me).wait_recv()
        for j, chip in enumerate(chips):
            copy(4 + j, (*chip, 1 - c), me).wait_recv()
        for cp in first + passed:
            cp.wait_send()

    hbm = pl.BlockSpec(memory_space=pl.ANY)
    return pl.pallas_call(
        body,
        out_shape=jax.ShapeDtypeStruct(buf.shape, buf.dtype),
        in_specs=[hbm],
        out_specs=hbm,
        input_output_aliases={0: 0},
        scratch_shapes=[pltpu.SemaphoreType.DMA((7,)), pltpu.SemaphoreType.DMA((7,))],
        name=name,
    )(buf)


_HBM = pl.BlockSpec(memory_space=pltpu.HBM)
_SEM = pl.BlockSpec(memory_space=pltpu.SEMAPHORE)
_EFFECT = pltpu.SideEffectType.DATAFLOW_SIDE_EFFECTING


def _exchange_copy(kind, bufs, send_sems, recv_sems, me, k, arriving):
    peer = _flip(me, k)
    my_id, peer_id = _dev_id(me), _dev_id(peer)
    if kind == "gather":
        slot = bufs[0].at[peer_id if arriving else my_id]
        src, dst = slot, slot
    else:
        src = bufs[0].at[my_id if arriving else peer_id]
        dst = bufs[1].at[peer_id if arriving else my_id]
    return pltpu.make_async_remote_copy(
        src_ref=src, dst_ref=dst, send_sem=send_sems.at[k - 1], recv_sem=recv_sems.at[k - 1],
        device_id=peer, device_id_type=MESH)


def _exchange_start(arrays, kind, name, after=None):
    n = len(arrays)
    n_after = 0 if after is None else 1
    if kind == "gather":
        bufs = [[a] for a in arrays]
    else:
        bufs = [[a, lax.empty(a.shape, a.dtype)] for a in arrays]
    nb = len(bufs[0])
    flat = [b for group in bufs for b in group]

    def body(*refs):
        outs_at = nb * n + n_after
        send = refs[outs_at:outs_at + n]
        recv = refs[outs_at + n:outs_at + 2 * n]
        token = refs[outs_at + 2 * n + nb * n]
        me = _my_pos()
        for a in range(n):
            for k in range(1, N_DEV):
                _exchange_copy(kind, refs[nb * a:nb * (a + 1)], send[a], recv[a], me, k, False).start()
        token[...] = jnp.zeros(token.shape, token.dtype)

    sem = pltpu.SemaphoreType.DMA((N_DEV - 1,))
    outs = pl.pallas_call(
        body,
        name=name,
        out_shape=([sem] * (2 * n) + [pltpu.HBM(b.shape, b.dtype) for b in flat]
                   + [jax.ShapeDtypeStruct((8, 128), F32)]),
        in_specs=[_HBM] * (nb * n) + [pl.BlockSpec(memory_space=pl.ANY)] * n_after,
        out_specs=[_SEM] * (2 * n) + [_HBM] * (nb * n) + [pl.BlockSpec(memory_space=pltpu.VMEM)],
        input_output_aliases={i: 2 * n + i for i in range(nb * n)},
        compiler_params=pltpu.CompilerParams(has_side_effects=_EFFECT),
    )(*[pltpu.with_memory_space_constraint(b, pltpu.HBM) for b in flat],
      *([after] if n_after else []))
    flights = [(outs[a], outs[n + a], list(outs[2 * n + nb * a:2 * n + nb * (a + 1)]))
               for a in range(n)]
    return flights, outs[2 * n + nb * n]


def _exchange_wait(flights, kind, after, name):
    n = len(flights)
    nb = len(flights[0][2])
    flat = [b for f in flights for b in f[2]]

    def body(*refs):
        send = refs[nb * n:nb * n + n]
        recv = refs[nb * n + n:nb * n + 2 * n]
        me = _my_pos()
        for a in range(n):
            for k in range(1, N_DEV):
                bufs = refs[nb * a:nb * (a + 1)]
                _exchange_copy(kind, bufs, send[a], recv[a], me, k, False).wait_send()
                _exchange_copy(kind, bufs, send[a], recv[a], me, k, True).wait_recv()

    outs = pl.pallas_call(
        body,
        name=name,
        out_shape=[pltpu.HBM(b.shape, b.dtype) for b in flat],
        in_specs=[_HBM] * (nb * n) + [_SEM] * (2 * n) + [pl.BlockSpec(memory_space=pl.ANY)],
        out_specs=[_HBM] * (nb * n),
        input_output_aliases={i: i for i in range(nb * n)},
        compiler_params=pltpu.CompilerParams(has_side_effects=_EFFECT),
    )(*flat, *[f[0] for f in flights], *[f[1] for f in flights], after)
    return [list(outs[nb * a:nb * (a + 1)]) for a in range(n)]


def _t5_buckets_block():
    qi = np.arange(BLK)[:, None]
    ki = np.arange(2 * BLK)[None, :]
    n = np.maximum(qi + BLK - ki, 0)
    max_exact = NUM_BUCKETS // 2
    large = max_exact + (np.log(np.maximum(n, 1) / max_exact)
                         / np.log(MAX_DISTANCE / max_exact)
                         * (NUM_BUCKETS - max_exact)).astype(np.int32)
    large = np.minimum(large, NUM_BUCKETS - 1)
    return np.where(n < max_exact, n, large).astype(np.int32)


def _band_mask():
    qi = np.arange(BLK)[:, None]
    ki = np.arange(2 * BLK)[None, :]
    dist = qi + BLK - ki
    return (dist >= 0) & (dist < BLK)


def _attn_scores(q_ref, kp_ref, kc_ref, hkv):
    c0 = hkv * HEAD_DIM
    kk = jnp.concatenate([kp_ref[:, c0:c0 + HEAD_DIM], kc_ref[:, c0:c0 + HEAD_DIM]],
                         axis=0).astype(BF16)
    qg = jnp.concatenate(
        [q_ref[:, (hkv * GROUP + g) * HEAD_DIM:(hkv * GROUP + g + 1) * HEAD_DIM]
         for g in range(GROUP)], axis=0).astype(BF16)
    s = lax.dot_general(qg, kk, (((1,), (1,)), ((), ())), preferred_element_type=F32)
    return qg, kk, s


def _attn_softmax(s, bias_ref, sink_ref, hkv):
    r0, r1 = hkv * GROUP * BLK, (hkv + 1) * GROUP * BLK
    s = s * (HEAD_DIM ** -0.5) + bias_ref[r0:r1, :]
    sink = sink_ref[r0:r1, :]
    m = jnp.maximum(jnp.max(s, axis=-1, keepdims=True), sink)
    p = jnp.exp(s - m)
    e_sink = jnp.exp(sink - m)
    inv = 1.0 / (jnp.sum(p, axis=-1, keepdims=True) + e_sink)
    return p * inv, e_sink * inv


def _kv_rows(p_ref, c_ref, hkv):
    c0 = hkv * HEAD_DIM
    return jnp.concatenate([p_ref[:, c0:c0 + HEAD_DIM], c_ref[:, c0:c0 + HEAD_DIM]],
                           axis=0).astype(BF16)


def _attn_in_specs(bias2):
    prev = lambda n: jnp.maximum(n - 1, 0)
    return [
        pl.BlockSpec((BLK, ATTN_W), lambda n: (n, 0)),
        pl.BlockSpec((BLK, KV_W), lambda n: (prev(n), ATTN_W // KV_W)),
        pl.BlockSpec((BLK, KV_W), lambda n: (n, ATTN_W // KV_W)),
        pl.BlockSpec((BLK, KV_W), lambda n: (prev(n), ATTN_W // KV_W + 1)),
        pl.BlockSpec((BLK, KV_W), lambda n: (n, ATTN_W // KV_W + 1)),
        pl.BlockSpec((None,) + bias2.shape[1:], lambda n: (jnp.minimum(n, 1), 0, 0)),
    ]


def _attention_fwd(proj, bias2, sinkcol, n_rows):
    nb = n_rows // BLK

    def body(q_ref, kp_ref, kc_ref, vp_ref, vc_ref, bias_ref, sink_ref, o_ref):
        heads = range(N_KV_HEADS)
        scores = [_attn_scores(q_ref, kp_ref, kc_ref, hkv)[2] for hkv in heads]
        probs = [_attn_softmax(scores[hkv], bias_ref, sink_ref, hkv)[0] for hkv in heads]
        outs = [jnp.dot(probs[hkv].astype(BF16), _kv_rows(vp_ref, vc_ref, hkv),
                        preferred_element_type=F32) for hkv in heads]
        for hkv in heads:
            for g in range(GROUP):
                h = hkv * GROUP + g
                o_ref[:, h * HEAD_DIM:(h + 1) * HEAD_DIM] = (
                    outs[hkv][g * BLK:(g + 1) * BLK, :].astype(o_ref.dtype))

    return pl.pallas_call(
        body,
        grid=(nb,),
        in_specs=_attn_in_specs(bias2) + [pl.BlockSpec(sinkcol.shape, lambda n: (0, 0))],
        out_specs=pl.BlockSpec((BLK, ATTN_W), lambda n: (n, 0)),
        out_shape=jax.ShapeDtypeStruct((n_rows, ATTN_W), BF16),
        compiler_params=_cparams(("parallel",)),
        name="attn_fwd",
    )(proj, proj, proj, proj, proj, bias2, sinkcol)


def _attention_bwd(proj, attn, dattn, bias2, sinkcol, n_rows):
    nb = n_rows // BLK
    scale = HEAD_DIM ** -0.5
    dn_t = (((0,), (0,)), ((), ()))

    def body(q_ref, kp_ref, kc_ref, vp_ref, vc_ref, bias_ref, o_ref, do_ref, sink_ref,
             dq_ref, dkc_ref, dkp_ref, dvc_ref, dvp_ref, dbias_ref, dsink_ref):
        @pl.when(pl.program_id(0) == 0)
        def _():
            dbias_ref[...] = jnp.zeros(dbias_ref.shape, F32)
            dsink_ref[...] = jnp.zeros(dsink_ref.shape, F32)

        heads = range(N_KV_HEADS)
        qk = [_attn_scores(q_ref, kp_ref, kc_ref, hkv) for hkv in heads]
        dog, dps, deltas = [], [], []
        for hkv in heads:
            hs = [hkv * GROUP + g for g in range(GROUP)]
            d_o = jnp.concatenate([do_ref[:, h * HEAD_DIM:(h + 1) * HEAD_DIM] for h in hs], axis=0)
            o = jnp.concatenate([o_ref[:, h * HEAD_DIM:(h + 1) * HEAD_DIM] for h in hs], axis=0)
            deltas.append(jnp.sum(d_o.astype(F32) * o.astype(F32), axis=-1, keepdims=True))
            dog.append(d_o.astype(BF16))
            dps.append(lax.dot_general(dog[hkv], _kv_rows(vp_ref, vc_ref, hkv),
                                       (((1,), (1,)), ((), ())), preferred_element_type=F32))
        p16, ds16 = [], []
        for hkv in heads:
            r0, r1 = hkv * GROUP * BLK, (hkv + 1) * GROUP * BLK
            p, p_sink = _attn_softmax(qk[hkv][2], bias_ref, sink_ref, hkv)
            ds = p * (dps[hkv] - deltas[hkv])
            dbias_ref[r0:r1, :] += ds
            dsink_ref[r0:r1, :] += -(p_sink * deltas[hkv])
            p16.append(p.astype(BF16))
            ds16.append(ds.astype(BF16))
        for hkv in heads:
            c0 = hkv * HEAD_DIM
            qg, kk, _ = qk[hkv]
            dqg = jnp.dot(ds16[hkv], kk, preferred_element_type=F32) * scale
            dkk = lax.dot_general(ds16[hkv], qg, dn_t, preferred_element_type=F32) * scale
            dvv = lax.dot_general(p16[hkv], dog[hkv], dn_t, preferred_element_type=F32)
            for g in range(GROUP):
                h = hkv * GROUP + g
                dq_ref[:, h * HEAD_DIM:(h + 1) * HEAD_DIM] = (
                    dqg[g * BLK:(g + 1) * BLK, :].astype(dq_ref.dtype))
            dkp_ref[:, c0:c0 + HEAD_DIM] = dkk[:BLK].astype(dkp_ref.dtype)
            dkc_ref[:, c0:c0 + HEAD_DIM] = dkk[BLK:].astype(dkc_ref.dtype)
            dvp_ref[:, c0:c0 + HEAD_DIM] = dvv[:BLK].astype(dvp_ref.dtype)
            dvc_ref[:, c0:c0 + HEAD_DIM] = dvv[BLK:].astype(dvc_ref.dtype)

    kv_out = pl.BlockSpec((BLK, KV_W), lambda n: (n, 0))
    kv_shape = jax.ShapeDtypeStruct((n_rows, KV_W), F32)
    acc_shape = bias2.shape[1:]
    return pl.pallas_call(
        body,
        grid=(nb,),
        in_specs=_attn_in_specs(bias2) + [
            pl.BlockSpec((BLK, ATTN_W), lambda n: (n, 0)),
            pl.BlockSpec((BLK, ATTN_W), lambda n: (n, 0)),
            pl.BlockSpec(sinkcol.shape, lambda n: (0, 0)),
        ],
        out_specs=[
            pl.BlockSpec((BLK, ATTN_W), lambda n: (n, 0)),
            kv_out, kv_out, kv_out, kv_out,
            pl.BlockSpec(acc_shape, lambda n: (0, 0)),
            pl.BlockSpec(sinkcol.shape, lambda n: (0, 0)),
        ],
        out_shape=[
            jax.ShapeDtypeStruct((n_rows, ATTN_W), BF16),
            kv_shape, kv_shape, kv_shape, kv_shape,
            jax.ShapeDtypeStruct(acc_shape, F32),
            jax.ShapeDtypeStruct(sinkcol.shape, F32),
        ],
        compiler_params=_cparams(("arbitrary",)),
        name="attn_bwd",
    )(proj, proj, proj, proj, proj, bias2, attn, dattn, sinkcol)


def _bias_tables(rel_bias_t, onehot_t, band_first, band_rest):
    def body(rb_ref, oh_ref, mf_ref, mr_ref, out_ref):
        acc = jnp.zeros((N_Q_HEADS, BLK * 2 * BLK), F32)
        for part in _split3(rb_ref[...]):
            acc = acc + jnp.dot(part, oh_ref[...], preferred_element_type=F32)
        out_ref[0] = jnp.where(mf_ref[...] > 0.0, acc, NEG_INF)
        out_ref[1] = jnp.where(mr_ref[...] > 0.0, acc, NEG_INF)

    return pl.pallas_call(
        body,
        out_shape=jax.ShapeDtypeStruct((2, N_Q_HEADS, BLK * 2 * BLK), F32),
        compiler_params=pltpu.CompilerParams(vmem_limit_bytes=VMEM_LIMIT),
        name="bias_tables",
    )(rel_bias_t, onehot_t, band_first, band_rest)


def _split3(a):
    hi = a.astype(BF16)
    r1 = a - hi.astype(F32)
    mid = r1.astype(BF16)
    lo = (r1 - mid.astype(F32)).astype(BF16)
    return hi, mid, lo


def _bucket_reduce(dbias, dsink, onehot_t):
    def body(db_ref, ds_ref, oh_ref, ob_ref, os_ref):
        acc = jnp.zeros((N_Q_HEADS, 128), F32)
        for part in _split3(db_ref[...]):
            acc = acc + lax.dot_general(part, oh_ref[...], (((1,), (1,)), ((), ())),
                                        preferred_element_type=F32)
        ob_ref[...] = acc
        os_ref[...] = jnp.broadcast_to(jnp.sum(ds_ref[...], axis=-1, keepdims=True),
                                       os_ref.shape)

    return pl.pallas_call(
        body,
        out_shape=[jax.ShapeDtypeStruct((N_Q_HEADS, 128), F32),
                   jax.ShapeDtypeStruct((N_Q_HEADS, 128), F32)],
        compiler_params=pltpu.CompilerParams(vmem_limit_bytes=VMEM_LIMIT),
        name="bias_bucket_reduce",
    )(dbias, dsink, onehot_t)


def _disc(lr, li, ls, btr, bti):
    lam_re = jnp.minimum(lr, -1e-4)
    delta = jnp.exp(ls)
    mag = jnp.exp(lam_re * delta)
    ang = li * delta
    ar, ai = mag * jnp.cos(ang), mag * jnp.sin(ang)
    nr, ni = ar - 1.0, ai
    den = lam_re * lam_re + li * li
    fr = (nr * lam_re + ni * li) / den
    fi = (ni * lam_re - nr * li) / den
    bbr = fr * btr - fi * bti
    bbi = fr * bti + fi * btr
    return ar, ai, bbr, bbi


def _block_mask():
    row = lax.broadcasted_iota(jnp.int32, (SSM_W, SSM_H), 0)
    col = lax.broadcasted_iota(jnp.int32, (SSM_W, SSM_H), 1)
    return (row // SSM_P) == (col // SSM_N)


def _ssm_setup(lr, li, ls, btr, bti, ctr, cti):
    def body(lr_ref, li_ref, ls_ref, btr_ref, bti_ref, ctr_ref, cti_ref, a_ref, b_ref, c_ref):
        ar, ai, bbr, bbi = _disc(lr_ref[...], li_ref[...], ls_ref[...], btr_ref[...], bti_ref[...])
        a_ref[:, :SSM_H] = ar
        a_ref[:, SSM_H:] = ai
        mask = _block_mask()
        blk = lambda t: jnp.where(mask, jnp.tile(t, (SSM_G, 1)), 0.0)
        b_ref[:, :SSM_H] = blk(bbr).astype(BF16)
        b_ref[:, SSM_H:] = blk(bbi).astype(BF16)
        c_ref[:, :SSM_H] = blk(ctr_ref[...]).astype(BF16)
        c_ref[:, SSM_H:] = blk(-cti_ref[...]).astype(BF16)

    return pl.pallas_call(
        body,
        out_shape=[jax.ShapeDtypeStruct((1, 2 * SSM_H), F32),
                   jax.ShapeDtypeStruct((SSM_W, 2 * SSM_H), BF16),
                   jax.ShapeDtypeStruct((SSM_W, 2 * SSM_H), BF16)],
        compiler_params=pltpu.CompilerParams(vmem_limit_bytes=VMEM_LIMIT),
        name="ssm_setup",
    )(lr, li, ls, btr, bti, ctr, cti)


def _ssm_param_bwd(lr, li, ls, btr, bti, dacc, dbcat, dccat, gind):
    def body(lr_ref, li_ref, ls_ref, btr_ref, bti_ref, dacc_ref, db_ref, dc_ref, g_ref,
             dlr_ref, dli_ref, dls_ref, dbtr_ref, dbti_ref, dctr_ref, dcti_ref):
        dar = jnp.sum(dacc_ref[:, :SSM_H], axis=0, keepdims=True)
        dai = jnp.sum(dacc_ref[:, SSM_H:], axis=0, keepdims=True)
        col = lax.broadcasted_iota(jnp.int32, (SSM_P, 2 * SSM_H), 1)
        grp = (col % SSM_H) // SSM_N
        db = jnp.zeros((SSM_P, 2 * SSM_H), F32)
        dc = jnp.zeros((SSM_P, 2 * SSM_H), F32)
        for g in range(SSM_G):
            sel = grp == g
            db = db + jnp.where(sel, db_ref[g * SSM_P:(g + 1) * SSM_P, :], 0.0)
            dc = dc + jnp.where(sel, dc_ref[g * SSM_P:(g + 1) * SSM_P, :], 0.0)
        dctr_ref[...] = dc[:, :SSM_H]
        dcti_ref[...] = -dc[:, SSM_H:]
        prim = (lr_ref[...], li_ref[...], ls_ref[...], btr_ref[...], bti_ref[...])
        _, vjp = jax.vjp(_disc, *prim)
        dlr, dli, dls, dbtr, dbti = vjp((dar, dai, db[:, :SSM_H], db[:, SSM_H:]))
        dlr_ref[...] = dlr
        dli_ref[...] = dli
        dbtr_ref[...] = dbtr
        dbti_ref[...] = dbti
        acc = jnp.zeros((8, 128), F32)
        for part in _split3(jnp.broadcast_to(dls, (8, SSM_H))):
            acc = acc + jnp.dot(part, g_ref[...], preferred_element_type=F32)
        dls_ref[...] = acc

    vec = jax.ShapeDtypeStruct((1, SSM_H), F32)
    mat = jax.ShapeDtypeStruct((SSM_P, SSM_H), F32)
    return pl.pallas_call(
        body,
        out_shape=[vec, vec, jax.ShapeDtypeStruct((8, 128), F32), mat, mat, mat, mat],
        compiler_params=pltpu.CompilerParams(vmem_limit_bytes=VMEM_LIMIT),
        name="ssm_param_bwd",
    )(lr, li, ls, btr, bti, dacc, dbcat, dccat, gind)


SCAN_TR = 256


def _cmul_add(vr, vi, pr, pi, sr, si):
    return vr + pr * sr - pi * si, vi + pr * si + pi * sr


def _bcast_row(v, row, which):
    b = jnp.where(row == which, v, 0.0)
    b = b + pltpu.roll(b, 4, 0)
    b = b + pltpu.roll(b, 2, 0)
    return b + pltpu.roll(b, 1, 0)


def _scan_tables(a_ref, tab_ref, reverse):
    H = SSM_H
    ar = jnp.broadcast_to(a_ref[:, :H], (8, H))
    ai = jnp.broadcast_to(a_ref[:, H:], (8, H))
    if reverse:
        ai = -ai
    row = lax.broadcasted_iota(jnp.int32, (8, H), 0)
    pw = [(ar, ai)]
    for _ in range(7):
        cr, ci = pw[-1]
        pw.append((cr * ar - ci * ai, cr * ai + ci * ar))
    pcr = jnp.zeros((8, H), F32)
    pci = jnp.zeros((8, H), F32)
    for e in range(8):
        sel = (row == (7 - e)) if reverse else (row == e)
        pcr = jnp.where(sel, pw[e][0], pcr)
        pci = jnp.where(sel, pw[e][1], pci)
    tab_ref[0, :, :H] = pcr
    tab_ref[0, :, H:] = pci
    for t, k in enumerate((1, 2, 4)):
        keep = (row < 8 - k) if reverse else (row >= k)
        tab_ref[1 + t, :, :H] = jnp.where(keep, pw[k - 1][0], 0.0)
        tab_ref[1 + t, :, H:] = jnp.where(keep, pw[k - 1][1], 0.0)


def _scan_group(vr, vi, cr, ci, tab_ref, reverse):
    H = SSM_H
    for t, k in enumerate((1, 2, 4)):
        sh = 8 - k if reverse else k
        vr, vi = _cmul_add(vr, vi, tab_ref[1 + t, :, :H], tab_ref[1 + t, :, H:],
                           pltpu.roll(vr, sh, 0), pltpu.roll(vi, sh, 0))
    return _cmul_add(vr, vi, tab_ref[0, :, :H], tab_ref[0, :, H:], cr, ci)


def _scan_fwd(bu, abar, n_rows):
    H = SSM_H
    nt = n_rows // SCAN_TR

    def body(bu_ref, a_ref, xs_ref, xp_ref, tab_ref, carry_ref):
        @pl.when(pl.program_id(0) == 0)
        def _():
            _scan_tables(a_ref, tab_ref, False)
            carry_ref[...] = jnp.zeros(carry_ref.shape, F32)

        row = lax.broadcasted_iota(jnp.int32, (8, H), 0)

        def group(j, carry):
            cr, ci = carry
            r0 = pl.multiple_of(j * 16, 16)
            xr, xi = [], []
            for half in range(2):
                rr = pl.multiple_of(r0 + 8 * half, 8)
                vr, vi = _scan_group(bu_ref[pl.ds(rr, 8), :H], bu_ref[pl.ds(rr, 8), H:],
                                     cr, ci, tab_ref, False)
                xp_ref[pl.ds(rr, 8), :H] = jnp.where(row == 0, cr, pltpu.roll(vr, 1, 0))
                xp_ref[pl.ds(rr, 8), H:] = jnp.where(row == 0, ci, pltpu.roll(vi, 1, 0))
                cr, ci = _bcast_row(vr, row, 7), _bcast_row(vi, row, 7)
                xr.append(vr)
                xi.append(vi)
            xs_ref[pl.ds(r0, 16), :H] = jnp.concatenate(xr, axis=0).astype(BF16)
            xs_ref[pl.ds(r0, 16), H:] = jnp.concatenate(xi, axis=0).astype(BF16)
            return cr, ci

        cr, ci = lax.fori_loop(0, SCAN_TR // 16, group,
                               (carry_ref[:, :H], carry_ref[:, H:]))
        carry_ref[:, :H] = cr
        carry_ref[:, H:] = ci

    return pl.pallas_call(
        body,
        grid=(nt,),
        in_specs=[pl.BlockSpec((SCAN_TR, 2 * H), lambda i: (i, 0)),
                  pl.BlockSpec((1, 2 * H), lambda i: (0, 0))],
        out_specs=[pl.BlockSpec((SCAN_TR, 2 * H), lambda i: (i, 0)),
                   pl.BlockSpec((SCAN_TR, 2 * H), lambda i: (i, 0))],
        out_shape=[jax.ShapeDtypeStruct((n_rows, 2 * H), BF16),
                   jax.ShapeDtypeStruct((n_rows, 2 * H), F32)],
        scratch_shapes=[pltpu.VMEM((4, 8, 2 * H), F32), pltpu.VMEM((8, 2 * H), F32)],
        compiler_params=_cparams(("arbitrary",)),
        name="ssm_scan_fwd",
    )(bu, abar)


def _scan_bwd(gx, xprev, abar, n_rows):
    H = SSM_H
    nt = n_rows // SCAN_TR

    def body(g_ref, xp_ref, a_ref, h_ref, da_ref, tab_ref, carry_ref):
        @pl.when(pl.program_id(0) == 0)
        def _():
            _scan_tables(a_ref, tab_ref, True)
            carry_ref[...] = jnp.zeros(carry_ref.shape, F32)
            da_ref[...] = jnp.zeros(da_ref.shape, F32)

        row = lax.broadcasted_iota(jnp.int32, (8, H), 0)
        n16 = SCAN_TR // 16

        def group(jj, carry):
            cr, ci = carry
            r0 = pl.multiple_of((n16 - 1 - jj) * 16, 16)
            hr, hi = [None, None], [None, None]
            for half in (1, 0):
                rr = pl.multiple_of(r0 + 8 * half, 8)
                vr, vi = _scan_group(g_ref[pl.ds(rr, 8), :H], g_ref[pl.ds(rr, 8), H:],
                                     cr, ci, tab_ref, True)
                pr, pi = xp_ref[pl.ds(rr, 8), :H], xp_ref[pl.ds(rr, 8), H:]
                da_ref[:, :H] += vr * pr + vi * pi
                da_ref[:, H:] += vi * pr - vr * pi
                cr, ci = _bcast_row(vr, row, 0), _bcast_row(vi, row, 0)
                hr[half], hi[half] = vr, vi
            h_ref[pl.ds(r0, 16), :H] = jnp.concatenate(hr, axis=0).astype(BF16)
            h_ref[pl.ds(r0, 16), H:] = jnp.concatenate(hi, axis=0).astype(BF16)
            return cr, ci

        cr, ci = lax.fori_loop(0, n16, group, (carry_ref[:, :H], carry_ref[:, H:]))
        carry_ref[:, :H] = cr
        carry_ref[:, H:] = ci

    rev = lambda i: (nt - 1 - i, 0)
    return pl.pallas_call(
        body,
        grid=(nt,),
        in_specs=[pl.BlockSpec((SCAN_TR, 2 * H), rev),
                  pl.BlockSpec((SCAN_TR, 2 * H), rev),
                  pl.BlockSpec((1, 2 * H), lambda i: (0, 0))],
        out_specs=[pl.BlockSpec((SCAN_TR, 2 * H), rev),
                   pl.BlockSpec((8, 2 * H), lambda i: (0, 0))],
        out_shape=[jax.ShapeDtypeStruct((n_rows, 2 * H), BF16),
                   jax.ShapeDtypeStruct((8, 2 * H), F32)],
        scratch_shapes=[pltpu.VMEM((4, 8, 2 * H), F32), pltpu.VMEM((8, 2 * H), F32)],
        compiler_params=_cparams(("arbitrary",)),
        name="ssm_scan_bwd",
    )(gx, xprev, abar)


def _adamw(parts, w, m, v, *, tr, ch, name, prefetch=None):
    n_rows, cols = w.shape
    n_parts = len(parts)
    c1 = 1.0 - ADAM_B1 ** ADAM_STEP
    c2 = 1.0 - ADAM_B2 ** ADAM_STEP

    def fn(rv, vv, i, nt):
        g = rv[0].astype(F32)
        for p in rv[1:n_parts]:
            g = g + p.astype(F32)
        wv, mv, vval = rv[n_parts:]
        nm = ADAM_B1 * mv + (1.0 - ADAM_B1) * g
        nv = ADAM_B2 * vval + (1.0 - ADAM_B2) * (g * g)
        delta = -ADAM_LR * ((nm / c1) / (jnp.sqrt(nv / c2) + ADAM_EPS) + ADAM_WD * wv)
        return [g, delta, nm, nv], []

    rows = [_row(arr, lead=lead) for (arr, lead) in parts] + [_row(w), _row(m), _row(v)]
    return _rowwise(fn, rows, [], [(cols, F32)] * 4, [], n_rows=n_rows, tr=tr, ch=ch, name=name,
                    prefetch=prefetch)


_PACK = [
    ("b_ada", 6), ("norm1_g", 1), ("b_in", 3), ("norm2_g", 1), ("final_g", 1),
    ("lambda_re", 1), ("lambda_im", 1), ("log_step", 1), ("attn_sinks", 1),
    ("rel_bias", 1), ("b_glu", 1), ("ssm_d", 1), ("loss", 1),
    ("ssm_b_re", 16), ("ssm_b_im", 16), ("ssm_c_re", 16), ("ssm_c_im", 16),
]
_PACK_OFF = {}
_off = 0
for _n, _r in _PACK:
    _PACK_OFF[_n] = (_off, _r)
    _off += _r
PACK_ROWS = -(-_off // 8) * 8


def _to_rows(a, rows):
    flat = a.reshape(-1).astype(F32)
    pad = rows * PACK_W - flat.shape[0]
    if pad:
        flat = jnp.pad(flat, (0, pad))
    return flat.reshape(rows, PACK_W)


def _b_to_rows(b):
    return jnp.transpose(b, (2, 0, 1)).reshape(SSM_P, SSM_H)


def _rows_to_b(r):
    return jnp.transpose(r.reshape(SSM_P, SSM_G, SSM_N), (1, 2, 0))


def _c_to_rows(cm):
    return jnp.transpose(cm, (1, 0, 2)).reshape(SSM_P, SSM_H)


def _rows_to_c(r):
    return jnp.transpose(r.reshape(SSM_P, SSM_G, SSM_N), (1, 0, 2))


def _pack(vals):
    out = jnp.zeros((PACK_ROWS, PACK_W), F32)
    for n, r in _PACK:
        if n in vals:
            pieces = vals[n] if isinstance(vals[n], list) else [vals[n]]
            rows_each = r // len(pieces)
            for i, piece in enumerate(pieces):
                out = lax.dynamic_update_slice(out, _to_rows(piece, rows_each),
                                               (_PACK_OFF[n][0] + i * rows_each, 0))
    return out


def _unpack(packed, name, shape):
    o, r = _PACK_OFF[name]
    n = int(np.prod(shape))
    return packed[o:o + r].reshape(-1)[:n].reshape(shape)


def _small_params_packed(p):
    return {
        "b_ada": p["b_ada"], "norm1_g": p["norm1_g"], "b_in": p["b_in"],
        "norm2_g": p["norm2_g"], "final_g": p["final_g"],
        "lambda_re": p["lambda_re"], "lambda_im": p["lambda_im"],
        "log_step": p["log_step"], "attn_sinks": p["attn_sinks"],
        "rel_bias": p["rel_bias"], "b_glu": p["b_glu"], "ssm_d": p["ssm_d"],
        "ssm_b_re": _b_to_rows(p["ssm_b_re"][0]), "ssm_b_im": _b_to_rows(p["ssm_b_im"][0]),
        "ssm_c_re": _c_to_rows(p["ssm_c_re"][0]), "ssm_c_im": _c_to_rows(p["ssm_c_im"][0]),
    }


_SMALL_SHAPES = {
    "b_ada": (1, N_MOD * D), "norm1_g": (1, D), "b_in": (1, IN_W), "norm2_g": (1, D),
    "final_g": (D,), "lambda_re": (1, SSM_G, SSM_N), "lambda_im": (1, SSM_G, SSM_N),
    "log_step": (1, SSM_G), "attn_sinks": (1, N_Q_HEADS), "rel_bias": (NUM_BUCKETS, N_Q_HEADS),
    "b_glu": (1, SSM_W), "ssm_d": (1, SSM_W),
}


def _unpack_small(packed, name):
    if name in ("ssm_b_re", "ssm_b_im"):
        o, r = _PACK_OFF[name]
        return _rows_to_b(packed[o:o + r])[None]
    if name in ("ssm_c_re", "ssm_c_im"):
        o, r = _PACK_OFF[name]
        return _rows_to_c(packed[o:o + r])[None]
    return _unpack(packed, name, _SMALL_SHAPES[name])


WEIGHT_ORDER = ['w_ada', 'b_ada', 'norm1_g', 'w_in', 'b_in', 'attn_sinks', 'rel_bias', 'lambda_re',
                'lambda_im', 'log_step', 'ssm_b_re', 'ssm_b_im', 'ssm_c_re', 'ssm_c_im', 'ssm_d',
                'w_glu', 'b_glu', 'w_attn_proj', 'w_ssm_proj', 'w_out', 'norm2_g', 'w_ff1', 'w_ff2',
                'final_g']
BIG = ['w_in', 'w_glu', 'w_attn_proj', 'w_ssm_proj', 'w_out', 'w_ff1', 'w_ff2']


ADAMW_TILE_ELEMS = 1 << 18


def _adamw_rows(rows, cols):
    tr = rows
    while tr * cols > ADAMW_TILE_ELEMS and tr % 32 == 0:
        tr //= 2
    return tr


def _cast_to_slot(w, me1, name):
    rows, cols = w.shape
    tr = min(rows, 256)

    def body(me_ref, w_ref, o_ref):
        o_ref[...] = w_ref[...].astype(BF16)

    return pl.pallas_call(
        body,
        grid_spec=pltpu.PrefetchScalarGridSpec(
            num_scalar_prefetch=1, grid=(rows // tr,),
            in_specs=[pl.BlockSpec((tr, cols), lambda i, me_ref: (i, 0))],
            out_specs=pl.BlockSpec((None, tr, cols), lambda i, me_ref: (me_ref[0], i, 0))),
        out_shape=jax.ShapeDtypeStruct((N_DEV, rows, cols), BF16),
        compiler_params=_cparams(("arbitrary",)),
        name=name,
    )(me1, w)


def kernel(x, c, w_ada, b_ada, norm1_g, w_in, b_in, attn_sinks, rel_bias, lambda_re, lambda_im, log_step, ssm_b_re, ssm_b_im, ssm_c_re, ssm_c_im, ssm_d, w_glu, b_glu, w_attn_proj, w_ssm_proj, w_out, norm2_g, w_ff1, w_ff2, final_g, loss_target, m_w_ada, m_b_ada, m_norm1_g, m_w_in, m_b_in, m_attn_sinks, m_rel_bias, m_lambda_re, m_lambda_im, m_log_step, m_ssm_b_re, m_ssm_b_im, m_ssm_c_re, m_ssm_c_im, m_ssm_d, m_w_glu, m_b_glu, m_w_attn_proj, m_w_ssm_proj, m_w_out, m_norm2_g, m_w_ff1, m_w_ff2, m_final_g, v_w_ada, v_b_ada, v_norm1_g, v_w_in, v_b_in, v_attn_sinks, v_rel_bias, v_lambda_re, v_lambda_im, v_log_step, v_ssm_b_re, v_ssm_b_im, v_ssm_c_re, v_ssm_c_im, v_ssm_d, v_w_glu, v_b_glu, v_w_attn_proj, v_w_ssm_proj, v_w_out, v_norm2_g, v_w_ff1, v_w_ff2, v_final_g):
    loc = dict(locals())
    W = {n: loc[n] for n in WEIGHT_ORDER}
    Mo = {n: loc["m_" + n] for n in WEIGHT_ORDER}
    Vo = {n: loc["v_" + n] for n in WEIGHT_ORDER}
    S = x.shape[1]
    TM = min(512, S)
    TS = min(1024, S)
    TR = min(256, S)
    me = 4 * lax.axis_index("x") + 2 * lax.axis_index("y") + lax.axis_index("c")
    x2d = x.reshape(S, D)
    tgt = loss_target.reshape(S, D)

    shard = {n: W[n][0] for n in BIG}
    me1 = jnp.reshape(me, (1,)).astype(jnp.int32)
    zone = {n: _cast_to_slot(shard[n], me1, "cast_" + n) for n in BIG}
    G = {"w_in": _allgather_inplace(zone["w_in"], "allgather_w_in")}
    later = [n for n in BIG if n != "w_in"]
    flights, tok_w = _exchange_start([zone[n] for n in later], "gather", "weights_start", G["w_in"])
    w_flight = dict(zip(later, flights))

    c_all = _small_allgather(c, "allgather_c").reshape(N_DEV, D)
    cs = _rowwise(lambda rv, vv, i, nt: ([rv[0] * _sigmoid(rv[0])], []), [_row(c_all)], [],
                  [(D, F32)], [], n_rows=N_DEV, tr=8, ch=8, name="silu_c")[0]
    n_ada = N_MOD * D // N_DEV
    b_ada_cols = lax.dynamic_slice(b_ada, (0, me * n_ada), (1, n_ada))
    mod_piece = _matmul(cs, w_ada[0], mode="nn", dims=(N_DEV, n_ada, D), tiles=(N_DEV, 512, D),
                        out_dtypes=[F32], name="ada_fwd", bias=b_ada_cols)
    mod_all = _small_allgather(mod_piece, "allgather_mod")
    mod_b = lax.dynamic_index_in_dim(mod_all, me, axis=1, keepdims=False).reshape(N_MOD, D)
    sh1, sc1, g1, sh2, sc2, g2 = [mod_b[i:i + 1] for i in range(N_MOD)]

    def f_norm1(rv, vv, i, nt):
        xv, (g, sc, sh) = rv[0], vv
        return [(xv * _rms(xv) * g) * (1.0 + sc) + sh], []

    h = _rowwise(f_norm1, [_row(x2d)], [norm1_g, sc1, sh1], [(D, BF16)], [],
                 n_rows=S, tr=TR, ch=32, name="norm1_fwd")[0]
    proj = _matmul(h, G["w_in"], mode="nn", dims=(S, IN_W, D), tiles=(TM, 768, D),
                   out_dtypes=[F32], name="in_proj", b3=True, bias=b_in, dep=tok_w)

    buckets = _t5_buckets_block()
    band = _band_mask()
    onehot_t = jnp.asarray(
        (np.arange(128)[:, None] == buckets.reshape(-1)[None, :]).astype(np.float32), BF16)
    band_first = band & (np.arange(2 * BLK)[None, :] >= BLK)
    rel_bias_t = jnp.pad(jnp.transpose(rel_bias), ((0, 0), (0, 128 - NUM_BUCKETS)))
    bias2 = _bias_tables(rel_bias_t, onehot_t,
                         jnp.asarray(band_first.reshape(1, -1).astype(np.float32)),
                         jnp.asarray(band.reshape(1, -1).astype(np.float32))
                         ).reshape(2, N_Q_HEADS * BLK, 2 * BLK)
    sinkcol = jnp.repeat(attn_sinks.reshape(N_Q_HEADS), BLK).reshape(N_Q_HEADS * BLK, 1)
    attn = _attention_fwd(proj, bias2, sinkcol, S)
    mixer = ["w_attn_proj", "w_glu", "w_ssm_proj", "w_out"]
    landed = _exchange_wait([w_flight[n] for n in mixer], "gather", attn, "weights_wait_mixer")
    G.update((n, bufs[0]) for n, bufs in zip(mixer, landed))
    w_glu_f = G["w_glu"].reshape(SSM_W, SSM_W)
    w_out_f = G["w_out"].reshape(D, D)
    y_attn = _matmul(attn, G["w_attn_proj"], mode="nn", dims=(S, D, ATTN_W), tiles=(TM, 256, ATTN_W),
                     out_dtypes=[F32], name="attn_proj", b3=True)

    lam_re = lambda_re.reshape(1, SSM_H)
    lam_im = lambda_im.reshape(1, SSM_H)
    ls_x = jnp.repeat(log_step.reshape(SSM_G), SSM_N).reshape(1, SSM_H)
    btr, bti = _b_to_rows(ssm_b_re[0]), _b_to_rows(ssm_b_im[0])
    ctr, cti = _c_to_rows(ssm_c_re[0]), _c_to_rows(ssm_c_im[0])
    abar, bcat, ccat = _ssm_setup(lam_re, lam_im, ls_x, btr, bti, ctr, cti)
    u_blk = (ATTN_W + 2 * KV_W) // SSM_W
    bu = _matmul(proj, bcat, mode="nn", dims=(S, 2 * SSM_H, SSM_W), tiles=(TM, 2048, SSM_W),
                 out_dtypes=[F32], name="ssm_bu", a_off=u_blk)
    xs, xprev = _scan_fwd(bu, abar, S)
    yc = _matmul(xs, ccat, mode="nt", dims=(S, SSM_W, 2 * SSM_H), tiles=(TM, SSM_W, 2 * SSM_H),
                 out_dtypes=[F32], name="ssm_cx")

    def f_ssm_out(rv, vv, i, nt):
        y = rv[0] + vv[0] * rv[1]
        return [y, _gelu(y)], []

    y_ssm_pre, z = _rowwise(f_ssm_out, [_row(yc), _row(proj, u_blk, SSM_W)], [ssm_d],
                            [(SSM_W, F32), (SSM_W, BF16)], [], n_rows=S, tr=TM, ch=32, name="ssm_out")
    zg = _matmul(z, w_glu_f, mode="nn", dims=(S, SSM_W, SSM_W), tiles=(TM, SSM_W, SSM_W),
                 out_dtypes=[F32], name="glu_proj", bias=b_glu)
    z2 = _rowwise(lambda rv, vv, i, nt: ([rv[0].astype(F32) * _sigmoid(rv[1])], []),
                  [_row(z), _row(zg)], [], [(SSM_W, BF16)], [], n_rows=S, tr=TM, ch=32, name="glu_gate")[0]
    y_ssm = _matmul(z2, G["w_ssm_proj"], mode="nn", dims=(S, D, SSM_W), tiles=(TM, 256, SSM_W),
                    out_dtypes=[F32], name="ssm_proj", b3=True)

    ga_row = _row(proj, 1, D)
    gs_row = _row(proj, 2, D)

    def f_merge(rv, vv, i, nt):
        ga, gs, ya, ys = rv
        return [_sigmoid(ga) * ya + _sigmoid(gs) * ys], []

    merged = _rowwise(f_merge, [ga_row, gs_row, _row(y_attn), _row(y_ssm)], [], [(D, BF16)], [],
                      n_rows=S, tr=TR, ch=32, name="merge")[0]
    mo = _matmul(merged, w_out_f, mode="nn", dims=(S, D, D), tiles=(TM, 1024, D),
                 out_dtypes=[F32], name="out_proj")

    def f_norm2(rv, vv, i, nt):
        xv, mv = rv
        g1v, g, sc, sh = vv
        x1v = xv + g1v * mv
        return [x1v, (x1v * _rms(x1v) * g) * (1.0 + sc) + sh], []

    x1, h2 = _rowwise(f_norm2, [_row(x2d), _row(mo)], [g1, norm2_g, sc2, sh2],
                      [(D, F32), (D, BF16)], [], n_rows=S, tr=TR, ch=32, name="norm2_fwd")

    def relu_sq(acc):
        r = jnp.maximum(acc, 0.0)
        return r * r, r

    G["w_ff1"] = _exchange_wait([w_flight["w_ff1"]], "gather", h2, "weights_wait_ff1")[0][0]
    act, relu = _matmul(h2, G["w_ff1"], mode="nn", dims=(S, D_FF, D), tiles=(TM, 1024, D),
                        out_dtypes=[BF16, BF16], name="ff1", b3=True, epilogue=relu_sq)
    w_ff2_f = _exchange_wait([w_flight["w_ff2"]], "gather", act, "weights_wait_ff2")[0][0].reshape(D_FF, D)
    ff = _matmul(act, w_ff2_f, mode="nn", dims=(S, D, D_FF), tiles=(TM, 1024, 2048),
                 out_dtypes=[F32], name="ff2")

    def f_loss(rv, vv, i, nt):
        x1v, ffv, tv = rv
        g2v, gf = vv
        x2v = x1v + g2v * ffv
        r = _rms(x2v)
        xh = x2v * r
        diff = xh * gf - tv
        dy = diff * (1.0 / D)
        dxh = dy * gf
        dx2 = r * (dxh - xh * jnp.mean(dxh * xh, axis=-1, keepdims=True))
        return [dx2, dx2 * g2v], [_colsum(0.5 * diff * diff * (1.0 / D)), _colsum(dy * xh),
                                  _colsum(dx2 * ffv)]

    dx2, dff, loss_cols, d_final_g, dg2 = _rowwise(
        f_loss, [_row(x1), _row(ff), _row(tgt)], [g2, final_g.reshape(1, D)],
        [(D, F32), (D, BF16)], [(1, D)] * 3, n_rows=S, tr=TR, ch=32, name="loss_bwd")

    df1 = _matmul(dff, w_ff2_f, mode="nt", dims=(S, D_FF, D), tiles=(TM, 1024, D),
                  out_dtypes=[BF16], name="ff2_dgrad", extras=(relu,),
                  epilogue=lambda acc, r: (acc * (2.0 * r.astype(F32)),))
    gw_ff2 = _matmul(act, dff, mode="tn", dims=(D_FF, D, S), tiles=(1024, 1024, TS),
                     out_dtypes=[BF16], name="ff2_wgrad").reshape(N_DEV, D_FF // N_DEV, D)
    g_flight = {}
    (g_flight["w_ff2"],), tok = _exchange_start([gw_ff2], "scatter", "grads_start_ff2")
    dh2 = _matmul(df1, G["w_ff1"], mode="nt", dims=(S, D, D_FF), tiles=(TM, D, 1024),
                  out_dtypes=[F32], name="ff1_dgrad", b3=True, dep=tok)
    gw_ff1 = _matmul(h2, df1, mode="tn", dims=(D, D_FF, S), tiles=(1024, 1024, TS),
                     out_dtypes=[BF16], name="ff1_wgrad", out3=True)
    (g_flight["w_ff1"],), tok = _exchange_start([gw_ff1], "scatter", "grads_start_ff1")

    def f_norm2_bwd(rv, vv, i, nt):
        x1v, dh, dx2v, mv = rv
        g, sc, g1v = vv
        r = _rms(x1v)
        xh = x1v * r
        t = xh * g
        dt = dh * (1.0 + sc)
        dxh = dt * g
        dx1 = dx2v + r * (dxh - xh * jnp.mean(dxh * xh, axis=-1, keepdims=True))
        return [dx1, dx1 * g1v], [_colsum(dh), _colsum(dh * t), _colsum(dt * xh), _colsum(dx1 * mv)]

    dx1, dmo, dsh2, dsc2, d_norm2_g, dg1 = _rowwise(
        f_norm2_bwd, [_row(x1), _row(dh2), _row(dx2), _row(mo)], [norm2_g, sc2, g1],
        [(D, F32), (D, BF16)], [(1, D)] * 4, n_rows=S, tr=TR, ch=16, name="norm2_bwd", dep=tok)

    dmerged = _matmul(dmo, w_out_f, mode="nt", dims=(S, D, D), tiles=(TM, 1024, D),
                      out_dtypes=[F32], name="out_dgrad")
    gw_out = _matmul(merged, dmo, mode="tn", dims=(D, D, S), tiles=(1024, 1024, TS),
                     out_dtypes=[BF16], name="out_wgrad").reshape(N_DEV, D // N_DEV, D)
    (g_flight["w_out"],), tok = _exchange_start([gw_out], "scatter", "grads_start_out")

    def f_merge_bwd(rv, vv, i, nt):
        dm, ga, gs, ya, ys = rv
        sa, ss = _sigmoid(ga), _sigmoid(gs)
        return [dm * sa, dm * ss, dm * ya * sa * (1.0 - sa), dm * ys * ss * (1.0 - ss)], []

    dy_attn, dy_ssm, dga, dgs = _rowwise(
        f_merge_bwd, [_row(dmerged), ga_row, gs_row, _row(y_attn), _row(y_ssm)], [],
        [(D, BF16)] * 4, [], n_rows=S, tr=TR, ch=16, name="merge_bwd", dep=tok)

    dz2 = _matmul(dy_ssm, G["w_ssm_proj"], mode="nt", dims=(S, SSM_W, D), tiles=(TM, SSM_W, 256),
                  out_dtypes=[F32], name="ssm_proj_dgrad", b3=True)
    gw_ssm_proj = _matmul(z2, dy_ssm, mode="tn", dims=(SSM_W, D, S), tiles=(SSM_W, 256, TS),
                          out_dtypes=[BF16], name="ssm_proj_wgrad", out3=True)

    def f_glu_bwd(rv, vv, i, nt):
        dz2v, zv, zgv = rv
        sg = _sigmoid(zgv)
        dzg = dz2v * zv.astype(F32) * sg * (1.0 - sg)
        return [dzg, dz2v * sg], [_colsum(dzg)]

    dzg, dz_a, d_b_glu = _rowwise(f_glu_bwd, [_row(dz2), _row(z), _row(zg)], [],
                                  [(SSM_W, BF16), (SSM_W, F32)], [(1, SSM_W)],
                                  n_rows=S, tr=TM, ch=32, name="glu_bwd")
    dz_b = _matmul(dzg, w_glu_f, mode="nt", dims=(S, SSM_W, SSM_W), tiles=(TM, SSM_W, SSM_W),
                   out_dtypes=[F32], name="glu_dgrad")
    gw_glu = _matmul(z, dzg, mode="tn", dims=(SSM_W, SSM_W, S), tiles=(SSM_W, SSM_W, TS),
                     out_dtypes=[BF16], name="glu_wgrad").reshape(N_DEV, SSM_W // N_DEV, SSM_W)
    (g_flight["w_ssm_proj"], g_flight["w_glu"]), tok = _exchange_start(
        [gw_ssm_proj, gw_glu], "scatter", "grads_start_ssm")

    def f_ssm_out_bwd(rv, vv, i, nt):
        dza, dzb, yv, uv = rv
        dy = (dza + dzb) * _gelu_grad(yv)
        return [dy, dy * vv[0]], [_colsum(dy * uv)]

    dy_s, du_a, d_ssm_d = _rowwise(
        f_ssm_out_bwd, [_row(dz_a), _row(dz_b), _row(y_ssm_pre), _row(proj, u_blk, SSM_W)], [ssm_d],
        [(SSM_W, BF16), (SSM_W, F32)], [(1, SSM_W)], n_rows=S, tr=TM, ch=32, name="ssm_out_bwd", dep=tok)
    gx = _matmul(dy_s, ccat, mode="nn", dims=(S, 2 * SSM_H, SSM_W), tiles=(TM, 2048, SSM_W),
                 out_dtypes=[F32], name="ssm_cx_dgrad")
    dccat = _matmul(dy_s, xs, mode="tn", dims=(SSM_W, 2 * SSM_H, S), tiles=(SSM_W, 2048, TS),
                    out_dtypes=[F32], name="ssm_c_wgrad")
    hs, dacc = _scan_bwd(gx, xprev, abar, S)
    du_b = _matmul(hs, bcat, mode="nt", dims=(S, SSM_W, 2 * SSM_H), tiles=(TM, SSM_W, 2 * SSM_H),
                   out_dtypes=[F32], name="ssm_bu_dgrad")
    dbcat = _matmul(proj, hs, mode="tn", dims=(SSM_W, 2 * SSM_H, S), tiles=(SSM_W, 2048, TS),
                    out_dtypes=[F32], name="ssm_b_wgrad", a_off=u_blk)
    grp = np.arange(SSM_H) // SSM_N
    gind = jnp.asarray((grp[:, None] == np.arange(128)[None, :]).astype(np.float32), BF16)
    d_lam_re, d_lam_im, d_ls, d_btr, d_bti, d_ctr, d_cti = _ssm_param_bwd(
        lam_re, lam_im, ls_x, btr, bti, dacc, dbcat, dccat, gind)

    dattn = _matmul(dy_attn, G["w_attn_proj"], mode="nt", dims=(S, ATTN_W, D), tiles=(TM, ATTN_W, 256),
                    out_dtypes=[BF16], name="attn_proj_dgrad", b3=True)
    gw_attn_proj = _matmul(attn, dy_attn, mode="tn", dims=(ATTN_W, D, S), tiles=(ATTN_W, 256, TS),
                           out_dtypes=[BF16], name="attn_proj_wgrad", out3=True)
    (g_flight["w_attn_proj"],), tok = _exchange_start(
        [gw_attn_proj], "scatter", "grads_start_attn")
    dq, dkc, dkp, dvc, dvp, dbias, dsink = _attention_bwd(proj, attn, dattn, bias2, sinkcol, S)
    d_bias_b, d_sinks = _bucket_reduce(dbias.reshape(N_Q_HEADS, BLK * 2 * BLK),
                                       dsink.reshape(N_Q_HEADS, BLK), onehot_t)

    def f_dproj(rv, vv, i, nt):
        dqv, kc, kp, vc, vp, dua, dub, gav, gsv = rv
        keep = (i < nt - 1).astype(F32)
        dp = jnp.concatenate([dqv.astype(F32), kc + keep * kp, vc + keep * vp, dua + dub,
                              gav.astype(F32), gsv.astype(F32)], axis=-1)
        return [dp], [_colsum(dp)]

    dproj, d_b_in = _rowwise(
        f_dproj, [_row(dq), _row(dkc), _row(dkp, shift=1), _row(dvc), _row(dvp, shift=1),
                  _row(du_a), _row(du_b), _row(dga), _row(dgs)], [],
        [(IN_W, BF16)], [(1, IN_W)], n_rows=S, tr=BLK, ch=16, name="dproj", dep=tok)
    gw_in = _matmul(h, dproj, mode="tn", dims=(D, IN_W, S), tiles=(1024, 768, TS),
                    out_dtypes=[BF16], name="in_wgrad", out3=True)
    (g_flight["w_in"],), tok = _exchange_start([gw_in], "scatter", "grads_start_in")
    dh = _matmul(dproj, G["w_in"], mode="nt", dims=(S, D, IN_W), tiles=(TM, D, 768),
                 out_dtypes=[F32], name="in_dgrad", b3=True, dep=tok)

    def f_norm1_bwd(rv, vv, i, nt):
        xv, dhv, dx1v = rv
        g, sc = vv
        r = _rms(xv)
        xh = xv * r
        t = xh * g
        dt = dhv * (1.0 + sc)
        dxh = dt * g
        dxv = dx1v + r * (dxh - xh * jnp.mean(dxh * xh, axis=-1, keepdims=True))
        return [dxv], [_colsum(dhv), _colsum(dhv * t), _colsum(dt * xh)]

    grad_x, dsh1, dsc1, d_norm1_g = _rowwise(
        f_norm1_bwd, [_row(x2d), _row(dh), _row(dx1)], [norm1_g, sc1],
        [(D, F32)], [(1, D)] * 3, n_rows=S, tr=TR, ch=32, name="norm1_bwd")

    part = _pack({
        "b_ada": [dsh1, dsc1, dg1, dsh2, dsc2, dg2], "norm1_g": d_norm1_g, "b_in": d_b_in, "norm2_g": d_norm2_g,
        "final_g": d_final_g, "lambda_re": d_lam_re, "lambda_im": d_lam_im,
        "log_step": d_ls[0, :SSM_G], "attn_sinks": d_sinks[:, 0],
        "rel_bias": jnp.transpose(d_bias_b[:, :NUM_BUCKETS]), "b_glu": d_b_glu, "ssm_d": d_ssm_d,
        "loss": loss_cols, "ssm_b_re": d_btr, "ssm_b_im": d_bti, "ssm_c_re": d_ctr, "ssm_c_im": d_cti,
    })
    part_all = _small_allgather(part, "allgather_small_grads")
    wp, mp, vp = [_pack(_small_params_packed(p)) for p in (W, Mo, Vo)]
    sg, sdelta, sm, sv = _adamw([(part_all, d) for d in range(N_DEV)], wp, mp, vp,
                                tr=PACK_ROWS, ch=8, name="adamw_small")
    lo, _ = _PACK_OFF["loss"]
    loss = jnp.sum(sg[lo])

    o_ada, _ = _PACK_OFF["b_ada"]
    dmod_all = part_all[:, o_ada:o_ada + N_MOD, :].reshape(N_DEV, N_MOD * D)
    dmod_cols = lax.dynamic_slice(dmod_all, (0, me * n_ada), (N_DEV, n_ada))
    gw_ada = _matmul(cs, dmod_cols, mode="tn", dims=(D, n_ada, N_DEV), tiles=(D, 512, N_DEV),
                     out_dtypes=[F32], name="ada_wgrad")

    big_out = {"w_ada": _adamw([(gw_ada, 0)], w_ada[0], m_w_ada[0], v_w_ada[0],
                               tr=_adamw_rows(D, n_ada), ch=16, name="adamw_w_ada")}
    after = big_out["w_ada"][0]
    for n in ["w_ff2", "w_ff1", "w_out", "w_ssm_proj", "w_glu", "w_attn_proj", "w_in"]:
        own, recv = _exchange_wait([g_flight[n]], "scatter", after, "grads_wait_" + n[2:])[0]
        rows, cols = shard[n].shape
        parts = [(own, lambda m: m[0])] + [
            (recv, lambda m, j=j: jnp.where(j >= m[0], j + 1, j)) for j in range(N_DEV - 1)]
        big_out[n] = _adamw(parts, shard[n], Mo[n][0], Vo[n][0], tr=_adamw_rows(rows, cols), ch=16,
                            name="adamw_" + n, prefetch=me1)
        after = big_out[n][0]

    def leaf(kind, n):
        if n in big_out:
            return big_out[n][kind][None]
        return _unpack_small((sg, sdelta, sm, sv)[kind], n)

    outs = [loss, grad_x.reshape(1, S, D)]
    for kind in range(4):
        outs.extend(leaf(kind, n) for n in WEIGHT_ORDER)
    return tuple(outs)
```

```python
import functools
import math

import numpy as np
import jax
import jax.numpy as jnp
from jax import lax
from jax.experimental import pallas as pl
from jax.experimental.pallas import tpu as pltpu

F32 = jnp.float32
BF16 = jnp.bfloat16
MESH = pl.DeviceIdType.MESH

N_DEV = 8
D = 2048
HEAD_DIM = 64
N_Q_HEADS = 16
N_KV_HEADS = 4
GROUP = N_Q_HEADS // N_KV_HEADS
ATTN_W = N_Q_HEADS * HEAD_DIM
KV_W = N_KV_HEADS * HEAD_DIM
BLK = 128
NUM_BUCKETS = 32
MAX_DISTANCE = 128
NEG_INF = -1e30
SSM_W = 512
SSM_P = 16
SSM_G = 32
SSM_N = 64
SSM_H = SSM_G * SSM_N
D_FF = 4 * D
IN_W = ATTN_W + 2 * KV_W + SSM_W + 2 * D
N_MOD = 6
EPS = 1e-6

ADAM_LR = 0.001
ADAM_B1 = 0.9
ADAM_B2 = 0.999
ADAM_EPS = 1e-08
ADAM_WD = 0.01
ADAM_STEP = 10

VMEM_LIMIT = 56 * 1024 * 1024
PACK_W = 2048


def _cparams(sem):
    return pltpu.CompilerParams(dimension_semantics=sem, vmem_limit_bytes=VMEM_LIMIT)


def _matmul(a, b, *, mode, dims, tiles, out_dtypes, name, a_off=0, b3=False,
            out3=False, bias=None, extras=(), epilogue=None, dep=None, a_index=None, b_index=None):
    M, N, K = dims
    tm, tn, tk = tiles
    assert M % tm == 0 and N % tn == 0 and K % tk == 0, (name, dims, tiles)
    gm, gn, gk = M // tm, N // tn, K // tk
    n_extra = len(extras)
    has_bias = bias is not None
    n_out = len(out_dtypes)

    if mode == "nn":
        a_spec = pl.BlockSpec((tm, tk), lambda i, j, k: (i, a_off + k))
        if b3:
            nb = (N // N_DEV) // tn
            assert nb * tn * N_DEV == N
            b_spec = pl.BlockSpec((None, tk, tn), lambda i, j, k: (j // nb, k, j % nb))
        else:
            b_spec = pl.BlockSpec((tk, tn), lambda i, j, k: (k, j))
        dn = (((1,), (0,)), ((), ()))
    elif mode == "nt":
        a_spec = pl.BlockSpec((tm, tk), lambda i, j, k: (i, a_off + k))
        if b3:
            nb = (K // N_DEV) // tk
            assert nb * tk * N_DEV == K
            b_spec = pl.BlockSpec((None, tn, tk), lambda i, j, k: (k // nb, j, k % nb))
        else:
            b_spec = pl.BlockSpec((tn, tk), lambda i, j, k: (j, k))
        dn = (((1,), (1,)), ((), ()))
    else:
        a_spec = pl.BlockSpec((tk, tm), lambda i, j, k: (k, a_off + i))
        b_spec = pl.BlockSpec((tk, tn), lambda i, j, k: (k, j))
        dn = (((0,), (0,)), ((), ()))
    if a_index is not None:
        a_spec = pl.BlockSpec(a_spec.block_shape, a_index)
    if b_index is not None:
        b_spec = pl.BlockSpec(b_spec.block_shape, b_index)

    if out3:
        nbo = (N // N_DEV) // tn
        assert nbo * tn * N_DEV == N
        o_spec = pl.BlockSpec((None, tm, tn), lambda i, j, k: (j // nbo, i, j % nbo))
        o_shape = (N_DEV, M, N // N_DEV)
    else:
        o_spec = pl.BlockSpec((tm, tn), lambda i, j, k: (i, j))
        o_shape = (M, N)

    in_specs = [a_spec, b_spec]
    args = [a, b]
    if has_bias:
        in_specs.append(pl.BlockSpec((1, tn), lambda i, j, k: (0, j)))
        args.append(bias)
    for e in extras:
        in_specs.append(pl.BlockSpec((tm, tn), lambda i, j, k: (i, j)))
        args.append(e)
    n_dep = 0 if dep is None else 1
    if n_dep:
        in_specs.append(pl.BlockSpec(memory_space=pl.ANY))
        args.append(dep)

    def body(*refs):
        a_ref, b_ref = refs[0], refs[1]
        pos = 2
        bias_ref = None
        if has_bias:
            bias_ref = refs[pos]
            pos += 1
        extra_refs = refs[pos:pos + n_extra]
        pos += n_extra + n_dep
        out_refs = refs[pos:pos + n_out]
        acc_ref = refs[pos + n_out] if gk > 1 else None

        part = lax.dot_general(a_ref[...].astype(BF16), b_ref[...].astype(BF16), dn,
                               preferred_element_type=F32)

        def finish(acc):
            if has_bias:
                acc = acc + bias_ref[...]
            if epilogue is None:
                vals = (acc,)
            else:
                vals = epilogue(acc, *[e[...] for e in extra_refs])
            for o_ref, val in zip(out_refs, vals):
                o_ref[...] = val.astype(o_ref.dtype)

        if gk == 1:
            finish(part)
        else:
            k = pl.program_id(2)

            @pl.when(k == 0)
            def _():
                acc_ref[...] = part

            @pl.when(k > 0)
            def _():
                acc_ref[...] += part

            @pl.when(k == gk - 1)
            def _():
                finish(acc_ref[...])

    outs = pl.pallas_call(
        body,
        grid=(gm, gn, gk),
        in_specs=in_specs,
        out_specs=[o_spec] * n_out,
        out_shape=[jax.ShapeDtypeStruct(o_shape, dt) for dt in out_dtypes],
        scratch_shapes=([pltpu.VMEM((tm, tn), F32)] if gk > 1 else []),
        compiler_params=_cparams(("parallel", "parallel", "arbitrary")),
        name=name,
    )(*args)
    return outs[0] if n_out == 1 else outs


def _rowwise(fn, rows, vecs, row_outs, sum_outs, *, n_rows, tr, ch, name, dep=None, prefetch=None):
    assert n_rows % tr == 0 and tr % ch == 0
    nt = n_rows // tr
    nr, nv, nro, nso = len(rows), len(vecs), len(row_outs), len(sum_outs)
    in_specs, args = [], []
    n_pf = 0 if prefetch is None else 1
    for (arr, lead, cblk, w, shift) in rows:
        if shift:
            ridx = lambda i, shift=shift: jnp.minimum(i + shift, nt - 1)
        else:
            ridx = lambda i: i
        if arr.ndim == 3:
            def imap(i, *pf, lead=lead, cblk=cblk, ridx=ridx):
                return (lead(pf[0]) if callable(lead) else lead, ridx(i), cblk)
            in_specs.append(pl.BlockSpec((None, tr, w), imap))
        else:
            in_specs.append(pl.BlockSpec(
                (tr, w), lambda i, *pf, cblk=cblk, ridx=ridx: (ridx(i), cblk)))
        args.append(arr)
    for v in vecs:
        in_specs.append(pl.BlockSpec(v.shape, lambda i, *pf, nd=v.ndim: (0,) * nd))
        args.append(v)
    n_dep = 0 if dep is None else 1
    if n_dep:
        in_specs.append(pl.BlockSpec(memory_space=pl.ANY))
        args.append(dep)
    out_specs = [pl.BlockSpec((tr, w), lambda i, *pf: (i, 0)) for (w, _) in row_outs]
    out_shape = [jax.ShapeDtypeStruct((n_rows, w), dt) for (w, dt) in row_outs]
    for (r, w) in sum_outs:
        out_specs.append(pl.BlockSpec((r, w), lambda i, *pf: (0, 0)))
        out_shape.append(jax.ShapeDtypeStruct((r, w), F32))

    def body(*refs):
        refs = refs[n_pf:]
        i = pl.program_id(0)
        r_in = refs[:nr]
        v_in = refs[nr:nr + nv]
        r_out = refs[nr + nv + n_dep:nr + nv + n_dep + nro]
        s_out = refs[nr + nv + n_dep + nro:]
        if nso:
            @pl.when(i == 0)
            def _():
                for s in s_out:
                    s[...] = jnp.zeros(s.shape, F32)
        vvals = [v[...] for v in v_in]

        def chunk(ci, carry):
            r0 = pl.multiple_of(ci * ch, ch)
            rv = [r[pl.ds(r0, ch), :] for r in r_in]
            ro, so = fn(rv, vvals, i, nt)
            for ref, val in zip(r_out, ro):
                ref[pl.ds(r0, ch), :] = val.astype(ref.dtype)
            for ref, val in zip(s_out, so):
                ref[...] += val
            return carry

        lax.fori_loop(0, tr // ch, chunk, 0)

    outs = pl.pallas_call(
        body,
        grid_spec=pltpu.PrefetchScalarGridSpec(
            num_scalar_prefetch=n_pf, grid=(nt,), in_specs=in_specs, out_specs=out_specs),
        out_shape=out_shape,
        compiler_params=_cparams(("arbitrary",)),
        name=name,
    )(*([prefetch] if n_pf else []), *args)
    return outs


def _row(arr, cblk=0, w=None, lead=0, shift=0):
    return (arr, lead, cblk, arr.shape[-1] if w is None else w, shift)


def _colsum(v):
    return jnp.sum(v, axis=0, keepdims=True)


def _rms(x):
    return lax.rsqrt(jnp.mean(x * x, axis=-1, keepdims=True) + EPS)


def _sigmoid(x):
    return 1.0 / (1.0 + jnp.exp(-x))


_GELU_C = math.sqrt(2.0 / math.pi)


def _gelu(x):
    return 0.5 * x * (1.0 + jnp.tanh(_GELU_C * (x + 0.044715 * (x * x * x))))


def _gelu_grad(x):
    t = jnp.tanh(_GELU_C * (x + 0.044715 * (x * x * x)))
    return 0.5 * (1.0 + t) + 0.5 * x * (1.0 - t * t) * (_GELU_C * (1.0 + 3.0 * 0.044715 * (x * x)))


def _my_pos():
    return lax.axis_index("x"), lax.axis_index("y"), lax.axis_index("c")


def _flip(pos, k):
    x, y, c = pos
    return (1 - x if k & 4 else x, 1 - y if k & 2 else y, 1 - c if k & 1 else c)


def _dev_id(pos):
    return 4 * pos[0] + 2 * pos[1] + pos[2]


def _small_allgather(x, name):
    r, c = x.shape

    def body(x_ref, out_ref, send_sems, recv_sems):
        me = _my_pos()
        out_ref[_dev_id(me)] = x_ref[...]
        copies = []
        for k in range(1, N_DEV):
            cp = pltpu.make_async_remote_copy(
                src_ref=x_ref, dst_ref=out_ref.at[_dev_id(me)],
                send_sem=send_sems.at[k - 1], recv_sem=recv_sems.at[k - 1],
                device_id=_flip(me, k), device_id_type=MESH)
            cp.start()
            copies.append(cp)
        for k in range(1, N_DEV):
            peer = _flip(me, k)
            pltpu.make_async_remote_copy(
                src_ref=x_ref, dst_ref=out_ref.at[_dev_id(peer)],
                send_sem=send_sems.at[k - 1], recv_sem=recv_sems.at[k - 1],
                device_id=peer, device_id_type=MESH).wait_recv()
        for cp in copies:
            cp.wait_send()

    return pl.pallas_call(
        body,
        out_shape=jax.ShapeDtypeStruct((N_DEV, r, c), x.dtype),
        in_specs=[pl.BlockSpec(memory_space=pltpu.VMEM)],
        out_specs=pl.BlockSpec(memory_space=pltpu.VMEM),
        scratch_shapes=[pltpu.SemaphoreType.DMA((N_DEV - 1,)),
                        pltpu.SemaphoreType.DMA((N_DEV - 1,))],
        compiler_params=pltpu.CompilerParams(vmem_limit_bytes=VMEM_LIMIT),
        name=name,
    )(x)


def _allgather_inplace(buf, name):
    def body(_, out, send_sems, recv_sems):
        x, y, c = _my_pos()
        me, sib = (x, y, c), (x, y, 1 - c)
        chips = [(1 - x, y), (x, 1 - y), (1 - x, 1 - y)]

        def copy(k, block, to):
            slot = out.at[_dev_id(block)]
            return pltpu.make_async_remote_copy(
                src_ref=slot, dst_ref=slot, send_sem=send_sems.at[k], recv_sem=recv_sems.at[k],
                device_id=to, device_id_type=MESH)

        first = [copy(0, me, sib)] + [copy(1 + j, me, (*chip, c)) for j, chip in enumerate(chips)]
        for cp in first:
            cp.start()
        passed = []
        for j, chip in enumerate(chips):
            copy(1 + j, (*chip, c), me).wait_recv()
            cp = copy(4 + j, (*chip, c), sib)
            cp.start()
            passed.append(cp)
        copy(0, sib, me).wait_recv()
        for j, chip in enumerate(chips):
            copy(4 + j, (*chip, 1 - c), me).wait_recv()
        for cp in first + passed:
            cp.wait_send()

    hbm = pl.BlockSpec(memory_space=pl.ANY)
    return pl.pallas_call(
        body,
        out_shape=jax.ShapeDtypeStruct(buf.shape, buf.dtype),
        in_specs=[hbm],
        out_specs=hbm,
        input_output_aliases={0: 0},
        scratch_shapes=[pltpu.SemaphoreType.DMA((7,)), pltpu.SemaphoreType.DMA((7,))],
        name=name,
    )(buf)


_HBM = pl.BlockSpec(memory_space=pltpu.HBM)
_SEM = pl.BlockSpec(memory_space=pltpu.SEMAPHORE)
_EFFECT = pltpu.SideEffectType.DATAFLOW_SIDE_EFFECTING


def _exchange_copy(kind, bufs, send_sems, recv_sems, me, k, arriving):
    peer = _flip(me, k)
    my_id, peer_id = _dev_id(me), _dev_id(peer)
    if kind == "gather":
        slot = bufs[0].at[peer_id if arriving else my_id]
        src, dst = slot, slot
    else:
        src = bufs[0].at[my_id if arriving else peer_id]
        dst = bufs[1].at[peer_id if arriving else my_id]
    return pltpu.make_async_remote_copy(
        src_ref=src, dst_ref=dst, send_sem=send_sems.at[k - 1], recv_sem=recv_sems.at[k - 1],
        device_id=peer, device_id_type=MESH)


def _exchange_start(arrays, kind, name, after=None):
    n = len(arrays)
    n_after = 0 if after is None else 1
    if kind == "gather":
        bufs = [[a] for a in arrays]
    else:
        bufs = [[a, lax.empty(a.shape, a.dtype)] for a in arrays]
    nb = len(bufs[0])
    flat = [b for group in bufs for b in group]

    def body(*refs):
        outs_at = nb * n + n_after
        send = refs[outs_at:outs_at + n]
        recv = refs[outs_at + n:outs_at + 2 * n]
        token = refs[outs_at + 2 * n + nb * n]
        me = _my_pos()
        for a in range(n):
            for k in range(1, N_DEV):
                _exchange_copy(kind, refs[nb * a:nb * (a + 1)], send[a], recv[a], me, k, False).start()
        token[...] = jnp.zeros(token.shape, token.dtype)

    sem = pltpu.SemaphoreType.DMA((N_DEV - 1,))
    outs = pl.pallas_call(
        body,
        name=name,
        out_shape=([sem] * (2 * n) + [pltpu.HBM(b.shape, b.dtype) for b in flat]
                   + [jax.ShapeDtypeStruct((8, 128), F32)]),
        in_specs=[_HBM] * (nb * n) + [pl.BlockSpec(memory_space=pl.ANY)] * n_after,
        out_specs=[_SEM] * (2 * n) + [_HBM] * (nb * n) + [pl.BlockSpec(memory_space=pltpu.VMEM)],
        input_output_aliases={i: 2 * n + i for i in range(nb * n)},
        compiler_params=pltpu.CompilerParams(has_side_effects=_EFFECT),
    )(*[pltpu.with_memory_space_constraint(b, pltpu.HBM) for b in flat],
      *([after] if n_after else []))
    flights = [(outs[a], outs[n + a], list(outs[2 * n + nb * a:2 * n + nb * (a + 1)]))
               for a in range(n)]
    return flights, outs[2 * n + nb * n]


def _exchange_wait(flights, kind, after, name):
    n = len(flights)
    nb = len(flights[0][2])
    flat = [b for f in flights for b in f[2]]

    def body(*refs):
        send = refs[nb * n:nb * n + n]
        recv = refs[nb * n + n:nb * n + 2 * n]
        me = _my_pos()
        for a in range(n):
            for k in range(1, N_DEV):
                bufs = refs[nb * a:nb * (a + 1)]
                _exchange_copy(kind, bufs, send[a], recv[a], me, k, False).wait_send()
                _exchange_copy(kind, bufs, send[a], recv[a], me, k, True).wait_recv()

    outs = pl.pallas_call(
        body,
        name=name,
        out_shape=[pltpu.HBM(b.shape, b.dtype) for b in flat],
        in_specs=[_HBM] * (nb * n) + [_SEM] * (2 * n) + [pl.BlockSpec(memory_space=pl.ANY)],
        out_specs=[_HBM] * (nb * n),
        input_output_aliases={i: i for i in range(nb * n)},
        compiler_params=pltpu.CompilerParams(has_side_effects=_EFFECT),
    )(*flat, *[f[0] for f in flights], *[f[1] for f in flights], after)
    return [list(outs[nb * a:nb * (a + 1)]) for a in range(n)]


def _t5_buckets_block():
    qi = np.arange(BLK)[:, None]
    ki = np.arange(2 * BLK)[None, :]
    n = np.maximum(qi + BLK - ki, 0)
    max_exact = NUM_BUCKETS // 2
    large = max_exact + (np.log(np.maximum(n, 1) / max_exact)
                         / np.log(MAX_DISTANCE / max_exact)
                         * (NUM_BUCKETS - max_exact)).astype(np.int32)
    large = np.minimum(large, NUM_BUCKETS - 1)
    return np.where(n < max_exact, n, large).astype(np.int32)


def _band_mask():
    qi = np.arange(BLK)[:, None]
    ki = np.arange(2 * BLK)[None, :]
    dist = qi + BLK - ki
    return (dist >= 0) & (dist < BLK)


def _attn_scores(q_ref, kp_ref, kc_ref, hkv):
    c0 = hkv * HEAD_DIM
    kk = jnp.concatenate([kp_ref[:, c0:c0 + HEAD_DIM], kc_ref[:, c0:c0 + HEAD_DIM]],
                         axis=0).astype(BF16)
    qg = jnp.concatenate(
        [q_ref[:, (hkv * GROUP + g) * HEAD_DIM:(hkv * GROUP + g + 1) * HEAD_DIM]
         for g in range(GROUP)], axis=0).astype(BF16)
    s = lax.dot_general(qg, kk, (((1,), (1,)), ((), ())), preferred_element_type=F32)
    return qg, kk, s


def _attn_softmax(s, bias_ref, sink_ref, hkv):
    r0, r1 = hkv * GROUP * BLK, (hkv + 1) * GROUP * BLK
    s = s * (HEAD_DIM ** -0.5) + bias_ref[r0:r1, :]
    sink = sink_ref[r0:r1, :]
    m = jnp.maximum(jnp.max(s, axis=-1, keepdims=True), sink)
    p = jnp.exp(s - m)
    e_sink = jnp.exp(sink - m)
    inv = 1.0 / (jnp.sum(p, axis=-1, keepdims=True) + e_sink)
    return p * inv, e_sink * inv


def _kv_rows(p_ref, c_ref, hkv):
    c0 = hkv * HEAD_DIM
    return jnp.concatenate([p_ref[:, c0:c0 + HEAD_DIM], c_ref[:, c0:c0 + HEAD_DIM]],
                           axis=0).astype(BF16)


def _attn_in_specs(bias2):
    prev = lambda n: jnp.maximum(n - 1, 0)
    return [
        pl.BlockSpec((BLK, ATTN_W), lambda n: (n, 0)),
        pl.BlockSpec((BLK, KV_W), lambda n: (prev(n), ATTN_W // KV_W)),
        pl.BlockSpec((BLK, KV_W), lambda n: (n, ATTN_W // KV_W)),
        pl.BlockSpec((BLK, KV_W), lambda n: (prev(n), ATTN_W // KV_W + 1)),
        pl.BlockSpec((BLK, KV_W), lambda n: (n, ATTN_W // KV_W + 1)),
        pl.BlockSpec((None,) + bias2.shape[1:], lambda n: (jnp.minimum(n, 1), 0, 0)),
    ]


def _attention_fwd(proj, bias2, sinkcol, n_rows):
    nb = n_rows // BLK

    def body(q_ref, kp_ref, kc_ref, vp_ref, vc_ref, bias_ref, sink_ref, o_ref):
        heads = range(N_KV_HEADS)
        scores = [_attn_scores(q_ref, kp_ref, kc_ref, hkv)[2] for hkv in heads]
        probs = [_attn_softmax(scores[hkv], bias_ref, sink_ref, hkv)[0] for hkv in heads]
        outs = [jnp.dot(probs[hkv].astype(BF16), _kv_rows(vp_ref, vc_ref, hkv),
                        preferred_element_type=F32) for hkv in heads]
        for hkv in heads:
            for g in range(GROUP):
                h = hkv * GROUP + g
                o_ref[:, h * HEAD_DIM:(h + 1) * HEAD_DIM] = (
                    outs[hkv][g * BLK:(g + 1) * BLK, :].astype(o_ref.dtype))

    return pl.pallas_call(
        body,
        grid=(nb,),
        in_specs=_attn_in_specs(bias2) + [pl.BlockSpec(sinkcol.shape, lambda n: (0, 0))],
        out_specs=pl.BlockSpec((BLK, ATTN_W), lambda n: (n, 0)),
        out_shape=jax.ShapeDtypeStruct((n_rows, ATTN_W), BF16),
        compiler_params=_cparams(("parallel",)),
        name="attn_fwd",
    )(proj, proj, proj, proj, proj, bias2, sinkcol)


def _attention_bwd(proj, attn, dattn, bias2, sinkcol, n_rows):
    nb = n_rows // BLK
    scale = HEAD_DIM ** -0.5
    dn_t = (((0,), (0,)), ((), ()))

    def body(q_ref, kp_ref, kc_ref, vp_ref, vc_ref, bias_ref, o_ref, do_ref, sink_ref,
             dq_ref, dkc_ref, dkp_ref, dvc_ref, dvp_ref, dbias_ref, dsink_ref):
        @pl.when(pl.program_id(0) == 0)
        def _():
            dbias_ref[...] = jnp.zeros(dbias_ref.shape, F32)
            dsink_ref[...] = jnp.zeros(dsink_ref.shape, F32)

        heads = range(N_KV_HEADS)
        qk = [_attn_scores(q_ref, kp_ref, kc_ref, hkv) for hkv in heads]
        dog, dps, deltas = [], [], []
        for hkv in heads:
            hs = [hkv * GROUP + g for g in range(GROUP)]
            d_o = jnp.concatenate([do_ref[:, h * HEAD_DIM:(h + 1) * HEAD_DIM] for h in hs], axis=0)
            o = jnp.concatenate([o_ref[:, h * HEAD_DIM:(h + 1) * HEAD_DIM] for h in hs], axis=0)
            deltas.append(jnp.sum(d_o.astype(F32) * o.astype(F32), axis=-1, keepdims=True))
            dog.append(d_o.astype(BF16))
            dps.append(lax.dot_general(dog[hkv], _kv_rows(vp_ref, vc_ref, hkv),
                                       (((1,), (1,)), ((), ())), preferred_element_type=F32))
        p16, ds16 = [], []
        for hkv in heads:
            r0, r1 = hkv * GROUP * BLK, (hkv + 1) * GROUP * BLK
            p, p_sink = _attn_softmax(qk[hkv][2], bias_ref, sink_ref, hkv)
            ds = p * (dps[hkv] - deltas[hkv])
            dbias_ref[r0:r1, :] += ds
            dsink_ref[r0:r1, :] += -(p_sink * deltas[hkv])
            p16.append(p.astype(BF16))
            ds16.append(ds.astype(BF16))
        for hkv in heads:
            c0 = hkv * HEAD_DIM
            qg, kk, _ = qk[hkv]
            dqg = jnp.dot(ds16[hkv], kk, preferred_element_type=F32) * scale
            dkk = lax.dot_general(ds16[hkv], qg, dn_t, preferred_element_type=F32) * scale
            dvv = lax.dot_general(p16[hkv], dog[hkv], dn_t, preferred_element_type=F32)
            for g in range(GROUP):
                h = hkv * GROUP + g
                dq_ref[:, h * HEAD_DIM:(h + 1) * HEAD_DIM] = (
                    dqg[g * BLK:(g + 1) * BLK, :].astype(dq_ref.dtype))
            dkp_ref[:, c0:c0 + HEAD_DIM] = dkk[:BLK].astype(dkp_ref.dtype)
            dkc_ref[:, c0:c0 + HEAD_DIM] = dkk[BLK:].astype(dkc_ref.dtype)
            dvp_ref[:, c0:c0 + HEAD_DIM] = dvv[:BLK].astype(dvp_ref.dtype)
            dvc_ref[:, c0:c0 + HEAD_DIM] = dvv[BLK:].astype(dvc_ref.dtype)

    kv_out = pl.BlockSpec((BLK, KV_W), lambda n: (n, 0))
    kv_shape = jax.ShapeDtypeStruct((n_rows, KV_W), F32)
    acc_shape = bias2.shape[1:]
    return pl.pallas_call(
        body,
        grid=(nb,),
        in_specs=_attn_in_specs(bias2) + [
            pl.BlockSpec((BLK, ATTN_W), lambda n: (n, 0)),
            pl.BlockSpec((BLK, ATTN_W), lambda n: (n, 0)),
            pl.BlockSpec(sinkcol.shape, lambda n: (0, 0)),
        ],
        out_specs=[
            pl.BlockSpec((BLK, ATTN_W), lambda n: (n, 0)),
            kv_out, kv_out, kv_out, kv_out,
            pl.BlockSpec(acc_shape, lambda n: (0, 0)),
            pl.BlockSpec(sinkcol.shape, lambda n: (0, 0)),
        ],
        out_shape=[
            jax.ShapeDtypeStruct((n_rows, ATTN_W), BF16),
            kv_shape, kv_shape, kv_shape, kv_shape,
            jax.ShapeDtypeStruct(acc_shape, F32),
            jax.ShapeDtypeStruct(sinkcol.shape, F32),
        ],
        compiler_params=_cparams(("arbitrary",)),
        name="attn_bwd",
    )(proj, proj, proj, proj, proj, bias2, attn, dattn, sinkcol)


def _bias_tables(rel_bias_t, onehot_t, band_first, band_rest):
    def body(rb_ref, oh_ref, mf_ref, mr_ref, out_ref):
        acc = jnp.zeros((N_Q_HEADS, BLK * 2 * BLK), F32)
        for part in _split3(rb_ref[...]):
            acc = acc + jnp.dot(part, oh_ref[...], preferred_element_type=F32)
        out_ref[0] = jnp.where(mf_ref[...] > 0.0, acc, NEG_INF)
        out_ref[1] = jnp.where(mr_ref[...] > 0.0, acc, NEG_INF)

    return pl.pallas_call(
        body,
        out_shape=jax.ShapeDtypeStruct((2, N_Q_HEADS, BLK * 2 * BLK), F32),
        compiler_params=pltpu.CompilerParams(vmem_limit_bytes=VMEM_LIMIT),
        name="bias_tables",
    )(rel_bias_t, onehot_t, band_first, band_rest)


def _split3(a):
    hi = a.astype(BF16)
    r1 = a - hi.astype(F32)
    mid = r1.astype(BF16)
    lo = (r1 - mid.astype(F32)).astype(BF16)
    return hi, mid, lo


def _bucket_reduce(dbias, dsink, onehot_t):
    def body(db_ref, ds_ref, oh_ref, ob_ref, os_ref):
        acc = jnp.zeros((N_Q_HEADS, 128), F32)
        for part in _split3(db_ref[...]):
            acc = acc + lax.dot_general(part, oh_ref[...], (((1,), (1,)), ((), ())),
                                        preferred_element_type=F32)
        ob_ref[...] = acc
        os_ref[...] = jnp.broadcast_to(jnp.sum(ds_ref[...], axis=-1, keepdims=True),
                                       os_ref.shape)

    return pl.pallas_call(
        body,
        out_shape=[jax.ShapeDtypeStruct((N_Q_HEADS, 128), F32),
                   jax.ShapeDtypeStruct((N_Q_HEADS, 128), F32)],
        compiler_params=pltpu.CompilerParams(vmem_limit_bytes=VMEM_LIMIT),
        name="bias_bucket_reduce",
    )(dbias, dsink, onehot_t)


def _disc(lr, li, ls, btr, bti):
    lam_re = jnp.minimum(lr, -1e-4)
    delta = jnp.exp(ls)
    mag = jnp.exp(lam_re * delta)
    ang = li * delta
    ar, ai = mag * jnp.cos(ang), mag * jnp.sin(ang)
    nr, ni = ar - 1.0, ai
    den = lam_re * lam_re + li * li
    fr = (nr * lam_re + ni * li) / den
    fi = (ni * lam_re - nr * li) / den
    bbr = fr * btr - fi * bti
    bbi = fr * bti + fi * btr
    return ar, ai, bbr, bbi


def _block_mask():
    row = lax.broadcasted_iota(jnp.int32, (SSM_W, SSM_H), 0)
    col = lax.broadcasted_iota(jnp.int32, (SSM_W, SSM_H), 1)
    return (row // SSM_P) == (col // SSM_N)


def _ssm_setup(lr, li, ls, btr, bti, ctr, cti):
    def body(lr_ref, li_ref, ls_ref, btr_ref, bti_ref, ctr_ref, cti_ref, a_ref, b_ref, c_ref):
        ar, ai, bbr, bbi = _disc(lr_ref[...], li_ref[...], ls_ref[...], btr_ref[...], bti_ref[...])
        a_ref[:, :SSM_H] = ar
        a_ref[:, SSM_H:] = ai
        mask = _block_mask()
        blk = lambda t: jnp.where(mask, jnp.tile(t, (SSM_G, 1)), 0.0)
        b_ref[:, :SSM_H] = blk(bbr).astype(BF16)
        b_ref[:, SSM_H:] = blk(bbi).astype(BF16)
        c_ref[:, :SSM_H] = blk(ctr_ref[...]).astype(BF16)
        c_ref[:, SSM_H:] = blk(-cti_ref[...]).astype(BF16)

    return pl.pallas_call(
        body,
        out_shape=[jax.ShapeDtypeStruct((1, 2 * SSM_H), F32),
                   jax.ShapeDtypeStruct((SSM_W, 2 * SSM_H), BF16),
                   jax.ShapeDtypeStruct((SSM_W, 2 * SSM_H), BF16)],
        compiler_params=pltpu.CompilerParams(vmem_limit_bytes=VMEM_LIMIT),
        name="ssm_setup",
    )(lr, li, ls, btr, bti, ctr, cti)


def _ssm_param_bwd(lr, li, ls, btr, bti, dacc, dbcat, dccat, gind):
    def body(lr_ref, li_ref, ls_ref, btr_ref, bti_ref, dacc_ref, db_ref, dc_ref, g_ref,
             dlr_ref, dli_ref, dls_ref, dbtr_ref, dbti_ref, dctr_ref, dcti_ref):
        dar = jnp.sum(dacc_ref[:, :SSM_H], axis=0, keepdims=True)
        dai = jnp.sum(dacc_ref[:, SSM_H:], axis=0, keepdims=True)
        col = lax.broadcasted_iota(jnp.int32, (SSM_P, 2 * SSM_H), 1)
        grp = (col % SSM_H) // SSM_N
        db = jnp.zeros((SSM_P, 2 * SSM_H), F32)
        dc = jnp.zeros((SSM_P, 2 * SSM_H), F32)
        half = SSM_G // 2
        for g in range(SSM_G):
            sel = grp == g
            r0 = (g % half) * SSM_P
            db = db + jnp.where(sel, db_ref[r0:r0 + SSM_P, :], 0.0)
            dc = dc + jnp.where(sel, dc_ref[r0:r0 + SSM_P, :], 0.0)
        dctr_ref[...] = dc[:, :SSM_H]
        dcti_ref[...] = -dc[:, SSM_H:]
        prim = (lr_ref[...], li_ref[...], ls_ref[...], btr_ref[...], bti_ref[...])
        _, vjp = jax.vjp(_disc, *prim)
        dlr, dli, dls, dbtr, dbti = vjp((dar, dai, db[:, :SSM_H], db[:, SSM_H:]))
        dlr_ref[...] = dlr
        dli_ref[...] = dli
        dbtr_ref[...] = dbtr
        dbti_ref[...] = dbti
        acc = jnp.zeros((8, 128), F32)
        for part in _split3(jnp.broadcast_to(dls, (8, SSM_H))):
            acc = acc + jnp.dot(part, g_ref[...], preferred_element_type=F32)
        dls_ref[...] = acc

    vec = jax.ShapeDtypeStruct((1, SSM_H), F32)
    mat = jax.ShapeDtypeStruct((SSM_P, SSM_H), F32)
    return pl.pallas_call(
        body,
        out_shape=[vec, vec, jax.ShapeDtypeStruct((8, 128), F32), mat, mat, mat, mat],
        compiler_params=pltpu.CompilerParams(vmem_limit_bytes=VMEM_LIMIT),
        name="ssm_param_bwd",
    )(lr, li, ls, btr, bti, dacc, dbcat, dccat, gind)


SCAN_TR = 256


def _cmul_add(vr, vi, pr, pi, sr, si):
    return vr + pr * sr - pi * si, vi + pr * si + pi * sr


def _bcast_row(v, row, which):
    b = jnp.where(row == which, v, 0.0)
    b = b + pltpu.roll(b, 4, 0)
    b = b + pltpu.roll(b, 2, 0)
    return b + pltpu.roll(b, 1, 0)


def _scan_tables(a_ref, tab_ref, reverse):
    H = SSM_H
    ar = jnp.broadcast_to(a_ref[:, :H], (8, H))
    ai = jnp.broadcast_to(a_ref[:, H:], (8, H))
    if reverse:
        ai = -ai
    row = lax.broadcasted_iota(jnp.int32, (8, H), 0)
    pw = [(ar, ai)]
    for _ in range(7):
        cr, ci = pw[-1]
        pw.append((cr * ar - ci * ai, cr * ai + ci * ar))
    pcr = jnp.zeros((8, H), F32)
    pci = jnp.zeros((8, H), F32)
    for e in range(8):
        sel = (row == (7 - e)) if reverse else (row == e)
        pcr = jnp.where(sel, pw[e][0], pcr)
        pci = jnp.where(sel, pw[e][1], pci)
    tab_ref[0, :, :H] = pcr
    tab_ref[0, :, H:] = pci
    for t, k in enumerate((1, 2, 4)):
        keep = (row < 8 - k) if reverse else (row >= k)
        tab_ref[1 + t, :, :H] = jnp.where(keep, pw[k - 1][0], 0.0)
        tab_ref[1 + t, :, H:] = jnp.where(keep, pw[k - 1][1], 0.0)


def _scan_group(vr, vi, cr, ci, tab_ref, reverse):
    H = SSM_H
    for t, k in enumerate((1, 2, 4)):
        sh = 8 - k if reverse else k
        vr, vi = _cmul_add(vr, vi, tab_ref[1 + t, :, :H], tab_ref[1 + t, :, H:],
                           pltpu.roll(vr, sh, 0), pltpu.roll(vi, sh, 0))
    return _cmul_add(vr, vi, tab_ref[0, :, :H], tab_ref[0, :, H:], cr, ci)


def _scan_fwd(bu, abar, n_rows):
    H = SSM_H
    nt = n_rows // SCAN_TR

    def body(bu_ref, a_ref, xs_ref, xp_ref, tab_ref, carry_ref):
        @pl.when(pl.program_id(0) == 0)
        def _():
            _scan_tables(a_ref, tab_ref, False)
            carry_ref[...] = jnp.zeros(carry_ref.shape, F32)

        row = lax.broadcasted_iota(jnp.int32, (8, H), 0)

        def group(j, carry):
            cr, ci = carry
            r0 = pl.multiple_of(j * 16, 16)
            xr, xi = [], []
            for half in range(2):
                rr = pl.multiple_of(r0 + 8 * half, 8)
                vr, vi = _scan_group(bu_ref[pl.ds(rr, 8), :H], bu_ref[pl.ds(rr, 8), H:],
                                     cr, ci, tab_ref, False)
                xp_ref[pl.ds(rr, 8), :H] = jnp.where(row == 0, cr, pltpu.roll(vr, 1, 0))
                xp_ref[pl.ds(rr, 8), H:] = jnp.where(row == 0, ci, pltpu.roll(vi, 1, 0))
                cr, ci = _bcast_row(vr, row, 7), _bcast_row(vi, row, 7)
                xr.append(vr)
                xi.append(vi)
            xs_ref[pl.ds(r0, 16), :H] = jnp.concatenate(xr, axis=0).astype(BF16)
            xs_ref[pl.ds(r0, 16), H:] = jnp.concatenate(xi, axis=0).astype(BF16)
            return cr, ci

        cr, ci = lax.fori_loop(0, SCAN_TR // 16, group,
                               (carry_ref[:, :H], carry_ref[:, H:]))
        carry_ref[:, :H] = cr
        carry_ref[:, H:] = ci

    return pl.pallas_call(
        body,
        grid=(nt,),
        in_specs=[pl.BlockSpec((SCAN_TR, 2 * H), lambda i: (i, 0)),
                  pl.BlockSpec((1, 2 * H), lambda i: (0, 0))],
        out_specs=[pl.BlockSpec((SCAN_TR, 2 * H), lambda i: (i, 0)),
                   pl.BlockSpec((SCAN_TR, 2 * H), lambda i: (i, 0))],
        out_shape=[jax.ShapeDtypeStruct((n_rows, 2 * H), BF16),
                   jax.ShapeDtypeStruct((n_rows, 2 * H), F32)],
        scratch_shapes=[pltpu.VMEM((4, 8, 2 * H), F32), pltpu.VMEM((8, 2 * H), F32)],
        compiler_params=_cparams(("arbitrary",)),
        name="ssm_scan_fwd",
    )(bu, abar)


def _scan_bwd(gx, xprev, abar, n_rows):
    H = SSM_H
    nt = n_rows // SCAN_TR

    def body(g_ref, xp_ref, a_ref, h_ref, da_ref, tab_ref, carry_ref):
        @pl.when(pl.program_id(0) == 0)
        def _():
            _scan_tables(a_ref, tab_ref, True)
            carry_ref[...] = jnp.zeros(carry_ref.shape, F32)
            da_ref[...] = jnp.zeros(da_ref.shape, F32)

        row = lax.broadcasted_iota(jnp.int32, (8, H), 0)
        n16 = SCAN_TR // 16

        def group(jj, carry):
            cr, ci = carry
            r0 = pl.multiple_of((n16 - 1 - jj) * 16, 16)
            hr, hi = [None, None], [None, None]
            for half in (1, 0):
                rr = pl.multiple_of(r0 + 8 * half, 8)
                vr, vi = _scan_group(g_ref[pl.ds(rr, 8), :H], g_ref[pl.ds(rr, 8), H:],
                                     cr, ci, tab_ref, True)
                pr, pi = xp_ref[pl.ds(rr, 8), :H], xp_ref[pl.ds(rr, 8), H:]
                da_ref[:, :H] += vr * pr + vi * pi
                da_ref[:, H:] += vi * pr - vr * pi
                cr, ci = _bcast_row(vr, row, 0), _bcast_row(vi, row, 0)
                hr[half], hi[half] = vr, vi
            h_ref[pl.ds(r0, 16), :H] = jnp.concatenate(hr, axis=0).astype(BF16)
            h_ref[pl.ds(r0, 16), H:] = jnp.concatenate(hi, axis=0).astype(BF16)
            return cr, ci

        cr, ci = lax.fori_loop(0, n16, group, (carry_ref[:, :H], carry_ref[:, H:]))
        carry_ref[:, :H] = cr
        carry_ref[:, H:] = ci

    rev = lambda i: (nt - 1 - i, 0)
    return pl.pallas_call(
        body,
        grid=(nt,),
        in_specs=[pl.BlockSpec((SCAN_TR, 2 * H), rev),
                  pl.BlockSpec((SCAN_TR, 2 * H), rev),
                  pl.BlockSpec((1, 2 * H), lambda i: (0, 0))],
        out_specs=[pl.BlockSpec((SCAN_TR, 2 * H), rev),
                   pl.BlockSpec((8, 2 * H), lambda i: (0, 0))],
        out_shape=[jax.ShapeDtypeStruct((n_rows, 2 * H), BF16),
                   jax.ShapeDtypeStruct((8, 2 * H), F32)],
        scratch_shapes=[pltpu.VMEM((4, 8, 2 * H), F32), pltpu.VMEM((8, 2 * H), F32)],
        compiler_params=_cparams(("arbitrary",)),
        name="ssm_scan_bwd",
    )(gx, xprev, abar)


def _adamw(parts, w, m, v, *, tr, ch, name, prefetch=None):
    n_rows, cols = w.shape
    n_parts = len(parts)
    c1 = 1.0 - ADAM_B1 ** ADAM_STEP
    c2 = 1.0 - ADAM_B2 ** ADAM_STEP

    def fn(rv, vv, i, nt):
        g = rv[0].astype(F32)
        for p in rv[1:n_parts]:
            g = g + p.astype(F32)
        wv, mv, vval = rv[n_parts:]
        nm = ADAM_B1 * mv + (1.0 - ADAM_B1) * g
        nv = ADAM_B2 * vval + (1.0 - ADAM_B2) * (g * g)
        delta = -ADAM_LR * ((nm / c1) / (jnp.sqrt(nv / c2) + ADAM_EPS) + ADAM_WD * wv)
        return [g, delta, nm, nv], []

    rows = [_row(arr, lead=lead) for (arr, lead) in parts] + [_row(w), _row(m), _row(v)]
    return _rowwise(fn, rows, [], [(cols, F32)] * 4, [], n_rows=n_rows, tr=tr, ch=ch, name=name,
                    prefetch=prefetch)


_PACK = [
    ("b_ada", 6), ("norm1_g", 1), ("b_in", 3), ("norm2_g", 1), ("final_g", 1),
    ("lambda_re", 1), ("lambda_im", 1), ("log_step", 1), ("attn_sinks", 1),
    ("rel_bias", 1), ("b_glu", 1), ("ssm_d", 1), ("loss", 1),
    ("ssm_b_re", 16), ("ssm_b_im", 16), ("ssm_c_re", 16), ("ssm_c_im", 16),
]
_PACK_OFF = {}
_off = 0
for _n, _r in _PACK:
    _PACK_OFF[_n] = (_off, _r)
    _off += _r
PACK_ROWS = -(-_off // 8) * 8


def _to_rows(a, rows):
    flat = a.reshape(-1).astype(F32)
    pad = rows * PACK_W - flat.shape[0]
    if pad:
        flat = jnp.pad(flat, (0, pad))
    return flat.reshape(rows, PACK_W)


def _b_to_rows(b):
    return jnp.transpose(b, (2, 0, 1)).reshape(SSM_P, SSM_H)


def _rows_to_b(r):
    return jnp.transpose(r.reshape(SSM_P, SSM_G, SSM_N), (1, 2, 0))


def _c_to_rows(cm):
    return jnp.transpose(cm, (1, 0, 2)).reshape(SSM_P, SSM_H)


def _rows_to_c(r):
    return jnp.transpose(r.reshape(SSM_P, SSM_G, SSM_N), (1, 0, 2))


def _pack(vals):
    out = jnp.zeros((PACK_ROWS, PACK_W), F32)
    for n, r in _PACK:
        if n in vals:
            pieces = vals[n] if isinstance(vals[n], list) else [vals[n]]
            rows_each = r // len(pieces)
            for i, piece in enumerate(pieces):
                out = lax.dynamic_update_slice(out, _to_rows(piece, rows_each),
                                               (_PACK_OFF[n][0] + i * rows_each, 0))
    return out


def _unpack(packed, name, shape):
    o, r = _PACK_OFF[name]
    n = int(np.prod(shape))
    return packed[o:o + r].reshape(-1)[:n].reshape(shape)


def _small_params_packed(p):
    return {
        "b_ada": p["b_ada"], "norm1_g": p["norm1_g"], "b_in": p["b_in"],
        "norm2_g": p["norm2_g"], "final_g": p["final_g"],
        "lambda_re": p["lambda_re"], "lambda_im": p["lambda_im"],
        "log_step": p["log_step"], "attn_sinks": p["attn_sinks"],
        "rel_bias": p["rel_bias"], "b_glu": p["b_glu"], "ssm_d": p["ssm_d"],
        "ssm_b_re": _b_to_rows(p["ssm_b_re"][0]), "ssm_b_im": _b_to_rows(p["ssm_b_im"][0]),
        "ssm_c_re": _c_to_rows(p["ssm_c_re"][0]), "ssm_c_im": _c_to_rows(p["ssm_c_im"][0]),
    }


_SMALL_SHAPES = {
    "b_ada": (1, N_MOD * D), "norm1_g": (1, D), "b_in": (1, IN_W), "norm2_g": (1, D),
    "final_g": (D,), "lambda_re": (1, SSM_G, SSM_N), "lambda_im": (1, SSM_G, SSM_N),
    "log_step": (1, SSM_G), "attn_sinks": (1, N_Q_HEADS), "rel_bias": (NUM_BUCKETS, N_Q_HEADS),
    "b_glu": (1, SSM_W), "ssm_d": (1, SSM_W),
}


def _unpack_small(packed, name):
    if name in ("ssm_b_re", "ssm_b_im"):
        o, r = _PACK_OFF[name]
        return _rows_to_b(packed[o:o + r])[None]
    if name in ("ssm_c_re", "ssm_c_im"):
        o, r = _PACK_OFF[name]
        return _rows_to_c(packed[o:o + r])[None]
    return _unpack(packed, name, _SMALL_SHAPES[name])


WEIGHT_ORDER = ['w_ada', 'b_ada', 'norm1_g', 'w_in', 'b_in', 'attn_sinks', 'rel_bias', 'lambda_re',
                'lambda_im', 'log_step', 'ssm_b_re', 'ssm_b_im', 'ssm_c_re', 'ssm_c_im', 'ssm_d',
                'w_glu', 'b_glu', 'w_attn_proj', 'w_ssm_proj', 'w_out', 'norm2_g', 'w_ff1', 'w_ff2',
                'final_g']
BIG = ['w_in', 'w_glu', 'w_attn_proj', 'w_ssm_proj', 'w_out', 'w_ff1', 'w_ff2']


ADAMW_TILE_ELEMS = 1 << 18


def _to_col_blocks(w):
    k, n = w.shape
    return jnp.transpose(w.reshape(k, N_DEV, n // N_DEV), (1, 0, 2))


def _adamw_rows(rows, cols):
    tr = rows
    while tr * cols > ADAMW_TILE_ELEMS and tr % 32 == 0:
        tr //= 2
    return tr


def _cast_to_slot(w, me1, name):
    rows, cols = w.shape
    tr = min(rows, 256)

    def body(me_ref, w_ref, o_ref):
        o_ref[...] = w_ref[...].astype(BF16)

    return pl.pallas_call(
        body,
        grid_spec=pltpu.PrefetchScalarGridSpec(
            num_scalar_prefetch=1, grid=(rows // tr,),
            in_specs=[pl.BlockSpec((tr, cols), lambda i, me_ref: (i, 0))],
            out_specs=pl.BlockSpec((None, tr, cols), lambda i, me_ref: (me_ref[0], i, 0))),
        out_shape=jax.ShapeDtypeStruct((N_DEV, rows, cols), BF16),
        compiler_params=_cparams(("arbitrary",)),
        name=name,
    )(me1, w)


def kernel(x, c, w_ada, b_ada, norm1_g, w_in, b_in, attn_sinks, rel_bias, lambda_re, lambda_im, log_step, ssm_b_re, ssm_b_im, ssm_c_re, ssm_c_im, ssm_d, w_glu, b_glu, w_attn_proj, w_ssm_proj, w_out, norm2_g, w_ff1, w_ff2, final_g, loss_target, m_w_ada, m_b_ada, m_norm1_g, m_w_in, m_b_in, m_attn_sinks, m_rel_bias, m_lambda_re, m_lambda_im, m_log_step, m_ssm_b_re, m_ssm_b_im, m_ssm_c_re, m_ssm_c_im, m_ssm_d, m_w_glu, m_b_glu, m_w_attn_proj, m_w_ssm_proj, m_w_out, m_norm2_g, m_w_ff1, m_w_ff2, m_final_g, v_w_ada, v_b_ada, v_norm1_g, v_w_in, v_b_in, v_attn_sinks, v_rel_bias, v_lambda_re, v_lambda_im, v_log_step, v_ssm_b_re, v_ssm_b_im, v_ssm_c_re, v_ssm_c_im, v_ssm_d, v_w_glu, v_b_glu, v_w_attn_proj, v_w_ssm_proj, v_w_out, v_norm2_g, v_w_ff1, v_w_ff2, v_final_g):
    loc = dict(locals())
    W = {n: loc[n] for n in WEIGHT_ORDER}
    Mo = {n: loc["m_" + n] for n in WEIGHT_ORDER}
    Vo = {n: loc["v_" + n] for n in WEIGHT_ORDER}
    S = x.shape[1]
    TM = min(512, S)
    TS = min(1024, S)
    TR = min(256, S)
    TW = min(1024, S)
    me = 4 * lax.axis_index("x") + 2 * lax.axis_index("y") + lax.axis_index("c")
    x2d = x.reshape(S, D)
    tgt = loss_target.reshape(S, D)

    shard = {n: W[n][0] for n in BIG}
    me1 = jnp.reshape(me, (1,)).astype(jnp.int32)
    zone = {n: _cast_to_slot(shard[n], me1, "cast_" + n) for n in BIG}
    (w_in_flight,), tok_in = _exchange_start([zone["w_in"]], "gather", "w_in_start")
    later = [n for n in BIG if n != "w_in"]
    flights, tok_w = _exchange_start([zone[n] for n in later], "gather", "weights_start", tok_in)
    w_flight = dict(zip(later, flights))
    G = {}

    c_all = _small_allgather(c, "allgather_c").reshape(N_DEV, D)
    cs = _rowwise(lambda rv, vv, i, nt: ([rv[0] * _sigmoid(rv[0])], []), [_row(c_all)], [],
                  [(D, F32)], [], n_rows=N_DEV, tr=8, ch=8, name="silu_c")[0]
    n_ada = N_MOD * D // N_DEV
    b_ada_cols = lax.dynamic_slice(b_ada, (0, me * n_ada), (1, n_ada))
    mod_piece = _matmul(cs, w_ada[0], mode="nn", dims=(N_DEV, n_ada, D), tiles=(N_DEV, 512, D),
                        out_dtypes=[F32], name="ada_fwd", bias=b_ada_cols)
    mod_all = _small_allgather(mod_piece, "allgather_mod")
    mod_b = lax.dynamic_index_in_dim(mod_all, me, axis=1, keepdims=False).reshape(N_MOD, D)
    sh1, sc1, g1, sh2, sc2, g2 = [mod_b[i:i + 1] for i in range(N_MOD)]

    def f_norm1(rv, vv, i, nt):
        xv, (g, sc, sh) = rv[0], vv
        return [(xv * _rms(xv) * g) * (1.0 + sc) + sh], []

    h = _rowwise(f_norm1, [_row(x2d)], [norm1_g, sc1, sh1], [(D, BF16)], [],
                 n_rows=S, tr=TR, ch=32, name="norm1_fwd", dep=tok_w)[0]
    G["w_in"] = _exchange_wait([w_in_flight], "gather", h, "w_in_wait")[0][0]
    proj = _matmul(h, G["w_in"], mode="nn", dims=(S, IN_W, D), tiles=(TM, 768, D),
                   out_dtypes=[F32], name="in_proj", b3=True, bias=b_in)

    buckets = _t5_buckets_block()
    band = _band_mask()
    onehot_t = jnp.asarray(
        (np.arange(128)[:, None] == buckets.reshape(-1)[None, :]).astype(np.float32), BF16)
    band_first = band & (np.arange(2 * BLK)[None, :] >= BLK)
    rel_bias_t = jnp.pad(jnp.transpose(rel_bias), ((0, 0), (0, 128 - NUM_BUCKETS)))
    bias2 = _bias_tables(rel_bias_t, onehot_t,
                         jnp.asarray(band_first.reshape(1, -1).astype(np.float32)),
                         jnp.asarray(band.reshape(1, -1).astype(np.float32))
                         ).reshape(2, N_Q_HEADS * BLK, 2 * BLK)
    sinkcol = jnp.repeat(attn_sinks.reshape(N_Q_HEADS), BLK).reshape(N_Q_HEADS * BLK, 1)
    attn = _attention_fwd(proj, bias2, sinkcol, S)
    mixer = ["w_attn_proj", "w_glu", "w_ssm_proj", "w_out"]
    landed = _exchange_wait([w_flight[n] for n in mixer], "gather", attn, "weights_wait_mixer")
    G.update((n, bufs[0]) for n, bufs in zip(mixer, landed))
    w_glu_f = G["w_glu"].reshape(SSM_W, SSM_W)
    w_out_f = G["w_out"].reshape(D, D)
    w_ap_f = jnp.transpose(G["w_attn_proj"], (1, 0, 2)).reshape(ATTN_W, D)
    w_sp_f = jnp.transpose(G["w_ssm_proj"], (1, 0, 2)).reshape(SSM_W, D)
    y_attn = _matmul(attn, w_ap_f, mode="nn", dims=(S, D, ATTN_W), tiles=(TW, 1024, ATTN_W),
                     out_dtypes=[F32], name="attn_proj")

    lam_re = lambda_re.reshape(1, SSM_H)
    lam_im = lambda_im.reshape(1, SSM_H)
    ls_x = jnp.repeat(log_step.reshape(SSM_G), SSM_N).reshape(1, SSM_H)
    btr, bti = _b_to_rows(ssm_b_re[0]), _b_to_rows(ssm_b_im[0])
    ctr, cti = _c_to_rows(ssm_c_re[0]), _c_to_rows(ssm_c_im[0])
    abar, bcat, ccat = _ssm_setup(lam_re, lam_im, ls_x, btr, bti, ctr, cti)
    u_blk = (ATTN_W + 2 * KV_W) // SSM_W
    hw = SSM_W // 2
    u_half = (ATTN_W + 2 * KV_W) // hw
    bu = _matmul(proj, bcat, mode="nn", dims=(S, 2 * SSM_H, hw), tiles=(TM, 1024, hw),
                 out_dtypes=[F32], name="ssm_bu",
                 a_index=lambda i, j, k: (i, u_half + j % 2), b_index=lambda i, j, k: (j % 2, j))
    xs, xprev = _scan_fwd(bu, abar, S)
    yc = _matmul(xs, ccat, mode="nt", dims=(S, SSM_W, SSM_H), tiles=(TM, hw, 1024),
                 out_dtypes=[F32], name="ssm_cx",
                 a_index=lambda i, j, k: (i, j + 2 * k), b_index=lambda i, j, k: (j, j + 2 * k))

    def f_ssm_out(rv, vv, i, nt):
        y = rv[0] + vv[0] * rv[1]
        return [y, _gelu(y)], []

    y_ssm_pre, z = _rowwise(f_ssm_out, [_row(yc), _row(proj, u_blk, SSM_W)], [ssm_d],
                            [(SSM_W, F32), (SSM_W, BF16)], [], n_rows=S, tr=TM, ch=32, name="ssm_out")
    zg = _matmul(z, w_glu_f, mode="nn", dims=(S, SSM_W, SSM_W), tiles=(TM, SSM_W, SSM_W),
                 out_dtypes=[F32], name="glu_proj", bias=b_glu)
    z2 = _rowwise(lambda rv, vv, i, nt: ([rv[0].astype(F32) * _sigmoid(rv[1])], []),
                  [_row(z), _row(zg)], [], [(SSM_W, BF16)], [], n_rows=S, tr=TM, ch=32, name="glu_gate")[0]
    y_ssm = _matmul(z2, w_sp_f, mode="nn", dims=(S, D, SSM_W), tiles=(TW, 1024, SSM_W),
                    out_dtypes=[F32], name="ssm_proj")

    ga_row = _row(proj, 1, D)
    gs_row = _row(proj, 2, D)

    def f_merge(rv, vv, i, nt):
        ga, gs, ya, ys = rv
        return [_sigmoid(ga) * ya + _sigmoid(gs) * ys], []

    merged = _rowwise(f_merge, [ga_row, gs_row, _row(y_attn), _row(y_ssm)], [], [(D, BF16)], [],
                      n_rows=S, tr=TR, ch=32, name="merge")[0]
    mo = _matmul(merged, w_out_f, mode="nn", dims=(S, D, D), tiles=(TW, 1024, D),
                 out_dtypes=[F32], name="out_proj")

    def f_norm2(rv, vv, i, nt):
        xv, mv = rv
        g1v, g, sc, sh = vv
        x1v = xv + g1v * mv
        return [x1v, (x1v * _rms(x1v) * g) * (1.0 + sc) + sh], []

    x1, h2 = _rowwise(f_norm2, [_row(x2d), _row(mo)], [g1, norm2_g, sc2, sh2],
                      [(D, F32), (D, BF16)], [], n_rows=S, tr=TR, ch=32, name="norm2_fwd")

    def relu_sq(acc):
        r = jnp.maximum(acc, 0.0)
        return r * r, r

    G["w_ff1"] = _exchange_wait([w_flight["w_ff1"]], "gather", h2, "weights_wait_ff1")[0][0]
    act, relu = _matmul(h2, G["w_ff1"], mode="nn", dims=(S, D_FF, D), tiles=(TM, 1024, D),
                        out_dtypes=[BF16, BF16], name="ff1", b3=True, epilogue=relu_sq)
    w_ff2_f = _exchange_wait([w_flight["w_ff2"]], "gather", act, "weights_wait_ff2")[0][0].reshape(D_FF, D)
    ff = _matmul(act, w_ff2_f, mode="nn", dims=(S, D, D_FF), tiles=(TM, 1024, 2048),
                 out_dtypes=[F32], name="ff2")

    def f_loss(rv, vv, i, nt):
        x1v, ffv, tv = rv
        g2v, gf = vv
        x2v = x1v + g2v * ffv
        r = _rms(x2v)
        xh = x2v * r
        diff = xh * gf - tv
        dy = diff * (1.0 / D)
        dxh = dy * gf
        dx2 = r * (dxh - xh * jnp.mean(dxh * xh, axis=-1, keepdims=True))
        return [dx2, dx2 * g2v], [_colsum(0.5 * diff * diff * (1.0 / D)), _colsum(dy * xh),
                                  _colsum(dx2 * ffv)]

    dx2, dff, loss_cols, d_final_g, dg2 = _rowwise(
        f_loss, [_row(x1), _row(ff), _row(tgt)], [g2, final_g.reshape(1, D)],
        [(D, F32), (D, BF16)], [(1, D)] * 3, n_rows=S, tr=TR, ch=32, name="loss_bwd")

    df1 = _matmul(dff, w_ff2_f, mode="nt", dims=(S, D_FF, D), tiles=(TM, 1024, D),
                  out_dtypes=[BF16], name="ff2_dgrad", extras=(relu,),
                  epilogue=lambda acc, r: (acc * (2.0 * r.astype(F32)),))
    gw_ff2 = _matmul(act, dff, mode="tn", dims=(D_FF, D, S), tiles=(1024, 1024, TS),
                     out_dtypes=[BF16], name="ff2_wgrad").reshape(N_DEV, D_FF // N_DEV, D)
    g_flight = {}
    (g_flight["w_ff2"],), tok = _exchange_start([gw_ff2], "scatter", "grads_start_ff2")
    dh2 = _matmul(df1, G["w_ff1"], mode="nt", dims=(S, D, D_FF), tiles=(TM, D, 1024),
                  out_dtypes=[F32], name="ff1_dgrad", b3=True, dep=tok)
    gw_ff1 = _matmul(h2, df1, mode="tn", dims=(D, D_FF, S), tiles=(1024, 1024, TS),
                     out_dtypes=[BF16], name="ff1_wgrad", out3=True)
    (g_flight["w_ff1"],), tok = _exchange_start([gw_ff1], "scatter", "grads_start_ff1")

    def f_norm2_bwd(rv, vv, i, nt):
        x1v, dh, dx2v, mv = rv
        g, sc, g1v = vv
        r = _rms(x1v)
        xh = x1v * r
        t = xh * g
        dt = dh * (1.0 + sc)
        dxh = dt * g
        dx1 = dx2v + r * (dxh - xh * jnp.mean(dxh * xh, axis=-1, keepdims=True))
        return [dx1, dx1 * g1v], [_colsum(dh), _colsum(dh * t), _colsum(dt * xh), _colsum(dx1 * mv)]

    dx1, dmo, dsh2, dsc2, d_norm2_g, dg1 = _rowwise(
        f_norm2_bwd, [_row(x1), _row(dh2), _row(dx2), _row(mo)], [norm2_g, sc2, g1],
        [(D, F32), (D, BF16)], [(1, D)] * 4, n_rows=S, tr=TR, ch=16, name="norm2_bwd", dep=tok)

    dmerged = _matmul(dmo, w_out_f, mode="nt", dims=(S, D, D), tiles=(TW, 1024, D),
                      out_dtypes=[F32], name="out_dgrad")
    gw_out = _matmul(merged, dmo, mode="tn", dims=(D, D, S), tiles=(1024, 1024, TS),
                     out_dtypes=[BF16], name="out_wgrad").reshape(N_DEV, D // N_DEV, D)
    (g_flight["w_out"],), tok = _exchange_start([gw_out], "scatter", "grads_start_out")

    def f_merge_bwd(rv, vv, i, nt):
        dm, ga, gs, ya, ys = rv
        sa, ss = _sigmoid(ga), _sigmoid(gs)
        return [dm * sa, dm * ss, dm * ya * sa * (1.0 - sa), dm * ys * ss * (1.0 - ss)], []

    dy_attn, dy_ssm, dga, dgs = _rowwise(
        f_merge_bwd, [_row(dmerged), ga_row, gs_row, _row(y_attn), _row(y_ssm)], [],
        [(D, BF16)] * 4, [], n_rows=S, tr=TR, ch=16, name="merge_bwd", dep=tok)

    dz2 = _matmul(dy_ssm, w_sp_f, mode="nt", dims=(S, SSM_W, D), tiles=(TW, SSM_W, D),
                  out_dtypes=[F32], name="ssm_proj_dgrad")
    gw_ssm_proj = _to_col_blocks(_matmul(z2, dy_ssm, mode="tn", dims=(SSM_W, D, S), tiles=(SSM_W, 1024, TS),
                                         out_dtypes=[BF16], name="ssm_proj_wgrad"))

    def f_glu_bwd(rv, vv, i, nt):
        dz2v, zv, zgv = rv
        sg = _sigmoid(zgv)
        dzg = dz2v * zv.astype(F32) * sg * (1.0 - sg)
        return [dzg, dz2v * sg], [_colsum(dzg)]

    dzg, dz_a, d_b_glu = _rowwise(f_glu_bwd, [_row(dz2), _row(z), _row(zg)], [],
                                  [(SSM_W, BF16), (SSM_W, F32)], [(1, SSM_W)],
                                  n_rows=S, tr=TM, ch=32, name="glu_bwd")
    dz_b = _matmul(dzg, w_glu_f, mode="nt", dims=(S, SSM_W, SSM_W), tiles=(TM, SSM_W, SSM_W),
                   out_dtypes=[F32], name="glu_dgrad")
    gw_glu = _matmul(z, dzg, mode="tn", dims=(SSM_W, SSM_W, S), tiles=(SSM_W, SSM_W, TS),
                     out_dtypes=[BF16], name="glu_wgrad").reshape(N_DEV, SSM_W // N_DEV, SSM_W)
    (g_flight["w_ssm_proj"], g_flight["w_glu"]), tok = _exchange_start(
        [gw_ssm_proj, gw_glu], "scatter", "grads_start_ssm")

    def f_ssm_out_bwd(rv, vv, i, nt):
        dza, dzb, yv, uv = rv
        dy = (dza + dzb) * _gelu_grad(yv)
        return [dy, dy * vv[0]], [_colsum(dy * uv)]

    dy_s, du_a, d_ssm_d = _rowwise(
        f_ssm_out_bwd, [_row(dz_a), _row(dz_b), _row(y_ssm_pre), _row(proj, u_blk, SSM_W)], [ssm_d],
        [(SSM_W, BF16), (SSM_W, F32)], [(1, SSM_W)], n_rows=S, tr=TM, ch=32, name="ssm_out_bwd", dep=tok)
    gx = _matmul(dy_s, ccat, mode="nn", dims=(S, 2 * SSM_H, hw), tiles=(TM, 1024, hw),
                 out_dtypes=[F32], name="ssm_cx_dgrad",
                 a_index=lambda i, j, k: (i, j % 2), b_index=lambda i, j, k: (j % 2, j))
    dccat = _matmul(dy_s, xs, mode="tn", dims=(hw, 2 * SSM_H, S), tiles=(hw, 1024, TS),
                    out_dtypes=[F32], name="ssm_c_wgrad", a_index=lambda i, j, k: (k, j % 2))
    hs, dacc = _scan_bwd(gx, xprev, abar, S)
    du_b = _matmul(hs, bcat, mode="nt", dims=(S, SSM_W, SSM_H), tiles=(TM, hw, 1024),
                   out_dtypes=[F32], name="ssm_bu_dgrad",
                   a_index=lambda i, j, k: (i, j + 2 * k), b_index=lambda i, j, k: (j, j + 2 * k))
    dbcat = _matmul(proj, hs, mode="tn", dims=(hw, 2 * SSM_H, S), tiles=(hw, 1024, TS),
                    out_dtypes=[F32], name="ssm_b_wgrad", a_index=lambda i, j, k: (k, u_half + j % 2))
    grp = np.arange(SSM_H) // SSM_N
    gind = jnp.asarray((grp[:, None] == np.arange(128)[None, :]).astype(np.float32), BF16)
    d_lam_re, d_lam_im, d_ls, d_btr, d_bti, d_ctr, d_cti = _ssm_param_bwd(
        lam_re, lam_im, ls_x, btr, bti, dacc, dbcat, dccat, gind)

    dattn = _matmul(dy_attn, w_ap_f, mode="nt", dims=(S, ATTN_W, D), tiles=(TW, ATTN_W, D),
                    out_dtypes=[BF16], name="attn_proj_dgrad")
    gw_attn_proj = _to_col_blocks(_matmul(attn, dy_attn, mode="tn", dims=(ATTN_W, D, S), tiles=(ATTN_W, 1024, TS),
                                          out_dtypes=[BF16], name="attn_proj_wgrad"))
    (g_flight["w_attn_proj"],), tok = _exchange_start(
        [gw_attn_proj], "scatter", "grads_start_attn")
    dq, dkc, dkp, dvc, dvp, dbias, dsink = _attention_bwd(proj, attn, dattn, bias2, sinkcol, S)
    d_bias_b, d_sinks = _bucket_reduce(dbias.reshape(N_Q_HEADS, BLK * 2 * BLK),
                                       dsink.reshape(N_Q_HEADS, BLK), onehot_t)

    def f_dproj(rv, vv, i, nt):
        dqv, kc, kp, vc, vp, dua, dub, gav, gsv = rv
        keep = (i < nt - 1).astype(F32)
        dp = jnp.concatenate([dqv.astype(F32), kc + keep * kp, vc + keep * vp, dua + dub,
                              gav.astype(F32), gsv.astype(F32)], axis=-1)
        return [dp], [_colsum(dp)]

    dproj, d_b_in = _rowwise(
        f_dproj, [_row(dq), _row(dkc), _row(dkp, shift=1), _row(dvc), _row(dvp, shift=1),
                  _row(du_a), _row(du_b), _row(dga), _row(dgs)], [],
        [(IN_W, BF16)], [(1, IN_W)], n_rows=S, tr=BLK, ch=16, name="dproj", dep=tok)
    gw_in = _matmul(h, dproj, mode="tn", dims=(D, IN_W, S), tiles=(1024, 768, TS),
                    out_dtypes=[BF16], name="in_wgrad", out3=True)
    (g_flight["w_in"],), tok = _exchange_start([gw_in], "scatter", "grads_start_in")
    dh = _matmul(dproj, G["w_in"], mode="nt", dims=(S, D, IN_W), tiles=(TM, D, 768),
                 out_dtypes=[F32], name="in_dgrad", b3=True, dep=tok)

    def f_norm1_bwd(rv, vv, i, nt):
        xv, dhv, dx1v = rv
        g, sc = vv
        r = _rms(xv)
        xh = xv * r
        t = xh * g
        dt = dhv * (1.0 + sc)
        dxh = dt * g
        dxv = dx1v + r * (dxh - xh * jnp.mean(dxh * xh, axis=-1, keepdims=True))
        return [dxv], [_colsum(dhv), _colsum(dhv * t), _colsum(dt * xh)]

    grad_x, dsh1, dsc1, d_norm1_g = _rowwise(
        f_norm1_bwd, [_row(x2d), _row(dh), _row(dx1)], [norm1_g, sc1],
        [(D, F32)], [(1, D)] * 3, n_rows=S, tr=TR, ch=32, name="norm1_bwd")

    part = _pack({
        "b_ada": [dsh1, dsc1, dg1, dsh2, dsc2, dg2], "norm1_g": d_norm1_g, "b_in": d_b_in, "norm2_g": d_norm2_g,
        "final_g": d_final_g, "lambda_re": d_lam_re, "lambda_im": d_lam_im,
        "log_step": d_ls[0, :SSM_G], "attn_sinks": d_sinks[:, 0],
        "rel_bias": jnp.transpose(d_bias_b[:, :NUM_BUCKETS]), "b_glu": d_b_glu, "ssm_d": d_ssm_d,
        "loss": loss_cols, "ssm_b_re": d_btr, "ssm_b_im": d_bti, "ssm_c_re": d_ctr, "ssm_c_im": d_cti,
    })
    zone_small = lax.dynamic_update_slice(lax.empty((N_DEV, PACK_ROWS, PACK_W), F32), part[None], (me, 0, 0))
    (small_flight,), after = _exchange_start([zone_small], "gather", "small_grads_start")

    big_out = {}
    for n in ["w_ff2", "w_ff1", "w_out", "w_ssm_proj", "w_glu", "w_attn_proj", "w_in"]:
        own, recv = _exchange_wait([g_flight[n]], "scatter", after, "grads_wait_" + n[2:])[0]
        rows, cols = shard[n].shape
        parts = [(own, lambda m: m[0])] + [
            (recv, lambda m, j=j: jnp.where(j >= m[0], j + 1, j)) for j in range(N_DEV - 1)]
        big_out[n] = _adamw(parts, shard[n], Mo[n][0], Vo[n][0], tr=_adamw_rows(rows, cols), ch=16,
                            name="adamw_" + n, prefetch=me1)
        after = big_out[n][0]

    part_all = _exchange_wait([small_flight], "gather", after, "small_grads_wait")[0][0]
    wp, mp, vp = [_pack(_small_params_packed(p)) for p in (W, Mo, Vo)]
    sg, sdelta, sm, sv = _adamw([(part_all, d) for d in range(N_DEV)], wp, mp, vp,
                                tr=PACK_ROWS, ch=8, name="adamw_small")
    lo, _ = _PACK_OFF["loss"]
    loss = jnp.sum(sg[lo])

    o_ada, _ = _PACK_OFF["b_ada"]
    dmod_all = part_all[:, o_ada:o_ada + N_MOD, :].reshape(N_DEV, N_MOD * D)
    dmod_cols = lax.dynamic_slice(dmod_all, (0, me * n_ada), (N_DEV, n_ada))
    gw_ada = _matmul(cs, dmod_cols, mode="tn", dims=(D, n_ada, N_DEV), tiles=(D, 512, N_DEV),
                     out_dtypes=[F32], name="ada_wgrad")
    big_out["w_ada"] = _adamw([(gw_ada, 0)], w_ada[0], m_w_ada[0], v_w_ada[0],
                              tr=_adamw_rows(D, n_ada), ch=16, name="adamw_w_ada")

    def leaf(kind, n):
        if n in big_out:
            return big_out[n][kind][None]
        return _unpack_small((sg, sdelta, sm, sv)[kind], n)

    outs = [loss, grad_x.reshape(1, S, D)]
    for kind in range(4):
        outs.extend(leaf(kind, n) for n in WEIGHT_ORDER)
    return tuple(outs)
```

```python
import functools
import math

import numpy as np
import jax
import jax.numpy as jnp
from jax import lax
from jax.experimental import pallas as pl
from jax.experimental.pallas import tpu as pltpu

F32 = jnp.float32
BF16 = jnp.bfloat16
MESH = pl.DeviceIdType.MESH

N_DEV = 8
D = 2048
HEAD_DIM = 64
N_Q_HEADS = 16
N_KV_HEADS = 4
GROUP = N_Q_HEADS // N_KV_HEADS
ATTN_W = N_Q_HEADS * HEAD_DIM
KV_W = N_KV_HEADS * HEAD_DIM
BLK = 128
NUM_BUCKETS = 32
MAX_DISTANCE = 128
NEG_INF = -1e30
SSM_W = 512
SSM_P = 16
SSM_G = 32
SSM_N = 64
SSM_H = SSM_G * SSM_N
D_FF = 4 * D
IN_W = ATTN_W + 2 * KV_W + SSM_W + 2 * D
N_MOD = 6
EPS = 1e-6

ADAM_LR = 0.001
ADAM_B1 = 0.9
ADAM_B2 = 0.999
ADAM_EPS = 1e-08
ADAM_WD = 0.01
ADAM_STEP = 10

VMEM_LIMIT = 56 * 1024 * 1024
PACK_W = 2048


def _cparams(sem):
    return pltpu.CompilerParams(dimension_semantics=sem, vmem_limit_bytes=VMEM_LIMIT)


def _matmul(a, b, *, mode, dims, tiles, out_dtypes, name, a_off=0, b3=False,
            out3=False, bias=None, extras=(), epilogue=None, dep=None, a_index=None, b_index=None):
    M, N, K = dims
    tm, tn, tk = tiles
    assert M % tm == 0 and N % tn == 0 and K % tk == 0, (name, dims, tiles)
    gm, gn, gk = M // tm, N // tn, K // tk
    n_extra = len(extras)
    has_bias = bias is not None
    n_out = len(out_dtypes)

    if mode == "nn":
        a_spec = pl.BlockSpec((tm, tk), lambda i, j, k: (i, a_off + k))
        if b3:
            nb = (N // N_DEV) // tn
            assert nb * tn * N_DEV == N
            b_spec = pl.BlockSpec((None, tk, tn), lambda i, j, k: (j // nb, k, j % nb))
        else:
            b_spec = pl.BlockSpec((tk, tn), lambda i, j, k: (k, j))
        dn = (((1,), (0,)), ((), ()))
    elif mode == "nt":
        a_spec = pl.BlockSpec((tm, tk), lambda i, j, k: (i, a_off + k))
        if b3:
            nb = (K // N_DEV) // tk
            assert nb * tk * N_DEV == K
            b_spec = pl.BlockSpec((None, tn, tk), lambda i, j, k: (k // nb, j, k % nb))
        else:
            b_spec = pl.BlockSpec((tn, tk), lambda i, j, k: (j, k))
        dn = (((1,), (1,)), ((), ()))
    else:
        a_spec = pl.BlockSpec((tk, tm), lambda i, j, k: (k, a_off + i))
        b_spec = pl.BlockSpec((tk, tn), lambda i, j, k: (k, j))
        dn = (((0,), (0,)), ((), ()))
    if a_index is not None:
        a_spec = pl.BlockSpec(a_spec.block_shape, a_index)
    if b_index is not None:
        b_spec = pl.BlockSpec(b_spec.block_shape, b_index)

    if out3:
        nbo = (N // N_DEV) // tn
        assert nbo * tn * N_DEV == N
        o_spec = pl.BlockSpec((None, tm, tn), lambda i, j, k: (j // nbo, i, j % nbo))
        o_shape = (N_DEV, M, N // N_DEV)
    else:
        o_spec = pl.BlockSpec((tm, tn), lambda i, j, k: (i, j))
        o_shape = (M, N)

    in_specs = [a_spec, b_spec]
    args = [a, b]
    if has_bias:
        in_specs.append(pl.BlockSpec((1, tn), lambda i, j, k: (0, j)))
        args.append(bias)
    for e in extras:
        in_specs.append(pl.BlockSpec((tm, tn), lambda i, j, k: (i, j)))
        args.append(e)
    n_dep = 0 if dep is None else 1
    if n_dep:
        in_specs.append(pl.BlockSpec(memory_space=pl.ANY))
        args.append(dep)

    def body(*refs):
        a_ref, b_ref = refs[0], refs[1]
        pos = 2
        bias_ref = None
        if has_bias:
            bias_ref = refs[pos]
            pos += 1
        extra_refs = refs[pos:pos + n_extra]
        pos += n_extra + n_dep
        out_refs = refs[pos:pos + n_out]
        acc_ref = refs[pos + n_out] if gk > 1 else None

        part = lax.dot_general(a_ref[...].astype(BF16), b_ref[...].astype(BF16), dn,
                               preferred_element_type=F32)

        def finish(acc):
            if has_bias:
                acc = acc + bias_ref[...]
            if epilogue is None:
                vals = (acc,)
            else:
                vals = epilogue(acc, *[e[...] for e in extra_refs])
            for o_ref, val in zip(out_refs, vals):
                o_ref[...] = val.astype(o_ref.dtype)

        if gk == 1:
            finish(part)
        else:
            k = pl.program_id(2)

            @pl.when(k == 0)
            def _():
                acc_ref[...] = part

            @pl.when(k > 0)
            def _():
                acc_ref[...] += part

            @pl.when(k == gk - 1)
            def _():
                finish(acc_ref[...])

    outs = pl.pallas_call(
        body,
        grid=(gm, gn, gk),
        in_specs=in_specs,
        out_specs=[o_spec] * n_out,
        out_shape=[jax.ShapeDtypeStruct(o_shape, dt) for dt in out_dtypes],
        scratch_shapes=([pltpu.VMEM((tm, tn), F32)] if gk > 1 else []),
        compiler_params=_cparams(("parallel", "parallel", "arbitrary")),
        name=name,
    )(*args)
    return outs[0] if n_out == 1 else outs


def _rowwise(fn, rows, vecs, row_outs, sum_outs, *, n_rows, tr, ch, name, dep=None, prefetch=None):
    assert n_rows % tr == 0 and tr % ch == 0
    nt = n_rows // tr
    nr, nv, nro, nso = len(rows), len(vecs), len(row_outs), len(sum_outs)
    in_specs, args = [], []
    n_pf = 0 if prefetch is None else 1
    for (arr, lead, cblk, w, shift) in rows:
        if shift:
            ridx = lambda i, shift=shift: jnp.minimum(i + shift, nt - 1)
        else:
            ridx = lambda i: i
        if arr.ndim == 3:
            def imap(i, *pf, lead=lead, cblk=cblk, ridx=ridx):
                return (lead(pf[0]) if callable(lead) else lead, ridx(i), cblk)
            in_specs.append(pl.BlockSpec((None, tr, w), imap))
        else:
            in_specs.append(pl.BlockSpec(
                (tr, w), lambda i, *pf, cblk=cblk, ridx=ridx: (ridx(i), cblk)))
        args.append(arr)
    for v in vecs:
        in_specs.append(pl.BlockSpec(v.shape, lambda i, *pf, nd=v.ndim: (0,) * nd))
        args.append(v)
    n_dep = 0 if dep is None else 1
    if n_dep:
        in_specs.append(pl.BlockSpec(memory_space=pl.ANY))
        args.append(dep)
    out_specs = [pl.BlockSpec((tr, w), lambda i, *pf: (i, 0)) for (w, _) in row_outs]
    out_shape = [jax.ShapeDtypeStruct((n_rows, w), dt) for (w, dt) in row_outs]
    for (r, w) in sum_outs:
        out_specs.append(pl.BlockSpec((r, w), lambda i, *pf: (0, 0)))
        out_shape.append(jax.ShapeDtypeStruct((r, w), F32))

    def body(*refs):
        refs = refs[n_pf:]
        i = pl.program_id(0)
        r_in = refs[:nr]
        v_in = refs[nr:nr + nv]
        r_out = refs[nr + nv + n_dep:nr + nv + n_dep + nro]
        s_out = refs[nr + nv + n_dep + nro:]
        if nso:
            @pl.when(i == 0)
            def _():
                for s in s_out:
                    s[...] = jnp.zeros(s.shape, F32)
        vvals = [v[...] for v in v_in]

        def chunk(ci, carry):
            r0 = pl.multiple_of(ci * ch, ch)
            rv = [r[pl.ds(r0, ch), :] for r in r_in]
            ro, so = fn(rv, vvals, i, nt)
            for ref, val in zip(r_out, ro):
                ref[pl.ds(r0, ch), :] = val.astype(ref.dtype)
            for ref, val in zip(s_out, so):
                ref[...] += val
            return carry

        lax.fori_loop(0, tr // ch, chunk, 0)

    outs = pl.pallas_call(
        body,
        grid_spec=pltpu.PrefetchScalarGridSpec(
            num_scalar_prefetch=n_pf, grid=(nt,), in_specs=in_specs, out_specs=out_specs),
        out_shape=out_shape,
        compiler_params=_cparams(("arbitrary",)),
        name=name,
    )(*([prefetch] if n_pf else []), *args)
    return outs


def _row(arr, cblk=0, w=None, lead=0, shift=0):
    return (arr, lead, cblk, arr.shape[-1] if w is None else w, shift)


def _colsum(v):
    return jnp.sum(v, axis=0, keepdims=True)


def _rms(x):
    return lax.rsqrt(jnp.mean(x * x, axis=-1, keepdims=True) + EPS)


def _sigmoid(x):
    return 1.0 / (1.0 + jnp.exp(-x))


_GELU_C = math.sqrt(2.0 / math.pi)


def _gelu(x):
    return 0.5 * x * (1.0 + jnp.tanh(_GELU_C * (x + 0.044715 * (x * x * x))))


def _gelu_grad(x):
    t = jnp.tanh(_GELU_C * (x + 0.044715 * (x * x * x)))
    return 0.5 * (1.0 + t) + 0.5 * x * (1.0 - t * t) * (_GELU_C * (1.0 + 3.0 * 0.044715 * (x * x)))


def _my_pos():
    return lax.axis_index("x"), lax.axis_index("y"), lax.axis_index("c")


def _flip(pos, k):
    x, y, c = pos
    return (1 - x if k & 4 else x, 1 - y if k & 2 else y, 1 - c if k & 1 else c)


def _dev_id(pos):
    return 4 * pos[0] + 2 * pos[1] + pos[2]


def _small_allgather(x, name):
    r, c = x.shape

    def body(x_ref, out_ref, send_sems, recv_sems):
        me = _my_pos()
        out_ref[_dev_id(me)] = x_ref[...]
        copies = []
        for k in range(1, N_DEV):
            cp = pltpu.make_async_remote_copy(
                src_ref=x_ref, dst_ref=out_ref.at[_dev_id(me)],
                send_sem=send_sems.at[k - 1], recv_sem=recv_sems.at[k - 1],
                device_id=_flip(me, k), device_id_type=MESH)
            cp.start()
            copies.append(cp)
        for k in range(1, N_DEV):
            peer = _flip(me, k)
            pltpu.make_async_remote_copy(
                src_ref=x_ref, dst_ref=out_ref.at[_dev_id(peer)],
                send_sem=send_sems.at[k - 1], recv_sem=recv_sems.at[k - 1],
                device_id=peer, device_id_type=MESH).wait_recv()
        for cp in copies:
            cp.wait_send()

    return pl.pallas_call(
        body,
        out_shape=jax.ShapeDtypeStruct((N_DEV, r, c), x.dtype),
        in_specs=[pl.BlockSpec(memory_space=pltpu.VMEM)],
        out_specs=pl.BlockSpec(memory_space=pltpu.VMEM),
        scratch_shapes=[pltpu.SemaphoreType.DMA((N_DEV - 1,)),
                        pltpu.SemaphoreType.DMA((N_DEV - 1,))],
        compiler_params=pltpu.CompilerParams(vmem_limit_bytes=VMEM_LIMIT),
        name=name,
    )(x)


_HBM = pl.BlockSpec(memory_space=pltpu.HBM)
_SEM = pl.BlockSpec(memory_space=pltpu.SEMAPHORE)
_EFFECT = pltpu.SideEffectType.DATAFLOW_SIDE_EFFECTING


def _relay_copy(zone, send_sems, recv_sems, k, block, to):
    slot = zone.at[_dev_id(block)]
    return pltpu.make_async_remote_copy(
        src_ref=slot, dst_ref=slot, send_sem=send_sems.at[k], recv_sem=recv_sems.at[k],
        device_id=to, device_id_type=MESH)


def _relay_peers():
    x, y, c = _my_pos()
    return (x, y, c), (x, y, 1 - c), [(1 - x, y), (x, 1 - y), (1 - x, 1 - y)]


def _relay_gather_start(zone, name):
    def body(zone_ref, send, recv, _, token):
        me, sib, chips = _relay_peers()
        _relay_copy(zone_ref, send, recv, 0, me, sib).start()
        for j, chip in enumerate(chips):
            _relay_copy(zone_ref, send, recv, 1 + j, me, (*chip, me[2])).start()
        token[...] = jnp.zeros(token.shape, token.dtype)

    sem = pltpu.SemaphoreType.DMA((4,))
    return pl.pallas_call(
        body,
        name=name,
        out_shape=[sem, sem, pltpu.HBM(zone.shape, zone.dtype), jax.ShapeDtypeStruct((8, 128), F32)],
        in_specs=[_HBM],
        out_specs=[_SEM, _SEM, _HBM, pl.BlockSpec(memory_space=pltpu.VMEM)],
        input_output_aliases={0: 2},
        compiler_params=pltpu.CompilerParams(has_side_effects=_EFFECT),
    )(pltpu.with_memory_space_constraint(zone, pltpu.HBM))


def _relay_gather_arrive(send_sems, recv_sems, zone, after, name):
    def body(zone_ref, send, recv, after_ref, _):
        me, sib, chips = _relay_peers()
        _relay_copy(zone_ref, send, recv, 0, sib, me).wait_recv()
        _relay_copy(zone_ref, send, recv, 0, me, sib).wait_send()
        for j, chip in enumerate(chips):
            _relay_copy(zone_ref, send, recv, 1 + j, (*chip, me[2]), me).wait_recv()
            _relay_copy(zone_ref, send, recv, 1 + j, me, (*chip, me[2])).wait_send()

    return pl.pallas_call(
        body,
        name=name,
        out_shape=pltpu.HBM(zone.shape, zone.dtype),
        in_specs=[_HBM, _SEM, _SEM, pl.BlockSpec(memory_space=pl.ANY)],
        out_specs=_HBM,
        input_output_aliases={0: 0},
        compiler_params=pltpu.CompilerParams(has_side_effects=_EFFECT),
    )(zone, send_sems, recv_sems, after)


def _relay_gather_finish(zone, name):
    def body(_, out, send, recv):
        me, sib, chips = _relay_peers()
        sent = []
        for j, chip in enumerate(chips):
            cp = _relay_copy(out, send, recv, j, (*chip, me[2]), sib)
            cp.start()
            sent.append(cp)
        for j, chip in enumerate(chips):
            _relay_copy(out, send, recv, j, (*chip, sib[2]), me).wait_recv()
        for cp in sent:
            cp.wait_send()

    hbm = pl.BlockSpec(memory_space=pl.ANY)
    return pl.pallas_call(
        body,
        name=name,
        out_shape=jax.ShapeDtypeStruct(zone.shape, zone.dtype),
        in_specs=[hbm],
        out_specs=hbm,
        input_output_aliases={0: 0},
        scratch_shapes=[pltpu.SemaphoreType.DMA((3,)), pltpu.SemaphoreType.DMA((3,))],
    )(zone)


def _exchange_copy(kind, bufs, send_sems, recv_sems, me, k, arriving):
    peer = _flip(me, k)
    my_id, peer_id = _dev_id(me), _dev_id(peer)
    if kind == "gather":
        slot = bufs[0].at[peer_id if arriving else my_id]
        src, dst = slot, slot
    else:
        src = bufs[0].at[my_id if arriving else peer_id]
        dst = bufs[1].at[peer_id if arriving else my_id]
    return pltpu.make_async_remote_copy(
        src_ref=src, dst_ref=dst, send_sem=send_sems.at[k - 1], recv_sem=recv_sems.at[k - 1],
        device_id=peer, device_id_type=MESH)


def _exchange_start(arrays, kind, name, after=None):
    n = len(arrays)
    n_after = 0 if after is None else 1
    if kind == "gather":
        bufs = [[a] for a in arrays]
    else:
        bufs = [[a, lax.empty(a.shape, a.dtype)] for a in arrays]
    nb = len(bufs[0])
    flat = [b for group in bufs for b in group]

    def body(*refs):
        outs_at = nb * n + n_after
        send = refs[outs_at:outs_at + n]
        recv = refs[outs_at + n:outs_at + 2 * n]
        token = refs[outs_at + 2 * n + nb * n]
        me = _my_pos()
        for a in range(n):
            for k in range(1, N_DEV):
                _exchange_copy(kind, refs[nb * a:nb * (a + 1)], send[a], recv[a], me, k, False).start()
        token[...] = jnp.zeros(token.shape, token.dtype)

    sem = pltpu.SemaphoreType.DMA((N_DEV - 1,))
    outs = pl.pallas_call(
        body,
        name=name,
        out_shape=([sem] * (2 * n) + [pltpu.HBM(b.shape, b.dtype) for b in flat]
                   + [jax.ShapeDtypeStruct((8, 128), F32)]),
        in_specs=[_HBM] * (nb * n) + [pl.BlockSpec(memory_space=pl.ANY)] * n_after,
        out_specs=[_SEM] * (2 * n) + [_HBM] * (nb * n) + [pl.BlockSpec(memory_space=pltpu.VMEM)],
        input_output_aliases={i: 2 * n + i for i in range(nb * n)},
        compiler_params=pltpu.CompilerParams(has_side_effects=_EFFECT),
    )(*[pltpu.with_memory_space_constraint(b, pltpu.HBM) for b in flat],
      *([after] if n_after else []))
    flights = [(outs[a], outs[n + a], list(outs[2 * n + nb * a:2 * n + nb * (a + 1)]))
               for a in range(n)]
    return flights, outs[2 * n + nb * n]


def _exchange_wait(flights, kind, after, name):
    n = len(flights)
    nb = len(flights[0][2])
    flat = [b for f in flights for b in f[2]]

    def body(*refs):
        send = refs[nb * n:nb * n + n]
        recv = refs[nb * n + n:nb * n + 2 * n]
        me = _my_pos()
        for a in range(n):
            for k in range(1, N_DEV):
                bufs = refs[nb * a:nb * (a + 1)]
                _exchange_copy(kind, bufs, send[a], recv[a], me, k, False).wait_send()
                _exchange_copy(kind, bufs, send[a], recv[a], me, k, True).wait_recv()

    outs = pl.pallas_call(
        body,
        name=name,
        out_shape=[pltpu.HBM(b.shape, b.dtype) for b in flat],
        in_specs=[_HBM] * (nb * n) + [_SEM] * (2 * n) + [pl.BlockSpec(memory_space=pl.ANY)],
        out_specs=[_HBM] * (nb * n),
        input_output_aliases={i: i for i in range(nb * n)},
        compiler_params=pltpu.CompilerParams(has_side_effects=_EFFECT),
    )(*flat, *[f[0] for f in flights], *[f[1] for f in flights], after)
    return [list(outs[nb * a:nb * (a + 1)]) for a in range(n)]


def _t5_buckets_block():
    qi = np.arange(BLK)[:, None]
    ki = np.arange(2 * BLK)[None, :]
    n = np.maximum(qi + BLK - ki, 0)
    max_exact = NUM_BUCKETS // 2
    large = max_exact + (np.log(np.maximum(n, 1) / max_exact)
                         / np.log(MAX_DISTANCE / max_exact)
                         * (NUM_BUCKETS - max_exact)).astype(np.int32)
    large = np.minimum(large, NUM_BUCKETS - 1)
    return np.where(n < max_exact, n, large).astype(np.int32)


def _band_mask():
    qi = np.arange(BLK)[:, None]
    ki = np.arange(2 * BLK)[None, :]
    dist = qi + BLK - ki
    return (dist >= 0) & (dist < BLK)


def _attn_scores(q_ref, kp_ref, kc_ref, hkv):
    c0 = hkv * HEAD_DIM
    kk = jnp.concatenate([kp_ref[:, c0:c0 + HEAD_DIM], kc_ref[:, c0:c0 + HEAD_DIM]],
                         axis=0).astype(BF16)
    qg = jnp.concatenate(
        [q_ref[:, (hkv * GROUP + g) * HEAD_DIM:(hkv * GROUP + g + 1) * HEAD_DIM]
         for g in range(GROUP)], axis=0).astype(BF16)
    s = lax.dot_general(qg, kk, (((1,), (1,)), ((), ())), preferred_element_type=F32)
    return qg, kk, s


def _attn_softmax(s, bias_ref, sink_ref, hkv):
    r0, r1 = hkv * GROUP * BLK, (hkv + 1) * GROUP * BLK
    s = s * (HEAD_DIM ** -0.5) + bias_ref[r0:r1, :]
    sink = sink_ref[r0:r1, :]
    m = jnp.maximum(jnp.max(s, axis=-1, keepdims=True), sink)
    p = jnp.exp(s - m)
    e_sink = jnp.exp(sink - m)
    inv = 1.0 / (jnp.sum(p, axis=-1, keepdims=True) + e_sink)
    return p * inv, e_sink * inv


def _kv_rows(p_ref, c_ref, hkv):
    c0 = hkv * HEAD_DIM
    return jnp.concatenate([p_ref[:, c0:c0 + HEAD_DIM], c_ref[:, c0:c0 + HEAD_DIM]],
                           axis=0).astype(BF16)


def _attn_in_specs(bias2):
    prev = lambda n: jnp.maximum(n - 1, 0)
    return [
        pl.BlockSpec((BLK, ATTN_W), lambda n: (n, 0)),
        pl.BlockSpec((BLK, KV_W), lambda n: (prev(n), ATTN_W // KV_W)),
        pl.BlockSpec((BLK, KV_W), lambda n: (n, ATTN_W // KV_W)),
        pl.BlockSpec((BLK, KV_W), lambda n: (prev(n), ATTN_W // KV_W + 1)),
        pl.BlockSpec((BLK, KV_W), lambda n: (n, ATTN_W // KV_W + 1)),
        pl.BlockSpec((None,) + bias2.shape[1:], lambda n: (jnp.minimum(n, 1), 0, 0)),
    ]


def _attention_fwd(proj, bias2, sinkcol, n_rows):
    nb = n_rows // BLK

    def body(q_ref, kp_ref, kc_ref, vp_ref, vc_ref, bias_ref, sink_ref, o_ref):
        heads = range(N_KV_HEADS)
        scores = [_attn_scores(q_ref, kp_ref, kc_ref, hkv)[2] for hkv in heads]
        probs = [_attn_softmax(scores[hkv], bias_ref, sink_ref, hkv)[0] for hkv in heads]
        outs = [jnp.dot(probs[hkv].astype(BF16), _kv_rows(vp_ref, vc_ref, hkv),
                        preferred_element_type=F32) for hkv in heads]
        for hkv in heads:
            for g in range(GROUP):
                h = hkv * GROUP + g
                o_ref[:, h * HEAD_DIM:(h + 1) * HEAD_DIM] = (
                    outs[hkv][g * BLK:(g + 1) * BLK, :].astype(o_ref.dtype))

    return pl.pallas_call(
        body,
        grid=(nb,),
        in_specs=_attn_in_specs(bias2) + [pl.BlockSpec(sinkcol.shape, lambda n: (0, 0))],
        out_specs=pl.BlockSpec((BLK, ATTN_W), lambda n: (n, 0)),
        out_shape=jax.ShapeDtypeStruct((n_rows, ATTN_W), BF16),
        compiler_params=_cparams(("parallel",)),
        name="attn_fwd",
    )(proj, proj, proj, proj, proj, bias2, sinkcol)


def _attention_bwd(proj, attn, dattn, bias2, sinkcol, n_rows):
    nb = n_rows // BLK
    scale = HEAD_DIM ** -0.5
    dn_t = (((0,), (0,)), ((), ()))

    def body(q_ref, kp_ref, kc_ref, vp_ref, vc_ref, bias_ref, o_ref, do_ref, sink_ref,
             dq_ref, dkc_ref, dkp_ref, dvc_ref, dvp_ref, dbias_ref, dsink_ref):
        @pl.when(pl.program_id(0) == 0)
        def _():
            dbias_ref[...] = jnp.zeros(dbias_ref.shape, F32)
            dsink_ref[...] = jnp.zeros(dsink_ref.shape, F32)

        heads = range(N_KV_HEADS)
        qk = [_attn_scores(q_ref, kp_ref, kc_ref, hkv) for hkv in heads]
        dog, dps, deltas = [], [], []
        for hkv in heads:
            hs = [hkv * GROUP + g for g in range(GROUP)]
            d_o = jnp.concatenate([do_ref[:, h * HEAD_DIM:(h + 1) * HEAD_DIM] for h in hs], axis=0)
            o = jnp.concatenate([o_ref[:, h * HEAD_DIM:(h + 1) * HEAD_DIM] for h in hs], axis=0)
            deltas.append(jnp.sum(d_o.astype(F32) * o.astype(F32), axis=-1, keepdims=True))
            dog.append(d_o.astype(BF16))
            dps.append(lax.dot_general(dog[hkv], _kv_rows(vp_ref, vc_ref, hkv),
                                       (((1,), (1,)), ((), ())), preferred_element_type=F32))
        p16, ds16 = [], []
        for hkv in heads:
            r0, r1 = hkv * GROUP * BLK, (hkv + 1) * GROUP * BLK
            p, p_sink = _attn_softmax(qk[hkv][2], bias_ref, sink_ref, hkv)
            ds = p * (dps[hkv] - deltas[hkv])
            dbias_ref[r0:r1, :] += ds
            dsink_ref[r0:r1, :] += -(p_sink * deltas[hkv])
            p16.append(p.astype(BF16))
            ds16.append(ds.astype(BF16))
        for hkv in heads:
            c0 = hkv * HEAD_DIM
            qg, kk, _ = qk[hkv]
            dqg = jnp.dot(ds16[hkv], kk, preferred_element_type=F32) * scale
            dkk = lax.dot_general(ds16[hkv], qg, dn_t, preferred_element_type=F32) * scale
            dvv = lax.dot_general(p16[hkv], dog[hkv], dn_t, preferred_element_type=F32)
            for g in range(GROUP):
                h = hkv * GROUP + g
                dq_ref[:, h * HEAD_DIM:(h + 1) * HEAD_DIM] = (
                    dqg[g * BLK:(g + 1) * BLK, :].astype(dq_ref.dtype))
            dkp_ref[:, c0:c0 + HEAD_DIM] = dkk[:BLK].astype(dkp_ref.dtype)
            dkc_ref[:, c0:c0 + HEAD_DIM] = dkk[BLK:].astype(dkc_ref.dtype)
            dvp_ref[:, c0:c0 + HEAD_DIM] = dvv[:BLK].astype(dvp_ref.dtype)
            dvc_ref[:, c0:c0 + HEAD_DIM] = dvv[BLK:].astype(dvc_ref.dtype)

    kv_out = pl.BlockSpec((BLK, KV_W), lambda n: (n, 0))
    kv_shape = jax.ShapeDtypeStruct((n_rows, KV_W), F32)
    acc_shape = bias2.shape[1:]
    return pl.pallas_call(
        body,
        grid=(nb,),
        in_specs=_attn_in_specs(bias2) + [
            pl.BlockSpec((BLK, ATTN_W), lambda n: (n, 0)),
            pl.BlockSpec((BLK, ATTN_W), lambda n: (n, 0)),
            pl.BlockSpec(sinkcol.shape, lambda n: (0, 0)),
        ],
        out_specs=[
            pl.BlockSpec((BLK, ATTN_W), lambda n: (n, 0)),
            kv_out, kv_out, kv_out, kv_out,
            pl.BlockSpec(acc_shape, lambda n: (0, 0)),
            pl.BlockSpec(sinkcol.shape, lambda n: (0, 0)),
        ],
        out_shape=[
            jax.ShapeDtypeStruct((n_rows, ATTN_W), BF16),
            kv_shape, kv_shape, kv_shape, kv_shape,
            jax.ShapeDtypeStruct(acc_shape, F32),
            jax.ShapeDtypeStruct(sinkcol.shape, F32),
        ],
        compiler_params=_cparams(("arbitrary",)),
        name="attn_bwd",
    )(proj, proj, proj, proj, proj, bias2, attn, dattn, sinkcol)


def _bias_tables(rel_bias_t, onehot_t, band_first, band_rest):
    def body(rb_ref, oh_ref, mf_ref, mr_ref, out_ref):
        acc = jnp.zeros((N_Q_HEADS, BLK * 2 * BLK), F32)
        for part in _split3(rb_ref[...]):
            acc = acc + jnp.dot(part, oh_ref[...], preferred_element_type=F32)
        out_ref[0] = jnp.where(mf_ref[...] > 0.0, acc, NEG_INF)
        out_ref[1] = jnp.where(mr_ref[...] > 0.0, acc, NEG_INF)

    return pl.pallas_call(
        body,
        out_shape=jax.ShapeDtypeStruct((2, N_Q_HEADS, BLK * 2 * BLK), F32),
        compiler_params=pltpu.CompilerParams(vmem_limit_bytes=VMEM_LIMIT),
        name="bias_tables",
    )(rel_bias_t, onehot_t, band_first, band_rest)


def _split3(a):
    hi = a.astype(BF16)
    r1 = a - hi.astype(F32)
    mid = r1.astype(BF16)
    lo = (r1 - mid.astype(F32)).astype(BF16)
    return hi, mid, lo


def _bucket_reduce(dbias, dsink, onehot_t):
    def body(db_ref, ds_ref, oh_ref, ob_ref, os_ref):
        acc = jnp.zeros((N_Q_HEADS, 128), F32)
        for part in _split3(db_ref[...]):
            acc = acc + lax.dot_general(part, oh_ref[...], (((1,), (1,)), ((), ())),
                                        preferred_element_type=F32)
        ob_ref[...] = acc
        os_ref[...] = jnp.broadcast_to(jnp.sum(ds_ref[...], axis=-1, keepdims=True),
                                       os_ref.shape)

    return pl.pallas_call(
        body,
        out_shape=[jax.ShapeDtypeStruct((N_Q_HEADS, 128), F32),
                   jax.ShapeDtypeStruct((N_Q_HEADS, 128), F32)],
        compiler_params=pltpu.CompilerParams(vmem_limit_bytes=VMEM_LIMIT),
        name="bias_bucket_reduce",
    )(dbias, dsink, onehot_t)


def _disc(lr, li, ls, btr, bti):
    lam_re = jnp.minimum(lr, -1e-4)
    delta = jnp.exp(ls)
    mag = jnp.exp(lam_re * delta)
    ang = li * delta
    ar, ai = mag * jnp.cos(ang), mag * jnp.sin(ang)
    nr, ni = ar - 1.0, ai
    den = lam_re * lam_re + li * li
    fr = (nr * lam_re + ni * li) / den
    fi = (ni * lam_re - nr * li) / den
    bbr = fr * btr - fi * bti
    bbi = fr * bti + fi * btr
    return ar, ai, bbr, bbi


def _block_mask():
    row = lax.broadcasted_iota(jnp.int32, (SSM_W, SSM_H), 0)
    col = lax.broadcasted_iota(jnp.int32, (SSM_W, SSM_H), 1)
    return (row // SSM_P) == (col // SSM_N)


def _ssm_setup(lr, li, ls, btr, bti, ctr, cti):
    def body(lr_ref, li_ref, ls_ref, btr_ref, bti_ref, ctr_ref, cti_ref, a_ref, b_ref, c_ref):
        ar, ai, bbr, bbi = _disc(lr_ref[...], li_ref[...], ls_ref[...], btr_ref[...], bti_ref[...])
        a_ref[:, :SSM_H] = ar
        a_ref[:, SSM_H:] = ai
        mask = _block_mask()
        blk = lambda t: jnp.where(mask, jnp.tile(t, (SSM_G, 1)), 0.0)
        b_ref[:, :SSM_H] = blk(bbr).astype(BF16)
        b_ref[:, SSM_H:] = blk(bbi).astype(BF16)
        c_ref[:, :SSM_H] = blk(ctr_ref[...]).astype(BF16)
        c_ref[:, SSM_H:] = blk(-cti_ref[...]).astype(BF16)

    return pl.pallas_call(
        body,
        out_shape=[jax.ShapeDtypeStruct((1, 2 * SSM_H), F32),
                   jax.ShapeDtypeStruct((SSM_W, 2 * SSM_H), BF16),
                   jax.ShapeDtypeStruct((SSM_W, 2 * SSM_H), BF16)],
        compiler_params=pltpu.CompilerParams(vmem_limit_bytes=VMEM_LIMIT),
        name="ssm_setup",
    )(lr, li, ls, btr, bti, ctr, cti)


def _ssm_param_bwd(lr, li, ls, btr, bti, dacc, dbcat, dccat, gind):
    def body(lr_ref, li_ref, ls_ref, btr_ref, bti_ref, dacc_ref, db_ref, dc_ref, g_ref,
             dlr_ref, dli_ref, dls_ref, dbtr_ref, dbti_ref, dctr_ref, dcti_ref):
        dar = jnp.sum(dacc_ref[:, :SSM_H], axis=0, keepdims=True)
        dai = jnp.sum(dacc_ref[:, SSM_H:], axis=0, keepdims=True)
        col = lax.broadcasted_iota(jnp.int32, (SSM_P, 2 * SSM_H), 1)
        grp = (col % SSM_H) // SSM_N
        db = jnp.zeros((SSM_P, 2 * SSM_H), F32)
        dc = jnp.zeros((SSM_P, 2 * SSM_H), F32)
        half = SSM_G // 2
        for g in range(SSM_G):
            sel = grp == g
            r0 = (g % half) * SSM_P
            db = db + jnp.where(sel, db_ref[r0:r0 + SSM_P, :], 0.0)
            dc = dc + jnp.where(sel, dc_ref[r0:r0 + SSM_P, :], 0.0)
        dctr_ref[...] = dc[:, :SSM_H]
        dcti_ref[...] = -dc[:, SSM_H:]
        prim = (lr_ref[...], li_ref[...], ls_ref[...], btr_ref[...], bti_ref[...])
        _, vjp = jax.vjp(_disc, *prim)
        dlr, dli, dls, dbtr, dbti = vjp((dar, dai, db[:, :SSM_H], db[:, SSM_H:]))
        dlr_ref[...] = dlr
        dli_ref[...] = dli
        dbtr_ref[...] = dbtr
        dbti_ref[...] = dbti
        acc = jnp.zeros((8, 128), F32)
        for part in _split3(jnp.broadcast_to(dls, (8, SSM_H))):
            acc = acc + jnp.dot(part, g_ref[...], preferred_element_type=F32)
        dls_ref[...] = acc

    vec = jax.ShapeDtypeStruct((1, SSM_H), F32)
    mat = jax.ShapeDtypeStruct((SSM_P, SSM_H), F32)
    return pl.pallas_call(
        body,
        out_shape=[vec, vec, jax.ShapeDtypeStruct((8, 128), F32), mat, mat, mat, mat],
        compiler_params=pltpu.CompilerParams(vmem_limit_bytes=VMEM_LIMIT),
        name="ssm_param_bwd",
    )(lr, li, ls, btr, bti, dacc, dbcat, dccat, gind)


SCAN_TR = 256


def _cmul_add(vr, vi, pr, pi, sr, si):
    return vr + pr * sr - pi * si, vi + pr * si + pi * sr


def _bcast_row(v, row, which):
    b = jnp.where(row == which, v, 0.0)
    b = b + pltpu.roll(b, 4, 0)
    b = b + pltpu.roll(b, 2, 0)
    return b + pltpu.roll(b, 1, 0)


def _scan_tables(a_ref, tab_ref, reverse):
    H = SSM_H
    ar = jnp.broadcast_to(a_ref[:, :H], (8, H))
    ai = jnp.broadcast_to(a_ref[:, H:], (8, H))
    if reverse:
        ai = -ai
    row = lax.broadcasted_iota(jnp.int32, (8, H), 0)
    pw = [(ar, ai)]
    for _ in range(7):
        cr, ci = pw[-1]
        pw.append((cr * ar - ci * ai, cr * ai + ci * ar))
    pcr = jnp.zeros((8, H), F32)
    pci = jnp.zeros((8, H), F32)
    for e in range(8):
        sel = (row == (7 - e)) if reverse else (row == e)
        pcr = jnp.where(sel, pw[e][0], pcr)
        pci = jnp.where(sel, pw[e][1], pci)
    tab_ref[0, :, :H] = pcr
    tab_ref[0, :, H:] = pci
    for t, k in enumerate((1, 2, 4)):
        keep = (row < 8 - k) if reverse else (row >= k)
        tab_ref[1 + t, :, :H] = jnp.where(keep, pw[k - 1][0], 0.0)
        tab_ref[1 + t, :, H:] = jnp.where(keep, pw[k - 1][1], 0.0)


def _scan_group(vr, vi, cr, ci, tab_ref, reverse):
    H = SSM_H
    for t, k in enumerate((1, 2, 4)):
        sh = 8 - k if reverse else k
        vr, vi = _cmul_add(vr, vi, tab_ref[1 + t, :, :H], tab_ref[1 + t, :, H:],
                           pltpu.roll(vr, sh, 0), pltpu.roll(vi, sh, 0))
    return _cmul_add(vr, vi, tab_ref[0, :, :H], tab_ref[0, :, H:], cr, ci)


def _scan_fwd(proj, u_blk, bcat, ccat, abar, n_rows):
    H = SSM_H
    nt = n_rows // SCAN_TR

    def body(u_ref, b_ref, c_ref, a_ref, xs_ref, xp_ref, yc_ref, bu_ref, tab_ref, carry_ref):
        @pl.when(pl.program_id(0) == 0)
        def _():
            _scan_tables(a_ref, tab_ref, False)
            carry_ref[...] = jnp.zeros(carry_ref.shape, F32)

        bu_ref[...] = jnp.dot(u_ref[...].astype(BF16), b_ref[...], preferred_element_type=F32)
        row = lax.broadcasted_iota(jnp.int32, (8, H), 0)

        def group(j, carry):
            cr, ci = carry
            r0 = pl.multiple_of(j * 16, 16)
            xr, xi = [], []
            for half in range(2):
                rr = pl.multiple_of(r0 + 8 * half, 8)
                vr, vi = _scan_group(bu_ref[pl.ds(rr, 8), :H], bu_ref[pl.ds(rr, 8), H:],
                                     cr, ci, tab_ref, False)
                xp_ref[pl.ds(rr, 8), :H] = jnp.where(row == 0, cr, pltpu.roll(vr, 1, 0))
                xp_ref[pl.ds(rr, 8), H:] = jnp.where(row == 0, ci, pltpu.roll(vi, 1, 0))
                cr, ci = _bcast_row(vr, row, 7), _bcast_row(vi, row, 7)
                xr.append(vr)
                xi.append(vi)
            xs_ref[pl.ds(r0, 16), :H] = jnp.concatenate(xr, axis=0).astype(BF16)
            xs_ref[pl.ds(r0, 16), H:] = jnp.concatenate(xi, axis=0).astype(BF16)
            return cr, ci

        cr, ci = lax.fori_loop(0, SCAN_TR // 16, group,
                               (carry_ref[:, :H], carry_ref[:, H:]))
        carry_ref[:, :H] = cr
        carry_ref[:, H:] = ci
        yc_ref[...] = lax.dot_general(xs_ref[...], c_ref[...], (((1,), (1,)), ((), ())),
                                      preferred_element_type=F32)

    tile = lambda w: pl.BlockSpec((SCAN_TR, w), lambda i: (i, 0))
    whole = lambda a: pl.BlockSpec(a.shape, lambda i: (0, 0))
    return pl.pallas_call(
        body,
        grid=(nt,),
        in_specs=[pl.BlockSpec((SCAN_TR, SSM_W), lambda i: (i, u_blk)),
                  whole(bcat), whole(ccat), whole(abar)],
        out_specs=[tile(2 * H), tile(2 * H), tile(SSM_W)],
        out_shape=[jax.ShapeDtypeStruct((n_rows, 2 * H), BF16),
                   jax.ShapeDtypeStruct((n_rows, 2 * H), F32),
                   jax.ShapeDtypeStruct((n_rows, SSM_W), F32)],
        scratch_shapes=[pltpu.VMEM((SCAN_TR, 2 * H), F32), pltpu.VMEM((4, 8, 2 * H), F32),
                        pltpu.VMEM((8, 2 * H), F32)],
        compiler_params=_cparams(("arbitrary",)),
        name="ssm_scan_fwd",
    )(proj, bcat, ccat, abar)


def _scan_bwd(dy, xprev, bcat, ccat, abar, n_rows):
    H = SSM_H
    nt = n_rows // SCAN_TR

    def body(dy_ref, xp_ref, b_ref, c_ref, a_ref, h_ref, da_ref, du_ref, g_ref, tab_ref, carry_ref):
        @pl.when(pl.program_id(0) == 0)
        def _():
            _scan_tables(a_ref, tab_ref, True)
            carry_ref[...] = jnp.zeros(carry_ref.shape, F32)
            da_ref[...] = jnp.zeros(da_ref.shape, F32)

        g_ref[...] = jnp.dot(dy_ref[...], c_ref[...], preferred_element_type=F32)
        row = lax.broadcasted_iota(jnp.int32, (8, H), 0)
        n16 = SCAN_TR // 16

        def group(jj, carry):
            cr, ci = carry
            r0 = pl.multiple_of((n16 - 1 - jj) * 16, 16)
            hr, hi = [None, None], [None, None]
            for half in (1, 0):
                rr = pl.multiple_of(r0 + 8 * half, 8)
                vr, vi = _scan_group(g_ref[pl.ds(rr, 8), :H], g_ref[pl.ds(rr, 8), H:],
                                     cr, ci, tab_ref, True)
                pr, pi = xp_ref[pl.ds(rr, 8), :H], xp_ref[pl.ds(rr, 8), H:]
                da_ref[:, :H] += vr * pr + vi * pi
                da_ref[:, H:] += vi * pr - vr * pi
                cr, ci = _bcast_row(vr, row, 0), _bcast_row(vi, row, 0)
                hr[half], hi[half] = vr, vi
            h_ref[pl.ds(r0, 16), :H] = jnp.concatenate(hr, axis=0).astype(BF16)
            h_ref[pl.ds(r0, 16), H:] = jnp.concatenate(hi, axis=0).astype(BF16)
            return cr, ci

        cr, ci = lax.fori_loop(0, n16, group, (carry_ref[:, :H], carry_ref[:, H:]))
        carry_ref[:, :H] = cr
        carry_ref[:, H:] = ci
        du_ref[...] = lax.dot_general(h_ref[...], b_ref[...], (((1,), (1,)), ((), ())),
                                      preferred_element_type=F32)

    rev = lambda i: (nt - 1 - i, 0)
    whole = lambda a: pl.BlockSpec(a.shape, lambda i: (0, 0))
    return pl.pallas_call(
        body,
        grid=(nt,),
        in_specs=[pl.BlockSpec((SCAN_TR, SSM_W), rev),
                  pl.BlockSpec((SCAN_TR, 2 * H), rev),
                  whole(bcat), whole(ccat), whole(abar)],
        out_specs=[pl.BlockSpec((SCAN_TR, 2 * H), rev),
                   pl.BlockSpec((8, 2 * H), lambda i: (0, 0)),
                   pl.BlockSpec((SCAN_TR, SSM_W), rev)],
        out_shape=[jax.ShapeDtypeStruct((n_rows, 2 * H), BF16),
                   jax.ShapeDtypeStruct((8, 2 * H), F32),
                   jax.ShapeDtypeStruct((n_rows, SSM_W), F32)],
        scratch_shapes=[pltpu.VMEM((SCAN_TR, 2 * H), F32), pltpu.VMEM((4, 8, 2 * H), F32),
                        pltpu.VMEM((8, 2 * H), F32)],
        compiler_params=_cparams(("arbitrary",)),
        name="ssm_scan_bwd",
    )(dy, xprev, bcat, ccat, abar)


def _adamw(parts, w, m, v, *, tr, ch, name, prefetch=None):
    n_rows, cols = w.shape
    n_parts = len(parts)
    c1 = 1.0 - ADAM_B1 ** ADAM_STEP
    c2 = 1.0 - ADAM_B2 ** ADAM_STEP

    def fn(rv, vv, i, nt):
        g = rv[0].astype(F32)
        for p in rv[1:n_parts]:
            g = g + p.astype(F32)
        wv, mv, vval = rv[n_parts:]
        nm = ADAM_B1 * mv + (1.0 - ADAM_B1) * g
        nv = ADAM_B2 * vval + (1.0 - ADAM_B2) * (g * g)
        delta = -ADAM_LR * ((nm / c1) / (jnp.sqrt(nv / c2) + ADAM_EPS) + ADAM_WD * wv)
        return [g, delta, nm, nv], []

    rows = [_row(arr, lead=lead) for (arr, lead) in parts] + [_row(w), _row(m), _row(v)]
    return _rowwise(fn, rows, [], [(cols, F32)] * 4, [], n_rows=n_rows, tr=tr, ch=ch, name=name,
                    prefetch=prefetch)


_PACK = [
    ("b_ada", 6), ("norm1_g", 1), ("b_in", 3), ("norm2_g", 1), ("final_g", 1),
    ("lambda_re", 1), ("lambda_im", 1), ("log_step", 1), ("attn_sinks", 1),
    ("rel_bias", 1), ("b_glu", 1), ("ssm_d", 1), ("loss", 1),
    ("ssm_b_re", 16), ("ssm_b_im", 16), ("ssm_c_re", 16), ("ssm_c_im", 16),
]
_PACK_OFF = {}
_off = 0
for _n, _r in _PACK:
    _PACK_OFF[_n] = (_off, _r)
    _off += _r
PACK_ROWS = -(-_off // 8) * 8


def _to_rows(a, rows):
    flat = a.reshape(-1).astype(F32)
    pad = rows * PACK_W - flat.shape[0]
    if pad:
        flat = jnp.pad(flat, (0, pad))
    return flat.reshape(rows, PACK_W)


def _b_to_rows(b):
    return jnp.transpose(b, (2, 0, 1)).reshape(SSM_P, SSM_H)


def _rows_to_b(r):
    return jnp.transpose(r.reshape(SSM_P, SSM_G, SSM_N), (1, 2, 0))


def _c_to_rows(cm):
    return jnp.transpose(cm, (1, 0, 2)).reshape(SSM_P, SSM_H)


def _rows_to_c(r):
    return jnp.transpose(r.reshape(SSM_P, SSM_G, SSM_N), (1, 0, 2))


def _pack(vals):
    out = jnp.zeros((PACK_ROWS, PACK_W), F32)
    for n, r in _PACK:
        if n in vals:
            pieces = vals[n] if isinstance(vals[n], list) else [vals[n]]
            rows_each = r // len(pieces)
            for i, piece in enumerate(pieces):
                out = lax.dynamic_update_slice(out, _to_rows(piece, rows_each),
                                               (_PACK_OFF[n][0] + i * rows_each, 0))
    return out


def _unpack(packed, name, shape):
    o, r = _PACK_OFF[name]
    n = int(np.prod(shape))
    return packed[o:o + r].reshape(-1)[:n].reshape(shape)


def _small_params_packed(p):
    return {
        "b_ada": p["b_ada"], "norm1_g": p["norm1_g"], "b_in": p["b_in"],
        "norm2_g": p["norm2_g"], "final_g": p["final_g"],
        "lambda_re": p["lambda_re"], "lambda_im": p["lambda_im"],
        "log_step": p["log_step"], "attn_sinks": p["attn_sinks"],
        "rel_bias": p["rel_bias"], "b_glu": p["b_glu"], "ssm_d": p["ssm_d"],
        "ssm_b_re": _b_to_rows(p["ssm_b_re"][0]), "ssm_b_im": _b_to_rows(p["ssm_b_im"][0]),
        "ssm_c_re": _c_to_rows(p["ssm_c_re"][0]), "ssm_c_im": _c_to_rows(p["ssm_c_im"][0]),
    }


_SMALL_SHAPES = {
    "b_ada": (1, N_MOD * D), "norm1_g": (1, D), "b_in": (1, IN_W), "norm2_g": (1, D),
    "final_g": (D,), "lambda_re": (1, SSM_G, SSM_N), "lambda_im": (1, SSM_G, SSM_N),
    "log_step": (1, SSM_G), "attn_sinks": (1, N_Q_HEADS), "rel_bias": (NUM_BUCKETS, N_Q_HEADS),
    "b_glu": (1, SSM_W), "ssm_d": (1, SSM_W),
}


def _unpack_small(packed, name):
    if name in ("ssm_b_re", "ssm_b_im"):
        o, r = _PACK_OFF[name]
        return _rows_to_b(packed[o:o + r])[None]
    if name in ("ssm_c_re", "ssm_c_im"):
        o, r = _PACK_OFF[name]
        return _rows_to_c(packed[o:o + r])[None]
    return _unpack(packed, name, _SMALL_SHAPES[name])


WEIGHT_ORDER = ['w_ada', 'b_ada', 'norm1_g', 'w_in', 'b_in', 'attn_sinks', 'rel_bias', 'lambda_re',
                'lambda_im', 'log_step', 'ssm_b_re', 'ssm_b_im', 'ssm_c_re', 'ssm_c_im', 'ssm_d',
                'w_glu', 'b_glu', 'w_attn_proj', 'w_ssm_proj', 'w_out', 'norm2_g', 'w_ff1', 'w_ff2',
                'final_g']
BIG = ['w_in', 'w_glu', 'w_attn_proj', 'w_ssm_proj', 'w_out', 'w_ff1', 'w_ff2']


ADAMW_TILE_ELEMS = 1 << 18


def _to_col_blocks(w):
    k, n = w.shape
    return jnp.transpose(w.reshape(k, N_DEV, n // N_DEV), (1, 0, 2))


def _adamw_rows(rows, cols):
    tr = rows
    while tr * cols > ADAMW_TILE_ELEMS and tr % 32 == 0:
        tr //= 2
    return tr


def _cast_to_slot(w, me1, name):
    rows, cols = w.shape
    tr = min(rows, 256)

    def body(me_ref, w_ref, o_ref):
        o_ref[...] = w_ref[...].astype(BF16)

    return pl.pallas_call(
        body,
        grid_spec=pltpu.PrefetchScalarGridSpec(
            num_scalar_prefetch=1, grid=(rows // tr,),
            in_specs=[pl.BlockSpec((tr, cols), lambda i, me_ref: (i, 0))],
            out_specs=pl.BlockSpec((None, tr, cols), lambda i, me_ref: (me_ref[0], i, 0))),
        out_shape=jax.ShapeDtypeStruct((N_DEV, rows, cols), BF16),
        compiler_params=_cparams(("arbitrary",)),
        name=name,
    )(me1, w)


def kernel(x, c, w_ada, b_ada, norm1_g, w_in, b_in, attn_sinks, rel_bias, lambda_re, lambda_im, log_step, ssm_b_re, ssm_b_im, ssm_c_re, ssm_c_im, ssm_d, w_glu, b_glu, w_attn_proj, w_ssm_proj, w_out, norm2_g, w_ff1, w_ff2, final_g, loss_target, m_w_ada, m_b_ada, m_norm1_g, m_w_in, m_b_in, m_attn_sinks, m_rel_bias, m_lambda_re, m_lambda_im, m_log_step, m_ssm_b_re, m_ssm_b_im, m_ssm_c_re, m_ssm_c_im, m_ssm_d, m_w_glu, m_b_glu, m_w_attn_proj, m_w_ssm_proj, m_w_out, m_norm2_g, m_w_ff1, m_w_ff2, m_final_g, v_w_ada, v_b_ada, v_norm1_g, v_w_in, v_b_in, v_attn_sinks, v_rel_bias, v_lambda_re, v_lambda_im, v_log_step, v_ssm_b_re, v_ssm_b_im, v_ssm_c_re, v_ssm_c_im, v_ssm_d, v_w_glu, v_b_glu, v_w_attn_proj, v_w_ssm_proj, v_w_out, v_norm2_g, v_w_ff1, v_w_ff2, v_final_g):
    loc = dict(locals())
    W = {n: loc[n] for n in WEIGHT_ORDER}
    Mo = {n: loc["m_" + n] for n in WEIGHT_ORDER}
    Vo = {n: loc["v_" + n] for n in WEIGHT_ORDER}
    S = x.shape[1]
    TM = min(512, S)
    TS = min(1024, S)
    TR = min(256, S)
    TW = min(1024, S)
    me = 4 * lax.axis_index("x") + 2 * lax.axis_index("y") + lax.axis_index("c")
    x2d = x.reshape(S, D)
    tgt = loss_target.reshape(S, D)

    shard = {n: W[n][0] for n in BIG}
    me1 = jnp.reshape(me, (1,)).astype(jnp.int32)
    zone = {n: _cast_to_slot(shard[n], me1, "cast_" + n) for n in BIG}
    send_in, recv_in, zone_in, tok_in = _relay_gather_start(zone["w_in"], "w_in_start")
    G = {}

    c_all = _small_allgather(c, "allgather_c").reshape(N_DEV, D)
    cs = _rowwise(lambda rv, vv, i, nt: ([rv[0] * _sigmoid(rv[0])], []), [_row(c_all)], [],
                  [(D, F32)], [], n_rows=N_DEV, tr=8, ch=8, name="silu_c")[0]
    n_ada = N_MOD * D // N_DEV
    b_ada_cols = lax.dynamic_slice(b_ada, (0, me * n_ada), (1, n_ada))
    mod_piece = _matmul(cs, w_ada[0], mode="nn", dims=(N_DEV, n_ada, D), tiles=(N_DEV, 512, D),
                        out_dtypes=[F32], name="ada_fwd", bias=b_ada_cols)
    mod_all = _small_allgather(mod_piece, "allgather_mod")
    mod_b = lax.dynamic_index_in_dim(mod_all, me, axis=1, keepdims=False).reshape(N_MOD, D)
    sh1, sc1, g1, sh2, sc2, g2 = [mod_b[i:i + 1] for i in range(N_MOD)]

    def f_norm1(rv, vv, i, nt):
        xv, (g, sc, sh) = rv[0], vv
        return [(xv * _rms(xv) * g) * (1.0 + sc) + sh], []

    h = _rowwise(f_norm1, [_row(x2d)], [norm1_g, sc1, sh1], [(D, BF16)], [],
                 n_rows=S, tr=TR, ch=32, name="norm1_fwd", dep=tok_in)[0]
    zone_in = _relay_gather_arrive(send_in, recv_in, zone_in, h, "w_in_arrive")
    later = [n for n in BIG if n != "w_in"]
    flights, tok_w = _exchange_start([zone[n] for n in later], "gather", "weights_start", zone_in)
    w_flight = dict(zip(later, flights))
    G["w_in"] = _relay_gather_finish(zone_in, "w_in_relay")
    proj = _matmul(h, G["w_in"], mode="nn", dims=(S, IN_W, D), tiles=(TM, 768, D),
                   out_dtypes=[F32], name="in_proj", b3=True, bias=b_in, dep=tok_w)

    buckets = _t5_buckets_block()
    band = _band_mask()
    onehot_t = jnp.asarray(
        (np.arange(128)[:, None] == buckets.reshape(-1)[None, :]).astype(np.float32), BF16)
    band_first = band & (np.arange(2 * BLK)[None, :] >= BLK)
    rel_bias_t = jnp.pad(jnp.transpose(rel_bias), ((0, 0), (0, 128 - NUM_BUCKETS)))
    bias2 = _bias_tables(rel_bias_t, onehot_t,
                         jnp.asarray(band_first.reshape(1, -1).astype(np.float32)),
                         jnp.asarray(band.reshape(1, -1).astype(np.float32))
                         ).reshape(2, N_Q_HEADS * BLK, 2 * BLK)
    sinkcol = jnp.repeat(attn_sinks.reshape(N_Q_HEADS), BLK).reshape(N_Q_HEADS * BLK, 1)
    attn = _attention_fwd(proj, bias2, sinkcol, S)
    mixer = ["w_attn_proj", "w_glu", "w_ssm_proj", "w_out"]
    landed = _exchange_wait([w_flight[n] for n in mixer], "gather", attn, "weights_wait_mixer")
    G.update((n, bufs[0]) for n, bufs in zip(mixer, landed))
    w_glu_f = G["w_glu"].reshape(SSM_W, SSM_W)
    w_out_f = G["w_out"].reshape(D, D)
    w_ap_f = jnp.transpose(G["w_attn_proj"], (1, 0, 2)).reshape(ATTN_W, D)
    w_sp_f = jnp.transpose(G["w_ssm_proj"], (1, 0, 2)).reshape(SSM_W, D)
    y_attn = _matmul(attn, w_ap_f, mode="nn", dims=(S, D, ATTN_W), tiles=(TW, 1024, ATTN_W),
                     out_dtypes=[F32], name="attn_proj")

    lam_re = lambda_re.reshape(1, SSM_H)
    lam_im = lambda_im.reshape(1, SSM_H)
    ls_x = jnp.repeat(log_step.reshape(SSM_G), SSM_N).reshape(1, SSM_H)
    btr, bti = _b_to_rows(ssm_b_re[0]), _b_to_rows(ssm_b_im[0])
    ctr, cti = _c_to_rows(ssm_c_re[0]), _c_to_rows(ssm_c_im[0])
    abar, bcat, ccat = _ssm_setup(lam_re, lam_im, ls_x, btr, bti, ctr, cti)
    u_blk = (ATTN_W + 2 * KV_W) // SSM_W
    xs, xprev, yc = _scan_fwd(proj, u_blk, bcat, ccat, abar, S)

    def f_ssm_out(rv, vv, i, nt):
        y = rv[0] + vv[0] * rv[1]
        return [y, _gelu(y)], []

    y_ssm_pre, z = _rowwise(f_ssm_out, [_row(yc), _row(proj, u_blk, SSM_W)], [ssm_d],
                            [(SSM_W, F32), (SSM_W, BF16)], [], n_rows=S, tr=TM, ch=32, name="ssm_out")
    zg = _matmul(z, w_glu_f, mode="nn", dims=(S, SSM_W, SSM_W), tiles=(TM, SSM_W, SSM_W),
                 out_dtypes=[F32], name="glu_proj", bias=b_glu)
    z2 = _rowwise(lambda rv, vv, i, nt: ([rv[0].astype(F32) * _sigmoid(rv[1])], []),
                  [_row(z), _row(zg)], [], [(SSM_W, BF16)], [], n_rows=S, tr=TM, ch=32, name="glu_gate")[0]
    y_ssm = _matmul(z2, w_sp_f, mode="nn", dims=(S, D, SSM_W), tiles=(TW, 1024, SSM_W),
                    out_dtypes=[F32], name="ssm_proj")

    ga_row = _row(proj, 1, D)
    gs_row = _row(proj, 2, D)

    def f_merge(rv, vv, i, nt):
        ga, gs, ya, ys = rv
        return [_sigmoid(ga) * ya + _sigmoid(gs) * ys], []

    merged = _rowwise(f_merge, [ga_row, gs_row, _row(y_attn), _row(y_ssm)], [], [(D, BF16)], [],
                      n_rows=S, tr=TR, ch=32, name="merge")[0]
    mo = _matmul(merged, w_out_f, mode="nn", dims=(S, D, D), tiles=(TW, 1024, D),
                 out_dtypes=[F32], name="out_proj")

    def f_norm2(rv, vv, i, nt):
        xv, mv = rv
        g1v, g, sc, sh = vv
        x1v = xv + g1v * mv
        return [x1v, (x1v * _rms(x1v) * g) * (1.0 + sc) + sh], []

    x1, h2 = _rowwise(f_norm2, [_row(x2d), _row(mo)], [g1, norm2_g, sc2, sh2],
                      [(D, F32), (D, BF16)], [], n_rows=S, tr=TR, ch=32, name="norm2_fwd")

    def relu_sq(acc):
        r = jnp.maximum(acc, 0.0)
        return r * r, r

    G["w_ff1"] = _exchange_wait([w_flight["w_ff1"]], "gather", h2, "weights_wait_ff1")[0][0]
    act, relu = _matmul(h2, G["w_ff1"], mode="nn", dims=(S, D_FF, D), tiles=(TM, 1024, D),
                        out_dtypes=[BF16, BF16], name="ff1", b3=True, epilogue=relu_sq)
    w_ff2_f = _exchange_wait([w_flight["w_ff2"]], "gather", act, "weights_wait_ff2")[0][0].reshape(D_FF, D)
    ff = _matmul(act, w_ff2_f, mode="nn", dims=(S, D, D_FF), tiles=(TM, 1024, 2048),
                 out_dtypes=[F32], name="ff2")

    def f_loss(rv, vv, i, nt):
        x1v, ffv, tv = rv
        g2v, gf = vv
        x2v = x1v + g2v * ffv
        r = _rms(x2v)
        xh = x2v * r
        diff = xh * gf - tv
        dy = diff * (1.0 / D)
        dxh = dy * gf
        dx2 = r * (dxh - xh * jnp.mean(dxh * xh, axis=-1, keepdims=True))
        return [dx2, dx2 * g2v], [_colsum(0.5 * diff * diff * (1.0 / D)), _colsum(dy * xh),
                                  _colsum(dx2 * ffv)]

    dx2, dff, loss_cols, d_final_g, dg2 = _rowwise(
        f_loss, [_row(x1), _row(ff), _row(tgt)], [g2, final_g.reshape(1, D)],
        [(D, F32), (D, BF16)], [(1, D)] * 3, n_rows=S, tr=TR, ch=32, name="loss_bwd")

    df1 = _matmul(dff, w_ff2_f, mode="nt", dims=(S, D_FF, D), tiles=(TM, 1024, D),
                  out_dtypes=[BF16], name="ff2_dgrad", extras=(relu,),
                  epilogue=lambda acc, r: (acc * (2.0 * r.astype(F32)),))
    gw_ff2 = _matmul(act, dff, mode="tn", dims=(D_FF, D, S), tiles=(1024, 1024, TS),
                     out_dtypes=[BF16], name="ff2_wgrad").reshape(N_DEV, D_FF // N_DEV, D)
    g_flight = {}
    (g_flight["w_ff2"],), tok = _exchange_start([gw_ff2], "scatter", "grads_start_ff2")
    dh2 = _matmul(df1, G["w_ff1"], mode="nt", dims=(S, D, D_FF), tiles=(TM, D, 1024),
                  out_dtypes=[F32], name="ff1_dgrad", b3=True, dep=tok)
    gw_ff1 = _matmul(h2, df1, mode="tn", dims=(D, D_FF, S), tiles=(1024, 1024, TS),
                     out_dtypes=[BF16], name="ff1_wgrad", out3=True)
    (g_flight["w_ff1"],), tok = _exchange_start([gw_ff1], "scatter", "grads_start_ff1")

    def f_norm2_bwd(rv, vv, i, nt):
        x1v, dh, dx2v, mv = rv
        g, sc, g1v = vv
        r = _rms(x1v)
        xh = x1v * r
        t = xh * g
        dt = dh * (1.0 + sc)
        dxh = dt * g
        dx1 = dx2v + r * (dxh - xh * jnp.mean(dxh * xh, axis=-1, keepdims=True))
        return [dx1, dx1 * g1v], [_colsum(dh), _colsum(dh * t), _colsum(dt * xh), _colsum(dx1 * mv)]

    dx1, dmo, dsh2, dsc2, d_norm2_g, dg1 = _rowwise(
        f_norm2_bwd, [_row(x1), _row(dh2), _row(dx2), _row(mo)], [norm2_g, sc2, g1],
        [(D, F32), (D, BF16)], [(1, D)] * 4, n_rows=S, tr=TR, ch=16, name="norm2_bwd", dep=tok)

    dmerged = _matmul(dmo, w_out_f, mode="nt", dims=(S, D, D), tiles=(TW, 1024, D),
                      out_dtypes=[F32], name="out_dgrad")
    gw_out = _matmul(merged, dmo, mode="tn", dims=(D, D, S), tiles=(1024, 1024, TS),
                     out_dtypes=[BF16], name="out_wgrad").reshape(N_DEV, D // N_DEV, D)
    (g_flight["w_out"],), tok = _exchange_start([gw_out], "scatter", "grads_start_out")

    def f_merge_bwd(rv, vv, i, nt):
        dm, ga, gs, ya, ys = rv
        sa, ss = _sigmoid(ga), _sigmoid(gs)
        return [dm * sa, dm * ss, dm * ya * sa * (1.0 - sa), dm * ys * ss * (1.0 - ss)], []

    dy_attn, dy_ssm, dga, dgs = _rowwise(
        f_merge_bwd, [_row(dmerged), ga_row, gs_row, _row(y_attn), _row(y_ssm)], [],
        [(D, BF16)] * 4, [], n_rows=S, tr=TR, ch=16, name="merge_bwd", dep=tok)

    dz2 = _matmul(dy_ssm, w_sp_f, mode="nt", dims=(S, SSM_W, D), tiles=(TW, SSM_W, D),
                  out_dtypes=[F32], name="ssm_proj_dgrad")
    gw_ssm_proj = _to_col_blocks(_matmul(z2, dy_ssm, mode="tn", dims=(SSM_W, D, S), tiles=(SSM_W, 1024, TS),
                                         out_dtypes=[BF16], name="ssm_proj_wgrad"))

    def f_glu_bwd(rv, vv, i, nt):
        dz2v, zv, zgv = rv
        sg = _sigmoid(zgv)
        dzg = dz2v * zv.astype(F32) * sg * (1.0 - sg)
        return [dzg, dz2v * sg], [_colsum(dzg)]

    dzg, dz_a, d_b_glu = _rowwise(f_glu_bwd, [_row(dz2), _row(z), _row(zg)], [],
                                  [(SSM_W, BF16), (SSM_W, F32)], [(1, SSM_W)],
                                  n_rows=S, tr=TM, ch=32, name="glu_bwd")
    dz_b = _matmul(dzg, w_glu_f, mode="nt", dims=(S, SSM_W, SSM_W), tiles=(TM, SSM_W, SSM_W),
                   out_dtypes=[F32], name="glu_dgrad")
    gw_glu = _matmul(z, dzg, mode="tn", dims=(SSM_W, SSM_W, S), tiles=(SSM_W, SSM_W, TS),
                     out_dtypes=[BF16], name="glu_wgrad").reshape(N_DEV, SSM_W // N_DEV, SSM_W)
    (g_flight["w_ssm_proj"], g_flight["w_glu"]), tok = _exchange_start(
        [gw_ssm_proj, gw_glu], "scatter", "grads_start_ssm")

    def f_ssm_out_bwd(rv, vv, i, nt):
        dza, dzb, yv, uv = rv
        dy = (dza + dzb) * _gelu_grad(yv)
        return [dy, dy * vv[0]], [_colsum(dy * uv)]

    dy_s, du_a, d_ssm_d = _rowwise(
        f_ssm_out_bwd, [_row(dz_a), _row(dz_b), _row(y_ssm_pre), _row(proj, u_blk, SSM_W)], [ssm_d],
        [(SSM_W, BF16), (SSM_W, F32)], [(1, SSM_W)], n_rows=S, tr=TM, ch=32, name="ssm_out_bwd", dep=tok)
    hw = SSM_W // 2
    u_half = (ATTN_W + 2 * KV_W) // hw
    dccat = _matmul(dy_s, xs, mode="tn", dims=(hw, 2 * SSM_H, S), tiles=(hw, 1024, TS),
                    out_dtypes=[F32], name="ssm_c_wgrad", a_index=lambda i, j, k: (k, j % 2))
    hs, dacc, du_b = _scan_bwd(dy_s, xprev, bcat, ccat, abar, S)
    dbcat = _matmul(proj, hs, mode="tn", dims=(hw, 2 * SSM_H, S), tiles=(hw, 1024, TS),
                    out_dtypes=[F32], name="ssm_b_wgrad", a_index=lambda i, j, k: (k, u_half + j % 2))
    grp = np.arange(SSM_H) // SSM_N
    gind = jnp.asarray((grp[:, None] == np.arange(128)[None, :]).astype(np.float32), BF16)
    d_lam_re, d_lam_im, d_ls, d_btr, d_bti, d_ctr, d_cti = _ssm_param_bwd(
        lam_re, lam_im, ls_x, btr, bti, dacc, dbcat, dccat, gind)

    dattn = _matmul(dy_attn, w_ap_f, mode="nt", dims=(S, ATTN_W, D), tiles=(TW, ATTN_W, D),
                    out_dtypes=[BF16], name="attn_proj_dgrad")
    gw_attn_proj = _to_col_blocks(_matmul(attn, dy_attn, mode="tn", dims=(ATTN_W, D, S), tiles=(ATTN_W, 1024, TS),
                                          out_dtypes=[BF16], name="attn_proj_wgrad"))
    (g_flight["w_attn_proj"],), tok = _exchange_start(
        [gw_attn_proj], "scatter", "grads_start_attn")
    dq, dkc, dkp, dvc, dvp, dbias, dsink = _attention_bwd(proj, attn, dattn, bias2, sinkcol, S)
    d_bias_b, d_sinks = _bucket_reduce(dbias.reshape(N_Q_HEADS, BLK * 2 * BLK),
                                       dsink.reshape(N_Q_HEADS, BLK), onehot_t)

    def f_dproj(rv, vv, i, nt):
        dqv, kc, kp, vc, vp, dua, dub, gav, gsv = rv
        keep = (i < nt - 1).astype(F32)
        dp = jnp.concatenate([dqv.astype(F32), kc + keep * kp, vc + keep * vp, dua + dub,
                              gav.astype(F32), gsv.astype(F32)], axis=-1)
        return [dp], [_colsum(dp)]

    dproj, d_b_in = _rowwise(
        f_dproj, [_row(dq), _row(dkc), _row(dkp, shift=1), _row(dvc), _row(dvp, shift=1),
                  _row(du_a), _row(du_b), _row(dga), _row(dgs)], [],
        [(IN_W, BF16)], [(1, IN_W)], n_rows=S, tr=BLK, ch=16, name="dproj", dep=tok)
    gw_in = _matmul(h, dproj, mode="tn", dims=(D, IN_W, S), tiles=(1024, 768, TS),
                    out_dtypes=[BF16], name="in_wgrad", out3=True)
    (g_flight["w_in"],), tok = _exchange_start([gw_in], "scatter", "grads_start_in")
    dh = _matmul(dproj, G["w_in"], mode="nt", dims=(S, D, IN_W), tiles=(TM, D, 768),
                 out_dtypes=[F32], name="in_dgrad", b3=True, dep=tok)

    def f_norm1_bwd(rv, vv, i, nt):
        xv, dhv, dx1v = rv
        g, sc = vv
        r = _rms(xv)
        xh = xv * r
        t = xh * g
        dt = dhv * (1.0 + sc)
        dxh = dt * g
        dxv = dx1v + r * (dxh - xh * jnp.mean(dxh * xh, axis=-1, keepdims=True))
        return [dxv], [_colsum(dhv), _colsum(dhv * t), _colsum(dt * xh)]

    grad_x, dsh1, dsc1, d_norm1_g = _rowwise(
        f_norm1_bwd, [_row(x2d), _row(dh), _row(dx1)], [norm1_g, sc1],
        [(D, F32)], [(1, D)] * 3, n_rows=S, tr=TR, ch=32, name="norm1_bwd")

    part = _pack({
        "b_ada": [dsh1, dsc1, dg1, dsh2, dsc2, dg2], "norm1_g": d_norm1_g, "b_in": d_b_in, "norm2_g": d_norm2_g,
        "final_g": d_final_g, "lambda_re": d_lam_re, "lambda_im": d_lam_im,
        "log_step": d_ls[0, :SSM_G], "attn_sinks": d_sinks[:, 0],
        "rel_bias": jnp.transpose(d_bias_b[:, :NUM_BUCKETS]), "b_glu": d_b_glu, "ssm_d": d_ssm_d,
        "loss": loss_cols, "ssm_b_re": d_btr, "ssm_b_im": d_bti, "ssm_c_re": d_ctr, "ssm_c_im": d_cti,
    })
    zone_small = lax.dynamic_update_slice(lax.empty((N_DEV, PACK_ROWS, PACK_W), F32), part[None], (me, 0, 0))
    (small_flight,), after = _exchange_start([zone_small], "gather", "small_grads_start")

    big_out = {}
    for n in ["w_ff2", "w_ff1", "w_out", "w_ssm_proj", "w_glu", "w_attn_proj", "w_in"]:
        own, recv = _exchange_wait([g_flight[n]], "scatter", after, "grads_wait_" + n[2:])[0]
        rows, cols = shard[n].shape
        parts = [(own, lambda m: m[0])] + [
            (recv, lambda m, j=j: jnp.where(j >= m[0], j + 1, j)) for j in range(N_DEV - 1)]
        big_out[n] = _adamw(parts, shard[n], Mo[n][0], Vo[n][0], tr=_adamw_rows(rows, cols), ch=16,
                            name="adamw_" + n, prefetch=me1)
        after = big_out[n][0]

    part_all = _exchange_wait([small_flight], "gather", after, "small_grads_wait")[0][0]
    wp, mp, vp = [_pack(_small_params_packed(p)) for p in (W, Mo, Vo)]
    sg, sdelta, sm, sv = _adamw([(part_all, d) for d in range(N_DEV)], wp, mp, vp,
                                tr=PACK_ROWS, ch=8, name="adamw_small")
    lo, _ = _PACK_OFF["loss"]
    loss = jnp.sum(sg[lo])

    o_ada, _ = _PACK_OFF["b_ada"]
    dmod_all = part_all[:, o_ada:o_ada + N_MOD, :].reshape(N_DEV, N_MOD * D)
    dmod_cols = lax.dynamic_slice(dmod_all, (0, me * n_ada), (N_DEV, n_ada))
    gw_ada = _matmul(cs, dmod_cols, mode="tn", dims=(D, n_ada, N_DEV), tiles=(D, 512, N_DEV),
                     out_dtypes=[F32], name="ada_wgrad")
    big_out["w_ada"] = _adamw([(gw_ada, 0)], w_ada[0], m_w_ada[0], v_w_ada[0],
                              tr=_adamw_rows(D, n_ada), ch=16, name="adamw_w_ada")

    def leaf(kind, n):
        if n in big_out:
            return big_out[n][kind][None]
        return _unpack_small((sg, sdelta, sm, sv)[kind], n)

    outs = [loss, grad_x.reshape(1, S, D)]
    for kind in range(4):
        outs.extend(leaf(kind, n) for n in WEIGHT_ORDER)
    return tuple(outs)
```

```python
import functools
import math

import numpy as np
import jax
import jax.numpy as jnp
from jax import lax
from jax.experimental import pallas as pl
from jax.experimental.pallas import tpu as pltpu

F32 = jnp.float32
BF16 = jnp.bfloat16
MESH = pl.DeviceIdType.MESH

N_DEV = 8
D = 2048
HEAD_DIM = 64
N_Q_HEADS = 16
N_KV_HEADS = 4
GROUP = N_Q_HEADS // N_KV_HEADS
ATTN_W = N_Q_HEADS * HEAD_DIM
KV_W = N_KV_HEADS * HEAD_DIM
BLK = 128
NUM_BUCKETS = 32
MAX_DISTANCE = 128
NEG_INF = -1e30
SSM_W = 512
SSM_P = 16
SSM_G = 32
SSM_N = 64
SSM_H = SSM_G * SSM_N
D_FF = 4 * D
IN_W = ATTN_W + 2 * KV_W + SSM_W + 2 * D
N_MOD = 6
EPS = 1e-6

ADAM_LR = 0.001
ADAM_B1 = 0.9
ADAM_B2 = 0.999
ADAM_EPS = 1e-08
ADAM_WD = 0.01
ADAM_STEP = 10

VMEM_LIMIT = 56 * 1024 * 1024
PACK_W = 2048


def _cparams(sem):
    return pltpu.CompilerParams(dimension_semantics=sem, vmem_limit_bytes=VMEM_LIMIT)


def _matmul(a, b, *, mode, dims, tiles, out_dtypes, name, a_off=0, b3=False,
            out3=False, bias=None, extras=(), epilogue=None, dep=None, a_index=None, b_index=None):
    M, N, K = dims
    tm, tn, tk = tiles
    assert M % tm == 0 and N % tn == 0 and K % tk == 0, (name, dims, tiles)
    gm, gn, gk = M // tm, N // tn, K // tk
    n_extra = len(extras)
    has_bias = bias is not None
    n_out = len(out_dtypes)

    if mode == "nn":
        a_spec = pl.BlockSpec((tm, tk), lambda i, j, k: (i, a_off + k))
        if b3:
            nb = (N // N_DEV) // tn
            assert nb * tn * N_DEV == N
            b_spec = pl.BlockSpec((None, tk, tn), lambda i, j, k: (j // nb, k, j % nb))
        else:
            b_spec = pl.BlockSpec((tk, tn), lambda i, j, k: (k, j))
        dn = (((1,), (0,)), ((), ()))
    elif mode == "nt":
        a_spec = pl.BlockSpec((tm, tk), lambda i, j, k: (i, a_off + k))
        if b3:
            nb = (K // N_DEV) // tk
            assert nb * tk * N_DEV == K
            b_spec = pl.BlockSpec((None, tn, tk), lambda i, j, k: (k // nb, j, k % nb))
        else:
            b_spec = pl.BlockSpec((tn, tk), lambda i, j, k: (j, k))
        dn = (((1,), (1,)), ((), ()))
    else:
        a_spec = pl.BlockSpec((tk, tm), lambda i, j, k: (k, a_off + i))
        b_spec = pl.BlockSpec((tk, tn), lambda i, j, k: (k, j))
        dn = (((0,), (0,)), ((), ()))
    if a_index is not None:
        a_spec = pl.BlockSpec(a_spec.block_shape, a_index)
    if b_index is not None:
        b_spec = pl.BlockSpec(b_spec.block_shape, b_index)

    if out3:
        nbo = (N // N_DEV) // tn
        assert nbo * tn * N_DEV == N
        o_spec = pl.BlockSpec((None, tm, tn), lambda i, j, k: (j // nbo, i, j % nbo))
        o_shape = (N_DEV, M, N // N_DEV)
    else:
        o_spec = pl.BlockSpec((tm, tn), lambda i, j, k: (i, j))
        o_shape = (M, N)

    in_specs = [a_spec, b_spec]
    args = [a, b]
    if has_bias:
        in_specs.append(pl.BlockSpec((1, tn), lambda i, j, k: (0, j)))
        args.append(bias)
    for e in extras:
        in_specs.append(pl.BlockSpec((tm, tn), lambda i, j, k: (i, j)))
        args.append(e)
    n_dep = 0 if dep is None else 1
    if n_dep:
        in_specs.append(pl.BlockSpec(memory_space=pl.ANY))
        args.append(dep)

    def body(*refs):
        a_ref, b_ref = refs[0], refs[1]
        pos = 2
        bias_ref = None
        if has_bias:
            bias_ref = refs[pos]
            pos += 1
        extra_refs = refs[pos:pos + n_extra]
        pos += n_extra + n_dep
        out_refs = refs[pos:pos + n_out]
        acc_ref = refs[pos + n_out] if gk > 1 else None

        part = lax.dot_general(a_ref[...].astype(BF16), b_ref[...].astype(BF16), dn,
                               preferred_element_type=F32)

        def finish(acc):
            if has_bias:
                acc = acc + bias_ref[...]
            if epilogue is None:
                vals = (acc,)
            else:
                vals = epilogue(acc, *[e[...] for e in extra_refs])
            for o_ref, val in zip(out_refs, vals):
                o_ref[...] = val.astype(o_ref.dtype)

        if gk == 1:
            finish(part)
        else:
            k = pl.program_id(2)

            @pl.when(k == 0)
            def _():
                acc_ref[...] = part

            @pl.when(k > 0)
            def _():
                acc_ref[...] += part

            @pl.when(k == gk - 1)
            def _():
                finish(acc_ref[...])

    outs = pl.pallas_call(
        body,
        grid=(gm, gn, gk),
        in_specs=in_specs,
        out_specs=[o_spec] * n_out,
        out_shape=[jax.ShapeDtypeStruct(o_shape, dt) for dt in out_dtypes],
        scratch_shapes=([pltpu.VMEM((tm, tn), F32)] if gk > 1 else []),
        compiler_params=_cparams(("parallel", "parallel", "arbitrary")),
        name=name,
    )(*args)
    return outs[0] if n_out == 1 else outs


def _rowwise(fn, rows, vecs, row_outs, sum_outs, *, n_rows, tr, ch, name, dep=None, prefetch=None):
    assert n_rows % tr == 0 and tr % ch == 0
    nt = n_rows // tr
    nr, nv, nro, nso = len(rows), len(vecs), len(row_outs), len(sum_outs)
    in_specs, args = [], []
    n_pf = 0 if prefetch is None else 1
    for (arr, lead, cblk, w, shift) in rows:
        if shift:
            ridx = lambda i, shift=shift: jnp.minimum(i + shift, nt - 1)
        else:
            ridx = lambda i: i
        if arr.ndim == 3:
            def imap(i, *pf, lead=lead, cblk=cblk, ridx=ridx):
                return (lead(pf[0]) if callable(lead) else lead, ridx(i), cblk)
            in_specs.append(pl.BlockSpec((None, tr, w), imap))
        else:
            in_specs.append(pl.BlockSpec(
                (tr, w), lambda i, *pf, cblk=cblk, ridx=ridx: (ridx(i), cblk)))
        args.append(arr)
    for v in vecs:
        in_specs.append(pl.BlockSpec(v.shape, lambda i, *pf, nd=v.ndim: (0,) * nd))
        args.append(v)
    n_dep = 0 if dep is None else 1
    if n_dep:
        in_specs.append(pl.BlockSpec(memory_space=pl.ANY))
        args.append(dep)
    out_specs = [pl.BlockSpec((tr, w), lambda i, *pf: (i, 0)) for (w, _) in row_outs]
    out_shape = [jax.ShapeDtypeStruct((n_rows, w), dt) for (w, dt) in row_outs]
    for (r, w) in sum_outs:
        out_specs.append(pl.BlockSpec((r, w), lambda i, *pf: (0, 0)))
        out_shape.append(jax.ShapeDtypeStruct((r, w), F32))

    def body(*refs):
        refs = refs[n_pf:]
        i = pl.program_id(0)
        r_in = refs[:nr]
        v_in = refs[nr:nr + nv]
        r_out = refs[nr + nv + n_dep:nr + nv + n_dep + nro]
        s_out = refs[nr + nv + n_dep + nro:]
        if nso:
            @pl.when(i == 0)
            def _():
                for s in s_out:
                    s[...] = jnp.zeros(s.shape, F32)
        vvals = [v[...] for v in v_in]

        def chunk(ci, carry):
            r0 = pl.multiple_of(ci * ch, ch)
            rv = [r[pl.ds(r0, ch), :].astype(F32) for r in r_in]
            ro, so = fn(rv, vvals, i, nt)
            for ref, val in zip(r_out, ro):
                ref[pl.ds(r0, ch), :] = val.astype(ref.dtype)
            for ref, val in zip(s_out, so):
                ref[...] += val
            return carry

        lax.fori_loop(0, tr // ch, chunk, 0)

    outs = pl.pallas_call(
        body,
        grid_spec=pltpu.PrefetchScalarGridSpec(
            num_scalar_prefetch=n_pf, grid=(nt,), in_specs=in_specs, out_specs=out_specs),
        out_shape=out_shape,
        compiler_params=_cparams(("arbitrary",)),
        name=name,
    )(*([prefetch] if n_pf else []), *args)
    return outs


def _row(arr, cblk=0, w=None, lead=0, shift=0):
    return (arr, lead, cblk, arr.shape[-1] if w is None else w, shift)


def _colsum(v):
    return jnp.sum(v, axis=0, keepdims=True)


def _rms(x):
    return lax.rsqrt(jnp.mean(x * x, axis=-1, keepdims=True) + EPS)


def _sigmoid(x):
    return 1.0 / (1.0 + jnp.exp(-x))


_GELU_C = math.sqrt(2.0 / math.pi)


def _gelu(x):
    return 0.5 * x * (1.0 + jnp.tanh(_GELU_C * (x + 0.044715 * (x * x * x))))


def _gelu_grad(x):
    t = jnp.tanh(_GELU_C * (x + 0.044715 * (x * x * x)))
    return 0.5 * (1.0 + t) + 0.5 * x * (1.0 - t * t) * (_GELU_C * (1.0 + 3.0 * 0.044715 * (x * x)))


def _my_pos():
    return lax.axis_index("x"), lax.axis_index("y"), lax.axis_index("c")


def _flip(pos, k):
    x, y, c = pos
    return (1 - x if k & 4 else x, 1 - y if k & 2 else y, 1 - c if k & 1 else c)


def _dev_id(pos):
    return 4 * pos[0] + 2 * pos[1] + pos[2]


def _small_allgather(x, name):
    r, c = x.shape

    def body(x_ref, out_ref, send_sems, recv_sems):
        me = _my_pos()
        out_ref[_dev_id(me)] = x_ref[...]
        copies = []
        for k in range(1, N_DEV):
            cp = pltpu.make_async_remote_copy(
                src_ref=x_ref, dst_ref=out_ref.at[_dev_id(me)],
                send_sem=send_sems.at[k - 1], recv_sem=recv_sems.at[k - 1],
                device_id=_flip(me, k), device_id_type=MESH)
            cp.start()
            copies.append(cp)
        for k in range(1, N_DEV):
            peer = _flip(me, k)
            pltpu.make_async_remote_copy(
                src_ref=x_ref, dst_ref=out_ref.at[_dev_id(peer)],
                send_sem=send_sems.at[k - 1], recv_sem=recv_sems.at[k - 1],
                device_id=peer, device_id_type=MESH).wait_recv()
        for cp in copies:
            cp.wait_send()

    return pl.pallas_call(
        body,
        out_shape=jax.ShapeDtypeStruct((N_DEV, r, c), x.dtype),
        in_specs=[pl.BlockSpec(memory_space=pltpu.VMEM)],
        out_specs=pl.BlockSpec(memory_space=pltpu.VMEM),
        scratch_shapes=[pltpu.SemaphoreType.DMA((N_DEV - 1,)),
                        pltpu.SemaphoreType.DMA((N_DEV - 1,))],
        compiler_params=pltpu.CompilerParams(vmem_limit_bytes=VMEM_LIMIT),
        name=name,
    )(x)


_HBM = pl.BlockSpec(memory_space=pltpu.HBM)
_SEM = pl.BlockSpec(memory_space=pltpu.SEMAPHORE)
_EFFECT = pltpu.SideEffectType.DATAFLOW_SIDE_EFFECTING


def _relay_copy(zone, send_sems, recv_sems, k, block, to):
    slot = zone.at[_dev_id(block)]
    return pltpu.make_async_remote_copy(
        src_ref=slot, dst_ref=slot, send_sem=send_sems.at[k], recv_sem=recv_sems.at[k],
        device_id=to, device_id_type=MESH)


def _relay_peers():
    x, y, c = _my_pos()
    return (x, y, c), (x, y, 1 - c), [(1 - x, y), (x, 1 - y), (1 - x, 1 - y)]


def _relay_gather_start(zone, name):
    def body(zone_ref, send, recv, _, token):
        me, sib, chips = _relay_peers()
        _relay_copy(zone_ref, send, recv, 0, me, sib).start()
        for j, chip in enumerate(chips):
            _relay_copy(zone_ref, send, recv, 1 + j, me, (*chip, me[2])).start()
        token[...] = jnp.zeros(token.shape, token.dtype)

    sem = pltpu.SemaphoreType.DMA((4,))
    return pl.pallas_call(
        body,
        name=name,
        out_shape=[sem, sem, pltpu.HBM(zone.shape, zone.dtype), jax.ShapeDtypeStruct((8, 128), F32)],
        in_specs=[_HBM],
        out_specs=[_SEM, _SEM, _HBM, pl.BlockSpec(memory_space=pltpu.VMEM)],
        input_output_aliases={0: 2},
        compiler_params=pltpu.CompilerParams(has_side_effects=_EFFECT),
    )(pltpu.with_memory_space_constraint(zone, pltpu.HBM))


def _relay_gather_arrive(send_sems, recv_sems, zone, after, name):
    def body(zone_ref, send, recv, after_ref, _):
        me, sib, chips = _relay_peers()
        _relay_copy(zone_ref, send, recv, 0, sib, me).wait_recv()
        _relay_copy(zone_ref, send, recv, 0, me, sib).wait_send()
        for j, chip in enumerate(chips):
            _relay_copy(zone_ref, send, recv, 1 + j, (*chip, me[2]), me).wait_recv()
            _relay_copy(zone_ref, send, recv, 1 + j, me, (*chip, me[2])).wait_send()

    return pl.pallas_call(
        body,
        name=name,
        out_shape=pltpu.HBM(zone.shape, zone.dtype),
        in_specs=[_HBM, _SEM, _SEM, pl.BlockSpec(memory_space=pl.ANY)],
        out_specs=_HBM,
        input_output_aliases={0: 0},
        compiler_params=pltpu.CompilerParams(has_side_effects=_EFFECT),
    )(zone, send_sems, recv_sems, after)


def _relay_gather_finish(zone, name):
    def body(_, out, send, recv):
        me, sib, chips = _relay_peers()
        sent = []
        for j, chip in enumerate(chips):
            cp = _relay_copy(out, send, recv, j, (*chip, me[2]), sib)
            cp.start()
            sent.append(cp)
        for j, chip in enumerate(chips):
            _relay_copy(out, send, recv, j, (*chip, sib[2]), me).wait_recv()
        for cp in sent:
            cp.wait_send()

    hbm = pl.BlockSpec(memory_space=pl.ANY)
    return pl.pallas_call(
        body,
        name=name,
        out_shape=jax.ShapeDtypeStruct(zone.shape, zone.dtype),
        in_specs=[hbm],
        out_specs=hbm,
        input_output_aliases={0: 0},
        scratch_shapes=[pltpu.SemaphoreType.DMA((3,)), pltpu.SemaphoreType.DMA((3,))],
    )(zone)


def _exchange_copy(kind, bufs, send_sems, recv_sems, me, k, arriving):
    peer = _flip(me, k)
    my_id, peer_id = _dev_id(me), _dev_id(peer)
    if kind == "gather":
        slot = bufs[0].at[peer_id if arriving else my_id]
        src, dst = slot, slot
    else:
        src = bufs[0].at[my_id if arriving else peer_id]
        dst = bufs[1].at[peer_id if arriving else my_id]
    return pltpu.make_async_remote_copy(
        src_ref=src, dst_ref=dst, send_sem=send_sems.at[k - 1], recv_sem=recv_sems.at[k - 1],
        device_id=peer, device_id_type=MESH)


def _exchange_start(arrays, kind, name, after=None):
    n = len(arrays)
    n_after = 0 if after is None else 1
    if kind == "gather":
        bufs = [[a] for a in arrays]
    else:
        bufs = [[a, lax.empty(a.shape, a.dtype)] for a in arrays]
    nb = len(bufs[0])
    flat = [b for group in bufs for b in group]

    def body(*refs):
        outs_at = nb * n + n_after
        send = refs[outs_at:outs_at + n]
        recv = refs[outs_at + n:outs_at + 2 * n]
        token = refs[outs_at + 2 * n + nb * n]
        me = _my_pos()
        for a in range(n):
            for k in range(1, N_DEV):
                _exchange_copy(kind, refs[nb * a:nb * (a + 1)], send[a], recv[a], me, k, False).start()
        token[...] = jnp.zeros(token.shape, token.dtype)

    sem = pltpu.SemaphoreType.DMA((N_DEV - 1,))
    outs = pl.pallas_call(
        body,
        name=name,
        out_shape=([sem] * (2 * n) + [pltpu.HBM(b.shape, b.dtype) for b in flat]
                   + [jax.ShapeDtypeStruct((8, 128), F32)]),
        in_specs=[_HBM] * (nb * n) + [pl.BlockSpec(memory_space=pl.ANY)] * n_after,
        out_specs=[_SEM] * (2 * n) + [_HBM] * (nb * n) + [pl.BlockSpec(memory_space=pltpu.VMEM)],
        input_output_aliases={i: 2 * n + i for i in range(nb * n)},
        compiler_params=pltpu.CompilerParams(has_side_effects=_EFFECT),
    )(*[pltpu.with_memory_space_constraint(b, pltpu.HBM) for b in flat],
      *([after] if n_after else []))
    flights = [(outs[a], outs[n + a], list(outs[2 * n + nb * a:2 * n + nb * (a + 1)]))
               for a in range(n)]
    return flights, outs[2 * n + nb * n]


def _exchange_wait(flights, kind, after, name):
    n = len(flights)
    nb = len(flights[0][2])
    flat = [b for f in flights for b in f[2]]

    def body(*refs):
        send = refs[nb * n:nb * n + n]
        recv = refs[nb * n + n:nb * n + 2 * n]
        me = _my_pos()
        for a in range(n):
            for k in range(1, N_DEV):
                bufs = refs[nb * a:nb * (a + 1)]
                _exchange_copy(kind, bufs, send[a], recv[a], me, k, False).wait_send()
                _exchange_copy(kind, bufs, send[a], recv[a], me, k, True).wait_recv()

    outs = pl.pallas_call(
        body,
        name=name,
        out_shape=[pltpu.HBM(b.shape, b.dtype) for b in flat],
        in_specs=[_HBM] * (nb * n) + [_SEM] * (2 * n) + [pl.BlockSpec(memory_space=pl.ANY)],
        out_specs=[_HBM] * (nb * n),
        input_output_aliases={i: i for i in range(nb * n)},
        compiler_params=pltpu.CompilerParams(has_side_effects=_EFFECT),
    )(*flat, *[f[0] for f in flights], *[f[1] for f in flights], after)
    return [list(outs[nb * a:nb * (a + 1)]) for a in range(n)]


def _t5_buckets_block():
    qi = np.arange(BLK)[:, None]
    ki = np.arange(2 * BLK)[None, :]
    n = np.maximum(qi + BLK - ki, 0)
    max_exact = NUM_BUCKETS // 2
    large = max_exact + (np.log(np.maximum(n, 1) / max_exact)
                         / np.log(MAX_DISTANCE / max_exact)
                         * (NUM_BUCKETS - max_exact)).astype(np.int32)
    large = np.minimum(large, NUM_BUCKETS - 1)
    return np.where(n < max_exact, n, large).astype(np.int32)


def _band_mask():
    qi = np.arange(BLK)[:, None]
    ki = np.arange(2 * BLK)[None, :]
    dist = qi + BLK - ki
    return (dist >= 0) & (dist < BLK)


def _attn_scores(q_ref, kp_ref, kc_ref, hkv):
    c0 = hkv * HEAD_DIM
    kk = jnp.concatenate([kp_ref[:, c0:c0 + HEAD_DIM], kc_ref[:, c0:c0 + HEAD_DIM]],
                         axis=0).astype(BF16)
    qg = jnp.concatenate(
        [q_ref[:, (hkv * GROUP + g) * HEAD_DIM:(hkv * GROUP + g + 1) * HEAD_DIM]
         for g in range(GROUP)], axis=0).astype(BF16)
    s = lax.dot_general(qg, kk, (((1,), (1,)), ((), ())), preferred_element_type=F32)
    return qg, kk, s


def _attn_softmax(s, bias_ref, sink_ref, hkv):
    r0, r1 = hkv * GROUP * BLK, (hkv + 1) * GROUP * BLK
    s = s * (HEAD_DIM ** -0.5) + bias_ref[r0:r1, :]
    sink = sink_ref[r0:r1, :]
    m = jnp.maximum(jnp.max(s, axis=-1, keepdims=True), sink)
    p = jnp.exp(s - m)
    e_sink = jnp.exp(sink - m)
    inv = 1.0 / (jnp.sum(p, axis=-1, keepdims=True) + e_sink)
    return p * inv, e_sink * inv


def _kv_rows(p_ref, c_ref, hkv):
    c0 = hkv * HEAD_DIM
    return jnp.concatenate([p_ref[:, c0:c0 + HEAD_DIM], c_ref[:, c0:c0 + HEAD_DIM]],
                           axis=0).astype(BF16)


def _attn_in_specs(bias2):
    prev = lambda n: jnp.maximum(n - 1, 0)
    return [
        pl.BlockSpec((BLK, ATTN_W), lambda n: (n, 0)),
        pl.BlockSpec((BLK, KV_W), lambda n: (prev(n), ATTN_W // KV_W)),
        pl.BlockSpec((BLK, KV_W), lambda n: (n, ATTN_W // KV_W)),
        pl.BlockSpec((BLK, KV_W), lambda n: (prev(n), ATTN_W // KV_W + 1)),
        pl.BlockSpec((BLK, KV_W), lambda n: (n, ATTN_W // KV_W + 1)),
        pl.BlockSpec((None,) + bias2.shape[1:], lambda n: (jnp.minimum(n, 1), 0, 0)),
    ]


def _attention_fwd(proj, bias2, sinkcol, n_rows):
    nb = n_rows // BLK

    def body(q_ref, kp_ref, kc_ref, vp_ref, vc_ref, bias_ref, sink_ref, o_ref):
        heads = range(N_KV_HEADS)
        scores = [_attn_scores(q_ref, kp_ref, kc_ref, hkv)[2] for hkv in heads]
        probs = [_attn_softmax(scores[hkv], bias_ref, sink_ref, hkv)[0] for hkv in heads]
        outs = [jnp.dot(probs[hkv].astype(BF16), _kv_rows(vp_ref, vc_ref, hkv),
                        preferred_element_type=F32) for hkv in heads]
        for hkv in heads:
            for g in range(GROUP):
                h = hkv * GROUP + g
                o_ref[:, h * HEAD_DIM:(h + 1) * HEAD_DIM] = (
                    outs[hkv][g * BLK:(g + 1) * BLK, :].astype(o_ref.dtype))

    return pl.pallas_call(
        body,
        grid=(nb,),
        in_specs=_attn_in_specs(bias2) + [pl.BlockSpec(sinkcol.shape, lambda n: (0, 0))],
        out_specs=pl.BlockSpec((BLK, ATTN_W), lambda n: (n, 0)),
        out_shape=jax.ShapeDtypeStruct((n_rows, ATTN_W), BF16),
        compiler_params=_cparams(("parallel",)),
        name="attn_fwd",
    )(proj, proj, proj, proj, proj, bias2, sinkcol)


def _attention_bwd(proj, attn, dattn, bias2, sinkcol, n_rows):
    nb = n_rows // BLK
    scale = HEAD_DIM ** -0.5
    dn_t = (((0,), (0,)), ((), ()))

    def body(q_ref, kp_ref, kc_ref, vp_ref, vc_ref, bias_ref, o_ref, do_ref, sink_ref,
             dq_ref, dkc_ref, dkp_ref, dvc_ref, dvp_ref, dbias_ref, dsink_ref):
        @pl.when(pl.program_id(0) == 0)
        def _():
            dbias_ref[...] = jnp.zeros(dbias_ref.shape, F32)
            dsink_ref[...] = jnp.zeros(dsink_ref.shape, F32)

        heads = range(N_KV_HEADS)
        qk = [_attn_scores(q_ref, kp_ref, kc_ref, hkv) for hkv in heads]
        dog, dps, deltas = [], [], []
        for hkv in heads:
            hs = [hkv * GROUP + g for g in range(GROUP)]
            d_o = jnp.concatenate([do_ref[:, h * HEAD_DIM:(h + 1) * HEAD_DIM] for h in hs], axis=0)
            o = jnp.concatenate([o_ref[:, h * HEAD_DIM:(h + 1) * HEAD_DIM] for h in hs], axis=0)
            deltas.append(jnp.sum(d_o.astype(F32) * o.astype(F32), axis=-1, keepdims=True))
            dog.append(d_o.astype(BF16))
            dps.append(lax.dot_general(dog[hkv], _kv_rows(vp_ref, vc_ref, hkv),
                                       (((1,), (1,)), ((), ())), preferred_element_type=F32))
        p16, ds16 = [], []
        for hkv in heads:
            r0, r1 = hkv * GROUP * BLK, (hkv + 1) * GROUP * BLK
            p, p_sink = _attn_softmax(qk[hkv][2], bias_ref, sink_ref, hkv)
            ds = p * (dps[hkv] - deltas[hkv])
            dbias_ref[r0:r1, :] += ds
            dsink_ref[r0:r1, :] += -(p_sink * deltas[hkv])
            p16.append(p.astype(BF16))
            ds16.append(ds.astype(BF16))
        for hkv in heads:
            c0 = hkv * HEAD_DIM
            qg, kk, _ = qk[hkv]
            dqg = jnp.dot(ds16[hkv], kk, preferred_element_type=F32) * scale
            dkk = lax.dot_general(ds16[hkv], qg, dn_t, preferred_element_type=F32) * scale
            dvv = lax.dot_general(p16[hkv], dog[hkv], dn_t, preferred_element_type=F32)
            for g in range(GROUP):
                h = hkv * GROUP + g
                dq_ref[:, h * HEAD_DIM:(h + 1) * HEAD_DIM] = (
                    dqg[g * BLK:(g + 1) * BLK, :].astype(dq_ref.dtype))
            dkp_ref[:, c0:c0 + HEAD_DIM] = dkk[:BLK].astype(dkp_ref.dtype)
            dkc_ref[:, c0:c0 + HEAD_DIM] = dkk[BLK:].astype(dkc_ref.dtype)
            dvp_ref[:, c0:c0 + HEAD_DIM] = dvv[:BLK].astype(dvp_ref.dtype)
            dvc_ref[:, c0:c0 + HEAD_DIM] = dvv[BLK:].astype(dvc_ref.dtype)

    kv_out = pl.BlockSpec((BLK, KV_W), lambda n: (n, 0))
    kv_shape = jax.ShapeDtypeStruct((n_rows, KV_W), F32)
    acc_shape = bias2.shape[1:]
    return pl.pallas_call(
        body,
        grid=(nb,),
        in_specs=_attn_in_specs(bias2) + [
            pl.BlockSpec((BLK, ATTN_W), lambda n: (n, 0)),
            pl.BlockSpec((BLK, ATTN_W), lambda n: (n, 0)),
            pl.BlockSpec(sinkcol.shape, lambda n: (0, 0)),
        ],
        out_specs=[
            pl.BlockSpec((BLK, ATTN_W), lambda n: (n, 0)),
            kv_out, kv_out, kv_out, kv_out,
            pl.BlockSpec(acc_shape, lambda n: (0, 0)),
            pl.BlockSpec(sinkcol.shape, lambda n: (0, 0)),
        ],
        out_shape=[
            jax.ShapeDtypeStruct((n_rows, ATTN_W), BF16),
            kv_shape, kv_shape, kv_shape, kv_shape,
            jax.ShapeDtypeStruct(acc_shape, F32),
            jax.ShapeDtypeStruct(sinkcol.shape, F32),
        ],
        compiler_params=_cparams(("arbitrary",)),
        name="attn_bwd",
    )(proj, proj, proj, proj, proj, bias2, attn, dattn, sinkcol)


def _bias_tables(rel_bias_t, onehot_t, band_first, band_rest):
    def body(rb_ref, oh_ref, mf_ref, mr_ref, out_ref):
        acc = jnp.zeros((N_Q_HEADS, BLK * 2 * BLK), F32)
        for part in _split3(rb_ref[...]):
            acc = acc + jnp.dot(part, oh_ref[...], preferred_element_type=F32)
        out_ref[0] = jnp.where(mf_ref[...] > 0.0, acc, NEG_INF)
        out_ref[1] = jnp.where(mr_ref[...] > 0.0, acc, NEG_INF)

    return pl.pallas_call(
        body,
        out_shape=jax.ShapeDtypeStruct((2, N_Q_HEADS, BLK * 2 * BLK), F32),
        compiler_params=pltpu.CompilerParams(vmem_limit_bytes=VMEM_LIMIT),
        name="bias_tables",
    )(rel_bias_t, onehot_t, band_first, band_rest)


def _split3(a):
    hi = a.astype(BF16)
    r1 = a - hi.astype(F32)
    mid = r1.astype(BF16)
    lo = (r1 - mid.astype(F32)).astype(BF16)
    return hi, mid, lo


def _bucket_reduce(dbias, dsink, onehot_t):
    def body(db_ref, ds_ref, oh_ref, ob_ref, os_ref):
        acc = jnp.zeros((N_Q_HEADS, 128), F32)
        for part in _split3(db_ref[...]):
            acc = acc + lax.dot_general(part, oh_ref[...], (((1,), (1,)), ((), ())),
                                        preferred_element_type=F32)
        ob_ref[...] = acc
        os_ref[...] = jnp.broadcast_to(jnp.sum(ds_ref[...], axis=-1, keepdims=True),
                                       os_ref.shape)

    return pl.pallas_call(
        body,
        out_shape=[jax.ShapeDtypeStruct((N_Q_HEADS, 128), F32),
                   jax.ShapeDtypeStruct((N_Q_HEADS, 128), F32)],
        compiler_params=pltpu.CompilerParams(vmem_limit_bytes=VMEM_LIMIT),
        name="bias_bucket_reduce",
    )(dbias, dsink, onehot_t)


def _disc(lr, li, ls, btr, bti):
    lam_re = jnp.minimum(lr, -1e-4)
    delta = jnp.exp(ls)
    mag = jnp.exp(lam_re * delta)
    ang = li * delta
    ar, ai = mag * jnp.cos(ang), mag * jnp.sin(ang)
    nr, ni = ar - 1.0, ai
    den = lam_re * lam_re + li * li
    fr = (nr * lam_re + ni * li) / den
    fi = (ni * lam_re - nr * li) / den
    bbr = fr * btr - fi * bti
    bbi = fr * bti + fi * btr
    return ar, ai, bbr, bbi


def _block_mask():
    row = lax.broadcasted_iota(jnp.int32, (SSM_W, SSM_H), 0)
    col = lax.broadcasted_iota(jnp.int32, (SSM_W, SSM_H), 1)
    return (row // SSM_P) == (col // SSM_N)


def _ssm_setup(lr, li, ls, btr, bti, ctr, cti):
    def body(lr_ref, li_ref, ls_ref, btr_ref, bti_ref, ctr_ref, cti_ref, a_ref, b_ref, c_ref):
        ar, ai, bbr, bbi = _disc(lr_ref[...], li_ref[...], ls_ref[...], btr_ref[...], bti_ref[...])
        a_ref[:, :SSM_H] = ar
        a_ref[:, SSM_H:] = ai
        mask = _block_mask()
        blk = lambda t: jnp.where(mask, jnp.tile(t, (SSM_G, 1)), 0.0)
        b_ref[:, :SSM_H] = blk(bbr).astype(BF16)
        b_ref[:, SSM_H:] = blk(bbi).astype(BF16)
        c_ref[:, :SSM_H] = blk(ctr_ref[...]).astype(BF16)
        c_ref[:, SSM_H:] = blk(-cti_ref[...]).astype(BF16)

    return pl.pallas_call(
        body,
        out_shape=[jax.ShapeDtypeStruct((1, 2 * SSM_H), F32),
                   jax.ShapeDtypeStruct((SSM_W, 2 * SSM_H), BF16),
                   jax.ShapeDtypeStruct((SSM_W, 2 * SSM_H), BF16)],
        compiler_params=pltpu.CompilerParams(vmem_limit_bytes=VMEM_LIMIT),
        name="ssm_setup",
    )(lr, li, ls, btr, bti, ctr, cti)


def _ssm_param_bwd(lr, li, ls, btr, bti, dacc, dbcat, dccat, gind):
    def body(lr_ref, li_ref, ls_ref, btr_ref, bti_ref, dacc_ref, db_ref, dc_ref, g_ref,
             dlr_ref, dli_ref, dls_ref, dbtr_ref, dbti_ref, dctr_ref, dcti_ref):
        dar = jnp.sum(dacc_ref[:, :SSM_H], axis=0, keepdims=True)
        dai = jnp.sum(dacc_ref[:, SSM_H:], axis=0, keepdims=True)
        col = lax.broadcasted_iota(jnp.int32, (SSM_P, 2 * SSM_H), 1)
        grp = (col % SSM_H) // SSM_N
        db = jnp.zeros((SSM_P, 2 * SSM_H), F32)
        dc = jnp.zeros((SSM_P, 2 * SSM_H), F32)
        half = SSM_G // 2
        for g in range(SSM_G):
            sel = grp == g
            r0 = (g % half) * SSM_P
            db = db + jnp.where(sel, db_ref[r0:r0 + SSM_P, :], 0.0)
            dc = dc + jnp.where(sel, dc_ref[r0:r0 + SSM_P, :], 0.0)
        dctr_ref[...] = dc[:, :SSM_H]
        dcti_ref[...] = -dc[:, SSM_H:]
        prim = (lr_ref[...], li_ref[...], ls_ref[...], btr_ref[...], bti_ref[...])
        _, vjp = jax.vjp(_disc, *prim)
        dlr, dli, dls, dbtr, dbti = vjp((dar, dai, db[:, :SSM_H], db[:, SSM_H:]))
        dlr_ref[...] = dlr
        dli_ref[...] = dli
        dbtr_ref[...] = dbtr
        dbti_ref[...] = dbti
        acc = jnp.zeros((8, 128), F32)
        for part in _split3(jnp.broadcast_to(dls, (8, SSM_H))):
            acc = acc + jnp.dot(part, g_ref[...], preferred_element_type=F32)
        dls_ref[...] = acc

    vec = jax.ShapeDtypeStruct((1, SSM_H), F32)
    mat = jax.ShapeDtypeStruct((SSM_P, SSM_H), F32)
    return pl.pallas_call(
        body,
        out_shape=[vec, vec, jax.ShapeDtypeStruct((8, 128), F32), mat, mat, mat, mat],
        compiler_params=pltpu.CompilerParams(vmem_limit_bytes=VMEM_LIMIT),
        name="ssm_param_bwd",
    )(lr, li, ls, btr, bti, dacc, dbcat, dccat, gind)


SCAN_TR = 256


def _cmul_add(vr, vi, pr, pi, sr, si):
    return vr + pr * sr - pi * si, vi + pr * si + pi * sr


def _bcast_row(v, row, which):
    b = jnp.where(row == which, v, 0.0)
    b = b + pltpu.roll(b, 4, 0)
    b = b + pltpu.roll(b, 2, 0)
    return b + pltpu.roll(b, 1, 0)


def _scan_tables(a_ref, tab_ref, reverse):
    H = SSM_H
    ar = jnp.broadcast_to(a_ref[:, :H], (8, H))
    ai = jnp.broadcast_to(a_ref[:, H:], (8, H))
    if reverse:
        ai = -ai
    row = lax.broadcasted_iota(jnp.int32, (8, H), 0)
    pw = [(ar, ai)]
    for _ in range(7):
        cr, ci = pw[-1]
        pw.append((cr * ar - ci * ai, cr * ai + ci * ar))
    pcr = jnp.zeros((8, H), F32)
    pci = jnp.zeros((8, H), F32)
    for e in range(8):
        sel = (row == (7 - e)) if reverse else (row == e)
        pcr = jnp.where(sel, pw[e][0], pcr)
        pci = jnp.where(sel, pw[e][1], pci)
    tab_ref[0, :, :H] = pcr
    tab_ref[0, :, H:] = pci
    for t, k in enumerate((1, 2, 4)):
        keep = (row < 8 - k) if reverse else (row >= k)
        tab_ref[1 + t, :, :H] = jnp.where(keep, pw[k - 1][0], 0.0)
        tab_ref[1 + t, :, H:] = jnp.where(keep, pw[k - 1][1], 0.0)


def _scan_group(vr, vi, cr, ci, tab_ref, reverse):
    H = SSM_H
    for t, k in enumerate((1, 2, 4)):
        sh = 8 - k if reverse else k
        vr, vi = _cmul_add(vr, vi, tab_ref[1 + t, :, :H], tab_ref[1 + t, :, H:],
                           pltpu.roll(vr, sh, 0), pltpu.roll(vi, sh, 0))
    return _cmul_add(vr, vi, tab_ref[0, :, :H], tab_ref[0, :, H:], cr, ci)


def _scan_fwd(proj, u_blk, bcat, ccat, abar, n_rows):
    H = SSM_H
    nt = n_rows // SCAN_TR

    def body(u_ref, b_ref, c_ref, a_ref, xs_ref, xp_ref, yc_ref, bu_ref, tab_ref, carry_ref):
        @pl.when(pl.program_id(0) == 0)
        def _():
            _scan_tables(a_ref, tab_ref, False)
            carry_ref[...] = jnp.zeros(carry_ref.shape, F32)

        bu_ref[...] = jnp.dot(u_ref[...].astype(BF16), b_ref[...], preferred_element_type=F32)
        row = lax.broadcasted_iota(jnp.int32, (8, H), 0)

        def group(j, carry):
            cr, ci = carry
            r0 = pl.multiple_of(j * 16, 16)
            xr, xi = [], []
            for half in range(2):
                rr = pl.multiple_of(r0 + 8 * half, 8)
                vr, vi = _scan_group(bu_ref[pl.ds(rr, 8), :H], bu_ref[pl.ds(rr, 8), H:],
                                     cr, ci, tab_ref, False)
                xp_ref[pl.ds(rr, 8), :H] = jnp.where(row == 0, cr, pltpu.roll(vr, 1, 0))
                xp_ref[pl.ds(rr, 8), H:] = jnp.where(row == 0, ci, pltpu.roll(vi, 1, 0))
                cr, ci = _bcast_row(vr, row, 7), _bcast_row(vi, row, 7)
                xr.append(vr)
                xi.append(vi)
            xs_ref[pl.ds(r0, 16), :H] = jnp.concatenate(xr, axis=0).astype(BF16)
            xs_ref[pl.ds(r0, 16), H:] = jnp.concatenate(xi, axis=0).astype(BF16)
            return cr, ci

        cr, ci = lax.fori_loop(0, SCAN_TR // 16, group,
                               (carry_ref[:, :H], carry_ref[:, H:]))
        carry_ref[:, :H] = cr
        carry_ref[:, H:] = ci
        yc_ref[...] = lax.dot_general(xs_ref[...], c_ref[...], (((1,), (1,)), ((), ())),
                                      preferred_element_type=F32)

    tile = lambda w: pl.BlockSpec((SCAN_TR, w), lambda i: (i, 0))
    whole = lambda a: pl.BlockSpec(a.shape, lambda i: (0, 0))
    return pl.pallas_call(
        body,
        grid=(nt,),
        in_specs=[pl.BlockSpec((SCAN_TR, SSM_W), lambda i: (i, u_blk)),
                  whole(bcat), whole(ccat), whole(abar)],
        out_specs=[tile(2 * H), tile(2 * H), tile(SSM_W)],
        out_shape=[jax.ShapeDtypeStruct((n_rows, 2 * H), BF16),
                   jax.ShapeDtypeStruct((n_rows, 2 * H), F32),
                   jax.ShapeDtypeStruct((n_rows, SSM_W), F32)],
        scratch_shapes=[pltpu.VMEM((SCAN_TR, 2 * H), F32), pltpu.VMEM((4, 8, 2 * H), F32),
                        pltpu.VMEM((8, 2 * H), F32)],
        compiler_params=_cparams(("arbitrary",)),
        name="ssm_scan_fwd",
    )(proj, bcat, ccat, abar)


def _scan_bwd(dy, xprev, bcat, ccat, abar, n_rows):
    H = SSM_H
    nt = n_rows // SCAN_TR

    def body(dy_ref, xp_ref, b_ref, c_ref, a_ref, h_ref, da_ref, du_ref, g_ref, tab_ref, carry_ref):
        @pl.when(pl.program_id(0) == 0)
        def _():
            _scan_tables(a_ref, tab_ref, True)
            carry_ref[...] = jnp.zeros(carry_ref.shape, F32)
            da_ref[...] = jnp.zeros(da_ref.shape, F32)

        g_ref[...] = jnp.dot(dy_ref[...], c_ref[...], preferred_element_type=F32)
        row = lax.broadcasted_iota(jnp.int32, (8, H), 0)
        n16 = SCAN_TR // 16

        def group(jj, carry):
            cr, ci = carry
            r0 = pl.multiple_of((n16 - 1 - jj) * 16, 16)
            hr, hi = [None, None], [None, None]
            for half in (1, 0):
                rr = pl.multiple_of(r0 + 8 * half, 8)
                vr, vi = _scan_group(g_ref[pl.ds(rr, 8), :H], g_ref[pl.ds(rr, 8), H:],
                                     cr, ci, tab_ref, True)
                pr, pi = xp_ref[pl.ds(rr, 8), :H], xp_ref[pl.ds(rr, 8), H:]
                da_ref[:, :H] += vr * pr + vi * pi
                da_ref[:, H:] += vi * pr - vr * pi
                cr, ci = _bcast_row(vr, row, 0), _bcast_row(vi, row, 0)
                hr[half], hi[half] = vr, vi
            h_ref[pl.ds(r0, 16), :H] = jnp.concatenate(hr, axis=0).astype(BF16)
            h_ref[pl.ds(r0, 16), H:] = jnp.concatenate(hi, axis=0).astype(BF16)
            return cr, ci

        cr, ci = lax.fori_loop(0, n16, group, (carry_ref[:, :H], carry_ref[:, H:]))
        carry_ref[:, :H] = cr
        carry_ref[:, H:] = ci
        du_ref[...] = lax.dot_general(h_ref[...], b_ref[...], (((1,), (1,)), ((), ())),
                                      preferred_element_type=F32)

    rev = lambda i: (nt - 1 - i, 0)
    whole = lambda a: pl.BlockSpec(a.shape, lambda i: (0, 0))
    return pl.pallas_call(
        body,
        grid=(nt,),
        in_specs=[pl.BlockSpec((SCAN_TR, SSM_W), rev),
                  pl.BlockSpec((SCAN_TR, 2 * H), rev),
                  whole(bcat), whole(ccat), whole(abar)],
        out_specs=[pl.BlockSpec((SCAN_TR, 2 * H), rev),
                   pl.BlockSpec((8, 2 * H), lambda i: (0, 0)),
                   pl.BlockSpec((SCAN_TR, SSM_W), rev)],
        out_shape=[jax.ShapeDtypeStruct((n_rows, 2 * H), BF16),
                   jax.ShapeDtypeStruct((8, 2 * H), F32),
                   jax.ShapeDtypeStruct((n_rows, SSM_W), F32)],
        scratch_shapes=[pltpu.VMEM((SCAN_TR, 2 * H), F32), pltpu.VMEM((4, 8, 2 * H), F32),
                        pltpu.VMEM((8, 2 * H), F32)],
        compiler_params=_cparams(("arbitrary",)),
        name="ssm_scan_bwd",
    )(dy, xprev, bcat, ccat, abar)


def _adamw(parts, w, m, v, *, tr, ch, name, prefetch=None):
    n_rows, cols = w.shape
    n_parts = len(parts)
    c1 = 1.0 - ADAM_B1 ** ADAM_STEP
    c2 = 1.0 - ADAM_B2 ** ADAM_STEP

    def fn(rv, vv, i, nt):
        g = rv[0].astype(F32)
        for p in rv[1:n_parts]:
            g = g + p.astype(F32)
        wv, mv, vval = rv[n_parts:]
        nm = ADAM_B1 * mv + (1.0 - ADAM_B1) * g
        nv = ADAM_B2 * vval + (1.0 - ADAM_B2) * (g * g)
        delta = -ADAM_LR * ((nm / c1) / (jnp.sqrt(nv / c2) + ADAM_EPS) + ADAM_WD * wv)
        return [g, delta, nm, nv], []

    rows = [_row(arr, lead=lead) for (arr, lead) in parts] + [_row(w), _row(m), _row(v)]
    return _rowwise(fn, rows, [], [(cols, F32)] * 4, [], n_rows=n_rows, tr=tr, ch=ch, name=name,
                    prefetch=prefetch)


_PACK = [
    ("b_ada", 6), ("norm1_g", 1), ("b_in", 3), ("norm2_g", 1), ("final_g", 1),
    ("lambda_re", 1), ("lambda_im", 1), ("log_step", 1), ("attn_sinks", 1),
    ("rel_bias", 1), ("b_glu", 1), ("ssm_d", 1), ("loss", 1),
    ("ssm_b_re", 16), ("ssm_b_im", 16), ("ssm_c_re", 16), ("ssm_c_im", 16),
]
_PACK_OFF = {}
_off = 0
for _n, _r in _PACK:
    _PACK_OFF[_n] = (_off, _r)
    _off += _r
PACK_ROWS = -(-_off // 8) * 8


def _to_rows(a, rows):
    flat = a.reshape(-1).astype(F32)
    pad = rows * PACK_W - flat.shape[0]
    if pad:
        flat = jnp.pad(flat, (0, pad))
    return flat.reshape(rows, PACK_W)


def _b_to_rows(b):
    return jnp.transpose(b, (2, 0, 1)).reshape(SSM_P, SSM_H)


def _rows_to_b(r):
    return jnp.transpose(r.reshape(SSM_P, SSM_G, SSM_N), (1, 2, 0))


def _c_to_rows(cm):
    return jnp.transpose(cm, (1, 0, 2)).reshape(SSM_P, SSM_H)


def _rows_to_c(r):
    return jnp.transpose(r.reshape(SSM_P, SSM_G, SSM_N), (1, 0, 2))


def _pack(vals):
    out = jnp.zeros((PACK_ROWS, PACK_W), F32)
    for n, r in _PACK:
        if n in vals:
            pieces = vals[n] if isinstance(vals[n], list) else [vals[n]]
            rows_each = r // len(pieces)
            for i, piece in enumerate(pieces):
                out = lax.dynamic_update_slice(out, _to_rows(piece, rows_each),
                                               (_PACK_OFF[n][0] + i * rows_each, 0))
    return out


def _unpack(packed, name, shape):
    o, r = _PACK_OFF[name]
    n = int(np.prod(shape))
    return packed[o:o + r].reshape(-1)[:n].reshape(shape)


def _small_params_packed(p):
    return {
        "b_ada": p["b_ada"], "norm1_g": p["norm1_g"], "b_in": p["b_in"],
        "norm2_g": p["norm2_g"], "final_g": p["final_g"],
        "lambda_re": p["lambda_re"], "lambda_im": p["lambda_im"],
        "log_step": p["log_step"], "attn_sinks": p["attn_sinks"],
        "rel_bias": p["rel_bias"], "b_glu": p["b_glu"], "ssm_d": p["ssm_d"],
        "ssm_b_re": _b_to_rows(p["ssm_b_re"][0]), "ssm_b_im": _b_to_rows(p["ssm_b_im"][0]),
        "ssm_c_re": _c_to_rows(p["ssm_c_re"][0]), "ssm_c_im": _c_to_rows(p["ssm_c_im"][0]),
    }


_SMALL_SHAPES = {
    "b_ada": (1, N_MOD * D), "norm1_g": (1, D), "b_in": (1, IN_W), "norm2_g": (1, D),
    "final_g": (D,), "lambda_re": (1, SSM_G, SSM_N), "lambda_im": (1, SSM_G, SSM_N),
    "log_step": (1, SSM_G), "attn_sinks": (1, N_Q_HEADS), "rel_bias": (NUM_BUCKETS, N_Q_HEADS),
    "b_glu": (1, SSM_W), "ssm_d": (1, SSM_W),
}


def _unpack_small(packed, name):
    if name in ("ssm_b_re", "ssm_b_im"):
        o, r = _PACK_OFF[name]
        return _rows_to_b(packed[o:o + r])[None]
    if name in ("ssm_c_re", "ssm_c_im"):
        o, r = _PACK_OFF[name]
        return _rows_to_c(packed[o:o + r])[None]
    return _unpack(packed, name, _SMALL_SHAPES[name])


WEIGHT_ORDER = ['w_ada', 'b_ada', 'norm1_g', 'w_in', 'b_in', 'attn_sinks', 'rel_bias', 'lambda_re',
                'lambda_im', 'log_step', 'ssm_b_re', 'ssm_b_im', 'ssm_c_re', 'ssm_c_im', 'ssm_d',
                'w_glu', 'b_glu', 'w_attn_proj', 'w_ssm_proj', 'w_out', 'norm2_g', 'w_ff1', 'w_ff2',
                'final_g']
BIG = ['w_in', 'w_glu', 'w_attn_proj', 'w_ssm_proj', 'w_out', 'w_ff1', 'w_ff2']


ADAMW_TILE_ELEMS = 1 << 18


def _to_col_blocks(w):
    k, n = w.shape
    return jnp.transpose(w.reshape(k, N_DEV, n // N_DEV), (1, 0, 2))


def _adamw_rows(rows, cols):
    tr = rows
    while tr * cols > ADAMW_TILE_ELEMS and tr % 32 == 0:
        tr //= 2
    return tr


def _cast_to_slot(w, me1, name):
    rows, cols = w.shape
    tr = min(rows, 256)

    def body(me_ref, w_ref, o_ref):
        o_ref[...] = w_ref[...].astype(BF16)

    return pl.pallas_call(
        body,
        grid_spec=pltpu.PrefetchScalarGridSpec(
            num_scalar_prefetch=1, grid=(rows // tr,),
            in_specs=[pl.BlockSpec((tr, cols), lambda i, me_ref: (i, 0))],
            out_specs=pl.BlockSpec((None, tr, cols), lambda i, me_ref: (me_ref[0], i, 0))),
        out_shape=jax.ShapeDtypeStruct((N_DEV, rows, cols), BF16),
        compiler_params=_cparams(("arbitrary",)),
        name=name,
    )(me1, w)


def kernel(x, c, w_ada, b_ada, norm1_g, w_in, b_in, attn_sinks, rel_bias, lambda_re, lambda_im, log_step, ssm_b_re, ssm_b_im, ssm_c_re, ssm_c_im, ssm_d, w_glu, b_glu, w_attn_proj, w_ssm_proj, w_out, norm2_g, w_ff1, w_ff2, final_g, loss_target, m_w_ada, m_b_ada, m_norm1_g, m_w_in, m_b_in, m_attn_sinks, m_rel_bias, m_lambda_re, m_lambda_im, m_log_step, m_ssm_b_re, m_ssm_b_im, m_ssm_c_re, m_ssm_c_im, m_ssm_d, m_w_glu, m_b_glu, m_w_attn_proj, m_w_ssm_proj, m_w_out, m_norm2_g, m_w_ff1, m_w_ff2, m_final_g, v_w_ada, v_b_ada, v_norm1_g, v_w_in, v_b_in, v_attn_sinks, v_rel_bias, v_lambda_re, v_lambda_im, v_log_step, v_ssm_b_re, v_ssm_b_im, v_ssm_c_re, v_ssm_c_im, v_ssm_d, v_w_glu, v_b_glu, v_w_attn_proj, v_w_ssm_proj, v_w_out, v_norm2_g, v_w_ff1, v_w_ff2, v_final_g):
    loc = dict(locals())
    W = {n: loc[n] for n in WEIGHT_ORDER}
    Mo = {n: loc["m_" + n] for n in WEIGHT_ORDER}
    Vo = {n: loc["v_" + n] for n in WEIGHT_ORDER}
    S = x.shape[1]
    TM = min(512, S)
    TS = min(1024, S)
    TR = min(256, S)
    TW = min(1024, S)
    me = 4 * lax.axis_index("x") + 2 * lax.axis_index("y") + lax.axis_index("c")
    x2d = x.reshape(S, D)
    tgt = loss_target.reshape(S, D)

    shard = {n: W[n][0] for n in BIG}
    me1 = jnp.reshape(me, (1,)).astype(jnp.int32)
    zone = {n: _cast_to_slot(shard[n], me1, "cast_" + n) for n in BIG}
    send_in, recv_in, zone_in, tok_in = _relay_gather_start(zone["w_in"], "w_in_start")
    G = {}

    c_all = _small_allgather(c + tok_in[:1, :1], "allgather_c").reshape(N_DEV, D)
    cs = _rowwise(lambda rv, vv, i, nt: ([rv[0] * _sigmoid(rv[0])], []), [_row(c_all)], [],
                  [(D, F32)], [], n_rows=N_DEV, tr=8, ch=8, name="silu_c")[0]
    n_ada = N_MOD * D // N_DEV
    b_ada_cols = lax.dynamic_slice(b_ada, (0, me * n_ada), (1, n_ada))
    mod_piece = _matmul(cs, w_ada[0], mode="nn", dims=(N_DEV, n_ada, D), tiles=(N_DEV, 512, D),
                        out_dtypes=[F32], name="ada_fwd", bias=b_ada_cols)
    mod_all = _small_allgather(mod_piece, "allgather_mod")
    mod_b = lax.dynamic_index_in_dim(mod_all, me, axis=1, keepdims=False).reshape(N_MOD, D)
    sh1, sc1, g1, sh2, sc2, g2 = [mod_b[i:i + 1] for i in range(N_MOD)]

    def f_norm1(rv, vv, i, nt):
        xv, (g, sc, sh) = rv[0], vv
        return [(xv * _rms(xv) * g) * (1.0 + sc) + sh], []

    h = _rowwise(f_norm1, [_row(x2d)], [norm1_g, sc1, sh1], [(D, BF16)], [],
                 n_rows=S, tr=TR, ch=32, name="norm1_fwd", dep=tok_in)[0]
    zone_in = _relay_gather_arrive(send_in, recv_in, zone_in, h, "w_in_arrive")
    later = [n for n in BIG if n != "w_in"]
    flights, tok_w = _exchange_start([zone[n] for n in later], "gather", "weights_start", zone_in)
    w_flight = dict(zip(later, flights))
    G["w_in"] = _relay_gather_finish(zone_in, "w_in_relay")
    proj = _matmul(h, G["w_in"], mode="nn", dims=(S, IN_W, D), tiles=(TM, 768, D),
                   out_dtypes=[BF16], name="in_proj", b3=True, bias=b_in, dep=tok_w)

    buckets = _t5_buckets_block()
    band = _band_mask()
    onehot_t = jnp.asarray(
        (np.arange(128)[:, None] == buckets.reshape(-1)[None, :]).astype(np.float32), BF16)
    band_first = band & (np.arange(2 * BLK)[None, :] >= BLK)
    rel_bias_t = jnp.pad(jnp.transpose(rel_bias), ((0, 0), (0, 128 - NUM_BUCKETS)))
    bias2 = _bias_tables(rel_bias_t, onehot_t,
                         jnp.asarray(band_first.reshape(1, -1).astype(np.float32)),
                         jnp.asarray(band.reshape(1, -1).astype(np.float32))
                         ).reshape(2, N_Q_HEADS * BLK, 2 * BLK)
    sinkcol = jnp.repeat(attn_sinks.reshape(N_Q_HEADS), BLK).reshape(N_Q_HEADS * BLK, 1)
    attn = _attention_fwd(proj, bias2, sinkcol, S)
    mixer = ["w_attn_proj", "w_glu", "w_ssm_proj", "w_out"]
    landed = _exchange_wait([w_flight[n] for n in mixer], "gather", attn, "weights_wait_mixer")
    G.update((n, bufs[0]) for n, bufs in zip(mixer, landed))
    w_glu_f = G["w_glu"].reshape(SSM_W, SSM_W)
    w_out_f = G["w_out"].reshape(D, D)
    w_ap_f = jnp.transpose(G["w_attn_proj"], (1, 0, 2)).reshape(ATTN_W, D)
    w_sp_f = jnp.transpose(G["w_ssm_proj"], (1, 0, 2)).reshape(SSM_W, D)
    y_attn = _matmul(attn, w_ap_f, mode="nn", dims=(S, D, ATTN_W), tiles=(TW, 1024, ATTN_W),
                     out_dtypes=[BF16], name="attn_proj")

    lam_re = lambda_re.reshape(1, SSM_H)
    lam_im = lambda_im.reshape(1, SSM_H)
    ls_x = jnp.repeat(log_step.reshape(SSM_G), SSM_N).reshape(1, SSM_H)
    btr, bti = _b_to_rows(ssm_b_re[0]), _b_to_rows(ssm_b_im[0])
    ctr, cti = _c_to_rows(ssm_c_re[0]), _c_to_rows(ssm_c_im[0])
    abar, bcat, ccat = _ssm_setup(lam_re, lam_im, ls_x, btr, bti, ctr, cti)
    u_blk = (ATTN_W + 2 * KV_W) // SSM_W
    xs, xprev, yc = _scan_fwd(proj, u_blk, bcat, ccat, abar, S)

    def f_ssm_out(rv, vv, i, nt):
        y = rv[0] + vv[0] * rv[1]
        return [y, _gelu(y)], []

    y_ssm_pre, z = _rowwise(f_ssm_out, [_row(yc), _row(proj, u_blk, SSM_W)], [ssm_d],
                            [(SSM_W, F32), (SSM_W, BF16)], [], n_rows=S, tr=TM, ch=32, name="ssm_out")
    zg = _matmul(z, w_glu_f, mode="nn", dims=(S, SSM_W, SSM_W), tiles=(TM, SSM_W, SSM_W),
                 out_dtypes=[F32], name="glu_proj", bias=b_glu)
    z2 = _rowwise(lambda rv, vv, i, nt: ([rv[0].astype(F32) * _sigmoid(rv[1])], []),
                  [_row(z), _row(zg)], [], [(SSM_W, BF16)], [], n_rows=S, tr=TM, ch=32, name="glu_gate")[0]
    y_ssm = _matmul(z2, w_sp_f, mode="nn", dims=(S, D, SSM_W), tiles=(TW, 1024, SSM_W),
                    out_dtypes=[BF16], name="ssm_proj")

    ga_row = _row(proj, 1, D)
    gs_row = _row(proj, 2, D)

    def f_merge(rv, vv, i, nt):
        ga, gs, ya, ys = rv
        return [_sigmoid(ga) * ya + _sigmoid(gs) * ys], []

    merged = _rowwise(f_merge, [ga_row, gs_row, _row(y_attn), _row(y_ssm)], [], [(D, BF16)], [],
                      n_rows=S, tr=TR, ch=32, name="merge")[0]
    mo = _matmul(merged, w_out_f, mode="nn", dims=(S, D, D), tiles=(TW, 1024, D),
                 out_dtypes=[BF16], name="out_proj")

    def f_norm2(rv, vv, i, nt):
        xv, mv = rv
        g1v, g, sc, sh = vv
        x1v = xv + g1v * mv
        return [x1v, (x1v * _rms(x1v) * g) * (1.0 + sc) + sh], []

    x1, h2 = _rowwise(f_norm2, [_row(x2d), _row(mo)], [g1, norm2_g, sc2, sh2],
                      [(D, F32), (D, BF16)], [], n_rows=S, tr=TR, ch=32, name="norm2_fwd")

    def relu_sq(acc):
        r = jnp.maximum(acc, 0.0)
        return r * r, r

    G["w_ff1"] = _exchange_wait([w_flight["w_ff1"]], "gather", h2, "weights_wait_ff1")[0][0]
    act, relu = _matmul(h2, G["w_ff1"], mode="nn", dims=(S, D_FF, D), tiles=(TM, 1024, D),
                        out_dtypes=[BF16, BF16], name="ff1", b3=True, epilogue=relu_sq)
    w_ff2_f = _exchange_wait([w_flight["w_ff2"]], "gather", act, "weights_wait_ff2")[0][0].reshape(D_FF, D)
    ff = _matmul(act, w_ff2_f, mode="nn", dims=(S, D, D_FF), tiles=(TM, 1024, 2048),
                 out_dtypes=[BF16], name="ff2")

    def f_loss(rv, vv, i, nt):
        x1v, ffv, tv = rv
        g2v, gf = vv
        x2v = x1v + g2v * ffv
        r = _rms(x2v)
        xh = x2v * r
        diff = xh * gf - tv
        dy = diff * (1.0 / D)
        dxh = dy * gf
        dx2 = r * (dxh - xh * jnp.mean(dxh * xh, axis=-1, keepdims=True))
        return [dx2, dx2 * g2v], [_colsum(0.5 * diff * diff * (1.0 / D)), _colsum(dy * xh),
                                  _colsum(dx2 * ffv)]

    dx2, dff, loss_cols, d_final_g, dg2 = _rowwise(
        f_loss, [_row(x1), _row(ff), _row(tgt)], [g2, final_g.reshape(1, D)],
        [(D, F32), (D, BF16)], [(1, D)] * 3, n_rows=S, tr=TR, ch=32, name="loss_bwd")

    df1 = _matmul(dff, w_ff2_f, mode="nt", dims=(S, D_FF, D), tiles=(TM, 1024, D),
                  out_dtypes=[BF16], name="ff2_dgrad", extras=(relu,),
                  epilogue=lambda acc, r: (acc * (2.0 * r.astype(F32)),))
    gw_ff2 = _matmul(act, dff, mode="tn", dims=(D_FF, D, S), tiles=(1024, 1024, TS),
                     out_dtypes=[BF16], name="ff2_wgrad").reshape(N_DEV, D_FF // N_DEV, D)
    g_flight = {}
    (g_flight["w_ff2"],), tok = _exchange_start([gw_ff2], "scatter", "grads_start_ff2")
    dh2 = _matmul(df1, G["w_ff1"], mode="nt", dims=(S, D, D_FF), tiles=(TM, D, 1024),
                  out_dtypes=[BF16], name="ff1_dgrad", b3=True, dep=tok)
    gw_ff1 = _matmul(h2, df1, mode="tn", dims=(D, D_FF, S), tiles=(1024, 1024, TS),
                     out_dtypes=[BF16], name="ff1_wgrad", out3=True)
    (g_flight["w_ff1"],), tok = _exchange_start([gw_ff1], "scatter", "grads_start_ff1")

    def f_norm2_bwd(rv, vv, i, nt):
        x1v, dh, dx2v, mv = rv
        g, sc, g1v = vv
        r = _rms(x1v)
        xh = x1v * r
        t = xh * g
        dt = dh * (1.0 + sc)
        dxh = dt * g
        dx1 = dx2v + r * (dxh - xh * jnp.mean(dxh * xh, axis=-1, keepdims=True))
        return [dx1, dx1 * g1v], [_colsum(dh), _colsum(dh * t), _colsum(dt * xh), _colsum(dx1 * mv)]

    dx1, dmo, dsh2, dsc2, d_norm2_g, dg1 = _rowwise(
        f_norm2_bwd, [_row(x1), _row(dh2), _row(dx2), _row(mo)], [norm2_g, sc2, g1],
        [(D, F32), (D, BF16)], [(1, D)] * 4, n_rows=S, tr=TR, ch=16, name="norm2_bwd", dep=tok)

    dmerged = _matmul(dmo, w_out_f, mode="nt", dims=(S, D, D), tiles=(TW, 1024, D),
                      out_dtypes=[BF16], name="out_dgrad")
    gw_out = _matmul(merged, dmo, mode="tn", dims=(D, D, S), tiles=(1024, 1024, TS),
                     out_dtypes=[BF16], name="out_wgrad").reshape(N_DEV, D // N_DEV, D)
    (g_flight["w_out"],), tok = _exchange_start([gw_out], "scatter", "grads_start_out")

    def f_merge_bwd(rv, vv, i, nt):
        dm, ga, gs, ya, ys = rv
        sa, ss = _sigmoid(ga), _sigmoid(gs)
        return [dm * sa, dm * ss, dm * ya * sa * (1.0 - sa), dm * ys * ss * (1.0 - ss)], []

    dy_attn, dy_ssm, dga, dgs = _rowwise(
        f_merge_bwd, [_row(dmerged), ga_row, gs_row, _row(y_attn), _row(y_ssm)], [],
        [(D, BF16)] * 4, [], n_rows=S, tr=TR, ch=16, name="merge_bwd", dep=tok)

    dz2 = _matmul(dy_ssm, w_sp_f, mode="nt", dims=(S, SSM_W, D), tiles=(TW, SSM_W, D),
                  out_dtypes=[F32], name="ssm_proj_dgrad")
    gw_ssm_proj = _to_col_blocks(_matmul(z2, dy_ssm, mode="tn", dims=(SSM_W, D, S), tiles=(SSM_W, 1024, TS),
                                         out_dtypes=[BF16], name="ssm_proj_wgrad"))

    def f_glu_bwd(rv, vv, i, nt):
        dz2v, zv, zgv = rv
        sg = _sigmoid(zgv)
        dzg = dz2v * zv.astype(F32) * sg * (1.0 - sg)
        return [dzg, dz2v * sg], [_colsum(dzg)]

    dzg, dz_a, d_b_glu = _rowwise(f_glu_bwd, [_row(dz2), _row(z), _row(zg)], [],
                                  [(SSM_W, BF16), (SSM_W, F32)], [(1, SSM_W)],
                                  n_rows=S, tr=TM, ch=32, name="glu_bwd")
    dz_b = _matmul(dzg, w_glu_f, mode="nt", dims=(S, SSM_W, SSM_W), tiles=(TM, SSM_W, SSM_W),
                   out_dtypes=[F32], name="glu_dgrad")
    gw_glu = _matmul(z, dzg, mode="tn", dims=(SSM_W, SSM_W, S), tiles=(SSM_W, SSM_W, TS),
                     out_dtypes=[BF16], name="glu_wgrad").reshape(N_DEV, SSM_W // N_DEV, SSM_W)
    (g_flight["w_ssm_proj"], g_flight["w_glu"]), tok = _exchange_start(
        [gw_ssm_proj, gw_glu], "scatter", "grads_start_ssm")

    def f_ssm_out_bwd(rv, vv, i, nt):
        dza, dzb, yv, uv = rv
        dy = (dza + dzb) * _gelu_grad(yv)
        return [dy, dy * vv[0]], [_colsum(dy * uv)]

    dy_s, du_a, d_ssm_d = _rowwise(
        f_ssm_out_bwd, [_row(dz_a), _row(dz_b), _row(y_ssm_pre), _row(proj, u_blk, SSM_W)], [ssm_d],
        [(SSM_W, BF16), (SSM_W, F32)], [(1, SSM_W)], n_rows=S, tr=TM, ch=32, name="ssm_out_bwd", dep=tok)
    hw = SSM_W // 2
    u_half = (ATTN_W + 2 * KV_W) // hw
    dccat = _matmul(dy_s, xs, mode="tn", dims=(hw, 2 * SSM_H, S), tiles=(hw, 1024, TS),
                    out_dtypes=[F32], name="ssm_c_wgrad", a_index=lambda i, j, k: (k, j % 2))
    hs, dacc, du_b = _scan_bwd(dy_s, xprev, bcat, ccat, abar, S)
    dbcat = _matmul(proj, hs, mode="tn", dims=(hw, 2 * SSM_H, S), tiles=(hw, 1024, TS),
                    out_dtypes=[F32], name="ssm_b_wgrad", a_index=lambda i, j, k: (k, u_half + j % 2))
    grp = np.arange(SSM_H) // SSM_N
    gind = jnp.asarray((grp[:, None] == np.arange(128)[None, :]).astype(np.float32), BF16)
    d_lam_re, d_lam_im, d_ls, d_btr, d_bti, d_ctr, d_cti = _ssm_param_bwd(
        lam_re, lam_im, ls_x, btr, bti, dacc, dbcat, dccat, gind)

    dattn = _matmul(dy_attn, w_ap_f, mode="nt", dims=(S, ATTN_W, D), tiles=(TW, ATTN_W, D),
                    out_dtypes=[BF16], name="attn_proj_dgrad")
    gw_attn_proj = _to_col_blocks(_matmul(attn, dy_attn, mode="tn", dims=(ATTN_W, D, S), tiles=(ATTN_W, 1024, TS),
                                          out_dtypes=[BF16], name="attn_proj_wgrad"))
    (g_flight["w_attn_proj"],), tok = _exchange_start(
        [gw_attn_proj], "scatter", "grads_start_attn")
    dq, dkc, dkp, dvc, dvp, dbias, dsink = _attention_bwd(proj, attn, dattn, bias2, sinkcol, S)
    d_bias_b, d_sinks = _bucket_reduce(dbias.reshape(N_Q_HEADS, BLK * 2 * BLK),
                                       dsink.reshape(N_Q_HEADS, BLK), onehot_t)

    def f_dproj(rv, vv, i, nt):
        dqv, kc, kp, vc, vp, dua, dub, gav, gsv = rv
        keep = (i < nt - 1).astype(F32)
        dp = jnp.concatenate([dqv.astype(F32), kc + keep * kp, vc + keep * vp, dua + dub,
                              gav.astype(F32), gsv.astype(F32)], axis=-1)
        return [dp], [_colsum(dp)]

    dproj, d_b_in = _rowwise(
        f_dproj, [_row(dq), _row(dkc), _row(dkp, shift=1), _row(dvc), _row(dvp, shift=1),
                  _row(du_a), _row(du_b), _row(dga), _row(dgs)], [],
        [(IN_W, BF16)], [(1, IN_W)], n_rows=S, tr=BLK, ch=16, name="dproj", dep=tok)
    gw_in = _matmul(h, dproj, mode="tn", dims=(D, IN_W, S), tiles=(1024, 768, TS),
                    out_dtypes=[BF16], name="in_wgrad", out3=True)
    (g_flight["w_in"],), tok = _exchange_start([gw_in], "scatter", "grads_start_in")
    dh = _matmul(dproj, G["w_in"], mode="nt", dims=(S, D, IN_W), tiles=(TM, D, 768),
                 out_dtypes=[BF16], name="in_dgrad", b3=True, dep=tok)

    def f_norm1_bwd(rv, vv, i, nt):
        xv, dhv, dx1v = rv
        g, sc = vv
        r = _rms(xv)
        xh = xv * r
        t = xh * g
        dt = dhv * (1.0 + sc)
        dxh = dt * g
        dxv = dx1v + r * (dxh - xh * jnp.mean(dxh * xh, axis=-1, keepdims=True))
        return [dxv], [_colsum(dhv), _colsum(dhv * t), _colsum(dt * xh)]

    grad_x, dsh1, dsc1, d_norm1_g = _rowwise(
        f_norm1_bwd, [_row(x2d), _row(dh), _row(dx1)], [norm1_g, sc1],
        [(D, F32)], [(1, D)] * 3, n_rows=S, tr=TR, ch=32, name="norm1_bwd")

    part = _pack({
        "b_ada": [dsh1, dsc1, dg1, dsh2, dsc2, dg2], "norm1_g": d_norm1_g, "b_in": d_b_in, "norm2_g": d_norm2_g,
        "final_g": d_final_g, "lambda_re": d_lam_re, "lambda_im": d_lam_im,
        "log_step": d_ls[0, :SSM_G], "attn_sinks": d_sinks[:, 0],
        "rel_bias": jnp.transpose(d_bias_b[:, :NUM_BUCKETS]), "b_glu": d_b_glu, "ssm_d": d_ssm_d,
        "loss": loss_cols, "ssm_b_re": d_btr, "ssm_b_im": d_bti, "ssm_c_re": d_ctr, "ssm_c_im": d_cti,
    })
    zone_small = lax.dynamic_update_slice(lax.empty((N_DEV, PACK_ROWS, PACK_W), F32), part[None], (me, 0, 0))
    (small_flight,), after = _exchange_start([zone_small], "gather", "small_grads_start")

    big_out = {}
    for n in ["w_ff2", "w_ff1", "w_out", "w_ssm_proj", "w_glu", "w_attn_proj", "w_in"]:
        own, recv = _exchange_wait([g_flight[n]], "scatter", after, "grads_wait_" + n[2:])[0]
        rows, cols = shard[n].shape
        parts = [(own, lambda m: m[0])] + [
            (recv, lambda m, j=j: jnp.where(j >= m[0], j + 1, j)) for j in range(N_DEV - 1)]
        big_out[n] = _adamw(parts, shard[n], Mo[n][0], Vo[n][0], tr=_adamw_rows(rows, cols), ch=16,
                            name="adamw_" + n, prefetch=me1)
        after = big_out[n][0]

    part_all = _exchange_wait([small_flight], "gather", after, "small_grads_wait")[0][0]
    wp, mp, vp = [_pack(_small_params_packed(p)) for p in (W, Mo, Vo)]
    sg, sdelta, sm, sv = _adamw([(part_all, d) for d in range(N_DEV)], wp, mp, vp,
                                tr=PACK_ROWS, ch=8, name="adamw_small")
    lo, _ = _PACK_OFF["loss"]
    loss = jnp.sum(sg[lo])

    o_ada, _ = _PACK_OFF["b_ada"]
    dmod_all = part_all[:, o_ada:o_ada + N_MOD, :].reshape(N_DEV, N_MOD * D)
    dmod_cols = lax.dynamic_slice(dmod_all, (0, me * n_ada), (N_DEV, n_ada))
    gw_ada = _matmul(cs, dmod_cols, mode="tn", dims=(D, n_ada, N_DEV), tiles=(D, 512, N_DEV),
                     out_dtypes=[F32], name="ada_wgrad")
    big_out["w_ada"] = _adamw([(gw_ada, 0)], w_ada[0], m_w_ada[0], v_w_ada[0],
                              tr=_adamw_rows(D, n_ada), ch=16, name="adamw_w_ada")

    def leaf(kind, n):
        if n in big_out:
            return big_out[n][kind][None]
        return _unpack_small((sg, sdelta, sm, sv)[kind], n)

    outs = [loss, grad_x.reshape(1, S, D)]
    for kind in range(4):
        outs.extend(leaf(kind, n) for n in WEIGHT_ORDER)
    return tuple(outs)
```

```python
import functools
import math

import numpy as np
import jax
import jax.numpy as jnp
from jax import lax
from jax.experimental import pallas as pl
from jax.experimental.pallas import tpu as pltpu

F32 = jnp.float32
BF16 = jnp.bfloat16
MESH = pl.DeviceIdType.MESH

N_DEV = 8
D = 2048
HEAD_DIM = 64
N_Q_HEADS = 16
N_KV_HEADS = 4
GROUP = N_Q_HEADS // N_KV_HEADS
ATTN_W = N_Q_HEADS * HEAD_DIM
KV_W = N_KV_HEADS * HEAD_DIM
BLK = 128
NUM_BUCKETS = 32
MAX_DISTANCE = 128
NEG_INF = -1e30
SSM_W = 512
SSM_P = 16
SSM_G = 32
SSM_N = 64
SSM_H = SSM_G * SSM_N
D_FF = 4 * D
IN_W = ATTN_W + 2 * KV_W + SSM_W + 2 * D
N_MOD = 6
EPS = 1e-6

ADAM_LR = 0.001
ADAM_B1 = 0.9
ADAM_B2 = 0.999
ADAM_EPS = 1e-08
ADAM_WD = 0.01
ADAM_STEP = 10

VMEM_LIMIT = 56 * 1024 * 1024
PACK_W = 2048


def _cparams(sem):
    return pltpu.CompilerParams(dimension_semantics=sem, vmem_limit_bytes=VMEM_LIMIT)


def _matmul(a, b, *, mode, dims, tiles, out_dtypes, name, a_off=0, b3=False,
            out3=False, bias=None, extras=(), epilogue=None, dep=None, a_index=None, b_index=None):
    M, N, K = dims
    tm, tn, tk = tiles
    assert M % tm == 0 and N % tn == 0 and K % tk == 0, (name, dims, tiles)
    gm, gn, gk = M // tm, N // tn, K // tk
    n_extra = len(extras)
    has_bias = bias is not None
    n_out = len(out_dtypes)

    if mode == "nn":
        a_spec = pl.BlockSpec((tm, tk), lambda i, j, k: (i, a_off + k))
        if b3:
            nb = (N // N_DEV) // tn
            assert nb * tn * N_DEV == N
            b_spec = pl.BlockSpec((None, tk, tn), lambda i, j, k: (j // nb, k, j % nb))
        else:
            b_spec = pl.BlockSpec((tk, tn), lambda i, j, k: (k, j))
        dn = (((1,), (0,)), ((), ()))
    elif mode == "nt":
        a_spec = pl.BlockSpec((tm, tk), lambda i, j, k: (i, a_off + k))
        if b3:
            nb = (K // N_DEV) // tk
            assert nb * tk * N_DEV == K
            b_spec = pl.BlockSpec((None, tn, tk), lambda i, j, k: (k // nb, j, k % nb))
        else:
            b_spec = pl.BlockSpec((tn, tk), lambda i, j, k: (j, k))
        dn = (((1,), (1,)), ((), ()))
    else:
        a_spec = pl.BlockSpec((tk, tm), lambda i, j, k: (k, a_off + i))
        b_spec = pl.BlockSpec((tk, tn), lambda i, j, k: (k, j))
        dn = (((0,), (0,)), ((), ()))
    if a_index is not None:
        a_spec = pl.BlockSpec(a_spec.block_shape, a_index)
    if b_index is not None:
        b_spec = pl.BlockSpec(b_spec.block_shape, b_index)

    if out3:
        nbo = (N // N_DEV) // tn
        assert nbo * tn * N_DEV == N
        o_spec = pl.BlockSpec((None, tm, tn), lambda i, j, k: (j // nbo, i, j % nbo))
        o_shape = (N_DEV, M, N // N_DEV)
    else:
        o_spec = pl.BlockSpec((tm, tn), lambda i, j, k: (i, j))
        o_shape = (M, N)

    in_specs = [a_spec, b_spec]
    args = [a, b]
    if has_bias:
        in_specs.append(pl.BlockSpec((1, tn), lambda i, j, k: (0, j)))
        args.append(bias)
    for e in extras:
        in_specs.append(pl.BlockSpec((tm, tn), lambda i, j, k: (i, j)))
        args.append(e)
    n_dep = 0 if dep is None else 1
    if n_dep:
        in_specs.append(pl.BlockSpec(memory_space=pl.ANY))
        args.append(dep)

    def body(*refs):
        a_ref, b_ref = refs[0], refs[1]
        pos = 2
        bias_ref = None
        if has_bias:
            bias_ref = refs[pos]
            pos += 1
        extra_refs = refs[pos:pos + n_extra]
        pos += n_extra + n_dep
        out_refs = refs[pos:pos + n_out]
        acc_ref = refs[pos + n_out] if gk > 1 else None

        part = lax.dot_general(a_ref[...].astype(BF16), b_ref[...].astype(BF16), dn,
                               preferred_element_type=F32)

        def finish(acc):
            if has_bias:
                acc = acc + bias_ref[...]
            if epilogue is None:
                vals = (acc,)
            else:
                vals = epilogue(acc, *[e[...] for e in extra_refs])
            for o_ref, val in zip(out_refs, vals):
                o_ref[...] = val.astype(o_ref.dtype)

        if gk == 1:
            finish(part)
        else:
            k = pl.program_id(2)

            @pl.when(k == 0)
            def _():
                acc_ref[...] = part

            @pl.when(k > 0)
            def _():
                acc_ref[...] += part

            @pl.when(k == gk - 1)
            def _():
                finish(acc_ref[...])

    outs = pl.pallas_call(
        body,
        grid=(gm, gn, gk),
        in_specs=in_specs,
        out_specs=[o_spec] * n_out,
        out_shape=[jax.ShapeDtypeStruct(o_shape, dt) for dt in out_dtypes],
        scratch_shapes=([pltpu.VMEM((tm, tn), F32)] if gk > 1 else []),
        compiler_params=_cparams(("parallel", "parallel", "arbitrary")),
        name=name,
    )(*args)
    return outs[0] if n_out == 1 else outs


def _rowwise(fn, rows, vecs, row_outs, sum_outs, *, n_rows, tr, ch, name, dep=None, prefetch=None):
    assert n_rows % tr == 0 and tr % ch == 0
    nt = n_rows // tr
    nr, nv, nro, nso = len(rows), len(vecs), len(row_outs), len(sum_outs)
    in_specs, args = [], []
    n_pf = 0 if prefetch is None else 1
    for (arr, lead, cblk, w, shift) in rows:
        if shift:
            ridx = lambda i, shift=shift: jnp.minimum(i + shift, nt - 1)
        else:
            ridx = lambda i: i
        if arr.ndim == 3:
            def imap(i, *pf, lead=lead, cblk=cblk, ridx=ridx):
                return (lead(pf[0]) if callable(lead) else lead, ridx(i), cblk)
            in_specs.append(pl.BlockSpec((None, tr, w), imap))
        else:
            in_specs.append(pl.BlockSpec(
                (tr, w), lambda i, *pf, cblk=cblk, ridx=ridx: (ridx(i), cblk)))
        args.append(arr)
    for v in vecs:
        in_specs.append(pl.BlockSpec(v.shape, lambda i, *pf, nd=v.ndim: (0,) * nd))
        args.append(v)
    n_dep = 0 if dep is None else 1
    if n_dep:
        in_specs.append(pl.BlockSpec(memory_space=pl.ANY))
        args.append(dep)
    out_specs = [pl.BlockSpec((tr, w), lambda i, *pf: (i, 0)) for (w, _) in row_outs]
    out_shape = [jax.ShapeDtypeStruct((n_rows, w), dt) for (w, dt) in row_outs]
    for (r, w) in sum_outs:
        out_specs.append(pl.BlockSpec((r, w), lambda i, *pf: (0, 0)))
        out_shape.append(jax.ShapeDtypeStruct((r, w), F32))

    def body(*refs):
        refs = refs[n_pf:]
        i = pl.program_id(0)
        r_in = refs[:nr]
        v_in = refs[nr:nr + nv]
        r_out = refs[nr + nv + n_dep:nr + nv + n_dep + nro]
        s_out = refs[nr + nv + n_dep + nro:]
        if nso:
            @pl.when(i == 0)
            def _():
                for s in s_out:
                    s[...] = jnp.zeros(s.shape, F32)
        vvals = [v[...] for v in v_in]

        def chunk(ci, carry):
            r0 = pl.multiple_of(ci * ch, ch)
            rv = [r[pl.ds(r0, ch), :].astype(F32) for r in r_in]
            ro, so = fn(rv, vvals, i, nt)
            for ref, val in zip(r_out, ro):
                ref[pl.ds(r0, ch), :] = val.astype(ref.dtype)
            for ref, val in zip(s_out, so):
                ref[...] += val
            return carry

        lax.fori_loop(0, tr // ch, chunk, 0)

    outs = pl.pallas_call(
        body,
        grid_spec=pltpu.PrefetchScalarGridSpec(
            num_scalar_prefetch=n_pf, grid=(nt,), in_specs=in_specs, out_specs=out_specs),
        out_shape=out_shape,
        compiler_params=_cparams(("arbitrary",)),
        name=name,
    )(*([prefetch] if n_pf else []), *args)
    return outs


def _row(arr, cblk=0, w=None, lead=0, shift=0):
    return (arr, lead, cblk, arr.shape[-1] if w is None else w, shift)


def _colsum(v):
    return jnp.sum(v, axis=0, keepdims=True)


def _rms(x):
    return lax.rsqrt(jnp.mean(x * x, axis=-1, keepdims=True) + EPS)


def _sigmoid(x):
    return 1.0 / (1.0 + jnp.exp(-x))


_GELU_C = math.sqrt(2.0 / math.pi)


def _gelu(x):
    return 0.5 * x * (1.0 + jnp.tanh(_GELU_C * (x + 0.044715 * (x * x * x))))


def _gelu_grad(x):
    t = jnp.tanh(_GELU_C * (x + 0.044715 * (x * x * x)))
    return 0.5 * (1.0 + t) + 0.5 * x * (1.0 - t * t) * (_GELU_C * (1.0 + 3.0 * 0.044715 * (x * x)))


def _my_pos():
    return lax.axis_index("x"), lax.axis_index("y"), lax.axis_index("c")


def _flip(pos, k):
    x, y, c = pos
    return (1 - x if k & 4 else x, 1 - y if k & 2 else y, 1 - c if k & 1 else c)


def _dev_id(pos):
    return 4 * pos[0] + 2 * pos[1] + pos[2]


def _small_allgather(x, name):
    r, c = x.shape

    def body(x_ref, out_ref, send_sems, recv_sems):
        me = _my_pos()
        out_ref[_dev_id(me)] = x_ref[...]
        copies = []
        for k in range(1, N_DEV):
            cp = pltpu.make_async_remote_copy(
                src_ref=x_ref, dst_ref=out_ref.at[_dev_id(me)],
                send_sem=send_sems.at[k - 1], recv_sem=recv_sems.at[k - 1],
                device_id=_flip(me, k), device_id_type=MESH)
            cp.start()
            copies.append(cp)
        for k in range(1, N_DEV):
            peer = _flip(me, k)
            pltpu.make_async_remote_copy(
                src_ref=x_ref, dst_ref=out_ref.at[_dev_id(peer)],
                send_sem=send_sems.at[k - 1], recv_sem=recv_sems.at[k - 1],
                device_id=peer, device_id_type=MESH).wait_recv()
        for cp in copies:
            cp.wait_send()

    return pl.pallas_call(
        body,
        out_shape=jax.ShapeDtypeStruct((N_DEV, r, c), x.dtype),
        in_specs=[pl.BlockSpec(memory_space=pltpu.VMEM)],
        out_specs=pl.BlockSpec(memory_space=pltpu.VMEM),
        scratch_shapes=[pltpu.SemaphoreType.DMA((N_DEV - 1,)),
                        pltpu.SemaphoreType.DMA((N_DEV - 1,))],
        compiler_params=pltpu.CompilerParams(vmem_limit_bytes=VMEM_LIMIT),
        name=name,
    )(x)


_HBM = pl.BlockSpec(memory_space=pltpu.HBM)
_SEM = pl.BlockSpec(memory_space=pltpu.SEMAPHORE)
_EFFECT = pltpu.SideEffectType.DATAFLOW_SIDE_EFFECTING


def _relay_copy(zone, send_sems, recv_sems, k, block, to):
    slot = zone.at[_dev_id(block)]
    return pltpu.make_async_remote_copy(
        src_ref=slot, dst_ref=slot, send_sem=send_sems.at[k], recv_sem=recv_sems.at[k],
        device_id=to, device_id_type=MESH)


def _relay_peers():
    x, y, c = _my_pos()
    return (x, y, c), (x, y, 1 - c), [(1 - x, y), (x, 1 - y), (1 - x, 1 - y)]


def _relay_start_call(zones, n_sems, issue, name, after=None):
    n = len(zones)
    n_after = 0 if after is None else 1

    def body(*refs):
        refs = refs[:n] + refs[n + n_after:]
        send, recv, token = refs[n:2 * n], refs[2 * n:3 * n], refs[4 * n]
        for a in range(n):
            issue(refs[a], send[a], recv[a])
        token[...] = jnp.zeros(token.shape, token.dtype)

    sem = pltpu.SemaphoreType.DMA((n_sems,))
    outs = pl.pallas_call(
        body,
        name=name,
        out_shape=([sem] * (2 * n) + [pltpu.HBM(z.shape, z.dtype) for z in zones]
                   + [jax.ShapeDtypeStruct((8, 128), F32)]),
        in_specs=[_HBM] * n + [pl.BlockSpec(memory_space=pl.ANY)] * n_after,
        out_specs=[_SEM] * (2 * n) + [_HBM] * n + [pl.BlockSpec(memory_space=pltpu.VMEM)],
        input_output_aliases={a: 2 * n + a for a in range(n)},
        compiler_params=pltpu.CompilerParams(has_side_effects=_EFFECT),
    )(*[pltpu.with_memory_space_constraint(z, pltpu.HBM) for z in zones],
      *([after] if n_after else []))
    return [(outs[a], outs[n + a], outs[2 * n + a]) for a in range(n)], outs[3 * n]


def _relay_wait_call(flights, settle, after, name):
    n = len(flights)

    def body(*refs):
        send, recv = refs[n:2 * n], refs[2 * n:3 * n]
        for a in range(n):
            settle(refs[a], send[a], recv[a])

    outs = pl.pallas_call(
        body,
        name=name,
        out_shape=[pltpu.HBM(f[2].shape, f[2].dtype) for f in flights],
        in_specs=[_HBM] * n + [_SEM] * (2 * n) + [pl.BlockSpec(memory_space=pl.ANY)],
        out_specs=[_HBM] * n,
        input_output_aliases={a: a for a in range(n)},
        compiler_params=pltpu.CompilerParams(has_side_effects=_EFFECT),
    )(*[f[2] for f in flights], *[f[0] for f in flights], *[f[1] for f in flights], after)
    return list(outs)


def _relay_gather_start(zones, name, after=None):
    def issue(zone, send, recv):
        me, sib, chips = _relay_peers()
        _relay_copy(zone, send, recv, 0, me, sib).start()
        for j, chip in enumerate(chips):
            _relay_copy(zone, send, recv, 1 + j, me, (*chip, me[2])).start()
    return _relay_start_call(zones, 4, issue, name, after)


def _relay_gather_arrive(flights, after, name):
    def settle(zone, send, recv):
        me, sib, chips = _relay_peers()
        _relay_copy(zone, send, recv, 0, sib, me).wait_recv()
        _relay_copy(zone, send, recv, 0, me, sib).wait_send()
        for j, chip in enumerate(chips):
            _relay_copy(zone, send, recv, 1 + j, (*chip, me[2]), me).wait_recv()
            _relay_copy(zone, send, recv, 1 + j, me, (*chip, me[2])).wait_send()
    return _relay_wait_call(flights, settle, after, name)


def _relay_pass_start(zones, name, after=None):
    def issue(zone, send, recv):
        me, sib, chips = _relay_peers()
        for j, chip in enumerate(chips):
            _relay_copy(zone, send, recv, j, (*chip, me[2]), sib).start()
    return _relay_start_call(zones, 3, issue, name, after)


def _relay_pass_wait(flights, after, name):
    def settle(zone, send, recv):
        me, sib, chips = _relay_peers()
        for j, chip in enumerate(chips):
            _relay_copy(zone, send, recv, j, (*chip, sib[2]), me).wait_recv()
            _relay_copy(zone, send, recv, j, (*chip, me[2]), sib).wait_send()
    return _relay_wait_call(flights, settle, after, name)


def _exchange_copy(kind, bufs, send_sems, recv_sems, me, k, arriving):
    peer = _flip(me, k)
    my_id, peer_id = _dev_id(me), _dev_id(peer)
    if kind == "gather":
        slot = bufs[0].at[peer_id if arriving else my_id]
        src, dst = slot, slot
    else:
        src = bufs[0].at[my_id if arriving else peer_id]
        dst = bufs[1].at[peer_id if arriving else my_id]
    return pltpu.make_async_remote_copy(
        src_ref=src, dst_ref=dst, send_sem=send_sems.at[k - 1], recv_sem=recv_sems.at[k - 1],
        device_id=peer, device_id_type=MESH)


def _exchange_start(arrays, kind, name, after=None):
    n = len(arrays)
    n_after = 0 if after is None else 1
    if kind == "gather":
        bufs = [[a] for a in arrays]
    else:
        bufs = [[a, lax.empty(a.shape, a.dtype)] for a in arrays]
    nb = len(bufs[0])
    flat = [b for group in bufs for b in group]

    def body(*refs):
        outs_at = nb * n + n_after
        send = refs[outs_at:outs_at + n]
        recv = refs[outs_at + n:outs_at + 2 * n]
        token = refs[outs_at + 2 * n + nb * n]
        me = _my_pos()
        for a in range(n):
            for k in range(1, N_DEV):
                _exchange_copy(kind, refs[nb * a:nb * (a + 1)], send[a], recv[a], me, k, False).start()
        token[...] = jnp.zeros(token.shape, token.dtype)

    sem = pltpu.SemaphoreType.DMA((N_DEV - 1,))
    outs = pl.pallas_call(
        body,
        name=name,
        out_shape=([sem] * (2 * n) + [pltpu.HBM(b.shape, b.dtype) for b in flat]
                   + [jax.ShapeDtypeStruct((8, 128), F32)]),
        in_specs=[_HBM] * (nb * n) + [pl.BlockSpec(memory_space=pl.ANY)] * n_after,
        out_specs=[_SEM] * (2 * n) + [_HBM] * (nb * n) + [pl.BlockSpec(memory_space=pltpu.VMEM)],
        input_output_aliases={i: 2 * n + i for i in range(nb * n)},
        compiler_params=pltpu.CompilerParams(has_side_effects=_EFFECT),
    )(*[pltpu.with_memory_space_constraint(b, pltpu.HBM) for b in flat],
      *([after] if n_after else []))
    flights = [(outs[a], outs[n + a], list(outs[2 * n + nb * a:2 * n + nb * (a + 1)]))
               for a in range(n)]
    return flights, outs[2 * n + nb * n]


def _exchange_wait(flights, kind, after, name):
    n = len(flights)
    nb = len(flights[0][2])
    flat = [b for f in flights for b in f[2]]

    def body(*refs):
        send = refs[nb * n:nb * n + n]
        recv = refs[nb * n + n:nb * n + 2 * n]
        me = _my_pos()
        for a in range(n):
            for k in range(1, N_DEV):
                bufs = refs[nb * a:nb * (a + 1)]
                _exchange_copy(kind, bufs, send[a], recv[a], me, k, False).wait_send()
                _exchange_copy(kind, bufs, send[a], recv[a], me, k, True).wait_recv()

    outs = pl.pallas_call(
        body,
        name=name,
        out_shape=[pltpu.HBM(b.shape, b.dtype) for b in flat],
        in_specs=[_HBM] * (nb * n) + [_SEM] * (2 * n) + [pl.BlockSpec(memory_space=pl.ANY)],
        out_specs=[_HBM] * (nb * n),
        input_output_aliases={i: i for i in range(nb * n)},
        compiler_params=pltpu.CompilerParams(has_side_effects=_EFFECT),
    )(*flat, *[f[0] for f in flights], *[f[1] for f in flights], after)
    return [list(outs[nb * a:nb * (a + 1)]) for a in range(n)]


def _t5_buckets_block():
    qi = np.arange(BLK)[:, None]
    ki = np.arange(2 * BLK)[None, :]
    n = np.maximum(qi + BLK - ki, 0)
    max_exact = NUM_BUCKETS // 2
    large = max_exact + (np.log(np.maximum(n, 1) / max_exact)
                         / np.log(MAX_DISTANCE / max_exact)
                         * (NUM_BUCKETS - max_exact)).astype(np.int32)
    large = np.minimum(large, NUM_BUCKETS - 1)
    return np.where(n < max_exact, n, large).astype(np.int32)


def _band_mask():
    qi = np.arange(BLK)[:, None]
    ki = np.arange(2 * BLK)[None, :]
    dist = qi + BLK - ki
    return (dist >= 0) & (dist < BLK)


def _attn_scores(q_ref, kp_ref, kc_ref, hkv):
    c0 = hkv * HEAD_DIM
    kk = jnp.concatenate([kp_ref[:, c0:c0 + HEAD_DIM], kc_ref[:, c0:c0 + HEAD_DIM]],
                         axis=0).astype(BF16)
    qg = jnp.concatenate(
        [q_ref[:, (hkv * GROUP + g) * HEAD_DIM:(hkv * GROUP + g + 1) * HEAD_DIM]
         for g in range(GROUP)], axis=0).astype(BF16)
    s = lax.dot_general(qg, kk, (((1,), (1,)), ((), ())), preferred_element_type=F32)
    return qg, kk, s


def _attn_softmax(s, bias_ref, sink_ref, hkv):
    r0, r1 = hkv * GROUP * BLK, (hkv + 1) * GROUP * BLK
    s = s * (HEAD_DIM ** -0.5) + bias_ref[r0:r1, :]
    sink = sink_ref[r0:r1, :]
    m = jnp.maximum(jnp.max(s, axis=-1, keepdims=True), sink)
    p = jnp.exp(s - m)
    e_sink = jnp.exp(sink - m)
    inv = 1.0 / (jnp.sum(p, axis=-1, keepdims=True) + e_sink)
    return p * inv, e_sink * inv


def _kv_rows(p_ref, c_ref, hkv):
    c0 = hkv * HEAD_DIM
    return jnp.concatenate([p_ref[:, c0:c0 + HEAD_DIM], c_ref[:, c0:c0 + HEAD_DIM]],
                           axis=0).astype(BF16)


def _attn_in_specs(bias2):
    prev = lambda n: jnp.maximum(n - 1, 0)
    return [
        pl.BlockSpec((BLK, ATTN_W), lambda n: (n, 0)),
        pl.BlockSpec((BLK, KV_W), lambda n: (prev(n), ATTN_W // KV_W)),
        pl.BlockSpec((BLK, KV_W), lambda n: (n, ATTN_W // KV_W)),
        pl.BlockSpec((BLK, KV_W), lambda n: (prev(n), ATTN_W // KV_W + 1)),
        pl.BlockSpec((BLK, KV_W), lambda n: (n, ATTN_W // KV_W + 1)),
        pl.BlockSpec((None,) + bias2.shape[1:], lambda n: (jnp.minimum(n, 1), 0, 0)),
    ]


def _attention_fwd(proj, bias2, sinkcol, n_rows):
    nb = n_rows // BLK

    def body(q_ref, kp_ref, kc_ref, vp_ref, vc_ref, bias_ref, sink_ref, o_ref):
        heads = range(N_KV_HEADS)
        scores = [_attn_scores(q_ref, kp_ref, kc_ref, hkv)[2] for hkv in heads]
        probs = [_attn_softmax(scores[hkv], bias_ref, sink_ref, hkv)[0] for hkv in heads]
        outs = [jnp.dot(probs[hkv].astype(BF16), _kv_rows(vp_ref, vc_ref, hkv),
                        preferred_element_type=F32) for hkv in heads]
        for hkv in heads:
            for g in range(GROUP):
                h = hkv * GROUP + g
                o_ref[:, h * HEAD_DIM:(h + 1) * HEAD_DIM] = (
                    outs[hkv][g * BLK:(g + 1) * BLK, :].astype(o_ref.dtype))

    return pl.pallas_call(
        body,
        grid=(nb,),
        in_specs=_attn_in_specs(bias2) + [pl.BlockSpec(sinkcol.shape, lambda n: (0, 0))],
        out_specs=pl.BlockSpec((BLK, ATTN_W), lambda n: (n, 0)),
        out_shape=jax.ShapeDtypeStruct((n_rows, ATTN_W), BF16),
        compiler_params=_cparams(("parallel",)),
        name="attn_fwd",
    )(proj, proj, proj, proj, proj, bias2, sinkcol)


def _attention_bwd(proj, attn, dattn, bias2, sinkcol, n_rows):
    nb = n_rows // BLK
    scale = HEAD_DIM ** -0.5
    dn_t = (((0,), (0,)), ((), ()))

    def body(q_ref, kp_ref, kc_ref, vp_ref, vc_ref, bias_ref, o_ref, do_ref, sink_ref,
             dq_ref, dkc_ref, dkp_ref, dvc_ref, dvp_ref, dbias_ref, dsink_ref):
        @pl.when(pl.program_id(0) == 0)
        def _():
            dbias_ref[...] = jnp.zeros(dbias_ref.shape, F32)
            dsink_ref[...] = jnp.zeros(dsink_ref.shape, F32)

        heads = range(N_KV_HEADS)
        qk = [_attn_scores(q_ref, kp_ref, kc_ref, hkv) for hkv in heads]
        dog, dps, deltas = [], [], []
        for hkv in heads:
            hs = [hkv * GROUP + g for g in range(GROUP)]
            d_o = jnp.concatenate([do_ref[:, h * HEAD_DIM:(h + 1) * HEAD_DIM] for h in hs], axis=0)
            o = jnp.concatenate([o_ref[:, h * HEAD_DIM:(h + 1) * HEAD_DIM] for h in hs], axis=0)
            deltas.append(jnp.sum(d_o.astype(F32) * o.astype(F32), axis=-1, keepdims=True))
            dog.append(d_o.astype(BF16))
            dps.append(lax.dot_general(dog[hkv], _kv_rows(vp_ref, vc_ref, hkv),
                                       (((1,), (1,)), ((), ())), preferred_element_type=F32))
        p16, ds16 = [], []
        for hkv in heads:
            r0, r1 = hkv * GROUP * BLK, (hkv + 1) * GROUP * BLK
            p, p_sink = _attn_softmax(qk[hkv][2], bias_ref, sink_ref, hkv)
            ds = p * (dps[hkv] - deltas[hkv])
            dbias_ref[r0:r1, :] += ds
            dsink_ref[r0:r1, :] += -(p_sink * deltas[hkv])
            p16.append(p.astype(BF16))
            ds16.append(ds.astype(BF16))
        for hkv in heads:
            c0 = hkv * HEAD_DIM
            qg, kk, _ = qk[hkv]
            dqg = jnp.dot(ds16[hkv], kk, preferred_element_type=F32) * scale
            dkk = lax.dot_general(ds16[hkv], qg, dn_t, preferred_element_type=F32) * scale
            dvv = lax.dot_general(p16[hkv], dog[hkv], dn_t, preferred_element_type=F32)
            for g in range(GROUP):
                h = hkv * GROUP + g
                dq_ref[:, h * HEAD_DIM:(h + 1) * HEAD_DIM] = (
                    dqg[g * BLK:(g + 1) * BLK, :].astype(dq_ref.dtype))
            dkp_ref[:, c0:c0 + HEAD_DIM] = dkk[:BLK].astype(dkp_ref.dtype)
            dkc_ref[:, c0:c0 + HEAD_DIM] = dkk[BLK:].astype(dkc_ref.dtype)
            dvp_ref[:, c0:c0 + HEAD_DIM] = dvv[:BLK].astype(dvp_ref.dtype)
            dvc_ref[:, c0:c0 + HEAD_DIM] = dvv[BLK:].astype(dvc_ref.dtype)

    kv_out = pl.BlockSpec((BLK, KV_W), lambda n: (n, 0))
    kv_shape = jax.ShapeDtypeStruct((n_rows, KV_W), F32)
    acc_shape = bias2.shape[1:]
    return pl.pallas_call(
        body,
        grid=(nb,),
        in_specs=_attn_in_specs(bias2) + [
            pl.BlockSpec((BLK, ATTN_W), lambda n: (n, 0)),
            pl.BlockSpec((BLK, ATTN_W), lambda n: (n, 0)),
            pl.BlockSpec(sinkcol.shape, lambda n: (0, 0)),
        ],
        out_specs=[
            pl.BlockSpec((BLK, ATTN_W), lambda n: (n, 0)),
            kv_out, kv_out, kv_out, kv_out,
            pl.BlockSpec(acc_shape, lambda n: (0, 0)),
            pl.BlockSpec(sinkcol.shape, lambda n: (0, 0)),
        ],
        out_shape=[
            jax.ShapeDtypeStruct((n_rows, ATTN_W), BF16),
            kv_shape, kv_shape, kv_shape, kv_shape,
            jax.ShapeDtypeStruct(acc_shape, F32),
            jax.ShapeDtypeStruct(sinkcol.shape, F32),
        ],
        compiler_params=_cparams(("arbitrary",)),
        name="attn_bwd",
    )(proj, proj, proj, proj, proj, bias2, attn, dattn, sinkcol)


def _bias_tables(rel_bias_t, onehot_t, band_first, band_rest):
    def body(rb_ref, oh_ref, mf_ref, mr_ref, out_ref):
        acc = jnp.zeros((N_Q_HEADS, BLK * 2 * BLK), F32)
        for part in _split3(rb_ref[...]):
            acc = acc + jnp.dot(part, oh_ref[...], preferred_element_type=F32)
        out_ref[0] = jnp.where(mf_ref[...] > 0.0, acc, NEG_INF)
        out_ref[1] = jnp.where(mr_ref[...] > 0.0, acc, NEG_INF)

    return pl.pallas_call(
        body,
        out_shape=jax.ShapeDtypeStruct((2, N_Q_HEADS, BLK * 2 * BLK), F32),
        compiler_params=pltpu.CompilerParams(vmem_limit_bytes=VMEM_LIMIT),
        name="bias_tables",
    )(rel_bias_t, onehot_t, band_first, band_rest)


def _split3(a):
    hi = a.astype(BF16)
    r1 = a - hi.astype(F32)
    mid = r1.astype(BF16)
    lo = (r1 - mid.astype(F32)).astype(BF16)
    return hi, mid, lo


def _bucket_reduce(dbias, dsink, onehot_t):
    def body(db_ref, ds_ref, oh_ref, ob_ref, os_ref):
        acc = jnp.zeros((N_Q_HEADS, 128), F32)
        for part in _split3(db_ref[...]):
            acc = acc + lax.dot_general(part, oh_ref[...], (((1,), (1,)), ((), ())),
                                        preferred_element_type=F32)
        ob_ref[...] = acc
        os_ref[...] = jnp.broadcast_to(jnp.sum(ds_ref[...], axis=-1, keepdims=True),
                                       os_ref.shape)

    return pl.pallas_call(
        body,
        out_shape=[jax.ShapeDtypeStruct((N_Q_HEADS, 128), F32),
                   jax.ShapeDtypeStruct((N_Q_HEADS, 128), F32)],
        compiler_params=pltpu.CompilerParams(vmem_limit_bytes=VMEM_LIMIT),
        name="bias_bucket_reduce",
    )(dbias, dsink, onehot_t)


def _disc(lr, li, ls, btr, bti):
    lam_re = jnp.minimum(lr, -1e-4)
    delta = jnp.exp(ls)
    mag = jnp.exp(lam_re * delta)
    ang = li * delta
    ar, ai = mag * jnp.cos(ang), mag * jnp.sin(ang)
    nr, ni = ar - 1.0, ai
    den = lam_re * lam_re + li * li
    fr = (nr * lam_re + ni * li) / den
    fi = (ni * lam_re - nr * li) / den
    bbr = fr * btr - fi * bti
    bbi = fr * bti + fi * btr
    return ar, ai, bbr, bbi


def _block_mask():
    row = lax.broadcasted_iota(jnp.int32, (SSM_W, SSM_H), 0)
    col = lax.broadcasted_iota(jnp.int32, (SSM_W, SSM_H), 1)
    return (row // SSM_P) == (col // SSM_N)


def _ssm_setup(lr, li, ls, btr, bti, ctr, cti):
    def body(lr_ref, li_ref, ls_ref, btr_ref, bti_ref, ctr_ref, cti_ref, a_ref, b_ref, c_ref):
        ar, ai, bbr, bbi = _disc(lr_ref[...], li_ref[...], ls_ref[...], btr_ref[...], bti_ref[...])
        a_ref[:, :SSM_H] = ar
        a_ref[:, SSM_H:] = ai
        mask = _block_mask()
        blk = lambda t: jnp.where(mask, jnp.tile(t, (SSM_G, 1)), 0.0)
        b_ref[:, :SSM_H] = blk(bbr).astype(BF16)
        b_ref[:, SSM_H:] = blk(bbi).astype(BF16)
        c_ref[:, :SSM_H] = blk(ctr_ref[...]).astype(BF16)
        c_ref[:, SSM_H:] = blk(-cti_ref[...]).astype(BF16)

    return pl.pallas_call(
        body,
        out_shape=[jax.ShapeDtypeStruct((1, 2 * SSM_H), F32),
                   jax.ShapeDtypeStruct((SSM_W, 2 * SSM_H), BF16),
                   jax.ShapeDtypeStruct((SSM_W, 2 * SSM_H), BF16)],
        compiler_params=pltpu.CompilerParams(vmem_limit_bytes=VMEM_LIMIT),
        name="ssm_setup",
    )(lr, li, ls, btr, bti, ctr, cti)


def _ssm_param_bwd(lr, li, ls, btr, bti, dacc, dbcat, dccat, gind):
    def body(lr_ref, li_ref, ls_ref, btr_ref, bti_ref, dacc_ref, db_ref, dc_ref, g_ref,
             dlr_ref, dli_ref, dls_ref, dbtr_ref, dbti_ref, dctr_ref, dcti_ref):
        dar = jnp.sum(dacc_ref[:, :SSM_H], axis=0, keepdims=True)
        dai = jnp.sum(dacc_ref[:, SSM_H:], axis=0, keepdims=True)
        col = lax.broadcasted_iota(jnp.int32, (SSM_P, 2 * SSM_H), 1)
        grp = (col % SSM_H) // SSM_N
        db = jnp.zeros((SSM_P, 2 * SSM_H), F32)
        dc = jnp.zeros((SSM_P, 2 * SSM_H), F32)
        half = SSM_G // 2
        for g in range(SSM_G):
            sel = grp == g
            r0 = (g % half) * SSM_P
            db = db + jnp.where(sel, db_ref[r0:r0 + SSM_P, :], 0.0)
            dc = dc + jnp.where(sel, dc_ref[r0:r0 + SSM_P, :], 0.0)
        dctr_ref[...] = dc[:, :SSM_H]
        dcti_ref[...] = -dc[:, SSM_H:]
        prim = (lr_ref[...], li_ref[...], ls_ref[...], btr_ref[...], bti_ref[...])
        _, vjp = jax.vjp(_disc, *prim)
        dlr, dli, dls, dbtr, dbti = vjp((dar, dai, db[:, :SSM_H], db[:, SSM_H:]))
        dlr_ref[...] = dlr
        dli_ref[...] = dli
        dbtr_ref[...] = dbtr
        dbti_ref[...] = dbti
        acc = jnp.zeros((8, 128), F32)
        for part in _split3(jnp.broadcast_to(dls, (8, SSM_H))):
            acc = acc + jnp.dot(part, g_ref[...], preferred_element_type=F32)
        dls_ref[...] = acc

    vec = jax.ShapeDtypeStruct((1, SSM_H), F32)
    mat = jax.ShapeDtypeStruct((SSM_P, SSM_H), F32)
    return pl.pallas_call(
        body,
        out_shape=[vec, vec, jax.ShapeDtypeStruct((8, 128), F32), mat, mat, mat, mat],
        compiler_params=pltpu.CompilerParams(vmem_limit_bytes=VMEM_LIMIT),
        name="ssm_param_bwd",
    )(lr, li, ls, btr, bti, dacc, dbcat, dccat, gind)


SCAN_TR = 256


def _cmul_add(vr, vi, pr, pi, sr, si):
    return vr + pr * sr - pi * si, vi + pr * si + pi * sr


def _bcast_row(v, row, which):
    b = jnp.where(row == which, v, 0.0)
    b = b + pltpu.roll(b, 4, 0)
    b = b + pltpu.roll(b, 2, 0)
    return b + pltpu.roll(b, 1, 0)


def _scan_tables(a_ref, tab_ref, reverse):
    H = SSM_H
    ar = jnp.broadcast_to(a_ref[:, :H], (8, H))
    ai = jnp.broadcast_to(a_ref[:, H:], (8, H))
    if reverse:
        ai = -ai
    row = lax.broadcasted_iota(jnp.int32, (8, H), 0)
    pw = [(ar, ai)]
    for _ in range(7):
        cr, ci = pw[-1]
        pw.append((cr * ar - ci * ai, cr * ai + ci * ar))
    pcr = jnp.zeros((8, H), F32)
    pci = jnp.zeros((8, H), F32)
    for e in range(8):
        sel = (row == (7 - e)) if reverse else (row == e)
        pcr = jnp.where(sel, pw[e][0], pcr)
        pci = jnp.where(sel, pw[e][1], pci)
    tab_ref[0, :, :H] = pcr
    tab_ref[0, :, H:] = pci
    for t, k in enumerate((1, 2, 4)):
        keep = (row < 8 - k) if reverse else (row >= k)
        tab_ref[1 + t, :, :H] = jnp.where(keep, pw[k - 1][0], 0.0)
        tab_ref[1 + t, :, H:] = jnp.where(keep, pw[k - 1][1], 0.0)


def _scan_group(vr, vi, cr, ci, tab_ref, reverse):
    H = SSM_H
    for t, k in enumerate((1, 2, 4)):
        sh = 8 - k if reverse else k
        vr, vi = _cmul_add(vr, vi, tab_ref[1 + t, :, :H], tab_ref[1 + t, :, H:],
                           pltpu.roll(vr, sh, 0), pltpu.roll(vi, sh, 0))
    return _cmul_add(vr, vi, tab_ref[0, :, :H], tab_ref[0, :, H:], cr, ci)


def _scan_fwd(proj, u_blk, bcat, ccat, abar, n_rows):
    H = SSM_H
    nt = n_rows // SCAN_TR

    def body(u_ref, b_ref, c_ref, a_ref, xs_ref, xp_ref, yc_ref, bu_ref, tab_ref, carry_ref):
        @pl.when(pl.program_id(0) == 0)
        def _():
            _scan_tables(a_ref, tab_ref, False)
            carry_ref[...] = jnp.zeros(carry_ref.shape, F32)

        bu_ref[...] = jnp.dot(u_ref[...].astype(BF16), b_ref[...], preferred_element_type=F32)
        row = lax.broadcasted_iota(jnp.int32, (8, H), 0)

        def group(j, carry):
            cr, ci = carry
            r0 = pl.multiple_of(j * 16, 16)
            xr, xi = [], []
            for half in range(2):
                rr = pl.multiple_of(r0 + 8 * half, 8)
                vr, vi = _scan_group(bu_ref[pl.ds(rr, 8), :H], bu_ref[pl.ds(rr, 8), H:],
                                     cr, ci, tab_ref, False)
                xp_ref[pl.ds(rr, 8), :H] = jnp.where(row == 0, cr, pltpu.roll(vr, 1, 0))
                xp_ref[pl.ds(rr, 8), H:] = jnp.where(row == 0, ci, pltpu.roll(vi, 1, 0))
                cr, ci = _bcast_row(vr, row, 7), _bcast_row(vi, row, 7)
                xr.append(vr)
                xi.append(vi)
            xs_ref[pl.ds(r0, 16), :H] = jnp.concatenate(xr, axis=0).astype(BF16)
            xs_ref[pl.ds(r0, 16), H:] = jnp.concatenate(xi, axis=0).astype(BF16)
            return cr, ci

        cr, ci = lax.fori_loop(0, SCAN_TR // 16, group,
                               (carry_ref[:, :H], carry_ref[:, H:]))
        carry_ref[:, :H] = cr
        carry_ref[:, H:] = ci
        yc_ref[...] = lax.dot_general(xs_ref[...], c_ref[...], (((1,), (1,)), ((), ())),
                                      preferred_element_type=F32)

    tile = lambda w: pl.BlockSpec((SCAN_TR, w), lambda i: (i, 0))
    whole = lambda a: pl.BlockSpec(a.shape, lambda i: (0, 0))
    return pl.pallas_call(
        body,
        grid=(nt,),
        in_specs=[pl.BlockSpec((SCAN_TR, SSM_W), lambda i: (i, u_blk)),
                  whole(bcat), whole(ccat), whole(abar)],
        out_specs=[tile(2 * H), tile(2 * H), tile(SSM_W)],
        out_shape=[jax.ShapeDtypeStruct((n_rows, 2 * H), BF16),
                   jax.ShapeDtypeStruct((n_rows, 2 * H), F32),
                   jax.ShapeDtypeStruct((n_rows, SSM_W), F32)],
        scratch_shapes=[pltpu.VMEM((SCAN_TR, 2 * H), F32), pltpu.VMEM((4, 8, 2 * H), F32),
                        pltpu.VMEM((8, 2 * H), F32)],
        compiler_params=_cparams(("arbitrary",)),
        name="ssm_scan_fwd",
    )(proj, bcat, ccat, abar)


def _scan_bwd(dy, xprev, bcat, ccat, abar, n_rows):
    H = SSM_H
    nt = n_rows // SCAN_TR

    def body(dy_ref, xp_ref, b_ref, c_ref, a_ref, h_ref, da_ref, du_ref, g_ref, tab_ref, carry_ref):
        @pl.when(pl.program_id(0) == 0)
        def _():
            _scan_tables(a_ref, tab_ref, True)
            carry_ref[...] = jnp.zeros(carry_ref.shape, F32)
            da_ref[...] = jnp.zeros(da_ref.shape, F32)

        g_ref[...] = jnp.dot(dy_ref[...], c_ref[...], preferred_element_type=F32)
        row = lax.broadcasted_iota(jnp.int32, (8, H), 0)
        n16 = SCAN_TR // 16

        def group(jj, carry):
            cr, ci = carry
            r0 = pl.multiple_of((n16 - 1 - jj) * 16, 16)
            hr, hi = [None, None], [None, None]
            for half in (1, 0):
                rr = pl.multiple_of(r0 + 8 * half, 8)
                vr, vi = _scan_group(g_ref[pl.ds(rr, 8), :H], g_ref[pl.ds(rr, 8), H:],
                                     cr, ci, tab_ref, True)
                pr, pi = xp_ref[pl.ds(rr, 8), :H], xp_ref[pl.ds(rr, 8), H:]
                da_ref[:, :H] += vr * pr + vi * pi
                da_ref[:, H:] += vi * pr - vr * pi
                cr, ci = _bcast_row(vr, row, 0), _bcast_row(vi, row, 0)
                hr[half], hi[half] = vr, vi
            h_ref[pl.ds(r0, 16), :H] = jnp.concatenate(hr, axis=0).astype(BF16)
            h_ref[pl.ds(r0, 16), H:] = jnp.concatenate(hi, axis=0).astype(BF16)
            return cr, ci

        cr, ci = lax.fori_loop(0, n16, group, (carry_ref[:, :H], carry_ref[:, H:]))
        carry_ref[:, :H] = cr
        carry_ref[:, H:] = ci
        du_ref[...] = lax.dot_general(h_ref[...], b_ref[...], (((1,), (1,)), ((), ())),
                                      preferred_element_type=F32)

    rev = lambda i: (nt - 1 - i, 0)
    whole = lambda a: pl.BlockSpec(a.shape, lambda i: (0, 0))
    return pl.pallas_call(
        body,
        grid=(nt,),
        in_specs=[pl.BlockSpec((SCAN_TR, SSM_W), rev),
                  pl.BlockSpec((SCAN_TR, 2 * H), rev),
                  whole(bcat), whole(ccat), whole(abar)],
        out_specs=[pl.BlockSpec((SCAN_TR, 2 * H), rev),
                   pl.BlockSpec((8, 2 * H), lambda i: (0, 0)),
                   pl.BlockSpec((SCAN_TR, SSM_W), rev)],
        out_shape=[jax.ShapeDtypeStruct((n_rows, 2 * H), BF16),
                   jax.ShapeDtypeStruct((8, 2 * H), F32),
                   jax.ShapeDtypeStruct((n_rows, SSM_W), F32)],
        scratch_shapes=[pltpu.VMEM((SCAN_TR, 2 * H), F32), pltpu.VMEM((4, 8, 2 * H), F32),
                        pltpu.VMEM((8, 2 * H), F32)],
        compiler_params=_cparams(("arbitrary",)),
        name="ssm_scan_bwd",
    )(dy, xprev, bcat, ccat, abar)


def _adamw(parts, w, m, v, *, tr, ch, name, prefetch=None):
    n_rows, cols = w.shape
    n_parts = len(parts)
    c1 = 1.0 - ADAM_B1 ** ADAM_STEP
    c2 = 1.0 - ADAM_B2 ** ADAM_STEP

    def fn(rv, vv, i, nt):
        g = rv[0].astype(F32)
        for p in rv[1:n_parts]:
            g = g + p.astype(F32)
        wv, mv, vval = rv[n_parts:]
        nm = ADAM_B1 * mv + (1.0 - ADAM_B1) * g
        nv = ADAM_B2 * vval + (1.0 - ADAM_B2) * (g * g)
        delta = -ADAM_LR * ((nm / c1) / (jnp.sqrt(nv / c2) + ADAM_EPS) + ADAM_WD * wv)
        return [g, delta, nm, nv], []

    rows = [_row(arr, lead=lead) for (arr, lead) in parts] + [_row(w), _row(m), _row(v)]
    return _rowwise(fn, rows, [], [(cols, F32)] * 4, [], n_rows=n_rows, tr=tr, ch=ch, name=name,
                    prefetch=prefetch)


_PACK = [
    ("b_ada", 6), ("norm1_g", 1), ("b_in", 3), ("norm2_g", 1), ("final_g", 1),
    ("lambda_re", 1), ("lambda_im", 1), ("log_step", 1), ("attn_sinks", 1),
    ("rel_bias", 1), ("b_glu", 1), ("ssm_d", 1), ("loss", 1),
    ("ssm_b_re", 16), ("ssm_b_im", 16), ("ssm_c_re", 16), ("ssm_c_im", 16),
]
_PACK_OFF = {}
_off = 0
for _n, _r in _PACK:
    _PACK_OFF[_n] = (_off, _r)
    _off += _r
PACK_ROWS = -(-_off // 8) * 8


def _to_rows(a, rows):
    flat = a.reshape(-1).astype(F32)
    pad = rows * PACK_W - flat.shape[0]
    if pad:
        flat = jnp.pad(flat, (0, pad))
    return flat.reshape(rows, PACK_W)


def _b_to_rows(b):
    return jnp.transpose(b, (2, 0, 1)).reshape(SSM_P, SSM_H)


def _rows_to_b(r):
    return jnp.transpose(r.reshape(SSM_P, SSM_G, SSM_N), (1, 2, 0))


def _c_to_rows(cm):
    return jnp.transpose(cm, (1, 0, 2)).reshape(SSM_P, SSM_H)


def _rows_to_c(r):
    return jnp.transpose(r.reshape(SSM_P, SSM_G, SSM_N), (1, 0, 2))


def _pack(vals):
    out = jnp.zeros((PACK_ROWS, PACK_W), F32)
    for n, r in _PACK:
        if n in vals:
            pieces = vals[n] if isinstance(vals[n], list) else [vals[n]]
            rows_each = r // len(pieces)
            for i, piece in enumerate(pieces):
                out = lax.dynamic_update_slice(out, _to_rows(piece, rows_each),
                                               (_PACK_OFF[n][0] + i * rows_each, 0))
    return out


def _unpack(packed, name, shape):
    o, r = _PACK_OFF[name]
    n = int(np.prod(shape))
    return packed[o:o + r].reshape(-1)[:n].reshape(shape)


def _small_params_packed(p):
    return {
        "b_ada": p["b_ada"], "norm1_g": p["norm1_g"], "b_in": p["b_in"],
        "norm2_g": p["norm2_g"], "final_g": p["final_g"],
        "lambda_re": p["lambda_re"], "lambda_im": p["lambda_im"],
        "log_step": p["log_step"], "attn_sinks": p["attn_sinks"],
        "rel_bias": p["rel_bias"], "b_glu": p["b_glu"], "ssm_d": p["ssm_d"],
        "ssm_b_re": _b_to_rows(p["ssm_b_re"][0]), "ssm_b_im": _b_to_rows(p["ssm_b_im"][0]),
        "ssm_c_re": _c_to_rows(p["ssm_c_re"][0]), "ssm_c_im": _c_to_rows(p["ssm_c_im"][0]),
    }


_SMALL_SHAPES = {
    "b_ada": (1, N_MOD * D), "norm1_g": (1, D), "b_in": (1, IN_W), "norm2_g": (1, D),
    "final_g": (D,), "lambda_re": (1, SSM_G, SSM_N), "lambda_im": (1, SSM_G, SSM_N),
    "log_step": (1, SSM_G), "attn_sinks": (1, N_Q_HEADS), "rel_bias": (NUM_BUCKETS, N_Q_HEADS),
    "b_glu": (1, SSM_W), "ssm_d": (1, SSM_W),
}


def _unpack_small(packed, name):
    if name in ("ssm_b_re", "ssm_b_im"):
        o, r = _PACK_OFF[name]
        return _rows_to_b(packed[o:o + r])[None]
    if name in ("ssm_c_re", "ssm_c_im"):
        o, r = _PACK_OFF[name]
        return _rows_to_c(packed[o:o + r])[None]
    return _unpack(packed, name, _SMALL_SHAPES[name])


WEIGHT_ORDER = ['w_ada', 'b_ada', 'norm1_g', 'w_in', 'b_in', 'attn_sinks', 'rel_bias', 'lambda_re',
                'lambda_im', 'log_step', 'ssm_b_re', 'ssm_b_im', 'ssm_c_re', 'ssm_c_im', 'ssm_d',
                'w_glu', 'b_glu', 'w_attn_proj', 'w_ssm_proj', 'w_out', 'norm2_g', 'w_ff1', 'w_ff2',
                'final_g']
BIG = ['w_in', 'w_glu', 'w_attn_proj', 'w_ssm_proj', 'w_out', 'w_ff1', 'w_ff2']


ADAMW_TILE_ELEMS = 1 << 18


def _to_col_blocks(w):
    k, n = w.shape
    return jnp.transpose(w.reshape(k, N_DEV, n // N_DEV), (1, 0, 2))


def _adamw_rows(rows, cols):
    tr = rows
    while tr * cols > ADAMW_TILE_ELEMS and tr % 32 == 0:
        tr //= 2
    return tr


def _cast_to_slot(w, me1, name):
    rows, cols = w.shape
    tr = min(rows, 256)

    def body(me_ref, w_ref, o_ref):
        o_ref[...] = w_ref[...].astype(BF16)

    return pl.pallas_call(
        body,
        grid_spec=pltpu.PrefetchScalarGridSpec(
            num_scalar_prefetch=1, grid=(rows // tr,),
            in_specs=[pl.BlockSpec((tr, cols), lambda i, me_ref: (i, 0))],
            out_specs=pl.BlockSpec((None, tr, cols), lambda i, me_ref: (me_ref[0], i, 0))),
        out_shape=jax.ShapeDtypeStruct((N_DEV, rows, cols), BF16),
        compiler_params=_cparams(("arbitrary",)),
        name=name,
    )(me1, w)


def kernel(x, c, w_ada, b_ada, norm1_g, w_in, b_in, attn_sinks, rel_bias, lambda_re, lambda_im, log_step, ssm_b_re, ssm_b_im, ssm_c_re, ssm_c_im, ssm_d, w_glu, b_glu, w_attn_proj, w_ssm_proj, w_out, norm2_g, w_ff1, w_ff2, final_g, loss_target, m_w_ada, m_b_ada, m_norm1_g, m_w_in, m_b_in, m_attn_sinks, m_rel_bias, m_lambda_re, m_lambda_im, m_log_step, m_ssm_b_re, m_ssm_b_im, m_ssm_c_re, m_ssm_c_im, m_ssm_d, m_w_glu, m_b_glu, m_w_attn_proj, m_w_ssm_proj, m_w_out, m_norm2_g, m_w_ff1, m_w_ff2, m_final_g, v_w_ada, v_b_ada, v_norm1_g, v_w_in, v_b_in, v_attn_sinks, v_rel_bias, v_lambda_re, v_lambda_im, v_log_step, v_ssm_b_re, v_ssm_b_im, v_ssm_c_re, v_ssm_c_im, v_ssm_d, v_w_glu, v_b_glu, v_w_attn_proj, v_w_ssm_proj, v_w_out, v_norm2_g, v_w_ff1, v_w_ff2, v_final_g):
    loc = dict(locals())
    W = {n: loc[n] for n in WEIGHT_ORDER}
    Mo = {n: loc["m_" + n] for n in WEIGHT_ORDER}
    Vo = {n: loc["v_" + n] for n in WEIGHT_ORDER}
    S = x.shape[1]
    TM = min(512, S)
    TS = min(1024, S)
    TR = min(256, S)
    TW = min(1024, S)
    me = 4 * lax.axis_index("x") + 2 * lax.axis_index("y") + lax.axis_index("c")
    x2d = x.reshape(S, D)
    tgt = loss_target.reshape(S, D)

    c_all = _small_allgather(c, "allgather_c").reshape(N_DEV, D)
    cs = _rowwise(lambda rv, vv, i, nt: ([rv[0] * _sigmoid(rv[0])], []), [_row(c_all)], [],
                  [(D, F32)], [], n_rows=N_DEV, tr=8, ch=8, name="silu_c")[0]
    n_ada = N_MOD * D // N_DEV
    b_ada_cols = lax.dynamic_slice(b_ada, (0, me * n_ada), (1, n_ada))
    mod_piece = _matmul(cs, w_ada[0], mode="nn", dims=(N_DEV, n_ada, D), tiles=(N_DEV, 512, D),
                        out_dtypes=[F32], name="ada_fwd", bias=b_ada_cols)
    mod_all = _small_allgather(mod_piece, "allgather_mod")
    mod_b = lax.dynamic_index_in_dim(mod_all, me, axis=1, keepdims=False).reshape(N_MOD, D)
    sh1, sc1, g1, sh2, sc2, g2 = [mod_b[i:i + 1] for i in range(N_MOD)]

    shard = {n: W[n][0] for n in BIG}
    me1 = jnp.reshape(me, (1,)).astype(jnp.int32)
    zone = {n: _cast_to_slot(shard[n], me1, "cast_" + n) for n in BIG}
    (in_flight,), tok_in = _relay_gather_start([zone["w_in"]], "w_in_start", mod_all)
    G = {}

    def f_norm1(rv, vv, i, nt):
        xv, (g, sc, sh) = rv[0], vv
        return [(xv * _rms(xv) * g) * (1.0 + sc) + sh], []

    h = _rowwise(f_norm1, [_row(x2d)], [norm1_g, sc1, sh1], [(D, BF16)], [],
                 n_rows=S, tr=TR, ch=32, name="norm1_fwd", dep=tok_in)[0]
    (zone_in,) = _relay_gather_arrive([in_flight], h, "w_in_arrive")
    (in_pass,), tok_p = _relay_pass_start([zone_in], "w_in_pass_start")
    mixer = ["w_attn_proj", "w_glu", "w_ssm_proj", "w_out"]
    flights, tok_w = _exchange_start([zone[n] for n in mixer], "gather", "weights_start", tok_p)
    w_flight = dict(zip(mixer, flights))
    ff_flights, tok_w = _relay_gather_start([zone["w_ff1"], zone["w_ff2"]], "ff_weights_start", tok_w)
    (G["w_in"],) = _relay_pass_wait([in_pass], tok_w, "w_in_pass_wait")
    proj = _matmul(h, G["w_in"], mode="nn", dims=(S, IN_W, D), tiles=(TM, 768, D),
                   out_dtypes=[BF16], name="in_proj", b3=True, bias=b_in, dep=tok_w)

    buckets = _t5_buckets_block()
    band = _band_mask()
    onehot_t = jnp.asarray(
        (np.arange(128)[:, None] == buckets.reshape(-1)[None, :]).astype(np.float32), BF16)
    band_first = band & (np.arange(2 * BLK)[None, :] >= BLK)
    rel_bias_t = jnp.pad(jnp.transpose(rel_bias), ((0, 0), (0, 128 - NUM_BUCKETS)))
    bias2 = _bias_tables(rel_bias_t, onehot_t,
                         jnp.asarray(band_first.reshape(1, -1).astype(np.float32)),
                         jnp.asarray(band.reshape(1, -1).astype(np.float32))
                         ).reshape(2, N_Q_HEADS * BLK, 2 * BLK)
    sinkcol = jnp.repeat(attn_sinks.reshape(N_Q_HEADS), BLK).reshape(N_Q_HEADS * BLK, 1)
    attn = _attention_fwd(proj, bias2, sinkcol, S)
    landed = _exchange_wait([w_flight[n] for n in mixer], "gather", attn, "weights_wait_mixer")
    G.update((n, bufs[0]) for n, bufs in zip(mixer, landed))
    w_glu_f = G["w_glu"].reshape(SSM_W, SSM_W)
    w_out_f = G["w_out"].reshape(D, D)
    w_ap_f = jnp.transpose(G["w_attn_proj"], (1, 0, 2)).reshape(ATTN_W, D)
    w_sp_f = jnp.transpose(G["w_ssm_proj"], (1, 0, 2)).reshape(SSM_W, D)
    y_attn = _matmul(attn, w_ap_f, mode="nn", dims=(S, D, ATTN_W), tiles=(TW, 1024, ATTN_W),
                     out_dtypes=[BF16], name="attn_proj")

    lam_re = lambda_re.reshape(1, SSM_H)
    lam_im = lambda_im.reshape(1, SSM_H)
    ls_x = jnp.repeat(log_step.reshape(SSM_G), SSM_N).reshape(1, SSM_H)
    btr, bti = _b_to_rows(ssm_b_re[0]), _b_to_rows(ssm_b_im[0])
    ctr, cti = _c_to_rows(ssm_c_re[0]), _c_to_rows(ssm_c_im[0])
    abar, bcat, ccat = _ssm_setup(lam_re, lam_im, ls_x, btr, bti, ctr, cti)
    u_blk = (ATTN_W + 2 * KV_W) // SSM_W
    xs, xprev, yc = _scan_fwd(proj, u_blk, bcat, ccat, abar, S)

    def f_ssm_out(rv, vv, i, nt):
        y = rv[0] + vv[0] * rv[1]
        return [y, _gelu(y)], []

    y_ssm_pre, z = _rowwise(f_ssm_out, [_row(yc), _row(proj, u_blk, SSM_W)], [ssm_d],
                            [(SSM_W, F32), (SSM_W, BF16)], [], n_rows=S, tr=TM, ch=32, name="ssm_out")
    zg = _matmul(z, w_glu_f, mode="nn", dims=(S, SSM_W, SSM_W), tiles=(TM, SSM_W, SSM_W),
                 out_dtypes=[F32], name="glu_proj", bias=b_glu)
    z2 = _rowwise(lambda rv, vv, i, nt: ([rv[0].astype(F32) * _sigmoid(rv[1])], []),
                  [_row(z), _row(zg)], [], [(SSM_W, BF16)], [], n_rows=S, tr=TM, ch=32, name="glu_gate")[0]
    y_ssm = _matmul(z2, w_sp_f, mode="nn", dims=(S, D, SSM_W), tiles=(TW, 1024, SSM_W),
                    out_dtypes=[BF16], name="ssm_proj")

    ga_row = _row(proj, 1, D)
    gs_row = _row(proj, 2, D)

    def f_merge(rv, vv, i, nt):
        ga, gs, ya, ys = rv
        return [_sigmoid(ga) * ya + _sigmoid(gs) * ys], []

    merged = _rowwise(f_merge, [ga_row, gs_row, _row(y_attn), _row(y_ssm)], [], [(D, BF16)], [],
                      n_rows=S, tr=TR, ch=32, name="merge")[0]
    mo = _matmul(merged, w_out_f, mode="nn", dims=(S, D, D), tiles=(TW, 1024, D),
                 out_dtypes=[BF16], name="out_proj")

    ff_zones = _relay_gather_arrive(ff_flights, mo, "ff_weights_arrive")
    ff_pass, tok_fp = _relay_pass_start(ff_zones, "ff_weights_pass_start")

    def f_norm2(rv, vv, i, nt):
        xv, mv = rv
        g1v, g, sc, sh = vv
        x1v = xv + g1v * mv
        return [x1v, (x1v * _rms(x1v) * g) * (1.0 + sc) + sh], []

    x1, h2 = _rowwise(f_norm2, [_row(x2d), _row(mo)], [g1, norm2_g, sc2, sh2],
                      [(D, F32), (D, BF16)], [], n_rows=S, tr=TR, ch=32, name="norm2_fwd", dep=tok_fp)

    def relu_sq(acc):
        r = jnp.maximum(acc, 0.0)
        return r * r, r

    (G["w_ff1"],) = _relay_pass_wait(ff_pass[:1], h2, "w_ff1_pass_wait")
    act, relu = _matmul(h2, G["w_ff1"], mode="nn", dims=(S, D_FF, D), tiles=(TM, 1024, D),
                        out_dtypes=[BF16, BF16], name="ff1", b3=True, epilogue=relu_sq)
    w_ff2_f = _relay_pass_wait(ff_pass[1:], act, "w_ff2_pass_wait")[0].reshape(D_FF, D)
    ff = _matmul(act, w_ff2_f, mode="nn", dims=(S, D, D_FF), tiles=(TM, 1024, 2048),
                 out_dtypes=[BF16], name="ff2")

    def f_loss(rv, vv, i, nt):
        x1v, ffv, tv = rv
        g2v, gf = vv
        x2v = x1v + g2v * ffv
        r = _rms(x2v)
        xh = x2v * r
        diff = xh * gf - tv
        dy = diff * (1.0 / D)
        dxh = dy * gf
        dx2 = r * (dxh - xh * jnp.mean(dxh * xh, axis=-1, keepdims=True))
        return [dx2, dx2 * g2v], [_colsum(0.5 * diff * diff * (1.0 / D)), _colsum(dy * xh),
                                  _colsum(dx2 * ffv)]

    dx2, dff, loss_cols, d_final_g, dg2 = _rowwise(
        f_loss, [_row(x1), _row(ff), _row(tgt)], [g2, final_g.reshape(1, D)],
        [(D, F32), (D, BF16)], [(1, D)] * 3, n_rows=S, tr=TR, ch=32, name="loss_bwd")

    df1 = _matmul(dff, w_ff2_f, mode="nt", dims=(S, D_FF, D), tiles=(TM, 1024, D),
                  out_dtypes=[BF16], name="ff2_dgrad", extras=(relu,),
                  epilogue=lambda acc, r: (acc * (2.0 * r.astype(F32)),))
    gw_ff2 = _matmul(act, dff, mode="tn", dims=(D_FF, D, S), tiles=(1024, 1024, TS),
                     out_dtypes=[BF16], name="ff2_wgrad").reshape(N_DEV, D_FF // N_DEV, D)
    g_flight = {}
    (g_flight["w_ff2"],), tok = _exchange_start([gw_ff2], "scatter", "grads_start_ff2")
    dh2 = _matmul(df1, G["w_ff1"], mode="nt", dims=(S, D, D_FF), tiles=(TM, D, 1024),
                  out_dtypes=[BF16], name="ff1_dgrad", b3=True, dep=tok)
    gw_ff1 = _matmul(h2, df1, mode="tn", dims=(D, D_FF, S), tiles=(1024, 1024, TS),
                     out_dtypes=[BF16], name="ff1_wgrad", out3=True)
    (g_flight["w_ff1"],), tok = _exchange_start([gw_ff1], "scatter", "grads_start_ff1")

    def f_norm2_bwd(rv, vv, i, nt):
        x1v, dh, dx2v, mv = rv
        g, sc, g1v = vv
        r = _rms(x1v)
        xh = x1v * r
        t = xh * g
        dt = dh * (1.0 + sc)
        dxh = dt * g
        dx1 = dx2v + r * (dxh - xh * jnp.mean(dxh * xh, axis=-1, keepdims=True))
        return [dx1, dx1 * g1v], [_colsum(dh), _colsum(dh * t), _colsum(dt * xh), _colsum(dx1 * mv)]

    dx1, dmo, dsh2, dsc2, d_norm2_g, dg1 = _rowwise(
        f_norm2_bwd, [_row(x1), _row(dh2), _row(dx2), _row(mo)], [norm2_g, sc2, g1],
        [(D, F32), (D, BF16)], [(1, D)] * 4, n_rows=S, tr=TR, ch=16, name="norm2_bwd", dep=tok)

    dmerged = _matmul(dmo, w_out_f, mode="nt", dims=(S, D, D), tiles=(TW, 1024, D),
                      out_dtypes=[BF16], name="out_dgrad")
    gw_out = _matmul(merged, dmo, mode="tn", dims=(D, D, S), tiles=(1024, 1024, TS),
                     out_dtypes=[BF16], name="out_wgrad").reshape(N_DEV, D // N_DEV, D)
    (g_flight["w_out"],), tok = _exchange_start([gw_out], "scatter", "grads_start_out")

    def f_merge_bwd(rv, vv, i, nt):
        dm, ga, gs, ya, ys = rv
        sa, ss = _sigmoid(ga), _sigmoid(gs)
        return [dm * sa, dm * ss, dm * ya * sa * (1.0 - sa), dm * ys * ss * (1.0 - ss)], []

    dy_attn, dy_ssm, dga, dgs = _rowwise(
        f_merge_bwd, [_row(dmerged), ga_row, gs_row, _row(y_attn), _row(y_ssm)], [],
        [(D, BF16)] * 4, [], n_rows=S, tr=TR, ch=16, name="merge_bwd", dep=tok)

    dz2 = _matmul(dy_ssm, w_sp_f, mode="nt", dims=(S, SSM_W, D), tiles=(TW, SSM_W, D),
                  out_dtypes=[F32], name="ssm_proj_dgrad")
    gw_ssm_proj = _to_col_blocks(_matmul(z2, dy_ssm, mode="tn", dims=(SSM_W, D, S), tiles=(SSM_W, 1024, TS),
                                         out_dtypes=[BF16], name="ssm_proj_wgrad"))

    def f_glu_bwd(rv, vv, i, nt):
        dz2v, zv, zgv = rv
        sg = _sigmoid(zgv)
        dzg = dz2v * zv.astype(F32) * sg * (1.0 - sg)
        return [dzg, dz2v * sg], [_colsum(dzg)]

    dzg, dz_a, d_b_glu = _rowwise(f_glu_bwd, [_row(dz2), _row(z), _row(zg)], [],
                                  [(SSM_W, BF16), (SSM_W, F32)], [(1, SSM_W)],
                                  n_rows=S, tr=TM, ch=32, name="glu_bwd")
    dz_b = _matmul(dzg, w_glu_f, mode="nt", dims=(S, SSM_W, SSM_W), tiles=(TM, SSM_W, SSM_W),
                   out_dtypes=[F32], name="glu_dgrad")
    gw_glu = _matmul(z, dzg, mode="tn", dims=(SSM_W, SSM_W, S), tiles=(SSM_W, SSM_W, TS),
                     out_dtypes=[BF16], name="glu_wgrad").reshape(N_DEV, SSM_W // N_DEV, SSM_W)
    (g_flight["w_ssm_proj"], g_flight["w_glu"]), tok = _exchange_start(
        [gw_ssm_proj, gw_glu], "scatter", "grads_start_ssm")

    def f_ssm_out_bwd(rv, vv, i, nt):
        dza, dzb, yv, uv = rv
        dy = (dza + dzb) * _gelu_grad(yv)
        return [dy, dy * vv[0]], [_colsum(dy * uv)]

    dy_s, du_a, d_ssm_d = _rowwise(
        f_ssm_out_bwd, [_row(dz_a), _row(dz_b), _row(y_ssm_pre), _row(proj, u_blk, SSM_W)], [ssm_d],
        [(SSM_W, BF16), (SSM_W, F32)], [(1, SSM_W)], n_rows=S, tr=TM, ch=32, name="ssm_out_bwd", dep=tok)
    hw = SSM_W // 2
    u_half = (ATTN_W + 2 * KV_W) // hw
    dccat = _matmul(dy_s, xs, mode="tn", dims=(hw, 2 * SSM_H, S), tiles=(hw, 1024, TS),
                    out_dtypes=[F32], name="ssm_c_wgrad", a_index=lambda i, j, k: (k, j % 2))
    hs, dacc, du_b = _scan_bwd(dy_s, xprev, bcat, ccat, abar, S)
    dbcat = _matmul(proj, hs, mode="tn", dims=(hw, 2 * SSM_H, S), tiles=(hw, 1024, TS),
                    out_dtypes=[F32], name="ssm_b_wgrad", a_index=lambda i, j, k: (k, u_half + j % 2))
    grp = np.arange(SSM_H) // SSM_N
    gind = jnp.asarray((grp[:, None] == np.arange(128)[None, :]).astype(np.float32), BF16)
    d_lam_re, d_lam_im, d_ls, d_btr, d_bti, d_ctr, d_cti = _ssm_param_bwd(
        lam_re, lam_im, ls_x, btr, bti, dacc, dbcat, dccat, gind)

    dattn = _matmul(dy_attn, w_ap_f, mode="nt", dims=(S, ATTN_W, D), tiles=(TW, ATTN_W, D),
                    out_dtypes=[BF16], name="attn_proj_dgrad")
    gw_attn_proj = _to_col_blocks(_matmul(attn, dy_attn, mode="tn", dims=(ATTN_W, D, S), tiles=(ATTN_W, 1024, TS),
                                          out_dtypes=[BF16], name="attn_proj_wgrad"))
    (g_flight["w_attn_proj"],), tok = _exchange_start(
        [gw_attn_proj], "scatter", "grads_start_attn")
    dq, dkc, dkp, dvc, dvp, dbias, dsink = _attention_bwd(proj, attn, dattn, bias2, sinkcol, S)
    d_bias_b, d_sinks = _bucket_reduce(dbias.reshape(N_Q_HEADS, BLK * 2 * BLK),
                                       dsink.reshape(N_Q_HEADS, BLK), onehot_t)

    def f_dproj(rv, vv, i, nt):
        dqv, kc, kp, vc, vp, dua, dub, gav, gsv = rv
        keep = (i < nt - 1).astype(F32)
        dp = jnp.concatenate([dqv.astype(F32), kc + keep * kp, vc + keep * vp, dua + dub,
                              gav.astype(F32), gsv.astype(F32)], axis=-1)
        return [dp], [_colsum(dp)]

    dproj, d_b_in = _rowwise(
        f_dproj, [_row(dq), _row(dkc), _row(dkp, shift=1), _row(dvc), _row(dvp, shift=1),
                  _row(du_a), _row(du_b), _row(dga), _row(dgs)], [],
        [(IN_W, BF16)], [(1, IN_W)], n_rows=S, tr=BLK, ch=16, name="dproj", dep=tok)
    gw_in = _matmul(h, dproj, mode="tn", dims=(D, IN_W, S), tiles=(1024, 768, TS),
                    out_dtypes=[BF16], name="in_wgrad", out3=True)
    (g_flight["w_in"],), tok = _exchange_start([gw_in], "scatter", "grads_start_in")
    dh = _matmul(dproj, G["w_in"], mode="nt", dims=(S, D, IN_W), tiles=(TM, D, 768),
                 out_dtypes=[BF16], name="in_dgrad", b3=True, dep=tok)

    def f_norm1_bwd(rv, vv, i, nt):
        xv, dhv, dx1v = rv
        g, sc = vv
        r = _rms(xv)
        xh = xv * r
        t = xh * g
        dt = dhv * (1.0 + sc)
        dxh = dt * g
        dxv = dx1v + r * (dxh - xh * jnp.mean(dxh * xh, axis=-1, keepdims=True))
        return [dxv], [_colsum(dhv), _colsum(dhv * t), _colsum(dt * xh)]

    grad_x, dsh1, dsc1, d_norm1_g = _rowwise(
        f_norm1_bwd, [_row(x2d), _row(dh), _row(dx1)], [norm1_g, sc1],
        [(D, F32)], [(1, D)] * 3, n_rows=S, tr=TR, ch=32, name="norm1_bwd")

    part = _pack({
        "b_ada": [dsh1, dsc1, dg1, dsh2, dsc2, dg2], "norm1_g": d_norm1_g, "b_in": d_b_in, "norm2_g": d_norm2_g,
        "final_g": d_final_g, "lambda_re": d_lam_re, "lambda_im": d_lam_im,
        "log_step": d_ls[0, :SSM_G], "attn_sinks": d_sinks[:, 0],
        "rel_bias": jnp.transpose(d_bias_b[:, :NUM_BUCKETS]), "b_glu": d_b_glu, "ssm_d": d_ssm_d,
        "loss": loss_cols, "ssm_b_re": d_btr, "ssm_b_im": d_bti, "ssm_c_re": d_ctr, "ssm_c_im": d_cti,
    })
    zone_small = lax.dynamic_update_slice(lax.empty((N_DEV, PACK_ROWS, PACK_W), F32), part[None], (me, 0, 0))
    (small_flight,), after = _exchange_start([zone_small], "gather", "small_grads_start")

    big_out = {}
    for n in ["w_ff2", "w_ff1", "w_out", "w_ssm_proj", "w_glu", "w_attn_proj", "w_in"]:
        own, recv = _exchange_wait([g_flight[n]], "scatter", after, "grads_wait_" + n[2:])[0]
        rows, cols = shard[n].shape
        parts = [(own, lambda m: m[0])] + [
            (recv, lambda m, j=j: jnp.where(j >= m[0], j + 1, j)) for j in range(N_DEV - 1)]
        big_out[n] = _adamw(parts, shard[n], Mo[n][0], Vo[n][0], tr=_adamw_rows(rows, cols), ch=16,
                            name="adamw_" + n, prefetch=me1)
        after = big_out[n][0]

    part_all = _exchange_wait([small_flight], "gather", after, "small_grads_wait")[0][0]
    wp, mp, vp = [_pack(_small_params_packed(p)) for p in (W, Mo, Vo)]
    sg, sdelta, sm, sv = _adamw([(part_all, d) for d in range(N_DEV)], wp, mp, vp,
                                tr=PACK_ROWS, ch=8, name="adamw_small")
    lo, _ = _PACK_OFF["loss"]
    loss = jnp.sum(sg[lo])

    o_ada, _ = _PACK_OFF["b_ada"]
    dmod_all = part_all[:, o_ada:o_ada + N_MOD, :].reshape(N_DEV, N_MOD * D)
    dmod_cols = lax.dynamic_slice(dmod_all, (0, me * n_ada), (N_DEV, n_ada))
    gw_ada = _matmul(cs, dmod_cols, mode="tn", dims=(D, n_ada, N_DEV), tiles=(D, 512, N_DEV),
                     out_dtypes=[F32], name="ada_wgrad")
    big_out["w_ada"] = _adamw([(gw_ada, 0)], w_ada[0], m_w_ada[0], v_w_ada[0],
                              tr=_adamw_rows(D, n_ada), ch=16, name="adamw_w_ada")

    def leaf(kind, n):
        if n in big_out:
            return big_out[n][kind][None]
        return _unpack_small((sg, sdelta, sm, sv)[kind], n)

    outs = [loss, grad_x.reshape(1, S, D)]
    for kind in range(4):
        outs.extend(leaf(kind, n) for n in WEIGHT_ORDER)
    return tuple(outs)
```

```python
import functools
import math

import numpy as np
import jax
import jax.numpy as jnp
from jax import lax
from jax.experimental import pallas as pl
from jax.experimental.pallas import tpu as pltpu

F32 = jnp.float32
BF16 = jnp.bfloat16
MESH = pl.DeviceIdType.MESH

N_DEV = 8
D = 2048
HEAD_DIM = 64
N_Q_HEADS = 16
N_KV_HEADS = 4
GROUP = N_Q_HEADS // N_KV_HEADS
ATTN_W = N_Q_HEADS * HEAD_DIM
KV_W = N_KV_HEADS * HEAD_DIM
BLK = 128
NUM_BUCKETS = 32
MAX_DISTANCE = 128
NEG_INF = -1e30
SSM_W = 512
SSM_P = 16
SSM_G = 32
SSM_N = 64
SSM_H = SSM_G * SSM_N
D_FF = 4 * D
IN_W = ATTN_W + 2 * KV_W + SSM_W + 2 * D
N_MOD = 6
EPS = 1e-6

ADAM_LR = 0.001
ADAM_B1 = 0.9
ADAM_B2 = 0.999
ADAM_EPS = 1e-08
ADAM_WD = 0.01
ADAM_STEP = 10

VMEM_LIMIT = 56 * 1024 * 1024
PACK_W = 2048


def _cparams(sem):
    return pltpu.CompilerParams(dimension_semantics=sem, vmem_limit_bytes=VMEM_LIMIT)


def _matmul(a, b, *, mode, dims, tiles, out_dtypes, name, a_off=0, b3=False,
            out3=False, bias=None, extras=(), epilogue=None, dep=None, a_index=None, b_index=None):
    M, N, K = dims
    tm, tn, tk = tiles
    assert M % tm == 0 and N % tn == 0 and K % tk == 0, (name, dims, tiles)
    gm, gn, gk = M // tm, N // tn, K // tk
    n_extra = len(extras)
    has_bias = bias is not None
    n_out = len(out_dtypes)

    if mode == "nn":
        a_spec = pl.BlockSpec((tm, tk), lambda i, j, k: (i, a_off + k))
        if b3:
            nb = (N // N_DEV) // tn
            assert nb * tn * N_DEV == N
            b_spec = pl.BlockSpec((None, tk, tn), lambda i, j, k: (j // nb, k, j % nb))
        else:
            b_spec = pl.BlockSpec((tk, tn), lambda i, j, k: (k, j))
        dn = (((1,), (0,)), ((), ()))
    elif mode == "nt":
        a_spec = pl.BlockSpec((tm, tk), lambda i, j, k: (i, a_off + k))
        if b3:
            nb = (K // N_DEV) // tk
            assert nb * tk * N_DEV == K
            b_spec = pl.BlockSpec((None, tn, tk), lambda i, j, k: (k // nb, j, k % nb))
        else:
            b_spec = pl.BlockSpec((tn, tk), lambda i, j, k: (j, k))
        dn = (((1,), (1,)), ((), ()))
    else:
        a_spec = pl.BlockSpec((tk, tm), lambda i, j, k: (k, a_off + i))
        b_spec = pl.BlockSpec((tk, tn), lambda i, j, k: (k, j))
        dn = (((0,), (0,)), ((), ()))
    if a_index is not None:
        a_spec = pl.BlockSpec(a_spec.block_shape, a_index)
    if b_index is not None:
        b_spec = pl.BlockSpec(b_spec.block_shape, b_index)

    if out3:
        nbo = (N // N_DEV) // tn
        assert nbo * tn * N_DEV == N
        o_spec = pl.BlockSpec((None, tm, tn), lambda i, j, k: (j // nbo, i, j % nbo))
        o_shape = (N_DEV, M, N // N_DEV)
    else:
        o_spec = pl.BlockSpec((tm, tn), lambda i, j, k: (i, j))
        o_shape = (M, N)

    in_specs = [a_spec, b_spec]
    args = [a, b]
    if has_bias:
        in_specs.append(pl.BlockSpec((1, tn), lambda i, j, k: (0, j)))
        args.append(bias)
    for e in extras:
        in_specs.append(pl.BlockSpec((tm, tn), lambda i, j, k: (i, j)))
        args.append(e)
    n_dep = 0 if dep is None else 1
    if n_dep:
        in_specs.append(pl.BlockSpec(memory_space=pl.ANY))
        args.append(dep)

    def body(*refs):
        a_ref, b_ref = refs[0], refs[1]
        pos = 2
        bias_ref = None
        if has_bias:
            bias_ref = refs[pos]
            pos += 1
        extra_refs = refs[pos:pos + n_extra]
        pos += n_extra + n_dep
        out_refs = refs[pos:pos + n_out]
        acc_ref = refs[pos + n_out] if gk > 1 else None

        part = lax.dot_general(a_ref[...].astype(BF16), b_ref[...].astype(BF16), dn,
                               preferred_element_type=F32)

        def finish(acc):
            if has_bias:
                acc = acc + bias_ref[...]
            if epilogue is None:
                vals = (acc,)
            else:
                vals = epilogue(acc, *[e[...] for e in extra_refs])
            for o_ref, val in zip(out_refs, vals):
                o_ref[...] = val.astype(o_ref.dtype)

        if gk == 1:
            finish(part)
        else:
            k = pl.program_id(2)

            @pl.when(k == 0)
            def _():
                acc_ref[...] = part

            @pl.when(k > 0)
            def _():
                acc_ref[...] += part

            @pl.when(k == gk - 1)
            def _():
                finish(acc_ref[...])

    outs = pl.pallas_call(
        body,
        grid=(gm, gn, gk),
        in_specs=in_specs,
        out_specs=[o_spec] * n_out,
        out_shape=[jax.ShapeDtypeStruct(o_shape, dt) for dt in out_dtypes],
        scratch_shapes=([pltpu.VMEM((tm, tn), F32)] if gk > 1 else []),
        compiler_params=_cparams(("parallel", "parallel", "arbitrary")),
        name=name,
    )(*args)
    return outs[0] if n_out == 1 else outs


def _rowwise(fn, rows, vecs, row_outs, sum_outs, *, n_rows, tr, ch, name, dep=None, prefetch=None):
    assert n_rows % tr == 0 and tr % ch == 0
    nt = n_rows // tr
    nr, nv, nro, nso = len(rows), len(vecs), len(row_outs), len(sum_outs)
    in_specs, args = [], []
    n_pf = 0 if prefetch is None else 1
    for (arr, lead, cblk, w, shift) in rows:
        if shift:
            ridx = lambda i, shift=shift: jnp.minimum(i + shift, nt - 1)
        else:
            ridx = lambda i: i
        if arr.ndim == 3:
            def imap(i, *pf, lead=lead, cblk=cblk, ridx=ridx):
                return (lead(pf[0]) if callable(lead) else lead, ridx(i), cblk)
            in_specs.append(pl.BlockSpec((None, tr, w), imap))
        else:
            in_specs.append(pl.BlockSpec(
                (tr, w), lambda i, *pf, cblk=cblk, ridx=ridx: (ridx(i), cblk)))
        args.append(arr)
    for v in vecs:
        in_specs.append(pl.BlockSpec(v.shape, lambda i, *pf, nd=v.ndim: (0,) * nd))
        args.append(v)
    n_dep = 0 if dep is None else 1
    if n_dep:
        in_specs.append(pl.BlockSpec(memory_space=pl.ANY))
        args.append(dep)
    out_specs = [pl.BlockSpec((tr, w), lambda i, *pf: (i, 0)) for (w, _) in row_outs]
    out_shape = [jax.ShapeDtypeStruct((n_rows, w), dt) for (w, dt) in row_outs]
    for (r, w) in sum_outs:
        out_specs.append(pl.BlockSpec((r, w), lambda i, *pf: (0, 0)))
        out_shape.append(jax.ShapeDtypeStruct((r, w), F32))

    def body(*refs):
        refs = refs[n_pf:]
        i = pl.program_id(0)
        r_in = refs[:nr]
        v_in = refs[nr:nr + nv]
        r_out = refs[nr + nv + n_dep:nr + nv + n_dep + nro]
        s_out = refs[nr + nv + n_dep + nro:]
        s_out, s_acc = s_out[:nso], s_out[nso:]
        if nso:
            @pl.when(i == 0)
            def _():
                for s in s_acc:
                    s[...] = jnp.zeros(s.shape, F32)
        vvals = [v[...] for v in v_in]

        def chunk(ci, carry):
            r0 = pl.multiple_of(ci * ch, ch)
            rv = [r[pl.ds(r0, ch), :].astype(F32) for r in r_in]
            ro, so = fn(rv, vvals, i, nt)
            for ref, val in zip(r_out, ro):
                ref[pl.ds(r0, ch), :] = val.astype(ref.dtype)
            for ref, val in zip(s_acc, so):
                ref[...] += val
            return carry

        lax.fori_loop(0, tr // ch, chunk, 0)
        if nso:
            @pl.when(i == nt - 1)
            def _():
                for s, acc in zip(s_out, s_acc):
                    s[...] = jnp.sum(acc[...], axis=0, keepdims=True)

    outs = pl.pallas_call(
        body,
        grid_spec=pltpu.PrefetchScalarGridSpec(
            num_scalar_prefetch=n_pf, grid=(nt,), in_specs=in_specs, out_specs=out_specs,
            scratch_shapes=[pltpu.VMEM((8, w), F32) for (_, w) in sum_outs]),
        out_shape=out_shape,
        compiler_params=_cparams(("arbitrary",)),
        name=name,
    )(*([prefetch] if n_pf else []), *args)
    return outs


def _row(arr, cblk=0, w=None, lead=0, shift=0):
    return (arr, lead, cblk, arr.shape[-1] if w is None else w, shift)


def _colsum(v):
    parts = [v[8 * k:8 * (k + 1)] for k in range(v.shape[0] // 8)]
    return functools.reduce(lambda a, b: a + b, parts)


def _rms(x):
    return lax.rsqrt(jnp.mean(x * x, axis=-1, keepdims=True) + EPS)


def _sigmoid(x):
    return 1.0 / (1.0 + jnp.exp(-x))


_GELU_C = math.sqrt(2.0 / math.pi)


def _gelu(x):
    return 0.5 * x * (1.0 + jnp.tanh(_GELU_C * (x + 0.044715 * (x * x * x))))


def _gelu_grad(x):
    t = jnp.tanh(_GELU_C * (x + 0.044715 * (x * x * x)))
    return 0.5 * (1.0 + t) + 0.5 * x * (1.0 - t * t) * (_GELU_C * (1.0 + 3.0 * 0.044715 * (x * x)))


def _my_pos():
    return lax.axis_index("x"), lax.axis_index("y"), lax.axis_index("c")


def _flip(pos, k):
    x, y, c = pos
    return (1 - x if k & 4 else x, 1 - y if k & 2 else y, 1 - c if k & 1 else c)


def _dev_id(pos):
    return 4 * pos[0] + 2 * pos[1] + pos[2]


def _small_allgather(x, name):
    r, c = x.shape

    def body(x_ref, out_ref, send_sems, recv_sems):
        me = _my_pos()
        out_ref[_dev_id(me)] = x_ref[...]
        copies = []
        for k in range(1, N_DEV):
            cp = pltpu.make_async_remote_copy(
                src_ref=x_ref, dst_ref=out_ref.at[_dev_id(me)],
                send_sem=send_sems.at[k - 1], recv_sem=recv_sems.at[k - 1],
                device_id=_flip(me, k), device_id_type=MESH)
            cp.start()
            copies.append(cp)
        for k in range(1, N_DEV):
            peer = _flip(me, k)
            pltpu.make_async_remote_copy(
                src_ref=x_ref, dst_ref=out_ref.at[_dev_id(peer)],
                send_sem=send_sems.at[k - 1], recv_sem=recv_sems.at[k - 1],
                device_id=peer, device_id_type=MESH).wait_recv()
        for cp in copies:
            cp.wait_send()

    return pl.pallas_call(
        body,
        out_shape=jax.ShapeDtypeStruct((N_DEV, r, c), x.dtype),
        in_specs=[pl.BlockSpec(memory_space=pltpu.VMEM)],
        out_specs=pl.BlockSpec(memory_space=pltpu.VMEM),
        scratch_shapes=[pltpu.SemaphoreType.DMA((N_DEV - 1,)),
                        pltpu.SemaphoreType.DMA((N_DEV - 1,))],
        compiler_params=pltpu.CompilerParams(vmem_limit_bytes=VMEM_LIMIT),
        name=name,
    )(x)


_HBM = pl.BlockSpec(memory_space=pltpu.HBM)
_SEM = pl.BlockSpec(memory_space=pltpu.SEMAPHORE)
_EFFECT = pltpu.SideEffectType.DATAFLOW_SIDE_EFFECTING


def _relay_copy(zone, send_sems, recv_sems, k, block, to):
    slot = zone.at[_dev_id(block)]
    return pltpu.make_async_remote_copy(
        src_ref=slot, dst_ref=slot, send_sem=send_sems.at[k], recv_sem=recv_sems.at[k],
        device_id=to, device_id_type=MESH)


def _relay_peers():
    x, y, c = _my_pos()
    return (x, y, c), (x, y, 1 - c), [(1 - x, y), (x, 1 - y), (1 - x, 1 - y)]


def _relay_start_call(zones, n_sems, issue, name, after=None):
    n = len(zones)
    n_after = 0 if after is None else 1

    def body(*refs):
        refs = refs[:n] + refs[n + n_after:]
        send, recv, token = refs[n:2 * n], refs[2 * n:3 * n], refs[4 * n]
        for a in range(n):
            issue(refs[a], send[a], recv[a])
        token[...] = jnp.zeros(token.shape, token.dtype)

    sem = pltpu.SemaphoreType.DMA((n_sems,))
    outs = pl.pallas_call(
        body,
        name=name,
        out_shape=([sem] * (2 * n) + [pltpu.HBM(z.shape, z.dtype) for z in zones]
                   + [jax.ShapeDtypeStruct((8, 128), F32)]),
        in_specs=[_HBM] * n + [pl.BlockSpec(memory_space=pl.ANY)] * n_after,
        out_specs=[_SEM] * (2 * n) + [_HBM] * n + [pl.BlockSpec(memory_space=pltpu.VMEM)],
        input_output_aliases={a: 2 * n + a for a in range(n)},
        compiler_params=pltpu.CompilerParams(has_side_effects=_EFFECT),
    )(*[pltpu.with_memory_space_constraint(z, pltpu.HBM) for z in zones],
      *([after] if n_after else []))
    return [(outs[a], outs[n + a], outs[2 * n + a]) for a in range(n)], outs[3 * n]


def _relay_wait_call(flights, settle, after, name):
    n = len(flights)

    def body(*refs):
        send, recv = refs[n:2 * n], refs[2 * n:3 * n]
        for a in range(n):
            settle(refs[a], send[a], recv[a])

    outs = pl.pallas_call(
        body,
        name=name,
        out_shape=[pltpu.HBM(f[2].shape, f[2].dtype) for f in flights],
        in_specs=[_HBM] * n + [_SEM] * (2 * n) + [pl.BlockSpec(memory_space=pl.ANY)],
        out_specs=[_HBM] * n,
        input_output_aliases={a: a for a in range(n)},
        compiler_params=pltpu.CompilerParams(has_side_effects=_EFFECT),
    )(*[f[2] for f in flights], *[f[0] for f in flights], *[f[1] for f in flights], after)
    return list(outs)


def _relay_gather_start(zones, name, after=None):
    def issue(zone, send, recv):
        me, sib, chips = _relay_peers()
        _relay_copy(zone, send, recv, 0, me, sib).start()
        for j, chip in enumerate(chips):
            _relay_copy(zone, send, recv, 1 + j, me, (*chip, me[2])).start()
    return _relay_start_call(zones, 4, issue, name, after)


def _relay_gather_arrive(flights, after, name):
    def settle(zone, send, recv):
        me, sib, chips = _relay_peers()
        _relay_copy(zone, send, recv, 0, sib, me).wait_recv()
        _relay_copy(zone, send, recv, 0, me, sib).wait_send()
        for j, chip in enumerate(chips):
            _relay_copy(zone, send, recv, 1 + j, (*chip, me[2]), me).wait_recv()
            _relay_copy(zone, send, recv, 1 + j, me, (*chip, me[2])).wait_send()
    return _relay_wait_call(flights, settle, after, name)


def _relay_pass_start(zones, name, after=None):
    def issue(zone, send, recv):
        me, sib, chips = _relay_peers()
        for j, chip in enumerate(chips):
            _relay_copy(zone, send, recv, j, (*chip, me[2]), sib).start()
    return _relay_start_call(zones, 3, issue, name, after)


def _relay_pass_wait(flights, after, name):
    def settle(zone, send, recv):
        me, sib, chips = _relay_peers()
        for j, chip in enumerate(chips):
            _relay_copy(zone, send, recv, j, (*chip, sib[2]), me).wait_recv()
            _relay_copy(zone, send, recv, j, (*chip, me[2]), sib).wait_send()
    return _relay_wait_call(flights, settle, after, name)


def _exchange_copy(kind, bufs, send_sems, recv_sems, me, k, arriving):
    peer = _flip(me, k)
    my_id, peer_id = _dev_id(me), _dev_id(peer)
    if kind == "gather":
        slot = bufs[0].at[peer_id if arriving else my_id]
        src, dst = slot, slot
    else:
        src = bufs[0].at[my_id if arriving else peer_id]
        dst = bufs[1].at[peer_id if arriving else my_id]
    return pltpu.make_async_remote_copy(
        src_ref=src, dst_ref=dst, send_sem=send_sems.at[k - 1], recv_sem=recv_sems.at[k - 1],
        device_id=peer, device_id_type=MESH)


def _exchange_start(arrays, kind, name, after=None):
    n = len(arrays)
    n_after = 0 if after is None else 1
    if kind == "gather":
        bufs = [[a] for a in arrays]
    else:
        bufs = [[a, lax.empty(a.shape, a.dtype)] for a in arrays]
    nb = len(bufs[0])
    flat = [b for group in bufs for b in group]

    def body(*refs):
        outs_at = nb * n + n_after
        send = refs[outs_at:outs_at + n]
        recv = refs[outs_at + n:outs_at + 2 * n]
        token = refs[outs_at + 2 * n + nb * n]
        me = _my_pos()
        for a in range(n):
            for k in range(1, N_DEV):
                _exchange_copy(kind, refs[nb * a:nb * (a + 1)], send[a], recv[a], me, k, False).start()
        token[...] = jnp.zeros(token.shape, token.dtype)

    sem = pltpu.SemaphoreType.DMA((N_DEV - 1,))
    outs = pl.pallas_call(
        body,
        name=name,
        out_shape=([sem] * (2 * n) + [pltpu.HBM(b.shape, b.dtype) for b in flat]
                   + [jax.ShapeDtypeStruct((8, 128), F32)]),
        in_specs=[_HBM] * (nb * n) + [pl.BlockSpec(memory_space=pl.ANY)] * n_after,
        out_specs=[_SEM] * (2 * n) + [_HBM] * (nb * n) + [pl.BlockSpec(memory_space=pltpu.VMEM)],
        input_output_aliases={i: 2 * n + i for i in range(nb * n)},
        compiler_params=pltpu.CompilerParams(has_side_effects=_EFFECT),
    )(*[pltpu.with_memory_space_constraint(b, pltpu.HBM) for b in flat],
      *([after] if n_after else []))
    flights = [(outs[a], outs[n + a], list(outs[2 * n + nb * a:2 * n + nb * (a + 1)]))
               for a in range(n)]
    return flights, outs[2 * n + nb * n]


def _exchange_wait(flights, kind, after, name):
    n = len(flights)
    nb = len(flights[0][2])
    flat = [b for f in flights for b in f[2]]

    def body(*refs):
        send = refs[nb * n:nb * n + n]
        recv = refs[nb * n + n:nb * n + 2 * n]
        me = _my_pos()
        for a in range(n):
            for k in range(1, N_DEV):
                bufs = refs[nb * a:nb * (a + 1)]
                _exchange_copy(kind, bufs, send[a], recv[a], me, k, False).wait_send()
                _exchange_copy(kind, bufs, send[a], recv[a], me, k, True).wait_recv()

    outs = pl.pallas_call(
        body,
        name=name,
        out_shape=[pltpu.HBM(b.shape, b.dtype) for b in flat],
        in_specs=[_HBM] * (nb * n) + [_SEM] * (2 * n) + [pl.BlockSpec(memory_space=pl.ANY)],
        out_specs=[_HBM] * (nb * n),
        input_output_aliases={i: i for i in range(nb * n)},
        compiler_params=pltpu.CompilerParams(has_side_effects=_EFFECT),
    )(*flat, *[f[0] for f in flights], *[f[1] for f in flights], after)
    return [list(outs[nb * a:nb * (a + 1)]) for a in range(n)]


def _t5_buckets_block():
    qi = np.arange(BLK)[:, None]
    ki = np.arange(2 * BLK)[None, :]
    n = np.maximum(qi + BLK - ki, 0)
    max_exact = NUM_BUCKETS // 2
    large = max_exact + (np.log(np.maximum(n, 1) / max_exact)
                         / np.log(MAX_DISTANCE / max_exact)
                         * (NUM_BUCKETS - max_exact)).astype(np.int32)
    large = np.minimum(large, NUM_BUCKETS - 1)
    return np.where(n < max_exact, n, large).astype(np.int32)


def _band_mask():
    qi = np.arange(BLK)[:, None]
    ki = np.arange(2 * BLK)[None, :]
    dist = qi + BLK - ki
    return (dist >= 0) & (dist < BLK)


def _attn_scores(q_ref, kp_ref, kc_ref, hkv):
    c0 = hkv * HEAD_DIM
    kk = jnp.concatenate([kp_ref[:, c0:c0 + HEAD_DIM], kc_ref[:, c0:c0 + HEAD_DIM]],
                         axis=0).astype(BF16)
    qg = jnp.concatenate(
        [q_ref[:, (hkv * GROUP + g) * HEAD_DIM:(hkv * GROUP + g + 1) * HEAD_DIM]
         for g in range(GROUP)], axis=0).astype(BF16)
    s = lax.dot_general(qg, kk, (((1,), (1,)), ((), ())), preferred_element_type=F32)
    return qg, kk, s


def _attn_softmax(s, bias_ref, sink_ref, hkv):
    r0, r1 = hkv * GROUP * BLK, (hkv + 1) * GROUP * BLK
    s = s * (HEAD_DIM ** -0.5) + bias_ref[r0:r1, :]
    sink = sink_ref[r0:r1, :]
    m = jnp.maximum(jnp.max(s, axis=-1, keepdims=True), sink)
    p = jnp.exp(s - m)
    e_sink = jnp.exp(sink - m)
    inv = 1.0 / (jnp.sum(p, axis=-1, keepdims=True) + e_sink)
    return p * inv, e_sink * inv


def _kv_rows(p_ref, c_ref, hkv):
    c0 = hkv * HEAD_DIM
    return jnp.concatenate([p_ref[:, c0:c0 + HEAD_DIM], c_ref[:, c0:c0 + HEAD_DIM]],
                           axis=0).astype(BF16)


def _attn_in_specs(bias2):
    prev = lambda n: jnp.maximum(n - 1, 0)
    return [
        pl.BlockSpec((BLK, ATTN_W), lambda n: (n, 0)),
        pl.BlockSpec((BLK, KV_W), lambda n: (prev(n), ATTN_W // KV_W)),
        pl.BlockSpec((BLK, KV_W), lambda n: (n, ATTN_W // KV_W)),
        pl.BlockSpec((BLK, KV_W), lambda n: (prev(n), ATTN_W // KV_W + 1)),
        pl.BlockSpec((BLK, KV_W), lambda n: (n, ATTN_W // KV_W + 1)),
        pl.BlockSpec((None,) + bias2.shape[1:], lambda n: (jnp.minimum(n, 1), 0, 0)),
    ]


def _attention_fwd(proj, bias2, sinkcol, n_rows):
    nb = n_rows // BLK

    def body(q_ref, kp_ref, kc_ref, vp_ref, vc_ref, bias_ref, sink_ref, o_ref):
        heads = range(N_KV_HEADS)
        scores = [_attn_scores(q_ref, kp_ref, kc_ref, hkv)[2] for hkv in heads]
        probs = [_attn_softmax(scores[hkv], bias_ref, sink_ref, hkv)[0] for hkv in heads]
        outs = [jnp.dot(probs[hkv].astype(BF16), _kv_rows(vp_ref, vc_ref, hkv),
                        preferred_element_type=F32) for hkv in heads]
        for hkv in heads:
            for g in range(GROUP):
                h = hkv * GROUP + g
                o_ref[:, h * HEAD_DIM:(h + 1) * HEAD_DIM] = (
                    outs[hkv][g * BLK:(g + 1) * BLK, :].astype(o_ref.dtype))

    return pl.pallas_call(
        body,
        grid=(nb,),
        in_specs=_attn_in_specs(bias2) + [pl.BlockSpec(sinkcol.shape, lambda n: (0, 0))],
        out_specs=pl.BlockSpec((BLK, ATTN_W), lambda n: (n, 0)),
        out_shape=jax.ShapeDtypeStruct((n_rows, ATTN_W), BF16),
        compiler_params=_cparams(("parallel",)),
        name="attn_fwd",
    )(proj, proj, proj, proj, proj, bias2, sinkcol)


def _attention_bwd(proj, attn, dattn, bias2, sinkcol, n_rows):
    nb = n_rows // BLK
    scale = HEAD_DIM ** -0.5
    dn_t = (((0,), (0,)), ((), ()))

    def body(q_ref, kp_ref, kc_ref, vp_ref, vc_ref, bias_ref, o_ref, do_ref, sink_ref,
             dq_ref, dkc_ref, dkp_ref, dvc_ref, dvp_ref, dbias_ref, dsink_ref):
        @pl.when(pl.program_id(0) == 0)
        def _():
            dbias_ref[...] = jnp.zeros(dbias_ref.shape, F32)
            dsink_ref[...] = jnp.zeros(dsink_ref.shape, F32)

        heads = range(N_KV_HEADS)
        qk = [_attn_scores(q_ref, kp_ref, kc_ref, hkv) for hkv in heads]
        dog, dps, deltas = [], [], []
        for hkv in heads:
            hs = [hkv * GROUP + g for g in range(GROUP)]
            d_o = jnp.concatenate([do_ref[:, h * HEAD_DIM:(h + 1) * HEAD_DIM] for h in hs], axis=0)
            o = jnp.concatenate([o_ref[:, h * HEAD_DIM:(h + 1) * HEAD_DIM] for h in hs], axis=0)
            deltas.append(jnp.sum(d_o.astype(F32) * o.astype(F32), axis=-1, keepdims=True))
            dog.append(d_o.astype(BF16))
            dps.append(lax.dot_general(dog[hkv], _kv_rows(vp_ref, vc_ref, hkv),
                                       (((1,), (1,)), ((), ())), preferred_element_type=F32))
        p16, ds16 = [], []
        for hkv in heads:
            r0, r1 = hkv * GROUP * BLK, (hkv + 1) * GROUP * BLK
            p, p_sink = _attn_softmax(qk[hkv][2], bias_ref, sink_ref, hkv)
            ds = p * (dps[hkv] - deltas[hkv])
            dbias_ref[r0:r1, :] += ds
            dsink_ref[r0:r1, :] += -(p_sink * deltas[hkv])
            p16.append(p.astype(BF16))
            ds16.append(ds.astype(BF16))
        for hkv in heads:
            c0 = hkv * HEAD_DIM
            qg, kk, _ = qk[hkv]
            dqg = jnp.dot(ds16[hkv], kk, preferred_element_type=F32) * scale
            dkk = lax.dot_general(ds16[hkv], qg, dn_t, preferred_element_type=F32) * scale
            dvv = lax.dot_general(p16[hkv], dog[hkv], dn_t, preferred_element_type=F32)
            for g in range(GROUP):
                h = hkv * GROUP + g
                dq_ref[:, h * HEAD_DIM:(h + 1) * HEAD_DIM] = (
                    dqg[g * BLK:(g + 1) * BLK, :].astype(dq_ref.dtype))
            dkp_ref[:, c0:c0 + HEAD_DIM] = dkk[:BLK].astype(dkp_ref.dtype)
            dkc_ref[:, c0:c0 + HEAD_DIM] = dkk[BLK:].astype(dkc_ref.dtype)
            dvp_ref[:, c0:c0 + HEAD_DIM] = dvv[:BLK].astype(dvp_ref.dtype)
            dvc_ref[:, c0:c0 + HEAD_DIM] = dvv[BLK:].astype(dvc_ref.dtype)

    kv_out = pl.BlockSpec((BLK, KV_W), lambda n: (n, 0))
    kv_shape = jax.ShapeDtypeStruct((n_rows, KV_W), F32)
    acc_shape = bias2.shape[1:]
    return pl.pallas_call(
        body,
        grid=(nb,),
        in_specs=_attn_in_specs(bias2) + [
            pl.BlockSpec((BLK, ATTN_W), lambda n: (n, 0)),
            pl.BlockSpec((BLK, ATTN_W), lambda n: (n, 0)),
            pl.BlockSpec(sinkcol.shape, lambda n: (0, 0)),
        ],
        out_specs=[
            pl.BlockSpec((BLK, ATTN_W), lambda n: (n, 0)),
            kv_out, kv_out, kv_out, kv_out,
            pl.BlockSpec(acc_shape, lambda n: (0, 0)),
            pl.BlockSpec(sinkcol.shape, lambda n: (0, 0)),
        ],
        out_shape=[
            jax.ShapeDtypeStruct((n_rows, ATTN_W), BF16),
            kv_shape, kv_shape, kv_shape, kv_shape,
            jax.ShapeDtypeStruct(acc_shape, F32),
            jax.ShapeDtypeStruct(sinkcol.shape, F32),
        ],
        compiler_params=_cparams(("arbitrary",)),
        name="attn_bwd",
    )(proj, proj, proj, proj, proj, bias2, attn, dattn, sinkcol)


def _bias_tables(rel_bias_t, onehot_t, band_first, band_rest):
    def body(rb_ref, oh_ref, mf_ref, mr_ref, out_ref):
        acc = jnp.zeros((N_Q_HEADS, BLK * 2 * BLK), F32)
        for part in _split3(rb_ref[...]):
            acc = acc + jnp.dot(part, oh_ref[...], preferred_element_type=F32)
        out_ref[0] = jnp.where(mf_ref[...] > 0.0, acc, NEG_INF)
        out_ref[1] = jnp.where(mr_ref[...] > 0.0, acc, NEG_INF)

    return pl.pallas_call(
        body,
        out_shape=jax.ShapeDtypeStruct((2, N_Q_HEADS, BLK * 2 * BLK), F32),
        compiler_params=pltpu.CompilerParams(vmem_limit_bytes=VMEM_LIMIT),
        name="bias_tables",
    )(rel_bias_t, onehot_t, band_first, band_rest)


def _split3(a):
    hi = a.astype(BF16)
    r1 = a - hi.astype(F32)
    mid = r1.astype(BF16)
    lo = (r1 - mid.astype(F32)).astype(BF16)
    return hi, mid, lo


def _bucket_reduce(dbias, dsink, onehot_t):
    def body(db_ref, ds_ref, oh_ref, ob_ref, os_ref):
        acc = jnp.zeros((N_Q_HEADS, 128), F32)
        for part in _split3(db_ref[...]):
            acc = acc + lax.dot_general(part, oh_ref[...], (((1,), (1,)), ((), ())),
                                        preferred_element_type=F32)
        ob_ref[...] = acc
        os_ref[...] = jnp.broadcast_to(jnp.sum(ds_ref[...], axis=-1, keepdims=True),
                                       os_ref.shape)

    return pl.pallas_call(
        body,
        out_shape=[jax.ShapeDtypeStruct((N_Q_HEADS, 128), F32),
                   jax.ShapeDtypeStruct((N_Q_HEADS, 128), F32)],
        compiler_params=pltpu.CompilerParams(vmem_limit_bytes=VMEM_LIMIT),
        name="bias_bucket_reduce",
    )(dbias, dsink, onehot_t)


def _disc(lr, li, ls, btr, bti):
    lam_re = jnp.minimum(lr, -1e-4)
    delta = jnp.exp(ls)
    mag = jnp.exp(lam_re * delta)
    ang = li * delta
    ar, ai = mag * jnp.cos(ang), mag * jnp.sin(ang)
    nr, ni = ar - 1.0, ai
    den = lam_re * lam_re + li * li
    fr = (nr * lam_re + ni * li) / den
    fi = (ni * lam_re - nr * li) / den
    bbr = fr * btr - fi * bti
    bbi = fr * bti + fi * btr
    return ar, ai, bbr, bbi


def _block_mask():
    row = lax.broadcasted_iota(jnp.int32, (SSM_W, SSM_H), 0)
    col = lax.broadcasted_iota(jnp.int32, (SSM_W, SSM_H), 1)
    return (row // SSM_P) == (col // SSM_N)


def _ssm_setup(lr, li, ls, btr, bti, ctr, cti):
    def body(lr_ref, li_ref, ls_ref, btr_ref, bti_ref, ctr_ref, cti_ref, a_ref, b_ref, c_ref):
        ar, ai, bbr, bbi = _disc(lr_ref[...], li_ref[...], ls_ref[...], btr_ref[...], bti_ref[...])
        a_ref[:, :SSM_H] = ar
        a_ref[:, SSM_H:] = ai
        mask = _block_mask()
        blk = lambda t: jnp.where(mask, jnp.tile(t, (SSM_G, 1)), 0.0)
        b_ref[:, :SSM_H] = blk(bbr).astype(BF16)
        b_ref[:, SSM_H:] = blk(bbi).astype(BF16)
        c_ref[:, :SSM_H] = blk(ctr_ref[...]).astype(BF16)
        c_ref[:, SSM_H:] = blk(-cti_ref[...]).astype(BF16)

    return pl.pallas_call(
        body,
        out_shape=[jax.ShapeDtypeStruct((1, 2 * SSM_H), F32),
                   jax.ShapeDtypeStruct((SSM_W, 2 * SSM_H), BF16),
                   jax.ShapeDtypeStruct((SSM_W, 2 * SSM_H), BF16)],
        compiler_params=pltpu.CompilerParams(vmem_limit_bytes=VMEM_LIMIT),
        name="ssm_setup",
    )(lr, li, ls, btr, bti, ctr, cti)


def _ssm_param_bwd(lr, li, ls, btr, bti, dacc, dbcat, dccat, gind):
    def body(lr_ref, li_ref, ls_ref, btr_ref, bti_ref, dacc_ref, db_ref, dc_ref, g_ref,
             dlr_ref, dli_ref, dls_ref, dbtr_ref, dbti_ref, dctr_ref, dcti_ref):
        dar = jnp.sum(dacc_ref[:, :SSM_H], axis=0, keepdims=True)
        dai = jnp.sum(dacc_ref[:, SSM_H:], axis=0, keepdims=True)
        col = lax.broadcasted_iota(jnp.int32, (SSM_P, 2 * SSM_H), 1)
        grp = (col % SSM_H) // SSM_N
        db = jnp.zeros((SSM_P, 2 * SSM_H), F32)
        dc = jnp.zeros((SSM_P, 2 * SSM_H), F32)
        half = SSM_G // 2
        for g in range(SSM_G):
            sel = grp == g
            r0 = (g % half) * SSM_P
            db = db + jnp.where(sel, db_ref[r0:r0 + SSM_P, :], 0.0)
            dc = dc + jnp.where(sel, dc_ref[r0:r0 + SSM_P, :], 0.0)
        dctr_ref[...] = dc[:, :SSM_H]
        dcti_ref[...] = -dc[:, SSM_H:]
        prim = (lr_ref[...], li_ref[...], ls_ref[...], btr_ref[...], bti_ref[...])
        _, vjp = jax.vjp(_disc, *prim)
        dlr, dli, dls, dbtr, dbti = vjp((dar, dai, db[:, :SSM_H], db[:, SSM_H:]))
        dlr_ref[...] = dlr
        dli_ref[...] = dli
        dbtr_ref[...] = dbtr
        dbti_ref[...] = dbti
        acc = jnp.zeros((8, 128), F32)
        for part in _split3(jnp.broadcast_to(dls, (8, SSM_H))):
            acc = acc + jnp.dot(part, g_ref[...], preferred_element_type=F32)
        dls_ref[...] = acc

    vec = jax.ShapeDtypeStruct((1, SSM_H), F32)
    mat = jax.ShapeDtypeStruct((SSM_P, SSM_H), F32)
    return pl.pallas_call(
        body,
        out_shape=[vec, vec, jax.ShapeDtypeStruct((8, 128), F32), mat, mat, mat, mat],
        compiler_params=pltpu.CompilerParams(vmem_limit_bytes=VMEM_LIMIT),
        name="ssm_param_bwd",
    )(lr, li, ls, btr, bti, dacc, dbcat, dccat, gind)


SCAN_TR = 256


def _cmul_add(vr, vi, pr, pi, sr, si):
    return vr + pr * sr - pi * si, vi + pr * si + pi * sr


def _bcast_row(v, row, which):
    b = jnp.where(row == which, v, 0.0)
    b = b + pltpu.roll(b, 4, 0)
    b = b + pltpu.roll(b, 2, 0)
    return b + pltpu.roll(b, 1, 0)


def _scan_tables(a_ref, tab_ref, reverse):
    H = SSM_H
    ar = jnp.broadcast_to(a_ref[:, :H], (8, H))
    ai = jnp.broadcast_to(a_ref[:, H:], (8, H))
    if reverse:
        ai = -ai
    row = lax.broadcasted_iota(jnp.int32, (8, H), 0)
    pw = [(ar, ai)]
    for _ in range(7):
        cr, ci = pw[-1]
        pw.append((cr * ar - ci * ai, cr * ai + ci * ar))
    pcr = jnp.zeros((8, H), F32)
    pci = jnp.zeros((8, H), F32)
    for e in range(8):
        sel = (row == (7 - e)) if reverse else (row == e)
        pcr = jnp.where(sel, pw[e][0], pcr)
        pci = jnp.where(sel, pw[e][1], pci)
    tab_ref[0, :, :H] = pcr
    tab_ref[0, :, H:] = pci
    for t, k in enumerate((1, 2, 4)):
        keep = (row < 8 - k) if reverse else (row >= k)
        tab_ref[1 + t, :, :H] = jnp.where(keep, pw[k - 1][0], 0.0)
        tab_ref[1 + t, :, H:] = jnp.where(keep, pw[k - 1][1], 0.0)


def _scan_group(vr, vi, cr, ci, tab_ref, reverse):
    H = SSM_H
    for t, k in enumerate((1, 2, 4)):
        sh = 8 - k if reverse else k
        vr, vi = _cmul_add(vr, vi, tab_ref[1 + t, :, :H], tab_ref[1 + t, :, H:],
                           pltpu.roll(vr, sh, 0), pltpu.roll(vi, sh, 0))
    return _cmul_add(vr, vi, tab_ref[0, :, :H], tab_ref[0, :, H:], cr, ci)


def _scan_fwd(proj, u_blk, bcat, ccat, abar, n_rows):
    H = SSM_H
    nt = n_rows // SCAN_TR

    def body(u_ref, b_ref, c_ref, a_ref, xs_ref, xp_ref, yc_ref, bu_ref, tab_ref, carry_ref):
        @pl.when(pl.program_id(0) == 0)
        def _():
            _scan_tables(a_ref, tab_ref, False)
            carry_ref[...] = jnp.zeros(carry_ref.shape, F32)

        bu_ref[...] = jnp.dot(u_ref[...].astype(BF16), b_ref[...], preferred_element_type=F32)
        row = lax.broadcasted_iota(jnp.int32, (8, H), 0)

        def group(j, carry):
            cr, ci = carry
            r0 = pl.multiple_of(j * 16, 16)
            xr, xi = [], []
            for half in range(2):
                rr = pl.multiple_of(r0 + 8 * half, 8)
                vr, vi = _scan_group(bu_ref[pl.ds(rr, 8), :H], bu_ref[pl.ds(rr, 8), H:],
                                     cr, ci, tab_ref, False)
                xp_ref[pl.ds(rr, 8), :H] = jnp.where(row == 0, cr, pltpu.roll(vr, 1, 0))
                xp_ref[pl.ds(rr, 8), H:] = jnp.where(row == 0, ci, pltpu.roll(vi, 1, 0))
                cr, ci = _bcast_row(vr, row, 7), _bcast_row(vi, row, 7)
                xr.append(vr)
                xi.append(vi)
            xs_ref[pl.ds(r0, 16), :H] = jnp.concatenate(xr, axis=0).astype(BF16)
            xs_ref[pl.ds(r0, 16), H:] = jnp.concatenate(xi, axis=0).astype(BF16)
            return cr, ci

        cr, ci = lax.fori_loop(0, SCAN_TR // 16, group,
                               (carry_ref[:, :H], carry_ref[:, H:]))
        carry_ref[:, :H] = cr
        carry_ref[:, H:] = ci
        yc_ref[...] = lax.dot_general(xs_ref[...], c_ref[...], (((1,), (1,)), ((), ())),
                                      preferred_element_type=F32)

    tile = lambda w: pl.BlockSpec((SCAN_TR, w), lambda i: (i, 0))
    whole = lambda a: pl.BlockSpec(a.shape, lambda i: (0, 0))
    return pl.pallas_call(
        body,
        grid=(nt,),
        in_specs=[pl.BlockSpec((SCAN_TR, SSM_W), lambda i: (i, u_blk)),
                  whole(bcat), whole(ccat), whole(abar)],
        out_specs=[tile(2 * H), tile(2 * H), tile(SSM_W)],
        out_shape=[jax.ShapeDtypeStruct((n_rows, 2 * H), BF16),
                   jax.ShapeDtypeStruct((n_rows, 2 * H), F32),
                   jax.ShapeDtypeStruct((n_rows, SSM_W), F32)],
        scratch_shapes=[pltpu.VMEM((SCAN_TR, 2 * H), F32), pltpu.VMEM((4, 8, 2 * H), F32),
                        pltpu.VMEM((8, 2 * H), F32)],
        compiler_params=_cparams(("arbitrary",)),
        name="ssm_scan_fwd",
    )(proj, bcat, ccat, abar)


def _scan_bwd(dy, xprev, bcat, ccat, abar, n_rows):
    H = SSM_H
    nt = n_rows // SCAN_TR

    def body(dy_ref, xp_ref, b_ref, c_ref, a_ref, h_ref, da_ref, du_ref, g_ref, tab_ref, carry_ref):
        @pl.when(pl.program_id(0) == 0)
        def _():
            _scan_tables(a_ref, tab_ref, True)
            carry_ref[...] = jnp.zeros(carry_ref.shape, F32)
            da_ref[...] = jnp.zeros(da_ref.shape, F32)

        g_ref[...] = jnp.dot(dy_ref[...], c_ref[...], preferred_element_type=F32)
        row = lax.broadcasted_iota(jnp.int32, (8, H), 0)
        n16 = SCAN_TR // 16

        def group(jj, carry):
            cr, ci = carry
            r0 = pl.multiple_of((n16 - 1 - jj) * 16, 16)
            hr, hi = [None, None], [None, None]
            for half in (1, 0):
                rr = pl.multiple_of(r0 + 8 * half, 8)
                vr, vi = _scan_group(g_ref[pl.ds(rr, 8), :H], g_ref[pl.ds(rr, 8), H:],
                                     cr, ci, tab_ref, True)
                pr, pi = xp_ref[pl.ds(rr, 8), :H], xp_ref[pl.ds(rr, 8), H:]
                da_ref[:, :H] += vr * pr + vi * pi
                da_ref[:, H:] += vi * pr - vr * pi
                cr, ci = _bcast_row(vr, row, 0), _bcast_row(vi, row, 0)
                hr[half], hi[half] = vr, vi
            h_ref[pl.ds(r0, 16), :H] = jnp.concatenate(hr, axis=0).astype(BF16)
            h_ref[pl.ds(r0, 16), H:] = jnp.concatenate(hi, axis=0).astype(BF16)
            return cr, ci

        cr, ci = lax.fori_loop(0, n16, group, (carry_ref[:, :H], carry_ref[:, H:]))
        carry_ref[:, :H] = cr
        carry_ref[:, H:] = ci
        du_ref[...] = lax.dot_general(h_ref[...], b_ref[...], (((1,), (1,)), ((), ())),
                                      preferred_element_type=F32)

    rev = lambda i: (nt - 1 - i, 0)
    whole = lambda a: pl.BlockSpec(a.shape, lambda i: (0, 0))
    return pl.pallas_call(
        body,
        grid=(nt,),
        in_specs=[pl.BlockSpec((SCAN_TR, SSM_W), rev),
                  pl.BlockSpec((SCAN_TR, 2 * H), rev),
                  whole(bcat), whole(ccat), whole(abar)],
        out_specs=[pl.BlockSpec((SCAN_TR, 2 * H), rev),
                   pl.BlockSpec((8, 2 * H), lambda i: (0, 0)),
                   pl.BlockSpec((SCAN_TR, SSM_W), rev)],
        out_shape=[jax.ShapeDtypeStruct((n_rows, 2 * H), BF16),
                   jax.ShapeDtypeStruct((8, 2 * H), F32),
                   jax.ShapeDtypeStruct((n_rows, SSM_W), F32)],
        scratch_shapes=[pltpu.VMEM((SCAN_TR, 2 * H), F32), pltpu.VMEM((4, 8, 2 * H), F32),
                        pltpu.VMEM((8, 2 * H), F32)],
        compiler_params=_cparams(("arbitrary",)),
        name="ssm_scan_bwd",
    )(dy, xprev, bcat, ccat, abar)


def _adamw(parts, w, m, v, *, tr, ch, name, prefetch=None):
    n_rows, cols = w.shape
    n_parts = len(parts)
    c1 = 1.0 - ADAM_B1 ** ADAM_STEP
    c2 = 1.0 - ADAM_B2 ** ADAM_STEP

    def fn(rv, vv, i, nt):
        g = rv[0].astype(F32)
        for p in rv[1:n_parts]:
            g = g + p.astype(F32)
        wv, mv, vval = rv[n_parts:]
        nm = ADAM_B1 * mv + (1.0 - ADAM_B1) * g
        nv = ADAM_B2 * vval + (1.0 - ADAM_B2) * (g * g)
        delta = -ADAM_LR * ((nm / c1) / (jnp.sqrt(nv / c2) + ADAM_EPS) + ADAM_WD * wv)
        return [g, delta, nm, nv], []

    rows = [_row(arr, lead=lead) for (arr, lead) in parts] + [_row(w), _row(m), _row(v)]
    return _rowwise(fn, rows, [], [(cols, F32)] * 4, [], n_rows=n_rows, tr=tr, ch=ch, name=name,
                    prefetch=prefetch)


_PACK = [
    ("b_ada", 6), ("norm1_g", 1), ("b_in", 3), ("norm2_g", 1), ("final_g", 1),
    ("lambda_re", 1), ("lambda_im", 1), ("log_step", 1), ("attn_sinks", 1),
    ("rel_bias", 1), ("b_glu", 1), ("ssm_d", 1), ("loss", 1),
    ("ssm_b_re", 16), ("ssm_b_im", 16), ("ssm_c_re", 16), ("ssm_c_im", 16),
]
_PACK_OFF = {}
_off = 0
for _n, _r in _PACK:
    _PACK_OFF[_n] = (_off, _r)
    _off += _r
PACK_ROWS = -(-_off // 8) * 8


def _to_rows(a, rows):
    flat = a.reshape(-1).astype(F32)
    pad = rows * PACK_W - flat.shape[0]
    if pad:
        flat = jnp.pad(flat, (0, pad))
    return flat.reshape(rows, PACK_W)


def _b_to_rows(b):
    return jnp.transpose(b, (2, 0, 1)).reshape(SSM_P, SSM_H)


def _rows_to_b(r):
    return jnp.transpose(r.reshape(SSM_P, SSM_G, SSM_N), (1, 2, 0))


def _c_to_rows(cm):
    return jnp.transpose(cm, (1, 0, 2)).reshape(SSM_P, SSM_H)


def _rows_to_c(r):
    return jnp.transpose(r.reshape(SSM_P, SSM_G, SSM_N), (1, 0, 2))


def _pack(vals):
    out = jnp.zeros((PACK_ROWS, PACK_W), F32)
    for n, r in _PACK:
        if n in vals:
            pieces = vals[n] if isinstance(vals[n], list) else [vals[n]]
            rows_each = r // len(pieces)
            for i, piece in enumerate(pieces):
                out = lax.dynamic_update_slice(out, _to_rows(piece, rows_each),
                                               (_PACK_OFF[n][0] + i * rows_each, 0))
    return out


def _unpack(packed, name, shape):
    o, r = _PACK_OFF[name]
    n = int(np.prod(shape))
    return packed[o:o + r].reshape(-1)[:n].reshape(shape)


def _small_params_packed(p):
    return {
        "b_ada": p["b_ada"], "norm1_g": p["norm1_g"], "b_in": p["b_in"],
        "norm2_g": p["norm2_g"], "final_g": p["final_g"],
        "lambda_re": p["lambda_re"], "lambda_im": p["lambda_im"],
        "log_step": p["log_step"], "attn_sinks": p["attn_sinks"],
        "rel_bias": p["rel_bias"], "b_glu": p["b_glu"], "ssm_d": p["ssm_d"],
        "ssm_b_re": _b_to_rows(p["ssm_b_re"][0]), "ssm_b_im": _b_to_rows(p["ssm_b_im"][0]),
        "ssm_c_re": _c_to_rows(p["ssm_c_re"][0]), "ssm_c_im": _c_to_rows(p["ssm_c_im"][0]),
    }


_SMALL_SHAPES = {
    "b_ada": (1, N_MOD * D), "norm1_g": (1, D), "b_in": (1, IN_W), "norm2_g": (1, D),
    "final_g": (D,), "lambda_re": (1, SSM_G, SSM_N), "lambda_im": (1, SSM_G, SSM_N),
    "log_step": (1, SSM_G), "attn_sinks": (1, N_Q_HEADS), "rel_bias": (NUM_BUCKETS, N_Q_HEADS),
    "b_glu": (1, SSM_W), "ssm_d": (1, SSM_W),
}


def _unpack_small(packed, name):
    if name in ("ssm_b_re", "ssm_b_im"):
        o, r = _PACK_OFF[name]
        return _rows_to_b(packed[o:o + r])[None]
    if name in ("ssm_c_re", "ssm_c_im"):
        o, r = _PACK_OFF[name]
        return _rows_to_c(packed[o:o + r])[None]
    return _unpack(packed, name, _SMALL_SHAPES[name])


WEIGHT_ORDER = ['w_ada', 'b_ada', 'norm1_g', 'w_in', 'b_in', 'attn_sinks', 'rel_bias', 'lambda_re',
                'lambda_im', 'log_step', 'ssm_b_re', 'ssm_b_im', 'ssm_c_re', 'ssm_c_im', 'ssm_d',
                'w_glu', 'b_glu', 'w_attn_proj', 'w_ssm_proj', 'w_out', 'norm2_g', 'w_ff1', 'w_ff2',
                'final_g']
BIG = ['w_in', 'w_glu', 'w_attn_proj', 'w_ssm_proj', 'w_out', 'w_ff1', 'w_ff2']


ADAMW_TILE_ELEMS = 1 << 18


def _to_col_blocks(w):
    k, n = w.shape
    return jnp.transpose(w.reshape(k, N_DEV, n // N_DEV), (1, 0, 2))


def _adamw_rows(rows, cols):
    tr = rows
    while tr * cols > ADAMW_TILE_ELEMS and tr % 32 == 0:
        tr //= 2
    return tr


def _cast_to_slot(w, me1, name, dep=None):
    rows, cols = w.shape
    tr = min(rows, 256)
    n_dep = 0 if dep is None else 1

    def body(me_ref, w_ref, *rest):
        rest[-1][...] = w_ref[...].astype(BF16)

    return pl.pallas_call(
        body,
        grid_spec=pltpu.PrefetchScalarGridSpec(
            num_scalar_prefetch=1, grid=(rows // tr,),
            in_specs=[pl.BlockSpec((tr, cols), lambda i, me_ref: (i, 0))]
            + [pl.BlockSpec(memory_space=pl.ANY)] * n_dep,
            out_specs=pl.BlockSpec((None, tr, cols), lambda i, me_ref: (me_ref[0], i, 0))),
        out_shape=jax.ShapeDtypeStruct((N_DEV, rows, cols), BF16),
        compiler_params=_cparams(("arbitrary",)),
        name=name,
    )(me1, w, *([dep] if n_dep else []))


def kernel(x, c, w_ada, b_ada, norm1_g, w_in, b_in, attn_sinks, rel_bias, lambda_re, lambda_im, log_step, ssm_b_re, ssm_b_im, ssm_c_re, ssm_c_im, ssm_d, w_glu, b_glu, w_attn_proj, w_ssm_proj, w_out, norm2_g, w_ff1, w_ff2, final_g, loss_target, m_w_ada, m_b_ada, m_norm1_g, m_w_in, m_b_in, m_attn_sinks, m_rel_bias, m_lambda_re, m_lambda_im, m_log_step, m_ssm_b_re, m_ssm_b_im, m_ssm_c_re, m_ssm_c_im, m_ssm_d, m_w_glu, m_b_glu, m_w_attn_proj, m_w_ssm_proj, m_w_out, m_norm2_g, m_w_ff1, m_w_ff2, m_final_g, v_w_ada, v_b_ada, v_norm1_g, v_w_in, v_b_in, v_attn_sinks, v_rel_bias, v_lambda_re, v_lambda_im, v_log_step, v_ssm_b_re, v_ssm_b_im, v_ssm_c_re, v_ssm_c_im, v_ssm_d, v_w_glu, v_b_glu, v_w_attn_proj, v_w_ssm_proj, v_w_out, v_norm2_g, v_w_ff1, v_w_ff2, v_final_g):
    loc = dict(locals())
    W = {n: loc[n] for n in WEIGHT_ORDER}
    Mo = {n: loc["m_" + n] for n in WEIGHT_ORDER}
    Vo = {n: loc["v_" + n] for n in WEIGHT_ORDER}
    S = x.shape[1]
    TM = min(512, S)
    TS = min(1024, S)
    TR = min(256, S)
    TW = min(1024, S)
    me = 4 * lax.axis_index("x") + 2 * lax.axis_index("y") + lax.axis_index("c")
    x2d = x.reshape(S, D)
    tgt = loss_target.reshape(S, D)

    c_all = _small_allgather(c, "allgather_c").reshape(N_DEV, D)
    cs = _rowwise(lambda rv, vv, i, nt: ([rv[0] * _sigmoid(rv[0])], []), [_row(c_all)], [],
                  [(D, F32)], [], n_rows=N_DEV, tr=8, ch=8, name="silu_c")[0]
    n_ada = N_MOD * D // N_DEV
    b_ada_cols = lax.dynamic_slice(b_ada, (0, me * n_ada), (1, n_ada))
    mod_piece = _matmul(cs, w_ada[0], mode="nn", dims=(N_DEV, n_ada, D), tiles=(N_DEV, 512, D),
                        out_dtypes=[F32], name="ada_fwd", bias=b_ada_cols)
    mod_all = _small_allgather(mod_piece, "allgather_mod")
    mod_b = lax.dynamic_index_in_dim(mod_all, me, axis=1, keepdims=False).reshape(N_MOD, D)
    sh1, sc1, g1, sh2, sc2, g2 = [mod_b[i:i + 1] for i in range(N_MOD)]

    shard = {n: W[n][0] for n in BIG}
    me1 = jnp.reshape(me, (1,)).astype(jnp.int32)
    zone = {"w_in": _cast_to_slot(shard["w_in"], me1, "cast_w_in")}
    (in_flight,), tok_in = _relay_gather_start([zone["w_in"]], "w_in_start", mod_all)
    for n in BIG[1:]:
        zone[n] = _cast_to_slot(shard[n], me1, "cast_" + n, dep=tok_in)
    G = {}

    def f_norm1(rv, vv, i, nt):
        xv, (g, sc, sh) = rv[0], vv
        return [(xv * _rms(xv) * g) * (1.0 + sc) + sh], []

    h = _rowwise(f_norm1, [_row(x2d)], [norm1_g, sc1, sh1], [(D, BF16)], [],
                 n_rows=S, tr=TR, ch=32, name="norm1_fwd", dep=zone["w_ff2"])[0]
    (zone_in,) = _relay_gather_arrive([in_flight], h, "w_in_arrive")
    (in_pass,), tok_p = _relay_pass_start([zone_in], "w_in_pass_start")
    mixer = ["w_attn_proj", "w_glu", "w_ssm_proj", "w_out"]
    flights, tok_w = _exchange_start([zone[n] for n in mixer], "gather", "weights_start", tok_p)
    w_flight = dict(zip(mixer, flights))
    ff_flights, tok_w = _relay_gather_start([zone["w_ff1"], zone["w_ff2"]], "ff_weights_start", tok_w)
    (G["w_in"],) = _relay_pass_wait([in_pass], tok_w, "w_in_pass_wait")
    proj = _matmul(h, G["w_in"], mode="nn", dims=(S, IN_W, D), tiles=(TM, 768, D),
                   out_dtypes=[BF16], name="in_proj", b3=True, bias=b_in, dep=tok_w)

    buckets = _t5_buckets_block()
    band = _band_mask()
    onehot_t = jnp.asarray(
        (np.arange(128)[:, None] == buckets.reshape(-1)[None, :]).astype(np.float32), BF16)
    band_first = band & (np.arange(2 * BLK)[None, :] >= BLK)
    rel_bias_t = jnp.pad(jnp.transpose(rel_bias), ((0, 0), (0, 128 - NUM_BUCKETS)))
    bias2 = _bias_tables(rel_bias_t, onehot_t,
                         jnp.asarray(band_first.reshape(1, -1).astype(np.float32)),
                         jnp.asarray(band.reshape(1, -1).astype(np.float32))
                         ).reshape(2, N_Q_HEADS * BLK, 2 * BLK)
    sinkcol = jnp.repeat(attn_sinks.reshape(N_Q_HEADS), BLK).reshape(N_Q_HEADS * BLK, 1)
    attn = _attention_fwd(proj, bias2, sinkcol, S)
    landed = _exchange_wait([w_flight[n] for n in mixer], "gather", attn, "weights_wait_mixer")
    G.update((n, bufs[0]) for n, bufs in zip(mixer, landed))
    w_glu_f = G["w_glu"].reshape(SSM_W, SSM_W)
    w_out_f = G["w_out"].reshape(D, D)
    w_ap_f = jnp.transpose(G["w_attn_proj"], (1, 0, 2)).reshape(ATTN_W, D)
    w_sp_f = jnp.transpose(G["w_ssm_proj"], (1, 0, 2)).reshape(SSM_W, D)
    y_attn = _matmul(attn, w_ap_f, mode="nn", dims=(S, D, ATTN_W), tiles=(TW, 1024, ATTN_W),
                     out_dtypes=[BF16], name="attn_proj")

    lam_re = lambda_re.reshape(1, SSM_H)
    lam_im = lambda_im.reshape(1, SSM_H)
    ls_x = jnp.repeat(log_step.reshape(SSM_G), SSM_N).reshape(1, SSM_H)
    btr, bti = _b_to_rows(ssm_b_re[0]), _b_to_rows(ssm_b_im[0])
    ctr, cti = _c_to_rows(ssm_c_re[0]), _c_to_rows(ssm_c_im[0])
    abar, bcat, ccat = _ssm_setup(lam_re, lam_im, ls_x, btr, bti, ctr, cti)
    u_blk = (ATTN_W + 2 * KV_W) // SSM_W
    xs, xprev, yc = _scan_fwd(proj, u_blk, bcat, ccat, abar, S)

    def f_ssm_out(rv, vv, i, nt):
        y = rv[0] + vv[0] * rv[1]
        return [y, _gelu(y)], []

    y_ssm_pre, z = _rowwise(f_ssm_out, [_row(yc), _row(proj, u_blk, SSM_W)], [ssm_d],
                            [(SSM_W, F32), (SSM_W, BF16)], [], n_rows=S, tr=TM, ch=32, name="ssm_out")
    zg = _matmul(z, w_glu_f, mode="nn", dims=(S, SSM_W, SSM_W), tiles=(TM, SSM_W, SSM_W),
                 out_dtypes=[F32], name="glu_proj", bias=b_glu)
    z2 = _rowwise(lambda rv, vv, i, nt: ([rv[0].astype(F32) * _sigmoid(rv[1])], []),
                  [_row(z), _row(zg)], [], [(SSM_W, BF16)], [], n_rows=S, tr=TM, ch=32, name="glu_gate")[0]
    y_ssm = _matmul(z2, w_sp_f, mode="nn", dims=(S, D, SSM_W), tiles=(TW, 1024, SSM_W),
                    out_dtypes=[BF16], name="ssm_proj")

    ga_row = _row(proj, 1, D)
    gs_row = _row(proj, 2, D)

    def f_merge(rv, vv, i, nt):
        ga, gs, ya, ys = rv
        return [_sigmoid(ga) * ya + _sigmoid(gs) * ys], []

    merged = _rowwise(f_merge, [ga_row, gs_row, _row(y_attn), _row(y_ssm)], [], [(D, BF16)], [],
                      n_rows=S, tr=TR, ch=32, name="merge")[0]
    mo = _matmul(merged, w_out_f, mode="nn", dims=(S, D, D), tiles=(TW, 1024, D),
                 out_dtypes=[BF16], name="out_proj")

    ff_zones = _relay_gather_arrive(ff_flights, mo, "ff_weights_arrive")
    ff_pass, tok_fp = _relay_pass_start(ff_zones, "ff_weights_pass_start")

    def f_norm2(rv, vv, i, nt):
        xv, mv = rv
        g1v, g, sc, sh = vv
        x1v = xv + g1v * mv
        return [x1v, (x1v * _rms(x1v) * g) * (1.0 + sc) + sh], []

    x1, h2 = _rowwise(f_norm2, [_row(x2d), _row(mo)], [g1, norm2_g, sc2, sh2],
                      [(D, F32), (D, BF16)], [], n_rows=S, tr=TR, ch=32, name="norm2_fwd", dep=tok_fp)

    def relu_sq(acc):
        r = jnp.maximum(acc, 0.0)
        return r * r, r

    (G["w_ff1"],) = _relay_pass_wait(ff_pass[:1], h2, "w_ff1_pass_wait")
    act, relu = _matmul(h2, G["w_ff1"], mode="nn", dims=(S, D_FF, D), tiles=(TM, 1024, D),
                        out_dtypes=[BF16, BF16], name="ff1", b3=True, epilogue=relu_sq)
    w_ff2_f = _relay_pass_wait(ff_pass[1:], act, "w_ff2_pass_wait")[0].reshape(D_FF, D)
    ff = _matmul(act, w_ff2_f, mode="nn", dims=(S, D, D_FF), tiles=(TM, 1024, 2048),
                 out_dtypes=[BF16], name="ff2")

    def f_loss(rv, vv, i, nt):
        x1v, ffv, tv = rv
        g2v, gf = vv
        x2v = x1v + g2v * ffv
        r = _rms(x2v)
        xh = x2v * r
        diff = xh * gf - tv
        dy = diff * (1.0 / D)
        dxh = dy * gf
        dx2 = r * (dxh - xh * jnp.mean(dxh * xh, axis=-1, keepdims=True))
        return [dx2, dx2 * g2v], [_colsum(0.5 * diff * diff * (1.0 / D)), _colsum(dy * xh),
                                  _colsum(dx2 * ffv)]

    dx2, dff, loss_cols, d_final_g, dg2 = _rowwise(
        f_loss, [_row(x1), _row(ff), _row(tgt)], [g2, final_g.reshape(1, D)],
        [(D, F32), (D, BF16)], [(1, D)] * 3, n_rows=S, tr=TR, ch=32, name="loss_bwd")

    df1 = _matmul(dff, w_ff2_f, mode="nt", dims=(S, D_FF, D), tiles=(TM, 1024, D),
                  out_dtypes=[BF16], name="ff2_dgrad", extras=(relu,),
                  epilogue=lambda acc, r: (acc * (2.0 * r.astype(F32)),))
    gw_ff2 = _matmul(act, dff, mode="tn", dims=(D_FF, D, S), tiles=(1024, 1024, TS),
                     out_dtypes=[BF16], name="ff2_wgrad").reshape(N_DEV, D_FF // N_DEV, D)
    g_flight = {}
    (g_flight["w_ff2"],), tok = _exchange_start([gw_ff2], "scatter", "grads_start_ff2")
    dh2 = _matmul(df1, G["w_ff1"], mode="nt", dims=(S, D, D_FF), tiles=(TM, D, 1024),
                  out_dtypes=[BF16], name="ff1_dgrad", b3=True, dep=tok)
    gw_ff1 = _matmul(h2, df1, mode="tn", dims=(D, D_FF, S), tiles=(1024, 1024, TS),
                     out_dtypes=[BF16], name="ff1_wgrad", out3=True)
    (g_flight["w_ff1"],), tok = _exchange_start([gw_ff1], "scatter", "grads_start_ff1")

    def f_norm2_bwd(rv, vv, i, nt):
        x1v, dh, dx2v, mv = rv
        g, sc, g1v = vv
        r = _rms(x1v)
        xh = x1v * r
        t = xh * g
        dt = dh * (1.0 + sc)
        dxh = dt * g
        dx1 = dx2v + r * (dxh - xh * jnp.mean(dxh * xh, axis=-1, keepdims=True))
        return [dx1, dx1 * g1v], [_colsum(dh), _colsum(dh * t), _colsum(dt * xh), _colsum(dx1 * mv)]

    dx1, dmo, dsh2, dsc2, d_norm2_g, dg1 = _rowwise(
        f_norm2_bwd, [_row(x1), _row(dh2), _row(dx2), _row(mo)], [norm2_g, sc2, g1],
        [(D, F32), (D, BF16)], [(1, D)] * 4, n_rows=S, tr=TR, ch=16, name="norm2_bwd", dep=tok)

    dmerged = _matmul(dmo, w_out_f, mode="nt", dims=(S, D, D), tiles=(TW, 1024, D),
                      out_dtypes=[BF16], name="out_dgrad")
    gw_out = _matmul(merged, dmo, mode="tn", dims=(D, D, S), tiles=(1024, 1024, TS),
                     out_dtypes=[BF16], name="out_wgrad").reshape(N_DEV, D // N_DEV, D)
    (g_flight["w_out"],), tok = _exchange_start([gw_out], "scatter", "grads_start_out")

    def f_merge_bwd(rv, vv, i, nt):
        dm, ga, gs, ya, ys = rv
        sa, ss = _sigmoid(ga), _sigmoid(gs)
        return [dm * sa, dm * ss, dm * ya * sa * (1.0 - sa), dm * ys * ss * (1.0 - ss)], []

    dy_attn, dy_ssm, dga, dgs = _rowwise(
        f_merge_bwd, [_row(dmerged), ga_row, gs_row, _row(y_attn), _row(y_ssm)], [],
        [(D, BF16)] * 4, [], n_rows=S, tr=TR, ch=16, name="merge_bwd", dep=tok)

    dz2 = _matmul(dy_ssm, w_sp_f, mode="nt", dims=(S, SSM_W, D), tiles=(TW, SSM_W, D),
                  out_dtypes=[F32], name="ssm_proj_dgrad")
    gw_ssm_proj = _to_col_blocks(_matmul(z2, dy_ssm, mode="tn", dims=(SSM_W, D, S), tiles=(SSM_W, 1024, TS),
                                         out_dtypes=[BF16], name="ssm_proj_wgrad"))

    def f_glu_bwd(rv, vv, i, nt):
        dz2v, zv, zgv = rv
        sg = _sigmoid(zgv)
        dzg = dz2v * zv.astype(F32) * sg * (1.0 - sg)
        return [dzg, dz2v * sg], [_colsum(dzg)]

    dzg, dz_a, d_b_glu = _rowwise(f_glu_bwd, [_row(dz2), _row(z), _row(zg)], [],
                                  [(SSM_W, BF16), (SSM_W, F32)], [(1, SSM_W)],
                                  n_rows=S, tr=TM, ch=32, name="glu_bwd")
    dz_b = _matmul(dzg, w_glu_f, mode="nt", dims=(S, SSM_W, SSM_W), tiles=(TM, SSM_W, SSM_W),
                   out_dtypes=[F32], name="glu_dgrad")
    gw_glu = _matmul(z, dzg, mode="tn", dims=(SSM_W, SSM_W, S), tiles=(SSM_W, SSM_W, TS),
                     out_dtypes=[BF16], name="glu_wgrad").reshape(N_DEV, SSM_W // N_DEV, SSM_W)
    (g_flight["w_ssm_proj"], g_flight["w_glu"]), tok = _exchange_start(
        [gw_ssm_proj, gw_glu], "scatter", "grads_start_ssm")

    def f_ssm_out_bwd(rv, vv, i, nt):
        dza, dzb, yv, uv = rv
        dy = (dza + dzb) * _gelu_grad(yv)
        return [dy, dy * vv[0]], [_colsum(dy * uv)]

    dy_s, du_a, d_ssm_d = _rowwise(
        f_ssm_out_bwd, [_row(dz_a), _row(dz_b), _row(y_ssm_pre), _row(proj, u_blk, SSM_W)], [ssm_d],
        [(SSM_W, BF16), (SSM_W, F32)], [(1, SSM_W)], n_rows=S, tr=TM, ch=32, name="ssm_out_bwd", dep=tok)
    hw = SSM_W // 2
    u_half = (ATTN_W + 2 * KV_W) // hw
    dccat = _matmul(dy_s, xs, mode="tn", dims=(hw, 2 * SSM_H, S), tiles=(hw, 1024, TS),
                    out_dtypes=[F32], name="ssm_c_wgrad", a_index=lambda i, j, k: (k, j % 2))
    hs, dacc, du_b = _scan_bwd(dy_s, xprev, bcat, ccat, abar, S)
    dbcat = _matmul(proj, hs, mode="tn", dims=(hw, 2 * SSM_H, S), tiles=(hw, 1024, TS),
                    out_dtypes=[F32], name="ssm_b_wgrad", a_index=lambda i, j, k: (k, u_half + j % 2))
    grp = np.arange(SSM_H) // SSM_N
    gind = jnp.asarray((grp[:, None] == np.arange(128)[None, :]).astype(np.float32), BF16)
    d_lam_re, d_lam_im, d_ls, d_btr, d_bti, d_ctr, d_cti = _ssm_param_bwd(
        lam_re, lam_im, ls_x, btr, bti, dacc, dbcat, dccat, gind)

    dattn = _matmul(dy_attn, w_ap_f, mode="nt", dims=(S, ATTN_W, D), tiles=(TW, ATTN_W, D),
                    out_dtypes=[BF16], name="attn_proj_dgrad")
    gw_attn_proj = _to_col_blocks(_matmul(attn, dy_attn, mode="tn", dims=(ATTN_W, D, S), tiles=(ATTN_W, 1024, TS),
                                          out_dtypes=[BF16], name="attn_proj_wgrad"))
    (g_flight["w_attn_proj"],), tok = _exchange_start(
        [gw_attn_proj], "scatter", "grads_start_attn")
    dq, dkc, dkp, dvc, dvp, dbias, dsink = _attention_bwd(proj, attn, dattn, bias2, sinkcol, S)
    d_bias_b, d_sinks = _bucket_reduce(dbias.reshape(N_Q_HEADS, BLK * 2 * BLK),
                                       dsink.reshape(N_Q_HEADS, BLK), onehot_t)

    def f_dproj(rv, vv, i, nt):
        dqv, kc, kp, vc, vp, dua, dub, gav, gsv = rv
        keep = (i < nt - 1).astype(F32)
        dp = jnp.concatenate([dqv.astype(F32), kc + keep * kp, vc + keep * vp, dua + dub,
                              gav.astype(F32), gsv.astype(F32)], axis=-1)
        return [dp], [_colsum(dp)]

    dproj, d_b_in = _rowwise(
        f_dproj, [_row(dq), _row(dkc), _row(dkp, shift=1), _row(dvc), _row(dvp, shift=1),
                  _row(du_a), _row(du_b), _row(dga), _row(dgs)], [],
        [(IN_W, BF16)], [(1, IN_W)], n_rows=S, tr=BLK, ch=16, name="dproj", dep=tok)
    gw_in = _matmul(h, dproj, mode="tn", dims=(D, IN_W, S), tiles=(1024, 768, TS),
                    out_dtypes=[BF16], name="in_wgrad", out3=True)
    (g_flight["w_in"],), tok = _exchange_start([gw_in], "scatter", "grads_start_in")
    dh = _matmul(dproj, G["w_in"], mode="nt", dims=(S, D, IN_W), tiles=(TM, D, 768),
                 out_dtypes=[BF16], name="in_dgrad", b3=True, dep=tok)

    def f_norm1_bwd(rv, vv, i, nt):
        xv, dhv, dx1v = rv
        g, sc = vv
        r = _rms(xv)
        xh = xv * r
        t = xh * g
        dt = dhv * (1.0 + sc)
        dxh = dt * g
        dxv = dx1v + r * (dxh - xh * jnp.mean(dxh * xh, axis=-1, keepdims=True))
        return [dxv], [_colsum(dhv), _colsum(dhv * t), _colsum(dt * xh)]

    grad_x, dsh1, dsc1, d_norm1_g = _rowwise(
        f_norm1_bwd, [_row(x2d), _row(dh), _row(dx1)], [norm1_g, sc1],
        [(D, F32)], [(1, D)] * 3, n_rows=S, tr=TR, ch=32, name="norm1_bwd")

    part = _pack({
        "b_ada": [dsh1, dsc1, dg1, dsh2, dsc2, dg2], "norm1_g": d_norm1_g, "b_in": d_b_in, "norm2_g": d_norm2_g,
        "final_g": d_final_g, "lambda_re": d_lam_re, "lambda_im": d_lam_im,
        "log_step": d_ls[0, :SSM_G], "attn_sinks": d_sinks[:, 0],
        "rel_bias": jnp.transpose(d_bias_b[:, :NUM_BUCKETS]), "b_glu": d_b_glu, "ssm_d": d_ssm_d,
        "loss": loss_cols, "ssm_b_re": d_btr, "ssm_b_im": d_bti, "ssm_c_re": d_ctr, "ssm_c_im": d_cti,
    })
    zone_small = lax.dynamic_update_slice(lax.empty((N_DEV, PACK_ROWS, PACK_W), F32), part[None], (me, 0, 0))
    (small_flight,), after = _exchange_start([zone_small], "gather", "small_grads_start")

    big_out = {}
    for n in ["w_ff2", "w_ff1", "w_out", "w_ssm_proj", "w_glu", "w_attn_proj", "w_in"]:
        own, recv = _exchange_wait([g_flight[n]], "scatter", after, "grads_wait_" + n[2:])[0]
        rows, cols = shard[n].shape
        parts = [(own, lambda m: m[0])] + [
            (recv, lambda m, j=j: jnp.where(j >= m[0], j + 1, j)) for j in range(N_DEV - 1)]
        big_out[n] = _adamw(parts, shard[n], Mo[n][0], Vo[n][0], tr=_adamw_rows(rows, cols), ch=16,
                            name="adamw_" + n, prefetch=me1)
        after = big_out[n][0]

    part_all = _exchange_wait([small_flight], "gather", after, "small_grads_wait")[0][0]
    wp, mp, vp = [_pack(_small_params_packed(p)) for p in (W, Mo, Vo)]
    sg, sdelta, sm, sv = _adamw([(part_all, d) for d in range(N_DEV)], wp, mp, vp,
                                tr=PACK_ROWS, ch=8, name="adamw_small")
    lo, _ = _PACK_OFF["loss"]
    loss = jnp.sum(sg[lo])

    o_ada, _ = _PACK_OFF["b_ada"]
    dmod_all = part_all[:, o_ada:o_ada + N_MOD, :].reshape(N_DEV, N_MOD * D)
    dmod_cols = lax.dynamic_slice(dmod_all, (0, me * n_ada), (N_DEV, n_ada))
    gw_ada = _matmul(cs, dmod_cols, mode="tn", dims=(D, n_ada, N_DEV), tiles=(D, 512, N_DEV),
                     out_dtypes=[F32], name="ada_wgrad")
    big_out["w_ada"] = _adamw([(gw_ada, 0)], w_ada[0], m_w_ada[0], v_w_ada[0],
                              tr=_adamw_rows(D, n_ada), ch=16, name="adamw_w_ada")

    def leaf(kind, n):
        if n in big_out:
            return big_out[n][kind][None]
        return _unpack_small((sg, sdelta, sm, sv)[kind], n)

    outs = [loss, grad_x.reshape(1, S, D)]
    for kind in range(4):
        outs.extend(leaf(kind, n) for n in WEIGHT_ORDER)
    return tuple(outs)
```

```python
import functools
import math

import numpy as np
import jax
import jax.numpy as jnp
from jax import lax
from jax.experimental import pallas as pl
from jax.experimental.pallas import tpu as pltpu

F32 = jnp.float32
BF16 = jnp.bfloat16
MESH = pl.DeviceIdType.MESH

N_DEV = 8
D = 2048
HEAD_DIM = 64
N_Q_HEADS = 16
N_KV_HEADS = 4
GROUP = N_Q_HEADS // N_KV_HEADS
ATTN_W = N_Q_HEADS * HEAD_DIM
KV_W = N_KV_HEADS * HEAD_DIM
BLK = 128
NUM_BUCKETS = 32
MAX_DISTANCE = 128
NEG_INF = -1e30
SSM_W = 512
SSM_P = 16
SSM_G = 32
SSM_N = 64
SSM_H = SSM_G * SSM_N
D_FF = 4 * D
IN_W = ATTN_W + 2 * KV_W + SSM_W + 2 * D
N_MOD = 6
EPS = 1e-6

ADAM_LR = 0.001
ADAM_B1 = 0.9
ADAM_B2 = 0.999
ADAM_EPS = 1e-08
ADAM_WD = 0.01
ADAM_STEP = 10

VMEM_LIMIT = 56 * 1024 * 1024
PACK_W = 2048


def _cparams(sem):
    return pltpu.CompilerParams(dimension_semantics=sem, vmem_limit_bytes=VMEM_LIMIT)


def _matmul(a, b, *, mode, dims, tiles, out_dtypes, name, a_off=0, b3=False,
            out3=False, bias=None, extras=(), epilogue=None, dep=None, a_index=None, b_index=None):
    M, N, K = dims
    tm, tn, tk = tiles
    assert M % tm == 0 and N % tn == 0 and K % tk == 0, (name, dims, tiles)
    gm, gn, gk = M // tm, N // tn, K // tk
    n_extra = len(extras)
    has_bias = bias is not None
    n_out = len(out_dtypes)

    if mode == "nn":
        a_spec = pl.BlockSpec((tm, tk), lambda i, j, k: (i, a_off + k))
        if b3:
            nb = (N // N_DEV) // tn
            assert nb * tn * N_DEV == N
            b_spec = pl.BlockSpec((None, tk, tn), lambda i, j, k: (j // nb, k, j % nb))
        else:
            b_spec = pl.BlockSpec((tk, tn), lambda i, j, k: (k, j))
        dn = (((1,), (0,)), ((), ()))
    elif mode == "nt":
        a_spec = pl.BlockSpec((tm, tk), lambda i, j, k: (i, a_off + k))
        if b3:
            nb = (K // N_DEV) // tk
            assert nb * tk * N_DEV == K
            b_spec = pl.BlockSpec((None, tn, tk), lambda i, j, k: (k // nb, j, k % nb))
        else:
            b_spec = pl.BlockSpec((tn, tk), lambda i, j, k: (j, k))
        dn = (((1,), (1,)), ((), ()))
    else:
        a_spec = pl.BlockSpec((tk, tm), lambda i, j, k: (k, a_off + i))
        b_spec = pl.BlockSpec((tk, tn), lambda i, j, k: (k, j))
        dn = (((0,), (0,)), ((), ()))
    if a_index is not None:
        a_spec = pl.BlockSpec(a_spec.block_shape, a_index)
    if b_index is not None:
        b_spec = pl.BlockSpec(b_spec.block_shape, b_index)

    if out3:
        nbo = (N // N_DEV) // tn
        assert nbo * tn * N_DEV == N
        o_spec = pl.BlockSpec((None, tm, tn), lambda i, j, k: (j // nbo, i, j % nbo))
        o_shape = (N_DEV, M, N // N_DEV)
    else:
        o_spec = pl.BlockSpec((tm, tn), lambda i, j, k: (i, j))
        o_shape = (M, N)

    in_specs = [a_spec, b_spec]
    args = [a, b]
    if has_bias:
        in_specs.append(pl.BlockSpec((1, tn), lambda i, j, k: (0, j)))
        args.append(bias)
    for e in extras:
        in_specs.append(pl.BlockSpec((tm, tn), lambda i, j, k: (i, j)))
        args.append(e)
    n_dep = 0 if dep is None else 1
    if n_dep:
        in_specs.append(pl.BlockSpec(memory_space=pl.ANY))
        args.append(dep)

    def body(*refs):
        a_ref, b_ref = refs[0], refs[1]
        pos = 2
        bias_ref = None
        if has_bias:
            bias_ref = refs[pos]
            pos += 1
        extra_refs = refs[pos:pos + n_extra]
        pos += n_extra + n_dep
        out_refs = refs[pos:pos + n_out]
        acc_ref = refs[pos + n_out] if gk > 1 else None

        part = lax.dot_general(a_ref[...].astype(BF16), b_ref[...].astype(BF16), dn,
                               preferred_element_type=F32)

        def finish(acc):
            if has_bias:
                acc = acc + bias_ref[...]
            if epilogue is None:
                vals = (acc,)
            else:
                vals = epilogue(acc, *[e[...] for e in extra_refs])
            for o_ref, val in zip(out_refs, vals):
                o_ref[...] = val.astype(o_ref.dtype)

        if gk == 1:
            finish(part)
        else:
            k = pl.program_id(2)

            @pl.when(k == 0)
            def _():
                acc_ref[...] = part

            @pl.when(k > 0)
            def _():
                acc_ref[...] += part

            @pl.when(k == gk - 1)
            def _():
                finish(acc_ref[...])

    outs = pl.pallas_call(
        body,
        grid=(gm, gn, gk),
        in_specs=in_specs,
        out_specs=[o_spec] * n_out,
        out_shape=[jax.ShapeDtypeStruct(o_shape, dt) for dt in out_dtypes],
        scratch_shapes=([pltpu.VMEM((tm, tn), F32)] if gk > 1 else []),
        compiler_params=_cparams(("parallel", "parallel", "arbitrary")),
        name=name,
    )(*args)
    return outs[0] if n_out == 1 else outs


def _rowwise(fn, rows, vecs, row_outs, sum_outs, *, n_rows, tr, ch, name, dep=None, prefetch=None):
    assert n_rows % tr == 0 and tr % ch == 0
    nt = n_rows // tr
    nr, nv, nro, nso = len(rows), len(vecs), len(row_outs), len(sum_outs)
    in_specs, args = [], []
    n_pf = 0 if prefetch is None else 1
    for (arr, lead, cblk, w, shift) in rows:
        if shift:
            ridx = lambda i, shift=shift: jnp.minimum(i + shift, nt - 1)
        else:
            ridx = lambda i: i
        if arr.ndim == 3:
            def imap(i, *pf, lead=lead, cblk=cblk, ridx=ridx):
                return (lead(pf[0]) if callable(lead) else lead, ridx(i), cblk)
            in_specs.append(pl.BlockSpec((None, tr, w), imap))
        else:
            in_specs.append(pl.BlockSpec(
                (tr, w), lambda i, *pf, cblk=cblk, ridx=ridx: (ridx(i), cblk)))
        args.append(arr)
    for v in vecs:
        in_specs.append(pl.BlockSpec(v.shape, lambda i, *pf, nd=v.ndim: (0,) * nd))
        args.append(v)
    n_dep = 0 if dep is None else 1
    if n_dep:
        in_specs.append(pl.BlockSpec(memory_space=pl.ANY))
        args.append(dep)
    out_specs = [pl.BlockSpec((tr, w), lambda i, *pf: (i, 0)) for (w, _) in row_outs]
    out_shape = [jax.ShapeDtypeStruct((n_rows, w), dt) for (w, dt) in row_outs]
    for (r, w) in sum_outs:
        out_specs.append(pl.BlockSpec((r, w), lambda i, *pf: (0, 0)))
        out_shape.append(jax.ShapeDtypeStruct((r, w), F32))

    def body(*refs):
        refs = refs[n_pf:]
        i = pl.program_id(0)
        r_in = refs[:nr]
        v_in = refs[nr:nr + nv]
        r_out = refs[nr + nv + n_dep:nr + nv + n_dep + nro]
        s_out = refs[nr + nv + n_dep + nro:]
        s_out, s_acc = s_out[:nso], s_out[nso:]
        if nso:
            @pl.when(i == 0)
            def _():
                for s in s_acc:
                    s[...] = jnp.zeros(s.shape, F32)
        vvals = [v[...] for v in v_in]

        def chunk(ci, carry):
            r0 = pl.multiple_of(ci * ch, ch)
            rv = [r[pl.ds(r0, ch), :].astype(F32) for r in r_in]
            ro, so = fn(rv, vvals, i, nt)
            for ref, val in zip(r_out, ro):
                ref[pl.ds(r0, ch), :] = val.astype(ref.dtype)
            for ref, val in zip(s_acc, so):
                ref[...] += val
            return carry

        lax.fori_loop(0, tr // ch, chunk, 0)
        if nso:
            @pl.when(i == nt - 1)
            def _():
                for s, acc in zip(s_out, s_acc):
                    s[...] = jnp.sum(acc[...], axis=0, keepdims=True)

    outs = pl.pallas_call(
        body,
        grid_spec=pltpu.PrefetchScalarGridSpec(
            num_scalar_prefetch=n_pf, grid=(nt,), in_specs=in_specs, out_specs=out_specs,
            scratch_shapes=[pltpu.VMEM((8, w), F32) for (_, w) in sum_outs]),
        out_shape=out_shape,
        compiler_params=_cparams(("arbitrary",)),
        name=name,
    )(*([prefetch] if n_pf else []), *args)
    return outs


def _row(arr, cblk=0, w=None, lead=0, shift=0):
    return (arr, lead, cblk, arr.shape[-1] if w is None else w, shift)


def _colsum(v):
    parts = [v[8 * k:8 * (k + 1)] for k in range(v.shape[0] // 8)]
    return functools.reduce(lambda a, b: a + b, parts)


def _rms(x):
    return lax.rsqrt(jnp.mean(x * x, axis=-1, keepdims=True) + EPS)


def _sigmoid(x):
    return 1.0 / (1.0 + jnp.exp(-x))


_GELU_C = math.sqrt(2.0 / math.pi)


def _gelu(x):
    return 0.5 * x * (1.0 + jnp.tanh(_GELU_C * (x + 0.044715 * (x * x * x))))


def _gelu_grad(x):
    t = jnp.tanh(_GELU_C * (x + 0.044715 * (x * x * x)))
    return 0.5 * (1.0 + t) + 0.5 * x * (1.0 - t * t) * (_GELU_C * (1.0 + 3.0 * 0.044715 * (x * x)))


def _my_pos():
    return lax.axis_index("x"), lax.axis_index("y"), lax.axis_index("c")


def _flip(pos, k):
    x, y, c = pos
    return (1 - x if k & 4 else x, 1 - y if k & 2 else y, 1 - c if k & 1 else c)


def _dev_id(pos):
    return 4 * pos[0] + 2 * pos[1] + pos[2]


def _small_allgather(x, name):
    r, c = x.shape

    def body(x_ref, out_ref, send_sems, recv_sems):
        me = _my_pos()
        out_ref[_dev_id(me)] = x_ref[...]
        copies = []
        for k in range(1, N_DEV):
            cp = pltpu.make_async_remote_copy(
                src_ref=x_ref, dst_ref=out_ref.at[_dev_id(me)],
                send_sem=send_sems.at[k - 1], recv_sem=recv_sems.at[k - 1],
                device_id=_flip(me, k), device_id_type=MESH)
            cp.start()
            copies.append(cp)
        for k in range(1, N_DEV):
            peer = _flip(me, k)
            pltpu.make_async_remote_copy(
                src_ref=x_ref, dst_ref=out_ref.at[_dev_id(peer)],
                send_sem=send_sems.at[k - 1], recv_sem=recv_sems.at[k - 1],
                device_id=peer, device_id_type=MESH).wait_recv()
        for cp in copies:
            cp.wait_send()

    return pl.pallas_call(
        body,
        out_shape=jax.ShapeDtypeStruct((N_DEV, r, c), x.dtype),
        in_specs=[pl.BlockSpec(memory_space=pltpu.VMEM)],
        out_specs=pl.BlockSpec(memory_space=pltpu.VMEM),
        scratch_shapes=[pltpu.SemaphoreType.DMA((N_DEV - 1,)),
                        pltpu.SemaphoreType.DMA((N_DEV - 1,))],
        compiler_params=pltpu.CompilerParams(vmem_limit_bytes=VMEM_LIMIT),
        name=name,
    )(x)


_HBM = pl.BlockSpec(memory_space=pltpu.HBM)
_SEM = pl.BlockSpec(memory_space=pltpu.SEMAPHORE)
_EFFECT = pltpu.SideEffectType.DATAFLOW_SIDE_EFFECTING


def _relay_copy(zone, send_sems, recv_sems, k, block, to):
    slot = zone.at[_dev_id(block)]
    return pltpu.make_async_remote_copy(
        src_ref=slot, dst_ref=slot, send_sem=send_sems.at[k], recv_sem=recv_sems.at[k],
        device_id=to, device_id_type=MESH)


def _relay_peers():
    x, y, c = _my_pos()
    return (x, y, c), (x, y, 1 - c), [(1 - x, y), (x, 1 - y), (1 - x, 1 - y)]


def _relay_start_call(zones, n_sems, issue, name, after=None):
    n = len(zones)
    n_after = 0 if after is None else 1

    def body(*refs):
        refs = refs[:n] + refs[n + n_after:]
        send, recv, token = refs[n:2 * n], refs[2 * n:3 * n], refs[4 * n]
        for a in range(n):
            issue(refs[a], send[a], recv[a])
        token[...] = jnp.zeros(token.shape, token.dtype)

    sem = pltpu.SemaphoreType.DMA((n_sems,))
    outs = pl.pallas_call(
        body,
        name=name,
        out_shape=([sem] * (2 * n) + [pltpu.HBM(z.shape, z.dtype) for z in zones]
                   + [jax.ShapeDtypeStruct((8, 128), F32)]),
        in_specs=[_HBM] * n + [pl.BlockSpec(memory_space=pl.ANY)] * n_after,
        out_specs=[_SEM] * (2 * n) + [_HBM] * n + [pl.BlockSpec(memory_space=pltpu.VMEM)],
        input_output_aliases={a: 2 * n + a for a in range(n)},
        compiler_params=pltpu.CompilerParams(has_side_effects=_EFFECT),
    )(*[pltpu.with_memory_space_constraint(z, pltpu.HBM) for z in zones],
      *([after] if n_after else []))
    return [(outs[a], outs[n + a], outs[2 * n + a]) for a in range(n)], outs[3 * n]


def _relay_wait_call(flights, settle, after, name):
    n = len(flights)

    def body(*refs):
        send, recv = refs[n:2 * n], refs[2 * n:3 * n]
        for a in range(n):
            settle(refs[a], send[a], recv[a])

    outs = pl.pallas_call(
        body,
        name=name,
        out_shape=[pltpu.HBM(f[2].shape, f[2].dtype) for f in flights],
        in_specs=[_HBM] * n + [_SEM] * (2 * n) + [pl.BlockSpec(memory_space=pl.ANY)],
        out_specs=[_HBM] * n,
        input_output_aliases={a: a for a in range(n)},
        compiler_params=pltpu.CompilerParams(has_side_effects=_EFFECT),
    )(*[f[2] for f in flights], *[f[0] for f in flights], *[f[1] for f in flights], after)
    return list(outs)


def _relay_gather_start(zones, name, after=None):
    def issue(zone, send, recv):
        me, sib, chips = _relay_peers()
        _relay_copy(zone, send, recv, 0, me, sib).start()
        for j, chip in enumerate(chips):
            _relay_copy(zone, send, recv, 1 + j, me, (*chip, me[2])).start()
    return _relay_start_call(zones, 4, issue, name, after)


def _relay_gather_arrive(flights, after, name):
    def settle(zone, send, recv):
        me, sib, chips = _relay_peers()
        _relay_copy(zone, send, recv, 0, sib, me).wait_recv()
        _relay_copy(zone, send, recv, 0, me, sib).wait_send()
        for j, chip in enumerate(chips):
            _relay_copy(zone, send, recv, 1 + j, (*chip, me[2]), me).wait_recv()
            _relay_copy(zone, send, recv, 1 + j, me, (*chip, me[2])).wait_send()
    return _relay_wait_call(flights, settle, after, name)


def _relay_pass_start(zones, name, after=None):
    def issue(zone, send, recv):
        me, sib, chips = _relay_peers()
        for j, chip in enumerate(chips):
            _relay_copy(zone, send, recv, j, (*chip, me[2]), sib).start()
    return _relay_start_call(zones, 3, issue, name, after)


def _relay_pass_wait(flights, after, name):
    def settle(zone, send, recv):
        me, sib, chips = _relay_peers()
        for j, chip in enumerate(chips):
            _relay_copy(zone, send, recv, j, (*chip, sib[2]), me).wait_recv()
            _relay_copy(zone, send, recv, j, (*chip, me[2]), sib).wait_send()
    return _relay_wait_call(flights, settle, after, name)


def _exchange_copy(kind, bufs, send_sems, recv_sems, me, k, arriving):
    peer = _flip(me, k)
    my_id, peer_id = _dev_id(me), _dev_id(peer)
    if kind == "gather":
        slot = bufs[0].at[peer_id if arriving else my_id]
        src, dst = slot, slot
    else:
        src = bufs[0].at[my_id if arriving else peer_id]
        dst = bufs[1].at[peer_id if arriving else my_id]
    return pltpu.make_async_remote_copy(
        src_ref=src, dst_ref=dst, send_sem=send_sems.at[k - 1], recv_sem=recv_sems.at[k - 1],
        device_id=peer, device_id_type=MESH)


def _exchange_start(arrays, kind, name, after=None):
    n = len(arrays)
    n_after = 0 if after is None else 1
    if kind == "gather":
        bufs = [[a] for a in arrays]
    else:
        bufs = [[a, lax.empty(a.shape, a.dtype)] for a in arrays]
    nb = len(bufs[0])
    flat = [b for group in bufs for b in group]

    def body(*refs):
        outs_at = nb * n + n_after
        send = refs[outs_at:outs_at + n]
        recv = refs[outs_at + n:outs_at + 2 * n]
        token = refs[outs_at + 2 * n + nb * n]
        me = _my_pos()
        for a in range(n):
            for k in range(1, N_DEV):
                _exchange_copy(kind, refs[nb * a:nb * (a + 1)], send[a], recv[a], me, k, False).start()
        token[...] = jnp.zeros(token.shape, token.dtype)

    sem = pltpu.SemaphoreType.DMA((N_DEV - 1,))
    outs = pl.pallas_call(
        body,
        name=name,
        out_shape=([sem] * (2 * n) + [pltpu.HBM(b.shape, b.dtype) for b in flat]
                   + [jax.ShapeDtypeStruct((8, 128), F32)]),
        in_specs=[_HBM] * (nb * n) + [pl.BlockSpec(memory_space=pl.ANY)] * n_after,
        out_specs=[_SEM] * (2 * n) + [_HBM] * (nb * n) + [pl.BlockSpec(memory_space=pltpu.VMEM)],
        input_output_aliases={i: 2 * n + i for i in range(nb * n)},
        compiler_params=pltpu.CompilerParams(has_side_effects=_EFFECT),
    )(*[pltpu.with_memory_space_constraint(b, pltpu.HBM) for b in flat],
      *([after] if n_after else []))
    flights = [(outs[a], outs[n + a], list(outs[2 * n + nb * a:2 * n + nb * (a + 1)]))
               for a in range(n)]
    return flights, outs[2 * n + nb * n]


def _exchange_wait(flights, kind, after, name):
    n = len(flights)
    nb = len(flights[0][2])
    flat = [b for f in flights for b in f[2]]

    def body(*refs):
        send = refs[nb * n:nb * n + n]
        recv = refs[nb * n + n:nb * n + 2 * n]
        me = _my_pos()
        for a in range(n):
            for k in range(1, N_DEV):
                bufs = refs[nb * a:nb * (a + 1)]
                _exchange_copy(kind, bufs, send[a], recv[a], me, k, False).wait_send()
                _exchange_copy(kind, bufs, send[a], recv[a], me, k, True).wait_recv()

    outs = pl.pallas_call(
        body,
        name=name,
        out_shape=[pltpu.HBM(b.shape, b.dtype) for b in flat],
        in_specs=[_HBM] * (nb * n) + [_SEM] * (2 * n) + [pl.BlockSpec(memory_space=pl.ANY)],
        out_specs=[_HBM] * (nb * n),
        input_output_aliases={i: i for i in range(nb * n)},
        compiler_params=pltpu.CompilerParams(has_side_effects=_EFFECT),
    )(*flat, *[f[0] for f in flights], *[f[1] for f in flights], after)
    return [list(outs[nb * a:nb * (a + 1)]) for a in range(n)]


def _t5_buckets_block():
    qi = np.arange(BLK)[:, None]
    ki = np.arange(2 * BLK)[None, :]
    n = np.maximum(qi + BLK - ki, 0)
    max_exact = NUM_BUCKETS // 2
    large = max_exact + (np.log(np.maximum(n, 1) / max_exact)
                         / np.log(MAX_DISTANCE / max_exact)
                         * (NUM_BUCKETS - max_exact)).astype(np.int32)
    large = np.minimum(large, NUM_BUCKETS - 1)
    return np.where(n < max_exact, n, large).astype(np.int32)


def _band_mask():
    qi = np.arange(BLK)[:, None]
    ki = np.arange(2 * BLK)[None, :]
    dist = qi + BLK - ki
    return (dist >= 0) & (dist < BLK)


def _attn_scores(q_ref, kp_ref, kc_ref, hkv):
    c0 = hkv * HEAD_DIM
    kk = jnp.concatenate([kp_ref[:, c0:c0 + HEAD_DIM], kc_ref[:, c0:c0 + HEAD_DIM]],
                         axis=0).astype(BF16)
    qg = jnp.concatenate(
        [q_ref[:, (hkv * GROUP + g) * HEAD_DIM:(hkv * GROUP + g + 1) * HEAD_DIM]
         for g in range(GROUP)], axis=0).astype(BF16)
    s = lax.dot_general(qg, kk, (((1,), (1,)), ((), ())), preferred_element_type=F32)
    return qg, kk, s


def _attn_softmax(s, bias_ref, sink_ref, hkv):
    r0, r1 = hkv * GROUP * BLK, (hkv + 1) * GROUP * BLK
    s = s * (HEAD_DIM ** -0.5) + bias_ref[r0:r1, :]
    sink = sink_ref[r0:r1, :]
    m = jnp.maximum(jnp.max(s, axis=-1, keepdims=True), sink)
    p = jnp.exp(s - m)
    e_sink = jnp.exp(sink - m)
    inv = 1.0 / (jnp.sum(p, axis=-1, keepdims=True) + e_sink)
    return p * inv, e_sink * inv


def _kv_rows(p_ref, c_ref, hkv):
    c0 = hkv * HEAD_DIM
    return jnp.concatenate([p_ref[:, c0:c0 + HEAD_DIM], c_ref[:, c0:c0 + HEAD_DIM]],
                           axis=0).astype(BF16)


ATT_Q = 2


def _attn_in_specs(bias2):
    prev = lambda n: jnp.maximum(ATT_Q * n - 1, 0)
    kcol = ATTN_W // KV_W
    return [
        pl.BlockSpec((ATT_Q * BLK, ATTN_W), lambda n: (n, 0)),
        pl.BlockSpec((BLK, KV_W), lambda n: (prev(n), kcol)),
        pl.BlockSpec((ATT_Q * BLK, KV_W), lambda n: (n, kcol)),
        pl.BlockSpec((BLK, KV_W), lambda n: (prev(n), kcol + 1)),
        pl.BlockSpec((ATT_Q * BLK, KV_W), lambda n: (n, kcol + 1)),
        pl.BlockSpec(bias2.shape, lambda n: (0, 0, 0)),
    ]


def _attn_views(t, q_ref, kp_ref, kc_ref, vp_ref, vc_ref, bias_ref):
    rows = pl.ds(t * BLK, BLK)
    before = pl.ds((t - 1) * BLK, BLK)
    table = jnp.minimum(pl.program_id(0), 1) if t == 0 else 1
    return (q_ref.at[rows, :],
            kp_ref if t == 0 else kc_ref.at[before, :], kc_ref.at[rows, :],
            vp_ref if t == 0 else vc_ref.at[before, :], vc_ref.at[rows, :],
            bias_ref.at[table])


def _attention_fwd(proj, bias2, sinkcol, n_rows):
    steps = n_rows // (ATT_Q * BLK)

    def body(q_ref, kp_ref, kc_ref, vp_ref, vc_ref, bias_ref, sink_ref, o_ref):
        views = [_attn_views(t, q_ref, kp_ref, kc_ref, vp_ref, vc_ref, bias_ref) for t in range(ATT_Q)]
        work = [(t, hkv) for t in range(ATT_Q) for hkv in range(N_KV_HEADS)]
        scores = {w: _attn_scores(views[w[0]][0], views[w[0]][1], views[w[0]][2], w[1])[2] for w in work}
        probs = {w: _attn_softmax(scores[w], views[w[0]][5], sink_ref, w[1])[0] for w in work}
        outs = {w: jnp.dot(probs[w].astype(BF16), _kv_rows(views[w[0]][3], views[w[0]][4], w[1]),
                           preferred_element_type=F32) for w in work}
        for t, hkv in work:
            for g in range(GROUP):
                h = hkv * GROUP + g
                o_ref[t * BLK:(t + 1) * BLK, h * HEAD_DIM:(h + 1) * HEAD_DIM] = (
                    outs[t, hkv][g * BLK:(g + 1) * BLK, :].astype(o_ref.dtype))

    return pl.pallas_call(
        body,
        grid=(steps,),
        in_specs=_attn_in_specs(bias2) + [pl.BlockSpec(sinkcol.shape, lambda n: (0, 0))],
        out_specs=pl.BlockSpec((ATT_Q * BLK, ATTN_W), lambda n: (n, 0)),
        out_shape=jax.ShapeDtypeStruct((n_rows, ATTN_W), BF16),
        compiler_params=_cparams(("parallel",)),
        name="attn_fwd",
    )(proj, proj, proj, proj, proj, bias2, sinkcol)


def _attention_bwd(proj, attn, dattn, bias2, sinkcol, n_rows):
    steps = n_rows // (ATT_Q * BLK)
    scale = HEAD_DIM ** -0.5
    dn_t = (((0,), (0,)), ((), ()))

    def body(q_ref, kp_ref, kc_ref, vp_ref, vc_ref, bias_ref, o_ref, do_ref, sink_ref,
             dq_ref, dkc_ref, dkp_ref, dvc_ref, dvp_ref, dbias_ref, dsink_ref):
        @pl.when(pl.program_id(0) == 0)
        def _():
            dbias_ref[...] = jnp.zeros(dbias_ref.shape, F32)
            dsink_ref[...] = jnp.zeros(dsink_ref.shape, F32)

        views = [_attn_views(t, q_ref, kp_ref, kc_ref, vp_ref, vc_ref, bias_ref) for t in range(ATT_Q)]
        work = [(t, hkv) for t in range(ATT_Q) for hkv in range(N_KV_HEADS)]
        qk = {w: _attn_scores(views[w[0]][0], views[w[0]][1], views[w[0]][2], w[1]) for w in work}
        dog, dps, deltas = {}, {}, {}
        for t, hkv in work:
            rows = slice(t * BLK, (t + 1) * BLK)
            hs = [hkv * GROUP + g for g in range(GROUP)]
            d_o = jnp.concatenate([do_ref[rows, h * HEAD_DIM:(h + 1) * HEAD_DIM] for h in hs], axis=0)
            o = jnp.concatenate([o_ref[rows, h * HEAD_DIM:(h + 1) * HEAD_DIM] for h in hs], axis=0)
            deltas[t, hkv] = jnp.sum(d_o.astype(F32) * o.astype(F32), axis=-1, keepdims=True)
            dog[t, hkv] = d_o.astype(BF16)
            dps[t, hkv] = lax.dot_general(dog[t, hkv], _kv_rows(views[t][3], views[t][4], hkv),
                                          (((1,), (1,)), ((), ())), preferred_element_type=F32)
        p16, ds16 = {}, {}
        for t, hkv in work:
            r0, r1 = hkv * GROUP * BLK, (hkv + 1) * GROUP * BLK
            p, p_sink = _attn_softmax(qk[t, hkv][2], views[t][5], sink_ref, hkv)
            ds = p * (dps[t, hkv] - deltas[t, hkv])
            dbias_ref[r0:r1, :] += ds
            dsink_ref[r0:r1, :] += -(p_sink * deltas[t, hkv])
            p16[t, hkv] = p.astype(BF16)
            ds16[t, hkv] = ds.astype(BF16)
        for t, hkv in work:
            rows = slice(t * BLK, (t + 1) * BLK)
            c0 = hkv * HEAD_DIM
            qg, kk, _ = qk[t, hkv]
            dqg = jnp.dot(ds16[t, hkv], kk, preferred_element_type=F32) * scale
            dkk = lax.dot_general(ds16[t, hkv], qg, dn_t, preferred_element_type=F32) * scale
            dvv = lax.dot_general(p16[t, hkv], dog[t, hkv], dn_t, preferred_element_type=F32)
            for g in range(GROUP):
                h = hkv * GROUP + g
                dq_ref[rows, h * HEAD_DIM:(h + 1) * HEAD_DIM] = (
                    dqg[g * BLK:(g + 1) * BLK, :].astype(dq_ref.dtype))
            dkp_ref[rows, c0:c0 + HEAD_DIM] = dkk[:BLK].astype(dkp_ref.dtype)
            dkc_ref[rows, c0:c0 + HEAD_DIM] = dkk[BLK:].astype(dkc_ref.dtype)
            dvp_ref[rows, c0:c0 + HEAD_DIM] = dvv[:BLK].astype(dvp_ref.dtype)
            dvc_ref[rows, c0:c0 + HEAD_DIM] = dvv[BLK:].astype(dvc_ref.dtype)

    wide = pl.BlockSpec((ATT_Q * BLK, ATTN_W), lambda n: (n, 0))
    kv_out = pl.BlockSpec((ATT_Q * BLK, KV_W), lambda n: (n, 0))
    kv_shape = jax.ShapeDtypeStruct((n_rows, KV_W), F32)
    acc_shape = bias2.shape[1:]
    return pl.pallas_call(
        body,
        grid=(steps,),
        in_specs=_attn_in_specs(bias2) + [wide, wide, pl.BlockSpec(sinkcol.shape, lambda n: (0, 0))],
        out_specs=[
            wide, kv_out, kv_out, kv_out, kv_out,
            pl.BlockSpec(acc_shape, lambda n: (0, 0)),
            pl.BlockSpec(sinkcol.shape, lambda n: (0, 0)),
        ],
        out_shape=[
            jax.ShapeDtypeStruct((n_rows, ATTN_W), BF16),
            kv_shape, kv_shape, kv_shape, kv_shape,
            jax.ShapeDtypeStruct(acc_shape, F32),
            jax.ShapeDtypeStruct(sinkcol.shape, F32),
        ],
        compiler_params=_cparams(("arbitrary",)),
        name="attn_bwd",
    )(proj, proj, proj, proj, proj, bias2, attn, dattn, sinkcol)


def _bias_tables(rel_bias_t, onehot_t, band_first, band_rest):
    def body(rb_ref, oh_ref, mf_ref, mr_ref, out_ref):
        acc = jnp.zeros((N_Q_HEADS, BLK * 2 * BLK), F32)
        for part in _split3(rb_ref[...]):
            acc = acc + jnp.dot(part, oh_ref[...], preferred_element_type=F32)
        out_ref[0] = jnp.where(mf_ref[...] > 0.0, acc, NEG_INF)
        out_ref[1] = jnp.where(mr_ref[...] > 0.0, acc, NEG_INF)

    return pl.pallas_call(
        body,
        out_shape=jax.ShapeDtypeStruct((2, N_Q_HEADS, BLK * 2 * BLK), F32),
        compiler_params=pltpu.CompilerParams(vmem_limit_bytes=VMEM_LIMIT),
        name="bias_tables",
    )(rel_bias_t, onehot_t, band_first, band_rest)


def _split3(a):
    hi = a.astype(BF16)
    r1 = a - hi.astype(F32)
    mid = r1.astype(BF16)
    lo = (r1 - mid.astype(F32)).astype(BF16)
    return hi, mid, lo


def _bucket_reduce(dbias, dsink, onehot_t):
    def body(db_ref, ds_ref, oh_ref, ob_ref, os_ref):
        acc = jnp.zeros((N_Q_HEADS, 128), F32)
        for part in _split3(db_ref[...]):
            acc = acc + lax.dot_general(part, oh_ref[...], (((1,), (1,)), ((), ())),
                                        preferred_element_type=F32)
        ob_ref[...] = acc
        os_ref[...] = jnp.broadcast_to(jnp.sum(ds_ref[...], axis=-1, keepdims=True),
                                       os_ref.shape)

    return pl.pallas_call(
        body,
        out_shape=[jax.ShapeDtypeStruct((N_Q_HEADS, 128), F32),
                   jax.ShapeDtypeStruct((N_Q_HEADS, 128), F32)],
        compiler_params=pltpu.CompilerParams(vmem_limit_bytes=VMEM_LIMIT),
        name="bias_bucket_reduce",
    )(dbias, dsink, onehot_t)


def _disc(lr, li, ls, btr, bti):
    lam_re = jnp.minimum(lr, -1e-4)
    delta = jnp.exp(ls)
    mag = jnp.exp(lam_re * delta)
    ang = li * delta
    ar, ai = mag * jnp.cos(ang), mag * jnp.sin(ang)
    nr, ni = ar - 1.0, ai
    den = lam_re * lam_re + li * li
    fr = (nr * lam_re + ni * li) / den
    fi = (ni * lam_re - nr * li) / den
    bbr = fr * btr - fi * bti
    bbi = fr * bti + fi * btr
    return ar, ai, bbr, bbi


def _block_mask():
    row = lax.broadcasted_iota(jnp.int32, (SSM_W, SSM_H), 0)
    col = lax.broadcasted_iota(jnp.int32, (SSM_W, SSM_H), 1)
    return (row // SSM_P) == (col // SSM_N)


def _ssm_setup(lr, li, ls, btr, bti, ctr, cti):
    def body(lr_ref, li_ref, ls_ref, btr_ref, bti_ref, ctr_ref, cti_ref, a_ref, b_ref, c_ref):
        ar, ai, bbr, bbi = _disc(lr_ref[...], li_ref[...], ls_ref[...], btr_ref[...], bti_ref[...])
        a_ref[:, :SSM_H] = ar
        a_ref[:, SSM_H:] = ai
        mask = _block_mask()
        blk = lambda t: jnp.where(mask, jnp.tile(t, (SSM_G, 1)), 0.0)
        b_ref[:, :SSM_H] = blk(bbr).astype(BF16)
        b_ref[:, SSM_H:] = blk(bbi).astype(BF16)
        c_ref[:, :SSM_H] = blk(ctr_ref[...]).astype(BF16)
        c_ref[:, SSM_H:] = blk(-cti_ref[...]).astype(BF16)

    return pl.pallas_call(
        body,
        out_shape=[jax.ShapeDtypeStruct((1, 2 * SSM_H), F32),
                   jax.ShapeDtypeStruct((SSM_W, 2 * SSM_H), BF16),
                   jax.ShapeDtypeStruct((SSM_W, 2 * SSM_H), BF16)],
        compiler_params=pltpu.CompilerParams(vmem_limit_bytes=VMEM_LIMIT),
        name="ssm_setup",
    )(lr, li, ls, btr, bti, ctr, cti)


def _ssm_param_bwd(lr, li, ls, btr, bti, dacc, dbcat, dccat, gind):
    def body(lr_ref, li_ref, ls_ref, btr_ref, bti_ref, dacc_ref, db_ref, dc_ref, g_ref,
             dlr_ref, dli_ref, dls_ref, dbtr_ref, dbti_ref, dctr_ref, dcti_ref):
        dar = jnp.sum(dacc_ref[:, :SSM_H], axis=0, keepdims=True)
        dai = jnp.sum(dacc_ref[:, SSM_H:], axis=0, keepdims=True)
        col = lax.broadcasted_iota(jnp.int32, (SSM_P, 2 * SSM_H), 1)
        grp = (col % SSM_H) // SSM_N
        db = jnp.zeros((SSM_P, 2 * SSM_H), F32)
        dc = jnp.zeros((SSM_P, 2 * SSM_H), F32)
        half = SSM_G // 2
        for g in range(SSM_G):
            sel = grp == g
            r0 = (g % half) * SSM_P
            db = db + jnp.where(sel, db_ref[r0:r0 + SSM_P, :], 0.0)
            dc = dc + jnp.where(sel, dc_ref[r0:r0 + SSM_P, :], 0.0)
        dctr_ref[...] = dc[:, :SSM_H]
        dcti_ref[...] = -dc[:, SSM_H:]
        prim = (lr_ref[...], li_ref[...], ls_ref[...], btr_ref[...], bti_ref[...])
        _, vjp = jax.vjp(_disc, *prim)
        dlr, dli, dls, dbtr, dbti = vjp((dar, dai, db[:, :SSM_H], db[:, SSM_H:]))
        dlr_ref[...] = dlr
        dli_ref[...] = dli
        dbtr_ref[...] = dbtr
        dbti_ref[...] = dbti
        acc = jnp.zeros((8, 128), F32)
        for part in _split3(jnp.broadcast_to(dls, (8, SSM_H))):
            acc = acc + jnp.dot(part, g_ref[...], preferred_element_type=F32)
        dls_ref[...] = acc

    vec = jax.ShapeDtypeStruct((1, SSM_H), F32)
    mat = jax.ShapeDtypeStruct((SSM_P, SSM_H), F32)
    return pl.pallas_call(
        body,
        out_shape=[vec, vec, jax.ShapeDtypeStruct((8, 128), F32), mat, mat, mat, mat],
        compiler_params=pltpu.CompilerParams(vmem_limit_bytes=VMEM_LIMIT),
        name="ssm_param_bwd",
    )(lr, li, ls, btr, bti, dacc, dbcat, dccat, gind)


SCAN_TR = 256


def _cmul_add(vr, vi, pr, pi, sr, si):
    return vr + pr * sr - pi * si, vi + pr * si + pi * sr


def _bcast_row(v, row, which):
    return jnp.broadcast_to(v[which:which + 1, :], v.shape)


def _scan_tables(a_ref, tab_ref, reverse):
    H = SSM_H
    ar = jnp.broadcast_to(a_ref[:, :H], (8, H))
    ai = jnp.broadcast_to(a_ref[:, H:], (8, H))
    if reverse:
        ai = -ai
    row = lax.broadcasted_iota(jnp.int32, (8, H), 0)
    pw = [(ar, ai)]
    for _ in range(7):
        cr, ci = pw[-1]
        pw.append((cr * ar - ci * ai, cr * ai + ci * ar))
    pcr = jnp.zeros((8, H), F32)
    pci = jnp.zeros((8, H), F32)
    for e in range(8):
        sel = (row == (7 - e)) if reverse else (row == e)
        pcr = jnp.where(sel, pw[e][0], pcr)
        pci = jnp.where(sel, pw[e][1], pci)
    tab_ref[0, :, :H] = pcr
    tab_ref[0, :, H:] = pci
    for t, k in enumerate((1, 2, 4)):
        keep = (row < 8 - k) if reverse else (row >= k)
        tab_ref[1 + t, :, :H] = jnp.where(keep, pw[k - 1][0], 0.0)
        tab_ref[1 + t, :, H:] = jnp.where(keep, pw[k - 1][1], 0.0)


def _scan_group(vr, vi, cr, ci, tab_ref, reverse):
    H = SSM_H
    for t, k in enumerate((1, 2, 4)):
        sh = 8 - k if reverse else k
        vr, vi = _cmul_add(vr, vi, tab_ref[1 + t, :, :H], tab_ref[1 + t, :, H:],
                           pltpu.roll(vr, sh, 0), pltpu.roll(vi, sh, 0))
    return _cmul_add(vr, vi, tab_ref[0, :, :H], tab_ref[0, :, H:], cr, ci)


def _blockdiag_expand(x, w_ref, out_ref):
    hw, cb = SSM_W // 2, SSM_H // 2
    for j in range(4):
        h = j % 2
        out_ref[:, j * cb:(j + 1) * cb] = jnp.dot(
            x[:, h * hw:(h + 1) * hw], w_ref[h * hw:(h + 1) * hw, j * cb:(j + 1) * cb],
            preferred_element_type=F32)


def _blockdiag_contract(x_ref, w_ref):
    hw, cb = SSM_W // 2, SSM_H // 2
    nt = (((1,), (1,)), ((), ()))
    halves = []
    for h in range(2):
        acc = None
        for j in (h, 2 + h):
            part = lax.dot_general(x_ref[:, j * cb:(j + 1) * cb],
                                   w_ref[h * hw:(h + 1) * hw, j * cb:(j + 1) * cb], nt,
                                   preferred_element_type=F32)
            acc = part if acc is None else acc + part
        halves.append(acc)
    return jnp.concatenate(halves, axis=1)


def _scan_fwd(proj, u_blk, bcat, ccat, abar, n_rows):
    H = SSM_H
    nt = n_rows // SCAN_TR

    def body(u_ref, b_ref, c_ref, a_ref, xs_ref, xp_ref, yc_ref, bu_ref, tab_ref, carry_ref):
        @pl.when(pl.program_id(0) == 0)
        def _():
            _scan_tables(a_ref, tab_ref, False)
            carry_ref[...] = jnp.zeros(carry_ref.shape, F32)

        _blockdiag_expand(u_ref[...].astype(BF16), b_ref, bu_ref)
        row = lax.broadcasted_iota(jnp.int32, (8, H), 0)

        def group(j, carry):
            cr, ci = carry
            r0 = pl.multiple_of(j * 16, 16)
            xr, xi = [], []
            for half in range(2):
                rr = pl.multiple_of(r0 + 8 * half, 8)
                vr, vi = _scan_group(bu_ref[pl.ds(rr, 8), :H], bu_ref[pl.ds(rr, 8), H:],
                                     cr, ci, tab_ref, False)
                xp_ref[pl.ds(rr, 8), :H] = jnp.where(row == 0, cr, pltpu.roll(vr, 1, 0))
                xp_ref[pl.ds(rr, 8), H:] = jnp.where(row == 0, ci, pltpu.roll(vi, 1, 0))
                cr, ci = _bcast_row(vr, row, 7), _bcast_row(vi, row, 7)
                xr.append(vr)
                xi.append(vi)
            xs_ref[pl.ds(r0, 16), :H] = jnp.concatenate(xr, axis=0).astype(BF16)
            xs_ref[pl.ds(r0, 16), H:] = jnp.concatenate(xi, axis=0).astype(BF16)
            return cr, ci

        cr, ci = lax.fori_loop(0, SCAN_TR // 16, group,
                               (carry_ref[:, :H], carry_ref[:, H:]))
        carry_ref[:, :H] = cr
        carry_ref[:, H:] = ci
        yc_ref[...] = _blockdiag_contract(xs_ref, c_ref)

    tile = lambda w: pl.BlockSpec((SCAN_TR, w), lambda i: (i, 0))
    whole = lambda a: pl.BlockSpec(a.shape, lambda i: (0, 0))
    return pl.pallas_call(
        body,
        grid=(nt,),
        in_specs=[pl.BlockSpec((SCAN_TR, SSM_W), lambda i: (i, u_blk)),
                  whole(bcat), whole(ccat), whole(abar)],
        out_specs=[tile(2 * H), tile(2 * H), tile(SSM_W)],
        out_shape=[jax.ShapeDtypeStruct((n_rows, 2 * H), BF16),
                   jax.ShapeDtypeStruct((n_rows, 2 * H), F32),
                   jax.ShapeDtypeStruct((n_rows, SSM_W), F32)],
        scratch_shapes=[pltpu.VMEM((SCAN_TR, 2 * H), F32), pltpu.VMEM((4, 8, 2 * H), F32),
                        pltpu.VMEM((8, 2 * H), F32)],
        compiler_params=_cparams(("arbitrary",)),
        name="ssm_scan_fwd",
    )(proj, bcat, ccat, abar)


def _scan_bwd(dy, xprev, bcat, ccat, abar, n_rows):
    H = SSM_H
    nt = n_rows // SCAN_TR

    def body(dy_ref, xp_ref, b_ref, c_ref, a_ref, h_ref, da_ref, du_ref, g_ref, tab_ref, carry_ref):
        @pl.when(pl.program_id(0) == 0)
        def _():
            _scan_tables(a_ref, tab_ref, True)
            carry_ref[...] = jnp.zeros(carry_ref.shape, F32)
            da_ref[...] = jnp.zeros(da_ref.shape, F32)

        _blockdiag_expand(dy_ref[...], c_ref, g_ref)
        row = lax.broadcasted_iota(jnp.int32, (8, H), 0)
        n16 = SCAN_TR // 16

        def group(jj, carry):
            cr, ci = carry
            r0 = pl.multiple_of((n16 - 1 - jj) * 16, 16)
            hr, hi = [None, None], [None, None]
            for half in (1, 0):
                rr = pl.multiple_of(r0 + 8 * half, 8)
                vr, vi = _scan_group(g_ref[pl.ds(rr, 8), :H], g_ref[pl.ds(rr, 8), H:],
                                     cr, ci, tab_ref, True)
                pr, pi = xp_ref[pl.ds(rr, 8), :H], xp_ref[pl.ds(rr, 8), H:]
                da_ref[:, :H] += vr * pr + vi * pi
                da_ref[:, H:] += vi * pr - vr * pi
                cr, ci = _bcast_row(vr, row, 0), _bcast_row(vi, row, 0)
                hr[half], hi[half] = vr, vi
            h_ref[pl.ds(r0, 16), :H] = jnp.concatenate(hr, axis=0).astype(BF16)
            h_ref[pl.ds(r0, 16), H:] = jnp.concatenate(hi, axis=0).astype(BF16)
            return cr, ci

        cr, ci = lax.fori_loop(0, n16, group, (carry_ref[:, :H], carry_ref[:, H:]))
        carry_ref[:, :H] = cr
        carry_ref[:, H:] = ci
        du_ref[...] = _blockdiag_contract(h_ref, b_ref)

    rev = lambda i: (nt - 1 - i, 0)
    whole = lambda a: pl.BlockSpec(a.shape, lambda i: (0, 0))
    return pl.pallas_call(
        body,
        grid=(nt,),
        in_specs=[pl.BlockSpec((SCAN_TR, SSM_W), rev),
                  pl.BlockSpec((SCAN_TR, 2 * H), rev),
                  whole(bcat), whole(ccat), whole(abar)],
        out_specs=[pl.BlockSpec((SCAN_TR, 2 * H), rev),
                   pl.BlockSpec((8, 2 * H), lambda i: (0, 0)),
                   pl.BlockSpec((SCAN_TR, SSM_W), rev)],
        out_shape=[jax.ShapeDtypeStruct((n_rows, 2 * H), BF16),
                   jax.ShapeDtypeStruct((8, 2 * H), F32),
                   jax.ShapeDtypeStruct((n_rows, SSM_W), F32)],
        scratch_shapes=[pltpu.VMEM((SCAN_TR, 2 * H), F32), pltpu.VMEM((4, 8, 2 * H), F32),
                        pltpu.VMEM((8, 2 * H), F32)],
        compiler_params=_cparams(("arbitrary",)),
        name="ssm_scan_bwd",
    )(dy, xprev, bcat, ccat, abar)


def _adamw(parts, w, m, v, *, tr, ch, name, prefetch=None):
    n_rows, cols = w.shape
    n_parts = len(parts)
    c1 = 1.0 - ADAM_B1 ** ADAM_STEP
    c2 = 1.0 - ADAM_B2 ** ADAM_STEP

    def fn(rv, vv, i, nt):
        g = rv[0].astype(F32)
        for p in rv[1:n_parts]:
            g = g + p.astype(F32)
        wv, mv, vval = rv[n_parts:]
        nm = ADAM_B1 * mv + (1.0 - ADAM_B1) * g
        nv = ADAM_B2 * vval + (1.0 - ADAM_B2) * (g * g)
        delta = -ADAM_LR * ((nm / c1) / (jnp.sqrt(nv / c2) + ADAM_EPS) + ADAM_WD * wv)
        return [g, delta, nm, nv], []

    rows = [_row(arr, lead=lead) for (arr, lead) in parts] + [_row(w), _row(m), _row(v)]
    return _rowwise(fn, rows, [], [(cols, F32)] * 4, [], n_rows=n_rows, tr=tr, ch=ch, name=name,
                    prefetch=prefetch)


_PACK = [
    ("b_ada", 6), ("norm1_g", 1), ("b_in", 3), ("norm2_g", 1), ("final_g", 1),
    ("lambda_re", 1), ("lambda_im", 1), ("log_step", 1), ("attn_sinks", 1),
    ("rel_bias", 1), ("b_glu", 1), ("ssm_d", 1), ("loss", 1),
    ("ssm_b_re", 16), ("ssm_b_im", 16), ("ssm_c_re", 16), ("ssm_c_im", 16),
]
_PACK_OFF = {}
_off = 0
for _n, _r in _PACK:
    _PACK_OFF[_n] = (_off, _r)
    _off += _r
PACK_ROWS = -(-_off // 8) * 8


def _to_rows(a, rows):
    flat = a.reshape(-1).astype(F32)
    pad = rows * PACK_W - flat.shape[0]
    if pad:
        flat = jnp.pad(flat, (0, pad))
    return flat.reshape(rows, PACK_W)


def _b_to_rows(b):
    return jnp.transpose(b, (2, 0, 1)).reshape(SSM_P, SSM_H)


def _rows_to_b(r):
    return jnp.transpose(r.reshape(SSM_P, SSM_G, SSM_N), (1, 2, 0))


def _c_to_rows(cm):
    return jnp.transpose(cm, (1, 0, 2)).reshape(SSM_P, SSM_H)


def _rows_to_c(r):
    return jnp.transpose(r.reshape(SSM_P, SSM_G, SSM_N), (1, 0, 2))


def _pack(vals):
    out = jnp.zeros((PACK_ROWS, PACK_W), F32)
    for n, r in _PACK:
        if n in vals:
            pieces = vals[n] if isinstance(vals[n], list) else [vals[n]]
            rows_each = r // len(pieces)
            for i, piece in enumerate(pieces):
                out = lax.dynamic_update_slice(out, _to_rows(piece, rows_each),
                                               (_PACK_OFF[n][0] + i * rows_each, 0))
    return out


def _unpack(packed, name, shape):
    o, r = _PACK_OFF[name]
    n = int(np.prod(shape))
    return packed[o:o + r].reshape(-1)[:n].reshape(shape)


def _small_params_packed(p):
    return {
        "b_ada": p["b_ada"], "norm1_g": p["norm1_g"], "b_in": p["b_in"],
        "norm2_g": p["norm2_g"], "final_g": p["final_g"],
        "lambda_re": p["lambda_re"], "lambda_im": p["lambda_im"],
        "log_step": p["log_step"], "attn_sinks": p["attn_sinks"],
        "rel_bias": p["rel_bias"], "b_glu": p["b_glu"], "ssm_d": p["ssm_d"],
        "ssm_b_re": _b_to_rows(p["ssm_b_re"][0]), "ssm_b_im": _b_to_rows(p["ssm_b_im"][0]),
        "ssm_c_re": _c_to_rows(p["ssm_c_re"][0]), "ssm_c_im": _c_to_rows(p["ssm_c_im"][0]),
    }


_SMALL_SHAPES = {
    "b_ada": (1, N_MOD * D), "norm1_g": (1, D), "b_in": (1, IN_W), "norm2_g": (1, D),
    "final_g": (D,), "lambda_re": (1, SSM_G, SSM_N), "lambda_im": (1, SSM_G, SSM_N),
    "log_step": (1, SSM_G), "attn_sinks": (1, N_Q_HEADS), "rel_bias": (NUM_BUCKETS, N_Q_HEADS),
    "b_glu": (1, SSM_W), "ssm_d": (1, SSM_W),
}


def _unpack_small(packed, name):
    if name in ("ssm_b_re", "ssm_b_im"):
        o, r = _PACK_OFF[name]
        return _rows_to_b(packed[o:o + r])[None]
    if name in ("ssm_c_re", "ssm_c_im"):
        o, r = _PACK_OFF[name]
        return _rows_to_c(packed[o:o + r])[None]
    return _unpack(packed, name, _SMALL_SHAPES[name])


WEIGHT_ORDER = ['w_ada', 'b_ada', 'norm1_g', 'w_in', 'b_in', 'attn_sinks', 'rel_bias', 'lambda_re',
                'lambda_im', 'log_step', 'ssm_b_re', 'ssm_b_im', 'ssm_c_re', 'ssm_c_im', 'ssm_d',
                'w_glu', 'b_glu', 'w_attn_proj', 'w_ssm_proj', 'w_out', 'norm2_g', 'w_ff1', 'w_ff2',
                'final_g']
BIG = ['w_in', 'w_glu', 'w_attn_proj', 'w_ssm_proj', 'w_out', 'w_ff1', 'w_ff2']


ADAMW_TILE_ELEMS = 1 << 18


def _to_col_blocks(w):
    k, n = w.shape
    return jnp.transpose(w.reshape(k, N_DEV, n // N_DEV), (1, 0, 2))


def _adamw_rows(rows, cols):
    tr = rows
    while tr * cols > ADAMW_TILE_ELEMS and tr % 32 == 0:
        tr //= 2
    return tr


def _cast_to_slot(w, me1, name, dep=None):
    rows, cols = w.shape
    tr = min(rows, 256)
    n_dep = 0 if dep is None else 1

    def body(me_ref, w_ref, *rest):
        rest[-1][...] = w_ref[...].astype(BF16)

    return pl.pallas_call(
        body,
        grid_spec=pltpu.PrefetchScalarGridSpec(
            num_scalar_prefetch=1, grid=(rows // tr,),
            in_specs=[pl.BlockSpec((tr, cols), lambda i, me_ref: (i, 0))]
            + [pl.BlockSpec(memory_space=pl.ANY)] * n_dep,
            out_specs=pl.BlockSpec((None, tr, cols), lambda i, me_ref: (me_ref[0], i, 0))),
        out_shape=jax.ShapeDtypeStruct((N_DEV, rows, cols), BF16),
        compiler_params=_cparams(("arbitrary",)),
        name=name,
    )(me1, w, *([dep] if n_dep else []))


def kernel(x, c, w_ada, b_ada, norm1_g, w_in, b_in, attn_sinks, rel_bias, lambda_re, lambda_im, log_step, ssm_b_re, ssm_b_im, ssm_c_re, ssm_c_im, ssm_d, w_glu, b_glu, w_attn_proj, w_ssm_proj, w_out, norm2_g, w_ff1, w_ff2, final_g, loss_target, m_w_ada, m_b_ada, m_norm1_g, m_w_in, m_b_in, m_attn_sinks, m_rel_bias, m_lambda_re, m_lambda_im, m_log_step, m_ssm_b_re, m_ssm_b_im, m_ssm_c_re, m_ssm_c_im, m_ssm_d, m_w_glu, m_b_glu, m_w_attn_proj, m_w_ssm_proj, m_w_out, m_norm2_g, m_w_ff1, m_w_ff2, m_final_g, v_w_ada, v_b_ada, v_norm1_g, v_w_in, v_b_in, v_attn_sinks, v_rel_bias, v_lambda_re, v_lambda_im, v_log_step, v_ssm_b_re, v_ssm_b_im, v_ssm_c_re, v_ssm_c_im, v_ssm_d, v_w_glu, v_b_glu, v_w_attn_proj, v_w_ssm_proj, v_w_out, v_norm2_g, v_w_ff1, v_w_ff2, v_final_g):
    loc = dict(locals())
    W = {n: loc[n] for n in WEIGHT_ORDER}
    Mo = {n: loc["m_" + n] for n in WEIGHT_ORDER}
    Vo = {n: loc["v_" + n] for n in WEIGHT_ORDER}
    S = x.shape[1]
    TM = min(512, S)
    TS = min(1024, S)
    TR = min(256, S)
    TW = min(1024, S)
    me = 4 * lax.axis_index("x") + 2 * lax.axis_index("y") + lax.axis_index("c")
    x2d = x.reshape(S, D)
    tgt = loss_target.reshape(S, D)

    c_all = _small_allgather(c, "allgather_c").reshape(N_DEV, D)
    cs = _rowwise(lambda rv, vv, i, nt: ([rv[0] * _sigmoid(rv[0])], []), [_row(c_all)], [],
                  [(D, F32)], [], n_rows=N_DEV, tr=8, ch=8, name="silu_c")[0]
    n_ada = N_MOD * D // N_DEV
    b_ada_cols = lax.dynamic_slice(b_ada, (0, me * n_ada), (1, n_ada))
    mod_piece = _matmul(cs, w_ada[0], mode="nn", dims=(N_DEV, n_ada, D), tiles=(N_DEV, 512, D),
                        out_dtypes=[F32], name="ada_fwd", bias=b_ada_cols)
    mod_all = _small_allgather(mod_piece, "allgather_mod")
    mod_b = lax.dynamic_index_in_dim(mod_all, me, axis=1, keepdims=False).reshape(N_MOD, D)
    sh1, sc1, g1, sh2, sc2, g2 = [mod_b[i:i + 1] for i in range(N_MOD)]

    shard = {n: W[n][0] for n in BIG}
    me1 = jnp.reshape(me, (1,)).astype(jnp.int32)
    zone = {"w_in": _cast_to_slot(shard["w_in"], me1, "cast_w_in")}
    (in_flight,), tok_in = _relay_gather_start([zone["w_in"]], "w_in_start", mod_all)
    for n in BIG[1:]:
        zone[n] = _cast_to_slot(shard[n], me1, "cast_" + n, dep=tok_in)
    G = {}

    def f_norm1(rv, vv, i, nt):
        xv, (g, sc, sh) = rv[0], vv
        return [(xv * _rms(xv) * g) * (1.0 + sc) + sh], []

    h = _rowwise(f_norm1, [_row(x2d)], [norm1_g, sc1, sh1], [(D, BF16)], [],
                 n_rows=S, tr=TR, ch=32, name="norm1_fwd", dep=zone["w_ff2"])[0]
    (zone_in,) = _relay_gather_arrive([in_flight], h, "w_in_arrive")
    (in_pass,), tok_p = _relay_pass_start([zone_in], "w_in_pass_start")
    mixer = ["w_attn_proj", "w_glu", "w_ssm_proj", "w_out"]
    flights, tok_w = _exchange_start([zone[n] for n in mixer], "gather", "weights_start", tok_p)
    w_flight = dict(zip(mixer, flights))
    ff_flights, tok_w = _relay_gather_start([zone["w_ff1"], zone["w_ff2"]], "ff_weights_start", tok_w)
    (G["w_in"],) = _relay_pass_wait([in_pass], tok_w, "w_in_pass_wait")
    proj = _matmul(h, G["w_in"], mode="nn", dims=(S, IN_W, D), tiles=(TM, 768, D),
                   out_dtypes=[BF16], name="in_proj", b3=True, bias=b_in, dep=tok_w)

    buckets = _t5_buckets_block()
    band = _band_mask()
    onehot_t = jnp.asarray(
        (np.arange(128)[:, None] == buckets.reshape(-1)[None, :]).astype(np.float32), BF16)
    band_first = band & (np.arange(2 * BLK)[None, :] >= BLK)
    rel_bias_t = jnp.pad(jnp.transpose(rel_bias), ((0, 0), (0, 128 - NUM_BUCKETS)))
    bias2 = _bias_tables(rel_bias_t, onehot_t,
                         jnp.asarray(band_first.reshape(1, -1).astype(np.float32)),
                         jnp.asarray(band.reshape(1, -1).astype(np.float32))
                         ).reshape(2, N_Q_HEADS * BLK, 2 * BLK)
    sinkcol = jnp.repeat(attn_sinks.reshape(N_Q_HEADS), BLK).reshape(N_Q_HEADS * BLK, 1)
    attn = _attention_fwd(proj, bias2, sinkcol, S)
    landed = _exchange_wait([w_flight[n] for n in mixer], "gather", attn, "weights_wait_mixer")
    G.update((n, bufs[0]) for n, bufs in zip(mixer, landed))
    w_glu_f = G["w_glu"].reshape(SSM_W, SSM_W)
    w_out_f = G["w_out"].reshape(D, D)
    w_ap_f = jnp.transpose(G["w_attn_proj"], (1, 0, 2)).reshape(ATTN_W, D)
    w_sp_f = jnp.transpose(G["w_ssm_proj"], (1, 0, 2)).reshape(SSM_W, D)
    y_attn = _matmul(attn, w_ap_f, mode="nn", dims=(S, D, ATTN_W), tiles=(TW, 1024, ATTN_W),
                     out_dtypes=[BF16], name="attn_proj")

    lam_re = lambda_re.reshape(1, SSM_H)
    lam_im = lambda_im.reshape(1, SSM_H)
    ls_x = jnp.repeat(log_step.reshape(SSM_G), SSM_N).reshape(1, SSM_H)
    btr, bti = _b_to_rows(ssm_b_re[0]), _b_to_rows(ssm_b_im[0])
    ctr, cti = _c_to_rows(ssm_c_re[0]), _c_to_rows(ssm_c_im[0])
    abar, bcat, ccat = _ssm_setup(lam_re, lam_im, ls_x, btr, bti, ctr, cti)
    u_blk = (ATTN_W + 2 * KV_W) // SSM_W
    xs, xprev, yc = _scan_fwd(proj, u_blk, bcat, ccat, abar, S)

    def f_ssm_out(rv, vv, i, nt):
        y = rv[0] + vv[0] * rv[1]
        return [y, _gelu(y)], []

    y_ssm_pre, z = _rowwise(f_ssm_out, [_row(yc), _row(proj, u_blk, SSM_W)], [ssm_d],
                            [(SSM_W, F32), (SSM_W, BF16)], [], n_rows=S, tr=TM, ch=32, name="ssm_out")
    zg = _matmul(z, w_glu_f, mode="nn", dims=(S, SSM_W, SSM_W), tiles=(TM, SSM_W, SSM_W),
                 out_dtypes=[F32], name="glu_proj", bias=b_glu)
    z2 = _rowwise(lambda rv, vv, i, nt: ([rv[0].astype(F32) * _sigmoid(rv[1])], []),
                  [_row(z), _row(zg)], [], [(SSM_W, BF16)], [], n_rows=S, tr=TM, ch=32, name="glu_gate")[0]
    y_ssm = _matmul(z2, w_sp_f, mode="nn", dims=(S, D, SSM_W), tiles=(TW, 1024, SSM_W),
                    out_dtypes=[BF16], name="ssm_proj")

    ga_row = _row(proj, 1, D)
    gs_row = _row(proj, 2, D)

    def f_merge(rv, vv, i, nt):
        ga, gs, ya, ys = rv
        return [_sigmoid(ga) * ya + _sigmoid(gs) * ys], []

    merged = _rowwise(f_merge, [ga_row, gs_row, _row(y_attn), _row(y_ssm)], [], [(D, BF16)], [],
                      n_rows=S, tr=TR, ch=32, name="merge")[0]
    mo = _matmul(merged, w_out_f, mode="nn", dims=(S, D, D), tiles=(TW, 1024, D),
                 out_dtypes=[BF16], name="out_proj")

    ff_zones = _relay_gather_arrive(ff_flights, mo, "ff_weights_arrive")
    ff_pass, tok_fp = _relay_pass_start(ff_zones, "ff_weights_pass_start")

    def f_norm2(rv, vv, i, nt):
        xv, mv = rv
        g1v, g, sc, sh = vv
        x1v = xv + g1v * mv
        return [x1v, (x1v * _rms(x1v) * g) * (1.0 + sc) + sh], []

    x1, h2 = _rowwise(f_norm2, [_row(x2d), _row(mo)], [g1, norm2_g, sc2, sh2],
                      [(D, F32), (D, BF16)], [], n_rows=S, tr=TR, ch=32, name="norm2_fwd", dep=tok_fp)

    def relu_sq(acc):
        r = jnp.maximum(acc, 0.0)
        return r * r, r

    (G["w_ff1"],) = _relay_pass_wait(ff_pass[:1], h2, "w_ff1_pass_wait")
    act, relu = _matmul(h2, G["w_ff1"], mode="nn", dims=(S, D_FF, D), tiles=(TM, 1024, D),
                        out_dtypes=[BF16, BF16], name="ff1", b3=True, epilogue=relu_sq)
    w_ff2_f = _relay_pass_wait(ff_pass[1:], act, "w_ff2_pass_wait")[0].reshape(D_FF, D)
    ff = _matmul(act, w_ff2_f, mode="nn", dims=(S, D, D_FF), tiles=(TM, 1024, 2048),
                 out_dtypes=[BF16], name="ff2")

    def f_loss(rv, vv, i, nt):
        x1v, ffv, tv = rv
        g2v, gf = vv
        x2v = x1v + g2v * ffv
        r = _rms(x2v)
        xh = x2v * r
        diff = xh * gf - tv
        dy = diff * (1.0 / D)
        dxh = dy * gf
        dx2 = r * (dxh - xh * jnp.mean(dxh * xh, axis=-1, keepdims=True))
        return [dx2, dx2 * g2v], [_colsum(0.5 * diff * diff * (1.0 / D)), _colsum(dy * xh),
                                  _colsum(dx2 * ffv)]

    dx2, dff, loss_cols, d_final_g, dg2 = _rowwise(
        f_loss, [_row(x1), _row(ff), _row(tgt)], [g2, final_g.reshape(1, D)],
        [(D, F32), (D, BF16)], [(1, D)] * 3, n_rows=S, tr=TR, ch=32, name="loss_bwd")

    df1 = _matmul(dff, w_ff2_f, mode="nt", dims=(S, D_FF, D), tiles=(TM, 1024, D),
                  out_dtypes=[BF16], name="ff2_dgrad", extras=(relu,),
                  epilogue=lambda acc, r: (acc * (2.0 * r.astype(F32)),))
    gw_ff2 = _matmul(act, dff, mode="tn", dims=(D_FF, D, S), tiles=(1024, 1024, TS),
                     out_dtypes=[BF16], name="ff2_wgrad").reshape(N_DEV, D_FF // N_DEV, D)
    g_flight = {}
    (g_flight["w_ff2"],), tok = _exchange_start([gw_ff2], "scatter", "grads_start_ff2")
    dh2 = _matmul(df1, G["w_ff1"], mode="nt", dims=(S, D, D_FF), tiles=(TM, D, 1024),
                  out_dtypes=[BF16], name="ff1_dgrad", b3=True, dep=tok)
    gw_ff1 = _matmul(h2, df1, mode="tn", dims=(D, D_FF, S), tiles=(1024, 1024, TS),
                     out_dtypes=[BF16], name="ff1_wgrad", out3=True)
    (g_flight["w_ff1"],), tok = _exchange_start([gw_ff1], "scatter", "grads_start_ff1")

    def f_norm2_bwd(rv, vv, i, nt):
        x1v, dh, dx2v, mv = rv
        g, sc, g1v = vv
        r = _rms(x1v)
        xh = x1v * r
        t = xh * g
        dt = dh * (1.0 + sc)
        dxh = dt * g
        dx1 = dx2v + r * (dxh - xh * jnp.mean(dxh * xh, axis=-1, keepdims=True))
        return [dx1, dx1 * g1v], [_colsum(dh), _colsum(dh * t), _colsum(dt * xh), _colsum(dx1 * mv)]

    dx1, dmo, dsh2, dsc2, d_norm2_g, dg1 = _rowwise(
        f_norm2_bwd, [_row(x1), _row(dh2), _row(dx2), _row(mo)], [norm2_g, sc2, g1],
        [(D, F32), (D, BF16)], [(1, D)] * 4, n_rows=S, tr=TR, ch=16, name="norm2_bwd", dep=tok)

    dmerged = _matmul(dmo, w_out_f, mode="nt", dims=(S, D, D), tiles=(TW, 1024, D),
                      out_dtypes=[BF16], name="out_dgrad")
    gw_out = _matmul(merged, dmo, mode="tn", dims=(D, D, S), tiles=(1024, 1024, TS),
                     out_dtypes=[BF16], name="out_wgrad").reshape(N_DEV, D // N_DEV, D)
    (g_flight["w_out"],), tok = _exchange_start([gw_out], "scatter", "grads_start_out")

    def f_merge_bwd(rv, vv, i, nt):
        dm, ga, gs, ya, ys = rv
        sa, ss = _sigmoid(ga), _sigmoid(gs)
        return [dm * sa, dm * ss, dm * ya * sa * (1.0 - sa), dm * ys * ss * (1.0 - ss)], []

    dy_attn, dy_ssm, dga, dgs = _rowwise(
        f_merge_bwd, [_row(dmerged), ga_row, gs_row, _row(y_attn), _row(y_ssm)], [],
        [(D, BF16)] * 4, [], n_rows=S, tr=TR, ch=16, name="merge_bwd", dep=tok)

    dz2 = _matmul(dy_ssm, w_sp_f, mode="nt", dims=(S, SSM_W, D), tiles=(TW, SSM_W, D),
                  out_dtypes=[F32], name="ssm_proj_dgrad")
    gw_ssm_proj = _to_col_blocks(_matmul(z2, dy_ssm, mode="tn", dims=(SSM_W, D, S), tiles=(SSM_W, 1024, TS),
                                         out_dtypes=[BF16], name="ssm_proj_wgrad"))

    def f_glu_bwd(rv, vv, i, nt):
        dz2v, zv, zgv = rv
        sg = _sigmoid(zgv)
        dzg = dz2v * zv.astype(F32) * sg * (1.0 - sg)
        return [dzg, dz2v * sg], [_colsum(dzg)]

    dzg, dz_a, d_b_glu = _rowwise(f_glu_bwd, [_row(dz2), _row(z), _row(zg)], [],
                                  [(SSM_W, BF16), (SSM_W, F32)], [(1, SSM_W)],
                                  n_rows=S, tr=TM, ch=32, name="glu_bwd")
    dz_b = _matmul(dzg, w_glu_f, mode="nt", dims=(S, SSM_W, SSM_W), tiles=(TM, SSM_W, SSM_W),
                   out_dtypes=[F32], name="glu_dgrad")
    gw_glu = _matmul(z, dzg, mode="tn", dims=(SSM_W, SSM_W, S), tiles=(SSM_W, SSM_W, TS),
                     out_dtypes=[BF16], name="glu_wgrad").reshape(N_DEV, SSM_W // N_DEV, SSM_W)
    (g_flight["w_ssm_proj"], g_flight["w_glu"]), tok = _exchange_start(
        [gw_ssm_proj, gw_glu], "scatter", "grads_start_ssm")

    def f_ssm_out_bwd(rv, vv, i, nt):
        dza, dzb, yv, uv = rv
        dy = (dza + dzb) * _gelu_grad(yv)
        return [dy, dy * vv[0]], [_colsum(dy * uv)]

    dy_s, du_a, d_ssm_d = _rowwise(
        f_ssm_out_bwd, [_row(dz_a), _row(dz_b), _row(y_ssm_pre), _row(proj, u_blk, SSM_W)], [ssm_d],
        [(SSM_W, BF16), (SSM_W, F32)], [(1, SSM_W)], n_rows=S, tr=TM, ch=32, name="ssm_out_bwd", dep=tok)
    hw = SSM_W // 2
    u_half = (ATTN_W + 2 * KV_W) // hw
    dccat = _matmul(dy_s, xs, mode="tn", dims=(hw, 2 * SSM_H, S), tiles=(hw, 1024, TS),
                    out_dtypes=[F32], name="ssm_c_wgrad", a_index=lambda i, j, k: (k, j % 2))
    hs, dacc, du_b = _scan_bwd(dy_s, xprev, bcat, ccat, abar, S)
    dbcat = _matmul(proj, hs, mode="tn", dims=(hw, 2 * SSM_H, S), tiles=(hw, 1024, TS),
                    out_dtypes=[F32], name="ssm_b_wgrad", a_index=lambda i, j, k: (k, u_half + j % 2))
    grp = np.arange(SSM_H) // SSM_N
    gind = jnp.asarray((grp[:, None] == np.arange(128)[None, :]).astype(np.float32), BF16)
    d_lam_re, d_lam_im, d_ls, d_btr, d_bti, d_ctr, d_cti = _ssm_param_bwd(
        lam_re, lam_im, ls_x, btr, bti, dacc, dbcat, dccat, gind)

    dattn = _matmul(dy_attn, w_ap_f, mode="nt", dims=(S, ATTN_W, D), tiles=(TW, ATTN_W, D),
                    out_dtypes=[BF16], name="attn_proj_dgrad")
    gw_attn_proj = _to_col_blocks(_matmul(attn, dy_attn, mode="tn", dims=(ATTN_W, D, S), tiles=(ATTN_W, 1024, TS),
                                          out_dtypes=[BF16], name="attn_proj_wgrad"))
    (g_flight["w_attn_proj"],), tok = _exchange_start(
        [gw_attn_proj], "scatter", "grads_start_attn")
    dq, dkc, dkp, dvc, dvp, dbias, dsink = _attention_bwd(proj, attn, dattn, bias2, sinkcol, S)
    d_bias_b, d_sinks = _bucket_reduce(dbias.reshape(N_Q_HEADS, BLK * 2 * BLK),
                                       dsink.reshape(N_Q_HEADS, BLK), onehot_t)

    def f_dproj(rv, vv, i, nt):
        dqv, kc, kp, vc, vp, dua, dub, gav, gsv = rv
        keep = (i < nt - 1).astype(F32)
        dp = jnp.concatenate([dqv.astype(F32), kc + keep * kp, vc + keep * vp, dua + dub,
                              gav.astype(F32), gsv.astype(F32)], axis=-1)
        return [dp], [_colsum(dp)]

    dproj, d_b_in = _rowwise(
        f_dproj, [_row(dq), _row(dkc), _row(dkp, shift=1), _row(dvc), _row(dvp, shift=1),
                  _row(du_a), _row(du_b), _row(dga), _row(dgs)], [],
        [(IN_W, BF16)], [(1, IN_W)], n_rows=S, tr=BLK, ch=16, name="dproj", dep=tok)
    gw_in = _matmul(h, dproj, mode="tn", dims=(D, IN_W, S), tiles=(1024, 768, TS),
                    out_dtypes=[BF16], name="in_wgrad", out3=True)
    (g_flight["w_in"],), tok = _exchange_start([gw_in], "scatter", "grads_start_in")
    dh = _matmul(dproj, G["w_in"], mode="nt", dims=(S, D, IN_W), tiles=(TM, D, 768),
                 out_dtypes=[BF16], name="in_dgrad", b3=True, dep=tok)

    def f_norm1_bwd(rv, vv, i, nt):
        xv, dhv, dx1v = rv
        g, sc = vv
        r = _rms(xv)
        xh = xv * r
        t = xh * g
        dt = dhv * (1.0 + sc)
        dxh = dt * g
        dxv = dx1v + r * (dxh - xh * jnp.mean(dxh * xh, axis=-1, keepdims=True))
        return [dxv], [_colsum(dhv), _colsum(dhv * t), _colsum(dt * xh)]

    grad_x, dsh1, dsc1, d_norm1_g = _rowwise(
        f_norm1_bwd, [_row(x2d), _row(dh), _row(dx1)], [norm1_g, sc1],
        [(D, F32)], [(1, D)] * 3, n_rows=S, tr=TR, ch=32, name="norm1_bwd")

    part = _pack({
        "b_ada": [dsh1, dsc1, dg1, dsh2, dsc2, dg2], "norm1_g": d_norm1_g, "b_in": d_b_in, "norm2_g": d_norm2_g,
        "final_g": d_final_g, "lambda_re": d_lam_re, "lambda_im": d_lam_im,
        "log_step": d_ls[0, :SSM_G], "attn_sinks": d_sinks[:, 0],
        "rel_bias": jnp.transpose(d_bias_b[:, :NUM_BUCKETS]), "b_glu": d_b_glu, "ssm_d": d_ssm_d,
        "loss": loss_cols, "ssm_b_re": d_btr, "ssm_b_im": d_bti, "ssm_c_re": d_ctr, "ssm_c_im": d_cti,
    })
    zone_small = lax.dynamic_update_slice(lax.empty((N_DEV, PACK_ROWS, PACK_W), F32), part[None], (me, 0, 0))
    (small_flight,), after = _exchange_start([zone_small], "gather", "small_grads_start")

    big_out = {}
    for n in ["w_ff2", "w_ff1", "w_out", "w_ssm_proj", "w_glu", "w_attn_proj", "w_in"]:
        own, recv = _exchange_wait([g_flight[n]], "scatter", after, "grads_wait_" + n[2:])[0]
        rows, cols = shard[n].shape
        parts = [(own, lambda m: m[0])] + [
            (recv, lambda m, j=j: jnp.where(j >= m[0], j + 1, j)) for j in range(N_DEV - 1)]
        big_out[n] = _adamw(parts, shard[n], Mo[n][0], Vo[n][0], tr=_adamw_rows(rows, cols), ch=16,
                            name="adamw_" + n, prefetch=me1)
        after = big_out[n][0]

    part_all = _exchange_wait([small_flight], "gather", after, "small_grads_wait")[0][0]
    wp, mp, vp = [_pack(_small_params_packed(p)) for p in (W, Mo, Vo)]
    sg, sdelta, sm, sv = _adamw([(part_all, d) for d in range(N_DEV)], wp, mp, vp,
                                tr=PACK_ROWS, ch=8, name="adamw_small")
    lo, _ = _PACK_OFF["loss"]
    loss = jnp.sum(sg[lo])

    o_ada, _ = _PACK_OFF["b_ada"]
    dmod_all = part_all[:, o_ada:o_ada + N_MOD, :].reshape(N_DEV, N_MOD * D)
    dmod_cols = lax.dynamic_slice(dmod_all, (0, me * n_ada), (N_DEV, n_ada))
    gw_ada = _matmul(cs, dmod_cols, mode="tn", dims=(D, n_ada, N_DEV), tiles=(D, 512, N_DEV),
                     out_dtypes=[F32], name="ada_wgrad")
    big_out["w_ada"] = _adamw([(gw_ada, 0)], w_ada[0], m_w_ada[0], v_w_ada[0],
                              tr=_adamw_rows(D, n_ada), ch=16, name="adamw_w_ada")

    def leaf(kind, n):
        if n in big_out:
            return big_out[n][kind][None]
        return _unpack_small((sg, sdelta, sm, sv)[kind], n)

    outs = [loss, grad_x.reshape(1, S, D)]
    for kind in range(4):
        outs.extend(leaf(kind, n) for n in WEIGHT_ORDER)
    return tuple(outs)
```

```python
import functools
import math

import numpy as np
import jax
import jax.numpy as jnp
from jax import lax
from jax.experimental import pallas as pl
from jax.experimental.pallas import tpu as pltpu

F32 = jnp.float32
BF16 = jnp.bfloat16
MESH = pl.DeviceIdType.MESH

N_DEV = 8
D = 2048
HEAD_DIM = 64
N_Q_HEADS = 16
N_KV_HEADS = 4
GROUP = N_Q_HEADS // N_KV_HEADS
ATTN_W = N_Q_HEADS * HEAD_DIM
KV_W = N_KV_HEADS * HEAD_DIM
BLK = 128
NUM_BUCKETS = 32
MAX_DISTANCE = 128
NEG_INF = -1e30
SSM_W = 512
SSM_P = 16
SSM_G = 32
SSM_N = 64
SSM_H = SSM_G * SSM_N
D_FF = 4 * D
IN_W = ATTN_W + 2 * KV_W + SSM_W + 2 * D
N_MOD = 6
EPS = 1e-6

ADAM_LR = 0.001
ADAM_B1 = 0.9
ADAM_B2 = 0.999
ADAM_EPS = 1e-08
ADAM_WD = 0.01
ADAM_STEP = 10

VMEM_LIMIT = 56 * 1024 * 1024
PACK_W = 2048


def _cparams(sem):
    return pltpu.CompilerParams(dimension_semantics=sem, vmem_limit_bytes=VMEM_LIMIT)


def _matmul(a, b, *, mode, dims, tiles, out_dtypes, name, a_off=0, b3=False,
            out3=False, bias=None, extras=(), epilogue=None, dep=None, a_index=None, b_index=None):
    M, N, K = dims
    tm, tn, tk = tiles
    assert M % tm == 0 and N % tn == 0 and K % tk == 0, (name, dims, tiles)
    gm, gn, gk = M // tm, N // tn, K // tk
    n_extra = len(extras)
    has_bias = bias is not None
    n_out = len(out_dtypes)

    if mode == "nn":
        a_spec = pl.BlockSpec((tm, tk), lambda i, j, k: (i, a_off + k))
        if b3:
            nb = (N // N_DEV) // tn
            assert nb * tn * N_DEV == N
            b_spec = pl.BlockSpec((None, tk, tn), lambda i, j, k: (j // nb, k, j % nb))
        else:
            b_spec = pl.BlockSpec((tk, tn), lambda i, j, k: (k, j))
        dn = (((1,), (0,)), ((), ()))
    elif mode == "nt":
        a_spec = pl.BlockSpec((tm, tk), lambda i, j, k: (i, a_off + k))
        if b3:
            nb = (K // N_DEV) // tk
            assert nb * tk * N_DEV == K
            b_spec = pl.BlockSpec((None, tn, tk), lambda i, j, k: (k // nb, j, k % nb))
        else:
            b_spec = pl.BlockSpec((tn, tk), lambda i, j, k: (j, k))
        dn = (((1,), (1,)), ((), ()))
    else:
        a_spec = pl.BlockSpec((tk, tm), lambda i, j, k: (k, a_off + i))
        b_spec = pl.BlockSpec((tk, tn), lambda i, j, k: (k, j))
        dn = (((0,), (0,)), ((), ()))
    if a_index is not None:
        a_spec = pl.BlockSpec(a_spec.block_shape, a_index)
    if b_index is not None:
        b_spec = pl.BlockSpec(b_spec.block_shape, b_index)

    if out3:
        nbo = (N // N_DEV) // tn
        assert nbo * tn * N_DEV == N
        o_spec = pl.BlockSpec((None, tm, tn), lambda i, j, k: (j // nbo, i, j % nbo))
        o_shape = (N_DEV, M, N // N_DEV)
    else:
        o_spec = pl.BlockSpec((tm, tn), lambda i, j, k: (i, j))
        o_shape = (M, N)

    in_specs = [a_spec, b_spec]
    args = [a, b]
    if has_bias:
        in_specs.append(pl.BlockSpec((1, tn), lambda i, j, k: (0, j)))
        args.append(bias)
    for e in extras:
        in_specs.append(pl.BlockSpec((tm, tn), lambda i, j, k: (i, j)))
        args.append(e)
    n_dep = 0 if dep is None else 1
    if n_dep:
        in_specs.append(pl.BlockSpec(memory_space=pl.ANY))
        args.append(dep)

    def body(*refs):
        a_ref, b_ref = refs[0], refs[1]
        pos = 2
        bias_ref = None
        if has_bias:
            bias_ref = refs[pos]
            pos += 1
        extra_refs = refs[pos:pos + n_extra]
        pos += n_extra + n_dep
        out_refs = refs[pos:pos + n_out]
        acc_ref = refs[pos + n_out] if gk > 1 else None

        part = lax.dot_general(a_ref[...].astype(BF16), b_ref[...].astype(BF16), dn,
                               preferred_element_type=F32)

        def finish(acc):
            if has_bias:
                acc = acc + bias_ref[...]
            if epilogue is None:
                vals = (acc,)
            else:
                vals = epilogue(acc, *[e[...] for e in extra_refs])
            for o_ref, val in zip(out_refs, vals):
                o_ref[...] = val.astype(o_ref.dtype)

        if gk == 1:
            finish(part)
        else:
            k = pl.program_id(2)

            @pl.when(k == 0)
            def _():
                acc_ref[...] = part

            @pl.when(k > 0)
            def _():
                acc_ref[...] += part

            @pl.when(k == gk - 1)
            def _():
                finish(acc_ref[...])

    outs = pl.pallas_call(
        body,
        grid=(gm, gn, gk),
        in_specs=in_specs,
        out_specs=[o_spec] * n_out,
        out_shape=[jax.ShapeDtypeStruct(o_shape, dt) for dt in out_dtypes],
        scratch_shapes=([pltpu.VMEM((tm, tn), F32)] if gk > 1 else []),
        compiler_params=_cparams(("parallel", "parallel", "arbitrary")),
        name=name,
    )(*args)
    return outs[0] if n_out == 1 else outs


def _rowwise(fn, rows, vecs, row_outs, sum_outs, *, n_rows, tr, ch, name, dep=None, prefetch=None):
    assert n_rows % tr == 0 and tr % ch == 0
    nt = n_rows // tr
    nr, nv, nro, nso = len(rows), len(vecs), len(row_outs), len(sum_outs)
    in_specs, args = [], []
    n_pf = 0 if prefetch is None else 1
    for (arr, lead, cblk, w, shift) in rows:
        if shift:
            ridx = lambda i, shift=shift: jnp.minimum(i + shift, nt - 1)
        else:
            ridx = lambda i: i
        if arr.ndim == 3:
            def imap(i, *pf, lead=lead, cblk=cblk, ridx=ridx):
                return (lead(pf[0]) if callable(lead) else lead, ridx(i), cblk)
            in_specs.append(pl.BlockSpec((None, tr, w), imap))
        else:
            in_specs.append(pl.BlockSpec(
                (tr, w), lambda i, *pf, cblk=cblk, ridx=ridx: (ridx(i), cblk)))
        args.append(arr)
    for v in vecs:
        in_specs.append(pl.BlockSpec(v.shape, lambda i, *pf, nd=v.ndim: (0,) * nd))
        args.append(v)
    n_dep = 0 if dep is None else 1
    if n_dep:
        in_specs.append(pl.BlockSpec(memory_space=pl.ANY))
        args.append(dep)
    out_specs = [pl.BlockSpec((tr, w), lambda i, *pf: (i, 0)) for (w, _) in row_outs]
    out_shape = [jax.ShapeDtypeStruct((n_rows, w), dt) for (w, dt) in row_outs]
    for (r, w) in sum_outs:
        out_specs.append(pl.BlockSpec((r, w), lambda i, *pf: (0, 0)))
        out_shape.append(jax.ShapeDtypeStruct((r, w), F32))

    def body(*refs):
        refs = refs[n_pf:]
        i = pl.program_id(0)
        r_in = refs[:nr]
        v_in = refs[nr:nr + nv]
        r_out = refs[nr + nv + n_dep:nr + nv + n_dep + nro]
        s_out = refs[nr + nv + n_dep + nro:]
        s_out, s_acc = s_out[:nso], s_out[nso:]
        if nso:
            @pl.when(i == 0)
            def _():
                for s in s_acc:
                    s[...] = jnp.zeros(s.shape, F32)
        vvals = [v[...] for v in v_in]

        def chunk(ci, carry):
            r0 = pl.multiple_of(ci * ch, ch)
            rv = [r[pl.ds(r0, ch), :].astype(F32) for r in r_in]
            pieces = [fn([v[8 * k:8 * (k + 1)] for v in rv], vvals, i, nt) for k in range(ch // 8)]
            for j, ref in enumerate(r_out):
                val = jnp.concatenate([ro[j] for ro, _ in pieces], axis=0) if ch > 8 else pieces[0][0][j]
                ref[pl.ds(r0, ch), :] = val.astype(ref.dtype)
            for j, ref in enumerate(s_acc):
                ref[...] += functools.reduce(lambda a, b: a + b, [so[j] for _, so in pieces])
            return carry

        lax.fori_loop(0, tr // ch, chunk, 0)
        if nso:
            @pl.when(i == nt - 1)
            def _():
                for s, acc in zip(s_out, s_acc):
                    s[...] = jnp.sum(acc[...], axis=0, keepdims=True)

    outs = pl.pallas_call(
        body,
        grid_spec=pltpu.PrefetchScalarGridSpec(
            num_scalar_prefetch=n_pf, grid=(nt,), in_specs=in_specs, out_specs=out_specs,
            scratch_shapes=[pltpu.VMEM((8, w), F32) for (_, w) in sum_outs]),
        out_shape=out_shape,
        compiler_params=_cparams(("arbitrary",)),
        name=name,
    )(*([prefetch] if n_pf else []), *args)
    return outs


def _row(arr, cblk=0, w=None, lead=0, shift=0):
    return (arr, lead, cblk, arr.shape[-1] if w is None else w, shift)


def _colsum(v):
    parts = [v[8 * k:8 * (k + 1)] for k in range(v.shape[0] // 8)]
    return functools.reduce(lambda a, b: a + b, parts)


def _rms(x):
    return lax.rsqrt(jnp.mean(x * x, axis=-1, keepdims=True) + EPS)


def _sigmoid(x):
    return 1.0 / (1.0 + jnp.exp(-x))


_GELU_C = math.sqrt(2.0 / math.pi)


def _gelu(x):
    return 0.5 * x * (1.0 + jnp.tanh(_GELU_C * (x + 0.044715 * (x * x * x))))


def _gelu_grad(x):
    t = jnp.tanh(_GELU_C * (x + 0.044715 * (x * x * x)))
    return 0.5 * (1.0 + t) + 0.5 * x * (1.0 - t * t) * (_GELU_C * (1.0 + 3.0 * 0.044715 * (x * x)))


def _my_pos():
    return lax.axis_index("x"), lax.axis_index("y"), lax.axis_index("c")


def _flip(pos, k):
    x, y, c = pos
    return (1 - x if k & 4 else x, 1 - y if k & 2 else y, 1 - c if k & 1 else c)


def _dev_id(pos):
    return 4 * pos[0] + 2 * pos[1] + pos[2]


def _small_allgather(x, name):
    r, c = x.shape

    def body(x_ref, out_ref, send_sems, recv_sems):
        me = _my_pos()
        out_ref[_dev_id(me)] = x_ref[...]
        copies = []
        for k in range(1, N_DEV):
            cp = pltpu.make_async_remote_copy(
                src_ref=x_ref, dst_ref=out_ref.at[_dev_id(me)],
                send_sem=send_sems.at[k - 1], recv_sem=recv_sems.at[k - 1],
                device_id=_flip(me, k), device_id_type=MESH)
            cp.start()
            copies.append(cp)
        for k in range(1, N_DEV):
            peer = _flip(me, k)
            pltpu.make_async_remote_copy(
                src_ref=x_ref, dst_ref=out_ref.at[_dev_id(peer)],
                send_sem=send_sems.at[k - 1], recv_sem=recv_sems.at[k - 1],
                device_id=peer, device_id_type=MESH).wait_recv()
        for cp in copies:
            cp.wait_send()

    return pl.pallas_call(
        body,
        out_shape=jax.ShapeDtypeStruct((N_DEV, r, c), x.dtype),
        in_specs=[pl.BlockSpec(memory_space=pltpu.VMEM)],
        out_specs=pl.BlockSpec(memory_space=pltpu.VMEM),
        scratch_shapes=[pltpu.SemaphoreType.DMA((N_DEV - 1,)),
                        pltpu.SemaphoreType.DMA((N_DEV - 1,))],
        compiler_params=pltpu.CompilerParams(vmem_limit_bytes=VMEM_LIMIT),
        name=name,
    )(x)


_HBM = pl.BlockSpec(memory_space=pltpu.HBM)
_SEM = pl.BlockSpec(memory_space=pltpu.SEMAPHORE)
_EFFECT = pltpu.SideEffectType.DATAFLOW_SIDE_EFFECTING


def _relay_copy(zone, send_sems, recv_sems, k, block, to):
    slot = zone.at[_dev_id(block)]
    return pltpu.make_async_remote_copy(
        src_ref=slot, dst_ref=slot, send_sem=send_sems.at[k], recv_sem=recv_sems.at[k],
        device_id=to, device_id_type=MESH)


def _relay_peers():
    x, y, c = _my_pos()
    return (x, y, c), (x, y, 1 - c), [(1 - x, y), (x, 1 - y), (1 - x, 1 - y)]


def _relay_start_call(zones, n_sems, issue, name, after=None):
    n = len(zones)
    n_after = 0 if after is None else 1

    def body(*refs):
        refs = refs[:n] + refs[n + n_after:]
        send, recv, token = refs[n:2 * n], refs[2 * n:3 * n], refs[4 * n]
        for a in range(n):
            issue(refs[a], send[a], recv[a])
        token[...] = jnp.zeros(token.shape, token.dtype)

    sem = pltpu.SemaphoreType.DMA((n_sems,))
    outs = pl.pallas_call(
        body,
        name=name,
        out_shape=([sem] * (2 * n) + [pltpu.HBM(z.shape, z.dtype) for z in zones]
                   + [jax.ShapeDtypeStruct((8, 128), F32)]),
        in_specs=[_HBM] * n + [pl.BlockSpec(memory_space=pl.ANY)] * n_after,
        out_specs=[_SEM] * (2 * n) + [_HBM] * n + [pl.BlockSpec(memory_space=pltpu.VMEM)],
        input_output_aliases={a: 2 * n + a for a in range(n)},
        compiler_params=pltpu.CompilerParams(has_side_effects=_EFFECT),
    )(*[pltpu.with_memory_space_constraint(z, pltpu.HBM) for z in zones],
      *([after] if n_after else []))
    return [(outs[a], outs[n + a], outs[2 * n + a]) for a in range(n)], outs[3 * n]


def _relay_wait_call(flights, settle, after, name):
    n = len(flights)

    def body(*refs):
        send, recv = refs[n:2 * n], refs[2 * n:3 * n]
        for a in range(n):
            settle(refs[a], send[a], recv[a])

    outs = pl.pallas_call(
        body,
        name=name,
        out_shape=[pltpu.HBM(f[2].shape, f[2].dtype) for f in flights],
        in_specs=[_HBM] * n + [_SEM] * (2 * n) + [pl.BlockSpec(memory_space=pl.ANY)],
        out_specs=[_HBM] * n,
        input_output_aliases={a: a for a in range(n)},
        compiler_params=pltpu.CompilerParams(has_side_effects=_EFFECT),
    )(*[f[2] for f in flights], *[f[0] for f in flights], *[f[1] for f in flights], after)
    return list(outs)


def _relay_gather_start(zones, name, after=None):
    def issue(zone, send, recv):
        me, sib, chips = _relay_peers()
        _relay_copy(zone, send, recv, 0, me, sib).start()
        for j, chip in enumerate(chips):
            _relay_copy(zone, send, recv, 1 + j, me, (*chip, me[2])).start()
    return _relay_start_call(zones, 4, issue, name, after)


def _relay_gather_arrive(flights, after, name):
    def settle(zone, send, recv):
        me, sib, chips = _relay_peers()
        _relay_copy(zone, send, recv, 0, sib, me).wait_recv()
        _relay_copy(zone, send, recv, 0, me, sib).wait_send()
        for j, chip in enumerate(chips):
            _relay_copy(zone, send, recv, 1 + j, (*chip, me[2]), me).wait_recv()
            _relay_copy(zone, send, recv, 1 + j, me, (*chip, me[2])).wait_send()
    return _relay_wait_call(flights, settle, after, name)


def _relay_pass_start(zones, name, after=None):
    def issue(zone, send, recv):
        me, sib, chips = _relay_peers()
        for j, chip in enumerate(chips):
            _relay_copy(zone, send, recv, j, (*chip, me[2]), sib).start()
    return _relay_start_call(zones, 3, issue, name, after)


def _relay_pass_wait(flights, after, name):
    def settle(zone, send, recv):
        me, sib, chips = _relay_peers()
        for j, chip in enumerate(chips):
            _relay_copy(zone, send, recv, j, (*chip, sib[2]), me).wait_recv()
            _relay_copy(zone, send, recv, j, (*chip, me[2]), sib).wait_send()
    return _relay_wait_call(flights, settle, after, name)


def _exchange_copy(kind, bufs, send_sems, recv_sems, me, k, arriving):
    peer = _flip(me, k)
    my_id, peer_id = _dev_id(me), _dev_id(peer)
    if kind == "gather":
        slot = bufs[0].at[peer_id if arriving else my_id]
        src, dst = slot, slot
    else:
        src = bufs[0].at[my_id if arriving else peer_id]
        dst = bufs[1].at[peer_id if arriving else my_id]
    return pltpu.make_async_remote_copy(
        src_ref=src, dst_ref=dst, send_sem=send_sems.at[k - 1], recv_sem=recv_sems.at[k - 1],
        device_id=peer, device_id_type=MESH)


def _exchange_start(arrays, kind, name, after=None):
    n = len(arrays)
    n_after = 0 if after is None else 1
    if kind == "gather":
        bufs = [[a] for a in arrays]
    else:
        bufs = [[a, lax.empty(a.shape, a.dtype)] for a in arrays]
    nb = len(bufs[0])
    flat = [b for group in bufs for b in group]

    def body(*refs):
        outs_at = nb * n + n_after
        send = refs[outs_at:outs_at + n]
        recv = refs[outs_at + n:outs_at + 2 * n]
        token = refs[outs_at + 2 * n + nb * n]
        me = _my_pos()
        for a in range(n):
            for k in range(1, N_DEV):
                _exchange_copy(kind, refs[nb * a:nb * (a + 1)], send[a], recv[a], me, k, False).start()
        token[...] = jnp.zeros(token.shape, token.dtype)

    sem = pltpu.SemaphoreType.DMA((N_DEV - 1,))
    outs = pl.pallas_call(
        body,
        name=name,
        out_shape=([sem] * (2 * n) + [pltpu.HBM(b.shape, b.dtype) for b in flat]
                   + [jax.ShapeDtypeStruct((8, 128), F32)]),
        in_specs=[_HBM] * (nb * n) + [pl.BlockSpec(memory_space=pl.ANY)] * n_after,
        out_specs=[_SEM] * (2 * n) + [_HBM] * (nb * n) + [pl.BlockSpec(memory_space=pltpu.VMEM)],
        input_output_aliases={i: 2 * n + i for i in range(nb * n)},
        compiler_params=pltpu.CompilerParams(has_side_effects=_EFFECT),
    )(*[pltpu.with_memory_space_constraint(b, pltpu.HBM) for b in flat],
      *([after] if n_after else []))
    flights = [(outs[a], outs[n + a], list(outs[2 * n + nb * a:2 * n + nb * (a + 1)]))
               for a in range(n)]
    return flights, outs[2 * n + nb * n]


def _exchange_wait(flights, kind, after, name):
    n = len(flights)
    nb = len(flights[0][2])
    flat = [b for f in flights for b in f[2]]

    def body(*refs):
        send = refs[nb * n:nb * n + n]
        recv = refs[nb * n + n:nb * n + 2 * n]
        me = _my_pos()
        for a in range(n):
            for k in range(1, N_DEV):
                bufs = refs[nb * a:nb * (a + 1)]
                _exchange_copy(kind, bufs, send[a], recv[a], me, k, False).wait_send()
                _exchange_copy(kind, bufs, send[a], recv[a], me, k, True).wait_recv()

    outs = pl.pallas_call(
        body,
        name=name,
        out_shape=[pltpu.HBM(b.shape, b.dtype) for b in flat],
        in_specs=[_HBM] * (nb * n) + [_SEM] * (2 * n) + [pl.BlockSpec(memory_space=pl.ANY)],
        out_specs=[_HBM] * (nb * n),
        input_output_aliases={i: i for i in range(nb * n)},
        compiler_params=pltpu.CompilerParams(has_side_effects=_EFFECT),
    )(*flat, *[f[0] for f in flights], *[f[1] for f in flights], after)
    return [list(outs[nb * a:nb * (a + 1)]) for a in range(n)]


def _t5_buckets_block():
    qi = np.arange(BLK)[:, None]
    ki = np.arange(2 * BLK)[None, :]
    n = np.maximum(qi + BLK - ki, 0)
    max_exact = NUM_BUCKETS // 2
    large = max_exact + (np.log(np.maximum(n, 1) / max_exact)
                         / np.log(MAX_DISTANCE / max_exact)
                         * (NUM_BUCKETS - max_exact)).astype(np.int32)
    large = np.minimum(large, NUM_BUCKETS - 1)
    return np.where(n < max_exact, n, large).astype(np.int32)


def _band_mask():
    qi = np.arange(BLK)[:, None]
    ki = np.arange(2 * BLK)[None, :]
    dist = qi + BLK - ki
    return (dist >= 0) & (dist < BLK)


def _attn_scores(q_ref, kp_ref, kc_ref, hkv):
    c0 = hkv * HEAD_DIM
    kk = jnp.concatenate([kp_ref[:, c0:c0 + HEAD_DIM], kc_ref[:, c0:c0 + HEAD_DIM]],
                         axis=0).astype(BF16)
    qg = jnp.concatenate(
        [q_ref[:, (hkv * GROUP + g) * HEAD_DIM:(hkv * GROUP + g + 1) * HEAD_DIM]
         for g in range(GROUP)], axis=0).astype(BF16)
    s = lax.dot_general(qg, kk, (((1,), (1,)), ((), ())), preferred_element_type=F32)
    return qg, kk, s


def _attn_softmax(s, bias_ref, sink_ref, hkv):
    r0, r1 = hkv * GROUP * BLK, (hkv + 1) * GROUP * BLK
    s = s * (HEAD_DIM ** -0.5) + bias_ref[r0:r1, :]
    sink = sink_ref[r0:r1, :]
    m = jnp.maximum(jnp.max(s, axis=-1, keepdims=True), sink)
    p = jnp.exp(s - m)
    e_sink = jnp.exp(sink - m)
    inv = 1.0 / (jnp.sum(p, axis=-1, keepdims=True) + e_sink)
    return p * inv, e_sink * inv


def _kv_rows(p_ref, c_ref, hkv):
    c0 = hkv * HEAD_DIM
    return jnp.concatenate([p_ref[:, c0:c0 + HEAD_DIM], c_ref[:, c0:c0 + HEAD_DIM]],
                           axis=0).astype(BF16)


ATT_Q_FWD = 4
ATT_Q_BWD = 2


def _attn_in_specs(bias2, nq):
    prev = lambda n: jnp.maximum(nq * n - 1, 0)
    kcol = ATTN_W // KV_W
    return [
        pl.BlockSpec((nq * BLK, ATTN_W), lambda n: (n, 0)),
        pl.BlockSpec((BLK, KV_W), lambda n: (prev(n), kcol)),
        pl.BlockSpec((nq * BLK, KV_W), lambda n: (n, kcol)),
        pl.BlockSpec((BLK, KV_W), lambda n: (prev(n), kcol + 1)),
        pl.BlockSpec((nq * BLK, KV_W), lambda n: (n, kcol + 1)),
        pl.BlockSpec(bias2.shape, lambda n: (0, 0, 0)),
    ]


def _attn_views(t, q_ref, kp_ref, kc_ref, vp_ref, vc_ref, bias_ref):
    rows = pl.ds(t * BLK, BLK)
    before = pl.ds((t - 1) * BLK, BLK)
    table = jnp.minimum(pl.program_id(0), 1) if t == 0 else 1
    return (q_ref.at[rows, :],
            kp_ref if t == 0 else kc_ref.at[before, :], kc_ref.at[rows, :],
            vp_ref if t == 0 else vc_ref.at[before, :], vc_ref.at[rows, :],
            bias_ref.at[table])


def _attention_fwd(proj, bias2, sinkcol, n_rows):
    nq = min(ATT_Q_FWD, n_rows // BLK)
    steps = n_rows // (nq * BLK)

    def body(q_ref, kp_ref, kc_ref, vp_ref, vc_ref, bias_ref, sink_ref, o_ref):
        views = [_attn_views(t, q_ref, kp_ref, kc_ref, vp_ref, vc_ref, bias_ref) for t in range(nq)]
        work = [(t, hkv) for t in range(nq) for hkv in range(N_KV_HEADS)]
        scores = {w: _attn_scores(views[w[0]][0], views[w[0]][1], views[w[0]][2], w[1])[2] for w in work}
        probs = {w: _attn_softmax(scores[w], views[w[0]][5], sink_ref, w[1])[0] for w in work}
        outs = {w: jnp.dot(probs[w].astype(BF16), _kv_rows(views[w[0]][3], views[w[0]][4], w[1]),
                           preferred_element_type=F32) for w in work}
        for t, hkv in work:
            for g in range(GROUP):
                h = hkv * GROUP + g
                o_ref[t * BLK:(t + 1) * BLK, h * HEAD_DIM:(h + 1) * HEAD_DIM] = (
                    outs[t, hkv][g * BLK:(g + 1) * BLK, :].astype(o_ref.dtype))

    return pl.pallas_call(
        body,
        grid=(steps,),
        in_specs=_attn_in_specs(bias2, nq) + [pl.BlockSpec(sinkcol.shape, lambda n: (0, 0))],
        out_specs=pl.BlockSpec((nq * BLK, ATTN_W), lambda n: (n, 0)),
        out_shape=jax.ShapeDtypeStruct((n_rows, ATTN_W), BF16),
        compiler_params=_cparams(("parallel",)),
        name="attn_fwd",
    )(proj, proj, proj, proj, proj, bias2, sinkcol)


def _attention_bwd(proj, attn, dattn, bias2, sinkcol, n_rows):
    nq = min(ATT_Q_BWD, n_rows // BLK)
    steps = n_rows // (nq * BLK)
    scale = HEAD_DIM ** -0.5
    dn_t = (((0,), (0,)), ((), ()))

    def body(q_ref, kp_ref, kc_ref, vp_ref, vc_ref, bias_ref, o_ref, do_ref, sink_ref,
             dq_ref, dkc_ref, dkp_ref, dvc_ref, dvp_ref, dbias_ref, dsink_ref):
        @pl.when(pl.program_id(0) == 0)
        def _():
            dbias_ref[...] = jnp.zeros(dbias_ref.shape, F32)
            dsink_ref[...] = jnp.zeros(dsink_ref.shape, F32)

        views = [_attn_views(t, q_ref, kp_ref, kc_ref, vp_ref, vc_ref, bias_ref) for t in range(nq)]
        work = [(t, hkv) for t in range(nq) for hkv in range(N_KV_HEADS)]
        qk ={w: _attn_scores(views[w[0]][0], views[w[0]][1], views[w[0]][2], w[1]) for w in work}
        dog, dps, deltas = {}, {}, {}
        for t, hkv in work:
            rows = slice(t * BLK, (t + 1) * BLK)
            hs = [hkv * GROUP + g for g in range(GROUP)]
            d_o = jnp.concatenate([do_ref[rows, h * HEAD_DIM:(h + 1) * HEAD_DIM] for h in hs], axis=0)
            o = jnp.concatenate([o_ref[rows, h * HEAD_DIM:(h + 1) * HEAD_DIM] for h in hs], axis=0)
            deltas[t, hkv] = jnp.sum(d_o.astype(F32) * o.astype(F32), axis=-1, keepdims=True)
            dog[t, hkv] = d_o.astype(BF16)
            dps[t, hkv] = lax.dot_general(dog[t, hkv], _kv_rows(views[t][3], views[t][4], hkv),
                                          (((1,), (1,)), ((), ())), preferred_element_type=F32)
        p16, ds16 = {}, {}
        for t, hkv in work:
            r0, r1 = hkv * GROUP * BLK, (hkv + 1) * GROUP * BLK
            p, p_sink = _attn_softmax(qk[t, hkv][2], views[t][5], sink_ref, hkv)
            ds = p * (dps[t, hkv] - deltas[t, hkv])
            dbias_ref[r0:r1, :] += ds
            dsink_ref[r0:r1, :] += -(p_sink * deltas[t, hkv])
            p16[t, hkv] = p.astype(BF16)
            ds16[t, hkv] = ds.astype(BF16)
        for t, hkv in work:
            rows = slice(t * BLK, (t + 1) * BLK)
            c0 = hkv * HEAD_DIM
            qg, kk, _ = qk[t, hkv]
            dqg = jnp.dot(ds16[t, hkv], kk, preferred_element_type=F32) * scale
            dkk = lax.dot_general(ds16[t, hkv], qg, dn_t, preferred_element_type=F32) * scale
            dvv = lax.dot_general(p16[t, hkv], dog[t, hkv], dn_t, preferred_element_type=F32)
            for g in range(GROUP):
                h = hkv * GROUP + g
                dq_ref[rows, h * HEAD_DIM:(h + 1) * HEAD_DIM] = (
                    dqg[g * BLK:(g + 1) * BLK, :].astype(dq_ref.dtype))
            dkp_ref[rows, c0:c0 + HEAD_DIM] = dkk[:BLK].astype(dkp_ref.dtype)
            dkc_ref[rows, c0:c0 + HEAD_DIM] = dkk[BLK:].astype(dkc_ref.dtype)
            dvp_ref[rows, c0:c0 + HEAD_DIM] = dvv[:BLK].astype(dvp_ref.dtype)
            dvc_ref[rows, c0:c0 + HEAD_DIM] = dvv[BLK:].astype(dvc_ref.dtype)

    wide = pl.BlockSpec((nq * BLK, ATTN_W), lambda n: (n, 0))
    kv_out = pl.BlockSpec((nq * BLK, KV_W), lambda n: (n, 0))
    kv_shape = jax.ShapeDtypeStruct((n_rows, KV_W), F32)
    acc_shape = bias2.shape[1:]
    return pl.pallas_call(
        body,
        grid=(steps,),
        in_specs=_attn_in_specs(bias2, nq) + [wide, wide, pl.BlockSpec(sinkcol.shape, lambda n: (0, 0))],
        out_specs=[
            wide, kv_out, kv_out, kv_out, kv_out,
            pl.BlockSpec(acc_shape, lambda n: (0, 0)),
            pl.BlockSpec(sinkcol.shape, lambda n: (0, 0)),
        ],
        out_shape=[
            jax.ShapeDtypeStruct((n_rows, ATTN_W), BF16),
            kv_shape, kv_shape, kv_shape, kv_shape,
            jax.ShapeDtypeStruct(acc_shape, F32),
            jax.ShapeDtypeStruct(sinkcol.shape, F32),
        ],
        compiler_params=_cparams(("arbitrary",)),
        name="attn_bwd",
    )(proj, proj, proj, proj, proj, bias2, attn, dattn, sinkcol)


def _bias_tables(rel_bias_t, onehot_t, band_first, band_rest):
    def body(rb_ref, oh_ref, mf_ref, mr_ref, out_ref):
        acc = jnp.zeros((N_Q_HEADS, BLK * 2 * BLK), F32)
        for part in _split3(rb_ref[...]):
            acc = acc + jnp.dot(part, oh_ref[...], preferred_element_type=F32)
        out_ref[0] = jnp.where(mf_ref[...] > 0.0, acc, NEG_INF)
        out_ref[1] = jnp.where(mr_ref[...] > 0.0, acc, NEG_INF)

    return pl.pallas_call(
        body,
        out_shape=jax.ShapeDtypeStruct((2, N_Q_HEADS, BLK * 2 * BLK), F32),
        compiler_params=pltpu.CompilerParams(vmem_limit_bytes=VMEM_LIMIT),
        name="bias_tables",
    )(rel_bias_t, onehot_t, band_first, band_rest)


def _split3(a):
    hi = a.astype(BF16)
    r1 = a - hi.astype(F32)
    mid = r1.astype(BF16)
    lo = (r1 - mid.astype(F32)).astype(BF16)
    return hi, mid, lo


def _bucket_reduce(dbias, dsink, onehot_t):
    def body(db_ref, ds_ref, oh_ref, ob_ref, os_ref):
        acc = jnp.zeros((N_Q_HEADS, 128), F32)
        for part in _split3(db_ref[...]):
            acc = acc + lax.dot_general(part, oh_ref[...], (((1,), (1,)), ((), ())),
                                        preferred_element_type=F32)
        ob_ref[...] = acc
        os_ref[...] = jnp.broadcast_to(jnp.sum(ds_ref[...], axis=-1, keepdims=True),
                                       os_ref.shape)

    return pl.pallas_call(
        body,
        out_shape=[jax.ShapeDtypeStruct((N_Q_HEADS, 128), F32),
                   jax.ShapeDtypeStruct((N_Q_HEADS, 128), F32)],
        compiler_params=pltpu.CompilerParams(vmem_limit_bytes=VMEM_LIMIT),
        name="bias_bucket_reduce",
    )(dbias, dsink, onehot_t)


def _disc(lr, li, ls, btr, bti):
    lam_re = jnp.minimum(lr, -1e-4)
    delta = jnp.exp(ls)
    mag = jnp.exp(lam_re * delta)
    ang = li * delta
    ar, ai = mag * jnp.cos(ang), mag * jnp.sin(ang)
    nr, ni = ar - 1.0, ai
    den = lam_re * lam_re + li * li
    fr = (nr * lam_re + ni * li) / den
    fi = (ni * lam_re - nr * li) / den
    bbr = fr * btr - fi * bti
    bbi = fr * bti + fi * btr
    return ar, ai, bbr, bbi


def _block_mask():
    row = lax.broadcasted_iota(jnp.int32, (SSM_W, SSM_H), 0)
    col = lax.broadcasted_iota(jnp.int32, (SSM_W, SSM_H), 1)
    return (row // SSM_P) == (col // SSM_N)


def _ssm_setup(lr, li, ls, btr, bti, ctr, cti):
    def body(lr_ref, li_ref, ls_ref, btr_ref, bti_ref, ctr_ref, cti_ref, a_ref, b_ref, c_ref):
        ar, ai, bbr, bbi = _disc(lr_ref[...], li_ref[...], ls_ref[...], btr_ref[...], bti_ref[...])
        a_ref[:, :SSM_H] = ar
        a_ref[:, SSM_H:] = ai
        mask = _block_mask()
        blk = lambda t: jnp.where(mask, jnp.tile(t, (SSM_G, 1)), 0.0)
        b_ref[:, :SSM_H] = blk(bbr).astype(BF16)
        b_ref[:, SSM_H:] = blk(bbi).astype(BF16)
        c_ref[:, :SSM_H] = blk(ctr_ref[...]).astype(BF16)
        c_ref[:, SSM_H:] = blk(-cti_ref[...]).astype(BF16)

    return pl.pallas_call(
        body,
        out_shape=[jax.ShapeDtypeStruct((1, 2 * SSM_H), F32),
                   jax.ShapeDtypeStruct((SSM_W, 2 * SSM_H), BF16),
                   jax.ShapeDtypeStruct((SSM_W, 2 * SSM_H), BF16)],
        compiler_params=pltpu.CompilerParams(vmem_limit_bytes=VMEM_LIMIT),
        name="ssm_setup",
    )(lr, li, ls, btr, bti, ctr, cti)


def _ssm_param_bwd(lr, li, ls, btr, bti, dacc, dbcat, dccat, gind):
    def body(lr_ref, li_ref, ls_ref, btr_ref, bti_ref, dacc_ref, db_ref, dc_ref, g_ref,
             dlr_ref, dli_ref, dls_ref, dbtr_ref, dbti_ref, dctr_ref, dcti_ref):
        dar = jnp.sum(dacc_ref[:, :SSM_H], axis=0, keepdims=True)
        dai = jnp.sum(dacc_ref[:, SSM_H:], axis=0, keepdims=True)
        col = lax.broadcasted_iota(jnp.int32, (SSM_P, 2 * SSM_H), 1)
        grp = (col % SSM_H) // SSM_N
        db = jnp.zeros((SSM_P, 2 * SSM_H), F32)
        dc = jnp.zeros((SSM_P, 2 * SSM_H), F32)
        half = SSM_G // 2
        for g in range(SSM_G):
            sel = grp == g
            r0 = (g % half) * SSM_P
            db = db + jnp.where(sel, db_ref[r0:r0 + SSM_P, :], 0.0)
            dc = dc + jnp.where(sel, dc_ref[r0:r0 + SSM_P, :], 0.0)
        dctr_ref[...] = dc[:, :SSM_H]
        dcti_ref[...] = -dc[:, SSM_H:]
        prim = (lr_ref[...], li_ref[...], ls_ref[...], btr_ref[...], bti_ref[...])
        _, vjp = jax.vjp(_disc, *prim)
        dlr, dli, dls, dbtr, dbti = vjp((dar, dai, db[:, :SSM_H], db[:, SSM_H:]))
        dlr_ref[...] = dlr
        dli_ref[...] = dli
        dbtr_ref[...] = dbtr
        dbti_ref[...] = dbti
        acc = jnp.zeros((8, 128), F32)
        for part in _split3(jnp.broadcast_to(dls, (8, SSM_H))):
            acc = acc + jnp.dot(part, g_ref[...], preferred_element_type=F32)
        dls_ref[...] = acc

    vec = jax.ShapeDtypeStruct((1, SSM_H), F32)
    mat = jax.ShapeDtypeStruct((SSM_P, SSM_H), F32)
    return pl.pallas_call(
        body,
        out_shape=[vec, vec, jax.ShapeDtypeStruct((8, 128), F32), mat, mat, mat, mat],
        compiler_params=pltpu.CompilerParams(vmem_limit_bytes=VMEM_LIMIT),
        name="ssm_param_bwd",
    )(lr, li, ls, btr, bti, dacc, dbcat, dccat, gind)


SCAN_TR = 256


def _cmul_add(vr, vi, pr, pi, sr, si):
    return vr + pr * sr - pi * si, vi + pr * si + pi * sr


def _bcast_row(v, row, which):
    return jnp.broadcast_to(v[which:which + 1, :], v.shape)


def _scan_tables(a_ref, tab_ref, reverse):
    H = SSM_H
    ar = jnp.broadcast_to(a_ref[:, :H], (8, H))
    ai = jnp.broadcast_to(a_ref[:, H:], (8, H))
    if reverse:
        ai = -ai
    row = lax.broadcasted_iota(jnp.int32, (8, H), 0)
    pw = [(ar, ai)]
    for _ in range(7):
        cr, ci = pw[-1]
        pw.append((cr * ar - ci * ai, cr * ai + ci * ar))
    pcr = jnp.zeros((8, H), F32)
    pci = jnp.zeros((8, H), F32)
    for e in range(8):
        sel = (row == (7 - e)) if reverse else (row == e)
        pcr = jnp.where(sel, pw[e][0], pcr)
        pci = jnp.where(sel, pw[e][1], pci)
    tab_ref[0, :, :H] = pcr
    tab_ref[0, :, H:] = pci
    for t, k in enumerate((1, 2, 4)):
        keep = (row < 8 - k) if reverse else (row >= k)
        tab_ref[1 + t, :, :H] = jnp.where(keep, pw[k - 1][0], 0.0)
        tab_ref[1 + t, :, H:] = jnp.where(keep, pw[k - 1][1], 0.0)


def _scan_group(vr, vi, cr, ci, tab_ref, reverse):
    H = SSM_H
    for t, k in enumerate((1, 2, 4)):
        sh = 8 - k if reverse else k
        vr, vi = _cmul_add(vr, vi, tab_ref[1 + t, :, :H], tab_ref[1 + t, :, H:],
                           pltpu.roll(vr, sh, 0), pltpu.roll(vi, sh, 0))
    return _cmul_add(vr, vi, tab_ref[0, :, :H], tab_ref[0, :, H:], cr, ci)


def _blockdiag_expand(x, w_ref, out_ref):
    hw, cb = SSM_W // 2, SSM_H // 2
    for j in range(4):
        h = j % 2
        out_ref[:, j * cb:(j + 1) * cb] = jnp.dot(
            x[:, h * hw:(h + 1) * hw], w_ref[h * hw:(h + 1) * hw, j * cb:(j + 1) * cb],
            preferred_element_type=F32)


def _blockdiag_contract(x_ref, w_ref):
    hw, cb = SSM_W // 2, SSM_H // 2
    nt = (((1,), (1,)), ((), ()))
    halves = []
    for h in range(2):
        acc = None
        for j in (h, 2 + h):
            part = lax.dot_general(x_ref[:, j * cb:(j + 1) * cb],
                                   w_ref[h * hw:(h + 1) * hw, j * cb:(j + 1) * cb], nt,
                                   preferred_element_type=F32)
            acc = part if acc is None else acc + part
        halves.append(acc)
    return jnp.concatenate(halves, axis=1)


def _scan_fwd(proj, u_blk, bcat, ccat, abar, n_rows):
    H = SSM_H
    nt = n_rows // SCAN_TR

    def body(u_ref, b_ref, c_ref, a_ref, xs_ref, xp_ref, yc_ref, bu_ref, tab_ref, carry_ref):
        @pl.when(pl.program_id(0) == 0)
        def _():
            _scan_tables(a_ref, tab_ref, False)
            carry_ref[...] = jnp.zeros(carry_ref.shape, F32)

        _blockdiag_expand(u_ref[...].astype(BF16), b_ref, bu_ref)
        row = lax.broadcasted_iota(jnp.int32, (8, H), 0)

        def group(j, carry):
            cr, ci = carry
            r0 = pl.multiple_of(j * 16, 16)
            xr, xi = [], []
            for half in range(2):
                rr = pl.multiple_of(r0 + 8 * half, 8)
                vr, vi = _scan_group(bu_ref[pl.ds(rr, 8), :H], bu_ref[pl.ds(rr, 8), H:],
                                     cr, ci, tab_ref, False)
                xp_ref[pl.ds(rr, 8), :H] = jnp.where(row == 0, cr, pltpu.roll(vr, 1, 0))
                xp_ref[pl.ds(rr, 8), H:] = jnp.where(row == 0, ci, pltpu.roll(vi, 1, 0))
                cr, ci = _bcast_row(vr, row, 7), _bcast_row(vi, row, 7)
                xr.append(vr)
                xi.append(vi)
            xs_ref[pl.ds(r0, 16), :H] = jnp.concatenate(xr, axis=0).astype(BF16)
            xs_ref[pl.ds(r0, 16), H:] = jnp.concatenate(xi, axis=0).astype(BF16)
            return cr, ci

        cr, ci = lax.fori_loop(0, SCAN_TR // 16, group,
                               (carry_ref[:, :H], carry_ref[:, H:]))
        carry_ref[:, :H] = cr
        carry_ref[:, H:] = ci
        yc_ref[...] = _blockdiag_contract(xs_ref, c_ref)

    tile = lambda w: pl.BlockSpec((SCAN_TR, w), lambda i: (i, 0))
    whole = lambda a: pl.BlockSpec(a.shape, lambda i: (0, 0))
    return pl.pallas_call(
        body,
        grid=(nt,),
        in_specs=[pl.BlockSpec((SCAN_TR, SSM_W), lambda i: (i, u_blk)),
                  whole(bcat), whole(ccat), whole(abar)],
        out_specs=[tile(2 * H), tile(2 * H), tile(SSM_W)],
        out_shape=[jax.ShapeDtypeStruct((n_rows, 2 * H), BF16),
                   jax.ShapeDtypeStruct((n_rows, 2 * H), F32),
                   jax.ShapeDtypeStruct((n_rows, SSM_W), F32)],
        scratch_shapes=[pltpu.VMEM((SCAN_TR, 2 * H), F32), pltpu.VMEM((4, 8, 2 * H), F32),
                        pltpu.VMEM((8, 2 * H), F32)],
        compiler_params=_cparams(("arbitrary",)),
        name="ssm_scan_fwd",
    )(proj, bcat, ccat, abar)


def _scan_bwd(dy, xprev, bcat, ccat, abar, n_rows):
    H = SSM_H
    nt = n_rows // SCAN_TR

    def body(dy_ref, xp_ref, b_ref, c_ref, a_ref, h_ref, da_ref, du_ref, g_ref, tab_ref, carry_ref):
        @pl.when(pl.program_id(0) == 0)
        def _():
            _scan_tables(a_ref, tab_ref, True)
            carry_ref[...] = jnp.zeros(carry_ref.shape, F32)
            da_ref[...] = jnp.zeros(da_ref.shape, F32)

        _blockdiag_expand(dy_ref[...], c_ref, g_ref)
        row = lax.broadcasted_iota(jnp.int32, (8, H), 0)
        n16 = SCAN_TR // 16

        def group(jj, carry):
            cr, ci = carry
            r0 = pl.multiple_of((n16 - 1 - jj) * 16, 16)
            hr, hi = [None, None], [None, None]
            for half in (1, 0):
                rr = pl.multiple_of(r0 + 8 * half, 8)
                vr, vi = _scan_group(g_ref[pl.ds(rr, 8), :H], g_ref[pl.ds(rr, 8), H:],
                                     cr, ci, tab_ref, True)
                pr, pi = xp_ref[pl.ds(rr, 8), :H], xp_ref[pl.ds(rr, 8), H:]
                da_ref[:, :H] += vr * pr + vi * pi
                da_ref[:, H:] += vi * pr - vr * pi
                cr, ci = _bcast_row(vr, row, 0), _bcast_row(vi, row, 0)
                hr[half], hi[half] = vr, vi
            h_ref[pl.ds(r0, 16), :H] = jnp.concatenate(hr, axis=0).astype(BF16)
            h_ref[pl.ds(r0, 16), H:] = jnp.concatenate(hi, axis=0).astype(BF16)
            return cr, ci

        cr, ci = lax.fori_loop(0, n16, group, (carry_ref[:, :H], carry_ref[:, H:]))
        carry_ref[:, :H] = cr
        carry_ref[:, H:] = ci
        du_ref[...] = _blockdiag_contract(h_ref, b_ref)

    rev = lambda i: (nt - 1 - i, 0)
    whole = lambda a: pl.BlockSpec(a.shape, lambda i: (0, 0))
    return pl.pallas_call(
        body,
        grid=(nt,),
        in_specs=[pl.BlockSpec((SCAN_TR, SSM_W), rev),
                  pl.BlockSpec((SCAN_TR, 2 * H), rev),
                  whole(bcat), whole(ccat), whole(abar)],
        out_specs=[pl.BlockSpec((SCAN_TR, 2 * H), rev),
                   pl.BlockSpec((8, 2 * H), lambda i: (0, 0)),
                   pl.BlockSpec((SCAN_TR, SSM_W), rev)],
        out_shape=[jax.ShapeDtypeStruct((n_rows, 2 * H), BF16),
                   jax.ShapeDtypeStruct((8, 2 * H), F32),
                   jax.ShapeDtypeStruct((n_rows, SSM_W), F32)],
        scratch_shapes=[pltpu.VMEM((SCAN_TR, 2 * H), F32), pltpu.VMEM((4, 8, 2 * H), F32),
                        pltpu.VMEM((8, 2 * H), F32)],
        compiler_params=_cparams(("arbitrary",)),
        name="ssm_scan_bwd",
    )(dy, xprev, bcat, ccat, abar)


def _adamw(parts, w, m, v, *, tr, ch, name, prefetch=None):
    n_rows, cols = w.shape
    n_parts = len(parts)
    c1 = 1.0 - ADAM_B1 ** ADAM_STEP
    c2 = 1.0 - ADAM_B2 ** ADAM_STEP

    def fn(rv, vv, i, nt):
        g = rv[0].astype(F32)
        for p in rv[1:n_parts]:
            g = g + p.astype(F32)
        wv, mv, vval = rv[n_parts:]
        nm = ADAM_B1 * mv + (1.0 - ADAM_B1) * g
        nv = ADAM_B2 * vval + (1.0 - ADAM_B2) * (g * g)
        delta = -ADAM_LR * ((nm / c1) / (jnp.sqrt(nv / c2) + ADAM_EPS) + ADAM_WD * wv)
        return [g, delta, nm, nv], []

    rows = [_row(arr, lead=lead) for (arr, lead) in parts] + [_row(w), _row(m), _row(v)]
    return _rowwise(fn, rows, [], [(cols, F32)] * 4, [], n_rows=n_rows, tr=tr, ch=ch, name=name,
                    prefetch=prefetch)


_PACK = [
    ("b_ada", 6), ("norm1_g", 1), ("b_in", 3), ("norm2_g", 1), ("final_g", 1),
    ("lambda_re", 1), ("lambda_im", 1), ("log_step", 1), ("attn_sinks", 1),
    ("rel_bias", 1), ("b_glu", 1), ("ssm_d", 1), ("loss", 1),
    ("ssm_b_re", 16), ("ssm_b_im", 16), ("ssm_c_re", 16), ("ssm_c_im", 16),
]
_PACK_OFF = {}
_off = 0
for _n, _r in _PACK:
    _PACK_OFF[_n] = (_off, _r)
    _off += _r
PACK_ROWS = -(-_off // 8) * 8


def _to_rows(a, rows):
    flat = a.reshape(-1).astype(F32)
    pad = rows * PACK_W - flat.shape[0]
    if pad:
        flat = jnp.pad(flat, (0, pad))
    return flat.reshape(rows, PACK_W)


def _b_to_rows(b):
    return jnp.transpose(b, (2, 0, 1)).reshape(SSM_P, SSM_H)


def _rows_to_b(r):
    return jnp.transpose(r.reshape(SSM_P, SSM_G, SSM_N), (1, 2, 0))


def _c_to_rows(cm):
    return jnp.transpose(cm, (1, 0, 2)).reshape(SSM_P, SSM_H)


def _rows_to_c(r):
    return jnp.transpose(r.reshape(SSM_P, SSM_G, SSM_N), (1, 0, 2))


def _pack(vals):
    out = jnp.zeros((PACK_ROWS, PACK_W), F32)
    for n, r in _PACK:
        if n in vals:
            pieces = vals[n] if isinstance(vals[n], list) else [vals[n]]
            rows_each = r // len(pieces)
            for i, piece in enumerate(pieces):
                out = lax.dynamic_update_slice(out, _to_rows(piece, rows_each),
                                               (_PACK_OFF[n][0] + i * rows_each, 0))
    return out


def _unpack(packed, name, shape):
    o, r = _PACK_OFF[name]
    n = int(np.prod(shape))
    return packed[o:o + r].reshape(-1)[:n].reshape(shape)


def _small_params_packed(p):
    return {
        "b_ada": p["b_ada"], "norm1_g": p["norm1_g"], "b_in": p["b_in"],
        "norm2_g": p["norm2_g"], "final_g": p["final_g"],
        "lambda_re": p["lambda_re"], "lambda_im": p["lambda_im"],
        "log_step": p["log_step"], "attn_sinks": p["attn_sinks"],
        "rel_bias": p["rel_bias"], "b_glu": p["b_glu"], "ssm_d": p["ssm_d"],
        "ssm_b_re": _b_to_rows(p["ssm_b_re"][0]), "ssm_b_im": _b_to_rows(p["ssm_b_im"][0]),
        "ssm_c_re": _c_to_rows(p["ssm_c_re"][0]), "ssm_c_im": _c_to_rows(p["ssm_c_im"][0]),
    }


_SMALL_SHAPES = {
    "b_ada": (1, N_MOD * D), "norm1_g": (1, D), "b_in": (1, IN_W), "norm2_g": (1, D),
    "final_g": (D,), "lambda_re": (1, SSM_G, SSM_N), "lambda_im": (1, SSM_G, SSM_N),
    "log_step": (1, SSM_G), "attn_sinks": (1, N_Q_HEADS), "rel_bias": (NUM_BUCKETS, N_Q_HEADS),
    "b_glu": (1, SSM_W), "ssm_d": (1, SSM_W),
}


def _unpack_small(packed, name):
    if name in ("ssm_b_re", "ssm_b_im"):
        o, r = _PACK_OFF[name]
        return _rows_to_b(packed[o:o + r])[None]
    if name in ("ssm_c_re", "ssm_c_im"):
        o, r = _PACK_OFF[name]
        return _rows_to_c(packed[o:o + r])[None]
    return _unpack(packed, name, _SMALL_SHAPES[name])


WEIGHT_ORDER = ['w_ada', 'b_ada', 'norm1_g', 'w_in', 'b_in', 'attn_sinks', 'rel_bias', 'lambda_re',
                'lambda_im', 'log_step', 'ssm_b_re', 'ssm_b_im', 'ssm_c_re', 'ssm_c_im', 'ssm_d',
                'w_glu', 'b_glu', 'w_attn_proj', 'w_ssm_proj', 'w_out', 'norm2_g', 'w_ff1', 'w_ff2',
                'final_g']
BIG = ['w_in', 'w_glu', 'w_attn_proj', 'w_ssm_proj', 'w_out', 'w_ff1', 'w_ff2']


ADAMW_TILE_ELEMS = 1 << 18


def _to_col_blocks(w):
    k, n = w.shape
    return jnp.transpose(w.reshape(k, N_DEV, n // N_DEV), (1, 0, 2))


def _adamw_rows(rows, cols):
    tr = rows
    while tr * cols > ADAMW_TILE_ELEMS and tr % 32 == 0:
        tr //= 2
    return tr


def _cast_to_slot(w, me1, name, dep=None):
    rows, cols = w.shape
    tr = min(rows, 256)
    n_dep = 0 if dep is None else 1

    def body(me_ref, w_ref, *rest):
        rest[-1][...] = w_ref[...].astype(BF16)

    return pl.pallas_call(
        body,
        grid_spec=pltpu.PrefetchScalarGridSpec(
            num_scalar_prefetch=1, grid=(rows // tr,),
            in_specs=[pl.BlockSpec((tr, cols), lambda i, me_ref: (i, 0))]
            + [pl.BlockSpec(memory_space=pl.ANY)] * n_dep,
            out_specs=pl.BlockSpec((None, tr, cols), lambda i, me_ref: (me_ref[0], i, 0))),
        out_shape=jax.ShapeDtypeStruct((N_DEV, rows, cols), BF16),
        compiler_params=_cparams(("arbitrary",)),
        name=name,
    )(me1, w, *([dep] if n_dep else []))


def kernel(x, c, w_ada, b_ada, norm1_g, w_in, b_in, attn_sinks, rel_bias, lambda_re, lambda_im, log_step, ssm_b_re, ssm_b_im, ssm_c_re, ssm_c_im, ssm_d, w_glu, b_glu, w_attn_proj, w_ssm_proj, w_out, norm2_g, w_ff1, w_ff2, final_g, loss_target, m_w_ada, m_b_ada, m_norm1_g, m_w_in, m_b_in, m_attn_sinks, m_rel_bias, m_lambda_re, m_lambda_im, m_log_step, m_ssm_b_re, m_ssm_b_im, m_ssm_c_re, m_ssm_c_im, m_ssm_d, m_w_glu, m_b_glu, m_w_attn_proj, m_w_ssm_proj, m_w_out, m_norm2_g, m_w_ff1, m_w_ff2, m_final_g, v_w_ada, v_b_ada, v_norm1_g, v_w_in, v_b_in, v_attn_sinks, v_rel_bias, v_lambda_re, v_lambda_im, v_log_step, v_ssm_b_re, v_ssm_b_im, v_ssm_c_re, v_ssm_c_im, v_ssm_d, v_w_glu, v_b_glu, v_w_attn_proj, v_w_ssm_proj, v_w_out, v_norm2_g, v_w_ff1, v_w_ff2, v_final_g):
    loc = dict(locals())
    W = {n: loc[n] for n in WEIGHT_ORDER}
    Mo = {n: loc["m_" + n] for n in WEIGHT_ORDER}
    Vo = {n: loc["v_" + n] for n in WEIGHT_ORDER}
    S = x.shape[1]
    TM = min(512, S)
    TS = min(1024, S)
    TR = min(256, S)
    TW = min(1024, S)
    me = 4 * lax.axis_index("x") + 2 * lax.axis_index("y") + lax.axis_index("c")
    x2d = x.reshape(S, D)
    tgt = loss_target.reshape(S, D)

    c_all = _small_allgather(c, "allgather_c").reshape(N_DEV, D)
    cs = _rowwise(lambda rv, vv, i, nt: ([rv[0] * _sigmoid(rv[0])], []), [_row(c_all)], [],
                  [(D, F32)], [], n_rows=N_DEV, tr=8, ch=8, name="silu_c")[0]
    n_ada = N_MOD * D // N_DEV
    b_ada_cols = lax.dynamic_slice(b_ada, (0, me * n_ada), (1, n_ada))
    mod_piece = _matmul(cs, w_ada[0], mode="nn", dims=(N_DEV, n_ada, D), tiles=(N_DEV, 512, D),
                        out_dtypes=[F32], name="ada_fwd", bias=b_ada_cols)
    mod_all = _small_allgather(mod_piece, "allgather_mod")
    mod_b = lax.dynamic_index_in_dim(mod_all, me, axis=1, keepdims=False).reshape(N_MOD, D)
    sh1, sc1, g1, sh2, sc2, g2 = [mod_b[i:i + 1] for i in range(N_MOD)]

    shard = {n: W[n][0] for n in BIG}
    me1 = jnp.reshape(me, (1,)).astype(jnp.int32)
    zone = {"w_in": _cast_to_slot(shard["w_in"], me1, "cast_w_in")}
    (in_flight,), tok_in = _relay_gather_start([zone["w_in"]], "w_in_start", mod_all)
    for n in BIG[1:]:
        zone[n] = _cast_to_slot(shard[n], me1, "cast_" + n, dep=tok_in)
    G = {}

    def f_norm1(rv, vv, i, nt):
        xv, (g, sc, sh) = rv[0], vv
        return [(xv * _rms(xv) * g) * (1.0 + sc) + sh], []

    h = _rowwise(f_norm1, [_row(x2d)], [norm1_g, sc1, sh1], [(D, BF16)], [],
                 n_rows=S, tr=TR, ch=32, name="norm1_fwd", dep=zone["w_ff2"])[0]
    (zone_in,) = _relay_gather_arrive([in_flight], h, "w_in_arrive")
    (in_pass,), tok_p = _relay_pass_start([zone_in], "w_in_pass_start")
    mixer = ["w_attn_proj", "w_glu", "w_ssm_proj", "w_out"]
    flights, tok_w = _exchange_start([zone[n] for n in mixer], "gather", "weights_start", tok_p)
    w_flight = dict(zip(mixer, flights))
    ff_flights, tok_w = _relay_gather_start([zone["w_ff1"], zone["w_ff2"]], "ff_weights_start", tok_w)
    (G["w_in"],) = _relay_pass_wait([in_pass], tok_w, "w_in_pass_wait")
    proj = _matmul(h, G["w_in"], mode="nn", dims=(S, IN_W, D), tiles=(TW, 768, D),
                   out_dtypes=[BF16], name="in_proj", b3=True, bias=b_in, dep=tok_w)

    buckets = _t5_buckets_block()
    band = _band_mask()
    onehot_t = jnp.asarray(
        (np.arange(128)[:, None] == buckets.reshape(-1)[None, :]).astype(np.float32), BF16)
    band_first = band & (np.arange(2 * BLK)[None, :] >= BLK)
    rel_bias_t = jnp.pad(jnp.transpose(rel_bias), ((0, 0), (0, 128 - NUM_BUCKETS)))
    bias2 = _bias_tables(rel_bias_t, onehot_t,
                         jnp.asarray(band_first.reshape(1, -1).astype(np.float32)),
                         jnp.asarray(band.reshape(1, -1).astype(np.float32))
                         ).reshape(2, N_Q_HEADS * BLK, 2 * BLK)
    sinkcol = jnp.repeat(attn_sinks.reshape(N_Q_HEADS), BLK).reshape(N_Q_HEADS * BLK, 1)
    attn = _attention_fwd(proj, bias2, sinkcol, S)
    landed = _exchange_wait([w_flight[n] for n in mixer], "gather", attn, "weights_wait_mixer")
    G.update((n, bufs[0]) for n, bufs in zip(mixer, landed))
    w_glu_f = G["w_glu"].reshape(SSM_W, SSM_W)
    w_out_f = G["w_out"].reshape(D, D)
    w_ap_f = jnp.transpose(G["w_attn_proj"], (1, 0, 2)).reshape(ATTN_W, D)
    w_sp_f = jnp.transpose(G["w_ssm_proj"], (1, 0, 2)).reshape(SSM_W, D)
    y_attn = _matmul(attn, w_ap_f, mode="nn", dims=(S, D, ATTN_W), tiles=(TW, 1024, ATTN_W),
                     out_dtypes=[BF16], name="attn_proj")

    lam_re = lambda_re.reshape(1, SSM_H)
    lam_im = lambda_im.reshape(1, SSM_H)
    ls_x = jnp.repeat(log_step.reshape(SSM_G), SSM_N).reshape(1, SSM_H)
    btr, bti = _b_to_rows(ssm_b_re[0]), _b_to_rows(ssm_b_im[0])
    ctr, cti = _c_to_rows(ssm_c_re[0]), _c_to_rows(ssm_c_im[0])
    abar, bcat, ccat = _ssm_setup(lam_re, lam_im, ls_x, btr, bti, ctr, cti)
    u_blk = (ATTN_W + 2 * KV_W) // SSM_W
    xs, xprev, yc = _scan_fwd(proj, u_blk, bcat, ccat, abar, S)

    def f_ssm_out(rv, vv, i, nt):
        y = rv[0] + vv[0] * rv[1]
        return [y, _gelu(y)], []

    y_ssm_pre, z = _rowwise(f_ssm_out, [_row(yc), _row(proj, u_blk, SSM_W)], [ssm_d],
                            [(SSM_W, F32), (SSM_W, BF16)], [], n_rows=S, tr=TM, ch=32, name="ssm_out")
    zg = _matmul(z, w_glu_f, mode="nn", dims=(S, SSM_W, SSM_W), tiles=(TM, SSM_W, SSM_W),
                 out_dtypes=[F32], name="glu_proj", bias=b_glu)
    z2 = _rowwise(lambda rv, vv, i, nt: ([rv[0].astype(F32) * _sigmoid(rv[1])], []),
                  [_row(z), _row(zg)], [], [(SSM_W, BF16)], [], n_rows=S, tr=TM, ch=32, name="glu_gate")[0]
    y_ssm = _matmul(z2, w_sp_f, mode="nn", dims=(S, D, SSM_W), tiles=(TW, 1024, SSM_W),
                    out_dtypes=[BF16], name="ssm_proj")

    ga_row = _row(proj, 1, D)
    gs_row = _row(proj, 2, D)

    def f_merge(rv, vv, i, nt):
        ga, gs, ya, ys = rv
        return [_sigmoid(ga) * ya + _sigmoid(gs) * ys], []

    merged = _rowwise(f_merge, [ga_row, gs_row, _row(y_attn), _row(y_ssm)], [], [(D, BF16)], [],
                      n_rows=S, tr=TR, ch=32, name="merge")[0]
    mo = _matmul(merged, w_out_f, mode="nn", dims=(S, D, D), tiles=(TW, 1024, D),
                 out_dtypes=[BF16], name="out_proj")

    ff_zones = _relay_gather_arrive(ff_flights, mo, "ff_weights_arrive")
    ff_pass, tok_fp = _relay_pass_start(ff_zones, "ff_weights_pass_start")

    def f_norm2(rv, vv, i, nt):
        xv, mv = rv
        g1v, g, sc, sh = vv
        x1v = xv + g1v * mv
        return [x1v, (x1v * _rms(x1v) * g) * (1.0 + sc) + sh], []

    x1, h2 = _rowwise(f_norm2, [_row(x2d), _row(mo)], [g1, norm2_g, sc2, sh2],
                      [(D, F32), (D, BF16)], [], n_rows=S, tr=TR, ch=32, name="norm2_fwd", dep=tok_fp)

    def relu_sq(acc):
        r = jnp.maximum(acc, 0.0)
        return r * r, r

    (G["w_ff1"],) = _relay_pass_wait(ff_pass[:1], h2, "w_ff1_pass_wait")
    act, relu = _matmul(h2, G["w_ff1"], mode="nn", dims=(S, D_FF, D), tiles=(TW, 1024, D),
                        out_dtypes=[BF16, BF16], name="ff1", b3=True, epilogue=relu_sq)
    w_ff2_f = _relay_pass_wait(ff_pass[1:], act, "w_ff2_pass_wait")[0].reshape(D_FF, D)
    ff = _matmul(act, w_ff2_f, mode="nn", dims=(S, D, D_FF), tiles=(TW, 1024, 2048),
                 out_dtypes=[BF16], name="ff2")

    def f_loss(rv, vv, i, nt):
        x1v, ffv, tv = rv
        g2v, gf = vv
        x2v = x1v + g2v * ffv
        r = _rms(x2v)
        xh = x2v * r
        diff = xh * gf - tv
        dy = diff * (1.0 / D)
        dxh = dy * gf
        dx2 = r * (dxh - xh * jnp.mean(dxh * xh, axis=-1, keepdims=True))
        return [dx2, dx2 * g2v], [_colsum(0.5 * diff * diff * (1.0 / D)), _colsum(dy * xh),
                                  _colsum(dx2 * ffv)]

    dx2, dff, loss_cols, d_final_g, dg2 = _rowwise(
        f_loss, [_row(x1), _row(ff), _row(tgt)], [g2, final_g.reshape(1, D)],
        [(D, F32), (D, BF16)], [(1, D)] * 3, n_rows=S, tr=TR, ch=32, name="loss_bwd")

    df1 = _matmul(dff, w_ff2_f, mode="nt", dims=(S, D_FF, D), tiles=(TW, 1024, D),
                  out_dtypes=[BF16], name="ff2_dgrad", extras=(relu,),
                  epilogue=lambda acc, r: (acc * (2.0 * r.astype(F32)),))
    gw_ff2 = _matmul(act, dff, mode="tn", dims=(D_FF, D, S), tiles=(1024, 1024, TS),
                     out_dtypes=[BF16], name="ff2_wgrad").reshape(N_DEV, D_FF // N_DEV, D)
    g_flight = {}
    (g_flight["w_ff2"],), tok = _exchange_start([gw_ff2], "scatter", "grads_start_ff2")
    dh2 = _matmul(df1, G["w_ff1"], mode="nt", dims=(S, D, D_FF), tiles=(TM, D, 1024),
                  out_dtypes=[BF16], name="ff1_dgrad", b3=True, dep=tok)
    gw_ff1 = _matmul(h2, df1, mode="tn", dims=(D, D_FF, S), tiles=(1024, 1024, TS),
                     out_dtypes=[BF16], name="ff1_wgrad", out3=True)
    (g_flight["w_ff1"],), tok = _exchange_start([gw_ff1], "scatter", "grads_start_ff1")

    def f_norm2_bwd(rv, vv, i, nt):
        x1v, dh, dx2v, mv = rv
        g, sc, g1v = vv
        r = _rms(x1v)
        xh = x1v * r
        t = xh * g
        dt = dh * (1.0 + sc)
        dxh = dt * g
        dx1 = dx2v + r * (dxh - xh * jnp.mean(dxh * xh, axis=-1, keepdims=True))
        return [dx1, dx1 * g1v], [_colsum(dh), _colsum(dh * t), _colsum(dt * xh), _colsum(dx1 * mv)]

    dx1, dmo, dsh2, dsc2, d_norm2_g, dg1 = _rowwise(
        f_norm2_bwd, [_row(x1), _row(dh2), _row(dx2), _row(mo)], [norm2_g, sc2, g1],
        [(D, F32), (D, BF16)], [(1, D)] * 4, n_rows=S, tr=TR, ch=16, name="norm2_bwd", dep=tok)

    dmerged = _matmul(dmo, w_out_f, mode="nt", dims=(S, D, D), tiles=(TW, 1024, D),
                      out_dtypes=[BF16], name="out_dgrad")
    gw_out = _matmul(merged, dmo, mode="tn", dims=(D, D, S), tiles=(1024, 1024, TS),
                     out_dtypes=[BF16], name="out_wgrad").reshape(N_DEV, D // N_DEV, D)
    (g_flight["w_out"],), tok = _exchange_start([gw_out], "scatter", "grads_start_out")

    def f_merge_bwd(rv, vv, i, nt):
        dm, ga, gs, ya, ys = rv
        sa, ss = _sigmoid(ga), _sigmoid(gs)
        return [dm * sa, dm * ss, dm * ya * sa * (1.0 - sa), dm * ys * ss * (1.0 - ss)], []

    dy_attn, dy_ssm, dga, dgs = _rowwise(
        f_merge_bwd, [_row(dmerged), ga_row, gs_row, _row(y_attn), _row(y_ssm)], [],
        [(D, BF16)] * 4, [], n_rows=S, tr=TR, ch=16, name="merge_bwd", dep=tok)

    dz2 = _matmul(dy_ssm, w_sp_f, mode="nt", dims=(S, SSM_W, D), tiles=(TW, SSM_W, D),
                  out_dtypes=[F32], name="ssm_proj_dgrad")
    gw_ssm_proj = _to_col_blocks(_matmul(z2, dy_ssm, mode="tn", dims=(SSM_W, D, S), tiles=(SSM_W, 1024, TS),
                                         out_dtypes=[BF16], name="ssm_proj_wgrad"))

    def f_glu_bwd(rv, vv, i, nt):
        dz2v, zv, zgv = rv
        sg = _sigmoid(zgv)
        dzg = dz2v * zv.astype(F32) * sg * (1.0 - sg)
        return [dzg, dz2v * sg], [_colsum(dzg)]

    dzg, dz_a, d_b_glu = _rowwise(f_glu_bwd, [_row(dz2), _row(z), _row(zg)], [],
                                  [(SSM_W, BF16), (SSM_W, F32)], [(1, SSM_W)],
                                  n_rows=S, tr=TM, ch=32, name="glu_bwd")
    dz_b = _matmul(dzg, w_glu_f, mode="nt", dims=(S, SSM_W, SSM_W), tiles=(TM, SSM_W, SSM_W),
                   out_dtypes=[F32], name="glu_dgrad")
    gw_glu = _matmul(z, dzg, mode="tn", dims=(SSM_W, SSM_W, S), tiles=(SSM_W, SSM_W, TS),
                     out_dtypes=[BF16], name="glu_wgrad").reshape(N_DEV, SSM_W // N_DEV, SSM_W)
    (g_flight["w_ssm_proj"], g_flight["w_glu"]), tok = _exchange_start(
        [gw_ssm_proj, gw_glu], "scatter", "grads_start_ssm")

    def f_ssm_out_bwd(rv, vv, i, nt):
        dza, dzb, yv, uv = rv
        dy = (dza + dzb) * _gelu_grad(yv)
        return [dy, dy * vv[0]], [_colsum(dy * uv)]

    dy_s, du_a, d_ssm_d = _rowwise(
        f_ssm_out_bwd, [_row(dz_a), _row(dz_b), _row(y_ssm_pre), _row(proj, u_blk, SSM_W)], [ssm_d],
        [(SSM_W, BF16), (SSM_W, F32)], [(1, SSM_W)], n_rows=S, tr=TM, ch=32, name="ssm_out_bwd", dep=tok)
    hw = SSM_W // 2
    u_half = (ATTN_W + 2 * KV_W) // hw
    dccat = _matmul(dy_s, xs, mode="tn", dims=(hw, 2 * SSM_H, S), tiles=(hw, 1024, TS),
                    out_dtypes=[F32], name="ssm_c_wgrad", a_index=lambda i, j, k: (k, j % 2))
    hs, dacc, du_b = _scan_bwd(dy_s, xprev, bcat, ccat, abar, S)
    dbcat = _matmul(proj, hs, mode="tn", dims=(hw, 2 * SSM_H, S), tiles=(hw, 1024, TS),
                    out_dtypes=[F32], name="ssm_b_wgrad", a_index=lambda i, j, k: (k, u_half + j % 2))
    grp = np.arange(SSM_H) // SSM_N
    gind = jnp.asarray((grp[:, None] == np.arange(128)[None, :]).astype(np.float32), BF16)
    d_lam_re, d_lam_im, d_ls, d_btr, d_bti, d_ctr, d_cti = _ssm_param_bwd(
        lam_re, lam_im, ls_x, btr, bti, dacc, dbcat, dccat, gind)

    dattn = _matmul(dy_attn, w_ap_f, mode="nt", dims=(S, ATTN_W, D), tiles=(TW, ATTN_W, D),
                    out_dtypes=[BF16], name="attn_proj_dgrad")
    gw_attn_proj = _to_col_blocks(_matmul(attn, dy_attn, mode="tn", dims=(ATTN_W, D, S), tiles=(ATTN_W, 1024, TS),
                                          out_dtypes=[BF16], name="attn_proj_wgrad"))
    (g_flight["w_attn_proj"],), tok = _exchange_start(
        [gw_attn_proj], "scatter", "grads_start_attn")
    dq, dkc, dkp, dvc, dvp, dbias, dsink = _attention_bwd(proj, attn, dattn, bias2, sinkcol, S)
    d_bias_b, d_sinks = _bucket_reduce(dbias.reshape(N_Q_HEADS, BLK * 2 * BLK),
                                       dsink.reshape(N_Q_HEADS, BLK), onehot_t)

    def f_dproj(rv, vv, i, nt):
        dqv, kc, kp, vc, vp, dua, dub, gav, gsv = rv
        keep = (i < nt - 1).astype(F32)
        dp = jnp.concatenate([dqv.astype(F32), kc + keep * kp, vc + keep * vp, dua + dub,
                              gav.astype(F32), gsv.astype(F32)], axis=-1)
        return [dp], [_colsum(dp)]

    dproj, d_b_in = _rowwise(
        f_dproj, [_row(dq), _row(dkc), _row(dkp, shift=1), _row(dvc), _row(dvp, shift=1),
                  _row(du_a), _row(du_b), _row(dga), _row(dgs)], [],
        [(IN_W, BF16)], [(1, IN_W)], n_rows=S, tr=BLK, ch=16, name="dproj", dep=tok)
    gw_in = _matmul(h, dproj, mode="tn", dims=(D, IN_W, S), tiles=(1024, 768, TS),
                    out_dtypes=[BF16], name="in_wgrad", out3=True)
    (g_flight["w_in"],), tok = _exchange_start([gw_in], "scatter", "grads_start_in")
    dh = _matmul(dproj, G["w_in"], mode="nt", dims=(S, D, IN_W), tiles=(TM, D, 768),
                 out_dtypes=[BF16], name="in_dgrad", b3=True, dep=tok)

    def f_norm1_bwd(rv, vv, i, nt):
        xv, dhv, dx1v = rv
        g, sc = vv
        r = _rms(xv)
        xh = xv * r
        t = xh * g
        dt = dhv * (1.0 + sc)
        dxh = dt * g
        dxv = dx1v + r * (dxh - xh * jnp.mean(dxh * xh, axis=-1, keepdims=True))
        return [dxv], [_colsum(dhv), _colsum(dhv * t), _colsum(dt * xh)]

    grad_x, dsh1, dsc1, d_norm1_g = _rowwise(
        f_norm1_bwd, [_row(x2d), _row(dh), _row(dx1)], [norm1_g, sc1],
        [(D, F32)], [(1, D)] * 3, n_rows=S, tr=TR, ch=32, name="norm1_bwd")

    part = _pack({
        "b_ada": [dsh1, dsc1, dg1, dsh2, dsc2, dg2], "norm1_g": d_norm1_g, "b_in": d_b_in, "norm2_g": d_norm2_g,
        "final_g": d_final_g, "lambda_re": d_lam_re, "lambda_im": d_lam_im,
        "log_step": d_ls[0, :SSM_G], "attn_sinks": d_sinks[:, 0],
        "rel_bias": jnp.transpose(d_bias_b[:, :NUM_BUCKETS]), "b_glu": d_b_glu, "ssm_d": d_ssm_d,
        "loss": loss_cols, "ssm_b_re": d_btr, "ssm_b_im": d_bti, "ssm_c_re": d_ctr, "ssm_c_im": d_cti,
    })
    zone_small = lax.dynamic_update_slice(lax.empty((N_DEV, PACK_ROWS, PACK_W), F32), part[None], (me, 0, 0))
    (small_flight,), after = _exchange_start([zone_small], "gather", "small_grads_start")

    big_out = {}
    for n in ["w_ff2", "w_ff1", "w_out", "w_ssm_proj", "w_glu", "w_attn_proj", "w_in"]:
        own, recv = _exchange_wait([g_flight[n]], "scatter", after, "grads_wait_" + n[2:])[0]
        rows, cols = shard[n].shape
        parts = [(own, lambda m: m[0])] + [
            (recv, lambda m, j=j: jnp.where(j >= m[0], j + 1, j)) for j in range(N_DEV - 1)]
        big_out[n] = _adamw(parts, shard[n], Mo[n][0], Vo[n][0], tr=_adamw_rows(rows, cols), ch=16,
                            name="adamw_" + n, prefetch=me1)
        after = big_out[n][0]

    part_all = _exchange_wait([small_flight], "gather", after, "small_grads_wait")[0][0]
    wp, mp, vp = [_pack(_small_params_packed(p)) for p in (W, Mo, Vo)]
    sg, sdelta, sm, sv = _adamw([(part_all, d) for d in range(N_DEV)], wp, mp, vp,
                                tr=PACK_ROWS, ch=8, name="adamw_small")
    lo, _ = _PACK_OFF["loss"]
    loss = jnp.sum(sg[lo])

    o_ada, _ = _PACK_OFF["b_ada"]
    dmod_all = part_all[:, o_ada:o_ada + N_MOD, :].reshape(N_DEV, N_MOD * D)
    dmod_cols = lax.dynamic_slice(dmod_all, (0, me * n_ada), (N_DEV, n_ada))
    gw_ada = _matmul(cs, dmod_cols, mode="tn", dims=(D, n_ada, N_DEV), tiles=(D, 512, N_DEV),
                     out_dtypes=[F32], name="ada_wgrad")
    big_out["w_ada"] = _adamw([(gw_ada, 0)], w_ada[0], m_w_ada[0], v_w_ada[0],
                              tr=_adamw_rows(D, n_ada), ch=16, name="adamw_w_ada")

    def leaf(kind, n):
        if n in big_out:
            return big_out[n][kind][None]
        return _unpack_small((sg, sdelta, sm, sv)[kind], n)

    outs = [loss, grad_x.reshape(1, S, D)]
    for kind in range(4):
        outs.extend(leaf(kind, n) for n in WEIGHT_ORDER)
    return tuple(outs)
```

```python
import functools
import math

import numpy as np
import jax
import jax.numpy as jnp
from jax import lax
from jax.experimental import pallas as pl
from jax.experimental.pallas import tpu as pltpu

F32 = jnp.float32
BF16 = jnp.bfloat16
MESH = pl.DeviceIdType.MESH

N_DEV = 8
D = 2048
HEAD_DIM = 64
N_Q_HEADS = 16
N_KV_HEADS = 4
GROUP = N_Q_HEADS // N_KV_HEADS
ATTN_W = N_Q_HEADS * HEAD_DIM
KV_W = N_KV_HEADS * HEAD_DIM
BLK = 128
NUM_BUCKETS = 32
MAX_DISTANCE = 128
NEG_INF = -1e30
SSM_W = 512
SSM_P = 16
SSM_G = 32
SSM_N = 64
SSM_H = SSM_G * SSM_N
D_FF = 4 * D
IN_W = ATTN_W + 2 * KV_W + SSM_W + 2 * D
N_MOD = 6
EPS = 1e-6

ADAM_LR = 0.001
ADAM_B1 = 0.9
ADAM_B2 = 0.999
ADAM_EPS = 1e-08
ADAM_WD = 0.01
ADAM_STEP = 10

VMEM_LIMIT = 56 * 1024 * 1024
PACK_W = 2048


def _cparams(sem):
    return pltpu.CompilerParams(dimension_semantics=sem, vmem_limit_bytes=VMEM_LIMIT)


def _matmul(a, b, *, mode, dims, tiles, out_dtypes, name, a_off=0, b3=False,
            out3=False, bias=None, extras=(), epilogue=None, dep=None, a_index=None, b_index=None):
    M, N, K = dims
    tm, tn, tk = tiles
    assert M % tm == 0 and N % tn == 0 and K % tk == 0, (name, dims, tiles)
    gm, gn, gk = M // tm, N // tn, K // tk
    n_extra = len(extras)
    has_bias = bias is not None
    n_out = len(out_dtypes)

    if mode == "nn":
        a_spec = pl.BlockSpec((tm, tk), lambda i, j, k: (i, a_off + k))
        if b3:
            nb = (N // N_DEV) // tn
            assert nb * tn * N_DEV == N
            b_spec = pl.BlockSpec((None, tk, tn), lambda i, j, k: (j // nb, k, j % nb))
        else:
            b_spec = pl.BlockSpec((tk, tn), lambda i, j, k: (k, j))
        dn = (((1,), (0,)), ((), ()))
    elif mode == "nt":
        a_spec = pl.BlockSpec((tm, tk), lambda i, j, k: (i, a_off + k))
        if b3:
            nb = (K // N_DEV) // tk
            assert nb * tk * N_DEV == K
            b_spec = pl.BlockSpec((None, tn, tk), lambda i, j, k: (k // nb, j, k % nb))
        else:
            b_spec = pl.BlockSpec((tn, tk), lambda i, j, k: (j, k))
        dn = (((1,), (1,)), ((), ()))
    else:
        a_spec = pl.BlockSpec((tk, tm), lambda i, j, k: (k, a_off + i))
        b_spec = pl.BlockSpec((tk, tn), lambda i, j, k: (k, j))
        dn = (((0,), (0,)), ((), ()))
    if a_index is not None:
        a_spec = pl.BlockSpec(a_spec.block_shape, a_index)
    if b_index is not None:
        b_spec = pl.BlockSpec(b_spec.block_shape, b_index)

    if out3:
        nbo = (N // N_DEV) // tn
        assert nbo * tn * N_DEV == N
        o_spec = pl.BlockSpec((None, tm, tn), lambda i, j, k: (j // nbo, i, j % nbo))
        o_shape = (N_DEV, M, N // N_DEV)
    else:
        o_spec = pl.BlockSpec((tm, tn), lambda i, j, k: (i, j))
        o_shape = (M, N)

    in_specs = [a_spec, b_spec]
    args = [a, b]
    if has_bias:
        in_specs.append(pl.BlockSpec((1, tn), lambda i, j, k: (0, j)))
        args.append(bias)
    for e in extras:
        in_specs.append(pl.BlockSpec((tm, tn), lambda i, j, k: (i, j)))
        args.append(e)
    n_dep = 0 if dep is None else 1
    if n_dep:
        in_specs.append(pl.BlockSpec(memory_space=pl.ANY))
        args.append(dep)

    def body(*refs):
        a_ref, b_ref = refs[0], refs[1]
        pos = 2
        bias_ref = None
        if has_bias:
            bias_ref = refs[pos]
            pos += 1
        extra_refs = refs[pos:pos + n_extra]
        pos += n_extra + n_dep
        out_refs = refs[pos:pos + n_out]
        acc_ref = refs[pos + n_out] if gk > 1 else None

        part = lax.dot_general(a_ref[...].astype(BF16), b_ref[...].astype(BF16), dn,
                               preferred_element_type=F32)

        def finish(acc):
            if has_bias:
                acc = acc + bias_ref[...]
            if epilogue is None:
                vals = (acc,)
            else:
                vals = epilogue(acc, *[e[...] for e in extra_refs])
            for o_ref, val in zip(out_refs, vals):
                o_ref[...] = val.astype(o_ref.dtype)

        if gk == 1:
            finish(part)
        else:
            k = pl.program_id(2)

            @pl.when(k == 0)
            def _():
                acc_ref[...] = part

            @pl.when(k > 0)
            def _():
                acc_ref[...] += part

            @pl.when(k == gk - 1)
            def _():
                finish(acc_ref[...])

    outs = pl.pallas_call(
        body,
        grid=(gm, gn, gk),
        in_specs=in_specs,
        out_specs=[o_spec] * n_out,
        out_shape=[jax.ShapeDtypeStruct(o_shape, dt) for dt in out_dtypes],
        scratch_shapes=([pltpu.VMEM((tm, tn), F32)] if gk > 1 else []),
        compiler_params=_cparams(("parallel", "parallel", "arbitrary")),
        name=name,
    )(*args)
    return outs[0] if n_out == 1 else outs


def _rowwise(fn, rows, vecs, row_outs, sum_outs, *, n_rows, tr, ch, name, dep=None, prefetch=None):
    assert n_rows % tr == 0 and tr % ch == 0
    nt = n_rows // tr
    nr, nv, nro, nso = len(rows), len(vecs), len(row_outs), len(sum_outs)
    in_specs, args = [], []
    n_pf = 0 if prefetch is None else 1
    for (arr, lead, cblk, w, shift) in rows:
        if shift:
            ridx = lambda i, shift=shift: jnp.minimum(i + shift, nt - 1)
        else:
            ridx = lambda i: i
        if arr.ndim == 3:
            def imap(i, *pf, lead=lead, cblk=cblk, ridx=ridx):
                return (lead(pf[0]) if callable(lead) else lead, ridx(i), cblk)
            in_specs.append(pl.BlockSpec((None, tr, w), imap))
        else:
            in_specs.append(pl.BlockSpec(
                (tr, w), lambda i, *pf, cblk=cblk, ridx=ridx: (ridx(i), cblk)))
        args.append(arr)
    for v in vecs:
        in_specs.append(pl.BlockSpec(v.shape, lambda i, *pf, nd=v.ndim: (0,) * nd))
        args.append(v)
    n_dep = 0 if dep is None else 1
    if n_dep:
        in_specs.append(pl.BlockSpec(memory_space=pl.ANY))
        args.append(dep)
    out_specs = [pl.BlockSpec((tr, w), lambda i, *pf: (i, 0)) for (w, _) in row_outs]
    out_shape = [jax.ShapeDtypeStruct((n_rows, w), dt) for (w, dt) in row_outs]
    for (r, w) in sum_outs:
        out_specs.append(pl.BlockSpec((r, w), lambda i, *pf: (0, 0)))
        out_shape.append(jax.ShapeDtypeStruct((r, w), F32))

    def body(*refs):
        refs = refs[n_pf:]
        i = pl.program_id(0)
        r_in = refs[:nr]
        v_in = refs[nr:nr + nv]
        r_out = refs[nr + nv + n_dep:nr + nv + n_dep + nro]
        s_out = refs[nr + nv + n_dep + nro:]
        s_out, s_acc = s_out[:nso], s_out[nso:]
        if nso:
            @pl.when(i == 0)
            def _():
                for s in s_acc:
                    s[...] = jnp.zeros(s.shape, F32)
        vvals = [v[...] for v in v_in]

        def chunk(ci, carry):
            r0 = pl.multiple_of(ci * ch, ch)
            rv = [r[pl.ds(r0, ch), :].astype(F32) for r in r_in]
            pieces = [fn([v[8 * k:8 * (k + 1)] for v in rv], vvals, i, nt) for k in range(ch // 8)]
            for j, ref in enumerate(r_out):
                val = jnp.concatenate([ro[j] for ro, _ in pieces], axis=0) if ch > 8 else pieces[0][0][j]
                ref[pl.ds(r0, ch), :] = val.astype(ref.dtype)
            for j, ref in enumerate(s_acc):
                ref[...] += functools.reduce(lambda a, b: a + b, [so[j] for _, so in pieces])
            return carry

        lax.fori_loop(0, tr // ch, chunk, 0)
        if nso:
            @pl.when(i == nt - 1)
            def _():
                for s, acc in zip(s_out, s_acc):
                    s[...] = jnp.sum(acc[...], axis=0, keepdims=True)

    outs = pl.pallas_call(
        body,
        grid_spec=pltpu.PrefetchScalarGridSpec(
            num_scalar_prefetch=n_pf, grid=(nt,), in_specs=in_specs, out_specs=out_specs,
            scratch_shapes=[pltpu.VMEM((8, w), F32) for (_, w) in sum_outs]),
        out_shape=out_shape,
        compiler_params=_cparams(("arbitrary",)),
        name=name,
    )(*([prefetch] if n_pf else []), *args)
    return outs


def _row(arr, cblk=0, w=None, lead=0, shift=0):
    return (arr, lead, cblk, arr.shape[-1] if w is None else w, shift)


def _colsum(v):
    parts = [v[8 * k:8 * (k + 1)] for k in range(v.shape[0] // 8)]
    return functools.reduce(lambda a, b: a + b, parts)


def _rms(x):
    return lax.rsqrt(jnp.mean(x * x, axis=-1, keepdims=True) + EPS)


def _sigmoid(x):
    return 1.0 / (1.0 + jnp.exp(-x))


_GELU_C = math.sqrt(2.0 / math.pi)


def _gelu(x):
    return 0.5 * x * (1.0 + jnp.tanh(_GELU_C * (x + 0.044715 * (x * x * x))))


def _gelu_grad(x):
    t = jnp.tanh(_GELU_C * (x + 0.044715 * (x * x * x)))
    return 0.5 * (1.0 + t) + 0.5 * x * (1.0 - t * t) * (_GELU_C * (1.0 + 3.0 * 0.044715 * (x * x)))


def _my_pos():
    return lax.axis_index("x"), lax.axis_index("y"), lax.axis_index("c")


def _flip(pos, k):
    x, y, c = pos
    return (1 - x if k & 4 else x, 1 - y if k & 2 else y, 1 - c if k & 1 else c)


def _dev_id(pos):
    return 4 * pos[0] + 2 * pos[1] + pos[2]


def _small_allgather(x, name):
    r, c = x.shape

    def body(x_ref, out_ref, send_sems, recv_sems):
        me = _my_pos()
        out_ref[_dev_id(me)] = x_ref[...]
        copies = []
        for k in range(1, N_DEV):
            cp = pltpu.make_async_remote_copy(
                src_ref=x_ref, dst_ref=out_ref.at[_dev_id(me)],
                send_sem=send_sems.at[k - 1], recv_sem=recv_sems.at[k - 1],
                device_id=_flip(me, k), device_id_type=MESH)
            cp.start()
            copies.append(cp)
        for k in range(1, N_DEV):
            peer = _flip(me, k)
            pltpu.make_async_remote_copy(
                src_ref=x_ref, dst_ref=out_ref.at[_dev_id(peer)],
                send_sem=send_sems.at[k - 1], recv_sem=recv_sems.at[k - 1],
                device_id=peer, device_id_type=MESH).wait_recv()
        for cp in copies:
            cp.wait_send()

    return pl.pallas_call(
        body,
        out_shape=jax.ShapeDtypeStruct((N_DEV, r, c), x.dtype),
        in_specs=[pl.BlockSpec(memory_space=pltpu.VMEM)],
        out_specs=pl.BlockSpec(memory_space=pltpu.VMEM),
        scratch_shapes=[pltpu.SemaphoreType.DMA((N_DEV - 1,)),
                        pltpu.SemaphoreType.DMA((N_DEV - 1,))],
        compiler_params=pltpu.CompilerParams(vmem_limit_bytes=VMEM_LIMIT),
        name=name,
    )(x)


_HBM = pl.BlockSpec(memory_space=pltpu.HBM)
_SEM = pl.BlockSpec(memory_space=pltpu.SEMAPHORE)
_EFFECT = pltpu.SideEffectType.DATAFLOW_SIDE_EFFECTING


def _relay_copy(zone, send_sems, recv_sems, k, block, to):
    slot = zone.at[_dev_id(block)]
    return pltpu.make_async_remote_copy(
        src_ref=slot, dst_ref=slot, send_sem=send_sems.at[k], recv_sem=recv_sems.at[k],
        device_id=to, device_id_type=MESH)


def _relay_peers():
    x, y, c = _my_pos()
    return (x, y, c), (x, y, 1 - c), [(1 - x, y), (x, 1 - y), (1 - x, 1 - y)]


def _relay_start_call(zones, n_sems, issue, name, after=None):
    n = len(zones)
    n_after = 0 if after is None else 1

    def body(*refs):
        refs = refs[:n] + refs[n + n_after:]
        send, recv, token = refs[n:2 * n], refs[2 * n:3 * n], refs[4 * n]
        for a in range(n):
            issue(refs[a], send[a], recv[a])
        token[...] = jnp.zeros(token.shape, token.dtype)

    sem = pltpu.SemaphoreType.DMA((n_sems,))
    outs = pl.pallas_call(
        body,
        name=name,
        out_shape=([sem] * (2 * n) + [pltpu.HBM(z.shape, z.dtype) for z in zones]
                   + [jax.ShapeDtypeStruct((8, 128), F32)]),
        in_specs=[_HBM] * n + [pl.BlockSpec(memory_space=pl.ANY)] * n_after,
        out_specs=[_SEM] * (2 * n) + [_HBM] * n + [pl.BlockSpec(memory_space=pltpu.VMEM)],
        input_output_aliases={a: 2 * n + a for a in range(n)},
        compiler_params=pltpu.CompilerParams(has_side_effects=_EFFECT),
    )(*[pltpu.with_memory_space_constraint(z, pltpu.HBM) for z in zones],
      *([after] if n_after else []))
    return [(outs[a], outs[n + a], outs[2 * n + a]) for a in range(n)], outs[3 * n]


def _relay_wait_call(flights, settle, after, name):
    n = len(flights)

    def body(*refs):
        send, recv = refs[n:2 * n], refs[2 * n:3 * n]
        for a in range(n):
            settle(refs[a], send[a], recv[a])

    outs = pl.pallas_call(
        body,
        name=name,
        out_shape=[pltpu.HBM(f[2].shape, f[2].dtype) for f in flights],
        in_specs=[_HBM] * n + [_SEM] * (2 * n) + [pl.BlockSpec(memory_space=pl.ANY)],
        out_specs=[_HBM] * n,
        input_output_aliases={a: a for a in range(n)},
        compiler_params=pltpu.CompilerParams(has_side_effects=_EFFECT),
    )(*[f[2] for f in flights], *[f[0] for f in flights], *[f[1] for f in flights], after)
    return list(outs)


def _relay_gather_start(zones, name, after=None):
    def issue(zone, send, recv):
        me, sib, chips = _relay_peers()
        _relay_copy(zone, send, recv, 0, me, sib).start()
        for j, chip in enumerate(chips):
            _relay_copy(zone, send, recv, 1 + j, me, (*chip, me[2])).start()
    return _relay_start_call(zones, 4, issue, name, after)


def _relay_gather_arrive(flights, after, name):
    def settle(zone, send, recv):
        me, sib, chips = _relay_peers()
        _relay_copy(zone, send, recv, 0, sib, me).wait_recv()
        _relay_copy(zone, send, recv, 0, me, sib).wait_send()
        for j, chip in enumerate(chips):
            _relay_copy(zone, send, recv, 1 + j, (*chip, me[2]), me).wait_recv()
            _relay_copy(zone, send, recv, 1 + j, me, (*chip, me[2])).wait_send()
    return _relay_wait_call(flights, settle, after, name)


def _relay_pass_start(zones, name, after=None):
    def issue(zone, send, recv):
        me, sib, chips = _relay_peers()
        for j, chip in enumerate(chips):
            _relay_copy(zone, send, recv, j, (*chip, me[2]), sib).start()
    return _relay_start_call(zones, 3, issue, name, after)


def _relay_pass_wait(flights, after, name):
    def settle(zone, send, recv):
        me, sib, chips = _relay_peers()
        for j, chip in enumerate(chips):
            _relay_copy(zone, send, recv, j, (*chip, sib[2]), me).wait_recv()
            _relay_copy(zone, send, recv, j, (*chip, me[2]), sib).wait_send()
    return _relay_wait_call(flights, settle, after, name)


def _exchange_copy(kind, bufs, send_sems, recv_sems, me, k, arriving):
    peer = _flip(me, k)
    my_id, peer_id = _dev_id(me), _dev_id(peer)
    if kind == "gather":
        slot = bufs[0].at[peer_id if arriving else my_id]
        src, dst = slot, slot
    else:
        src = bufs[0].at[my_id if arriving else peer_id]
        dst = bufs[1].at[peer_id if arriving else my_id]
    return pltpu.make_async_remote_copy(
        src_ref=src, dst_ref=dst, send_sem=send_sems.at[k - 1], recv_sem=recv_sems.at[k - 1],
        device_id=peer, device_id_type=MESH)


def _exchange_start(arrays, kind, name, after=None):
    n = len(arrays)
    n_after = 0 if after is None else 1
    if kind == "gather":
        bufs = [[a] for a in arrays]
    else:
        bufs = [[a, lax.empty(a.shape, a.dtype)] for a in arrays]
    nb = len(bufs[0])
    flat = [b for group in bufs for b in group]

    def body(*refs):
        outs_at = nb * n + n_after
        send = refs[outs_at:outs_at + n]
        recv = refs[outs_at + n:outs_at + 2 * n]
        token = refs[outs_at + 2 * n + nb * n]
        me = _my_pos()
        for a in range(n):
            for k in range(1, N_DEV):
                _exchange_copy(kind, refs[nb * a:nb * (a + 1)], send[a], recv[a], me, k, False).start()
        token[...] = jnp.zeros(token.shape, token.dtype)

    sem = pltpu.SemaphoreType.DMA((N_DEV - 1,))
    outs = pl.pallas_call(
        body,
        name=name,
        out_shape=([sem] * (2 * n) + [pltpu.HBM(b.shape, b.dtype) for b in flat]
                   + [jax.ShapeDtypeStruct((8, 128), F32)]),
        in_specs=[_HBM] * (nb * n) + [pl.BlockSpec(memory_space=pl.ANY)] * n_after,
        out_specs=[_SEM] * (2 * n) + [_HBM] * (nb * n) + [pl.BlockSpec(memory_space=pltpu.VMEM)],
        input_output_aliases={i: 2 * n + i for i in range(nb * n)},
        compiler_params=pltpu.CompilerParams(has_side_effects=_EFFECT),
    )(*[pltpu.with_memory_space_constraint(b, pltpu.HBM) for b in flat],
      *([after] if n_after else []))
    flights = [(outs[a], outs[n + a], list(outs[2 * n + nb * a:2 * n + nb * (a + 1)]))
               for a in range(n)]
    return flights, outs[2 * n + nb * n]


def _exchange_wait(flights, kind, after, name):
    n = len(flights)
    nb = len(flights[0][2])
    flat = [b for f in flights for b in f[2]]

    def body(*refs):
        send = refs[nb * n:nb * n + n]
        recv = refs[nb * n + n:nb * n + 2 * n]
        me = _my_pos()
        for a in range(n):
            for k in range(1, N_DEV):
                bufs = refs[nb * a:nb * (a + 1)]
                _exchange_copy(kind, bufs, send[a], recv[a], me, k, False).wait_send()
                _exchange_copy(kind, bufs, send[a], recv[a], me, k, True).wait_recv()

    outs = pl.pallas_call(
        body,
        name=name,
        out_shape=[pltpu.HBM(b.shape, b.dtype) for b in flat],
        in_specs=[_HBM] * (nb * n) + [_SEM] * (2 * n) + [pl.BlockSpec(memory_space=pl.ANY)],
        out_specs=[_HBM] * (nb * n),
        input_output_aliases={i: i for i in range(nb * n)},
        compiler_params=pltpu.CompilerParams(has_side_effects=_EFFECT),
    )(*flat, *[f[0] for f in flights], *[f[1] for f in flights], after)
    return [list(outs[nb * a:nb * (a + 1)]) for a in range(n)]


def _t5_buckets_block():
    qi = np.arange(BLK)[:, None]
    ki = np.arange(2 * BLK)[None, :]
    n = np.maximum(qi + BLK - ki, 0)
    max_exact = NUM_BUCKETS // 2
    large = max_exact + (np.log(np.maximum(n, 1) / max_exact)
                         / np.log(MAX_DISTANCE / max_exact)
                         * (NUM_BUCKETS - max_exact)).astype(np.int32)
    large = np.minimum(large, NUM_BUCKETS - 1)
    return np.where(n < max_exact, n, large).astype(np.int32)


def _band_mask():
    qi = np.arange(BLK)[:, None]
    ki = np.arange(2 * BLK)[None, :]
    dist = qi + BLK - ki
    return (dist >= 0) & (dist < BLK)


def _attn_scores(q_ref, kp_ref, kc_ref, hkv):
    c0 = hkv * HEAD_DIM
    kk = jnp.concatenate([kp_ref[:, c0:c0 + HEAD_DIM], kc_ref[:, c0:c0 + HEAD_DIM]],
                         axis=0).astype(BF16)
    qg = jnp.concatenate(
        [q_ref[:, (hkv * GROUP + g) * HEAD_DIM:(hkv * GROUP + g + 1) * HEAD_DIM]
         for g in range(GROUP)], axis=0).astype(BF16)
    s = lax.dot_general(qg, kk, (((1,), (1,)), ((), ())), preferred_element_type=F32)
    return qg, kk, s


def _attn_softmax(s, bias_ref, sink_ref, hkv):
    r0, r1 = hkv * GROUP * BLK, (hkv + 1) * GROUP * BLK
    s = s * (HEAD_DIM ** -0.5) + bias_ref[r0:r1, :]
    sink = sink_ref[r0:r1, :]
    m = jnp.maximum(jnp.max(s, axis=-1, keepdims=True), sink)
    p = jnp.exp(s - m)
    e_sink = jnp.exp(sink - m)
    inv = 1.0 / (jnp.sum(p, axis=-1, keepdims=True) + e_sink)
    return p * inv, e_sink * inv


def _kv_rows(p_ref, c_ref, hkv):
    c0 = hkv * HEAD_DIM
    return jnp.concatenate([p_ref[:, c0:c0 + HEAD_DIM], c_ref[:, c0:c0 + HEAD_DIM]],
                           axis=0).astype(BF16)


ATT_Q_FWD = 4
ATT_Q_BWD = 2


def _attn_in_specs(bias2, nq):
    prev = lambda n: jnp.maximum(nq * n - 1, 0)
    kcol = ATTN_W // KV_W
    return [
        pl.BlockSpec((nq * BLK, ATTN_W), lambda n: (n, 0)),
        pl.BlockSpec((BLK, KV_W), lambda n: (prev(n), kcol)),
        pl.BlockSpec((nq * BLK, KV_W), lambda n: (n, kcol)),
        pl.BlockSpec((BLK, KV_W), lambda n: (prev(n), kcol + 1)),
        pl.BlockSpec((nq * BLK, KV_W), lambda n: (n, kcol + 1)),
        pl.BlockSpec(bias2.shape, lambda n: (0, 0, 0)),
    ]


def _attn_views(t, q_ref, kp_ref, kc_ref, vp_ref, vc_ref, bias_ref):
    rows = pl.ds(t * BLK, BLK)
    before = pl.ds((t - 1) * BLK, BLK)
    table = jnp.minimum(pl.program_id(0), 1) if t == 0 else 1
    return (q_ref.at[rows, :],
            kp_ref if t == 0 else kc_ref.at[before, :], kc_ref.at[rows, :],
            vp_ref if t == 0 else vc_ref.at[before, :], vc_ref.at[rows, :],
            bias_ref.at[table])


def _attention_fwd(proj, bias2, sinkcol, n_rows):
    nq = min(ATT_Q_FWD, n_rows // BLK)
    steps = n_rows // (nq * BLK)

    def body(q_ref, kp_ref, kc_ref, vp_ref, vc_ref, bias_ref, sink_ref, o_ref):
        views = [_attn_views(t, q_ref, kp_ref, kc_ref, vp_ref, vc_ref, bias_ref) for t in range(nq)]
        work = [(t, hkv) for t in range(nq) for hkv in range(N_KV_HEADS)]
        scores = {w: _attn_scores(views[w[0]][0], views[w[0]][1], views[w[0]][2], w[1])[2] for w in work}
        probs = {w: _attn_softmax(scores[w], views[w[0]][5], sink_ref, w[1])[0] for w in work}
        outs = {w: jnp.dot(probs[w].astype(BF16), _kv_rows(views[w[0]][3], views[w[0]][4], w[1]),
                           preferred_element_type=F32) for w in work}
        for t, hkv in work:
            for g in range(GROUP):
                h = hkv * GROUP + g
                o_ref[t * BLK:(t + 1) * BLK, h * HEAD_DIM:(h + 1) * HEAD_DIM] = (
                    outs[t, hkv][g * BLK:(g + 1) * BLK, :].astype(o_ref.dtype))

    return pl.pallas_call(
        body,
        grid=(steps,),
        in_specs=_attn_in_specs(bias2, nq) + [pl.BlockSpec(sinkcol.shape, lambda n: (0, 0))],
        out_specs=pl.BlockSpec((nq * BLK, ATTN_W), lambda n: (n, 0)),
        out_shape=jax.ShapeDtypeStruct((n_rows, ATTN_W), BF16),
        compiler_params=_cparams(("parallel",)),
        name="attn_fwd",
    )(proj, proj, proj, proj, proj, bias2, sinkcol)


def _attention_bwd(proj, attn, dattn, bias2, sinkcol, n_rows):
    nq = min(ATT_Q_BWD, n_rows // BLK)
    steps = n_rows // (nq * BLK)
    scale = HEAD_DIM ** -0.5
    dn_t = (((0,), (0,)), ((), ()))

    def body(q_ref, kp_ref, kc_ref, vp_ref, vc_ref, bias_ref, o_ref, do_ref, sink_ref,
             dq_ref, dkc_ref, dkp_ref, dvc_ref, dvp_ref, dbias_ref, dsink_ref):
        @pl.when(pl.program_id(0) == 0)
        def _():
            dbias_ref[...] = jnp.zeros(dbias_ref.shape, F32)
            dsink_ref[...] = jnp.zeros(dsink_ref.shape, F32)

        views = [_attn_views(t, q_ref, kp_ref, kc_ref, vp_ref, vc_ref, bias_ref) for t in range(nq)]
        work = [(t, hkv) for t in range(nq) for hkv in range(N_KV_HEADS)]
        qk = {w: _attn_scores(views[w[0]][0], views[w[0]][1], views[w[0]][2], w[1]) for w in work}
        dog, dps, deltas = {}, {}, {}
        for t, hkv in work:
            rows = slice(t * BLK, (t + 1) * BLK)
            hs = [hkv * GROUP + g for g in range(GROUP)]
            d_o = jnp.concatenate([do_ref[rows, h * HEAD_DIM:(h + 1) * HEAD_DIM] for h in hs], axis=0)
            o = jnp.concatenate([o_ref[rows, h * HEAD_DIM:(h + 1) * HEAD_DIM] for h in hs], axis=0)
            deltas[t, hkv] = jnp.sum(d_o.astype(F32) * o.astype(F32), axis=-1, keepdims=True)
            dog[t, hkv] = d_o.astype(BF16)
            dps[t, hkv] = lax.dot_general(dog[t, hkv], _kv_rows(views[t][3], views[t][4], hkv),
                                          (((1,), (1,)), ((), ())), preferred_element_type=F32)
        p16, ds16 = {}, {}
        for t, hkv in work:
            r0, r1 = hkv * GROUP * BLK, (hkv + 1) * GROUP * BLK
            p, p_sink = _attn_softmax(qk[t, hkv][2], views[t][5], sink_ref, hkv)
            ds = p * (dps[t, hkv] - deltas[t, hkv])
            dbias_ref[r0:r1, :] += ds
            dsink_ref[r0:r1, :] += -(p_sink * deltas[t, hkv])
            p16[t, hkv] = p.astype(BF16)
            ds16[t, hkv] = ds.astype(BF16)
        for t, hkv in work:
            rows = slice(t * BLK, (t + 1) * BLK)
            c0 = hkv * HEAD_DIM
            qg, kk, _ = qk[t, hkv]
            dqg = jnp.dot(ds16[t, hkv], kk, preferred_element_type=F32) * scale
            dkk = lax.dot_general(ds16[t, hkv], qg, dn_t, preferred_element_type=F32) * scale
            dvv = lax.dot_general(p16[t, hkv], dog[t, hkv], dn_t, preferred_element_type=F32)
            for g in range(GROUP):
                h = hkv * GROUP + g
                dq_ref[rows, h * HEAD_DIM:(h + 1) * HEAD_DIM] = (
                    dqg[g * BLK:(g + 1) * BLK, :].astype(dq_ref.dtype))
            dkp_ref[rows, c0:c0 + HEAD_DIM] = dkk[:BLK].astype(dkp_ref.dtype)
            dkc_ref[rows, c0:c0 + HEAD_DIM] = dkk[BLK:].astype(dkc_ref.dtype)
            dvp_ref[rows, c0:c0 + HEAD_DIM] = dvv[:BLK].astype(dvp_ref.dtype)
            dvc_ref[rows, c0:c0 + HEAD_DIM] = dvv[BLK:].astype(dvc_ref.dtype)

    wide = pl.BlockSpec((nq * BLK, ATTN_W), lambda n: (n, 0))
    kv_out = pl.BlockSpec((nq * BLK, KV_W), lambda n: (n, 0))
    kv_shape = jax.ShapeDtypeStruct((n_rows, KV_W), F32)
    acc_shape = bias2.shape[1:]
    return pl.pallas_call(
        body,
        grid=(steps,),
        in_specs=_attn_in_specs(bias2, nq) + [wide, wide, pl.BlockSpec(sinkcol.shape, lambda n: (0, 0))],
        out_specs=[
            wide, kv_out, kv_out, kv_out, kv_out,
            pl.BlockSpec(acc_shape, lambda n: (0, 0)),
            pl.BlockSpec(sinkcol.shape, lambda n: (0, 0)),
        ],
        out_shape=[
            jax.ShapeDtypeStruct((n_rows, ATTN_W), BF16),
            kv_shape, kv_shape, kv_shape, kv_shape,
            jax.ShapeDtypeStruct(acc_shape, F32),
            jax.ShapeDtypeStruct(sinkcol.shape, F32),
        ],
        compiler_params=_cparams(("arbitrary",)),
        name="attn_bwd",
    )(proj, proj, proj, proj, proj, bias2, attn, dattn, sinkcol)


def _bias_tables(rel_bias_t, onehot_t, band_first, band_rest):
    def body(rb_ref, oh_ref, mf_ref, mr_ref, out_ref):
        acc = jnp.zeros((N_Q_HEADS, BLK * 2 * BLK), F32)
        for part in _split3(rb_ref[...]):
            acc = acc + jnp.dot(part, oh_ref[...], preferred_element_type=F32)
        out_ref[0] = jnp.where(mf_ref[...] > 0.0, acc, NEG_INF)
        out_ref[1] = jnp.where(mr_ref[...] > 0.0, acc, NEG_INF)

    return pl.pallas_call(
        body,
        out_shape=jax.ShapeDtypeStruct((2, N_Q_HEADS, BLK * 2 * BLK), F32),
        compiler_params=pltpu.CompilerParams(vmem_limit_bytes=VMEM_LIMIT),
        name="bias_tables",
    )(rel_bias_t, onehot_t, band_first, band_rest)


def _split3(a):
    hi = a.astype(BF16)
    r1 = a - hi.astype(F32)
    mid = r1.astype(BF16)
    lo = (r1 - mid.astype(F32)).astype(BF16)
    return hi, mid, lo


def _bucket_reduce(dbias, dsink, onehot_t):
    def body(db_ref, ds_ref, oh_ref, ob_ref, os_ref):
        acc = jnp.zeros((N_Q_HEADS, 128), F32)
        for part in _split3(db_ref[...]):
            acc = acc + lax.dot_general(part, oh_ref[...], (((1,), (1,)), ((), ())),
                                        preferred_element_type=F32)
        ob_ref[...] = acc
        os_ref[...] = jnp.broadcast_to(jnp.sum(ds_ref[...], axis=-1, keepdims=True),
                                       os_ref.shape)

    return pl.pallas_call(
        body,
        out_shape=[jax.ShapeDtypeStruct((N_Q_HEADS, 128), F32),
                   jax.ShapeDtypeStruct((N_Q_HEADS, 128), F32)],
        compiler_params=pltpu.CompilerParams(vmem_limit_bytes=VMEM_LIMIT),
        name="bias_bucket_reduce",
    )(dbias, dsink, onehot_t)


def _disc(lr, li, ls, btr, bti):
    lam_re = jnp.minimum(lr, -1e-4)
    delta = jnp.exp(ls)
    mag = jnp.exp(lam_re * delta)
    ang = li * delta
    ar, ai = mag * jnp.cos(ang), mag * jnp.sin(ang)
    nr, ni = ar - 1.0, ai
    den = lam_re * lam_re + li * li
    fr = (nr * lam_re + ni * li) / den
    fi = (ni * lam_re - nr * li) / den
    bbr = fr * btr - fi * bti
    bbi = fr * bti + fi * btr
    return ar, ai, bbr, bbi


def _block_mask():
    row = lax.broadcasted_iota(jnp.int32, (SSM_W, SSM_H), 0)
    col = lax.broadcasted_iota(jnp.int32, (SSM_W, SSM_H), 1)
    return (row // SSM_P) == (col // SSM_N)


def _ssm_setup(lr, li, ls, btr, bti, ctr, cti):
    def body(lr_ref, li_ref, ls_ref, btr_ref, bti_ref, ctr_ref, cti_ref, a_ref, b_ref, c_ref):
        ar, ai, bbr, bbi = _disc(lr_ref[...], li_ref[...], ls_ref[...], btr_ref[...], bti_ref[...])
        a_ref[:, :SSM_H] = ar
        a_ref[:, SSM_H:] = ai
        mask = _block_mask()
        blk = lambda t: jnp.where(mask, jnp.tile(t, (SSM_G, 1)), 0.0)
        b_ref[:, :SSM_H] = blk(bbr).astype(BF16)
        b_ref[:, SSM_H:] = blk(bbi).astype(BF16)
        c_ref[:, :SSM_H] = blk(ctr_ref[...]).astype(BF16)
        c_ref[:, SSM_H:] = blk(-cti_ref[...]).astype(BF16)

    return pl.pallas_call(
        body,
        out_shape=[jax.ShapeDtypeStruct((1, 2 * SSM_H), F32),
                   jax.ShapeDtypeStruct((SSM_W, 2 * SSM_H), BF16),
                   jax.ShapeDtypeStruct((SSM_W, 2 * SSM_H), BF16)],
        compiler_params=pltpu.CompilerParams(vmem_limit_bytes=VMEM_LIMIT),
        name="ssm_setup",
    )(lr, li, ls, btr, bti, ctr, cti)


def _ssm_param_bwd(lr, li, ls, btr, bti, dacc, dbcat, dccat, gind):
    def body(lr_ref, li_ref, ls_ref, btr_ref, bti_ref, dacc_ref, db_ref, dc_ref, g_ref,
             dlr_ref, dli_ref, dls_ref, dbtr_ref, dbti_ref, dctr_ref, dcti_ref):
        dar = jnp.sum(dacc_ref[:, :SSM_H], axis=0, keepdims=True)
        dai = jnp.sum(dacc_ref[:, SSM_H:], axis=0, keepdims=True)
        col = lax.broadcasted_iota(jnp.int32, (SSM_P, 2 * SSM_H), 1)
        grp = (col % SSM_H) // SSM_N
        db = jnp.zeros((SSM_P, 2 * SSM_H), F32)
        dc = jnp.zeros((SSM_P, 2 * SSM_H), F32)
        half = SSM_G // 2
        for g in range(SSM_G):
            sel = grp == g
            r0 = (g % half) * SSM_P
            db = db + jnp.where(sel, db_ref[r0:r0 + SSM_P, :], 0.0)
            dc = dc + jnp.where(sel, dc_ref[r0:r0 + SSM_P, :], 0.0)
        dctr_ref[...] = dc[:, :SSM_H]
        dcti_ref[...] = -dc[:, SSM_H:]
        prim = (lr_ref[...], li_ref[...], ls_ref[...], btr_ref[...], bti_ref[...])
        _, vjp = jax.vjp(_disc, *prim)
        dlr, dli, dls, dbtr, dbti = vjp((dar, dai, db[:, :SSM_H], db[:, SSM_H:]))
        dlr_ref[...] = dlr
        dli_ref[...] = dli
        dbtr_ref[...] = dbtr
        dbti_ref[...] = dbti
        acc = jnp.zeros((8, 128), F32)
        for part in _split3(jnp.broadcast_to(dls, (8, SSM_H))):
            acc = acc + jnp.dot(part, g_ref[...], preferred_element_type=F32)
        dls_ref[...] = acc

    vec = jax.ShapeDtypeStruct((1, SSM_H), F32)
    mat = jax.ShapeDtypeStruct((SSM_P, SSM_H), F32)
    return pl.pallas_call(
        body,
        out_shape=[vec, vec, jax.ShapeDtypeStruct((8, 128), F32), mat, mat, mat, mat],
        compiler_params=pltpu.CompilerParams(vmem_limit_bytes=VMEM_LIMIT),
        name="ssm_param_bwd",
    )(lr, li, ls, btr, bti, dacc, dbcat, dccat, gind)


SCAN_TR = 256


def _cmul_add(vr, vi, pr, pi, sr, si):
    return vr + pr * sr - pi * si, vi + pr * si + pi * sr


def _bcast_row(v, row, which):
    return jnp.broadcast_to(v[which:which + 1, :], v.shape)


def _scan_tables(a_ref, tab_ref, reverse):
    H = SSM_H
    ar = jnp.broadcast_to(a_ref[:, :H], (8, H))
    ai = jnp.broadcast_to(a_ref[:, H:], (8, H))
    if reverse:
        ai = -ai
    row = lax.broadcasted_iota(jnp.int32, (8, H), 0)
    pw = [(ar, ai)]
    for _ in range(7):
        cr, ci = pw[-1]
        pw.append((cr * ar - ci * ai, cr * ai + ci * ar))
    pcr = jnp.zeros((8, H), F32)
    pci = jnp.zeros((8, H), F32)
    for e in range(8):
        sel = (row == (7 - e)) if reverse else (row == e)
        pcr = jnp.where(sel, pw[e][0], pcr)
        pci = jnp.where(sel, pw[e][1], pci)
    tab_ref[0, :, :H] = pcr
    tab_ref[0, :, H:] = pci
    for t, k in enumerate((1, 2, 4)):
        keep = (row < 8 - k) if reverse else (row >= k)
        tab_ref[1 + t, :, :H] = jnp.where(keep, pw[k - 1][0], 0.0)
        tab_ref[1 + t, :, H:] = jnp.where(keep, pw[k - 1][1], 0.0)


def _scan_group(vr, vi, cr, ci, tab_ref, reverse):
    H = SSM_H
    for t, k in enumerate((1, 2, 4)):
        sh = 8 - k if reverse else k
        vr, vi = _cmul_add(vr, vi, tab_ref[1 + t, :, :H], tab_ref[1 + t, :, H:],
                           pltpu.roll(vr, sh, 0), pltpu.roll(vi, sh, 0))
    return _cmul_add(vr, vi, tab_ref[0, :, :H], tab_ref[0, :, H:], cr, ci)


def _blockdiag_expand(x, w_ref, out_ref):
    hw, cb = SSM_W // 2, SSM_H // 2
    for j in range(4):
        h = j % 2
        out_ref[:, j * cb:(j + 1) * cb] = jnp.dot(
            x[:, h * hw:(h + 1) * hw], w_ref[h * hw:(h + 1) * hw, j * cb:(j + 1) * cb],
            preferred_element_type=F32)


def _blockdiag_contract(x_ref, w_ref):
    hw, cb = SSM_W // 2, SSM_H // 2
    nt = (((1,), (1,)), ((), ()))
    halves = []
    for h in range(2):
        acc = None
        for j in (h, 2 + h):
            part = lax.dot_general(x_ref[:, j * cb:(j + 1) * cb],
                                   w_ref[h * hw:(h + 1) * hw, j * cb:(j + 1) * cb], nt,
                                   preferred_element_type=F32)
            acc = part if acc is None else acc + part
        halves.append(acc)
    return jnp.concatenate(halves, axis=1)


def _scan_fwd(proj, u_blk, bcat, ccat, abar, n_rows):
    H = SSM_H
    nt = n_rows // SCAN_TR

    def body(u_ref, b_ref, c_ref, a_ref, xs_ref, xp_ref, yc_ref, bu_ref, tab_ref, carry_ref):
        @pl.when(pl.program_id(0) == 0)
        def _():
            _scan_tables(a_ref, tab_ref, False)
            carry_ref[...] = jnp.zeros(carry_ref.shape, F32)

        _blockdiag_expand(u_ref[...].astype(BF16), b_ref, bu_ref)
        row = lax.broadcasted_iota(jnp.int32, (8, H), 0)

        def group(j, carry):
            cr, ci = carry
            r0 = pl.multiple_of(j * 16, 16)
            xr, xi = [], []
            for half in range(2):
                rr = pl.multiple_of(r0 + 8 * half, 8)
                vr, vi = _scan_group(bu_ref[pl.ds(rr, 8), :H], bu_ref[pl.ds(rr, 8), H:],
                                     cr, ci, tab_ref, False)
                xp_ref[pl.ds(rr, 8), :H] = jnp.where(row == 0, cr, pltpu.roll(vr, 1, 0))
                xp_ref[pl.ds(rr, 8), H:] = jnp.where(row == 0, ci, pltpu.roll(vi, 1, 0))
                cr, ci = _bcast_row(vr, row, 7), _bcast_row(vi, row, 7)
                xr.append(vr)
                xi.append(vi)
            xs_ref[pl.ds(r0, 16), :H] = jnp.concatenate(xr, axis=0).astype(BF16)
            xs_ref[pl.ds(r0, 16), H:] = jnp.concatenate(xi, axis=0).astype(BF16)
            return cr, ci

        cr, ci = lax.fori_loop(0, SCAN_TR // 16, group,
                               (carry_ref[:, :H], carry_ref[:, H:]))
        carry_ref[:, :H] = cr
        carry_ref[:, H:] = ci
        yc_ref[...] = _blockdiag_contract(xs_ref, c_ref)

    tile = lambda w: pl.BlockSpec((SCAN_TR, w), lambda i: (i, 0))
    whole = lambda a: pl.BlockSpec(a.shape, lambda i: (0, 0))
    return pl.pallas_call(
        body,
        grid=(nt,),
        in_specs=[pl.BlockSpec((SCAN_TR, SSM_W), lambda i: (i, u_blk)),
                  whole(bcat), whole(ccat), whole(abar)],
        out_specs=[tile(2 * H), tile(2 * H), tile(SSM_W)],
        out_shape=[jax.ShapeDtypeStruct((n_rows, 2 * H), BF16),
                   jax.ShapeDtypeStruct((n_rows, 2 * H), F32),
                   jax.ShapeDtypeStruct((n_rows, SSM_W), F32)],
        scratch_shapes=[pltpu.VMEM((SCAN_TR, 2 * H), F32), pltpu.VMEM((4, 8, 2 * H), F32),
                        pltpu.VMEM((8, 2 * H), F32)],
        compiler_params=_cparams(("arbitrary",)),
        name="ssm_scan_fwd",
    )(proj, bcat, ccat, abar)


def _scan_bwd(dy, xprev, bcat, ccat, abar, n_rows):
    H = SSM_H
    nt = n_rows // SCAN_TR

    def body(dy_ref, xp_ref, b_ref, c_ref, a_ref, h_ref, da_ref, du_ref, g_ref, tab_ref, carry_ref):
        @pl.when(pl.program_id(0) == 0)
        def _():
            _scan_tables(a_ref, tab_ref, True)
            carry_ref[...] = jnp.zeros(carry_ref.shape, F32)
            da_ref[...] = jnp.zeros(da_ref.shape, F32)

        _blockdiag_expand(dy_ref[...], c_ref, g_ref)
        row = lax.broadcasted_iota(jnp.int32, (8, H), 0)
        n16 = SCAN_TR // 16

        def group(jj, carry):
            cr, ci = carry
            r0 = pl.multiple_of((n16 - 1 - jj) * 16, 16)
            hr, hi = [None, None], [None, None]
            for half in (1, 0):
                rr = pl.multiple_of(r0 + 8 * half, 8)
                vr, vi = _scan_group(g_ref[pl.ds(rr, 8), :H], g_ref[pl.ds(rr, 8), H:],
                                     cr, ci, tab_ref, True)
                pr, pi = xp_ref[pl.ds(rr, 8), :H], xp_ref[pl.ds(rr, 8), H:]
                da_ref[:, :H] += vr * pr + vi * pi
                da_ref[:, H:] += vi * pr - vr * pi
                cr, ci = _bcast_row(vr, row, 0), _bcast_row(vi, row, 0)
                hr[half], hi[half] = vr, vi
            h_ref[pl.ds(r0, 16), :H] = jnp.concatenate(hr, axis=0).astype(BF16)
            h_ref[pl.ds(r0, 16), H:] = jnp.concatenate(hi, axis=0).astype(BF16)
            return cr, ci

        cr, ci = lax.fori_loop(0, n16, group, (carry_ref[:, :H], carry_ref[:, H:]))
        carry_ref[:, :H] = cr
        carry_ref[:, H:] = ci
        du_ref[...] = _blockdiag_contract(h_ref, b_ref)

    rev = lambda i: (nt - 1 - i, 0)
    whole = lambda a: pl.BlockSpec(a.shape, lambda i: (0, 0))
    return pl.pallas_call(
        body,
        grid=(nt,),
        in_specs=[pl.BlockSpec((SCAN_TR, SSM_W), rev),
                  pl.BlockSpec((SCAN_TR, 2 * H), rev),
                  whole(bcat), whole(ccat), whole(abar)],
        out_specs=[pl.BlockSpec((SCAN_TR, 2 * H), rev),
                   pl.BlockSpec((8, 2 * H), lambda i: (0, 0)),
                   pl.BlockSpec((SCAN_TR, SSM_W), rev)],
        out_shape=[jax.ShapeDtypeStruct((n_rows, 2 * H), BF16),
                   jax.ShapeDtypeStruct((8, 2 * H), F32),
                   jax.ShapeDtypeStruct((n_rows, SSM_W), F32)],
        scratch_shapes=[pltpu.VMEM((SCAN_TR, 2 * H), F32), pltpu.VMEM((4, 8, 2 * H), F32),
                        pltpu.VMEM((8, 2 * H), F32)],
        compiler_params=_cparams(("arbitrary",)),
        name="ssm_scan_bwd",
    )(dy, xprev, bcat, ccat, abar)


def _adamw(parts, w, m, v, *, tr, ch, name, prefetch=None):
    n_rows, cols = w.shape
    n_parts = len(parts)
    c1 = 1.0 - ADAM_B1 ** ADAM_STEP
    c2 = 1.0 - ADAM_B2 ** ADAM_STEP

    def fn(rv, vv, i, nt):
        g = rv[0].astype(F32)
        for p in rv[1:n_parts]:
            g = g + p.astype(F32)
        wv, mv, vval = rv[n_parts:]
        nm = ADAM_B1 * mv + (1.0 - ADAM_B1) * g
        nv = ADAM_B2 * vval + (1.0 - ADAM_B2) * (g * g)
        delta = -ADAM_LR * ((nm / c1) / (jnp.sqrt(nv / c2) + ADAM_EPS) + ADAM_WD * wv)
        return [g, delta, nm, nv], []

    rows = [_row(arr, lead=lead) for (arr, lead) in parts] + [_row(w), _row(m), _row(v)]
    return _rowwise(fn, rows, [], [(cols, F32)] * 4, [], n_rows=n_rows, tr=tr, ch=ch, name=name,
                    prefetch=prefetch)


_PACK = [
    ("b_ada", 6), ("norm1_g", 1), ("b_in", 3), ("norm2_g", 1), ("final_g", 1),
    ("lambda_re", 1), ("lambda_im", 1), ("log_step", 1), ("attn_sinks", 1),
    ("rel_bias", 1), ("b_glu", 1), ("ssm_d", 1), ("loss", 1),
    ("ssm_b_re", 16), ("ssm_b_im", 16), ("ssm_c_re", 16), ("ssm_c_im", 16),
]
_PACK_OFF = {}
_off = 0
for _n, _r in _PACK:
    _PACK_OFF[_n] = (_off, _r)
    _off += _r
PACK_ROWS = -(-_off // 8) * 8


def _to_rows(a, rows):
    flat = a.reshape(-1).astype(F32)
    pad = rows * PACK_W - flat.shape[0]
    if pad:
        flat = jnp.pad(flat, (0, pad))
    return flat.reshape(rows, PACK_W)


def _b_to_rows(b):
    return jnp.transpose(b, (2, 0, 1)).reshape(SSM_P, SSM_H)


def _rows_to_b(r):
    return jnp.transpose(r.reshape(SSM_P, SSM_G, SSM_N), (1, 2, 0))


def _c_to_rows(cm):
    return jnp.transpose(cm, (1, 0, 2)).reshape(SSM_P, SSM_H)


def _rows_to_c(r):
    return jnp.transpose(r.reshape(SSM_P, SSM_G, SSM_N), (1, 0, 2))


def _pack(vals):
    out = jnp.zeros((PACK_ROWS, PACK_W), F32)
    for n, r in _PACK:
        if n in vals:
            pieces = vals[n] if isinstance(vals[n], list) else [vals[n]]
            rows_each = r // len(pieces)
            for i, piece in enumerate(pieces):
                out = lax.dynamic_update_slice(out, _to_rows(piece, rows_each),
                                               (_PACK_OFF[n][0] + i * rows_each, 0))
    return out


def _unpack(packed, name, shape):
    o, r = _PACK_OFF[name]
    n = int(np.prod(shape))
    return packed[o:o + r].reshape(-1)[:n].reshape(shape)


def _small_params_packed(p):
    return {
        "b_ada": p["b_ada"], "norm1_g": p["norm1_g"], "b_in": p["b_in"],
        "norm2_g": p["norm2_g"], "final_g": p["final_g"],
        "lambda_re": p["lambda_re"], "lambda_im": p["lambda_im"],
        "log_step": p["log_step"], "attn_sinks": p["attn_sinks"],
        "rel_bias": p["rel_bias"], "b_glu": p["b_glu"], "ssm_d": p["ssm_d"],
        "ssm_b_re": _b_to_rows(p["ssm_b_re"][0]), "ssm_b_im": _b_to_rows(p["ssm_b_im"][0]),
        "ssm_c_re": _c_to_rows(p["ssm_c_re"][0]), "ssm_c_im": _c_to_rows(p["ssm_c_im"][0]),
    }


_SMALL_SHAPES = {
    "b_ada": (1, N_MOD * D), "norm1_g": (1, D), "b_in": (1, IN_W), "norm2_g": (1, D),
    "final_g": (D,), "lambda_re": (1, SSM_G, SSM_N), "lambda_im": (1, SSM_G, SSM_N),
    "log_step": (1, SSM_G), "attn_sinks": (1, N_Q_HEADS), "rel_bias": (NUM_BUCKETS, N_Q_HEADS),
    "b_glu": (1, SSM_W), "ssm_d": (1, SSM_W),
}


def _unpack_small(packed, name):
    if name in ("ssm_b_re", "ssm_b_im"):
        o, r = _PACK_OFF[name]
        return _rows_to_b(packed[o:o + r])[None]
    if name in ("ssm_c_re", "ssm_c_im"):
        o, r = _PACK_OFF[name]
        return _rows_to_c(packed[o:o + r])[None]
    return _unpack(packed, name, _SMALL_SHAPES[name])


WEIGHT_ORDER = ['w_ada', 'b_ada', 'norm1_g', 'w_in', 'b_in', 'attn_sinks', 'rel_bias', 'lambda_re',
                'lambda_im', 'log_step', 'ssm_b_re', 'ssm_b_im', 'ssm_c_re', 'ssm_c_im', 'ssm_d',
                'w_glu', 'b_glu', 'w_attn_proj', 'w_ssm_proj', 'w_out', 'norm2_g', 'w_ff1', 'w_ff2',
                'final_g']
BIG = ['w_in', 'w_glu', 'w_attn_proj', 'w_ssm_proj', 'w_out', 'w_ff1', 'w_ff2']


ADAMW_TILE_ELEMS = 1 << 18


def _to_col_blocks(w):
    k, n = w.shape
    return jnp.transpose(w.reshape(k, N_DEV, n // N_DEV), (1, 0, 2))


def _adamw_rows(rows, cols):
    tr = rows
    while tr * cols > ADAMW_TILE_ELEMS and tr % 32 == 0:
        tr //= 2
    return tr


def _cast_to_slot(w, me1, name, dep=None):
    rows, cols = w.shape
    tr = min(rows, 256)
    n_dep = 0 if dep is None else 1

    def body(me_ref, w_ref, *rest):
        rest[-1][...] = w_ref[...].astype(BF16)

    return pl.pallas_call(
        body,
        grid_spec=pltpu.PrefetchScalarGridSpec(
            num_scalar_prefetch=1, grid=(rows // tr,),
            in_specs=[pl.BlockSpec((tr, cols), lambda i, me_ref: (i, 0))]
            + [pl.BlockSpec(memory_space=pl.ANY)] * n_dep,
            out_specs=pl.BlockSpec((None, tr, cols), lambda i, me_ref: (me_ref[0], i, 0))),
        out_shape=jax.ShapeDtypeStruct((N_DEV, rows, cols), BF16),
        compiler_params=_cparams(("arbitrary",)),
        name=name,
    )(me1, w, *([dep] if n_dep else []))


def kernel(x, c, w_ada, b_ada, norm1_g, w_in, b_in, attn_sinks, rel_bias, lambda_re, lambda_im, log_step, ssm_b_re, ssm_b_im, ssm_c_re, ssm_c_im, ssm_d, w_glu, b_glu, w_attn_proj, w_ssm_proj, w_out, norm2_g, w_ff1, w_ff2, final_g, loss_target, m_w_ada, m_b_ada, m_norm1_g, m_w_in, m_b_in, m_attn_sinks, m_rel_bias, m_lambda_re, m_lambda_im, m_log_step, m_ssm_b_re, m_ssm_b_im, m_ssm_c_re, m_ssm_c_im, m_ssm_d, m_w_glu, m_b_glu, m_w_attn_proj, m_w_ssm_proj, m_w_out, m_norm2_g, m_w_ff1, m_w_ff2, m_final_g, v_w_ada, v_b_ada, v_norm1_g, v_w_in, v_b_in, v_attn_sinks, v_rel_bias, v_lambda_re, v_lambda_im, v_log_step, v_ssm_b_re, v_ssm_b_im, v_ssm_c_re, v_ssm_c_im, v_ssm_d, v_w_glu, v_b_glu, v_w_attn_proj, v_w_ssm_proj, v_w_out, v_norm2_g, v_w_ff1, v_w_ff2, v_final_g):
    loc = dict(locals())
    W = {n: loc[n] for n in WEIGHT_ORDER}
    Mo = {n: loc["m_" + n] for n in WEIGHT_ORDER}
    Vo = {n: loc["v_" + n] for n in WEIGHT_ORDER}
    S = x.shape[1]
    TM = min(512, S)
    TS = min(1024, S)
    TR = min(256, S)
    TW = min(1024, S)
    me = 4 * lax.axis_index("x") + 2 * lax.axis_index("y") + lax.axis_index("c")
    x2d = x.reshape(S, D)
    tgt = loss_target.reshape(S, D)

    c_all = _small_allgather(c, "allgather_c").reshape(N_DEV, D)
    cs = _rowwise(lambda rv, vv, i, nt: ([rv[0] * _sigmoid(rv[0])], []), [_row(c_all)], [],
                  [(D, F32)], [], n_rows=N_DEV, tr=8, ch=8, name="silu_c")[0]
    n_ada = N_MOD * D // N_DEV
    b_ada_cols = lax.dynamic_slice(b_ada, (0, me * n_ada), (1, n_ada))
    mod_piece = _matmul(cs, w_ada[0], mode="nn", dims=(N_DEV, n_ada, D), tiles=(N_DEV, 512, D),
                        out_dtypes=[F32], name="ada_fwd", bias=b_ada_cols)
    mod_all = _small_allgather(mod_piece, "allgather_mod")
    mod_b = lax.dynamic_index_in_dim(mod_all, me, axis=1, keepdims=False).reshape(N_MOD, D)
    sh1, sc1, g1, sh2, sc2, g2 = [mod_b[i:i + 1] for i in range(N_MOD)]

    shard = {n: W[n][0] for n in BIG}
    me1 = jnp.reshape(me, (1,)).astype(jnp.int32)
    zone = {"w_in": _cast_to_slot(shard["w_in"], me1, "cast_w_in")}
    (in_flight,), tok_in = _relay_gather_start([zone["w_in"]], "w_in_start", mod_all)
    for n in BIG[1:]:
        zone[n] = _cast_to_slot(shard[n], me1, "cast_" + n, dep=tok_in)
    G = {}

    def f_norm1(rv, vv, i, nt):
        xv, (g, sc, sh) = rv[0], vv
        return [(xv * _rms(xv) * g) * (1.0 + sc) + sh], []

    h = _rowwise(f_norm1, [_row(x2d)], [norm1_g, sc1, sh1], [(D, BF16)], [],
                 n_rows=S, tr=TR, ch=32, name="norm1_fwd", dep=zone["w_ff2"])[0]
    (zone_in,) = _relay_gather_arrive([in_flight], h, "w_in_arrive")
    (in_pass,), tok_p = _relay_pass_start([zone_in], "w_in_pass_start")
    mixer = ["w_attn_proj", "w_glu", "w_ssm_proj", "w_out"]
    later_flights, tok_w = _relay_gather_start(
        [zone[n] for n in mixer] + [zone["w_ff1"], zone["w_ff2"]], "weights_start", tok_p)
    mixer_flights, ff_flights = later_flights[:len(mixer)], later_flights[len(mixer):]
    (G["w_in"],) = _relay_pass_wait([in_pass], tok_w, "w_in_pass_wait")
    proj = _matmul(h, G["w_in"], mode="nn", dims=(S, IN_W, D), tiles=(TW, 768, D),
                   out_dtypes=[BF16], name="in_proj", b3=True, bias=b_in, dep=tok_w)

    buckets = _t5_buckets_block()
    band = _band_mask()
    onehot_t = jnp.asarray(
        (np.arange(128)[:, None] == buckets.reshape(-1)[None, :]).astype(np.float32), BF16)
    band_first = band & (np.arange(2 * BLK)[None, :] >= BLK)
    rel_bias_t = jnp.pad(jnp.transpose(rel_bias), ((0, 0), (0, 128 - NUM_BUCKETS)))
    bias2 = _bias_tables(rel_bias_t, onehot_t,
                         jnp.asarray(band_first.reshape(1, -1).astype(np.float32)),
                         jnp.asarray(band.reshape(1, -1).astype(np.float32))
                         ).reshape(2, N_Q_HEADS * BLK, 2 * BLK)
    sinkcol = jnp.repeat(attn_sinks.reshape(N_Q_HEADS), BLK).reshape(N_Q_HEADS * BLK, 1)
    attn = _attention_fwd(proj, bias2, sinkcol, S)
    mixer_zones = _relay_gather_arrive(mixer_flights, attn, "mixer_weights_arrive")
    mixer_pass, _ = _relay_pass_start(mixer_zones, "mixer_weights_pass_start")

    lam_re = lambda_re.reshape(1, SSM_H)
    lam_im = lambda_im.reshape(1, SSM_H)
    ls_x = jnp.repeat(log_step.reshape(SSM_G), SSM_N).reshape(1, SSM_H)
    btr, bti = _b_to_rows(ssm_b_re[0]), _b_to_rows(ssm_b_im[0])
    ctr, cti = _c_to_rows(ssm_c_re[0]), _c_to_rows(ssm_c_im[0])
    abar, bcat, ccat = _ssm_setup(lam_re, lam_im, ls_x, btr, bti, ctr, cti)
    u_blk = (ATTN_W + 2 * KV_W) // SSM_W
    xs, xprev, yc = _scan_fwd(proj, u_blk, bcat, ccat, abar, S)

    def f_ssm_out(rv, vv, i, nt):
        y = rv[0] + vv[0] * rv[1]
        return [y, _gelu(y)], []

    y_ssm_pre, z = _rowwise(f_ssm_out, [_row(yc), _row(proj, u_blk, SSM_W)], [ssm_d],
                            [(SSM_W, F32), (SSM_W, BF16)], [], n_rows=S, tr=TM, ch=32, name="ssm_out")
    G.update(zip(mixer, _relay_pass_wait(mixer_pass, z, "mixer_weights_pass_wait")))
    w_glu_f = G["w_glu"].reshape(SSM_W, SSM_W)
    w_out_f = G["w_out"].reshape(D, D)
    w_ap_f = jnp.transpose(G["w_attn_proj"], (1, 0, 2)).reshape(ATTN_W, D)
    w_sp_f = jnp.transpose(G["w_ssm_proj"], (1, 0, 2)).reshape(SSM_W, D)
    y_attn = _matmul(attn, w_ap_f, mode="nn", dims=(S, D, ATTN_W), tiles=(TW, 1024, ATTN_W),
                     out_dtypes=[BF16], name="attn_proj")
    zg = _matmul(z, w_glu_f, mode="nn", dims=(S, SSM_W, SSM_W), tiles=(TM, SSM_W, SSM_W),
                 out_dtypes=[F32], name="glu_proj", bias=b_glu)
    z2 = _rowwise(lambda rv, vv, i, nt: ([rv[0].astype(F32) * _sigmoid(rv[1])], []),
                  [_row(z), _row(zg)], [], [(SSM_W, BF16)], [], n_rows=S, tr=TM, ch=32, name="glu_gate")[0]
    y_ssm = _matmul(z2, w_sp_f, mode="nn", dims=(S, D, SSM_W), tiles=(TW, 1024, SSM_W),
                    out_dtypes=[BF16], name="ssm_proj")

    ga_row = _row(proj, 1, D)
    gs_row = _row(proj, 2, D)

    def f_merge(rv, vv, i, nt):
        ga, gs, ya, ys = rv
        return [_sigmoid(ga) * ya + _sigmoid(gs) * ys], []

    merged = _rowwise(f_merge, [ga_row, gs_row, _row(y_attn), _row(y_ssm)], [], [(D, BF16)], [],
                      n_rows=S, tr=TR, ch=32, name="merge")[0]
    mo = _matmul(merged, w_out_f, mode="nn", dims=(S, D, D), tiles=(TW, 1024, D),
                 out_dtypes=[BF16], name="out_proj")

    ff_zones = _relay_gather_arrive(ff_flights, mo, "ff_weights_arrive")
    ff_pass, tok_fp = _relay_pass_start(ff_zones, "ff_weights_pass_start")

    def f_norm2(rv, vv, i, nt):
        xv, mv = rv
        g1v, g, sc, sh = vv
        x1v = xv + g1v * mv
        return [x1v, (x1v * _rms(x1v) * g) * (1.0 + sc) + sh], []

    x1, h2 = _rowwise(f_norm2, [_row(x2d), _row(mo)], [g1, norm2_g, sc2, sh2],
                      [(D, F32), (D, BF16)], [], n_rows=S, tr=TR, ch=32, name="norm2_fwd", dep=tok_fp)

    def relu_sq(acc):
        r = jnp.maximum(acc, 0.0)
        return r * r, r

    (G["w_ff1"],) = _relay_pass_wait(ff_pass[:1], h2, "w_ff1_pass_wait")
    act, relu = _matmul(h2, G["w_ff1"], mode="nn", dims=(S, D_FF, D), tiles=(TW, 1024, D),
                        out_dtypes=[BF16, BF16], name="ff1", b3=True, epilogue=relu_sq)
    w_ff2_f = _relay_pass_wait(ff_pass[1:], act, "w_ff2_pass_wait")[0].reshape(D_FF, D)
    ff = _matmul(act, w_ff2_f, mode="nn", dims=(S, D, D_FF), tiles=(TW, 1024, 2048),
                 out_dtypes=[BF16], name="ff2")

    def f_loss(rv, vv, i, nt):
        x1v, ffv, tv = rv
        g2v, gf = vv
        x2v = x1v + g2v * ffv
        r = _rms(x2v)
        xh = x2v * r
        diff = xh * gf - tv
        dy = diff * (1.0 / D)
        dxh = dy * gf
        dx2 = r * (dxh - xh * jnp.mean(dxh * xh, axis=-1, keepdims=True))
        return [dx2, dx2 * g2v], [_colsum(0.5 * diff * diff * (1.0 / D)), _colsum(dy * xh),
                                  _colsum(dx2 * ffv)]

    dx2, dff, loss_cols, d_final_g, dg2 = _rowwise(
        f_loss, [_row(x1), _row(ff), _row(tgt)], [g2, final_g.reshape(1, D)],
        [(D, F32), (D, BF16)], [(1, D)] * 3, n_rows=S, tr=TR, ch=32, name="loss_bwd")

    df1 = _matmul(dff, w_ff2_f, mode="nt", dims=(S, D_FF, D), tiles=(TW, 1024, D),
                  out_dtypes=[BF16], name="ff2_dgrad", extras=(relu,),
                  epilogue=lambda acc, r: (acc * (2.0 * r.astype(F32)),))
    gw_ff2 = _matmul(act, dff, mode="tn", dims=(D_FF, D, S), tiles=(2048, 1024, TS),
                     out_dtypes=[BF16], name="ff2_wgrad").reshape(N_DEV, D_FF // N_DEV, D)
    g_flight = {}
    (g_flight["w_ff2"],), tok = _exchange_start([gw_ff2], "scatter", "grads_start_ff2")
    dh2 = _matmul(df1, G["w_ff1"], mode="nt", dims=(S, D, D_FF), tiles=(TW, D, 1024),
                  out_dtypes=[BF16], name="ff1_dgrad", b3=True, dep=tok)
    gw_ff1 = _matmul(h2, df1, mode="tn", dims=(D, D_FF, S), tiles=(2048, 1024, TS),
                     out_dtypes=[BF16], name="ff1_wgrad", out3=True)
    (g_flight["w_ff1"],), tok = _exchange_start([gw_ff1], "scatter", "grads_start_ff1")

    def f_norm2_bwd(rv, vv, i, nt):
        x1v, dh, dx2v, mv = rv
        g, sc, g1v = vv
        r = _rms(x1v)
        xh = x1v * r
        t = xh * g
        dt = dh * (1.0 + sc)
        dxh = dt * g
        dx1 = dx2v + r * (dxh - xh * jnp.mean(dxh * xh, axis=-1, keepdims=True))
        return [dx1, dx1 * g1v], [_colsum(dh), _colsum(dh * t), _colsum(dt * xh), _colsum(dx1 * mv)]

    dx1, dmo, dsh2, dsc2, d_norm2_g, dg1 = _rowwise(
        f_norm2_bwd, [_row(x1), _row(dh2), _row(dx2), _row(mo)], [norm2_g, sc2, g1],
        [(D, F32), (D, BF16)], [(1, D)] * 4, n_rows=S, tr=TR, ch=16, name="norm2_bwd", dep=tok)

    dmerged = _matmul(dmo, w_out_f, mode="nt", dims=(S, D, D), tiles=(TW, 1024, D),
                      out_dtypes=[BF16], name="out_dgrad")
    gw_out = _matmul(merged, dmo, mode="tn", dims=(D, D, S), tiles=(2048, 1024, TS),
                     out_dtypes=[BF16], name="out_wgrad").reshape(N_DEV, D // N_DEV, D)
    (g_flight["w_out"],), tok = _exchange_start([gw_out], "scatter", "grads_start_out")

    def f_merge_bwd(rv, vv, i, nt):
        dm, ga, gs, ya, ys = rv
        sa, ss = _sigmoid(ga), _sigmoid(gs)
        return [dm * sa, dm * ss, dm * ya * sa * (1.0 - sa), dm * ys * ss * (1.0 - ss)], []

    dy_attn, dy_ssm, dga, dgs = _rowwise(
        f_merge_bwd, [_row(dmerged), ga_row, gs_row, _row(y_attn), _row(y_ssm)], [],
        [(D, BF16)] * 4, [], n_rows=S, tr=TR, ch=16, name="merge_bwd", dep=tok)

    dz2 = _matmul(dy_ssm, w_sp_f, mode="nt", dims=(S, SSM_W, D), tiles=(TW, SSM_W, D),
                  out_dtypes=[F32], name="ssm_proj_dgrad")
    gw_ssm_proj = _to_col_blocks(_matmul(z2, dy_ssm, mode="tn", dims=(SSM_W, D, S), tiles=(SSM_W, 1024, TS),
                                         out_dtypes=[BF16], name="ssm_proj_wgrad"))

    def f_glu_bwd(rv, vv, i, nt):
        dz2v, zv, zgv = rv
        sg = _sigmoid(zgv)
        dzg = dz2v * zv.astype(F32) * sg * (1.0 - sg)
        return [dzg, dz2v * sg], [_colsum(dzg)]

    dzg, dz_a, d_b_glu = _rowwise(f_glu_bwd, [_row(dz2), _row(z), _row(zg)], [],
                                  [(SSM_W, BF16), (SSM_W, F32)], [(1, SSM_W)],
                                  n_rows=S, tr=TM, ch=32, name="glu_bwd")
    dz_b = _matmul(dzg, w_glu_f, mode="nt", dims=(S, SSM_W, SSM_W), tiles=(TM, SSM_W, SSM_W),
                   out_dtypes=[F32], name="glu_dgrad")
    gw_glu = _matmul(z, dzg, mode="tn", dims=(SSM_W, SSM_W, S), tiles=(SSM_W, SSM_W, TS),
                     out_dtypes=[BF16], name="glu_wgrad").reshape(N_DEV, SSM_W // N_DEV, SSM_W)
    (g_flight["w_ssm_proj"], g_flight["w_glu"]), tok = _exchange_start(
        [gw_ssm_proj, gw_glu], "scatter", "grads_start_ssm")

    def f_ssm_out_bwd(rv, vv, i, nt):
        dza, dzb, yv, uv = rv
        dy = (dza + dzb) * _gelu_grad(yv)
        return [dy, dy * vv[0]], [_colsum(dy * uv)]

    dy_s, du_a, d_ssm_d = _rowwise(
        f_ssm_out_bwd, [_row(dz_a), _row(dz_b), _row(y_ssm_pre), _row(proj, u_blk, SSM_W)], [ssm_d],
        [(SSM_W, BF16), (SSM_W, F32)], [(1, SSM_W)], n_rows=S, tr=TM, ch=32, name="ssm_out_bwd", dep=tok)
    hw = SSM_W // 2
    u_half = (ATTN_W + 2 * KV_W) // hw
    dccat = _matmul(dy_s, xs, mode="tn", dims=(hw, 2 * SSM_H, S), tiles=(hw, 1024, TS),
                    out_dtypes=[F32], name="ssm_c_wgrad", a_index=lambda i, j, k: (k, j % 2))
    hs, dacc, du_b = _scan_bwd(dy_s, xprev, bcat, ccat, abar, S)
    dbcat = _matmul(proj, hs, mode="tn", dims=(hw, 2 * SSM_H, S), tiles=(hw, 1024, TS),
                    out_dtypes=[F32], name="ssm_b_wgrad", a_index=lambda i, j, k: (k, u_half + j % 2))
    grp = np.arange(SSM_H) // SSM_N
    gind = jnp.asarray((grp[:, None] == np.arange(128)[None, :]).astype(np.float32), BF16)
    d_lam_re, d_lam_im, d_ls, d_btr, d_bti, d_ctr, d_cti = _ssm_param_bwd(
        lam_re, lam_im, ls_x, btr, bti, dacc, dbcat, dccat, gind)

    dattn = _matmul(dy_attn, w_ap_f, mode="nt", dims=(S, ATTN_W, D), tiles=(TW, ATTN_W, D),
                    out_dtypes=[BF16], name="attn_proj_dgrad")
    gw_attn_proj = _to_col_blocks(_matmul(attn, dy_attn, mode="tn", dims=(ATTN_W, D, S), tiles=(ATTN_W, 1024, TS),
                                          out_dtypes=[BF16], name="attn_proj_wgrad"))
    (g_flight["w_attn_proj"],), tok = _exchange_start(
        [gw_attn_proj], "scatter", "grads_start_attn")
    dq, dkc, dkp, dvc, dvp, dbias, dsink = _attention_bwd(proj, attn, dattn, bias2, sinkcol, S)
    d_bias_b, d_sinks = _bucket_reduce(dbias.reshape(N_Q_HEADS, BLK * 2 * BLK),
                                       dsink.reshape(N_Q_HEADS, BLK), onehot_t)

    def f_dproj(rv, vv, i, nt):
        dqv, kc, kp, vc, vp, dua, dub, gav, gsv = rv
        keep = (i < nt - 1).astype(F32)
        dp = jnp.concatenate([dqv.astype(F32), kc + keep * kp, vc + keep * vp, dua + dub,
                              gav.astype(F32), gsv.astype(F32)], axis=-1)
        return [dp], [_colsum(dp)]

    dproj, d_b_in = _rowwise(
        f_dproj, [_row(dq), _row(dkc), _row(dkp, shift=1), _row(dvc), _row(dvp, shift=1),
                  _row(du_a), _row(du_b), _row(dga), _row(dgs)], [],
        [(IN_W, BF16)], [(1, IN_W)], n_rows=S, tr=BLK, ch=16, name="dproj", dep=tok)
    gw_in = _matmul(h, dproj, mode="tn", dims=(D, IN_W, S), tiles=(2048, 768, TS),
                    out_dtypes=[BF16], name="in_wgrad", out3=True)
    (g_flight["w_in"],), tok = _exchange_start([gw_in], "scatter", "grads_start_in")
    dh = _matmul(dproj, G["w_in"], mode="nt", dims=(S, D, IN_W), tiles=(TW, D, 768),
                 out_dtypes=[BF16], name="in_dgrad", b3=True, dep=tok)

    def f_norm1_bwd(rv, vv, i, nt):
        xv, dhv, dx1v = rv
        g, sc = vv
        r = _rms(xv)
        xh = xv * r
        t = xh * g
        dt = dhv * (1.0 + sc)
        dxh = dt * g
        dxv = dx1v + r * (dxh - xh * jnp.mean(dxh * xh, axis=-1, keepdims=True))
        return [dxv], [_colsum(dhv), _colsum(dhv * t), _colsum(dt * xh)]

    grad_x, dsh1, dsc1, d_norm1_g = _rowwise(
        f_norm1_bwd, [_row(x2d), _row(dh), _row(dx1)], [norm1_g, sc1],
        [(D, F32)], [(1, D)] * 3, n_rows=S, tr=TR, ch=32, name="norm1_bwd")

    part = _pack({
        "b_ada": [dsh1, dsc1, dg1, dsh2, dsc2, dg2], "norm1_g": d_norm1_g, "b_in": d_b_in, "norm2_g": d_norm2_g,
        "final_g": d_final_g, "lambda_re": d_lam_re, "lambda_im": d_lam_im,
        "log_step": d_ls[0, :SSM_G], "attn_sinks": d_sinks[:, 0],
        "rel_bias": jnp.transpose(d_bias_b[:, :NUM_BUCKETS]), "b_glu": d_b_glu, "ssm_d": d_ssm_d,
        "loss": loss_cols, "ssm_b_re": d_btr, "ssm_b_im": d_bti, "ssm_c_re": d_ctr, "ssm_c_im": d_cti,
    })
    zone_small = lax.dynamic_update_slice(lax.empty((N_DEV, PACK_ROWS, PACK_W), F32), part[None], (me, 0, 0))
    (small_flight,), after = _exchange_start([zone_small], "gather", "small_grads_start")

    big_out = {}
    for n in ["w_ff2", "w_ff1", "w_out", "w_ssm_proj", "w_glu", "w_attn_proj", "w_in"]:
        own, recv = _exchange_wait([g_flight[n]], "scatter", after, "grads_wait_" + n[2:])[0]
        rows, cols = shard[n].shape
        parts = [(own, lambda m: m[0])] + [
            (recv, lambda m, j=j: jnp.where(j >= m[0], j + 1, j)) for j in range(N_DEV - 1)]
        big_out[n] = _adamw(parts, shard[n], Mo[n][0], Vo[n][0], tr=_adamw_rows(rows, cols), ch=16,
                            name="adamw_" + n, prefetch=me1)
        after = big_out[n][0]

    part_all = _exchange_wait([small_flight], "gather", after, "small_grads_wait")[0][0]
    wp, mp, vp = [_pack(_small_params_packed(p)) for p in (W, Mo, Vo)]
    sg, sdelta, sm, sv = _adamw([(part_all, d) for d in range(N_DEV)], wp, mp, vp,
                                tr=PACK_ROWS, ch=8, name="adamw_small")
    lo, _ = _PACK_OFF["loss"]
    loss = jnp.sum(sg[lo])

    o_ada, _ = _PACK_OFF["b_ada"]
    dmod_all = part_all[:, o_ada:o_ada + N_MOD, :].reshape(N_DEV, N_MOD * D)
    dmod_cols = lax.dynamic_slice(dmod_all, (0, me * n_ada), (N_DEV, n_ada))
    gw_ada = _matmul(cs, dmod_cols, mode="tn", dims=(D, n_ada, N_DEV), tiles=(D, 512, N_DEV),
                     out_dtypes=[F32], name="ada_wgrad")
    big_out["w_ada"] = _adamw([(gw_ada, 0)], w_ada[0], m_w_ada[0], v_w_ada[0],
                              tr=_adamw_rows(D, n_ada), ch=16, name="adamw_w_ada")

    def leaf(kind, n):
        if n in big_out:
            return big_out[n][kind][None]
        return _unpack_small((sg, sdelta, sm, sv)[kind], n)

    outs = [loss, grad_x.reshape(1, S, D)]
    for kind in range(4):
        outs.extend(leaf(kind, n) for n in WEIGHT_ORDER)
    return tuple(outs)
```

```python
import functools
import math

import numpy as np
import jax
import jax.numpy as jnp
from jax import lax
from jax.experimental import pallas as pl
from jax.experimental.pallas import tpu as pltpu

F32 = jnp.float32
BF16 = jnp.bfloat16
MESH = pl.DeviceIdType.MESH

N_DEV = 8
D = 2048
HEAD_DIM = 64
N_Q_HEADS = 16
N_KV_HEADS = 4
GROUP = N_Q_HEADS // N_KV_HEADS
ATTN_W = N_Q_HEADS * HEAD_DIM
KV_W = N_KV_HEADS * HEAD_DIM
BLK = 128
NUM_BUCKETS = 32
MAX_DISTANCE = 128
NEG_INF = -1e30
SSM_W = 512
SSM_P = 16
SSM_G = 32
SSM_N = 64
SSM_H = SSM_G * SSM_N
D_FF = 4 * D
IN_W = ATTN_W + 2 * KV_W + SSM_W + 2 * D
N_MOD = 6
EPS = 1e-6

ADAM_LR = 0.001
ADAM_B1 = 0.9
ADAM_B2 = 0.999
ADAM_EPS = 1e-08
ADAM_WD = 0.01
ADAM_STEP = 10

VMEM_LIMIT = 56 * 1024 * 1024
PACK_W = 2048


def _cparams(sem):
    return pltpu.CompilerParams(dimension_semantics=sem, vmem_limit_bytes=VMEM_LIMIT)


def _matmul(a, b, *, mode, dims, tiles, out_dtypes, name, a_off=0, b3=False,
            out3=False, bias=None, extras=(), epilogue=None, dep=None, a_index=None, b_index=None):
    M, N, K = dims
    tm, tn, tk = tiles
    assert M % tm == 0 and N % tn == 0 and K % tk == 0, (name, dims, tiles)
    gm, gn, gk = M // tm, N // tn, K // tk
    n_extra = len(extras)
    has_bias = bias is not None
    n_out = len(out_dtypes)

    if mode == "nn":
        a_spec = pl.BlockSpec((tm, tk), lambda i, j, k: (i, a_off + k))
        if b3:
            nb = (N // N_DEV) // tn
            assert nb * tn * N_DEV == N
            b_spec = pl.BlockSpec((None, tk, tn), lambda i, j, k: (j // nb, k, j % nb))
        else:
            b_spec = pl.BlockSpec((tk, tn), lambda i, j, k: (k, j))
        dn = (((1,), (0,)), ((), ()))
    elif mode == "nt":
        a_spec = pl.BlockSpec((tm, tk), lambda i, j, k: (i, a_off + k))
        if b3:
            nb = (K // N_DEV) // tk
            assert nb * tk * N_DEV == K
            b_spec = pl.BlockSpec((None, tn, tk), lambda i, j, k: (k // nb, j, k % nb))
        else:
            b_spec = pl.BlockSpec((tn, tk), lambda i, j, k: (j, k))
        dn = (((1,), (1,)), ((), ()))
    else:
        a_spec = pl.BlockSpec((tk, tm), lambda i, j, k: (k, a_off + i))
        b_spec = pl.BlockSpec((tk, tn), lambda i, j, k: (k, j))
        dn = (((0,), (0,)), ((), ()))
    if a_index is not None:
        a_spec = pl.BlockSpec(a_spec.block_shape, a_index)
    if b_index is not None:
        b_spec = pl.BlockSpec(b_spec.block_shape, b_index)

    if out3:
        nbo = (N // N_DEV) // tn
        assert nbo * tn * N_DEV == N
        o_spec = pl.BlockSpec((None, tm, tn), lambda i, j, k: (j // nbo, i, j % nbo))
        o_shape = (N_DEV, M, N // N_DEV)
    else:
        o_spec = pl.BlockSpec((tm, tn), lambda i, j, k: (i, j))
        o_shape = (M, N)

    in_specs = [a_spec, b_spec]
    args = [a, b]
    if has_bias:
        in_specs.append(pl.BlockSpec((1, tn), lambda i, j, k: (0, j)))
        args.append(bias)
    for e in extras:
        in_specs.append(pl.BlockSpec((tm, tn), lambda i, j, k: (i, j)))
        args.append(e)
    n_dep = 0 if dep is None else 1
    if n_dep:
        in_specs.append(pl.BlockSpec(memory_space=pl.ANY))
        args.append(dep)

    def body(*refs):
        a_ref, b_ref = refs[0], refs[1]
        pos = 2
        bias_ref = None
        if has_bias:
            bias_ref = refs[pos]
            pos += 1
        extra_refs = refs[pos:pos + n_extra]
        pos += n_extra + n_dep
        out_refs = refs[pos:pos + n_out]
        acc_ref = refs[pos + n_out] if gk > 1 else None

        part = lax.dot_general(a_ref[...].astype(BF16), b_ref[...].astype(BF16), dn,
                               preferred_element_type=F32)

        def finish(acc):
            if has_bias:
                acc = acc + bias_ref[...]
            if epilogue is None:
                vals = (acc,)
            else:
                vals = epilogue(acc, *[e[...] for e in extra_refs])
            for o_ref, val in zip(out_refs, vals):
                o_ref[...] = val.astype(o_ref.dtype)

        if gk == 1:
            finish(part)
        else:
            k = pl.program_id(2)

            @pl.when(k == 0)
            def _():
                acc_ref[...] = part

            @pl.when(k > 0)
            def _():
                acc_ref[...] += part

            @pl.when(k == gk - 1)
            def _():
                finish(acc_ref[...])

    outs = pl.pallas_call(
        body,
        grid=(gm, gn, gk),
        in_specs=in_specs,
        out_specs=[o_spec] * n_out,
        out_shape=[jax.ShapeDtypeStruct(o_shape, dt) for dt in out_dtypes],
        scratch_shapes=([pltpu.VMEM((tm, tn), F32)] if gk > 1 else []),
        compiler_params=_cparams(("parallel", "parallel", "arbitrary")),
        name=name,
    )(*args)
    return outs[0] if n_out == 1 else outs


def _rowwise(fn, rows, vecs, row_outs, sum_outs, *, n_rows, tr, ch, name, dep=None, prefetch=None):
    assert n_rows % tr == 0 and tr % ch == 0
    nt = n_rows // tr
    nr, nv, nro, nso = len(rows), len(vecs), len(row_outs), len(sum_outs)
    in_specs, args = [], []
    n_pf = 0 if prefetch is None else 1
    for (arr, lead, cblk, w, shift) in rows:
        if shift:
            ridx = lambda i, shift=shift: jnp.minimum(i + shift, nt - 1)
        else:
            ridx = lambda i: i
        if arr.ndim == 3:
            def imap(i, *pf, lead=lead, cblk=cblk, ridx=ridx):
                return (lead(pf[0]) if callable(lead) else lead, ridx(i), cblk)
            in_specs.append(pl.BlockSpec((None, tr, w), imap))
        else:
            in_specs.append(pl.BlockSpec(
                (tr, w), lambda i, *pf, cblk=cblk, ridx=ridx: (ridx(i), cblk)))
        args.append(arr)
    for v in vecs:
        in_specs.append(pl.BlockSpec(v.shape, lambda i, *pf, nd=v.ndim: (0,) * nd))
        args.append(v)
    n_dep = 0 if dep is None else 1
    if n_dep:
        in_specs.append(pl.BlockSpec(memory_space=pl.ANY))
        args.append(dep)
    out_specs = [pl.BlockSpec((tr, w), lambda i, *pf: (i, 0)) for (w, _) in row_outs]
    out_shape = [jax.ShapeDtypeStruct((n_rows, w), dt) for (w, dt) in row_outs]
    for (r, w) in sum_outs:
        out_specs.append(pl.BlockSpec((r, w), lambda i, *pf: (0, 0)))
        out_shape.append(jax.ShapeDtypeStruct((r, w), F32))

    def body(*refs):
        refs = refs[n_pf:]
        i = pl.program_id(0)
        r_in = refs[:nr]
        v_in = refs[nr:nr + nv]
        r_out = refs[nr + nv + n_dep:nr + nv + n_dep + nro]
        s_out = refs[nr + nv + n_dep + nro:]
        s_out, s_acc = s_out[:nso], s_out[nso:]
        if nso:
            @pl.when(i == 0)
            def _():
                for s in s_acc:
                    s[...] = jnp.zeros(s.shape, F32)
        vvals = [v[...] for v in v_in]

        def chunk(ci, carry):
            r0 = pl.multiple_of(ci * ch, ch)
            rv = [r[pl.ds(r0, ch), :].astype(F32) for r in r_in]
            pieces = [fn([v[8 * k:8 * (k + 1)] for v in rv], vvals, i, nt) for k in range(ch // 8)]
            for j, ref in enumerate(r_out):
                val = jnp.concatenate([ro[j] for ro, _ in pieces], axis=0) if ch > 8 else pieces[0][0][j]
                ref[pl.ds(r0, ch), :] = val.astype(ref.dtype)
            for j, ref in enumerate(s_acc):
                ref[...] += functools.reduce(lambda a, b: a + b, [so[j] for _, so in pieces])
            return carry

        lax.fori_loop(0, tr // ch, chunk, 0)
        if nso:
            @pl.when(i == nt - 1)
            def _():
                for s, acc in zip(s_out, s_acc):
                    s[...] = jnp.sum(acc[...], axis=0, keepdims=True)

    outs = pl.pallas_call(
        body,
        grid_spec=pltpu.PrefetchScalarGridSpec(
            num_scalar_prefetch=n_pf, grid=(nt,), in_specs=in_specs, out_specs=out_specs,
            scratch_shapes=[pltpu.VMEM((8, w), F32) for (_, w) in sum_outs]),
        out_shape=out_shape,
        compiler_params=_cparams(("arbitrary",)),
        name=name,
    )(*([prefetch] if n_pf else []), *args)
    return outs


def _row(arr, cblk=0, w=None, lead=0, shift=0):
    return (arr, lead, cblk, arr.shape[-1] if w is None else w, shift)


def _colsum(v):
    parts = [v[8 * k:8 * (k + 1)] for k in range(v.shape[0] // 8)]
    return functools.reduce(lambda a, b: a + b, parts)


def _rms(x):
    return lax.rsqrt(jnp.mean(x * x, axis=-1, keepdims=True) + EPS)


def _sigmoid(x):
    return 1.0 / (1.0 + jnp.exp(-x))


_GELU_C = math.sqrt(2.0 / math.pi)


def _gelu(x):
    return 0.5 * x * (1.0 + jnp.tanh(_GELU_C * (x + 0.044715 * (x * x * x))))


def _gelu_grad(x):
    t = jnp.tanh(_GELU_C * (x + 0.044715 * (x * x * x)))
    return 0.5 * (1.0 + t) + 0.5 * x * (1.0 - t * t) * (_GELU_C * (1.0 + 3.0 * 0.044715 * (x * x)))


def _my_pos():
    return lax.axis_index("x"), lax.axis_index("y"), lax.axis_index("c")


def _flip(pos, k):
    x, y, c = pos
    return (1 - x if k & 4 else x, 1 - y if k & 2 else y, 1 - c if k & 1 else c)


def _dev_id(pos):
    return 4 * pos[0] + 2 * pos[1] + pos[2]


def _small_allgather(x, name):
    r, c = x.shape

    def body(x_ref, out_ref, send_sems, recv_sems):
        me = _my_pos()
        out_ref[_dev_id(me)] = x_ref[...]
        copies = []
        for k in range(1, N_DEV):
            cp = pltpu.make_async_remote_copy(
                src_ref=x_ref, dst_ref=out_ref.at[_dev_id(me)],
                send_sem=send_sems.at[k - 1], recv_sem=recv_sems.at[k - 1],
                device_id=_flip(me, k), device_id_type=MESH)
            cp.start()
            copies.append(cp)
        for k in range(1, N_DEV):
            peer = _flip(me, k)
            pltpu.make_async_remote_copy(
                src_ref=x_ref, dst_ref=out_ref.at[_dev_id(peer)],
                send_sem=send_sems.at[k - 1], recv_sem=recv_sems.at[k - 1],
                device_id=peer, device_id_type=MESH).wait_recv()
        for cp in copies:
            cp.wait_send()

    return pl.pallas_call(
        body,
        out_shape=jax.ShapeDtypeStruct((N_DEV, r, c), x.dtype),
        in_specs=[pl.BlockSpec(memory_space=pltpu.VMEM)],
        out_specs=pl.BlockSpec(memory_space=pltpu.VMEM),
        scratch_shapes=[pltpu.SemaphoreType.DMA((N_DEV - 1,)),
                        pltpu.SemaphoreType.DMA((N_DEV - 1,))],
        compiler_params=pltpu.CompilerParams(vmem_limit_bytes=VMEM_LIMIT),
        name=name,
    )(x)


_HBM = pl.BlockSpec(memory_space=pltpu.HBM)
_SEM = pl.BlockSpec(memory_space=pltpu.SEMAPHORE)
_EFFECT = pltpu.SideEffectType.DATAFLOW_SIDE_EFFECTING


def _relay_copy(zone, send_sems, recv_sems, k, block, to):
    slot = zone.at[_dev_id(block)]
    return pltpu.make_async_remote_copy(
        src_ref=slot, dst_ref=slot, send_sem=send_sems.at[k], recv_sem=recv_sems.at[k],
        device_id=to, device_id_type=MESH)


def _relay_peers():
    x, y, c = _my_pos()
    return (x, y, c), (x, y, 1 - c), [(1 - x, y), (x, 1 - y), (1 - x, 1 - y)]


def _relay_start_call(zones, n_sems, issue, name, after=None):
    n = len(zones)
    n_after = 0 if after is None else 1

    def body(*refs):
        refs = refs[:n] + refs[n + n_after:]
        send, recv, token = refs[n:2 * n], refs[2 * n:3 * n], refs[4 * n]
        for a in range(n):
            issue(refs[a], send[a], recv[a])
        token[...] = jnp.zeros(token.shape, token.dtype)

    sem = pltpu.SemaphoreType.DMA((n_sems,))
    outs = pl.pallas_call(
        body,
        name=name,
        out_shape=([sem] * (2 * n) + [pltpu.HBM(z.shape, z.dtype) for z in zones]
                   + [jax.ShapeDtypeStruct((8, 128), F32)]),
        in_specs=[_HBM] * n + [pl.BlockSpec(memory_space=pl.ANY)] * n_after,
        out_specs=[_SEM] * (2 * n) + [_HBM] * n + [pl.BlockSpec(memory_space=pltpu.VMEM)],
        input_output_aliases={a: 2 * n + a for a in range(n)},
        compiler_params=pltpu.CompilerParams(has_side_effects=_EFFECT),
    )(*[pltpu.with_memory_space_constraint(z, pltpu.HBM) for z in zones],
      *([after] if n_after else []))
    return [(outs[a], outs[n + a], outs[2 * n + a]) for a in range(n)], outs[3 * n]


def _relay_wait_call(flights, settle, after, name):
    n = len(flights)
    after = list(after) if isinstance(after, (list, tuple)) else [after]

    def body(*refs):
        send, recv = refs[n:2 * n], refs[2 * n:3 * n]
        for a in range(n):
            settle(refs[a], send[a], recv[a])

    outs = pl.pallas_call(
        body,
        name=name,
        out_shape=[pltpu.HBM(f[2].shape, f[2].dtype) for f in flights],
        in_specs=[_HBM] * n + [_SEM] * (2 * n) + [pl.BlockSpec(memory_space=pl.ANY)] * len(after),
        out_specs=[_HBM] * n,
        input_output_aliases={a: a for a in range(n)},
        compiler_params=pltpu.CompilerParams(has_side_effects=_EFFECT),
    )(*[f[2] for f in flights], *[f[0] for f in flights], *[f[1] for f in flights], *after)
    return list(outs)


def _relay_gather_start(zones, name, after=None):
    def issue(zone, send, recv):
        me, sib, chips = _relay_peers()
        _relay_copy(zone, send, recv, 0, me, sib).start()
        for j, chip in enumerate(chips):
            _relay_copy(zone, send, recv, 1 + j, me, (*chip, me[2])).start()
    return _relay_start_call(zones, 4, issue, name, after)


def _relay_gather_arrive(flights, after, name):
    def settle(zone, send, recv):
        me, sib, chips = _relay_peers()
        _relay_copy(zone, send, recv, 0, sib, me).wait_recv()
        _relay_copy(zone, send, recv, 0, me, sib).wait_send()
        for j, chip in enumerate(chips):
            _relay_copy(zone, send, recv, 1 + j, (*chip, me[2]), me).wait_recv()
            _relay_copy(zone, send, recv, 1 + j, me, (*chip, me[2])).wait_send()
    return _relay_wait_call(flights, settle, after, name)


def _relay_pass_start(zones, name, after=None):
    def issue(zone, send, recv):
        me, sib, chips = _relay_peers()
        for j, chip in enumerate(chips):
            _relay_copy(zone, send, recv, j, (*chip, me[2]), sib).start()
    return _relay_start_call(zones, 3, issue, name, after)


def _relay_pass_wait(flights, after, name):
    def settle(zone, send, recv):
        me, sib, chips = _relay_peers()
        for j, chip in enumerate(chips):
            _relay_copy(zone, send, recv, j, (*chip, sib[2]), me).wait_recv()
            _relay_copy(zone, send, recv, j, (*chip, me[2]), sib).wait_send()
    return _relay_wait_call(flights, settle, after, name)


def _exchange_copy(kind, bufs, send_sems, recv_sems, me, k, arriving):
    peer = _flip(me, k)
    my_id, peer_id = _dev_id(me), _dev_id(peer)
    if kind == "gather":
        slot = bufs[0].at[peer_id if arriving else my_id]
        src, dst = slot, slot
    else:
        src = bufs[0].at[my_id if arriving else peer_id]
        dst = bufs[1].at[peer_id if arriving else my_id]
    return pltpu.make_async_remote_copy(
        src_ref=src, dst_ref=dst, send_sem=send_sems.at[k - 1], recv_sem=recv_sems.at[k - 1],
        device_id=peer, device_id_type=MESH)


def _exchange_start(arrays, kind, name, after=None):
    n = len(arrays)
    n_after = 0 if after is None else 1
    if kind == "gather":
        bufs = [[a] for a in arrays]
    else:
        bufs = [[a, lax.empty(a.shape, a.dtype)] for a in arrays]
    nb = len(bufs[0])
    flat = [b for group in bufs for b in group]

    def body(*refs):
        outs_at = nb * n + n_after
        send = refs[outs_at:outs_at + n]
        recv = refs[outs_at + n:outs_at + 2 * n]
        token = refs[outs_at + 2 * n + nb * n]
        me = _my_pos()
        for a in range(n):
            for k in range(1, N_DEV):
                _exchange_copy(kind, refs[nb * a:nb * (a + 1)], send[a], recv[a], me, k, False).start()
        token[...] = jnp.zeros(token.shape, token.dtype)

    sem = pltpu.SemaphoreType.DMA((N_DEV - 1,))
    outs = pl.pallas_call(
        body,
        name=name,
        out_shape=([sem] * (2 * n) + [pltpu.HBM(b.shape, b.dtype) for b in flat]
                   + [jax.ShapeDtypeStruct((8, 128), F32)]),
        in_specs=[_HBM] * (nb * n) + [pl.BlockSpec(memory_space=pl.ANY)] * n_after,
        out_specs=[_SEM] * (2 * n) + [_HBM] * (nb * n) + [pl.BlockSpec(memory_space=pltpu.VMEM)],
        input_output_aliases={i: 2 * n + i for i in range(nb * n)},
        compiler_params=pltpu.CompilerParams(has_side_effects=_EFFECT),
    )(*[pltpu.with_memory_space_constraint(b, pltpu.HBM) for b in flat],
      *([after] if n_after else []))
    flights = [(outs[a], outs[n + a], list(outs[2 * n + nb * a:2 * n + nb * (a + 1)]))
               for a in range(n)]
    return flights, outs[2 * n + nb * n]


def _exchange_wait(flights, kind, after, name):
    n = len(flights)
    nb = len(flights[0][2])
    flat = [b for f in flights for b in f[2]]

    def body(*refs):
        send = refs[nb * n:nb * n + n]
        recv = refs[nb * n + n:nb * n + 2 * n]
        me = _my_pos()
        for a in range(n):
            for k in range(1, N_DEV):
                bufs = refs[nb * a:nb * (a + 1)]
                _exchange_copy(kind, bufs, send[a], recv[a], me, k, False).wait_send()
                _exchange_copy(kind, bufs, send[a], recv[a], me, k, True).wait_recv()

    outs = pl.pallas_call(
        body,
        name=name,
        out_shape=[pltpu.HBM(b.shape, b.dtype) for b in flat],
        in_specs=[_HBM] * (nb * n) + [_SEM] * (2 * n) + [pl.BlockSpec(memory_space=pl.ANY)],
        out_specs=[_HBM] * (nb * n),
        input_output_aliases={i: i for i in range(nb * n)},
        compiler_params=pltpu.CompilerParams(has_side_effects=_EFFECT),
    )(*flat, *[f[0] for f in flights], *[f[1] for f in flights], after)
    return [list(outs[nb * a:nb * (a + 1)]) for a in range(n)]


def _t5_buckets_block():
    qi = np.arange(BLK)[:, None]
    ki = np.arange(2 * BLK)[None, :]
    n = np.maximum(qi + BLK - ki, 0)
    max_exact = NUM_BUCKETS // 2
    large = max_exact + (np.log(np.maximum(n, 1) / max_exact)
                         / np.log(MAX_DISTANCE / max_exact)
                         * (NUM_BUCKETS - max_exact)).astype(np.int32)
    large = np.minimum(large, NUM_BUCKETS - 1)
    return np.where(n < max_exact, n, large).astype(np.int32)


def _band_mask():
    qi = np.arange(BLK)[:, None]
    ki = np.arange(2 * BLK)[None, :]
    dist = qi + BLK - ki
    return (dist >= 0) & (dist < BLK)


def _attn_scores(q_ref, kp_ref, kc_ref, hkv):
    c0 = hkv * HEAD_DIM
    kk = jnp.concatenate([kp_ref[:, c0:c0 + HEAD_DIM], kc_ref[:, c0:c0 + HEAD_DIM]],
                         axis=0).astype(BF16)
    qg = jnp.concatenate(
        [q_ref[:, (hkv * GROUP + g) * HEAD_DIM:(hkv * GROUP + g + 1) * HEAD_DIM]
         for g in range(GROUP)], axis=0).astype(BF16)
    s = lax.dot_general(qg, kk, (((1,), (1,)), ((), ())), preferred_element_type=F32)
    return qg, kk, s


def _attn_softmax(s, bias_ref, sink_ref, hkv):
    r0, r1 = hkv * GROUP * BLK, (hkv + 1) * GROUP * BLK
    s = s * (HEAD_DIM ** -0.5) + bias_ref[r0:r1, :]
    sink = sink_ref[r0:r1, :]
    m = jnp.maximum(jnp.max(s, axis=-1, keepdims=True), sink)
    p = jnp.exp(s - m)
    e_sink = jnp.exp(sink - m)
    inv = 1.0 / (jnp.sum(p, axis=-1, keepdims=True) + e_sink)
    return p * inv, e_sink * inv


def _kv_rows(p_ref, c_ref, hkv):
    c0 = hkv * HEAD_DIM
    return jnp.concatenate([p_ref[:, c0:c0 + HEAD_DIM], c_ref[:, c0:c0 + HEAD_DIM]],
                           axis=0).astype(BF16)


ATT_Q_FWD = 4
ATT_Q_BWD = 2


def _attn_in_specs(bias2, nq):
    prev = lambda n: jnp.maximum(nq * n - 1, 0)
    kcol = ATTN_W // KV_W
    return [
        pl.BlockSpec((nq * BLK, ATTN_W), lambda n: (n, 0)),
        pl.BlockSpec((BLK, KV_W), lambda n: (prev(n), kcol)),
        pl.BlockSpec((nq * BLK, KV_W), lambda n: (n, kcol)),
        pl.BlockSpec((BLK, KV_W), lambda n: (prev(n), kcol + 1)),
        pl.BlockSpec((nq * BLK, KV_W), lambda n: (n, kcol + 1)),
        pl.BlockSpec(bias2.shape, lambda n: (0, 0, 0)),
    ]


def _attn_views(t, q_ref, kp_ref, kc_ref, vp_ref, vc_ref, bias_ref):
    rows = pl.ds(t * BLK, BLK)
    before = pl.ds((t - 1) * BLK, BLK)
    table = jnp.minimum(pl.program_id(0), 1) if t == 0 else 1
    return (q_ref.at[rows, :],
            kp_ref if t == 0 else kc_ref.at[before, :], kc_ref.at[rows, :],
            vp_ref if t == 0 else vc_ref.at[before, :], vc_ref.at[rows, :],
            bias_ref.at[table])


def _attention_fwd(proj, bias2, sinkcol, n_rows):
    nq = min(ATT_Q_FWD, n_rows // BLK)
    steps = n_rows // (nq * BLK)

    def body(q_ref, kp_ref, kc_ref, vp_ref, vc_ref, bias_ref, sink_ref, o_ref):
        views = [_attn_views(t, q_ref, kp_ref, kc_ref, vp_ref, vc_ref, bias_ref) for t in range(nq)]
        work = [(t, hkv) for t in range(nq) for hkv in range(N_KV_HEADS)]
        scores = {w: _attn_scores(views[w[0]][0], views[w[0]][1], views[w[0]][2], w[1])[2] for w in work}
        probs = {w: _attn_softmax(scores[w], views[w[0]][5], sink_ref, w[1])[0] for w in work}
        outs = {w: jnp.dot(probs[w].astype(BF16), _kv_rows(views[w[0]][3], views[w[0]][4], w[1]),
                           preferred_element_type=F32) for w in work}
        for t, hkv in work:
            for g in range(GROUP):
                h = hkv * GROUP + g
                o_ref[t * BLK:(t + 1) * BLK, h * HEAD_DIM:(h + 1) * HEAD_DIM] = (
                    outs[t, hkv][g * BLK:(g + 1) * BLK, :].astype(o_ref.dtype))

    return pl.pallas_call(
        body,
        grid=(steps,),
        in_specs=_attn_in_specs(bias2, nq) + [pl.BlockSpec(sinkcol.shape, lambda n: (0, 0))],
        out_specs=pl.BlockSpec((nq * BLK, ATTN_W), lambda n: (n, 0)),
        out_shape=jax.ShapeDtypeStruct((n_rows, ATTN_W), BF16),
        compiler_params=_cparams(("parallel",)),
        name="attn_fwd",
    )(proj, proj, proj, proj, proj, bias2, sinkcol)


def _attention_bwd(proj, attn, dattn, bias2, sinkcol, n_rows):
    nq = min(ATT_Q_BWD, n_rows // BLK)
    steps = n_rows // (nq * BLK)
    scale = HEAD_DIM ** -0.5
    dn_t = (((0,), (0,)), ((), ()))

    def body(q_ref, kp_ref, kc_ref, vp_ref, vc_ref, bias_ref, o_ref, do_ref, sink_ref,
             dq_ref, dkc_ref, dkp_ref, dvc_ref, dvp_ref, dbias_ref, dsink_ref):
        @pl.when(pl.program_id(0) == 0)
        def _():
            dbias_ref[...] = jnp.zeros(dbias_ref.shape, F32)
            dsink_ref[...] = jnp.zeros(dsink_ref.shape, F32)

        views = [_attn_views(t, q_ref, kp_ref, kc_ref, vp_ref, vc_ref, bias_ref) for t in range(nq)]
        work = [(t, hkv) for t in range(nq) for hkv in range(N_KV_HEADS)]
        qk = {w: _attn_scores(views[w[0]][0], views[w[0]][1], views[w[0]][2], w[1]) for w in work}
        dog, dps, deltas = {}, {}, {}
        for t, hkv in work:
            rows = slice(t * BLK, (t + 1) * BLK)
            hs = [hkv * GROUP + g for g in range(GROUP)]
            d_o = jnp.concatenate([do_ref[rows, h * HEAD_DIM:(h + 1) * HEAD_DIM] for h in hs], axis=0)
            o = jnp.concatenate([o_ref[rows, h * HEAD_DIM:(h + 1) * HEAD_DIM] for h in hs], axis=0)
            deltas[t, hkv] = jnp.sum(d_o.astype(F32) * o.astype(F32), axis=-1, keepdims=True)
            dog[t, hkv] = d_o.astype(BF16)
            dps[t, hkv] = lax.dot_general(dog[t, hkv], _kv_rows(views[t][3], views[t][4], hkv),
                                          (((1,), (1,)), ((), ())), preferred_element_type=F32)
        p16, ds16 = {}, {}
        for t, hkv in work:
            r0, r1 = hkv * GROUP * BLK, (hkv + 1) * GROUP * BLK
            p, p_sink = _attn_softmax(qk[t, hkv][2], views[t][5], sink_ref, hkv)
            ds = p * (dps[t, hkv] - deltas[t, hkv])
            dbias_ref[r0:r1, :] += ds
            dsink_ref[r0:r1, :] += -(p_sink * deltas[t, hkv])
            p16[t, hkv] = p.astype(BF16)
            ds16[t, hkv] = ds.astype(BF16)
        for t, hkv in work:
            rows = slice(t * BLK, (t + 1) * BLK)
            c0 = hkv * HEAD_DIM
            qg, kk, _ = qk[t, hkv]
            dqg = jnp.dot(ds16[t, hkv], kk, preferred_element_type=F32) * scale
            dkk = lax.dot_general(ds16[t, hkv], qg, dn_t, preferred_element_type=F32) * scale
            dvv = lax.dot_general(p16[t, hkv], dog[t, hkv], dn_t, preferred_element_type=F32)
            for g in range(GROUP):
                h = hkv * GROUP + g
                dq_ref[rows, h * HEAD_DIM:(h + 1) * HEAD_DIM] = (
                    dqg[g * BLK:(g + 1) * BLK, :].astype(dq_ref.dtype))
            dkp_ref[rows, c0:c0 + HEAD_DIM] = dkk[:BLK].astype(dkp_ref.dtype)
            dkc_ref[rows, c0:c0 + HEAD_DIM] = dkk[BLK:].astype(dkc_ref.dtype)
            dvp_ref[rows, c0:c0 + HEAD_DIM] = dvv[:BLK].astype(dvp_ref.dtype)
            dvc_ref[rows, c0:c0 + HEAD_DIM] = dvv[BLK:].astype(dvc_ref.dtype)

    wide = pl.BlockSpec((nq * BLK, ATTN_W), lambda n: (n, 0))
    kv_out = pl.BlockSpec((nq * BLK, KV_W), lambda n: (n, 0))
    kv_shape = jax.ShapeDtypeStruct((n_rows, KV_W), F32)
    acc_shape = bias2.shape[1:]
    return pl.pallas_call(
        body,
        grid=(steps,),
        in_specs=_attn_in_specs(bias2, nq) + [wide, wide, pl.BlockSpec(sinkcol.shape, lambda n: (0, 0))],
        out_specs=[
            wide, kv_out, kv_out, kv_out, kv_out,
            pl.BlockSpec(acc_shape, lambda n: (0, 0)),
            pl.BlockSpec(sinkcol.shape, lambda n: (0, 0)),
        ],
        out_shape=[
            jax.ShapeDtypeStruct((n_rows, ATTN_W), BF16),
            kv_shape, kv_shape, kv_shape, kv_shape,
            jax.ShapeDtypeStruct(acc_shape, F32),
            jax.ShapeDtypeStruct(sinkcol.shape, F32),
        ],
        compiler_params=_cparams(("arbitrary",)),
        name="attn_bwd",
    )(proj, proj, proj, proj, proj, bias2, attn, dattn, sinkcol)


def _bias_tables(rel_bias_t, onehot_t, band_first, band_rest):
    def body(rb_ref, oh_ref, mf_ref, mr_ref, out_ref):
        acc = jnp.zeros((N_Q_HEADS, BLK * 2 * BLK), F32)
        for part in _split3(rb_ref[...]):
            acc = acc + jnp.dot(part, oh_ref[...], preferred_element_type=F32)
        out_ref[0] = jnp.where(mf_ref[...] > 0.0, acc, NEG_INF)
        out_ref[1] = jnp.where(mr_ref[...] > 0.0, acc, NEG_INF)

    return pl.pallas_call(
        body,
        out_shape=jax.ShapeDtypeStruct((2, N_Q_HEADS, BLK * 2 * BLK), F32),
        compiler_params=pltpu.CompilerParams(vmem_limit_bytes=VMEM_LIMIT),
        name="bias_tables",
    )(rel_bias_t, onehot_t, band_first, band_rest)


def _split3(a):
    hi = a.astype(BF16)
    r1 = a - hi.astype(F32)
    mid = r1.astype(BF16)
    lo = (r1 - mid.astype(F32)).astype(BF16)
    return hi, mid, lo


def _bucket_reduce(dbias, dsink, onehot_t):
    def body(db_ref, ds_ref, oh_ref, ob_ref, os_ref):
        acc = jnp.zeros((N_Q_HEADS, 128), F32)
        for part in _split3(db_ref[...]):
            acc = acc + lax.dot_general(part, oh_ref[...], (((1,), (1,)), ((), ())),
                                        preferred_element_type=F32)
        ob_ref[...] = acc
        os_ref[...] = jnp.broadcast_to(jnp.sum(ds_ref[...], axis=-1, keepdims=True),
                                       os_ref.shape)

    return pl.pallas_call(
        body,
        out_shape=[jax.ShapeDtypeStruct((N_Q_HEADS, 128), F32),
                   jax.ShapeDtypeStruct((N_Q_HEADS, 128), F32)],
        compiler_params=pltpu.CompilerParams(vmem_limit_bytes=VMEM_LIMIT),
        name="bias_bucket_reduce",
    )(dbias, dsink, onehot_t)


def _disc(lr, li, ls, btr, bti):
    lam_re = jnp.minimum(lr, -1e-4)
    delta = jnp.exp(ls)
    mag = jnp.exp(lam_re * delta)
    ang = li * delta
    ar, ai = mag * jnp.cos(ang), mag * jnp.sin(ang)
    nr, ni = ar - 1.0, ai
    den = lam_re * lam_re + li * li
    fr = (nr * lam_re + ni * li) / den
    fi = (ni * lam_re - nr * li) / den
    bbr = fr * btr - fi * bti
    bbi = fr * bti + fi * btr
    return ar, ai, bbr, bbi


def _block_mask():
    row = lax.broadcasted_iota(jnp.int32, (SSM_W, SSM_H), 0)
    col = lax.broadcasted_iota(jnp.int32, (SSM_W, SSM_H), 1)
    return (row // SSM_P) == (col // SSM_N)


def _ssm_setup(lr, li, ls, btr, bti, ctr, cti):
    def body(lr_ref, li_ref, ls_ref, btr_ref, bti_ref, ctr_ref, cti_ref, a_ref, b_ref, c_ref):
        ar, ai, bbr, bbi = _disc(lr_ref[...], li_ref[...], ls_ref[...], btr_ref[...], bti_ref[...])
        a_ref[:, :SSM_H] = ar
        a_ref[:, SSM_H:] = ai
        mask = _block_mask()
        blk = lambda t: jnp.where(mask, jnp.tile(t, (SSM_G, 1)), 0.0)
        b_ref[:, :SSM_H] = blk(bbr).astype(BF16)
        b_ref[:, SSM_H:] = blk(bbi).astype(BF16)
        c_ref[:, :SSM_H] = blk(ctr_ref[...]).astype(BF16)
        c_ref[:, SSM_H:] = blk(-cti_ref[...]).astype(BF16)

    return pl.pallas_call(
        body,
        out_shape=[jax.ShapeDtypeStruct((1, 2 * SSM_H), F32),
                   jax.ShapeDtypeStruct((SSM_W, 2 * SSM_H), BF16),
                   jax.ShapeDtypeStruct((SSM_W, 2 * SSM_H), BF16)],
        compiler_params=pltpu.CompilerParams(vmem_limit_bytes=VMEM_LIMIT),
        name="ssm_setup",
    )(lr, li, ls, btr, bti, ctr, cti)


def _ssm_param_bwd(lr, li, ls, btr, bti, dacc, dbcat, dccat, gind):
    def body(lr_ref, li_ref, ls_ref, btr_ref, bti_ref, dacc_ref, db_ref, dc_ref, g_ref,
             dlr_ref, dli_ref, dls_ref, dbtr_ref, dbti_ref, dctr_ref, dcti_ref):
        dar = jnp.sum(dacc_ref[:, :SSM_H], axis=0, keepdims=True)
        dai = jnp.sum(dacc_ref[:, SSM_H:], axis=0, keepdims=True)
        col = lax.broadcasted_iota(jnp.int32, (SSM_P, 2 * SSM_H), 1)
        grp = (col % SSM_H) // SSM_N
        db = jnp.zeros((SSM_P, 2 * SSM_H), F32)
        dc = jnp.zeros((SSM_P, 2 * SSM_H), F32)
        half = SSM_G // 2
        for g in range(SSM_G):
            sel = grp == g
            r0 = (g % half) * SSM_P
            db = db + jnp.where(sel, db_ref[r0:r0 + SSM_P, :], 0.0)
            dc = dc + jnp.where(sel, dc_ref[r0:r0 + SSM_P, :], 0.0)
        dctr_ref[...] = dc[:, :SSM_H]
        dcti_ref[...] = -dc[:, SSM_H:]
        prim = (lr_ref[...], li_ref[...], ls_ref[...], btr_ref[...], bti_ref[...])
        _, vjp = jax.vjp(_disc, *prim)
        dlr, dli, dls, dbtr, dbti = vjp((dar, dai, db[:, :SSM_H], db[:, SSM_H:]))
        dlr_ref[...] = dlr
        dli_ref[...] = dli
        dbtr_ref[...] = dbtr
        dbti_ref[...] = dbti
        acc = jnp.zeros((8, 128), F32)
        for part in _split3(jnp.broadcast_to(dls, (8, SSM_H))):
            acc = acc + jnp.dot(part, g_ref[...], preferred_element_type=F32)
        dls_ref[...] = acc

    vec = jax.ShapeDtypeStruct((1, SSM_H), F32)
    mat = jax.ShapeDtypeStruct((SSM_P, SSM_H), F32)
    return pl.pallas_call(
        body,
        out_shape=[vec, vec, jax.ShapeDtypeStruct((8, 128), F32), mat, mat, mat, mat],
        compiler_params=pltpu.CompilerParams(vmem_limit_bytes=VMEM_LIMIT),
        name="ssm_param_bwd",
    )(lr, li, ls, btr, bti, dacc, dbcat, dccat, gind)


SCAN_TR = 256


def _cmul_add(vr, vi, pr, pi, sr, si):
    return vr + pr * sr - pi * si, vi + pr * si + pi * sr


def _bcast_row(v, row, which):
    return jnp.broadcast_to(v[which:which + 1, :], v.shape)


def _scan_tables(a_ref, tab_ref, reverse):
    H = SSM_H
    ar = jnp.broadcast_to(a_ref[:, :H], (8, H))
    ai = jnp.broadcast_to(a_ref[:, H:], (8, H))
    if reverse:
        ai = -ai
    row = lax.broadcasted_iota(jnp.int32, (8, H), 0)
    pw = [(ar, ai)]
    for _ in range(7):
        cr, ci = pw[-1]
        pw.append((cr * ar - ci * ai, cr * ai + ci * ar))
    pcr = jnp.zeros((8, H), F32)
    pci = jnp.zeros((8, H), F32)
    for e in range(8):
        sel = (row == (7 - e)) if reverse else (row == e)
        pcr = jnp.where(sel, pw[e][0], pcr)
        pci = jnp.where(sel, pw[e][1], pci)
    tab_ref[0, :, :H] = pcr
    tab_ref[0, :, H:] = pci
    for t, k in enumerate((1, 2, 4)):
        keep = (row < 8 - k) if reverse else (row >= k)
        tab_ref[1 + t, :, :H] = jnp.where(keep, pw[k - 1][0], 0.0)
        tab_ref[1 + t, :, H:] = jnp.where(keep, pw[k - 1][1], 0.0)


def _scan_group(vr, vi, cr, ci, tab_ref, reverse):
    H = SSM_H
    for t, k in enumerate((1, 2, 4)):
        sh = 8 - k if reverse else k
        vr, vi = _cmul_add(vr, vi, tab_ref[1 + t, :, :H], tab_ref[1 + t, :, H:],
                           pltpu.roll(vr, sh, 0), pltpu.roll(vi, sh, 0))
    return _cmul_add(vr, vi, tab_ref[0, :, :H], tab_ref[0, :, H:], cr, ci)


def _blockdiag_expand(x, w_ref, out_ref):
    hw, cb = SSM_W // 2, SSM_H // 2
    for j in range(4):
        h = j % 2
        out_ref[:, j * cb:(j + 1) * cb] = jnp.dot(
            x[:, h * hw:(h + 1) * hw], w_ref[h * hw:(h + 1) * hw, j * cb:(j + 1) * cb],
            preferred_element_type=F32)


def _blockdiag_contract(x_ref, w_ref):
    hw, cb = SSM_W // 2, SSM_H // 2
    nt = (((1,), (1,)), ((), ()))
    halves = []
    for h in range(2):
        acc = None
        for j in (h, 2 + h):
            part = lax.dot_general(x_ref[:, j * cb:(j + 1) * cb],
                                   w_ref[h * hw:(h + 1) * hw, j * cb:(j + 1) * cb], nt,
                                   preferred_element_type=F32)
            acc = part if acc is None else acc + part
        halves.append(acc)
    return jnp.concatenate(halves, axis=1)


def _scan_fwd(proj, u_blk, bcat, ccat, abar, n_rows):
    H = SSM_H
    nt = n_rows // SCAN_TR

    def body(u_ref, b_ref, c_ref, a_ref, xs_ref, xp_ref, yc_ref, bu_ref, tab_ref, carry_ref):
        @pl.when(pl.program_id(0) == 0)
        def _():
            _scan_tables(a_ref, tab_ref, False)
            carry_ref[...] = jnp.zeros(carry_ref.shape, F32)

        _blockdiag_expand(u_ref[...].astype(BF16), b_ref, bu_ref)
        row = lax.broadcasted_iota(jnp.int32, (8, H), 0)

        def group(j, carry):
            cr, ci = carry
            r0 = pl.multiple_of(j * 16, 16)
            xr, xi = [], []
            for half in range(2):
                rr = pl.multiple_of(r0 + 8 * half, 8)
                vr, vi = _scan_group(bu_ref[pl.ds(rr, 8), :H], bu_ref[pl.ds(rr, 8), H:],
                                     cr, ci, tab_ref, False)
                xp_ref[pl.ds(rr, 8), :H] = jnp.where(row == 0, cr, pltpu.roll(vr, 1, 0))
                xp_ref[pl.ds(rr, 8), H:] = jnp.where(row == 0, ci, pltpu.roll(vi, 1, 0))
                cr, ci = _bcast_row(vr, row, 7), _bcast_row(vi, row, 7)
                xr.append(vr)
                xi.append(vi)
            xs_ref[pl.ds(r0, 16), :H] = jnp.concatenate(xr, axis=0).astype(BF16)
            xs_ref[pl.ds(r0, 16), H:] = jnp.concatenate(xi, axis=0).astype(BF16)
            return cr, ci

        cr, ci = lax.fori_loop(0, SCAN_TR // 16, group,
                               (carry_ref[:, :H], carry_ref[:, H:]))
        carry_ref[:, :H] = cr
        carry_ref[:, H:] = ci
        yc_ref[...] = _blockdiag_contract(xs_ref, c_ref)

    tile = lambda w: pl.BlockSpec((SCAN_TR, w), lambda i: (i, 0))
    whole = lambda a: pl.BlockSpec(a.shape, lambda i: (0, 0))
    return pl.pallas_call(
        body,
        grid=(nt,),
        in_specs=[pl.BlockSpec((SCAN_TR, SSM_W), lambda i: (i, u_blk)),
                  whole(bcat), whole(ccat), whole(abar)],
        out_specs=[tile(2 * H), tile(2 * H), tile(SSM_W)],
        out_shape=[jax.ShapeDtypeStruct((n_rows, 2 * H), BF16),
                   jax.ShapeDtypeStruct((n_rows, 2 * H), F32),
                   jax.ShapeDtypeStruct((n_rows, SSM_W), F32)],
        scratch_shapes=[pltpu.VMEM((SCAN_TR, 2 * H), F32), pltpu.VMEM((4, 8, 2 * H), F32),
                        pltpu.VMEM((8, 2 * H), F32)],
        compiler_params=_cparams(("arbitrary",)),
        name="ssm_scan_fwd",
    )(proj, bcat, ccat, abar)


def _scan_bwd(dy, xprev, bcat, ccat, abar, n_rows):
    H = SSM_H
    nt = n_rows // SCAN_TR

    def body(dy_ref, xp_ref, b_ref, c_ref, a_ref, h_ref, da_ref, du_ref, g_ref, tab_ref, carry_ref):
        @pl.when(pl.program_id(0) == 0)
        def _():
            _scan_tables(a_ref, tab_ref, True)
            carry_ref[...] = jnp.zeros(carry_ref.shape, F32)
            da_ref[...] = jnp.zeros(da_ref.shape, F32)

        _blockdiag_expand(dy_ref[...], c_ref, g_ref)
        row = lax.broadcasted_iota(jnp.int32, (8, H), 0)
        n16 = SCAN_TR // 16

        def group(jj, carry):
            cr, ci = carry
            r0 = pl.multiple_of((n16 - 1 - jj) * 16, 16)
            hr, hi = [None, None], [None, None]
            for half in (1, 0):
                rr = pl.multiple_of(r0 + 8 * half, 8)
                vr, vi = _scan_group(g_ref[pl.ds(rr, 8), :H], g_ref[pl.ds(rr, 8), H:],
                                     cr, ci, tab_ref, True)
                pr, pi = xp_ref[pl.ds(rr, 8), :H], xp_ref[pl.ds(rr, 8), H:]
                da_ref[:, :H] += vr * pr + vi * pi
                da_ref[:, H:] += vi * pr - vr * pi
                cr, ci = _bcast_row(vr, row, 0), _bcast_row(vi, row, 0)
                hr[half], hi[half] = vr, vi
            h_ref[pl.ds(r0, 16), :H] = jnp.concatenate(hr, axis=0).astype(BF16)
            h_ref[pl.ds(r0, 16), H:] = jnp.concatenate(hi, axis=0).astype(BF16)
            return cr, ci

        cr, ci = lax.fori_loop(0, n16, group, (carry_ref[:, :H], carry_ref[:, H:]))
        carry_ref[:, :H] = cr
        carry_ref[:, H:] = ci
        du_ref[...] = _blockdiag_contract(h_ref, b_ref)

    rev = lambda i: (nt - 1 - i, 0)
    whole = lambda a: pl.BlockSpec(a.shape, lambda i: (0, 0))
    return pl.pallas_call(
        body,
        grid=(nt,),
        in_specs=[pl.BlockSpec((SCAN_TR, SSM_W), rev),
                  pl.BlockSpec((SCAN_TR, 2 * H), rev),
                  whole(bcat), whole(ccat), whole(abar)],
        out_specs=[pl.BlockSpec((SCAN_TR, 2 * H), rev),
                   pl.BlockSpec((8, 2 * H), lambda i: (0, 0)),
                   pl.BlockSpec((SCAN_TR, SSM_W), rev)],
        out_shape=[jax.ShapeDtypeStruct((n_rows, 2 * H), BF16),
                   jax.ShapeDtypeStruct((8, 2 * H), F32),
                   jax.ShapeDtypeStruct((n_rows, SSM_W), F32)],
        scratch_shapes=[pltpu.VMEM((SCAN_TR, 2 * H), F32), pltpu.VMEM((4, 8, 2 * H), F32),
                        pltpu.VMEM((8, 2 * H), F32)],
        compiler_params=_cparams(("arbitrary",)),
        name="ssm_scan_bwd",
    )(dy, xprev, bcat, ccat, abar)


def _adamw(parts, w, m, v, *, tr, ch, name, prefetch=None):
    n_rows, cols = w.shape
    n_parts = len(parts)
    c1 = 1.0 - ADAM_B1 ** ADAM_STEP
    c2 = 1.0 - ADAM_B2 ** ADAM_STEP

    def fn(rv, vv, i, nt):
        g = rv[0].astype(F32)
        for p in rv[1:n_parts]:
            g = g + p.astype(F32)
        wv, mv, vval = rv[n_parts:]
        nm = ADAM_B1 * mv + (1.0 - ADAM_B1) * g
        nv = ADAM_B2 * vval + (1.0 - ADAM_B2) * (g * g)
        delta = -ADAM_LR * ((nm / c1) / (jnp.sqrt(nv / c2) + ADAM_EPS) + ADAM_WD * wv)
        return [g, delta, nm, nv], []

    rows = [_row(arr, lead=lead) for (arr, lead) in parts] + [_row(w), _row(m), _row(v)]
    return _rowwise(fn, rows, [], [(cols, F32)] * 4, [], n_rows=n_rows, tr=tr, ch=ch, name=name,
                    prefetch=prefetch)


_PACK = [
    ("b_ada", 6), ("norm1_g", 1), ("b_in", 3), ("norm2_g", 1), ("final_g", 1),
    ("lambda_re", 1), ("lambda_im", 1), ("log_step", 1), ("attn_sinks", 1),
    ("rel_bias", 1), ("b_glu", 1), ("ssm_d", 1), ("loss", 1),
    ("ssm_b_re", 16), ("ssm_b_im", 16), ("ssm_c_re", 16), ("ssm_c_im", 16),
]
_PACK_OFF = {}
_off = 0
for _n, _r in _PACK:
    _PACK_OFF[_n] = (_off, _r)
    _off += _r
PACK_ROWS = -(-_off // 8) * 8


def _to_rows(a, rows):
    flat = a.reshape(-1).astype(F32)
    pad = rows * PACK_W - flat.shape[0]
    if pad:
        flat = jnp.pad(flat, (0, pad))
    return flat.reshape(rows, PACK_W)


def _b_to_rows(b):
    return jnp.transpose(b, (2, 0, 1)).reshape(SSM_P, SSM_H)


def _rows_to_b(r):
    return jnp.transpose(r.reshape(SSM_P, SSM_G, SSM_N), (1, 2, 0))


def _c_to_rows(cm):
    return jnp.transpose(cm, (1, 0, 2)).reshape(SSM_P, SSM_H)


def _rows_to_c(r):
    return jnp.transpose(r.reshape(SSM_P, SSM_G, SSM_N), (1, 0, 2))


def _pack(vals):
    out = jnp.zeros((PACK_ROWS, PACK_W), F32)
    for n, r in _PACK:
        if n in vals:
            pieces = vals[n] if isinstance(vals[n], list) else [vals[n]]
            rows_each = r // len(pieces)
            for i, piece in enumerate(pieces):
                out = lax.dynamic_update_slice(out, _to_rows(piece, rows_each),
                                               (_PACK_OFF[n][0] + i * rows_each, 0))
    return out


def _unpack(packed, name, shape):
    o, r = _PACK_OFF[name]
    n = int(np.prod(shape))
    return packed[o:o + r].reshape(-1)[:n].reshape(shape)


def _small_params_packed(p):
    return {
        "b_ada": p["b_ada"], "norm1_g": p["norm1_g"], "b_in": p["b_in"],
        "norm2_g": p["norm2_g"], "final_g": p["final_g"],
        "lambda_re": p["lambda_re"], "lambda_im": p["lambda_im"],
        "log_step": p["log_step"], "attn_sinks": p["attn_sinks"],
        "rel_bias": p["rel_bias"], "b_glu": p["b_glu"], "ssm_d": p["ssm_d"],
        "ssm_b_re": _b_to_rows(p["ssm_b_re"][0]), "ssm_b_im": _b_to_rows(p["ssm_b_im"][0]),
        "ssm_c_re": _c_to_rows(p["ssm_c_re"][0]), "ssm_c_im": _c_to_rows(p["ssm_c_im"][0]),
    }


_SMALL_SHAPES = {
    "b_ada": (1, N_MOD * D), "norm1_g": (1, D), "b_in": (1, IN_W), "norm2_g": (1, D),
    "final_g": (D,), "lambda_re": (1, SSM_G, SSM_N), "lambda_im": (1, SSM_G, SSM_N),
    "log_step": (1, SSM_G), "attn_sinks": (1, N_Q_HEADS), "rel_bias": (NUM_BUCKETS, N_Q_HEADS),
    "b_glu": (1, SSM_W), "ssm_d": (1, SSM_W),
}


def _unpack_small(packed, name):
    if name in ("ssm_b_re", "ssm_b_im"):
        o, r = _PACK_OFF[name]
        return _rows_to_b(packed[o:o + r])[None]
    if name in ("ssm_c_re", "ssm_c_im"):
        o, r = _PACK_OFF[name]
        return _rows_to_c(packed[o:o + r])[None]
    return _unpack(packed, name, _SMALL_SHAPES[name])


WEIGHT_ORDER = ['w_ada', 'b_ada', 'norm1_g', 'w_in', 'b_in', 'attn_sinks', 'rel_bias', 'lambda_re',
                'lambda_im', 'log_step', 'ssm_b_re', 'ssm_b_im', 'ssm_c_re', 'ssm_c_im', 'ssm_d',
                'w_glu', 'b_glu', 'w_attn_proj', 'w_ssm_proj', 'w_out', 'norm2_g', 'w_ff1', 'w_ff2',
                'final_g']
BIG = ['w_in', 'w_glu', 'w_attn_proj', 'w_ssm_proj', 'w_out', 'w_ff1', 'w_ff2']


ADAMW_TILE_ELEMS = 1 << 18


def _to_col_blocks(w):
    k, n = w.shape
    return jnp.transpose(w.reshape(k, N_DEV, n // N_DEV), (1, 0, 2))


def _adamw_rows(rows, cols):
    tr = rows
    while tr * cols > ADAMW_TILE_ELEMS and tr % 32 == 0:
        tr //= 2
    return tr


def _cast_to_slot(w, me1, name, dep=None):
    rows, cols = w.shape
    tr = min(rows, 256)
    n_dep = 0 if dep is None else 1

    def body(me_ref, w_ref, *rest):
        rest[-1][...] = w_ref[...].astype(BF16)

    return pl.pallas_call(
        body,
        grid_spec=pltpu.PrefetchScalarGridSpec(
            num_scalar_prefetch=1, grid=(rows // tr,),
            in_specs=[pl.BlockSpec((tr, cols), lambda i, me_ref: (i, 0))]
            + [pl.BlockSpec(memory_space=pl.ANY)] * n_dep,
            out_specs=pl.BlockSpec((None, tr, cols), lambda i, me_ref: (me_ref[0], i, 0))),
        out_shape=jax.ShapeDtypeStruct((N_DEV, rows, cols), BF16),
        compiler_params=_cparams(("arbitrary",)),
        name=name,
    )(me1, w, *([dep] if n_dep else []))


def kernel(x, c, w_ada, b_ada, norm1_g, w_in, b_in, attn_sinks, rel_bias, lambda_re, lambda_im, log_step, ssm_b_re, ssm_b_im, ssm_c_re, ssm_c_im, ssm_d, w_glu, b_glu, w_attn_proj, w_ssm_proj, w_out, norm2_g, w_ff1, w_ff2, final_g, loss_target, m_w_ada, m_b_ada, m_norm1_g, m_w_in, m_b_in, m_attn_sinks, m_rel_bias, m_lambda_re, m_lambda_im, m_log_step, m_ssm_b_re, m_ssm_b_im, m_ssm_c_re, m_ssm_c_im, m_ssm_d, m_w_glu, m_b_glu, m_w_attn_proj, m_w_ssm_proj, m_w_out, m_norm2_g, m_w_ff1, m_w_ff2, m_final_g, v_w_ada, v_b_ada, v_norm1_g, v_w_in, v_b_in, v_attn_sinks, v_rel_bias, v_lambda_re, v_lambda_im, v_log_step, v_ssm_b_re, v_ssm_b_im, v_ssm_c_re, v_ssm_c_im, v_ssm_d, v_w_glu, v_b_glu, v_w_attn_proj, v_w_ssm_proj, v_w_out, v_norm2_g, v_w_ff1, v_w_ff2, v_final_g):
    loc = dict(locals())
    W = {n: loc[n] for n in WEIGHT_ORDER}
    Mo = {n: loc["m_" + n] for n in WEIGHT_ORDER}
    Vo = {n: loc["v_" + n] for n in WEIGHT_ORDER}
    S = x.shape[1]
    TM = min(512, S)
    TS = min(1024, S)
    TR = min(256, S)
    TW = min(1024, S)
    TX = min(2048, S)
    me = 4 * lax.axis_index("x") + 2 * lax.axis_index("y") + lax.axis_index("c")
    x2d = x.reshape(S, D)
    tgt = loss_target.reshape(S, D)

    c_all = _small_allgather(c, "allgather_c").reshape(N_DEV, D)
    cs = _rowwise(lambda rv, vv, i, nt: ([rv[0] * _sigmoid(rv[0])], []), [_row(c_all)], [],
                  [(D, F32)], [], n_rows=N_DEV, tr=8, ch=8, name="silu_c")[0]
    n_ada = N_MOD * D // N_DEV
    b_ada_cols = lax.dynamic_slice(b_ada, (0, me * n_ada), (1, n_ada))
    mod_piece = _matmul(cs, w_ada[0], mode="nn", dims=(N_DEV, n_ada, D), tiles=(N_DEV, 512, D),
                        out_dtypes=[F32], name="ada_fwd", bias=b_ada_cols)
    mod_all = _small_allgather(mod_piece, "allgather_mod")
    mod_b = lax.dynamic_index_in_dim(mod_all, me, axis=1, keepdims=False).reshape(N_MOD, D)
    sh1, sc1, g1, sh2, sc2, g2 = [mod_b[i:i + 1] for i in range(N_MOD)]

    shard = {n: W[n][0] for n in BIG}
    me1 = jnp.reshape(me, (1,)).astype(jnp.int32)
    zone = {"w_in": _cast_to_slot(shard["w_in"], me1, "cast_w_in")}
    (in_flight,), tok_in = _relay_gather_start([zone["w_in"]], "w_in_start", mod_all)
    for n in BIG[1:]:
        zone[n] = _cast_to_slot(shard[n], me1, "cast_" + n, dep=tok_in)
    G = {}

    buckets = _t5_buckets_block()
    band = _band_mask()
    onehot_t = jnp.asarray(
        (np.arange(128)[:, None] == buckets.reshape(-1)[None, :]).astype(np.float32), BF16)
    band_first = band & (np.arange(2 * BLK)[None, :] >= BLK)
    rel_bias_t = jnp.pad(jnp.transpose(rel_bias), ((0, 0), (0, 128 - NUM_BUCKETS)))
    bias2 = _bias_tables(rel_bias_t, onehot_t,
                         jnp.asarray(band_first.reshape(1, -1).astype(np.float32)),
                         jnp.asarray(band.reshape(1, -1).astype(np.float32))
                         ).reshape(2, N_Q_HEADS * BLK, 2 * BLK)
    sinkcol = jnp.repeat(attn_sinks.reshape(N_Q_HEADS), BLK).reshape(N_Q_HEADS * BLK, 1)
    lam_re = lambda_re.reshape(1, SSM_H)
    lam_im = lambda_im.reshape(1, SSM_H)
    ls_x = jnp.repeat(log_step.reshape(SSM_G), SSM_N).reshape(1, SSM_H)
    btr, bti = _b_to_rows(ssm_b_re[0]), _b_to_rows(ssm_b_im[0])
    ctr, cti = _c_to_rows(ssm_c_re[0]), _c_to_rows(ssm_c_im[0])
    abar, bcat, ccat = _ssm_setup(lam_re, lam_im, ls_x, btr, bti, ctr, cti)

    def f_norm1(rv, vv, i, nt):
        xv, (g, sc, sh) = rv[0], vv
        return [(xv * _rms(xv) * g) * (1.0 + sc) + sh], []

    h = _rowwise(f_norm1, [_row(x2d)], [norm1_g, sc1, sh1], [(D, BF16)], [],
                 n_rows=S, tr=TR, ch=32, name="norm1_fwd", dep=zone["w_ff2"])[0]
    (zone_in,) = _relay_gather_arrive([in_flight], [h, bias2, bcat, zone["w_ff1"]], "w_in_arrive")
    (in_pass,), tok_p = _relay_pass_start([zone_in], "w_in_pass_start")
    mixer = ["w_attn_proj", "w_glu", "w_ssm_proj", "w_out"]
    later_flights, tok_w = _relay_gather_start(
        [zone[n] for n in mixer] + [zone["w_ff1"], zone["w_ff2"]], "weights_start", tok_p)
    mixer_flights, ff_flights = later_flights[:len(mixer)], later_flights[len(mixer):]
    (G["w_in"],) = _relay_pass_wait([in_pass], tok_w, "w_in_pass_wait")
    proj = _matmul(h, G["w_in"], mode="nn", dims=(S, IN_W, D), tiles=(TX, 768, D),
                   out_dtypes=[BF16], name="in_proj", b3=True, bias=b_in, dep=tok_w)

    attn = _attention_fwd(proj, bias2, sinkcol, S)
    mixer_zones = _relay_gather_arrive(mixer_flights, attn, "mixer_weights_arrive")
    mixer_pass, _ = _relay_pass_start(mixer_zones, "mixer_weights_pass_start")

    u_blk =(ATTN_W + 2 * KV_W) // SSM_W
    xs, xprev, yc = _scan_fwd(proj, u_blk, bcat, ccat, abar, S)

    def f_ssm_out(rv, vv, i, nt):
        y = rv[0] + vv[0] * rv[1]
        return [y, _gelu(y)], []

    y_ssm_pre, z = _rowwise(f_ssm_out, [_row(yc), _row(proj, u_blk, SSM_W)], [ssm_d],
                            [(SSM_W, F32), (SSM_W, BF16)], [], n_rows=S, tr=TM, ch=32, name="ssm_out")
    G.update(zip(mixer, _relay_pass_wait(mixer_pass, z, "mixer_weights_pass_wait")))
    w_glu_f = G["w_glu"].reshape(SSM_W, SSM_W)
    w_out_f = G["w_out"].reshape(D, D)
    w_ap_f = jnp.transpose(G["w_attn_proj"], (1, 0, 2)).reshape(ATTN_W, D)
    w_sp_f = jnp.transpose(G["w_ssm_proj"], (1, 0, 2)).reshape(SSM_W, D)
    y_attn = _matmul(attn, w_ap_f, mode="nn", dims=(S, D, ATTN_W), tiles=(TW, 1024, ATTN_W),
                     out_dtypes=[BF16], name="attn_proj")
    zg = _matmul(z, w_glu_f, mode="nn", dims=(S, SSM_W, SSM_W), tiles=(TM, SSM_W, SSM_W),
                 out_dtypes=[F32], name="glu_proj", bias=b_glu)
    z2 = _rowwise(lambda rv, vv, i, nt: ([rv[0].astype(F32) * _sigmoid(rv[1])], []),
                  [_row(z), _row(zg)], [], [(SSM_W, BF16)], [], n_rows=S, tr=TM, ch=32, name="glu_gate")[0]
    y_ssm = _matmul(z2, w_sp_f, mode="nn", dims=(S, D, SSM_W), tiles=(TW, 1024, SSM_W),
                    out_dtypes=[BF16], name="ssm_proj")

    ga_row = _row(proj, 1, D)
    gs_row = _row(proj, 2, D)

    def f_merge(rv, vv, i, nt):
        ga, gs, ya, ys = rv
        return [_sigmoid(ga) * ya + _sigmoid(gs) * ys], []

    merged = _rowwise(f_merge, [ga_row, gs_row, _row(y_attn), _row(y_ssm)], [], [(D, BF16)], [],
                      n_rows=S, tr=TR, ch=32, name="merge")[0]
    mo = _matmul(merged, w_out_f, mode="nn", dims=(S, D, D), tiles=(TW, 1024, D),
                 out_dtypes=[BF16], name="out_proj")

    ff_zones = _relay_gather_arrive(ff_flights, mo, "ff_weights_arrive")
    ff_pass, tok_fp = _relay_pass_start(ff_zones, "ff_weights_pass_start")

    def f_norm2(rv, vv, i, nt):
        xv, mv = rv
        g1v, g, sc, sh = vv
        x1v = xv + g1v * mv
        return [x1v, (x1v * _rms(x1v) * g) * (1.0 + sc) + sh], []

    x1, h2 = _rowwise(f_norm2, [_row(x2d), _row(mo)], [g1, norm2_g, sc2, sh2],
                      [(D, F32), (D, BF16)], [], n_rows=S, tr=TR, ch=32, name="norm2_fwd", dep=tok_fp)

    def relu_sq(acc):
        r = jnp.maximum(acc, 0.0)
        return r * r, r

    (G["w_ff1"],) = _relay_pass_wait(ff_pass[:1], h2, "w_ff1_pass_wait")
    act, relu = _matmul(h2, G["w_ff1"], mode="nn", dims=(S, D_FF, D), tiles=(TX, 1024, D),
                        out_dtypes=[BF16, BF16], name="ff1", b3=True, epilogue=relu_sq)
    w_ff2_f = _relay_pass_wait(ff_pass[1:], act, "w_ff2_pass_wait")[0].reshape(D_FF, D)
    ff = _matmul(act, w_ff2_f, mode="nn", dims=(S, D, D_FF), tiles=(TX, 1024, 2048),
                 out_dtypes=[BF16], name="ff2")

    def f_loss(rv, vv, i, nt):
        x1v, ffv, tv = rv
        g2v, gf = vv
        x2v = x1v + g2v * ffv
        r = _rms(x2v)
        xh = x2v * r
        diff = xh * gf - tv
        dy = diff * (1.0 / D)
        dxh = dy * gf
        dx2 = r * (dxh - xh * jnp.mean(dxh * xh, axis=-1, keepdims=True))
        return [dx2, dx2 * g2v], [_colsum(0.5 * diff * diff * (1.0 / D)), _colsum(dy * xh),
                                  _colsum(dx2 * ffv)]

    dx2, dff, loss_cols, d_final_g, dg2 = _rowwise(
        f_loss, [_row(x1), _row(ff), _row(tgt)], [g2, final_g.reshape(1, D)],
        [(D, F32), (D, BF16)], [(1, D)] * 3, n_rows=S, tr=TR, ch=32, name="loss_bwd")

    df1 = _matmul(dff, w_ff2_f, mode="nt", dims=(S, D_FF, D), tiles=(TX, 1024, D),
                  out_dtypes=[BF16], name="ff2_dgrad", extras=(relu,),
                  epilogue=lambda acc, r: (acc * (2.0 * r.astype(F32)),))
    gw_ff2 = _matmul(act, dff, mode="tn", dims=(D_FF, D, S), tiles=(2048, 1024, TS),
                     out_dtypes=[BF16], name="ff2_wgrad").reshape(N_DEV, D_FF // N_DEV, D)
    g_flight = {}
    (g_flight["w_ff2"],), tok = _exchange_start([gw_ff2], "scatter", "grads_start_ff2")
    dh2 = _matmul(df1, G["w_ff1"], mode="nt", dims=(S, D, D_FF), tiles=(TW, D, 1024),
                  out_dtypes=[BF16], name="ff1_dgrad", b3=True, dep=tok)
    gw_ff1 = _matmul(h2, df1, mode="tn", dims=(D, D_FF, S), tiles=(2048, 1024, TS),
                     out_dtypes=[BF16], name="ff1_wgrad", out3=True)
    (g_flight["w_ff1"],), tok = _exchange_start([gw_ff1], "scatter", "grads_start_ff1")

    def f_norm2_bwd(rv, vv, i, nt):
        x1v, dh, dx2v, mv = rv
        g, sc, g1v = vv
        r = _rms(x1v)
        xh = x1v * r
        t = xh * g
        dt = dh * (1.0 + sc)
        dxh = dt * g
        dx1 = dx2v + r * (dxh - xh * jnp.mean(dxh * xh, axis=-1, keepdims=True))
        return [dx1, dx1 * g1v], [_colsum(dh), _colsum(dh * t), _colsum(dt * xh), _colsum(dx1 * mv)]

    dx1, dmo, dsh2, dsc2, d_norm2_g, dg1 = _rowwise(
        f_norm2_bwd, [_row(x1), _row(dh2), _row(dx2), _row(mo)], [norm2_g, sc2, g1],
        [(D, F32), (D, BF16)], [(1, D)] * 4, n_rows=S, tr=TR, ch=16, name="norm2_bwd", dep=tok)

    dmerged = _matmul(dmo, w_out_f, mode="nt", dims=(S, D, D), tiles=(TW, 1024, D),
                      out_dtypes=[BF16], name="out_dgrad")
    gw_out = _matmul(merged, dmo, mode="tn", dims=(D, D, S), tiles=(2048, 1024, TS),
                     out_dtypes=[BF16], name="out_wgrad").reshape(N_DEV, D // N_DEV, D)
    (g_flight["w_out"],), tok = _exchange_start([gw_out], "scatter", "grads_start_out")

    def f_merge_bwd(rv, vv, i, nt):
        dm, ga, gs, ya, ys = rv
        sa, ss = _sigmoid(ga), _sigmoid(gs)
        return [dm * sa, dm * ss, dm * ya * sa * (1.0 - sa), dm * ys * ss * (1.0 - ss)], []

    dy_attn, dy_ssm, dga, dgs = _rowwise(
        f_merge_bwd, [_row(dmerged), ga_row, gs_row, _row(y_attn), _row(y_ssm)], [],
        [(D, BF16)] * 4, [], n_rows=S, tr=TR, ch=16, name="merge_bwd", dep=tok)

    dz2 = _matmul(dy_ssm, w_sp_f, mode="nt", dims=(S, SSM_W, D), tiles=(TW, SSM_W, D),
                  out_dtypes=[F32], name="ssm_proj_dgrad")
    gw_ssm_proj = _to_col_blocks(_matmul(z2, dy_ssm, mode="tn", dims=(SSM_W, D, S), tiles=(SSM_W, 1024, TS),
                                         out_dtypes=[BF16], name="ssm_proj_wgrad"))

    def f_glu_bwd(rv, vv, i, nt):
        dz2v, zv, zgv = rv
        sg = _sigmoid(zgv)
        dzg = dz2v * zv.astype(F32) * sg * (1.0 - sg)
        return [dzg, dz2v * sg], [_colsum(dzg)]

    dzg, dz_a, d_b_glu = _rowwise(f_glu_bwd, [_row(dz2), _row(z), _row(zg)], [],
                                  [(SSM_W, BF16), (SSM_W, F32)], [(1, SSM_W)],
                                  n_rows=S, tr=TM, ch=32, name="glu_bwd")
    dz_b = _matmul(dzg, w_glu_f, mode="nt", dims=(S, SSM_W, SSM_W), tiles=(TM, SSM_W, SSM_W),
                   out_dtypes=[F32], name="glu_dgrad")
    gw_glu = _matmul(z, dzg, mode="tn", dims=(SSM_W, SSM_W, S), tiles=(SSM_W, SSM_W, TS),
                     out_dtypes=[BF16], name="glu_wgrad").reshape(N_DEV, SSM_W // N_DEV, SSM_W)
    (g_flight["w_ssm_proj"], g_flight["w_glu"]), tok = _exchange_start(
        [gw_ssm_proj, gw_glu], "scatter", "grads_start_ssm")

    def f_ssm_out_bwd(rv, vv, i, nt):
        dza, dzb, yv, uv = rv
        dy = (dza + dzb) * _gelu_grad(yv)
        return [dy, dy * vv[0]], [_colsum(dy * uv)]

    dy_s, du_a, d_ssm_d = _rowwise(
        f_ssm_out_bwd, [_row(dz_a), _row(dz_b), _row(y_ssm_pre), _row(proj, u_blk, SSM_W)], [ssm_d],
        [(SSM_W, BF16), (SSM_W, F32)], [(1, SSM_W)], n_rows=S, tr=TM, ch=32, name="ssm_out_bwd", dep=tok)
    hw = SSM_W // 2
    u_half = (ATTN_W + 2 * KV_W) // hw
    dccat = _matmul(dy_s, xs, mode="tn", dims=(hw, 2 * SSM_H, S), tiles=(hw, 1024, TS),
                    out_dtypes=[F32], name="ssm_c_wgrad", a_index=lambda i, j, k: (k, j % 2))
    hs, dacc, du_b = _scan_bwd(dy_s, xprev, bcat, ccat, abar, S)
    dbcat = _matmul(proj, hs, mode="tn", dims=(hw, 2 * SSM_H, S), tiles=(hw, 1024, TS),
                    out_dtypes=[F32], name="ssm_b_wgrad", a_index=lambda i, j, k: (k, u_half + j % 2))
    grp = np.arange(SSM_H) // SSM_N
    gind = jnp.asarray((grp[:, None] == np.arange(128)[None, :]).astype(np.float32), BF16)
    d_lam_re, d_lam_im, d_ls, d_btr, d_bti, d_ctr, d_cti = _ssm_param_bwd(
        lam_re, lam_im, ls_x, btr, bti, dacc, dbcat, dccat, gind)

    dattn = _matmul(dy_attn, w_ap_f, mode="nt", dims=(S, ATTN_W, D), tiles=(TW, ATTN_W, D),
                    out_dtypes=[BF16], name="attn_proj_dgrad")
    gw_attn_proj = _to_col_blocks(_matmul(attn, dy_attn, mode="tn", dims=(ATTN_W, D, S), tiles=(ATTN_W, 1024, TS),
                                          out_dtypes=[BF16], name="attn_proj_wgrad"))
    (g_flight["w_attn_proj"],), tok = _exchange_start(
        [gw_attn_proj], "scatter", "grads_start_attn")
    dq, dkc, dkp, dvc, dvp, dbias, dsink = _attention_bwd(proj, attn, dattn, bias2, sinkcol, S)
    d_bias_b, d_sinks = _bucket_reduce(dbias.reshape(N_Q_HEADS, BLK * 2 * BLK),
                                       dsink.reshape(N_Q_HEADS, BLK), onehot_t)

    def f_dproj(rv, vv, i, nt):
        dqv, kc, kp, vc, vp, dua, dub, gav, gsv = rv
        keep = (i < nt - 1).astype(F32)
        dp = jnp.concatenate([dqv.astype(F32), kc + keep * kp, vc + keep * vp, dua + dub,
                              gav.astype(F32), gsv.astype(F32)], axis=-1)
        return [dp], [_colsum(dp)]

    dproj, d_b_in = _rowwise(
        f_dproj, [_row(dq), _row(dkc), _row(dkp, shift=1), _row(dvc), _row(dvp, shift=1),
                  _row(du_a), _row(du_b), _row(dga), _row(dgs)], [],
        [(IN_W, BF16)], [(1, IN_W)], n_rows=S, tr=BLK, ch=16, name="dproj", dep=tok)
    gw_in = _matmul(h, dproj, mode="tn", dims=(D, IN_W, S), tiles=(2048, 768, TS),
                    out_dtypes=[BF16], name="in_wgrad", out3=True)
    (g_flight["w_in"],), tok = _exchange_start([gw_in], "scatter", "grads_start_in")
    dh = _matmul(dproj, G["w_in"], mode="nt", dims=(S, D, IN_W), tiles=(TW, D, 768),
                 out_dtypes=[BF16], name="in_dgrad", b3=True, dep=tok)

    def f_norm1_bwd(rv, vv, i, nt):
        xv, dhv, dx1v = rv
        g, sc = vv
        r = _rms(xv)
        xh = xv * r
        t = xh * g
        dt = dhv * (1.0 + sc)
        dxh = dt * g
        dxv = dx1v + r * (dxh - xh * jnp.mean(dxh * xh, axis=-1, keepdims=True))
        return [dxv], [_colsum(dhv), _colsum(dhv * t), _colsum(dt * xh)]

    grad_x, dsh1, dsc1, d_norm1_g = _rowwise(
        f_norm1_bwd, [_row(x2d), _row(dh), _row(dx1)], [norm1_g, sc1],
        [(D, F32)], [(1, D)] * 3, n_rows=S, tr=TR, ch=32, name="norm1_bwd")

    part = _pack({
        "b_ada": [dsh1, dsc1, dg1, dsh2, dsc2, dg2], "norm1_g": d_norm1_g, "b_in": d_b_in, "norm2_g": d_norm2_g,
        "final_g": d_final_g, "lambda_re": d_lam_re, "lambda_im": d_lam_im,
        "log_step": d_ls[0, :SSM_G], "attn_sinks": d_sinks[:, 0],
        "rel_bias": jnp.transpose(d_bias_b[:, :NUM_BUCKETS]), "b_glu": d_b_glu, "ssm_d": d_ssm_d,
        "loss": loss_cols, "ssm_b_re": d_btr, "ssm_b_im": d_bti, "ssm_c_re": d_ctr, "ssm_c_im": d_cti,
    })
    zone_small = lax.dynamic_update_slice(lax.empty((N_DEV, PACK_ROWS, PACK_W), F32), part[None], (me, 0, 0))
    (small_flight,), after = _exchange_start([zone_small], "gather", "small_grads_start")

    big_out = {}
    for n in ["w_ff2", "w_ff1", "w_out", "w_ssm_proj", "w_glu", "w_attn_proj", "w_in"]:
        own, recv = _exchange_wait([g_flight[n]], "scatter", after, "grads_wait_" + n[2:])[0]
        rows, cols = shard[n].shape
        parts = [(own, lambda m: m[0])] + [
            (recv, lambda m, j=j: jnp.where(j >= m[0], j + 1, j)) for j in range(N_DEV - 1)]
        big_out[n] = _adamw(parts, shard[n], Mo[n][0], Vo[n][0], tr=_adamw_rows(rows, cols), ch=16,
                            name="adamw_" + n, prefetch=me1)
        after = big_out[n][0]

    part_all = _exchange_wait([small_flight], "gather", after, "small_grads_wait")[0][0]
    wp, mp, vp = [_pack(_small_params_packed(p)) for p in (W, Mo, Vo)]
    sg, sdelta, sm, sv = _adamw([(part_all, d) for d in range(N_DEV)], wp, mp, vp,
                                tr=PACK_ROWS, ch=8, name="adamw_small")
    lo, _ = _PACK_OFF["loss"]
    loss = jnp.sum(sg[lo])

    o_ada, _ = _PACK_OFF["b_ada"]
    dmod_all = part_all[:, o_ada:o_ada + N_MOD, :].reshape(N_DEV, N_MOD * D)
    dmod_cols = lax.dynamic_slice(dmod_all, (0, me * n_ada), (N_DEV, n_ada))
    gw_ada = _matmul(cs, dmod_cols, mode="tn", dims=(D, n_ada, N_DEV), tiles=(D, 512, N_DEV),
                     out_dtypes=[F32], name="ada_wgrad")
    big_out["w_ada"] = _adamw([(gw_ada, 0)], w_ada[0], m_w_ada[0], v_w_ada[0],
                              tr=_adamw_rows(D, n_ada), ch=16, name="adamw_w_ada")

    def leaf(kind, n):
        if n in big_out:
            return big_out[n][kind][None]
        return _unpack_small((sg, sdelta, sm, sv)[kind], n)

    outs = [loss, grad_x.reshape(1, S, D)]
    for kind in range(4):
        outs.extend(leaf(kind, n) for n in WEIGHT_ORDER)
    return tuple(outs)
```

```python
import functools
import math

import numpy as np
import jax
import jax.numpy as jnp
from jax import lax
from jax.experimental import pallas as pl
from jax.experimental.pallas import tpu as pltpu

F32 = jnp.float32
BF16 = jnp.bfloat16
MESH = pl.DeviceIdType.MESH

N_DEV = 8
D = 2048
HEAD_DIM = 64
N_Q_HEADS = 16
N_KV_HEADS = 4
GROUP = N_Q_HEADS // N_KV_HEADS
ATTN_W = N_Q_HEADS * HEAD_DIM
KV_W = N_KV_HEADS * HEAD_DIM
BLK = 128
NUM_BUCKETS = 32
MAX_DISTANCE = 128
NEG_INF = -1e30
SSM_W = 512
SSM_P = 16
SSM_G = 32
SSM_N = 64
SSM_H = SSM_G * SSM_N
D_FF = 4 * D
IN_W = ATTN_W + 2 * KV_W + SSM_W + 2 * D
N_MOD = 6
EPS = 1e-6

ADAM_LR = 0.001
ADAM_B1 = 0.9
ADAM_B2 = 0.999
ADAM_EPS = 1e-08
ADAM_WD = 0.01
ADAM_STEP = 10

VMEM_LIMIT = 56 * 1024 * 1024
PACK_W = 2048


def _cparams(sem):
    return pltpu.CompilerParams(dimension_semantics=sem, vmem_limit_bytes=VMEM_LIMIT)


def _matmul(a, b, *, mode, dims, tiles, out_dtypes, name, a_off=0, b3=False,
            out3=False, bias=None, extras=(), epilogue=None, dep=None, a_index=None, b_index=None):
    M, N, K = dims
    tm, tn, tk = tiles
    assert M % tm == 0 and N % tn == 0 and K % tk == 0, (name, dims, tiles)
    gm, gn, gk = M // tm, N // tn, K // tk
    n_extra = len(extras)
    has_bias = bias is not None
    n_out = len(out_dtypes)

    if mode == "nn":
        a_spec = pl.BlockSpec((tm, tk), lambda i, j, k: (i, a_off + k))
        if b3:
            nb = (N // N_DEV) // tn
            assert nb * tn * N_DEV == N
            b_spec = pl.BlockSpec((None, tk, tn), lambda i, j, k: (j // nb, k, j % nb))
        else:
            b_spec = pl.BlockSpec((tk, tn), lambda i, j, k: (k, j))
        dn = (((1,), (0,)), ((), ()))
    elif mode == "nt":
        a_spec = pl.BlockSpec((tm, tk), lambda i, j, k: (i, a_off + k))
        if b3:
            nb = (K // N_DEV) // tk
            assert nb * tk * N_DEV == K
            b_spec = pl.BlockSpec((None, tn, tk), lambda i, j, k: (k // nb, j, k % nb))
        else:
            b_spec = pl.BlockSpec((tn, tk), lambda i, j, k: (j, k))
        dn = (((1,), (1,)), ((), ()))
    else:
        a_spec = pl.BlockSpec((tk, tm), lambda i, j, k: (k, a_off + i))
        b_spec = pl.BlockSpec((tk, tn), lambda i, j, k: (k, j))
        dn = (((0,), (0,)), ((), ()))
    if a_index is not None:
        a_spec = pl.BlockSpec(a_spec.block_shape, a_index)
    if b_index is not None:
        b_spec = pl.BlockSpec(b_spec.block_shape, b_index)

    if out3:
        nbo = (N // N_DEV) // tn
        assert nbo * tn * N_DEV == N
        o_spec = pl.BlockSpec((None, tm, tn), lambda i, j, k: (j // nbo, i, j % nbo))
        o_shape = (N_DEV, M, N // N_DEV)
    else:
        o_spec = pl.BlockSpec((tm, tn), lambda i, j, k: (i, j))
        o_shape = (M, N)

    in_specs = [a_spec, b_spec]
    args = [a, b]
    if has_bias:
        in_specs.append(pl.BlockSpec((1, tn), lambda i, j, k: (0, j)))
        args.append(bias)
    for e in extras:
        in_specs.append(pl.BlockSpec((tm, tn), lambda i, j, k: (i, j)))
        args.append(e)
    n_dep = 0 if dep is None else 1
    if n_dep:
        in_specs.append(pl.BlockSpec(memory_space=pl.ANY))
        args.append(dep)

    def body(*refs):
        a_ref, b_ref = refs[0], refs[1]
        pos = 2
        bias_ref = None
        if has_bias:
            bias_ref = refs[pos]
            pos += 1
        extra_refs = refs[pos:pos + n_extra]
        pos += n_extra + n_dep
        out_refs = refs[pos:pos + n_out]
        acc_ref = refs[pos + n_out] if gk > 1 else None

        part = lax.dot_general(a_ref[...].astype(BF16), b_ref[...].astype(BF16), dn,
                               preferred_element_type=F32)

        def finish(acc):
            if has_bias:
                acc = acc + bias_ref[...]
            if epilogue is None:
                vals = (acc,)
            else:
                vals = epilogue(acc, *[e[...] for e in extra_refs])
            for o_ref, val in zip(out_refs, vals):
                o_ref[...] = val.astype(o_ref.dtype)

        if gk == 1:
            finish(part)
        else:
            k = pl.program_id(2)

            @pl.when(k == 0)
            def _():
                acc_ref[...] = part

            @pl.when(k > 0)
            def _():
                acc_ref[...] += part

            @pl.when(k == gk - 1)
            def _():
                finish(acc_ref[...])

    outs = pl.pallas_call(
        body,
        grid=(gm, gn, gk),
        in_specs=in_specs,
        out_specs=[o_spec] * n_out,
        out_shape=[jax.ShapeDtypeStruct(o_shape, dt) for dt in out_dtypes],
        scratch_shapes=([pltpu.VMEM((tm, tn), F32)] if gk > 1 else []),
        compiler_params=_cparams(("parallel", "parallel", "arbitrary")),
        name=name,
    )(*args)
    return outs[0] if n_out == 1 else outs


def _rowwise(fn, rows, vecs, row_outs, sum_outs, *, n_rows, tr, ch, name, dep=None, prefetch=None):
    assert n_rows % tr == 0 and tr % ch == 0
    nt = n_rows // tr
    nr, nv, nro, nso = len(rows), len(vecs), len(row_outs), len(sum_outs)
    in_specs, args = [], []
    n_pf = 0 if prefetch is None else 1
    for (arr, lead, cblk, w, shift) in rows:
        if shift:
            ridx = lambda i, shift=shift: jnp.minimum(i + shift, nt - 1)
        else:
            ridx = lambda i: i
        if arr.ndim == 3:
            def imap(i, *pf, lead=lead, cblk=cblk, ridx=ridx):
                return (lead(pf[0]) if callable(lead) else lead, ridx(i), cblk)
            in_specs.append(pl.BlockSpec((None, tr, w), imap))
        else:
            in_specs.append(pl.BlockSpec(
                (tr, w), lambda i, *pf, cblk=cblk, ridx=ridx: (ridx(i), cblk)))
        args.append(arr)
    for v in vecs:
        in_specs.append(pl.BlockSpec(v.shape, lambda i, *pf, nd=v.ndim: (0,) * nd))
        args.append(v)
    n_dep = 0 if dep is None else 1
    if n_dep:
        in_specs.append(pl.BlockSpec(memory_space=pl.ANY))
        args.append(dep)
    out_specs = [pl.BlockSpec((tr, w), lambda i, *pf: (i, 0)) for (w, _) in row_outs]
    out_shape = [jax.ShapeDtypeStruct((n_rows, w), dt) for (w, dt) in row_outs]
    for (r, w) in sum_outs:
        out_specs.append(pl.BlockSpec((r, w), lambda i, *pf: (0, 0)))
        out_shape.append(jax.ShapeDtypeStruct((r, w), F32))

    def body(*refs):
        refs = refs[n_pf:]
        i = pl.program_id(0)
        r_in = refs[:nr]
        v_in = refs[nr:nr + nv]
        r_out = refs[nr + nv + n_dep:nr + nv + n_dep + nro]
        s_out = refs[nr + nv + n_dep + nro:]
        s_out, s_acc = s_out[:nso], s_out[nso:]
        if nso:
            @pl.when(i == 0)
            def _():
                for s in s_acc:
                    s[...] = jnp.zeros(s.shape, F32)
        vvals = [v[...] for v in v_in]

        def chunk(ci, carry):
            r0 = pl.multiple_of(ci * ch, ch)
            rv = [r[pl.ds(r0, ch), :].astype(F32) for r in r_in]
            pieces = [fn([v[8 * k:8 * (k + 1)] for v in rv], vvals, i, nt) for k in range(ch // 8)]
            for j, ref in enumerate(r_out):
                val = jnp.concatenate([ro[j] for ro, _ in pieces], axis=0) if ch > 8 else pieces[0][0][j]
                ref[pl.ds(r0, ch), :] = val.astype(ref.dtype)
            for j, ref in enumerate(s_acc):
                ref[...] += functools.reduce(lambda a, b: a + b, [so[j] for _, so in pieces])
            return carry

        lax.fori_loop(0, tr // ch, chunk, 0)
        if nso:
            @pl.when(i == nt - 1)
            def _():
                for s, acc in zip(s_out, s_acc):
                    s[...] = jnp.sum(acc[...], axis=0, keepdims=True)

    outs = pl.pallas_call(
        body,
        grid_spec=pltpu.PrefetchScalarGridSpec(
            num_scalar_prefetch=n_pf, grid=(nt,), in_specs=in_specs, out_specs=out_specs,
            scratch_shapes=[pltpu.VMEM((8, w), F32) for (_, w) in sum_outs]),
        out_shape=out_shape,
        compiler_params=_cparams(("arbitrary",)),
        name=name,
    )(*([prefetch] if n_pf else []), *args)
    return outs


def _row(arr, cblk=0, w=None, lead=0, shift=0):
    return (arr, lead, cblk, arr.shape[-1] if w is None else w, shift)


def _colsum(v):
    parts = [v[8 * k:8 * (k + 1)] for k in range(v.shape[0] // 8)]
    return functools.reduce(lambda a, b: a + b, parts)


def _rms(x):
    return lax.rsqrt(jnp.mean(x * x, axis=-1, keepdims=True) + EPS)


def _sigmoid(x):
    return 1.0 / (1.0 + jnp.exp(-x))


_GELU_C = math.sqrt(2.0 / math.pi)


def _gelu(x):
    return 0.5 * x * (1.0 + jnp.tanh(_GELU_C * (x + 0.044715 * (x * x * x))))


def _gelu_grad(x):
    t = jnp.tanh(_GELU_C * (x + 0.044715 * (x * x * x)))
    return 0.5 * (1.0 + t) + 0.5 * x * (1.0 - t * t) * (_GELU_C * (1.0 + 3.0 * 0.044715 * (x * x)))


def _my_pos():
    return lax.axis_index("x"), lax.axis_index("y"), lax.axis_index("c")


def _flip(pos, k):
    x, y, c = pos
    return (1 - x if k & 4 else x, 1 - y if k & 2 else y, 1 - c if k & 1 else c)


def _dev_id(pos):
    return 4 * pos[0] + 2 * pos[1] + pos[2]


def _small_allgather(x, name):
    r, c = x.shape

    def body(x_ref, out_ref, send_sems, recv_sems):
        me = _my_pos()
        out_ref[_dev_id(me)] = x_ref[...]
        copies = []
        for k in range(1, N_DEV):
            cp = pltpu.make_async_remote_copy(
                src_ref=x_ref, dst_ref=out_ref.at[_dev_id(me)],
                send_sem=send_sems.at[k - 1], recv_sem=recv_sems.at[k - 1],
                device_id=_flip(me, k), device_id_type=MESH)
            cp.start()
            copies.append(cp)
        for k in range(1, N_DEV):
            peer = _flip(me, k)
            pltpu.make_async_remote_copy(
                src_ref=x_ref, dst_ref=out_ref.at[_dev_id(peer)],
                send_sem=send_sems.at[k - 1], recv_sem=recv_sems.at[k - 1],
                device_id=peer, device_id_type=MESH).wait_recv()
        for cp in copies:
            cp.wait_send()

    return pl.pallas_call(
        body,
        out_shape=jax.ShapeDtypeStruct((N_DEV, r, c), x.dtype),
        in_specs=[pl.BlockSpec(memory_space=pltpu.VMEM)],
        out_specs=pl.BlockSpec(memory_space=pltpu.VMEM),
        scratch_shapes=[pltpu.SemaphoreType.DMA((N_DEV - 1,)),
                        pltpu.SemaphoreType.DMA((N_DEV - 1,))],
        compiler_params=pltpu.CompilerParams(vmem_limit_bytes=VMEM_LIMIT),
        name=name,
    )(x)


_HBM = pl.BlockSpec(memory_space=pltpu.HBM)
_SEM = pl.BlockSpec(memory_space=pltpu.SEMAPHORE)
_EFFECT = pltpu.SideEffectType.DATAFLOW_SIDE_EFFECTING


def _relay_copy(zone, send_sems, recv_sems, k, block, to):
    slot = zone.at[_dev_id(block)]
    return pltpu.make_async_remote_copy(
        src_ref=slot, dst_ref=slot, send_sem=send_sems.at[k], recv_sem=recv_sems.at[k],
        device_id=to, device_id_type=MESH)


def _relay_peers():
    x, y, c = _my_pos()
    return (x, y, c), (x, y, 1 - c), [(1 - x, y), (x, 1 - y), (1 - x, 1 - y)]


def _relay_start_call(zones, n_sems, issue, name, after=None):
    n = len(zones)
    n_after = 0 if after is None else 1

    def body(*refs):
        refs = refs[:n] + refs[n + n_after:]
        send, recv, token = refs[n:2 * n], refs[2 * n:3 * n], refs[4 * n]
        for a in range(n):
            issue(refs[a], send[a], recv[a])
        token[...] = jnp.zeros(token.shape, token.dtype)

    sem = pltpu.SemaphoreType.DMA((n_sems,))
    outs = pl.pallas_call(
        body,
        name=name,
        out_shape=([sem] * (2 * n) + [pltpu.HBM(z.shape, z.dtype) for z in zones]
                   + [jax.ShapeDtypeStruct((8, 128), F32)]),
        in_specs=[_HBM] * n + [pl.BlockSpec(memory_space=pl.ANY)] * n_after,
        out_specs=[_SEM] * (2 * n) + [_HBM] * n + [pl.BlockSpec(memory_space=pltpu.VMEM)],
        input_output_aliases={a: 2 * n + a for a in range(n)},
        compiler_params=pltpu.CompilerParams(has_side_effects=_EFFECT),
    )(*[pltpu.with_memory_space_constraint(z, pltpu.HBM) for z in zones],
      *([after] if n_after else []))
    return [(outs[a], outs[n + a], outs[2 * n + a]) for a in range(n)], outs[3 * n]


def _relay_wait_call(flights, settle, after, name):
    n = len(flights)
    after = list(after) if isinstance(after, (list, tuple)) else [after]

    def body(*refs):
        send, recv = refs[n:2 * n], refs[2 * n:3 * n]
        for a in range(n):
            settle(refs[a], send[a], recv[a])

    outs = pl.pallas_call(
        body,
        name=name,
        out_shape=[pltpu.HBM(f[2].shape, f[2].dtype) for f in flights],
        in_specs=[_HBM] * n + [_SEM] * (2 * n) + [pl.BlockSpec(memory_space=pl.ANY)] * len(after),
        out_specs=[_HBM] * n,
        input_output_aliases={a: a for a in range(n)},
        compiler_params=pltpu.CompilerParams(has_side_effects=_EFFECT),
    )(*[f[2] for f in flights], *[f[0] for f in flights], *[f[1] for f in flights], *after)
    return list(outs)


def _relay_gather_start(zones, name, after=None):
    def issue(zone, send, recv):
        me, sib, chips = _relay_peers()
        _relay_copy(zone, send, recv, 0, me, sib).start()
        for j, chip in enumerate(chips):
            _relay_copy(zone, send, recv, 1 + j, me, (*chip, me[2])).start()
    return _relay_start_call(zones, 4, issue, name, after)


def _relay_gather_arrive(flights, after, name):
    def settle(zone, send, recv):
        me, sib, chips = _relay_peers()
        _relay_copy(zone, send, recv, 0, sib, me).wait_recv()
        _relay_copy(zone, send, recv, 0, me, sib).wait_send()
        for j, chip in enumerate(chips):
            _relay_copy(zone, send, recv, 1 + j, (*chip, me[2]), me).wait_recv()
            _relay_copy(zone, send, recv, 1 + j, me, (*chip, me[2])).wait_send()
    return _relay_wait_call(flights, settle, after, name)


def _relay_pass_start(zones, name, after=None):
    def issue(zone, send, recv):
        me, sib, chips = _relay_peers()
        for j, chip in enumerate(chips):
            _relay_copy(zone, send, recv, j, (*chip, me[2]), sib).start()
    return _relay_start_call(zones, 3, issue, name, after)


def _relay_pass_wait(flights, after, name):
    def settle(zone, send, recv):
        me, sib, chips = _relay_peers()
        for j, chip in enumerate(chips):
            _relay_copy(zone, send, recv, j, (*chip, sib[2]), me).wait_recv()
            _relay_copy(zone, send, recv, j, (*chip, me[2]), sib).wait_send()
    return _relay_wait_call(flights, settle, after, name)


def _exchange_copy(kind, bufs, send_sems, recv_sems, me, k, arriving):
    peer = _flip(me, k)
    my_id, peer_id = _dev_id(me), _dev_id(peer)
    if kind == "gather":
        slot = bufs[0].at[peer_id if arriving else my_id]
        src, dst = slot, slot
    else:
        src = bufs[0].at[my_id if arriving else peer_id]
        dst = bufs[1].at[peer_id if arriving else my_id]
    return pltpu.make_async_remote_copy(
        src_ref=src, dst_ref=dst, send_sem=send_sems.at[k - 1], recv_sem=recv_sems.at[k - 1],
        device_id=peer, device_id_type=MESH)


def _exchange_start(arrays, kind, name, after=None):
    n = len(arrays)
    n_after = 0 if after is None else 1
    if kind == "gather":
        bufs = [[a] for a in arrays]
    else:
        bufs = [[a, lax.empty(a.shape, a.dtype)] for a in arrays]
    nb = len(bufs[0])
    flat = [b for group in bufs for b in group]

    def body(*refs):
        outs_at = nb * n + n_after
        send = refs[outs_at:outs_at + n]
        recv = refs[outs_at + n:outs_at + 2 * n]
        token = refs[outs_at + 2 * n + nb * n]
        me = _my_pos()
        for a in range(n):
            for k in range(1, N_DEV):
                _exchange_copy(kind, refs[nb * a:nb * (a + 1)], send[a], recv[a], me, k, False).start()
        token[...] = jnp.zeros(token.shape, token.dtype)

    sem = pltpu.SemaphoreType.DMA((N_DEV - 1,))
    outs = pl.pallas_call(
        body,
        name=name,
        out_shape=([sem] * (2 * n) + [pltpu.HBM(b.shape, b.dtype) for b in flat]
                   + [jax.ShapeDtypeStruct((8, 128), F32)]),
        in_specs=[_HBM] * (nb * n) + [pl.BlockSpec(memory_space=pl.ANY)] * n_after,
        out_specs=[_SEM] * (2 * n) + [_HBM] * (nb * n) + [pl.BlockSpec(memory_space=pltpu.VMEM)],
        input_output_aliases={i: 2 * n + i for i in range(nb * n)},
        compiler_params=pltpu.CompilerParams(has_side_effects=_EFFECT),
    )(*[pltpu.with_memory_space_constraint(b, pltpu.HBM) for b in flat],
      *([after] if n_after else []))
    flights = [(outs[a], outs[n + a], list(outs[2 * n + nb * a:2 * n + nb * (a + 1)]))
               for a in range(n)]
    return flights, outs[2 * n + nb * n]


def _exchange_wait(flights, kind, after, name):
    n = len(flights)
    nb = len(flights[0][2])
    flat = [b for f in flights for b in f[2]]

    def body(*refs):
        send = refs[nb * n:nb * n + n]
        recv = refs[nb * n + n:nb * n + 2 * n]
        me = _my_pos()
        for a in range(n):
            for k in range(1, N_DEV):
                bufs = refs[nb * a:nb * (a + 1)]
                _exchange_copy(kind, bufs, send[a], recv[a], me, k, False).wait_send()
                _exchange_copy(kind, bufs, send[a], recv[a], me, k, True).wait_recv()

    outs = pl.pallas_call(
        body,
        name=name,
        out_shape=[pltpu.HBM(b.shape, b.dtype) for b in flat],
        in_specs=[_HBM] * (nb * n) + [_SEM] * (2 * n) + [pl.BlockSpec(memory_space=pl.ANY)],
        out_specs=[_HBM] * (nb * n),
        input_output_aliases={i: i for i in range(nb * n)},
        compiler_params=pltpu.CompilerParams(has_side_effects=_EFFECT),
    )(*flat, *[f[0] for f in flights], *[f[1] for f in flights], after)
    return [list(outs[nb * a:nb * (a + 1)]) for a in range(n)]


def _t5_buckets_block():
    qi = np.arange(BLK)[:, None]
    ki = np.arange(2 * BLK)[None, :]
    n = np.maximum(qi + BLK - ki, 0)
    max_exact = NUM_BUCKETS // 2
    large = max_exact + (np.log(np.maximum(n, 1) / max_exact)
                         / np.log(MAX_DISTANCE / max_exact)
                         * (NUM_BUCKETS - max_exact)).astype(np.int32)
    large = np.minimum(large, NUM_BUCKETS - 1)
    return np.where(n < max_exact, n, large).astype(np.int32)


def _band_mask():
    qi = np.arange(BLK)[:, None]
    ki = np.arange(2 * BLK)[None, :]
    dist = qi + BLK - ki
    return (dist >= 0) & (dist < BLK)


def _attn_scores(q_ref, kp_ref, kc_ref, hkv):
    c0 = hkv * HEAD_DIM
    kk = jnp.concatenate([kp_ref[:, c0:c0 + HEAD_DIM], kc_ref[:, c0:c0 + HEAD_DIM]],
                         axis=0).astype(BF16)
    qg = jnp.concatenate(
        [q_ref[:, (hkv * GROUP + g) * HEAD_DIM:(hkv * GROUP + g + 1) * HEAD_DIM]
         for g in range(GROUP)], axis=0).astype(BF16)
    s = lax.dot_general(qg, kk, (((1,), (1,)), ((), ())), preferred_element_type=F32)
    return qg, kk, s


def _attn_softmax(s, bias_ref, sink_ref, hkv):
    r0, r1 = hkv * GROUP * BLK, (hkv + 1) * GROUP * BLK
    s = s * (HEAD_DIM ** -0.5) + bias_ref[r0:r1, :]
    sink = sink_ref[r0:r1, :]
    m = jnp.maximum(jnp.max(s, axis=-1, keepdims=True), sink)
    p = jnp.exp(s - m)
    e_sink = jnp.exp(sink - m)
    inv = 1.0 / (jnp.sum(p, axis=-1, keepdims=True) + e_sink)
    return p * inv, e_sink * inv


def _kv_rows(p_ref, c_ref, hkv):
    c0 = hkv * HEAD_DIM
    return jnp.concatenate([p_ref[:, c0:c0 + HEAD_DIM], c_ref[:, c0:c0 + HEAD_DIM]],
                           axis=0).astype(BF16)


ATT_Q_FWD = 4
ATT_Q_BWD = 2


def _attn_in_specs(bias2, nq):
    prev = lambda n: jnp.maximum(nq * n - 1, 0)
    kcol = ATTN_W // KV_W
    return [
        pl.BlockSpec((nq * BLK, ATTN_W), lambda n: (n, 0)),
        pl.BlockSpec((BLK, KV_W), lambda n: (prev(n), kcol)),
        pl.BlockSpec((nq * BLK, KV_W), lambda n: (n, kcol)),
        pl.BlockSpec((BLK, KV_W), lambda n: (prev(n), kcol + 1)),
        pl.BlockSpec((nq * BLK, KV_W), lambda n: (n, kcol + 1)),
        pl.BlockSpec(bias2.shape, lambda n: (0, 0, 0)),
    ]


def _attn_views(t, q_ref, kp_ref, kc_ref, vp_ref, vc_ref, bias_ref):
    rows = pl.ds(t * BLK, BLK)
    before = pl.ds((t - 1) * BLK, BLK)
    table = jnp.minimum(pl.program_id(0), 1) if t == 0 else 1
    return (q_ref.at[rows, :],
            kp_ref if t == 0 else kc_ref.at[before, :], kc_ref.at[rows, :],
            vp_ref if t == 0 else vc_ref.at[before, :], vc_ref.at[rows, :],
            bias_ref.at[table])


def _attention_fwd(proj, bias2, sinkcol, n_rows):
    nq = min(ATT_Q_FWD, n_rows // BLK)
    steps = n_rows // (nq * BLK)

    def body(q_ref, kp_ref, kc_ref, vp_ref, vc_ref, bias_ref, sink_ref, o_ref):
        views = [_attn_views(t, q_ref, kp_ref, kc_ref, vp_ref, vc_ref, bias_ref) for t in range(nq)]
        work = [(t, hkv) for t in range(nq) for hkv in range(N_KV_HEADS)]
        scores = {w: _attn_scores(views[w[0]][0], views[w[0]][1], views[w[0]][2], w[1])[2] for w in work}
        probs = {w: _attn_softmax(scores[w], views[w[0]][5], sink_ref, w[1])[0] for w in work}
        outs = {w: jnp.dot(probs[w].astype(BF16), _kv_rows(views[w[0]][3], views[w[0]][4], w[1]),
                           preferred_element_type=F32) for w in work}
        for t, hkv in work:
            for g in range(GROUP):
                h = hkv * GROUP + g
                o_ref[t * BLK:(t + 1) * BLK, h * HEAD_DIM:(h + 1) * HEAD_DIM] = (
                    outs[t, hkv][g * BLK:(g + 1) * BLK, :].astype(o_ref.dtype))

    return pl.pallas_call(
        body,
        grid=(steps,),
        in_specs=_attn_in_specs(bias2, nq) + [pl.BlockSpec(sinkcol.shape, lambda n: (0, 0))],
        out_specs=pl.BlockSpec((nq * BLK, ATTN_W), lambda n: (n, 0)),
        out_shape=jax.ShapeDtypeStruct((n_rows, ATTN_W), BF16),
        compiler_params=_cparams(("parallel",)),
        name="attn_fwd",
    )(proj, proj, proj, proj, proj, bias2, sinkcol)


def _attention_bwd(proj, attn, dattn, bias2, sinkcol, n_rows):
    nq = min(ATT_Q_BWD, n_rows // BLK)
    steps = n_rows // (nq * BLK)
    scale = HEAD_DIM ** -0.5
    dn_t = (((0,), (0,)), ((), ()))

    def body(q_ref, kp_ref, kc_ref, vp_ref, vc_ref, bias_ref, o_ref, do_ref, sink_ref,
             dq_ref, dkc_ref, dkp_ref, dvc_ref, dvp_ref, dbias_ref, dsink_ref):
        @pl.when(pl.program_id(0) == 0)
        def _():
            dbias_ref[...] = jnp.zeros(dbias_ref.shape, F32)
            dsink_ref[...] = jnp.zeros(dsink_ref.shape, F32)

        views = [_attn_views(t, q_ref, kp_ref, kc_ref, vp_ref, vc_ref, bias_ref) for t in range(nq)]
        work = [(t, hkv) for t in range(nq) for hkv in range(N_KV_HEADS)]
        qk = {w: _attn_scores(views[w[0]][0], views[w[0]][1], views[w[0]][2], w[1]) for w in work}
        dog, dps, deltas = {}, {}, {}
        for t, hkv in work:
            rows = slice(t * BLK, (t + 1) * BLK)
            hs = [hkv * GROUP + g for g in range(GROUP)]
            d_o = jnp.concatenate([do_ref[rows, h * HEAD_DIM:(h + 1) * HEAD_DIM] for h in hs], axis=0)
            o = jnp.concatenate([o_ref[rows, h * HEAD_DIM:(h + 1) * HEAD_DIM] for h in hs], axis=0)
            deltas[t, hkv] = jnp.sum(d_o.astype(F32) * o.astype(F32), axis=-1, keepdims=True)
            dog[t, hkv] = d_o.astype(BF16)
            dps[t, hkv] = lax.dot_general(dog[t, hkv], _kv_rows(views[t][3], views[t][4], hkv),
                                          (((1,), (1,)), ((), ())), preferred_element_type=F32)
        p16, ds16 = {}, {}
        for t, hkv in work:
            r0, r1 = hkv * GROUP * BLK, (hkv + 1) * GROUP * BLK
            p, p_sink = _attn_softmax(qk[t, hkv][2], views[t][5], sink_ref, hkv)
            ds = p * (dps[t, hkv] - deltas[t, hkv])
            dbias_ref[r0:r1, :] += ds
            dsink_ref[r0:r1, :] += -(p_sink * deltas[t, hkv])
            p16[t, hkv] = p.astype(BF16)
            ds16[t, hkv] = ds.astype(BF16)
        for t, hkv in work:
            rows = slice(t * BLK, (t + 1) * BLK)
            c0 = hkv * HEAD_DIM
            qg, kk, _ = qk[t, hkv]
            dqg = jnp.dot(ds16[t, hkv], kk, preferred_element_type=F32) * scale
            dkk = lax.dot_general(ds16[t, hkv], qg, dn_t, preferred_element_type=F32) * scale
            dvv = lax.dot_general(p16[t, hkv], dog[t, hkv], dn_t, preferred_element_type=F32)
            for g in range(GROUP):
                h = hkv * GROUP + g
                dq_ref[rows, h * HEAD_DIM:(h + 1) * HEAD_DIM] = (
                    dqg[g * BLK:(g + 1) * BLK, :].astype(dq_ref.dtype))
            dkp_ref[rows, c0:c0 + HEAD_DIM] = dkk[:BLK].astype(dkp_ref.dtype)
            dkc_ref[rows, c0:c0 + HEAD_DIM] = dkk[BLK:].astype(dkc_ref.dtype)
            dvp_ref[rows, c0:c0 + HEAD_DIM] = dvv[:BLK].astype(dvp_ref.dtype)
            dvc_ref[rows, c0:c0 + HEAD_DIM] = dvv[BLK:].astype(dvc_ref.dtype)

    wide = pl.BlockSpec((nq * BLK, ATTN_W), lambda n: (n, 0))
    kv_out = pl.BlockSpec((nq * BLK, KV_W), lambda n: (n, 0))
    kv_shape = jax.ShapeDtypeStruct((n_rows, KV_W), F32)
    acc_shape = bias2.shape[1:]
    return pl.pallas_call(
        body,
        grid=(steps,),
        in_specs=_attn_in_specs(bias2, nq) + [wide, wide, pl.BlockSpec(sinkcol.shape, lambda n: (0, 0))],
        out_specs=[
            wide, kv_out, kv_out, kv_out, kv_out,
            pl.BlockSpec(acc_shape, lambda n: (0, 0)),
            pl.BlockSpec(sinkcol.shape, lambda n: (0, 0)),
        ],
        out_shape=[
            jax.ShapeDtypeStruct((n_rows, ATTN_W), BF16),
            kv_shape, kv_shape, kv_shape, kv_shape,
            jax.ShapeDtypeStruct(acc_shape, F32),
            jax.ShapeDtypeStruct(sinkcol.shape, F32),
        ],
        compiler_params=_cparams(("arbitrary",)),
        name="attn_bwd",
    )(proj, proj, proj, proj, proj, bias2, attn, dattn, sinkcol)


def _bias_tables(rel_bias_t, onehot_t, band_first, band_rest):
    def body(rb_ref, oh_ref, mf_ref, mr_ref, out_ref):
        acc = jnp.zeros((N_Q_HEADS, BLK * 2 * BLK), F32)
        for part in _split3(rb_ref[...]):
            acc = acc + jnp.dot(part, oh_ref[...], preferred_element_type=F32)
        out_ref[0] = jnp.where(mf_ref[...] > 0.0, acc, NEG_INF)
        out_ref[1] = jnp.where(mr_ref[...] > 0.0, acc, NEG_INF)

    return pl.pallas_call(
        body,
        out_shape=jax.ShapeDtypeStruct((2, N_Q_HEADS, BLK * 2 * BLK), F32),
        compiler_params=pltpu.CompilerParams(vmem_limit_bytes=VMEM_LIMIT),
        name="bias_tables",
    )(rel_bias_t, onehot_t, band_first, band_rest)


def _split3(a):
    hi = a.astype(BF16)
    r1 = a - hi.astype(F32)
    mid = r1.astype(BF16)
    lo = (r1 - mid.astype(F32)).astype(BF16)
    return hi, mid, lo


def _bucket_reduce(dbias, dsink, onehot_t):
    def body(db_ref, ds_ref, oh_ref, ob_ref, os_ref):
        acc = jnp.zeros((N_Q_HEADS, 128), F32)
        for part in _split3(db_ref[...]):
            acc = acc + lax.dot_general(part, oh_ref[...], (((1,), (1,)), ((), ())),
                                        preferred_element_type=F32)
        ob_ref[...] = acc
        os_ref[...] = jnp.broadcast_to(jnp.sum(ds_ref[...], axis=-1, keepdims=True),
                                       os_ref.shape)

    return pl.pallas_call(
        body,
        out_shape=[jax.ShapeDtypeStruct((N_Q_HEADS, 128), F32),
                   jax.ShapeDtypeStruct((N_Q_HEADS, 128), F32)],
        compiler_params=pltpu.CompilerParams(vmem_limit_bytes=VMEM_LIMIT),
        name="bias_bucket_reduce",
    )(dbias, dsink, onehot_t)


def _disc(lr, li, ls, btr, bti):
    lam_re = jnp.minimum(lr, -1e-4)
    delta = jnp.exp(ls)
    mag = jnp.exp(lam_re * delta)
    ang = li * delta
    ar, ai = mag * jnp.cos(ang), mag * jnp.sin(ang)
    nr, ni = ar - 1.0, ai
    den = lam_re * lam_re + li * li
    fr = (nr * lam_re + ni * li) / den
    fi = (ni * lam_re - nr * li) / den
    bbr = fr * btr - fi * bti
    bbi = fr * bti + fi * btr
    return ar, ai, bbr, bbi


def _block_mask():
    row = lax.broadcasted_iota(jnp.int32, (SSM_W, SSM_H), 0)
    col = lax.broadcasted_iota(jnp.int32, (SSM_W, SSM_H), 1)
    return (row // SSM_P) == (col // SSM_N)


def _ssm_setup(lr, li, ls, btr, bti, ctr, cti):
    def body(lr_ref, li_ref, ls_ref, btr_ref, bti_ref, ctr_ref, cti_ref, a_ref, b_ref, c_ref):
        ar, ai, bbr, bbi = _disc(lr_ref[...], li_ref[...], ls_ref[...], btr_ref[...], bti_ref[...])
        a_ref[:, :SSM_H] = ar
        a_ref[:, SSM_H:] = ai
        mask = _block_mask()
        blk = lambda t: jnp.where(mask, jnp.tile(t, (SSM_G, 1)), 0.0)
        b_ref[:, :SSM_H] = blk(bbr).astype(BF16)
        b_ref[:, SSM_H:] = blk(bbi).astype(BF16)
        c_ref[:, :SSM_H] = blk(ctr_ref[...]).astype(BF16)
        c_ref[:, SSM_H:] = blk(-cti_ref[...]).astype(BF16)

    return pl.pallas_call(
        body,
        out_shape=[jax.ShapeDtypeStruct((1, 2 * SSM_H), F32),
                   jax.ShapeDtypeStruct((SSM_W, 2 * SSM_H), BF16),
                   jax.ShapeDtypeStruct((SSM_W, 2 * SSM_H), BF16)],
        compiler_params=pltpu.CompilerParams(vmem_limit_bytes=VMEM_LIMIT),
        name="ssm_setup",
    )(lr, li, ls, btr, bti, ctr, cti)


def _ssm_param_bwd(lr, li, ls, btr, bti, dacc, dbcat, dccat, gind):
    def body(lr_ref, li_ref, ls_ref, btr_ref, bti_ref, dacc_ref, db_ref, dc_ref, g_ref,
             dlr_ref, dli_ref, dls_ref, dbtr_ref, dbti_ref, dctr_ref, dcti_ref):
        dar = jnp.sum(dacc_ref[:, :SSM_H], axis=0, keepdims=True)
        dai = jnp.sum(dacc_ref[:, SSM_H:], axis=0, keepdims=True)
        col = lax.broadcasted_iota(jnp.int32, (SSM_P, 2 * SSM_H), 1)
        grp = (col % SSM_H) // SSM_N
        db = jnp.zeros((SSM_P, 2 * SSM_H), F32)
        dc = jnp.zeros((SSM_P, 2 * SSM_H), F32)
        half = SSM_G // 2
        for g in range(SSM_G):
            sel = grp == g
            r0 = (g % half) * SSM_P
            db = db + jnp.where(sel, db_ref[r0:r0 + SSM_P, :], 0.0)
            dc = dc + jnp.where(sel, dc_ref[r0:r0 + SSM_P, :], 0.0)
        dctr_ref[...] = dc[:, :SSM_H]
        dcti_ref[...] = -dc[:, SSM_H:]
        prim = (lr_ref[...], li_ref[...], ls_ref[...], btr_ref[...], bti_ref[...])
        _, vjp = jax.vjp(_disc, *prim)
        dlr, dli, dls, dbtr, dbti = vjp((dar, dai, db[:, :SSM_H], db[:, SSM_H:]))
        dlr_ref[...] = dlr
        dli_ref[...] = dli
        dbtr_ref[...] = dbtr
        dbti_ref[...] = dbti
        acc = jnp.zeros((8, 128), F32)
        for part in _split3(jnp.broadcast_to(dls, (8, SSM_H))):
            acc = acc + jnp.dot(part, g_ref[...], preferred_element_type=F32)
        dls_ref[...] = acc

    vec = jax.ShapeDtypeStruct((1, SSM_H), F32)
    mat = jax.ShapeDtypeStruct((SSM_P, SSM_H), F32)
    return pl.pallas_call(
        body,
        out_shape=[vec, vec, jax.ShapeDtypeStruct((8, 128), F32), mat, mat, mat, mat],
        compiler_params=pltpu.CompilerParams(vmem_limit_bytes=VMEM_LIMIT),
        name="ssm_param_bwd",
    )(lr, li, ls, btr, bti, dacc, dbcat, dccat, gind)


SCAN_TR = 256


def _cmul_add(vr, vi, pr, pi, sr, si):
    return vr + pr * sr - pi * si, vi + pr * si + pi * sr


def _bcast_row(v, row, which):
    return jnp.broadcast_to(v[which:which + 1, :], v.shape)


def _scan_tables(a_ref, tab_ref, reverse):
    H = SSM_H
    ar = jnp.broadcast_to(a_ref[:, :H], (8, H))
    ai = jnp.broadcast_to(a_ref[:, H:], (8, H))
    if reverse:
        ai = -ai
    row = lax.broadcasted_iota(jnp.int32, (8, H), 0)
    pw = [(ar, ai)]
    for _ in range(7):
        cr, ci = pw[-1]
        pw.append((cr * ar - ci * ai, cr * ai + ci * ar))
    pcr = jnp.zeros((8, H), F32)
    pci = jnp.zeros((8, H), F32)
    for e in range(8):
        sel = (row == (7 - e)) if reverse else (row == e)
        pcr = jnp.where(sel, pw[e][0], pcr)
        pci = jnp.where(sel, pw[e][1], pci)
    tab_ref[0, :, :H] = pcr
    tab_ref[0, :, H:] = pci
    for t, k in enumerate((1, 2, 4)):
        keep = (row < 8 - k) if reverse else (row >= k)
        tab_ref[1 + t, :, :H] = jnp.where(keep, pw[k - 1][0], 0.0)
        tab_ref[1 + t, :, H:] = jnp.where(keep, pw[k - 1][1], 0.0)


def _scan_group(vr, vi, cr, ci, tab_ref, reverse):
    H = SSM_H
    for t, k in enumerate((1, 2, 4)):
        sh = 8 - k if reverse else k
        vr, vi = _cmul_add(vr, vi, tab_ref[1 + t, :, :H], tab_ref[1 + t, :, H:],
                           pltpu.roll(vr, sh, 0), pltpu.roll(vi, sh, 0))
    return _cmul_add(vr, vi, tab_ref[0, :, :H], tab_ref[0, :, H:], cr, ci)


def _blockdiag_expand(x, w_ref, out_ref):
    hw, cb = SSM_W // 2, SSM_H // 2
    for j in range(4):
        h = j % 2
        out_ref[:, j * cb:(j + 1) * cb] = jnp.dot(
            x[:, h * hw:(h + 1) * hw], w_ref[h * hw:(h + 1) * hw, j * cb:(j + 1) * cb],
            preferred_element_type=F32)


def _blockdiag_contract(x_ref, w_ref):
    hw, cb = SSM_W // 2, SSM_H // 2
    nt = (((1,), (1,)), ((), ()))
    halves = []
    for h in range(2):
        acc = None
        for j in (h, 2 + h):
            part = lax.dot_general(x_ref[:, j * cb:(j + 1) * cb],
                                   w_ref[h * hw:(h + 1) * hw, j * cb:(j + 1) * cb], nt,
                                   preferred_element_type=F32)
            acc = part if acc is None else acc + part
        halves.append(acc)
    return jnp.concatenate(halves, axis=1)


def _scan_fwd(proj, u_blk, bcat, ccat, abar, n_rows):
    H = SSM_H
    nt = n_rows // SCAN_TR

    def body(u_ref, b_ref, c_ref, a_ref, xs_ref, xp_ref, yc_ref, bu_ref, tab_ref, carry_ref):
        @pl.when(pl.program_id(0) == 0)
        def _():
            _scan_tables(a_ref, tab_ref, False)
            carry_ref[...] = jnp.zeros(carry_ref.shape, F32)

        _blockdiag_expand(u_ref[...].astype(BF16), b_ref, bu_ref)
        row = lax.broadcasted_iota(jnp.int32, (8, H), 0)

        def group(j, carry):
            cr, ci = carry
            r0 = pl.multiple_of(j * 16, 16)
            xr, xi = [], []
            for half in range(2):
                rr = pl.multiple_of(r0 + 8 * half, 8)
                vr, vi = _scan_group(bu_ref[pl.ds(rr, 8), :H], bu_ref[pl.ds(rr, 8), H:],
                                     cr, ci, tab_ref, False)
                xp_ref[pl.ds(rr, 8), :H] = jnp.where(row == 0, cr, pltpu.roll(vr, 1, 0))
                xp_ref[pl.ds(rr, 8), H:] = jnp.where(row == 0, ci, pltpu.roll(vi, 1, 0))
                cr, ci = _bcast_row(vr, row, 7), _bcast_row(vi, row, 7)
                xr.append(vr)
                xi.append(vi)
            xs_ref[pl.ds(r0, 16), :H] = jnp.concatenate(xr, axis=0).astype(BF16)
            xs_ref[pl.ds(r0, 16), H:] = jnp.concatenate(xi, axis=0).astype(BF16)
            return cr, ci

        cr, ci = lax.fori_loop(0, SCAN_TR // 16, group,
                               (carry_ref[:, :H], carry_ref[:, H:]))
        carry_ref[:, :H] = cr
        carry_ref[:, H:] = ci
        yc_ref[...] = _blockdiag_contract(xs_ref, c_ref)

    tile = lambda w: pl.BlockSpec((SCAN_TR, w), lambda i: (i, 0))
    whole = lambda a: pl.BlockSpec(a.shape, lambda i: (0, 0))
    return pl.pallas_call(
        body,
        grid=(nt,),
        in_specs=[pl.BlockSpec((SCAN_TR, SSM_W), lambda i: (i, u_blk)),
                  whole(bcat), whole(ccat), whole(abar)],
        out_specs=[tile(2 * H), tile(2 * H), tile(SSM_W)],
        out_shape=[jax.ShapeDtypeStruct((n_rows, 2 * H), BF16),
                   jax.ShapeDtypeStruct((n_rows, 2 * H), F32),
                   jax.ShapeDtypeStruct((n_rows, SSM_W), F32)],
        scratch_shapes=[pltpu.VMEM((SCAN_TR, 2 * H), F32), pltpu.VMEM((4, 8, 2 * H), F32),
                        pltpu.VMEM((8, 2 * H), F32)],
        compiler_params=_cparams(("arbitrary",)),
        name="ssm_scan_fwd",
    )(proj, bcat, ccat, abar)


def _scan_bwd(dy, xprev, bcat, ccat, abar, n_rows):
    H = SSM_H
    nt = n_rows // SCAN_TR

    def body(dy_ref, xp_ref, b_ref, c_ref, a_ref, h_ref, da_ref, du_ref, g_ref, tab_ref, carry_ref):
        @pl.when(pl.program_id(0) == 0)
        def _():
            _scan_tables(a_ref, tab_ref, True)
            carry_ref[...] = jnp.zeros(carry_ref.shape, F32)
            da_ref[...] = jnp.zeros(da_ref.shape, F32)

        _blockdiag_expand(dy_ref[...], c_ref, g_ref)
        row = lax.broadcasted_iota(jnp.int32, (8, H), 0)
        n16 = SCAN_TR // 16

        def group(jj, carry):
            cr, ci = carry
            r0 = pl.multiple_of((n16 - 1 - jj) * 16, 16)
            hr, hi = [None, None], [None, None]
            for half in (1, 0):
                rr = pl.multiple_of(r0 + 8 * half, 8)
                vr, vi = _scan_group(g_ref[pl.ds(rr, 8), :H], g_ref[pl.ds(rr, 8), H:],
                                     cr, ci, tab_ref, True)
                pr, pi = xp_ref[pl.ds(rr, 8), :H], xp_ref[pl.ds(rr, 8), H:]
                da_ref[:, :H] += vr * pr + vi * pi
                da_ref[:, H:] += vi * pr - vr * pi
                cr, ci = _bcast_row(vr, row, 0), _bcast_row(vi, row, 0)
                hr[half], hi[half] = vr, vi
            h_ref[pl.ds(r0, 16), :H] = jnp.concatenate(hr, axis=0).astype(BF16)
            h_ref[pl.ds(r0, 16), H:] = jnp.concatenate(hi, axis=0).astype(BF16)
            return cr, ci

        cr, ci = lax.fori_loop(0, n16, group, (carry_ref[:, :H], carry_ref[:, H:]))
        carry_ref[:, :H] = cr
        carry_ref[:, H:] = ci
        du_ref[...] = _blockdiag_contract(h_ref, b_ref)

    rev = lambda i: (nt - 1 - i, 0)
    whole = lambda a: pl.BlockSpec(a.shape, lambda i: (0, 0))
    return pl.pallas_call(
        body,
        grid=(nt,),
        in_specs=[pl.BlockSpec((SCAN_TR, SSM_W), rev),
                  pl.BlockSpec((SCAN_TR, 2 * H), rev),
                  whole(bcat), whole(ccat), whole(abar)],
        out_specs=[pl.BlockSpec((SCAN_TR, 2 * H), rev),
                   pl.BlockSpec((8, 2 * H), lambda i: (0, 0)),
                   pl.BlockSpec((SCAN_TR, SSM_W), rev)],
        out_shape=[jax.ShapeDtypeStruct((n_rows, 2 * H), BF16),
                   jax.ShapeDtypeStruct((8, 2 * H), F32),
                   jax.ShapeDtypeStruct((n_rows, SSM_W), F32)],
        scratch_shapes=[pltpu.VMEM((SCAN_TR, 2 * H), F32), pltpu.VMEM((4, 8, 2 * H), F32),
                        pltpu.VMEM((8, 2 * H), F32)],
        compiler_params=_cparams(("arbitrary",)),
        name="ssm_scan_bwd",
    )(dy, xprev, bcat, ccat, abar)


def _adamw(parts, w, m, v, *, tr, ch, name, prefetch=None):
    n_rows, cols = w.shape
    n_parts = len(parts)
    c1 = 1.0 - ADAM_B1 ** ADAM_STEP
    c2 = 1.0 - ADAM_B2 ** ADAM_STEP

    def fn(rv, vv, i, nt):
        g = rv[0].astype(F32)
        for p in rv[1:n_parts]:
            g = g + p.astype(F32)
        wv, mv, vval = rv[n_parts:]
        nm = ADAM_B1 * mv + (1.0 - ADAM_B1) * g
        nv = ADAM_B2 * vval + (1.0 - ADAM_B2) * (g * g)
        delta = -ADAM_LR * ((nm / c1) / (jnp.sqrt(nv / c2) + ADAM_EPS) + ADAM_WD * wv)
        return [g, delta, nm, nv], []

    rows = [_row(arr, lead=lead) for (arr, lead) in parts] + [_row(w), _row(m), _row(v)]
    return _rowwise(fn, rows, [], [(cols, F32)] * 4, [], n_rows=n_rows, tr=tr, ch=ch, name=name,
                    prefetch=prefetch)


_PACK = [
    ("b_ada", 6), ("norm1_g", 1), ("b_in", 3), ("norm2_g", 1), ("final_g", 1),
    ("lambda_re", 1), ("lambda_im", 1), ("log_step", 1), ("attn_sinks", 1),
    ("rel_bias", 1), ("b_glu", 1), ("ssm_d", 1), ("loss", 1),
    ("ssm_b_re", 16), ("ssm_b_im", 16), ("ssm_c_re", 16), ("ssm_c_im", 16),
]
_PACK_OFF = {}
_off = 0
for _n, _r in _PACK:
    _PACK_OFF[_n] = (_off, _r)
    _off += _r
PACK_ROWS = -(-_off // 8) * 8


def _to_rows(a, rows):
    flat = a.reshape(-1).astype(F32)
    pad = rows * PACK_W - flat.shape[0]
    if pad:
        flat = jnp.pad(flat, (0, pad))
    return flat.reshape(rows, PACK_W)


def _b_to_rows(b):
    return jnp.transpose(b, (2, 0, 1)).reshape(SSM_P, SSM_H)


def _rows_to_b(r):
    return jnp.transpose(r.reshape(SSM_P, SSM_G, SSM_N), (1, 2, 0))


def _c_to_rows(cm):
    return jnp.transpose(cm, (1, 0, 2)).reshape(SSM_P, SSM_H)


def _rows_to_c(r):
    return jnp.transpose(r.reshape(SSM_P, SSM_G, SSM_N), (1, 0, 2))


def _pack(vals):
    out = jnp.zeros((PACK_ROWS, PACK_W), F32)
    for n, r in _PACK:
        if n in vals:
            pieces = vals[n] if isinstance(vals[n], list) else [vals[n]]
            rows_each = r // len(pieces)
            for i, piece in enumerate(pieces):
                out = lax.dynamic_update_slice(out, _to_rows(piece, rows_each),
                                               (_PACK_OFF[n][0] + i * rows_each, 0))
    return out


def _unpack(packed, name, shape):
    o, r = _PACK_OFF[name]
    n = int(np.prod(shape))
    return packed[o:o + r].reshape(-1)[:n].reshape(shape)


def _small_params_packed(p):
    return {
        "b_ada": p["b_ada"], "norm1_g": p["norm1_g"], "b_in": p["b_in"],
        "norm2_g": p["norm2_g"], "final_g": p["final_g"],
        "lambda_re": p["lambda_re"], "lambda_im": p["lambda_im"],
        "log_step": p["log_step"], "attn_sinks": p["attn_sinks"],
        "rel_bias": p["rel_bias"], "b_glu": p["b_glu"], "ssm_d": p["ssm_d"],
        "ssm_b_re": _b_to_rows(p["ssm_b_re"][0]), "ssm_b_im": _b_to_rows(p["ssm_b_im"][0]),
        "ssm_c_re": _c_to_rows(p["ssm_c_re"][0]), "ssm_c_im": _c_to_rows(p["ssm_c_im"][0]),
    }


_SMALL_SHAPES = {
    "b_ada": (1, N_MOD * D), "norm1_g": (1, D), "b_in": (1, IN_W), "norm2_g": (1, D),
    "final_g": (D,), "lambda_re": (1, SSM_G, SSM_N), "lambda_im": (1, SSM_G, SSM_N),
    "log_step": (1, SSM_G), "attn_sinks": (1, N_Q_HEADS), "rel_bias": (NUM_BUCKETS, N_Q_HEADS),
    "b_glu": (1, SSM_W), "ssm_d": (1, SSM_W),
}


def _unpack_small(packed, name):
    if name in ("ssm_b_re", "ssm_b_im"):
        o, r = _PACK_OFF[name]
        return _rows_to_b(packed[o:o + r])[None]
    if name in ("ssm_c_re", "ssm_c_im"):
        o, r = _PACK_OFF[name]
        return _rows_to_c(packed[o:o + r])[None]
    return _unpack(packed, name, _SMALL_SHAPES[name])


WEIGHT_ORDER = ['w_ada', 'b_ada', 'norm1_g', 'w_in', 'b_in', 'attn_sinks', 'rel_bias', 'lambda_re',
                'lambda_im', 'log_step', 'ssm_b_re', 'ssm_b_im', 'ssm_c_re', 'ssm_c_im', 'ssm_d',
                'w_glu', 'b_glu', 'w_attn_proj', 'w_ssm_proj', 'w_out', 'norm2_g', 'w_ff1', 'w_ff2',
                'final_g']
BIG = ['w_in', 'w_glu', 'w_attn_proj', 'w_ssm_proj', 'w_out', 'w_ff1', 'w_ff2']


ADAMW_TILE_ELEMS = 1 << 18


def _to_col_blocks(w):
    k, n = w.shape
    return jnp.transpose(w.reshape(k, N_DEV, n // N_DEV), (1, 0, 2))


def _adamw_rows(rows, cols):
    tr = rows
    while tr * cols > ADAMW_TILE_ELEMS and tr % 32 == 0:
        tr //= 2
    return tr


def _cast_to_slot(w, me1, name, dep=None):
    rows, cols = w.shape
    tr = min(rows, 256)
    n_dep = 0 if dep is None else 1

    def body(me_ref, w_ref, *rest):
        rest[-1][...] = w_ref[...].astype(BF16)

    return pl.pallas_call(
        body,
        grid_spec=pltpu.PrefetchScalarGridSpec(
            num_scalar_prefetch=1, grid=(rows // tr,),
            in_specs=[pl.BlockSpec((tr, cols), lambda i, me_ref: (i, 0))]
            + [pl.BlockSpec(memory_space=pl.ANY)] * n_dep,
            out_specs=pl.BlockSpec((None, tr, cols), lambda i, me_ref: (me_ref[0], i, 0))),
        out_shape=jax.ShapeDtypeStruct((N_DEV, rows, cols), BF16),
        compiler_params=_cparams(("arbitrary",)),
        name=name,
    )(me1, w, *([dep] if n_dep else []))


def kernel(x, c, w_ada, b_ada, norm1_g, w_in, b_in, attn_sinks, rel_bias, lambda_re, lambda_im, log_step, ssm_b_re, ssm_b_im, ssm_c_re, ssm_c_im, ssm_d, w_glu, b_glu, w_attn_proj, w_ssm_proj, w_out, norm2_g, w_ff1, w_ff2, final_g, loss_target, m_w_ada, m_b_ada, m_norm1_g, m_w_in, m_b_in, m_attn_sinks, m_rel_bias, m_lambda_re, m_lambda_im, m_log_step, m_ssm_b_re, m_ssm_b_im, m_ssm_c_re, m_ssm_c_im, m_ssm_d, m_w_glu, m_b_glu, m_w_attn_proj, m_w_ssm_proj, m_w_out, m_norm2_g, m_w_ff1, m_w_ff2, m_final_g, v_w_ada, v_b_ada, v_norm1_g, v_w_in, v_b_in, v_attn_sinks, v_rel_bias, v_lambda_re, v_lambda_im, v_log_step, v_ssm_b_re, v_ssm_b_im, v_ssm_c_re, v_ssm_c_im, v_ssm_d, v_w_glu, v_b_glu, v_w_attn_proj, v_w_ssm_proj, v_w_out, v_norm2_g, v_w_ff1, v_w_ff2, v_final_g):
    loc = dict(locals())
    W = {n: loc[n] for n in WEIGHT_ORDER}
    Mo = {n: loc["m_" + n] for n in WEIGHT_ORDER}
    Vo = {n: loc["v_" + n] for n in WEIGHT_ORDER}
    S = x.shape[1]
    TM = min(512, S)
    TS = min(1024, S)
    TR = min(256, S)
    TW = min(1024, S)
    TX = min(2048, S)
    me = 4 * lax.axis_index("x") + 2 * lax.axis_index("y") + lax.axis_index("c")
    x2d = x.reshape(S, D)
    tgt = loss_target.reshape(S, D)

    c_all = _small_allgather(c, "allgather_c").reshape(N_DEV, D)
    cs = _rowwise(lambda rv, vv, i, nt: ([rv[0] * _sigmoid(rv[0])], []), [_row(c_all)], [],
                  [(D, F32)], [], n_rows=N_DEV, tr=8, ch=8, name="silu_c")[0]
    n_ada = N_MOD * D // N_DEV
    b_ada_cols = lax.dynamic_slice(b_ada, (0, me * n_ada), (1, n_ada))
    mod_piece = _matmul(cs, w_ada[0], mode="nn", dims=(N_DEV, n_ada, D), tiles=(N_DEV, 512, D),
                        out_dtypes=[F32], name="ada_fwd", bias=b_ada_cols)
    mod_all = _small_allgather(mod_piece, "allgather_mod")
    mod_b = lax.dynamic_index_in_dim(mod_all, me, axis=1, keepdims=False).reshape(N_MOD, D)
    sh1, sc1, g1, sh2, sc2, g2 = [mod_b[i:i + 1] for i in range(N_MOD)]

    shard = {n: W[n][0] for n in BIG}
    me1 = jnp.reshape(me, (1,)).astype(jnp.int32)
    zone = {"w_in": _cast_to_slot(shard["w_in"], me1, "cast_w_in")}
    (in_flight,), tok_in = _relay_gather_start([zone["w_in"]], "w_in_start", mod_all)
    for n in BIG[1:]:
        zone[n] = _cast_to_slot(shard[n], me1, "cast_" + n, dep=tok_in)
    G = {}

    buckets = _t5_buckets_block()
    band = _band_mask()
    onehot_t = jnp.asarray(
        (np.arange(128)[:, None] == buckets.reshape(-1)[None, :]).astype(np.float32), BF16)
    band_first = band & (np.arange(2 * BLK)[None, :] >= BLK)
    rel_bias_t = jnp.pad(jnp.transpose(rel_bias), ((0, 0), (0, 128 - NUM_BUCKETS)))
    bias2 = _bias_tables(rel_bias_t, onehot_t,
                         jnp.asarray(band_first.reshape(1, -1).astype(np.float32)),
                         jnp.asarray(band.reshape(1, -1).astype(np.float32))
                         ).reshape(2, N_Q_HEADS * BLK, 2 * BLK)
    sinkcol = jnp.repeat(attn_sinks.reshape(N_Q_HEADS), BLK).reshape(N_Q_HEADS * BLK, 1)
    lam_re = lambda_re.reshape(1, SSM_H)
    lam_im = lambda_im.reshape(1, SSM_H)
    ls_x = jnp.repeat(log_step.reshape(SSM_G), SSM_N).reshape(1, SSM_H)
    btr, bti = _b_to_rows(ssm_b_re[0]), _b_to_rows(ssm_b_im[0])
    ctr, cti = _c_to_rows(ssm_c_re[0]), _c_to_rows(ssm_c_im[0])
    abar, bcat, ccat = _ssm_setup(lam_re, lam_im, ls_x, btr, bti, ctr, cti)
    wp, mp, vp = [_pack(_small_params_packed(p)) for p in (W, Mo, Vo)]

    def f_norm1(rv, vv, i, nt):
        xv, (g, sc, sh) = rv[0], vv
        return [(xv * _rms(xv) * g) * (1.0 + sc) + sh], []

    h = _rowwise(f_norm1, [_row(x2d)], [norm1_g, sc1, sh1], [(D, BF16)], [],
                 n_rows=S, tr=TR, ch=32, name="norm1_fwd", dep=zone["w_ff2"])[0]
    (zone_in,) = _relay_gather_arrive([in_flight], [h, bias2, bcat, zone["w_ff1"], wp, mp, vp], "w_in_arrive")
    (in_pass,), tok_p = _relay_pass_start([zone_in], "w_in_pass_start")
    mixer = ["w_attn_proj", "w_glu", "w_ssm_proj", "w_out"]
    later_flights, tok_w = _relay_gather_start(
        [zone[n] for n in mixer] + [zone["w_ff1"], zone["w_ff2"]], "weights_start", tok_p)
    mixer_flights, ff_flights = later_flights[:len(mixer)], later_flights[len(mixer):]
    (G["w_in"],) = _relay_pass_wait([in_pass], tok_w, "w_in_pass_wait")
    proj = _matmul(h, G["w_in"], mode="nn", dims=(S, IN_W, D), tiles=(TX, 768, D),
                   out_dtypes=[BF16], name="in_proj", b3=True, bias=b_in, dep=tok_w)

    attn = _attention_fwd(proj, bias2, sinkcol, S)
    mixer_zones = _relay_gather_arrive(mixer_flights, attn, "mixer_weights_arrive")
    mixer_pass, _ = _relay_pass_start(mixer_zones, "mixer_weights_pass_start")

    u_blk = (ATTN_W + 2 * KV_W) // SSM_W
    xs, xprev, yc = _scan_fwd(proj, u_blk, bcat, ccat, abar, S)

    def f_ssm_out(rv, vv, i, nt):
        y = rv[0] + vv[0] * rv[1]
        return [y, _gelu(y)], []

    y_ssm_pre, z = _rowwise(f_ssm_out, [_row(yc), _row(proj, u_blk, SSM_W)], [ssm_d],
                            [(SSM_W, F32), (SSM_W, BF16)], [], n_rows=S, tr=TM, ch=32, name="ssm_out")
    G.update(zip(mixer, _relay_pass_wait(mixer_pass, z, "mixer_weights_pass_wait")))
    w_glu_f = G["w_glu"].reshape(SSM_W, SSM_W)
    w_out_f = G["w_out"].reshape(D, D)
    w_ap_f = jnp.transpose(G["w_attn_proj"], (1, 0, 2)).reshape(ATTN_W, D)
    w_sp_f = jnp.transpose(G["w_ssm_proj"], (1, 0, 2)).reshape(SSM_W, D)
    y_attn = _matmul(attn, w_ap_f, mode="nn", dims=(S, D, ATTN_W), tiles=(TW, 1024, ATTN_W),
                     out_dtypes=[BF16], name="attn_proj")
    zg = _matmul(z, w_glu_f, mode="nn", dims=(S, SSM_W, SSM_W), tiles=(TM, SSM_W, SSM_W),
                 out_dtypes=[F32], name="glu_proj", bias=b_glu)
    z2 = _rowwise(lambda rv, vv, i, nt: ([rv[0].astype(F32) * _sigmoid(rv[1])], []),
                  [_row(z), _row(zg)], [], [(SSM_W, BF16)], [], n_rows=S, tr=TM, ch=32, name="glu_gate")[0]
    y_ssm = _matmul(z2, w_sp_f, mode="nn", dims=(S, D, SSM_W), tiles=(TW, 1024, SSM_W),
                    out_dtypes=[BF16], name="ssm_proj")

    ga_row = _row(proj, 1, D)
    gs_row = _row(proj, 2, D)

    def f_merge(rv, vv, i, nt):
        ga, gs, ya, ys = rv
        return [_sigmoid(ga) * ya + _sigmoid(gs) * ys], []

    merged = _rowwise(f_merge, [ga_row, gs_row, _row(y_attn), _row(y_ssm)], [], [(D, BF16)], [],
                      n_rows=S, tr=TR, ch=32, name="merge")[0]
    mo = _matmul(merged, w_out_f, mode="nn", dims=(S, D, D), tiles=(TW, 1024, D),
                 out_dtypes=[BF16], name="out_proj")

    ff_zones = _relay_gather_arrive(ff_flights, mo, "ff_weights_arrive")
    ff_pass, tok_fp = _relay_pass_start(ff_zones, "ff_weights_pass_start")

    def f_norm2(rv, vv, i, nt):
        xv, mv = rv
        g1v, g, sc, sh = vv
        x1v = xv + g1v * mv
        return [x1v, (x1v * _rms(x1v) * g) * (1.0 + sc) + sh], []

    x1, h2 = _rowwise(f_norm2, [_row(x2d), _row(mo)], [g1, norm2_g, sc2, sh2],
                      [(D, F32), (D, BF16)], [], n_rows=S, tr=TR, ch=32, name="norm2_fwd", dep=tok_fp)

    def relu_sq(acc):
        r = jnp.maximum(acc, 0.0)
        return r * r, r

    (G["w_ff1"],) = _relay_pass_wait(ff_pass[:1], h2, "w_ff1_pass_wait")
    act, relu = _matmul(h2, G["w_ff1"], mode="nn", dims=(S, D_FF, D), tiles=(TX, 1024, D),
                        out_dtypes=[BF16, BF16], name="ff1", b3=True, epilogue=relu_sq)
    w_ff2_f = _relay_pass_wait(ff_pass[1:], act, "w_ff2_pass_wait")[0].reshape(D_FF, D)
    ff = _matmul(act, w_ff2_f, mode="nn", dims=(S, D, D_FF), tiles=(TX, 1024, 2048),
                 out_dtypes=[BF16], name="ff2")

    def f_loss(rv, vv, i, nt):
        x1v, ffv, tv = rv
        g2v, gf = vv
        x2v = x1v + g2v * ffv
        r = _rms(x2v)
        xh = x2v * r
        diff = xh * gf - tv
        dy = diff * (1.0 / D)
        dxh = dy * gf
        dx2 = r * (dxh - xh * jnp.mean(dxh * xh, axis=-1, keepdims=True))
        return [dx2, dx2 * g2v], [_colsum(0.5 * diff * diff * (1.0 / D)), _colsum(dy * xh),
                                  _colsum(dx2 * ffv)]

    dx2, dff, loss_cols, d_final_g, dg2 = _rowwise(
        f_loss, [_row(x1), _row(ff), _row(tgt)], [g2, final_g.reshape(1, D)],
        [(D, F32), (D, BF16)], [(1, D)] * 3, n_rows=S, tr=TR, ch=32, name="loss_bwd")

    df1 = _matmul(dff, w_ff2_f, mode="nt", dims=(S, D_FF, D), tiles=(TX, 1024, D),
                  out_dtypes=[BF16], name="ff2_dgrad", extras=(relu,),
                  epilogue=lambda acc, r: (acc * (2.0 * r.astype(F32)),))
    gw_ff2 = _matmul(act, dff, mode="tn", dims=(D_FF, D, S), tiles=(2048, 1024, TS),
                     out_dtypes=[BF16], name="ff2_wgrad").reshape(N_DEV, D_FF // N_DEV, D)
    g_flight = {}
    (g_flight["w_ff2"],), tok = _exchange_start([gw_ff2], "scatter", "grads_start_ff2")
    dh2 = _matmul(df1, G["w_ff1"], mode="nt", dims=(S, D, D_FF), tiles=(TW, D, 1024),
                  out_dtypes=[BF16], name="ff1_dgrad", b3=True, dep=tok)
    gw_ff1 = _matmul(h2, df1, mode="tn", dims=(D, D_FF, S), tiles=(2048, 1024, TS),
                     out_dtypes=[BF16], name="ff1_wgrad", out3=True)
    (g_flight["w_ff1"],), tok = _exchange_start([gw_ff1], "scatter", "grads_start_ff1")

    def f_norm2_bwd(rv, vv, i, nt):
        x1v, dh, dx2v, mv = rv
        g, sc, g1v = vv
        r = _rms(x1v)
        xh = x1v * r
        t = xh * g
        dt = dh * (1.0 + sc)
        dxh = dt * g
        dx1 = dx2v + r * (dxh - xh * jnp.mean(dxh * xh, axis=-1, keepdims=True))
        return [dx1, dx1 * g1v], [_colsum(dh), _colsum(dh * t), _colsum(dt * xh), _colsum(dx1 * mv)]

    dx1, dmo, dsh2, dsc2, d_norm2_g, dg1 = _rowwise(
        f_norm2_bwd, [_row(x1), _row(dh2), _row(dx2), _row(mo)], [norm2_g, sc2, g1],
        [(D, F32), (D, BF16)], [(1, D)] * 4, n_rows=S, tr=TR, ch=16, name="norm2_bwd", dep=tok)

    dmerged = _matmul(dmo, w_out_f, mode="nt", dims=(S, D, D), tiles=(TW, 1024, D),
                      out_dtypes=[BF16], name="out_dgrad")
    gw_out = _matmul(merged, dmo, mode="tn", dims=(D, D, S), tiles=(2048, 1024, TS),
                     out_dtypes=[BF16], name="out_wgrad").reshape(N_DEV, D // N_DEV, D)
    (g_flight["w_out"],), tok = _exchange_start([gw_out], "scatter", "grads_start_out")

    def f_merge_bwd(rv, vv, i, nt):
        dm, ga, gs, ya, ys = rv
        sa, ss = _sigmoid(ga), _sigmoid(gs)
        return [dm * sa, dm * ss, dm * ya * sa * (1.0 - sa), dm * ys * ss * (1.0 - ss)], []

    dy_attn, dy_ssm, dga, dgs = _rowwise(
        f_merge_bwd, [_row(dmerged), ga_row, gs_row, _row(y_attn), _row(y_ssm)], [],
        [(D, BF16)] * 4, [], n_rows=S, tr=TR, ch=16, name="merge_bwd", dep=tok)

    dz2 = _matmul(dy_ssm, w_sp_f, mode="nt", dims=(S, SSM_W, D), tiles=(TW, SSM_W, D),
                  out_dtypes=[F32], name="ssm_proj_dgrad")
    gw_ssm_proj = _to_col_blocks(_matmul(z2, dy_ssm, mode="tn", dims=(SSM_W, D, S), tiles=(SSM_W, 1024, TS),
                                         out_dtypes=[BF16], name="ssm_proj_wgrad"))

    def f_glu_bwd(rv, vv, i, nt):
        dz2v, zv, zgv = rv
        sg = _sigmoid(zgv)
        dzg = dz2v * zv.astype(F32) * sg * (1.0 - sg)
        return [dzg, dz2v * sg], [_colsum(dzg)]

    dzg, dz_a, d_b_glu = _rowwise(f_glu_bwd, [_row(dz2), _row(z), _row(zg)], [],
                                  [(SSM_W, BF16), (SSM_W, F32)], [(1, SSM_W)],
                                  n_rows=S, tr=TM, ch=32, name="glu_bwd")
    dz_b = _matmul(dzg, w_glu_f, mode="nt", dims=(S, SSM_W, SSM_W), tiles=(TM, SSM_W, SSM_W),
                   out_dtypes=[F32], name="glu_dgrad")
    gw_glu = _matmul(z, dzg, mode="tn", dims=(SSM_W, SSM_W, S), tiles=(SSM_W, SSM_W, TS),
                     out_dtypes=[BF16], name="glu_wgrad").reshape(N_DEV, SSM_W // N_DEV, SSM_W)
    (g_flight["w_ssm_proj"], g_flight["w_glu"]), tok = _exchange_start(
        [gw_ssm_proj, gw_glu], "scatter", "grads_start_ssm")

    def f_ssm_out_bwd(rv, vv, i, nt):
        dza, dzb, yv, uv = rv
        dy = (dza + dzb) * _gelu_grad(yv)
        return [dy, dy * vv[0]], [_colsum(dy * uv)]

    dy_s, du_a, d_ssm_d = _rowwise(
        f_ssm_out_bwd, [_row(dz_a), _row(dz_b), _row(y_ssm_pre), _row(proj, u_blk, SSM_W)], [ssm_d],
        [(SSM_W, BF16), (SSM_W, F32)], [(1, SSM_W)], n_rows=S, tr=TM, ch=32, name="ssm_out_bwd", dep=tok)
    hw = SSM_W // 2
    u_half = (ATTN_W + 2 * KV_W) // hw
    dccat = _matmul(dy_s, xs, mode="tn", dims=(hw, 2 * SSM_H, S), tiles=(hw, 1024, TS),
                    out_dtypes=[F32], name="ssm_c_wgrad", a_index=lambda i, j, k: (k, j % 2))
    hs, dacc, du_b = _scan_bwd(dy_s, xprev, bcat, ccat, abar, S)
    dbcat = _matmul(proj, hs, mode="tn", dims=(hw, 2 * SSM_H, S), tiles=(hw, 1024, TS),
                    out_dtypes=[F32], name="ssm_b_wgrad", a_index=lambda i, j, k: (k, u_half + j % 2))
    grp = np.arange(SSM_H) // SSM_N
    gind = jnp.asarray((grp[:, None] == np.arange(128)[None, :]).astype(np.float32), BF16)
    d_lam_re, d_lam_im, d_ls, d_btr, d_bti, d_ctr, d_cti = _ssm_param_bwd(
        lam_re, lam_im, ls_x, btr, bti, dacc, dbcat, dccat, gind)

    dattn = _matmul(dy_attn, w_ap_f, mode="nt", dims=(S, ATTN_W, D), tiles=(TW, ATTN_W, D),
                    out_dtypes=[BF16], name="attn_proj_dgrad")
    gw_attn_proj = _to_col_blocks(_matmul(attn, dy_attn, mode="tn", dims=(ATTN_W, D, S), tiles=(ATTN_W, 1024, TS),
                                          out_dtypes=[BF16], name="attn_proj_wgrad"))
    (g_flight["w_attn_proj"],), tok = _exchange_start(
        [gw_attn_proj], "scatter", "grads_start_attn")
    dq, dkc, dkp, dvc, dvp, dbias, dsink = _attention_bwd(proj, attn, dattn, bias2, sinkcol, S)
    d_bias_b, d_sinks = _bucket_reduce(dbias.reshape(N_Q_HEADS, BLK * 2 * BLK),
                                       dsink.reshape(N_Q_HEADS, BLK), onehot_t)

    def f_dproj(rv, vv, i, nt):
        dqv, kc, kp, vc, vp, dua, dub, gav, gsv = rv
        keep = (i < nt - 1).astype(F32)
        dp = jnp.concatenate([dqv.astype(F32), kc + keep * kp, vc + keep * vp, dua + dub,
                              gav.astype(F32), gsv.astype(F32)], axis=-1)
        return [dp], [_colsum(dp)]

    dproj, d_b_in = _rowwise(
        f_dproj, [_row(dq), _row(dkc), _row(dkp, shift=1), _row(dvc), _row(dvp, shift=1),
                  _row(du_a), _row(du_b), _row(dga), _row(dgs)], [],
        [(IN_W, BF16)], [(1, IN_W)], n_rows=S, tr=BLK, ch=16, name="dproj", dep=tok)
    gw_in = _matmul(h, dproj, mode="tn", dims=(D, IN_W, S), tiles=(2048, 768, TS),
                    out_dtypes=[BF16], name="in_wgrad", out3=True)
    (g_flight["w_in"],), tok = _exchange_start([gw_in], "scatter", "grads_start_in")
    dh = _matmul(dproj, G["w_in"], mode="nt", dims=(S, D, IN_W), tiles=(TW, D, 768),
                 out_dtypes=[BF16], name="in_dgrad", b3=True, dep=tok)

    def f_norm1_bwd(rv, vv, i, nt):
        xv, dhv, dx1v = rv
        g, sc = vv
        r = _rms(xv)
        xh = xv * r
        t = xh * g
        dt = dhv * (1.0 + sc)
        dxh = dt * g
        dxv = dx1v + r * (dxh - xh * jnp.mean(dxh * xh, axis=-1, keepdims=True))
        return [dxv], [_colsum(dhv), _colsum(dhv * t), _colsum(dt * xh)]

    grad_x, dsh1, dsc1, d_norm1_g = _rowwise(
        f_norm1_bwd, [_row(x2d), _row(dh), _row(dx1)], [norm1_g, sc1],
        [(D, F32)], [(1, D)] * 3, n_rows=S, tr=TR, ch=32, name="norm1_bwd")

    part = _pack({
        "b_ada": [dsh1, dsc1, dg1, dsh2, dsc2, dg2], "norm1_g": d_norm1_g, "b_in": d_b_in, "norm2_g": d_norm2_g,
        "final_g": d_final_g, "lambda_re": d_lam_re, "lambda_im": d_lam_im,
        "log_step": d_ls[0, :SSM_G], "attn_sinks": d_sinks[:, 0],
        "rel_bias": jnp.transpose(d_bias_b[:, :NUM_BUCKETS]), "b_glu": d_b_glu, "ssm_d": d_ssm_d,
        "loss": loss_cols, "ssm_b_re": d_btr, "ssm_b_im": d_bti, "ssm_c_re": d_ctr, "ssm_c_im": d_cti,
    })
    zone_small = lax.dynamic_update_slice(lax.empty((N_DEV, PACK_ROWS, PACK_W), F32), part[None], (me, 0, 0))
    (small_flight,), after = _exchange_start([zone_small], "gather", "small_grads_start")

    big_out = {}
    for n in ["w_ff2", "w_ff1", "w_out", "w_ssm_proj", "w_glu", "w_attn_proj", "w_in"]:
        own, recv = _exchange_wait([g_flight[n]], "scatter", after, "grads_wait_" + n[2:])[0]
        rows, cols = shard[n].shape
        parts = [(own, lambda m: m[0])] + [
            (recv, lambda m, j=j: jnp.where(j >= m[0], j + 1, j)) for j in range(N_DEV - 1)]
        big_out[n] = _adamw(parts, shard[n], Mo[n][0], Vo[n][0], tr=_adamw_rows(rows, cols), ch=16,
                            name="adamw_" + n, prefetch=me1)
        after = big_out[n][0]

    part_all = _exchange_wait([small_flight], "gather", after, "small_grads_wait")[0][0]
    sg, sdelta, sm, sv = _adamw([(part_all, d) for d in range(N_DEV)], wp, mp, vp,
                                tr=PACK_ROWS, ch=8, name="adamw_small")
    lo, _ = _PACK_OFF["loss"]
    loss = jnp.sum(sg[lo])

    o_ada, _ = _PACK_OFF["b_ada"]
    dmod_all = part_all[:, o_ada:o_ada + N_MOD, :].reshape(N_DEV, N_MOD * D)
    dmod_cols = lax.dynamic_slice(dmod_all, (0, me * n_ada), (N_DEV, n_ada))
    gw_ada = _matmul(cs, dmod_cols, mode="tn", dims=(D, n_ada, N_DEV), tiles=(D, 512, N_DEV),
                     out_dtypes=[F32], name="ada_wgrad")
    big_out["w_ada"] = _adamw([(gw_ada, 0)], w_ada[0], m_w_ada[0], v_w_ada[0],
                              tr=_adamw_rows(D, n_ada), ch=16, name="adamw_w_ada")

    def leaf(kind, n):
        if n in big_out:
            return big_out[n][kind][None]
        return _unpack_small((sg, sdelta, sm, sv)[kind], n)

    outs = [loss, grad_x.reshape(1, S, D)]
    for kind in range(4):
        outs.extend(leaf(kind, n) for n in WEIGHT_ORDER)
    return tuple(outs)
```

```python
import functools
import math

import numpy as np
import jax
import jax.numpy as jnp
from jax import lax
from jax.experimental import pallas as pl
from jax.experimental.pallas import tpu as pltpu

F32 = jnp.float32
BF16 = jnp.bfloat16
MESH = pl.DeviceIdType.MESH

N_DEV = 8
D = 2048
HEAD_DIM = 64
N_Q_HEADS = 16
N_KV_HEADS = 4
GROUP = N_Q_HEADS // N_KV_HEADS
ATTN_W = N_Q_HEADS * HEAD_DIM
KV_W = N_KV_HEADS * HEAD_DIM
BLK = 128
NUM_BUCKETS = 32
MAX_DISTANCE = 128
NEG_INF = -1e30
SSM_W = 512
SSM_P = 16
SSM_G = 32
SSM_N = 64
SSM_H = SSM_G * SSM_N
D_FF = 4 * D
IN_W = ATTN_W + 2 * KV_W + SSM_W + 2 * D
N_MOD = 6
EPS = 1e-6

ADAM_LR = 0.001
ADAM_B1 = 0.9
ADAM_B2 = 0.999
ADAM_EPS = 1e-08
ADAM_WD = 0.01
ADAM_STEP = 10

VMEM_LIMIT = 56 * 1024 * 1024
PACK_W = 2048


def _cparams(sem):
    return pltpu.CompilerParams(dimension_semantics=sem, vmem_limit_bytes=VMEM_LIMIT)


def _matmul(a, b, *, mode, dims, tiles, out_dtypes, name, a_off=0, b3=False,
            out3=False, bias=None, extras=(), epilogue=None, dep=None, a_index=None, b_index=None):
    M, N, K = dims
    tm, tn, tk = tiles
    assert M % tm == 0 and N % tn == 0 and K % tk == 0, (name, dims, tiles)
    gm, gn, gk = M // tm, N // tn, K // tk
    n_extra = len(extras)
    has_bias = bias is not None
    n_out = len(out_dtypes)

    if mode == "nn":
        a_spec = pl.BlockSpec((tm, tk), lambda i, j, k: (i, a_off + k))
        if b3:
            nb = (N // N_DEV) // tn
            assert nb * tn * N_DEV == N
            b_spec = pl.BlockSpec((None, tk, tn), lambda i, j, k: (j // nb, k, j % nb))
        else:
            b_spec = pl.BlockSpec((tk, tn), lambda i, j, k: (k, j))
        dn = (((1,), (0,)), ((), ()))
    elif mode == "nt":
        a_spec = pl.BlockSpec((tm, tk), lambda i, j, k: (i, a_off + k))
        if b3:
            nb = (K // N_DEV) // tk
            assert nb * tk * N_DEV == K
            b_spec = pl.BlockSpec((None, tn, tk), lambda i, j, k: (k // nb, j, k % nb))
        else:
            b_spec = pl.BlockSpec((tn, tk), lambda i, j, k: (j, k))
        dn = (((1,), (1,)), ((), ()))
    else:
        a_spec = pl.BlockSpec((tk, tm), lambda i, j, k: (k, a_off + i))
        b_spec = pl.BlockSpec((tk, tn), lambda i, j, k: (k, j))
        dn = (((0,), (0,)), ((), ()))
    if a_index is not None:
        a_spec = pl.BlockSpec(a_spec.block_shape, a_index)
    if b_index is not None:
        b_spec = pl.BlockSpec(b_spec.block_shape, b_index)

    if out3:
        nbo = (N // N_DEV) // tn
        assert nbo * tn * N_DEV == N
        o_spec = pl.BlockSpec((None, tm, tn), lambda i, j, k: (j // nbo, i, j % nbo))
        o_shape = (N_DEV, M, N // N_DEV)
    else:
        o_spec = pl.BlockSpec((tm, tn), lambda i, j, k: (i, j))
        o_shape = (M, N)

    in_specs = [a_spec, b_spec]
    args = [a, b]
    if has_bias:
        in_specs.append(pl.BlockSpec((1, tn), lambda i, j, k: (0, j)))
        args.append(bias)
    for e in extras:
        in_specs.append(pl.BlockSpec((tm, tn), lambda i, j, k: (i, j)))
        args.append(e)
    n_dep = 0 if dep is None else 1
    if n_dep:
        in_specs.append(pl.BlockSpec(memory_space=pl.ANY))
        args.append(dep)

    def body(*refs):
        a_ref, b_ref = refs[0], refs[1]
        pos = 2
        bias_ref = None
        if has_bias:
            bias_ref = refs[pos]
            pos += 1
        extra_refs = refs[pos:pos + n_extra]
        pos += n_extra + n_dep
        out_refs = refs[pos:pos + n_out]
        acc_ref = refs[pos + n_out] if gk > 1 else None

        part = lax.dot_general(a_ref[...].astype(BF16), b_ref[...].astype(BF16), dn,
                               preferred_element_type=F32)

        def finish(acc):
            if has_bias:
                acc = acc + bias_ref[...]
            if epilogue is None:
                vals = (acc,)
            else:
                vals = epilogue(acc, *[e[...] for e in extra_refs])
            for o_ref, val in zip(out_refs, vals):
                o_ref[...] = val.astype(o_ref.dtype)

        if gk == 1:
            finish(part)
        else:
            k = pl.program_id(2)

            @pl.when(k == 0)
            def _():
                acc_ref[...] = part

            @pl.when(k > 0)
            def _():
                acc_ref[...] += part

            @pl.when(k == gk - 1)
            def _():
                finish(acc_ref[...])

    outs = pl.pallas_call(
        body,
        grid=(gm, gn, gk),
        in_specs=in_specs,
        out_specs=[o_spec] * n_out,
        out_shape=[jax.ShapeDtypeStruct(o_shape, dt) for dt in out_dtypes],
        scratch_shapes=([pltpu.VMEM((tm, tn), F32)] if gk > 1 else []),
        compiler_params=_cparams(("parallel", "parallel", "arbitrary")),
        name=name,
    )(*args)
    return outs[0] if n_out == 1 else outs


def _rowwise(fn, rows, vecs, row_outs, sum_outs, *, n_rows, tr, ch, name, dep=None, prefetch=None):
    assert n_rows % tr == 0 and tr % ch == 0
    nt = n_rows // tr
    nr, nv, nro, nso = len(rows), len(vecs), len(row_outs), len(sum_outs)
    in_specs, args = [], []
    n_pf = 0 if prefetch is None else 1
    for (arr, lead, cblk, w, shift) in rows:
        if shift:
            ridx = lambda i, shift=shift: jnp.minimum(i + shift, nt - 1)
        else:
            ridx = lambda i: i
        if arr.ndim == 3:
            def imap(i, *pf, lead=lead, cblk=cblk, ridx=ridx):
                return (lead(pf[0]) if callable(lead) else lead, ridx(i), cblk)
            in_specs.append(pl.BlockSpec((None, tr, w), imap))
        else:
            in_specs.append(pl.BlockSpec(
                (tr, w), lambda i, *pf, cblk=cblk, ridx=ridx: (ridx(i), cblk)))
        args.append(arr)
    for v in vecs:
        in_specs.append(pl.BlockSpec(v.shape, lambda i, *pf, nd=v.ndim: (0,) * nd))
        args.append(v)
    n_dep = 0 if dep is None else 1
    if n_dep:
        in_specs.append(pl.BlockSpec(memory_space=pl.ANY))
        args.append(dep)
    out_specs = [pl.BlockSpec((tr, w), lambda i, *pf: (i, 0)) for (w, _) in row_outs]
    out_shape = [jax.ShapeDtypeStruct((n_rows, w), dt) for (w, dt) in row_outs]
    for (r, w) in sum_outs:
        out_specs.append(pl.BlockSpec((r, w), lambda i, *pf: (0, 0)))
        out_shape.append(jax.ShapeDtypeStruct((r, w), F32))

    def body(*refs):
        refs = refs[n_pf:]
        i = pl.program_id(0)
        r_in = refs[:nr]
        v_in = refs[nr:nr + nv]
        r_out = refs[nr + nv + n_dep:nr + nv + n_dep + nro]
        s_out = refs[nr + nv + n_dep + nro:]
        s_out, s_acc = s_out[:nso], s_out[nso:]
        if nso:
            @pl.when(i == 0)
            def _():
                for s in s_acc:
                    s[...] = jnp.zeros(s.shape, F32)
        vvals = [v[...] for v in v_in]

        def chunk(ci, carry):
            r0 = pl.multiple_of(ci * ch, ch)
            rv = [r[pl.ds(r0, ch), :].astype(F32) for r in r_in]
            pieces = [fn([v[8 * k:8 * (k + 1)] for v in rv], vvals, i, nt) for k in range(ch // 8)]
            for j, ref in enumerate(r_out):
                val = jnp.concatenate([ro[j] for ro, _ in pieces], axis=0) if ch > 8 else pieces[0][0][j]
                ref[pl.ds(r0, ch), :] = val.astype(ref.dtype)
            for j, ref in enumerate(s_acc):
                ref[...] += functools.reduce(lambda a, b: a + b, [so[j] for _, so in pieces])
            return carry

        lax.fori_loop(0, tr // ch, chunk, 0)
        if nso:
            @pl.when(i == nt - 1)
            def _():
                for s, acc in zip(s_out, s_acc):
                    s[...] = jnp.sum(acc[...], axis=0, keepdims=True)

    outs = pl.pallas_call(
        body,
        grid_spec=pltpu.PrefetchScalarGridSpec(
            num_scalar_prefetch=n_pf, grid=(nt,), in_specs=in_specs, out_specs=out_specs,
            scratch_shapes=[pltpu.VMEM((8, w), F32) for (_, w) in sum_outs]),
        out_shape=out_shape,
        compiler_params=_cparams(("arbitrary",)),
        name=name,
    )(*([prefetch] if n_pf else []), *args)
    return outs


def _row(arr, cblk=0, w=None, lead=0, shift=0):
    return (arr, lead, cblk, arr.shape[-1] if w is None else w, shift)


def _colsum(v):
    parts = [v[8 * k:8 * (k + 1)] for k in range(v.shape[0] // 8)]
    return functools.reduce(lambda a, b: a + b, parts)


def _rms(x):
    return lax.rsqrt(jnp.mean(x * x, axis=-1, keepdims=True) + EPS)


def _sigmoid(x):
    return 1.0 / (1.0 + jnp.exp(-x))


_GELU_C = math.sqrt(2.0 / math.pi)


def _gelu(x):
    return 0.5 * x * (1.0 + jnp.tanh(_GELU_C * (x + 0.044715 * (x * x * x))))


def _gelu_grad(x):
    t = jnp.tanh(_GELU_C * (x + 0.044715 * (x * x * x)))
    return 0.5 * (1.0 + t) + 0.5 * x * (1.0 - t * t) * (_GELU_C * (1.0 + 3.0 * 0.044715 * (x * x)))


def _my_pos():
    return lax.axis_index("x"), lax.axis_index("y"), lax.axis_index("c")


def _flip(pos, k):
    x, y, c = pos
    return (1 - x if k & 4 else x, 1 - y if k & 2 else y, 1 - c if k & 1 else c)


def _dev_id(pos):
    return 4 * pos[0] + 2 * pos[1] + pos[2]


def _small_allgather(x, name):
    r, c = x.shape

    def body(x_ref, out_ref, send_sems, recv_sems):
        me = _my_pos()
        out_ref[_dev_id(me)] = x_ref[...]
        copies = []
        for k in range(1, N_DEV):
            cp = pltpu.make_async_remote_copy(
                src_ref=x_ref, dst_ref=out_ref.at[_dev_id(me)],
                send_sem=send_sems.at[k - 1], recv_sem=recv_sems.at[k - 1],
                device_id=_flip(me, k), device_id_type=MESH)
            cp.start()
            copies.append(cp)
        for k in range(1, N_DEV):
            peer = _flip(me, k)
            pltpu.make_async_remote_copy(
                src_ref=x_ref, dst_ref=out_ref.at[_dev_id(peer)],
                send_sem=send_sems.at[k - 1], recv_sem=recv_sems.at[k - 1],
                device_id=peer, device_id_type=MESH).wait_recv()
        for cp in copies:
            cp.wait_send()

    return pl.pallas_call(
        body,
        out_shape=jax.ShapeDtypeStruct((N_DEV, r, c), x.dtype),
        in_specs=[pl.BlockSpec(memory_space=pltpu.VMEM)],
        out_specs=pl.BlockSpec(memory_space=pltpu.VMEM),
        scratch_shapes=[pltpu.SemaphoreType.DMA((N_DEV - 1,)),
                        pltpu.SemaphoreType.DMA((N_DEV - 1,))],
        compiler_params=pltpu.CompilerParams(vmem_limit_bytes=VMEM_LIMIT),
        name=name,
    )(x)


_HBM = pl.BlockSpec(memory_space=pltpu.HBM)
_SEM = pl.BlockSpec(memory_space=pltpu.SEMAPHORE)
_EFFECT = pltpu.SideEffectType.DATAFLOW_SIDE_EFFECTING


def _relay_copy(zone, send_sems, recv_sems, k, block, to):
    slot = zone.at[_dev_id(block)]
    return pltpu.make_async_remote_copy(
        src_ref=slot, dst_ref=slot, send_sem=send_sems.at[k], recv_sem=recv_sems.at[k],
        device_id=to, device_id_type=MESH)


def _relay_peers():
    x, y, c = _my_pos()
    return (x, y, c), (x, y, 1 - c), [(1 - x, y), (x, 1 - y), (1 - x, 1 - y)]


def _relay_start_call(zones, n_sems, issue, name, after=None):
    n = len(zones)
    n_after = 0 if after is None else 1

    def body(*refs):
        refs = refs[:n] + refs[n + n_after:]
        send, recv, token = refs[n:2 * n], refs[2 * n:3 * n], refs[4 * n]
        for a in range(n):
            issue(refs[a], send[a], recv[a])
        token[...] = jnp.zeros(token.shape, token.dtype)

    sem = pltpu.SemaphoreType.DMA((n_sems,))
    outs = pl.pallas_call(
        body,
        name=name,
        out_shape=([sem] * (2 * n) + [pltpu.HBM(z.shape, z.dtype) for z in zones]
                   + [jax.ShapeDtypeStruct((8, 128), F32)]),
        in_specs=[_HBM] * n + [pl.BlockSpec(memory_space=pl.ANY)] * n_after,
        out_specs=[_SEM] * (2 * n) + [_HBM] * n + [pl.BlockSpec(memory_space=pltpu.VMEM)],
        input_output_aliases={a: 2 * n + a for a in range(n)},
        compiler_params=pltpu.CompilerParams(has_side_effects=_EFFECT),
    )(*[pltpu.with_memory_space_constraint(z, pltpu.HBM) for z in zones],
      *([after] if n_after else []))
    return [(outs[a], outs[n + a], outs[2 * n + a]) for a in range(n)], outs[3 * n]


def _relay_wait_call(flights, settle, after, name):
    n = len(flights)
    after = list(after) if isinstance(after, (list, tuple)) else [after]

    def body(*refs):
        send, recv = refs[n:2 * n], refs[2 * n:3 * n]
        for a in range(n):
            settle(refs[a], send[a], recv[a])

    outs = pl.pallas_call(
        body,
        name=name,
        out_shape=[pltpu.HBM(f[2].shape, f[2].dtype) for f in flights],
        in_specs=[_HBM] * n + [_SEM] * (2 * n) + [pl.BlockSpec(memory_space=pl.ANY)] * len(after),
        out_specs=[_HBM] * n,
        input_output_aliases={a: a for a in range(n)},
        compiler_params=pltpu.CompilerParams(has_side_effects=_EFFECT),
    )(*[f[2] for f in flights], *[f[0] for f in flights], *[f[1] for f in flights], *after)
    return list(outs)


def _relay_gather_start(zones, name, after=None):
    def issue(zone, send, recv):
        me, sib, chips = _relay_peers()
        _relay_copy(zone, send, recv, 0, me, sib).start()
        for j, chip in enumerate(chips):
            _relay_copy(zone, send, recv, 1 + j, me, (*chip, me[2])).start()
    return _relay_start_call(zones, 4, issue, name, after)


def _relay_gather_arrive(flights, after, name):
    def settle(zone, send, recv):
        me, sib, chips = _relay_peers()
        _relay_copy(zone, send, recv, 0, sib, me).wait_recv()
        _relay_copy(zone, send, recv, 0, me, sib).wait_send()
        for j, chip in enumerate(chips):
            _relay_copy(zone, send, recv, 1 + j, (*chip, me[2]), me).wait_recv()
            _relay_copy(zone, send, recv, 1 + j, me, (*chip, me[2])).wait_send()
    return _relay_wait_call(flights, settle, after, name)


def _relay_pass_start(zones, name, after=None):
    def issue(zone, send, recv):
        me, sib, chips = _relay_peers()
        for j, chip in enumerate(chips):
            _relay_copy(zone, send, recv, j, (*chip, me[2]), sib).start()
    return _relay_start_call(zones, 3, issue, name, after)


def _relay_pass_wait(flights, after, name):
    def settle(zone, send, recv):
        me, sib, chips = _relay_peers()
        for j, chip in enumerate(chips):
            _relay_copy(zone, send, recv, j, (*chip, sib[2]), me).wait_recv()
            _relay_copy(zone, send, recv, j, (*chip, me[2]), sib).wait_send()
    return _relay_wait_call(flights, settle, after, name)


def _exchange_copy(kind, bufs, send_sems, recv_sems, me, k, arriving):
    peer = _flip(me, k)
    my_id, peer_id = _dev_id(me), _dev_id(peer)
    if kind == "gather":
        slot = bufs[0].at[peer_id if arriving else my_id]
        src, dst = slot, slot
    else:
        src = bufs[0].at[my_id if arriving else peer_id]
        dst = bufs[1].at[peer_id if arriving else my_id]
    return pltpu.make_async_remote_copy(
        src_ref=src, dst_ref=dst, send_sem=send_sems.at[k - 1], recv_sem=recv_sems.at[k - 1],
        device_id=peer, device_id_type=MESH)


def _exchange_start(arrays, kind, name, after=None):
    n = len(arrays)
    n_after = 0 if after is None else 1
    if kind == "gather":
        bufs = [[a] for a in arrays]
    else:
        bufs = [[a, lax.empty(a.shape, a.dtype)] for a in arrays]
    nb = len(bufs[0])
    flat = [b for group in bufs for b in group]

    def body(*refs):
        outs_at = nb * n + n_after
        send = refs[outs_at:outs_at + n]
        recv = refs[outs_at + n:outs_at + 2 * n]
        token = refs[outs_at + 2 * n + nb * n]
        me = _my_pos()
        for a in range(n):
            for k in range(1, N_DEV):
                _exchange_copy(kind, refs[nb * a:nb * (a + 1)], send[a], recv[a], me, k, False).start()
        token[...] = jnp.zeros(token.shape, token.dtype)

    sem = pltpu.SemaphoreType.DMA((N_DEV - 1,))
    outs = pl.pallas_call(
        body,
        name=name,
        out_shape=([sem] * (2 * n) + [pltpu.HBM(b.shape, b.dtype) for b in flat]
                   + [jax.ShapeDtypeStruct((8, 128), F32)]),
        in_specs=[_HBM] * (nb * n) + [pl.BlockSpec(memory_space=pl.ANY)] * n_after,
        out_specs=[_SEM] * (2 * n) + [_HBM] * (nb * n) + [pl.BlockSpec(memory_space=pltpu.VMEM)],
        input_output_aliases={i: 2 * n + i for i in range(nb * n)},
        compiler_params=pltpu.CompilerParams(has_side_effects=_EFFECT),
    )(*[pltpu.with_memory_space_constraint(b, pltpu.HBM) for b in flat],
      *([after] if n_after else []))
    flights = [(outs[a], outs[n + a], list(outs[2 * n + nb * a:2 * n + nb * (a + 1)]))
               for a in range(n)]
    return flights, outs[2 * n + nb * n]


def _exchange_wait(flights, kind, after, name):
    n = len(flights)
    nb = len(flights[0][2])
    flat = [b for f in flights for b in f[2]]

    def body(*refs):
        send = refs[nb * n:nb * n + n]
        recv = refs[nb * n + n:nb * n + 2 * n]
        me = _my_pos()
        for a in range(n):
            for k in range(1, N_DEV):
                bufs = refs[nb * a:nb * (a + 1)]
                _exchange_copy(kind, bufs, send[a], recv[a], me, k, False).wait_send()
                _exchange_copy(kind, bufs, send[a], recv[a], me, k, True).wait_recv()

    outs = pl.pallas_call(
        body,
        name=name,
        out_shape=[pltpu.HBM(b.shape, b.dtype) for b in flat],
        in_specs=[_HBM] * (nb * n) + [_SEM] * (2 * n) + [pl.BlockSpec(memory_space=pl.ANY)],
        out_specs=[_HBM] * (nb * n),
        input_output_aliases={i: i for i in range(nb * n)},
        compiler_params=pltpu.CompilerParams(has_side_effects=_EFFECT),
    )(*flat, *[f[0] for f in flights], *[f[1] for f in flights], after)
    return [list(outs[nb * a:nb * (a + 1)]) for a in range(n)]


def _t5_buckets_block():
    qi = np.arange(BLK)[:, None]
    ki = np.arange(2 * BLK)[None, :]
    n = np.maximum(qi + BLK - ki, 0)
    max_exact = NUM_BUCKETS // 2
    large = max_exact + (np.log(np.maximum(n, 1) / max_exact)
                         / np.log(MAX_DISTANCE / max_exact)
                         * (NUM_BUCKETS - max_exact)).astype(np.int32)
    large = np.minimum(large, NUM_BUCKETS - 1)
    return np.where(n < max_exact, n, large).astype(np.int32)


def _band_mask():
    qi = np.arange(BLK)[:, None]
    ki = np.arange(2 * BLK)[None, :]
    dist = qi + BLK - ki
    return (dist >= 0) & (dist < BLK)


def _attn_scores(q_ref, kp_ref, kc_ref, hkv):
    c0 = hkv * HEAD_DIM
    kk = jnp.concatenate([kp_ref[:, c0:c0 + HEAD_DIM], kc_ref[:, c0:c0 + HEAD_DIM]],
                         axis=0).astype(BF16)
    qg = jnp.concatenate(
        [q_ref[:, (hkv * GROUP + g) * HEAD_DIM:(hkv * GROUP + g + 1) * HEAD_DIM]
         for g in range(GROUP)], axis=0).astype(BF16)
    s = lax.dot_general(qg, kk, (((1,), (1,)), ((), ())), preferred_element_type=F32)
    return qg, kk, s


def _attn_softmax(s, bias_ref, sink_ref, hkv):
    r0, r1 = hkv * GROUP * BLK, (hkv + 1) * GROUP * BLK
    s = s * (HEAD_DIM ** -0.5) + bias_ref[r0:r1, :]
    sink = sink_ref[r0:r1, :]
    m = jnp.maximum(jnp.max(s, axis=-1, keepdims=True), sink)
    p = jnp.exp(s - m)
    e_sink = jnp.exp(sink - m)
    inv = 1.0 / (jnp.sum(p, axis=-1, keepdims=True) + e_sink)
    return p * inv, e_sink * inv


def _kv_rows(p_ref, c_ref, hkv):
    c0 = hkv * HEAD_DIM
    return jnp.concatenate([p_ref[:, c0:c0 + HEAD_DIM], c_ref[:, c0:c0 + HEAD_DIM]],
                           axis=0).astype(BF16)


ATT_Q_FWD = 4
ATT_Q_BWD = 2


def _attn_in_specs(bias2, nq):
    prev = lambda n: jnp.maximum(nq * n - 1, 0)
    kcol = ATTN_W // KV_W
    return [
        pl.BlockSpec((nq * BLK, ATTN_W), lambda n: (n, 0)),
        pl.BlockSpec((BLK, KV_W), lambda n: (prev(n), kcol)),
        pl.BlockSpec((nq * BLK, KV_W), lambda n: (n, kcol)),
        pl.BlockSpec((BLK, KV_W), lambda n: (prev(n), kcol + 1)),
        pl.BlockSpec((nq * BLK, KV_W), lambda n: (n, kcol + 1)),
        pl.BlockSpec(bias2.shape, lambda n: (0, 0, 0)),
    ]


def _attn_views(t, q_ref, kp_ref, kc_ref, vp_ref, vc_ref, bias_ref):
    rows = pl.ds(t * BLK, BLK)
    before = pl.ds((t - 1) * BLK, BLK)
    table = jnp.minimum(pl.program_id(0), 1) if t == 0 else 1
    return (q_ref.at[rows, :],
            kp_ref if t == 0 else kc_ref.at[before, :], kc_ref.at[rows, :],
            vp_ref if t == 0 else vc_ref.at[before, :], vc_ref.at[rows, :],
            bias_ref.at[table])


def _attention_fwd(proj, bias2, sinkcol, n_rows):
    nq = min(ATT_Q_FWD, n_rows // BLK)
    steps = n_rows // (nq * BLK)

    def body(q_ref, kp_ref, kc_ref, vp_ref, vc_ref, bias_ref, sink_ref, o_ref):
        views = [_attn_views(t, q_ref, kp_ref, kc_ref, vp_ref, vc_ref, bias_ref) for t in range(nq)]
        work = [(t, hkv) for t in range(nq) for hkv in range(N_KV_HEADS)]
        scores = {w: _attn_scores(views[w[0]][0], views[w[0]][1], views[w[0]][2], w[1])[2] for w in work}
        probs = {w: _attn_softmax(scores[w], views[w[0]][5], sink_ref, w[1])[0] for w in work}
        outs = {w: jnp.dot(probs[w].astype(BF16), _kv_rows(views[w[0]][3], views[w[0]][4], w[1]),
                           preferred_element_type=F32) for w in work}
        for t, hkv in work:
            for g in range(GROUP):
                h = hkv * GROUP + g
                o_ref[t * BLK:(t + 1) * BLK, h * HEAD_DIM:(h + 1) * HEAD_DIM] = (
                    outs[t, hkv][g * BLK:(g + 1) * BLK, :].astype(o_ref.dtype))

    return pl.pallas_call(
        body,
        grid=(steps,),
        in_specs=_attn_in_specs(bias2, nq) + [pl.BlockSpec(sinkcol.shape, lambda n: (0, 0))],
        out_specs=pl.BlockSpec((nq * BLK, ATTN_W), lambda n: (n, 0)),
        out_shape=jax.ShapeDtypeStruct((n_rows, ATTN_W), BF16),
        compiler_params=_cparams(("parallel",)),
        name="attn_fwd",
    )(proj, proj, proj, proj, proj, bias2, sinkcol)


def _attention_bwd(proj, attn, dattn, bias2, sinkcol, n_rows):
    nq = min(ATT_Q_BWD, n_rows // BLK)
    steps = n_rows // (nq * BLK)
    scale = HEAD_DIM ** -0.5
    dn_t = (((0,), (0,)), ((), ()))

    def body(q_ref, kp_ref, kc_ref, vp_ref, vc_ref, bias_ref, o_ref, do_ref, sink_ref,
             dq_ref, dkc_ref, dkp_ref, dvc_ref, dvp_ref, dbias_ref, dsink_ref):
        @pl.when(pl.program_id(0) == 0)
        def _():
            dbias_ref[...] = jnp.zeros(dbias_ref.shape, F32)
            dsink_ref[...] = jnp.zeros(dsink_ref.shape, F32)

        views = [_attn_views(t, q_ref, kp_ref, kc_ref, vp_ref, vc_ref, bias_ref) for t in range(nq)]
        work = [(t, hkv) for t in range(nq) for hkv in range(N_KV_HEADS)]
        qk = {w: _attn_scores(views[w[0]][0], views[w[0]][1], views[w[0]][2], w[1]) for w in work}
        dog, dps, deltas = {}, {}, {}
        for t, hkv in work:
            rows = slice(t * BLK, (t + 1) * BLK)
            hs = [hkv * GROUP + g for g in range(GROUP)]
            d_o = jnp.concatenate([do_ref[rows, h * HEAD_DIM:(h + 1) * HEAD_DIM] for h in hs], axis=0)
            o = jnp.concatenate([o_ref[rows, h * HEAD_DIM:(h + 1) * HEAD_DIM] for h in hs], axis=0)
            deltas[t, hkv] = jnp.sum(d_o.astype(F32) * o.astype(F32), axis=-1, keepdims=True)
            dog[t, hkv] = d_o.astype(BF16)
            dps[t, hkv] = lax.dot_general(dog[t, hkv], _kv_rows(views[t][3], views[t][4], hkv),
                                          (((1,), (1,)), ((), ())), preferred_element_type=F32)
        p16, ds16 = {}, {}
        for t, hkv in work:
            r0, r1 = hkv * GROUP * BLK, (hkv + 1) * GROUP * BLK
            p, p_sink = _attn_softmax(qk[t, hkv][2], views[t][5], sink_ref, hkv)
            ds = p * (dps[t, hkv] - deltas[t, hkv])
            dbias_ref[r0:r1, :] += ds
            dsink_ref[r0:r1, :] += -(p_sink * deltas[t, hkv])
            p16[t, hkv] = p.astype(BF16)
            ds16[t, hkv] = ds.astype(BF16)
        for t, hkv in work:
            rows = slice(t * BLK, (t + 1) * BLK)
            c0 = hkv * HEAD_DIM
            qg, kk, _ = qk[t, hkv]
            dqg = jnp.dot(ds16[t, hkv], kk, preferred_element_type=F32) * scale
            dkk = lax.dot_general(ds16[t, hkv], qg, dn_t, preferred_element_type=F32) * scale
            dvv = lax.dot_general(p16[t, hkv], dog[t, hkv], dn_t, preferred_element_type=F32)
            for g in range(GROUP):
                h = hkv * GROUP + g
                dq_ref[rows, h * HEAD_DIM:(h + 1) * HEAD_DIM] = (
                    dqg[g * BLK:(g + 1) * BLK, :].astype(dq_ref.dtype))
            dkp_ref[rows, c0:c0 + HEAD_DIM] = dkk[:BLK].astype(dkp_ref.dtype)
            dkc_ref[rows, c0:c0 + HEAD_DIM] = dkk[BLK:].astype(dkc_ref.dtype)
            dvp_ref[rows, c0:c0 + HEAD_DIM] = dvv[:BLK].astype(dvp_ref.dtype)
            dvc_ref[rows, c0:c0 + HEAD_DIM] = dvv[BLK:].astype(dvc_ref.dtype)

    wide = pl.BlockSpec((nq * BLK, ATTN_W), lambda n: (n, 0))
    kv_out = pl.BlockSpec((nq * BLK, KV_W), lambda n: (n, 0))
    kv_shape = jax.ShapeDtypeStruct((n_rows, KV_W), F32)
    acc_shape = bias2.shape[1:]
    return pl.pallas_call(
        body,
        grid=(steps,),
        in_specs=_attn_in_specs(bias2, nq) + [wide, wide, pl.BlockSpec(sinkcol.shape, lambda n: (0, 0))],
        out_specs=[
            wide, kv_out, kv_out, kv_out, kv_out,
            pl.BlockSpec(acc_shape, lambda n: (0, 0)),
            pl.BlockSpec(sinkcol.shape, lambda n: (0, 0)),
        ],
        out_shape=[
            jax.ShapeDtypeStruct((n_rows, ATTN_W), BF16),
            kv_shape, kv_shape, kv_shape, kv_shape,
            jax.ShapeDtypeStruct(acc_shape, F32),
            jax.ShapeDtypeStruct(sinkcol.shape, F32),
        ],
        compiler_params=_cparams(("arbitrary",)),
        name="attn_bwd",
    )(proj, proj, proj, proj, proj, bias2, attn, dattn, sinkcol)


def _bias_tables(rel_bias_t, onehot_t, band_first, band_rest):
    def body(rb_ref, oh_ref, mf_ref, mr_ref, out_ref):
        acc = jnp.zeros((N_Q_HEADS, BLK * 2 * BLK), F32)
        for part in _split3(rb_ref[...]):
            acc = acc + jnp.dot(part, oh_ref[...], preferred_element_type=F32)
        out_ref[0] = jnp.where(mf_ref[...] > 0.0, acc, NEG_INF)
        out_ref[1] = jnp.where(mr_ref[...] > 0.0, acc, NEG_INF)

    return pl.pallas_call(
        body,
        out_shape=jax.ShapeDtypeStruct((2, N_Q_HEADS, BLK * 2 * BLK), F32),
        compiler_params=pltpu.CompilerParams(vmem_limit_bytes=VMEM_LIMIT),
        name="bias_tables",
    )(rel_bias_t, onehot_t, band_first, band_rest)


def _split3(a):
    hi = a.astype(BF16)
    r1 = a - hi.astype(F32)
    mid = r1.astype(BF16)
    lo = (r1 - mid.astype(F32)).astype(BF16)
    return hi, mid, lo


def _bucket_reduce(dbias, dsink, onehot_t):
    def body(db_ref, ds_ref, oh_ref, ob_ref, os_ref):
        acc = jnp.zeros((N_Q_HEADS, 128), F32)
        for part in _split3(db_ref[...]):
            acc = acc + lax.dot_general(part, oh_ref[...], (((1,), (1,)), ((), ())),
                                        preferred_element_type=F32)
        ob_ref[...] = acc
        os_ref[...] = jnp.broadcast_to(jnp.sum(ds_ref[...], axis=-1, keepdims=True),
                                       os_ref.shape)

    return pl.pallas_call(
        body,
        out_shape=[jax.ShapeDtypeStruct((N_Q_HEADS, 128), F32),
                   jax.ShapeDtypeStruct((N_Q_HEADS, 128), F32)],
        compiler_params=pltpu.CompilerParams(vmem_limit_bytes=VMEM_LIMIT),
        name="bias_bucket_reduce",
    )(dbias, dsink, onehot_t)


def _disc(lr, li, ls, btr, bti):
    lam_re = jnp.minimum(lr, -1e-4)
    delta = jnp.exp(ls)
    mag = jnp.exp(lam_re * delta)
    ang = li * delta
    ar, ai = mag * jnp.cos(ang), mag * jnp.sin(ang)
    nr, ni = ar - 1.0, ai
    den = lam_re * lam_re + li * li
    fr = (nr * lam_re + ni * li) / den
    fi = (ni * lam_re - nr * li) / den
    bbr = fr * btr - fi * bti
    bbi = fr * bti + fi * btr
    return ar, ai, bbr, bbi


def _block_mask():
    row = lax.broadcasted_iota(jnp.int32, (SSM_W, SSM_H), 0)
    col = lax.broadcasted_iota(jnp.int32, (SSM_W, SSM_H), 1)
    return (row // SSM_P) == (col // SSM_N)


def _ssm_setup(lr, li, ls, btr, bti, ctr, cti):
    def body(lr_ref, li_ref, ls_ref, btr_ref, bti_ref, ctr_ref, cti_ref, a_ref, b_ref, c_ref):
        ar, ai, bbr, bbi = _disc(lr_ref[...], li_ref[...], ls_ref[...], btr_ref[...], bti_ref[...])
        a_ref[:, :SSM_H] = ar
        a_ref[:, SSM_H:] = ai
        mask = _block_mask()
        blk = lambda t: jnp.where(mask, jnp.tile(t, (SSM_G, 1)), 0.0)
        b_ref[:, :SSM_H] = blk(bbr).astype(BF16)
        b_ref[:, SSM_H:] = blk(bbi).astype(BF16)
        c_ref[:, :SSM_H] = blk(ctr_ref[...]).astype(BF16)
        c_ref[:, SSM_H:] = blk(-cti_ref[...]).astype(BF16)

    return pl.pallas_call(
        body,
        out_shape=[jax.ShapeDtypeStruct((1, 2 * SSM_H), F32),
                   jax.ShapeDtypeStruct((SSM_W, 2 * SSM_H), BF16),
                   jax.ShapeDtypeStruct((SSM_W, 2 * SSM_H), BF16)],
        compiler_params=pltpu.CompilerParams(vmem_limit_bytes=VMEM_LIMIT),
        name="ssm_setup",
    )(lr, li, ls, btr, bti, ctr, cti)


def _ssm_param_bwd(lr, li, ls, btr, bti, dacc, dbcat, dccat, gind):
    def body(lr_ref, li_ref, ls_ref, btr_ref, bti_ref, dacc_ref, db_ref, dc_ref, g_ref,
             dlr_ref, dli_ref, dls_ref, dbtr_ref, dbti_ref, dctr_ref, dcti_ref):
        dar = jnp.sum(dacc_ref[:, :SSM_H], axis=0, keepdims=True)
        dai = jnp.sum(dacc_ref[:, SSM_H:], axis=0, keepdims=True)
        col = lax.broadcasted_iota(jnp.int32, (SSM_P, 2 * SSM_H), 1)
        grp = (col % SSM_H) // SSM_N
        db = jnp.zeros((SSM_P, 2 * SSM_H), F32)
        dc = jnp.zeros((SSM_P, 2 * SSM_H), F32)
        half = SSM_G // 2
        for g in range(SSM_G):
            sel = grp == g
            r0 = (g % half) * SSM_P
            db = db + jnp.where(sel, db_ref[r0:r0 + SSM_P, :], 0.0)
            dc = dc + jnp.where(sel, dc_ref[r0:r0 + SSM_P, :], 0.0)
        dctr_ref[...] = dc[:, :SSM_H]
        dcti_ref[...] = -dc[:, SSM_H:]
        prim = (lr_ref[...], li_ref[...], ls_ref[...], btr_ref[...], bti_ref[...])
        _, vjp = jax.vjp(_disc, *prim)
        dlr, dli, dls, dbtr, dbti = vjp((dar, dai, db[:, :SSM_H], db[:, SSM_H:]))
        dlr_ref[...] = dlr
        dli_ref[...] = dli
        dbtr_ref[...] = dbtr
        dbti_ref[...] = dbti
        acc = jnp.zeros((8, 128), F32)
        for part in _split3(jnp.broadcast_to(dls, (8, SSM_H))):
            acc = acc + jnp.dot(part, g_ref[...], preferred_element_type=F32)
        dls_ref[...] = acc

    vec = jax.ShapeDtypeStruct((1, SSM_H), F32)
    mat = jax.ShapeDtypeStruct((SSM_P, SSM_H), F32)
    return pl.pallas_call(
        body,
        out_shape=[vec, vec, jax.ShapeDtypeStruct((8, 128), F32), mat, mat, mat, mat],
        compiler_params=pltpu.CompilerParams(vmem_limit_bytes=VMEM_LIMIT),
        name="ssm_param_bwd",
    )(lr, li, ls, btr, bti, dacc, dbcat, dccat, gind)


SCAN_TR = 256


def _cmul_add(vr, vi, pr, pi, sr, si):
    return vr + pr * sr - pi * si, vi + pr * si + pi * sr


def _bcast_row(v, row, which):
    return jnp.broadcast_to(v[which:which + 1, :], v.shape)


def _scan_tables(a_ref, tab_ref, reverse):
    H = SSM_H
    ar = jnp.broadcast_to(a_ref[:, :H], (8, H))
    ai = jnp.broadcast_to(a_ref[:, H:], (8, H))
    if reverse:
        ai = -ai
    row = lax.broadcasted_iota(jnp.int32, (8, H), 0)
    pw = [(ar, ai)]
    for _ in range(7):
        cr, ci = pw[-1]
        pw.append((cr * ar - ci * ai, cr * ai + ci * ar))
    pcr = jnp.zeros((8, H), F32)
    pci = jnp.zeros((8, H), F32)
    for e in range(8):
        sel = (row == (7 - e)) if reverse else (row == e)
        pcr = jnp.where(sel, pw[e][0], pcr)
        pci = jnp.where(sel, pw[e][1], pci)
    tab_ref[0, :, :H] = pcr
    tab_ref[0, :, H:] = pci
    for t, k in enumerate((1, 2, 4)):
        keep = (row < 8 - k) if reverse else (row >= k)
        tab_ref[1 + t, :, :H] = jnp.where(keep, pw[k - 1][0], 0.0)
        tab_ref[1 + t, :, H:] = jnp.where(keep, pw[k - 1][1], 0.0)


def _scan_group(vr, vi, cr, ci, tab_ref, reverse):
    H = SSM_H
    for t, k in enumerate((1, 2, 4)):
        sh = 8 - k if reverse else k
        vr, vi = _cmul_add(vr, vi, tab_ref[1 + t, :, :H], tab_ref[1 + t, :, H:],
                           pltpu.roll(vr, sh, 0), pltpu.roll(vi, sh, 0))
    return _cmul_add(vr, vi, tab_ref[0, :, :H], tab_ref[0, :, H:], cr, ci)


def _blockdiag_expand(x, w_ref, out_ref):
    hw, cb = SSM_W // 2, SSM_H // 2
    for j in range(4):
        h = j % 2
        out_ref[:, j * cb:(j + 1) * cb] = jnp.dot(
            x[:, h * hw:(h + 1) * hw], w_ref[h * hw:(h + 1) * hw, j * cb:(j + 1) * cb],
            preferred_element_type=F32)


def _blockdiag_contract(x_ref, w_ref):
    hw, cb = SSM_W // 2, SSM_H // 2
    nt = (((1,), (1,)), ((), ()))
    halves = []
    for h in range(2):
        acc = None
        for j in (h, 2 + h):
            part = lax.dot_general(x_ref[:, j * cb:(j + 1) * cb],
                                   w_ref[h * hw:(h + 1) * hw, j * cb:(j + 1) * cb], nt,
                                   preferred_element_type=F32)
            acc = part if acc is None else acc + part
        halves.append(acc)
    return jnp.concatenate(halves, axis=1)


def _scan_fwd(proj, u_blk, bcat, ccat, abar, n_rows):
    H = SSM_H
    nt = n_rows // SCAN_TR

    def body(u_ref, b_ref, c_ref, a_ref, xs_ref, xp_ref, yc_ref, bu_ref, tab_ref, carry_ref):
        @pl.when(pl.program_id(0) == 0)
        def _():
            _scan_tables(a_ref, tab_ref, False)
            carry_ref[...] = jnp.zeros(carry_ref.shape, F32)

        _blockdiag_expand(u_ref[...].astype(BF16), b_ref, bu_ref)
        row = lax.broadcasted_iota(jnp.int32, (8, H), 0)

        def group(j, carry):
            cr, ci = carry
            r0 = pl.multiple_of(j * 16, 16)
            xr, xi = [], []
            for half in range(2):
                rr = pl.multiple_of(r0 + 8 * half, 8)
                vr, vi = _scan_group(bu_ref[pl.ds(rr, 8), :H], bu_ref[pl.ds(rr, 8), H:],
                                     cr, ci, tab_ref, False)
                xp_ref[pl.ds(rr, 8), :H] = jnp.where(row == 0, cr, pltpu.roll(vr, 1, 0))
                xp_ref[pl.ds(rr, 8), H:] = jnp.where(row == 0, ci, pltpu.roll(vi, 1, 0))
                cr, ci = _bcast_row(vr, row, 7), _bcast_row(vi, row, 7)
                xr.append(vr)
                xi.append(vi)
            xs_ref[pl.ds(r0, 16), :H] = jnp.concatenate(xr, axis=0).astype(BF16)
            xs_ref[pl.ds(r0, 16), H:] = jnp.concatenate(xi, axis=0).astype(BF16)
            return cr, ci

        cr, ci = lax.fori_loop(0, SCAN_TR // 16, group,
                               (carry_ref[:, :H], carry_ref[:, H:]))
        carry_ref[:, :H] = cr
        carry_ref[:, H:] = ci
        yc_ref[...] = _blockdiag_contract(xs_ref, c_ref)

    tile = lambda w: pl.BlockSpec((SCAN_TR, w), lambda i: (i, 0))
    whole = lambda a: pl.BlockSpec(a.shape, lambda i: (0, 0))
    return pl.pallas_call(
        body,
        grid=(nt,),
        in_specs=[pl.BlockSpec((SCAN_TR, SSM_W), lambda i: (i, u_blk)),
                  whole(bcat), whole(ccat), whole(abar)],
        out_specs=[tile(2 * H), tile(2 * H), tile(SSM_W)],
        out_shape=[jax.ShapeDtypeStruct((n_rows, 2 * H), BF16),
                   jax.ShapeDtypeStruct((n_rows, 2 * H), F32),
                   jax.ShapeDtypeStruct((n_rows, SSM_W), F32)],
        scratch_shapes=[pltpu.VMEM((SCAN_TR, 2 * H), F32), pltpu.VMEM((4, 8, 2 * H), F32),
                        pltpu.VMEM((8, 2 * H), F32)],
        compiler_params=_cparams(("arbitrary",)),
        name="ssm_scan_fwd",
    )(proj, bcat, ccat, abar)


def _scan_bwd(dy, xprev, bcat, ccat, abar, n_rows):
    H = SSM_H
    nt = n_rows // SCAN_TR

    def body(dy_ref, xp_ref, b_ref, c_ref, a_ref, h_ref, da_ref, du_ref, g_ref, tab_ref, carry_ref):
        @pl.when(pl.program_id(0) == 0)
        def _():
            _scan_tables(a_ref, tab_ref, True)
            carry_ref[...] = jnp.zeros(carry_ref.shape, F32)
            da_ref[...] = jnp.zeros(da_ref.shape, F32)

        _blockdiag_expand(dy_ref[...], c_ref, g_ref)
        row = lax.broadcasted_iota(jnp.int32, (8, H), 0)
        n16 = SCAN_TR // 16

        def group(jj, carry):
            cr, ci = carry
            r0 = pl.multiple_of((n16 - 1 - jj) * 16, 16)
            hr, hi = [None, None], [None, None]
            for half in (1, 0):
                rr = pl.multiple_of(r0 + 8 * half, 8)
                vr, vi = _scan_group(g_ref[pl.ds(rr, 8), :H], g_ref[pl.ds(rr, 8), H:],
                                     cr, ci, tab_ref, True)
                pr, pi = xp_ref[pl.ds(rr, 8), :H], xp_ref[pl.ds(rr, 8), H:]
                da_ref[:, :H] += vr * pr + vi * pi
                da_ref[:, H:] += vi * pr - vr * pi
                cr, ci = _bcast_row(vr, row, 0), _bcast_row(vi, row, 0)
                hr[half], hi[half] = vr, vi
            h_ref[pl.ds(r0, 16), :H] = jnp.concatenate(hr, axis=0).astype(BF16)
            h_ref[pl.ds(r0, 16), H:] = jnp.concatenate(hi, axis=0).astype(BF16)
            return cr, ci

        cr, ci = lax.fori_loop(0, n16, group, (carry_ref[:, :H], carry_ref[:, H:]))
        carry_ref[:, :H] = cr
        carry_ref[:, H:] = ci
        du_ref[...] = _blockdiag_contract(h_ref, b_ref)

    rev = lambda i: (nt - 1 - i, 0)
    whole = lambda a: pl.BlockSpec(a.shape, lambda i: (0, 0))
    return pl.pallas_call(
        body,
        grid=(nt,),
        in_specs=[pl.BlockSpec((SCAN_TR, SSM_W), rev),
                  pl.BlockSpec((SCAN_TR, 2 * H), rev),
                  whole(bcat), whole(ccat), whole(abar)],
        out_specs=[pl.BlockSpec((SCAN_TR, 2 * H), rev),
                   pl.BlockSpec((8, 2 * H), lambda i: (0, 0)),
                   pl.BlockSpec((SCAN_TR, SSM_W), rev)],
        out_shape=[jax.ShapeDtypeStruct((n_rows, 2 * H), BF16),
                   jax.ShapeDtypeStruct((8, 2 * H), F32),
                   jax.ShapeDtypeStruct((n_rows, SSM_W), F32)],
        scratch_shapes=[pltpu.VMEM((SCAN_TR, 2 * H), F32), pltpu.VMEM((4, 8, 2 * H), F32),
                        pltpu.VMEM((8, 2 * H), F32)],
        compiler_params=_cparams(("arbitrary",)),
        name="ssm_scan_bwd",
    )(dy, xprev, bcat, ccat, abar)


def _adamw(parts, w, m, v, *, tr, ch, name, prefetch=None):
    n_rows, cols = w.shape
    n_parts = len(parts)
    c1 = 1.0 - ADAM_B1 ** ADAM_STEP
    c2 = 1.0 - ADAM_B2 ** ADAM_STEP

    def fn(rv, vv, i, nt):
        g = rv[0].astype(F32)
        for p in rv[1:n_parts]:
            g = g + p.astype(F32)
        wv, mv, vval = rv[n_parts:]
        nm = ADAM_B1 * mv + (1.0 - ADAM_B1) * g
        nv = ADAM_B2 * vval + (1.0 - ADAM_B2) * (g * g)
        delta = -ADAM_LR * ((nm / c1) / (jnp.sqrt(nv / c2) + ADAM_EPS) + ADAM_WD * wv)
        return [g, delta, nm, nv], []

    rows = [_row(arr, lead=lead) for (arr, lead) in parts] + [_row(w), _row(m), _row(v)]
    return _rowwise(fn, rows, [], [(cols, F32)] * 4, [], n_rows=n_rows, tr=tr, ch=ch, name=name,
                    prefetch=prefetch)


_PACK = [
    ("b_ada", 6), ("norm1_g", 1), ("b_in", 3), ("norm2_g", 1), ("final_g", 1),
    ("lambda_re", 1), ("lambda_im", 1), ("log_step", 1), ("attn_sinks", 1),
    ("rel_bias", 1), ("b_glu", 1), ("ssm_d", 1), ("loss", 1),
    ("ssm_b_re", 16), ("ssm_b_im", 16), ("ssm_c_re", 16), ("ssm_c_im", 16),
]
_PACK_OFF = {}
_off = 0
for _n, _r in _PACK:
    _PACK_OFF[_n] = (_off, _r)
    _off += _r
PACK_ROWS = -(-_off // 8) * 8


def _to_rows(a, rows):
    flat = a.reshape(-1).astype(F32)
    pad = rows * PACK_W - flat.shape[0]
    if pad:
        flat = jnp.pad(flat, (0, pad))
    return flat.reshape(rows, PACK_W)


def _b_to_rows(b):
    return jnp.transpose(b, (2, 0, 1)).reshape(SSM_P, SSM_H)


def _rows_to_b(r):
    return jnp.transpose(r.reshape(SSM_P, SSM_G, SSM_N), (1, 2, 0))


def _c_to_rows(cm):
    return jnp.transpose(cm, (1, 0, 2)).reshape(SSM_P, SSM_H)


def _rows_to_c(r):
    return jnp.transpose(r.reshape(SSM_P, SSM_G, SSM_N), (1, 0, 2))


def _pack(vals):
    out = jnp.zeros((PACK_ROWS, PACK_W), F32)
    for n, r in _PACK:
        if n in vals:
            pieces = vals[n] if isinstance(vals[n], list) else [vals[n]]
            rows_each = r // len(pieces)
            for i, piece in enumerate(pieces):
                out = lax.dynamic_update_slice(out, _to_rows(piece, rows_each),
                                               (_PACK_OFF[n][0] + i * rows_each, 0))
    return out


def _unpack(packed, name, shape):
    o, r = _PACK_OFF[name]
    n = int(np.prod(shape))
    return packed[o:o + r].reshape(-1)[:n].reshape(shape)


def _small_params_packed(p):
    return {
        "b_ada": p["b_ada"], "norm1_g": p["norm1_g"], "b_in": p["b_in"],
        "norm2_g": p["norm2_g"], "final_g": p["final_g"],
        "lambda_re": p["lambda_re"], "lambda_im": p["lambda_im"],
        "log_step": p["log_step"], "attn_sinks": p["attn_sinks"],
        "rel_bias": p["rel_bias"], "b_glu": p["b_glu"], "ssm_d": p["ssm_d"],
        "ssm_b_re": _b_to_rows(p["ssm_b_re"][0]), "ssm_b_im": _b_to_rows(p["ssm_b_im"][0]),
        "ssm_c_re": _c_to_rows(p["ssm_c_re"][0]), "ssm_c_im": _c_to_rows(p["ssm_c_im"][0]),
    }


_SMALL_SHAPES = {
    "b_ada": (1, N_MOD * D), "norm1_g": (1, D), "b_in": (1, IN_W), "norm2_g": (1, D),
    "final_g": (D,), "lambda_re": (1, SSM_G, SSM_N), "lambda_im": (1, SSM_G, SSM_N),
    "log_step": (1, SSM_G), "attn_sinks": (1, N_Q_HEADS), "rel_bias": (NUM_BUCKETS, N_Q_HEADS),
    "b_glu": (1, SSM_W), "ssm_d": (1, SSM_W),
}


def _unpack_small(packed, name):
    if name in ("ssm_b_re", "ssm_b_im"):
        o, r = _PACK_OFF[name]
        return _rows_to_b(packed[o:o + r])[None]
    if name in ("ssm_c_re", "ssm_c_im"):
        o, r = _PACK_OFF[name]
        return _rows_to_c(packed[o:o + r])[None]
    return _unpack(packed, name, _SMALL_SHAPES[name])


WEIGHT_ORDER = ['w_ada', 'b_ada', 'norm1_g', 'w_in', 'b_in', 'attn_sinks', 'rel_bias', 'lambda_re',
                'lambda_im', 'log_step', 'ssm_b_re', 'ssm_b_im', 'ssm_c_re', 'ssm_c_im', 'ssm_d',
                'w_glu', 'b_glu', 'w_attn_proj', 'w_ssm_proj', 'w_out', 'norm2_g', 'w_ff1', 'w_ff2',
                'final_g']
BIG = ['w_in', 'w_glu', 'w_attn_proj', 'w_ssm_proj', 'w_out', 'w_ff1', 'w_ff2']


ADAMW_TILE_ELEMS = 1 << 18


def _to_col_blocks(w):
    k, n = w.shape
    return jnp.transpose(w.reshape(k, N_DEV, n // N_DEV), (1, 0, 2))


def _adamw_rows(rows, cols):
    tr = rows
    while tr * cols > ADAMW_TILE_ELEMS and tr % 32 == 0:
        tr //= 2
    return tr


def _cast_to_slot(w, me1, name, dep=None):
    rows, cols = w.shape
    tr = min(rows, 256)
    n_dep = 0 if dep is None else 1

    def body(me_ref, w_ref, *rest):
        rest[-1][...] = w_ref[...].astype(BF16)

    return pl.pallas_call(
        body,
        grid_spec=pltpu.PrefetchScalarGridSpec(
            num_scalar_prefetch=1, grid=(rows // tr,),
            in_specs=[pl.BlockSpec((tr, cols), lambda i, me_ref: (i, 0))]
            + [pl.BlockSpec(memory_space=pl.ANY)] * n_dep,
            out_specs=pl.BlockSpec((None, tr, cols), lambda i, me_ref: (me_ref[0], i, 0))),
        out_shape=jax.ShapeDtypeStruct((N_DEV, rows, cols), BF16),
        compiler_params=_cparams(("arbitrary",)),
        name=name,
    )(me1, w, *([dep] if n_dep else []))


def kernel(x, c, w_ada, b_ada, norm1_g, w_in, b_in, attn_sinks, rel_bias, lambda_re, lambda_im, log_step, ssm_b_re, ssm_b_im, ssm_c_re, ssm_c_im, ssm_d, w_glu, b_glu, w_attn_proj, w_ssm_proj, w_out, norm2_g, w_ff1, w_ff2, final_g, loss_target, m_w_ada, m_b_ada, m_norm1_g, m_w_in, m_b_in, m_attn_sinks, m_rel_bias, m_lambda_re, m_lambda_im, m_log_step, m_ssm_b_re, m_ssm_b_im, m_ssm_c_re, m_ssm_c_im, m_ssm_d, m_w_glu, m_b_glu, m_w_attn_proj, m_w_ssm_proj, m_w_out, m_norm2_g, m_w_ff1, m_w_ff2, m_final_g, v_w_ada, v_b_ada, v_norm1_g, v_w_in, v_b_in, v_attn_sinks, v_rel_bias, v_lambda_re, v_lambda_im, v_log_step, v_ssm_b_re, v_ssm_b_im, v_ssm_c_re, v_ssm_c_im, v_ssm_d, v_w_glu, v_b_glu, v_w_attn_proj, v_w_ssm_proj, v_w_out, v_norm2_g, v_w_ff1, v_w_ff2, v_final_g):
    loc = dict(locals())
    W = {n: loc[n] for n in WEIGHT_ORDER}
    Mo = {n: loc["m_" + n] for n in WEIGHT_ORDER}
    Vo = {n: loc["v_" + n] for n in WEIGHT_ORDER}
    S = x.shape[1]
    TM = min(512, S)
    TS = min(1024, S)
    TR = min(256, S)
    TW = min(1024, S)
    TX = min(2048, S)
    me = 4 * lax.axis_index("x") + 2 * lax.axis_index("y") + lax.axis_index("c")
    x2d = x.reshape(S, D)
    tgt = loss_target.reshape(S, D)

    shard = {n: W[n][0] for n in BIG}
    me1 = jnp.reshape(me, (1,)).astype(jnp.int32)
    zone = {"w_in": _cast_to_slot(shard["w_in"], me1, "cast_w_in")}
    c_all = _small_allgather(c, "allgather_c").reshape(N_DEV, D)
    cs = _rowwise(lambda rv, vv, i, nt: ([rv[0] * _sigmoid(rv[0])], []), [_row(c_all)], [],
                  [(D, F32)], [], n_rows=N_DEV, tr=8, ch=8, name="silu_c")[0]
    n_ada = N_MOD * D // N_DEV
    b_ada_cols = lax.dynamic_slice(b_ada, (0, me * n_ada), (1, n_ada))
    mod_piece = _matmul(cs, w_ada[0], mode="nn", dims=(N_DEV, n_ada, D), tiles=(N_DEV, 512, D),
                        out_dtypes=[F32], name="ada_fwd", bias=b_ada_cols)
    mod_all = _small_allgather(mod_piece, "allgather_mod")
    mod_b = lax.dynamic_index_in_dim(mod_all, me, axis=1, keepdims=False).reshape(N_MOD, D)
    sh1, sc1, g1, sh2, sc2, g2 = [mod_b[i:i + 1] for i in range(N_MOD)]

    (in_flight,), tok_in = _relay_gather_start([zone["w_in"]], "w_in_start", mod_all)
    for n in BIG[1:]:
        zone[n] = _cast_to_slot(shard[n], me1, "cast_" + n, dep=tok_in)
    G = {}

    buckets = _t5_buckets_block()
    band = _band_mask()
    onehot_t = jnp.asarray(
        (np.arange(128)[:, None] == buckets.reshape(-1)[None, :]).astype(np.float32), BF16)
    band_first = band & (np.arange(2 * BLK)[None, :] >= BLK)
    rel_bias_t = jnp.pad(jnp.transpose(rel_bias), ((0, 0), (0, 128 - NUM_BUCKETS)))
    bias2 = _bias_tables(rel_bias_t, onehot_t,
                         jnp.asarray(band_first.reshape(1, -1).astype(np.float32)),
                         jnp.asarray(band.reshape(1, -1).astype(np.float32))
                         ).reshape(2, N_Q_HEADS * BLK, 2 * BLK)
    sinkcol = jnp.repeat(attn_sinks.reshape(N_Q_HEADS), BLK).reshape(N_Q_HEADS * BLK, 1)
    lam_re = lambda_re.reshape(1, SSM_H)
    lam_im = lambda_im.reshape(1, SSM_H)
    ls_x = jnp.repeat(log_step.reshape(SSM_G), SSM_N).reshape(1, SSM_H)
    btr, bti = _b_to_rows(ssm_b_re[0]), _b_to_rows(ssm_b_im[0])
    ctr, cti = _c_to_rows(ssm_c_re[0]), _c_to_rows(ssm_c_im[0])
    abar, bcat, ccat = _ssm_setup(lam_re, lam_im, ls_x, btr, bti, ctr, cti)
    wp, mp, vp = [_pack(_small_params_packed(p)) for p in (W, Mo, Vo)]

    def f_norm1(rv, vv, i, nt):
        xv, (g, sc, sh) = rv[0], vv
        return [(xv * _rms(xv) * g) * (1.0 + sc) + sh], []

    h = _rowwise(f_norm1, [_row(x2d)], [norm1_g, sc1, sh1], [(D, BF16)], [],
                 n_rows=S, tr=TR, ch=32, name="norm1_fwd", dep=zone["w_ff2"])[0]
    (zone_in,) = _relay_gather_arrive([in_flight], [h, bias2, bcat, wp, mp, vp] + [zone[n] for n in BIG[1:]], "w_in_arrive")
    (in_pass,), tok_p = _relay_pass_start([zone_in], "w_in_pass_start")
    mixer = ["w_attn_proj", "w_glu", "w_ssm_proj", "w_out"]
    later_flights, tok_w = _relay_gather_start(
        [zone[n] for n in mixer] + [zone["w_ff1"], zone["w_ff2"]], "weights_start", tok_p)
    mixer_flights, ff_flights = later_flights[:len(mixer)], later_flights[len(mixer):]
    (G["w_in"],) = _relay_pass_wait([in_pass], tok_w, "w_in_pass_wait")
    proj = _matmul(h, G["w_in"], mode="nn", dims=(S, IN_W, D), tiles=(TX, 768, D),
                   out_dtypes=[BF16], name="in_proj", b3=True, bias=b_in, dep=tok_w)

    attn = _attention_fwd(proj, bias2, sinkcol, S)
    mixer_zones = _relay_gather_arrive(mixer_flights, attn, "mixer_weights_arrive")
    mixer_pass, _ = _relay_pass_start(mixer_zones, "mixer_weights_pass_start")

    u_blk = (ATTN_W + 2 * KV_W) // SSM_W
    xs, xprev, yc = _scan_fwd(proj, u_blk, bcat, ccat, abar, S)

    def f_ssm_out(rv, vv, i, nt):
        y = rv[0] + vv[0] * rv[1]
        return [y, _gelu(y)], []

    y_ssm_pre, z = _rowwise(f_ssm_out, [_row(yc), _row(proj, u_blk, SSM_W)], [ssm_d],
                            [(SSM_W, F32), (SSM_W, BF16)], [], n_rows=S, tr=TM, ch=32, name="ssm_out")
    G.update(zip(mixer, _relay_pass_wait(mixer_pass, z, "mixer_weights_pass_wait")))
    w_glu_f = G["w_glu"].reshape(SSM_W, SSM_W)
    w_out_f = G["w_out"].reshape(D, D)
    w_ap_f = jnp.transpose(G["w_attn_proj"], (1, 0, 2)).reshape(ATTN_W, D)
    w_sp_f = jnp.transpose(G["w_ssm_proj"], (1, 0, 2)).reshape(SSM_W, D)
    y_attn = _matmul(attn, w_ap_f, mode="nn", dims=(S, D, ATTN_W), tiles=(TW, 1024, ATTN_W),
                     out_dtypes=[BF16], name="attn_proj")
    zg = _matmul(z, w_glu_f, mode="nn", dims=(S, SSM_W, SSM_W), tiles=(TM, SSM_W, SSM_W),
                 out_dtypes=[F32], name="glu_proj", bias=b_glu)
    z2 = _rowwise(lambda rv, vv, i, nt: ([rv[0].astype(F32) * _sigmoid(rv[1])], []),
                  [_row(z), _row(zg)], [], [(SSM_W, BF16)], [], n_rows=S, tr=TM, ch=32, name="glu_gate")[0]
    y_ssm = _matmul(z2, w_sp_f, mode="nn", dims=(S, D, SSM_W), tiles=(TW, 1024, SSM_W),
                    out_dtypes=[BF16], name="ssm_proj")

    ga_row = _row(proj, 1, D)
    gs_row = _row(proj, 2, D)

    def f_merge(rv, vv, i, nt):
        ga, gs, ya, ys = rv
        return [_sigmoid(ga) * ya + _sigmoid(gs) * ys], []

    merged = _rowwise(f_merge, [ga_row, gs_row, _row(y_attn), _row(y_ssm)], [], [(D, BF16)], [],
                      n_rows=S, tr=TR, ch=32, name="merge")[0]
    mo = _matmul(merged, w_out_f, mode="nn", dims=(S, D, D), tiles=(TW, 1024, D),
                 out_dtypes=[BF16], name="out_proj")

    ff_zones = _relay_gather_arrive(ff_flights, mo, "ff_weights_arrive")
    ff_pass, tok_fp = _relay_pass_start(ff_zones, "ff_weights_pass_start")

    def f_norm2(rv, vv, i, nt):
        xv, mv = rv
        g1v, g, sc, sh = vv
        x1v = xv + g1v * mv
        return [x1v, (x1v * _rms(x1v) * g) * (1.0 + sc) + sh], []

    x1, h2 = _rowwise(f_norm2, [_row(x2d), _row(mo)], [g1, norm2_g, sc2, sh2],
                      [(D, F32), (D, BF16)], [], n_rows=S, tr=TR, ch=32, name="norm2_fwd", dep=tok_fp)

    def relu_sq(acc):
        r = jnp.maximum(acc, 0.0)
        return r * r, r

    (G["w_ff1"],) = _relay_pass_wait(ff_pass[:1], h2, "w_ff1_pass_wait")
    act, relu = _matmul(h2, G["w_ff1"], mode="nn", dims=(S, D_FF, D), tiles=(TX, 1024, D),
                        out_dtypes=[BF16, BF16], name="ff1", b3=True, epilogue=relu_sq)
    w_ff2_f = _relay_pass_wait(ff_pass[1:], act, "w_ff2_pass_wait")[0].reshape(D_FF, D)
    ff = _matmul(act, w_ff2_f, mode="nn", dims=(S, D, D_FF), tiles=(TX, 1024, 2048),
                 out_dtypes=[BF16], name="ff2")

    def f_loss(rv, vv, i, nt):
        x1v, ffv, tv = rv
        g2v, gf = vv
        x2v = x1v + g2v * ffv
        r = _rms(x2v)
        xh = x2v * r
        diff = xh * gf - tv
        dy = diff * (1.0 / D)
        dxh = dy * gf
        dx2 = r * (dxh - xh * jnp.mean(dxh * xh, axis=-1, keepdims=True))
        return [dx2, dx2 * g2v], [_colsum(0.5 * diff * diff * (1.0 / D)), _colsum(dy * xh),
                                  _colsum(dx2 * ffv)]

    dx2, dff, loss_cols, d_final_g, dg2 = _rowwise(
        f_loss, [_row(x1), _row(ff), _row(tgt)], [g2, final_g.reshape(1, D)],
        [(D, F32), (D, BF16)], [(1, D)] * 3, n_rows=S, tr=TR, ch=32, name="loss_bwd")

    df1 = _matmul(dff, w_ff2_f, mode="nt", dims=(S, D_FF, D), tiles=(TX, 1024, D),
                  out_dtypes=[BF16], name="ff2_dgrad", extras=(relu,),
                  epilogue=lambda acc, r: (acc * (2.0 * r.astype(F32)),))
    gw_ff2 = _matmul(act, dff, mode="tn", dims=(D_FF, D, S), tiles=(2048, 1024, TS),
                     out_dtypes=[BF16], name="ff2_wgrad").reshape(N_DEV, D_FF // N_DEV, D)
    g_flight = {}
    (g_flight["w_ff2"],), tok = _exchange_start([gw_ff2], "scatter", "grads_start_ff2")
    dh2 = _matmul(df1, G["w_ff1"], mode="nt", dims=(S, D, D_FF), tiles=(TW, D, 1024),
                  out_dtypes=[BF16], name="ff1_dgrad", b3=True, dep=tok)
    gw_ff1 = _matmul(h2, df1, mode="tn", dims=(D, D_FF, S), tiles=(2048, 1024, TS),
                     out_dtypes=[BF16], name="ff1_wgrad", out3=True)
    (g_flight["w_ff1"],), tok = _exchange_start([gw_ff1], "scatter", "grads_start_ff1")

    def f_norm2_bwd(rv, vv, i, nt):
        x1v, dh, dx2v, mv = rv
        g, sc, g1v = vv
        r = _rms(x1v)
        xh = x1v * r
        t = xh * g
        dt = dh * (1.0 + sc)
        dxh = dt * g
        dx1 = dx2v + r * (dxh - xh * jnp.mean(dxh * xh, axis=-1, keepdims=True))
        return [dx1, dx1 * g1v], [_colsum(dh), _colsum(dh * t), _colsum(dt * xh), _colsum(dx1 * mv)]

    dx1, dmo, dsh2, dsc2, d_norm2_g, dg1 = _rowwise(
        f_norm2_bwd, [_row(x1), _row(dh2), _row(dx2), _row(mo)], [norm2_g, sc2, g1],
        [(D, F32), (D, BF16)], [(1, D)] * 4, n_rows=S, tr=TR, ch=16, name="norm2_bwd", dep=tok)

    dmerged = _matmul(dmo, w_out_f, mode="nt", dims=(S, D, D), tiles=(TW, 1024, D),
                      out_dtypes=[BF16], name="out_dgrad")
    gw_out = _matmul(merged, dmo, mode="tn", dims=(D, D, S), tiles=(2048, 1024, TS),
                     out_dtypes=[BF16], name="out_wgrad").reshape(N_DEV, D // N_DEV, D)
    (g_flight["w_out"],), tok = _exchange_start([gw_out], "scatter", "grads_start_out")

    def f_merge_bwd(rv, vv, i, nt):
        dm, ga, gs, ya, ys = rv
        sa, ss = _sigmoid(ga), _sigmoid(gs)
        return [dm * sa, dm * ss, dm * ya * sa * (1.0 - sa), dm * ys * ss * (1.0 - ss)], []

    dy_attn, dy_ssm, dga, dgs = _rowwise(
        f_merge_bwd, [_row(dmerged), ga_row, gs_row, _row(y_attn), _row(y_ssm)], [],
        [(D, BF16)] * 4, [], n_rows=S, tr=TR, ch=16, name="merge_bwd", dep=tok)

    dz2 = _matmul(dy_ssm, w_sp_f, mode="nt", dims=(S, SSM_W, D), tiles=(TW, SSM_W, D),
                  out_dtypes=[F32], name="ssm_proj_dgrad")
    gw_ssm_proj = _to_col_blocks(_matmul(z2, dy_ssm, mode="tn", dims=(SSM_W, D, S), tiles=(SSM_W, 1024, TS),
                                         out_dtypes=[BF16], name="ssm_proj_wgrad"))

    def f_glu_bwd(rv, vv, i, nt):
        dz2v, zv, zgv = rv
        sg = _sigmoid(zgv)
        dzg = dz2v * zv.astype(F32) * sg * (1.0 - sg)
        return [dzg, dz2v * sg], [_colsum(dzg)]

    dzg, dz_a, d_b_glu = _rowwise(f_glu_bwd, [_row(dz2), _row(z), _row(zg)], [],
                                  [(SSM_W, BF16), (SSM_W, F32)], [(1, SSM_W)],
                                  n_rows=S, tr=TM, ch=32, name="glu_bwd")
    dz_b = _matmul(dzg, w_glu_f, mode="nt", dims=(S, SSM_W, SSM_W), tiles=(TM, SSM_W, SSM_W),
                   out_dtypes=[F32], name="glu_dgrad")
    gw_glu = _matmul(z, dzg, mode="tn", dims=(SSM_W, SSM_W, S), tiles=(SSM_W, SSM_W, TS),
                     out_dtypes=[BF16], name="glu_wgrad").reshape(N_DEV, SSM_W // N_DEV, SSM_W)
    (g_flight["w_ssm_proj"], g_flight["w_glu"]), tok = _exchange_start(
        [gw_ssm_proj, gw_glu], "scatter", "grads_start_ssm")

    def f_ssm_out_bwd(rv, vv, i, nt):
        dza, dzb, yv, uv = rv
        dy = (dza + dzb) * _gelu_grad(yv)
        return [dy, dy * vv[0]], [_colsum(dy * uv)]

    dy_s, du_a, d_ssm_d = _rowwise(
        f_ssm_out_bwd, [_row(dz_a), _row(dz_b), _row(y_ssm_pre), _row(proj, u_blk, SSM_W)], [ssm_d],
        [(SSM_W, BF16), (SSM_W, F32)], [(1, SSM_W)], n_rows=S, tr=TM, ch=32, name="ssm_out_bwd", dep=tok)
    hw = SSM_W // 2
    u_half = (ATTN_W + 2 * KV_W) // hw
    dccat = _matmul(dy_s, xs, mode="tn", dims=(hw, 2 * SSM_H, S), tiles=(hw, 1024, TS),
                    out_dtypes=[F32], name="ssm_c_wgrad", a_index=lambda i, j, k: (k, j % 2))
    hs, dacc, du_b = _scan_bwd(dy_s, xprev, bcat, ccat, abar, S)
    dbcat = _matmul(proj, hs, mode="tn", dims=(hw, 2 * SSM_H, S), tiles=(hw, 1024, TS),
                    out_dtypes=[F32], name="ssm_b_wgrad", a_index=lambda i, j, k: (k, u_half + j % 2))
    grp = np.arange(SSM_H) // SSM_N
    gind = jnp.asarray((grp[:, None] == np.arange(128)[None, :]).astype(np.float32), BF16)
    d_lam_re, d_lam_im, d_ls, d_btr, d_bti, d_ctr, d_cti = _ssm_param_bwd(
        lam_re, lam_im, ls_x, btr, bti, dacc, dbcat, dccat, gind)

    dattn = _matmul(dy_attn, w_ap_f, mode="nt", dims=(S, ATTN_W, D), tiles=(TW, ATTN_W, D),
                    out_dtypes=[BF16], name="attn_proj_dgrad")
    gw_attn_proj = _to_col_blocks(_matmul(attn, dy_attn, mode="tn", dims=(ATTN_W, D, S), tiles=(ATTN_W, 1024, TS),
                                          out_dtypes=[BF16], name="attn_proj_wgrad"))
    (g_flight["w_attn_proj"],), tok = _exchange_start(
        [gw_attn_proj], "scatter", "grads_start_attn")
    dq, dkc, dkp, dvc, dvp, dbias, dsink = _attention_bwd(proj, attn, dattn, bias2, sinkcol, S)
    d_bias_b, d_sinks = _bucket_reduce(dbias.reshape(N_Q_HEADS, BLK * 2 * BLK),
                                       dsink.reshape(N_Q_HEADS, BLK), onehot_t)

    def f_dproj(rv, vv, i, nt):
        dqv, kc, kp, vc, vp, dua, dub, gav, gsv = rv
        keep = (i < nt - 1).astype(F32)
        dp = jnp.concatenate([dqv.astype(F32), kc + keep * kp, vc + keep * vp, dua + dub,
                              gav.astype(F32), gsv.astype(F32)], axis=-1)
        return [dp], [_colsum(dp)]

    dproj, d_b_in = _rowwise(
        f_dproj, [_row(dq), _row(dkc), _row(dkp, shift=1), _row(dvc), _row(dvp, shift=1),
                  _row(du_a), _row(du_b), _row(dga), _row(dgs)], [],
        [(IN_W, BF16)], [(1, IN_W)], n_rows=S, tr=BLK, ch=16, name="dproj", dep=tok)
    gw_in = _matmul(h, dproj, mode="tn", dims=(D, IN_W, S), tiles=(2048, 768, TS),
                    out_dtypes=[BF16], name="in_wgrad", out3=True)
    (g_flight["w_in"],), tok = _exchange_start([gw_in], "scatter", "grads_start_in")
    dh = _matmul(dproj, G["w_in"], mode="nt", dims=(S, D, IN_W), tiles=(TW, D, 768),
                 out_dtypes=[BF16], name="in_dgrad", b3=True, dep=tok)

    def f_norm1_bwd(rv, vv, i, nt):
        xv, dhv, dx1v = rv
        g, sc = vv
        r = _rms(xv)
        xh = xv * r
        t = xh * g
        dt = dhv * (1.0 + sc)
        dxh = dt * g
        dxv = dx1v + r * (dxh - xh * jnp.mean(dxh * xh, axis=-1, keepdims=True))
        return [dxv], [_colsum(dhv), _colsum(dhv * t), _colsum(dt * xh)]

    grad_x, dsh1, dsc1, d_norm1_g = _rowwise(
        f_norm1_bwd, [_row(x2d), _row(dh), _row(dx1)], [norm1_g, sc1],
        [(D, F32)], [(1, D)] * 3, n_rows=S, tr=TR, ch=32, name="norm1_bwd")

    part = _pack({
        "b_ada": [dsh1, dsc1, dg1, dsh2, dsc2, dg2], "norm1_g": d_norm1_g, "b_in": d_b_in, "norm2_g": d_norm2_g,
        "final_g": d_final_g, "lambda_re": d_lam_re, "lambda_im": d_lam_im,
        "log_step": d_ls[0, :SSM_G], "attn_sinks": d_sinks[:, 0],
        "rel_bias": jnp.transpose(d_bias_b[:, :NUM_BUCKETS]), "b_glu": d_b_glu, "ssm_d": d_ssm_d,
        "loss": loss_cols, "ssm_b_re": d_btr, "ssm_b_im": d_bti, "ssm_c_re": d_ctr, "ssm_c_im": d_cti,
    })
    zone_small = lax.dynamic_update_slice(lax.empty((N_DEV, PACK_ROWS, PACK_W), F32), part[None], (me, 0, 0))
    (small_flight,), after = _exchange_start([zone_small], "gather", "small_grads_start")

    big_out = {}
    for n in ["w_ff2", "w_ff1", "w_out", "w_ssm_proj", "w_glu", "w_attn_proj", "w_in"]:
        own, recv = _exchange_wait([g_flight[n]], "scatter", after, "grads_wait_" + n[2:])[0]
        rows, cols = shard[n].shape
        parts = [(own, lambda m: m[0])] + [
            (recv, lambda m, j=j: jnp.where(j >= m[0], j + 1, j)) for j in range(N_DEV - 1)]
        big_out[n] = _adamw(parts, shard[n], Mo[n][0], Vo[n][0], tr=_adamw_rows(rows, cols), ch=16,
                            name="adamw_" + n, prefetch=me1)
        after = big_out[n][0]

    part_all = _exchange_wait([small_flight], "gather", after, "small_grads_wait")[0][0]
    sg, sdelta, sm, sv = _adamw([(part_all, d) for d in range(N_DEV)], wp, mp, vp,
                                tr=PACK_ROWS, ch=8, name="adamw_small")
    lo, _ = _PACK_OFF["loss"]
    loss = jnp.sum(sg[lo])

    o_ada, _ = _PACK_OFF["b_ada"]
    dmod_all = part_all[:, o_ada:o_ada + N_MOD, :].reshape(N_DEV, N_MOD * D)
    dmod_cols = lax.dynamic_slice(dmod_all, (0, me * n_ada), (N_DEV, n_ada))
    gw_ada = _matmul(cs, dmod_cols, mode="tn", dims=(D, n_ada, N_DEV), tiles=(D, 512, N_DEV),
                     out_dtypes=[F32], name="ada_wgrad")
    big_out["w_ada"] = _adamw([(gw_ada, 0)], w_ada[0], m_w_ada[0], v_w_ada[0],
                              tr=_adamw_rows(D, n_ada), ch=16, name="adamw_w_ada")

    def leaf(kind, n):
        if n in big_out:
            return big_out[n][kind][None]
        return _unpack_small((sg, sdelta, sm, sv)[kind], n)

    outs = [loss, grad_x.reshape(1, S, D)]
    for kind in range(4):
        outs.extend(leaf(kind, n) for n in WEIGHT_ORDER)
    return tuple(outs)
```

```python
import functools
import math

import numpy as np
import jax
import jax.numpy as jnp
from jax import lax
from jax.experimental import pallas as pl
from jax.experimental.pallas import tpu as pltpu

F32 = jnp.float32
BF16 = jnp.bfloat16
MESH = pl.DeviceIdType.MESH

N_DEV = 8
D = 2048
HEAD_DIM = 64
N_Q_HEADS = 16
N_KV_HEADS = 4
GROUP = N_Q_HEADS // N_KV_HEADS
ATTN_W = N_Q_HEADS * HEAD_DIM
KV_W = N_KV_HEADS * HEAD_DIM
BLK = 128
NUM_BUCKETS = 32
MAX_DISTANCE = 128
NEG_INF = -1e30
SSM_W = 512
SSM_P = 16
SSM_G = 32
SSM_N = 64
SSM_H = SSM_G * SSM_N
D_FF = 4 * D
IN_W = ATTN_W + 2 * KV_W + SSM_W + 2 * D
N_MOD = 6
EPS = 1e-6

ADAM_LR = 0.001
ADAM_B1 = 0.9
ADAM_B2 = 0.999
ADAM_EPS = 1e-08
ADAM_WD = 0.01
ADAM_STEP = 10

VMEM_LIMIT = 56 * 1024 * 1024
PACK_W = 2048


def _cparams(sem):
    return pltpu.CompilerParams(dimension_semantics=sem, vmem_limit_bytes=VMEM_LIMIT)


def _matmul(a, b, *, mode, dims, tiles, out_dtypes, name, a_off=0, b3=False,
            out3=False, bias=None, extras=(), epilogue=None, dep=None, a_index=None, b_index=None):
    M, N, K = dims
    tm, tn, tk = tiles
    assert M % tm == 0 and N % tn == 0 and K % tk == 0, (name, dims, tiles)
    gm, gn, gk = M // tm, N // tn, K // tk
    n_extra = len(extras)
    has_bias = bias is not None
    n_out = len(out_dtypes)

    if mode == "nn":
        a_spec = pl.BlockSpec((tm, tk), lambda i, j, k: (i, a_off + k))
        if b3:
            nb = (N // N_DEV) // tn
            assert nb * tn * N_DEV == N
            b_spec = pl.BlockSpec((None, tk, tn), lambda i, j, k: (j // nb, k, j % nb))
        else:
            b_spec = pl.BlockSpec((tk, tn), lambda i, j, k: (k, j))
        dn = (((1,), (0,)), ((), ()))
    elif mode == "nt":
        a_spec = pl.BlockSpec((tm, tk), lambda i, j, k: (i, a_off + k))
        if b3 and tk > K // N_DEV:
            shard_w = K // N_DEV
            shards_per_step = tk // shard_w
            assert shards_per_step * shard_w == tk
            b_spec = pl.BlockSpec((shards_per_step, tn, shard_w), lambda i, j, k: (k, j, 0))
        elif b3:
            nb = (K // N_DEV) // tk
            assert nb * tk * N_DEV == K
            b_spec = pl.BlockSpec((None, tn, tk), lambda i, j, k: (k // nb, j, k % nb))
        else:
            b_spec = pl.BlockSpec((tn, tk), lambda i, j, k: (j, k))
        dn = (((1,), (1,)), ((), ()))
    else:
        a_spec = pl.BlockSpec((tk, tm), lambda i, j, k: (k, a_off + i))
        b_spec = pl.BlockSpec((tk, tn), lambda i, j, k: (k, j))
        dn = (((0,), (0,)), ((), ()))
    if a_index is not None:
        a_spec = pl.BlockSpec(a_spec.block_shape, a_index)
    if b_index is not None:
        b_spec = pl.BlockSpec(b_spec.block_shape, b_index)

    if out3:
        nbo = (N // N_DEV) // tn
        assert nbo * tn * N_DEV == N
        o_spec = pl.BlockSpec((None, tm, tn), lambda i, j, k: (j // nbo, i, j % nbo))
        o_shape = (N_DEV, M, N // N_DEV)
    else:
        o_spec = pl.BlockSpec((tm, tn), lambda i, j, k: (i, j))
        o_shape = (M, N)

    in_specs = [a_spec, b_spec]
    args = [a, b]
    if has_bias:
        in_specs.append(pl.BlockSpec((1, tn), lambda i, j, k: (0, j)))
        args.append(bias)
    for e in extras:
        in_specs.append(pl.BlockSpec((tm, tn), lambda i, j, k: (i, j)))
        args.append(e)
    n_dep = 0 if dep is None else 1
    if n_dep:
        in_specs.append(pl.BlockSpec(memory_space=pl.ANY))
        args.append(dep)

    def body(*refs):
        a_ref, b_ref = refs[0], refs[1]
        pos = 2
        bias_ref = None
        if has_bias:
            bias_ref = refs[pos]
            pos += 1
        extra_refs = refs[pos:pos + n_extra]
        pos += n_extra + n_dep
        out_refs = refs[pos:pos + n_out]
        acc_ref = refs[pos + n_out] if gk > 1 else None

        if len(b_spec.block_shape) == 3 and b_spec.block_shape[0] is not None:
            sw = b_spec.block_shape[2]
            part = functools.reduce(lambda x, y: x + y, [
                lax.dot_general(a_ref[:, s * sw:(s + 1) * sw].astype(BF16), b_ref[s].astype(BF16), dn,
                                preferred_element_type=F32) for s in range(b_spec.block_shape[0])])
        else:
            part = lax.dot_general(a_ref[...].astype(BF16), b_ref[...].astype(BF16), dn,
                                   preferred_element_type=F32)

        def finish(acc):
            if has_bias:
                acc = acc + bias_ref[...]
            if epilogue is None:
                vals = (acc,)
            else:
                vals = epilogue(acc, *[e[...] for e in extra_refs])
            for o_ref, val in zip(out_refs, vals):
                o_ref[...] = val.astype(o_ref.dtype)

        if gk == 1:
            finish(part)
        else:
            k = pl.program_id(2)

            @pl.when(k == 0)
            def _():
                acc_ref[...] = part

            @pl.when(k > 0)
            def _():
                acc_ref[...] += part

            @pl.when(k == gk - 1)
            def _():
                finish(acc_ref[...])

    outs = pl.pallas_call(
        body,
        grid=(gm, gn, gk),
        in_specs=in_specs,
        out_specs=[o_spec] * n_out,
        out_shape=[jax.ShapeDtypeStruct(o_shape, dt) for dt in out_dtypes],
        scratch_shapes=([pltpu.VMEM((tm, tn), F32)] if gk > 1 else []),
        compiler_params=_cparams(("parallel", "parallel", "arbitrary")),
        name=name,
    )(*args)
    return outs[0] if n_out == 1 else outs


def _rowwise(fn, rows, vecs, row_outs, sum_outs, *, n_rows, tr, ch, name, dep=None, prefetch=None):
    assert n_rows % tr == 0 and tr % ch == 0
    nt = n_rows // tr
    nr, nv, nro, nso = len(rows), len(vecs), len(row_outs), len(sum_outs)
    in_specs, args = [], []
    n_pf = 0 if prefetch is None else 1
    for (arr, lead, cblk, w, shift) in rows:
        if shift:
            ridx = lambda i, shift=shift: jnp.minimum(i + shift, nt - 1)
        else:
            ridx = lambda i: i
        if arr.ndim == 3:
            def imap(i, *pf, lead=lead, cblk=cblk, ridx=ridx):
                return (lead(pf[0]) if callable(lead) else lead, ridx(i), cblk)
            in_specs.append(pl.BlockSpec((None, tr, w), imap))
        else:
            in_specs.append(pl.BlockSpec(
                (tr, w), lambda i, *pf, cblk=cblk, ridx=ridx: (ridx(i), cblk)))
        args.append(arr)
    for v in vecs:
        in_specs.append(pl.BlockSpec(v.shape, lambda i, *pf, nd=v.ndim: (0,) * nd))
        args.append(v)
    n_dep = 0 if dep is None else 1
    if n_dep:
        in_specs.append(pl.BlockSpec(memory_space=pl.ANY))
        args.append(dep)
    out_specs = [pl.BlockSpec((tr, w), lambda i, *pf: (i, 0)) for (w, _) in row_outs]
    out_shape = [jax.ShapeDtypeStruct((n_rows, w), dt) for (w, dt) in row_outs]
    for (r, w) in sum_outs:
        out_specs.append(pl.BlockSpec((r, w), lambda i, *pf: (0, 0)))
        out_shape.append(jax.ShapeDtypeStruct((r, w), F32))

    def body(*refs):
        refs = refs[n_pf:]
        i = pl.program_id(0)
        r_in = refs[:nr]
        v_in = refs[nr:nr + nv]
        r_out = refs[nr + nv + n_dep:nr + nv + n_dep + nro]
        s_out = refs[nr + nv + n_dep + nro:]
        s_out, s_acc = s_out[:nso], s_out[nso:]
        if nso:
            @pl.when(i == 0)
            def _():
                for s in s_acc:
                    s[...] = jnp.zeros(s.shape, F32)
        vvals = [v[...] for v in v_in]

        def chunk(ci, carry):
            r0 = pl.multiple_of(ci * ch, ch)
            rv = [r[pl.ds(r0, ch), :].astype(F32) for r in r_in]
            pieces = [fn([v[8 * k:8 * (k + 1)] for v in rv], vvals, i, nt) for k in range(ch // 8)]
            for j, ref in enumerate(r_out):
                val = jnp.concatenate([ro[j] for ro, _ in pieces], axis=0) if ch > 8 else pieces[0][0][j]
                ref[pl.ds(r0, ch), :] = val.astype(ref.dtype)
            for j, ref in enumerate(s_acc):
                ref[...] += functools.reduce(lambda a, b: a + b, [so[j] for _, so in pieces])
            return carry

        lax.fori_loop(0, tr // ch, chunk, 0)
        if nso:
            @pl.when(i == nt - 1)
            def _():
                for s, acc in zip(s_out, s_acc):
                    s[...] = jnp.sum(acc[...], axis=0, keepdims=True)

    outs = pl.pallas_call(
        body,
        grid_spec=pltpu.PrefetchScalarGridSpec(
            num_scalar_prefetch=n_pf, grid=(nt,), in_specs=in_specs, out_specs=out_specs,
            scratch_shapes=[pltpu.VMEM((8, w), F32) for (_, w) in sum_outs]),
        out_shape=out_shape,
        compiler_params=_cparams(("arbitrary",)),
        name=name,
    )(*([prefetch] if n_pf else []), *args)
    return outs


def _row(arr, cblk=0, w=None, lead=0, shift=0):
    return (arr, lead, cblk, arr.shape[-1] if w is None else w, shift)


def _colsum(v):
    parts = [v[8 * k:8 * (k + 1)] for k in range(v.shape[0] // 8)]
    return functools.reduce(lambda a, b: a + b, parts)


def _rms(x):
    return lax.rsqrt(jnp.mean(x * x, axis=-1, keepdims=True) + EPS)


def _sigmoid(x):
    return 1.0 / (1.0 + jnp.exp(-x))


_GELU_C = math.sqrt(2.0 / math.pi)


def _gelu(x):
    return 0.5 * x * (1.0 + jnp.tanh(_GELU_C * (x + 0.044715 * (x * x * x))))


def _gelu_grad(x):
    t = jnp.tanh(_GELU_C * (x + 0.044715 * (x * x * x)))
    return 0.5 * (1.0 + t) + 0.5 * x * (1.0 - t * t) * (_GELU_C * (1.0 + 3.0 * 0.044715 * (x * x)))


def _my_pos():
    return lax.axis_index("x"), lax.axis_index("y"), lax.axis_index("c")


def _flip(pos, k):
    x, y, c = pos
    return (1 - x if k & 4 else x, 1 - y if k & 2 else y, 1 - c if k & 1 else c)


def _dev_id(pos):
    return 4 * pos[0] + 2 * pos[1] + pos[2]


def _small_allgather(x, name):
    r, c = x.shape

    def body(x_ref, out_ref, send_sems, recv_sems):
        me = _my_pos()
        out_ref[_dev_id(me)] = x_ref[...]
        copies = []
        for k in range(1, N_DEV):
            cp = pltpu.make_async_remote_copy(
                src_ref=x_ref, dst_ref=out_ref.at[_dev_id(me)],
                send_sem=send_sems.at[k - 1], recv_sem=recv_sems.at[k - 1],
                device_id=_flip(me, k), device_id_type=MESH)
            cp.start()
            copies.append(cp)
        for k in range(1, N_DEV):
            peer = _flip(me, k)
            pltpu.make_async_remote_copy(
                src_ref=x_ref, dst_ref=out_ref.at[_dev_id(peer)],
                send_sem=send_sems.at[k - 1], recv_sem=recv_sems.at[k - 1],
                device_id=peer, device_id_type=MESH).wait_recv()
        for cp in copies:
            cp.wait_send()

    return pl.pallas_call(
        body,
        out_shape=jax.ShapeDtypeStruct((N_DEV, r, c), x.dtype),
        in_specs=[pl.BlockSpec(memory_space=pltpu.VMEM)],
        out_specs=pl.BlockSpec(memory_space=pltpu.VMEM),
        scratch_shapes=[pltpu.SemaphoreType.DMA((N_DEV - 1,)),
                        pltpu.SemaphoreType.DMA((N_DEV - 1,))],
        compiler_params=pltpu.CompilerParams(vmem_limit_bytes=VMEM_LIMIT),
        name=name,
    )(x)


_HBM = pl.BlockSpec(memory_space=pltpu.HBM)
_SEM = pl.BlockSpec(memory_space=pltpu.SEMAPHORE)
_EFFECT = pltpu.SideEffectType.DATAFLOW_SIDE_EFFECTING


def _relay_copy(zone, send_sems, recv_sems, k, block, to):
    slot = zone.at[_dev_id(block)]
    return pltpu.make_async_remote_copy(
        src_ref=slot, dst_ref=slot, send_sem=send_sems.at[k], recv_sem=recv_sems.at[k],
        device_id=to, device_id_type=MESH)


def _relay_peers():
    x, y, c = _my_pos()
    return (x, y, c), (x, y, 1 - c), [(1 - x, y), (x, 1 - y), (1 - x, 1 - y)]


def _relay_start_call(zones, n_sems, issue, name, after=None):
    n = len(zones)
    n_after = 0 if after is None else 1

    def body(*refs):
        refs = refs[:n] + refs[n + n_after:]
        send, recv, token = refs[n:2 * n], refs[2 * n:3 * n], refs[4 * n]
        for a in range(n):
            issue(refs[a], send[a], recv[a])
        token[...] = jnp.zeros(token.shape, token.dtype)

    sem = pltpu.SemaphoreType.DMA((n_sems,))
    outs = pl.pallas_call(
        body,
        name=name,
        out_shape=([sem] * (2 * n) + [pltpu.HBM(z.shape, z.dtype) for z in zones]
                   + [jax.ShapeDtypeStruct((8, 128), F32)]),
        in_specs=[_HBM] * n + [pl.BlockSpec(memory_space=pl.ANY)] * n_after,
        out_specs=[_SEM] * (2 * n) + [_HBM] * n + [pl.BlockSpec(memory_space=pltpu.VMEM)],
        input_output_aliases={a: 2 * n + a for a in range(n)},
        compiler_params=pltpu.CompilerParams(has_side_effects=_EFFECT),
    )(*[pltpu.with_memory_space_constraint(z, pltpu.HBM) for z in zones],
      *([after] if n_after else []))
    return [(outs[a], outs[n + a], outs[2 * n + a]) for a in range(n)], outs[3 * n]


def _relay_wait_call(flights, settle, after, name):
    n = len(flights)
    after = list(after) if isinstance(after, (list, tuple)) else [after]

    def body(*refs):
        send, recv = refs[n:2 * n], refs[2 * n:3 * n]
        for a in range(n):
            settle(refs[a], send[a], recv[a])

    outs = pl.pallas_call(
        body,
        name=name,
        out_shape=[pltpu.HBM(f[2].shape, f[2].dtype) for f in flights],
        in_specs=[_HBM] * n + [_SEM] * (2 * n) + [pl.BlockSpec(memory_space=pl.ANY)] * len(after),
        out_specs=[_HBM] * n,
        input_output_aliases={a: a for a in range(n)},
        compiler_params=pltpu.CompilerParams(has_side_effects=_EFFECT),
    )(*[f[2] for f in flights], *[f[0] for f in flights], *[f[1] for f in flights], *after)
    return list(outs)


def _relay_gather_start(zones, name, after=None):
    def issue(zone, send, recv):
        me, sib, chips = _relay_peers()
        _relay_copy(zone, send, recv, 0, me, sib).start()
        for j, chip in enumerate(chips):
            _relay_copy(zone, send, recv, 1 + j, me, (*chip, me[2])).start()
    return _relay_start_call(zones, 4, issue, name, after)


def _relay_gather_arrive(flights, after, name):
    def settle(zone, send, recv):
        me, sib, chips = _relay_peers()
        _relay_copy(zone, send, recv, 0, sib, me).wait_recv()
        _relay_copy(zone, send, recv, 0, me, sib).wait_send()
        for j, chip in enumerate(chips):
            _relay_copy(zone, send, recv, 1 + j, (*chip, me[2]), me).wait_recv()
            _relay_copy(zone, send, recv, 1 + j, me, (*chip, me[2])).wait_send()
    return _relay_wait_call(flights, settle, after, name)


def _relay_pass_start(zones, name, after=None):
    def issue(zone, send, recv):
        me, sib, chips = _relay_peers()
        for j, chip in enumerate(chips):
            _relay_copy(zone, send, recv, j, (*chip, me[2]), sib).start()
    return _relay_start_call(zones, 3, issue, name, after)


def _relay_pass_wait(flights, after, name):
    def settle(zone, send, recv):
        me, sib, chips = _relay_peers()
        for j, chip in enumerate(chips):
            _relay_copy(zone, send, recv, j, (*chip, sib[2]), me).wait_recv()
            _relay_copy(zone, send, recv, j, (*chip, me[2]), sib).wait_send()
    return _relay_wait_call(flights, settle, after, name)


def _exchange_copy(kind, bufs, send_sems, recv_sems, me, k, arriving):
    peer = _flip(me, k)
    my_id, peer_id = _dev_id(me), _dev_id(peer)
    if kind == "gather":
        slot = bufs[0].at[peer_id if arriving else my_id]
        src, dst = slot, slot
    else:
        src = bufs[0].at[my_id if arriving else peer_id]
        dst = bufs[1].at[peer_id if arriving else my_id]
    return pltpu.make_async_remote_copy(
        src_ref=src, dst_ref=dst, send_sem=send_sems.at[k - 1], recv_sem=recv_sems.at[k - 1],
        device_id=peer, device_id_type=MESH)


def _exchange_start(arrays, kind, name, after=None):
    n = len(arrays)
    n_after = 0 if after is None else 1
    if kind == "gather":
        bufs = [[a] for a in arrays]
    else:
        bufs = [[a, lax.empty(a.shape, a.dtype)] for a in arrays]
    nb = len(bufs[0])
    flat = [b for group in bufs for b in group]

    def body(*refs):
        outs_at = nb * n + n_after
        send = refs[outs_at:outs_at + n]
        recv = refs[outs_at + n:outs_at + 2 * n]
        token = refs[outs_at + 2 * n + nb * n]
        me = _my_pos()
        for a in range(n):
            for k in range(1, N_DEV):
                _exchange_copy(kind, refs[nb * a:nb * (a + 1)], send[a], recv[a], me, k, False).start()
        token[...] = jnp.zeros(token.shape, token.dtype)

    sem = pltpu.SemaphoreType.DMA((N_DEV - 1,))
    outs = pl.pallas_call(
        body,
        name=name,
        out_shape=([sem] * (2 * n) + [pltpu.HBM(b.shape, b.dtype) for b in flat]
                   + [jax.ShapeDtypeStruct((8, 128), F32)]),
        in_specs=[_HBM] * (nb * n) + [pl.BlockSpec(memory_space=pl.ANY)] * n_after,
        out_specs=[_SEM] * (2 * n) + [_HBM] * (nb * n) + [pl.BlockSpec(memory_space=pltpu.VMEM)],
        input_output_aliases={i: 2 * n + i for i in range(nb * n)},
        compiler_params=pltpu.CompilerParams(has_side_effects=_EFFECT),
    )(*[pltpu.with_memory_space_constraint(b, pltpu.HBM) for b in flat],
      *([after] if n_after else []))
    flights = [(outs[a], outs[n + a], list(outs[2 * n + nb * a:2 * n + nb * (a + 1)]))
               for a in range(n)]
    return flights, outs[2 * n + nb * n]


def _exchange_wait(flights, kind, after, name):
    n = len(flights)
    nb = len(flights[0][2])
    flat = [b for f in flights for b in f[2]]

    def body(*refs):
        send = refs[nb * n:nb * n + n]
        recv = refs[nb * n + n:nb * n + 2 * n]
        me = _my_pos()
        for a in range(n):
            for k in range(1, N_DEV):
                bufs = refs[nb * a:nb * (a + 1)]
                _exchange_copy(kind, bufs, send[a], recv[a], me, k, False).wait_send()
                _exchange_copy(kind, bufs, send[a], recv[a], me, k, True).wait_recv()

    outs = pl.pallas_call(
        body,
        name=name,
        out_shape=[pltpu.HBM(b.shape, b.dtype) for b in flat],
        in_specs=[_HBM] * (nb * n) + [_SEM] * (2 * n) + [pl.BlockSpec(memory_space=pl.ANY)],
        out_specs=[_HBM] * (nb * n),
        input_output_aliases={i: i for i in range(nb * n)},
        compiler_params=pltpu.CompilerParams(has_side_effects=_EFFECT),
    )(*flat, *[f[0] for f in flights], *[f[1] for f in flights], after)
    return [list(outs[nb * a:nb * (a + 1)]) for a in range(n)]


def _t5_buckets_block():
    qi = np.arange(BLK)[:, None]
    ki = np.arange(2 * BLK)[None, :]
    n = np.maximum(qi + BLK - ki, 0)
    max_exact = NUM_BUCKETS // 2
    large = max_exact + (np.log(np.maximum(n, 1) / max_exact)
                         / np.log(MAX_DISTANCE / max_exact)
                         * (NUM_BUCKETS - max_exact)).astype(np.int32)
    large = np.minimum(large, NUM_BUCKETS - 1)
    return np.where(n < max_exact, n, large).astype(np.int32)


def _band_mask():
    qi = np.arange(BLK)[:, None]
    ki = np.arange(2 * BLK)[None, :]
    dist = qi + BLK - ki
    return (dist >= 0) & (dist < BLK)


def _attn_scores(q_ref, kp_ref, kc_ref, hkv):
    c0 = hkv * HEAD_DIM
    kk = jnp.concatenate([kp_ref[:, c0:c0 + HEAD_DIM], kc_ref[:, c0:c0 + HEAD_DIM]],
                         axis=0).astype(BF16)
    qg = jnp.concatenate(
        [q_ref[:, (hkv * GROUP + g) * HEAD_DIM:(hkv * GROUP + g + 1) * HEAD_DIM]
         for g in range(GROUP)], axis=0).astype(BF16)
    s = lax.dot_general(qg, kk, (((1,), (1,)), ((), ())), preferred_element_type=F32)
    return qg, kk, s


def _attn_softmax(s, bias_ref, sink_ref, hkv):
    r0, r1 = hkv * GROUP * BLK, (hkv + 1) * GROUP * BLK
    s = s * (HEAD_DIM ** -0.5) + bias_ref[r0:r1, :]
    sink = sink_ref[r0:r1, :]
    m = jnp.maximum(jnp.max(s, axis=-1, keepdims=True), sink)
    p = jnp.exp(s - m)
    e_sink = jnp.exp(sink - m)
    inv = 1.0 / (jnp.sum(p, axis=-1, keepdims=True) + e_sink)
    return p * inv, e_sink * inv


def _kv_rows(p_ref, c_ref, hkv):
    c0 = hkv * HEAD_DIM
    return jnp.concatenate([p_ref[:, c0:c0 + HEAD_DIM], c_ref[:, c0:c0 + HEAD_DIM]],
                           axis=0).astype(BF16)


ATT_Q_FWD = 4
ATT_Q_BWD = 2


def _attn_in_specs(bias2, nq):
    prev = lambda n: jnp.maximum(nq * n - 1, 0)
    kcol = ATTN_W // KV_W
    return [
        pl.BlockSpec((nq * BLK, ATTN_W), lambda n: (n, 0)),
        pl.BlockSpec((BLK, KV_W), lambda n: (prev(n), kcol)),
        pl.BlockSpec((nq * BLK, KV_W), lambda n: (n, kcol)),
        pl.BlockSpec((BLK, KV_W), lambda n: (prev(n), kcol + 1)),
        pl.BlockSpec((nq * BLK, KV_W), lambda n: (n, kcol + 1)),
        pl.BlockSpec(bias2.shape, lambda n: (0, 0, 0)),
    ]


def _attn_views(t, q_ref, kp_ref, kc_ref, vp_ref, vc_ref, bias_ref):
    rows = pl.ds(t * BLK, BLK)
    before = pl.ds((t - 1) * BLK, BLK)
    table = jnp.minimum(pl.program_id(0), 1) if t == 0 else 1
    return (q_ref.at[rows, :],
            kp_ref if t == 0 else kc_ref.at[before, :], kc_ref.at[rows, :],
            vp_ref if t == 0 else vc_ref.at[before, :], vc_ref.at[rows, :],
            bias_ref.at[table])


def _attention_fwd(proj, bias2, sinkcol, n_rows):
    nq = min(ATT_Q_FWD, n_rows // BLK)
    steps = n_rows // (nq * BLK)

    def body(q_ref, kp_ref, kc_ref, vp_ref, vc_ref, bias_ref, sink_ref, o_ref):
        views = [_attn_views(t, q_ref, kp_ref, kc_ref, vp_ref, vc_ref, bias_ref) for t in range(nq)]
        work = [(t, hkv) for t in range(nq) for hkv in range(N_KV_HEADS)]
        scores = {w: _attn_scores(views[w[0]][0], views[w[0]][1], views[w[0]][2], w[1])[2] for w in work}
        probs = {w: _attn_softmax(scores[w], views[w[0]][5], sink_ref, w[1])[0] for w in work}
        outs = {w: jnp.dot(probs[w].astype(BF16), _kv_rows(views[w[0]][3], views[w[0]][4], w[1]),
                           preferred_element_type=F32) for w in work}
        for t, hkv in work:
            for g in range(GROUP):
                h = hkv * GROUP + g
                o_ref[t * BLK:(t + 1) * BLK, h * HEAD_DIM:(h + 1) * HEAD_DIM] = (
                    outs[t, hkv][g * BLK:(g + 1) * BLK, :].astype(o_ref.dtype))

    return pl.pallas_call(
        body,
        grid=(steps,),
        in_specs=_attn_in_specs(bias2, nq) + [pl.BlockSpec(sinkcol.shape, lambda n: (0, 0))],
        out_specs=pl.BlockSpec((nq * BLK, ATTN_W), lambda n: (n, 0)),
        out_shape=jax.ShapeDtypeStruct((n_rows, ATTN_W), BF16),
        compiler_params=_cparams(("parallel",)),
        name="attn_fwd",
    )(proj, proj, proj, proj, proj, bias2, sinkcol)


def _attention_bwd(proj, attn, dattn, bias2, sinkcol, n_rows):
    nq = min(ATT_Q_BWD, n_rows // BLK)
    steps = n_rows // (nq * BLK)
    scale = HEAD_DIM ** -0.5
    dn_t = (((0,), (0,)), ((), ()))

    def body(q_ref, kp_ref, kc_ref, vp_ref, vc_ref, bias_ref, o_ref, do_ref, sink_ref,
             dq_ref, dkc_ref, dkp_ref, dvc_ref, dvp_ref, dbias_ref, dsink_ref):
        @pl.when(pl.program_id(0) == 0)
        def _():
            dbias_ref[...] = jnp.zeros(dbias_ref.shape, F32)
            dsink_ref[...] = jnp.zeros(dsink_ref.shape, F32)

        views = [_attn_views(t, q_ref, kp_ref, kc_ref, vp_ref, vc_ref, bias_ref) for t in range(nq)]
        work = [(t, hkv) for t in range(nq) for hkv in range(N_KV_HEADS)]
        qk = {w: _attn_scores(views[w[0]][0], views[w[0]][1], views[w[0]][2], w[1]) for w in work}
        dog, dps, deltas = {}, {}, {}
        for t, hkv in work:
            rows = slice(t * BLK, (t + 1) * BLK)
            hs = [hkv * GROUP + g for g in range(GROUP)]
            d_o = jnp.concatenate([do_ref[rows, h * HEAD_DIM:(h + 1) * HEAD_DIM] for h in hs], axis=0)
            o = jnp.concatenate([o_ref[rows, h * HEAD_DIM:(h + 1) * HEAD_DIM] for h in hs], axis=0)
            deltas[t, hkv] = jnp.sum(d_o.astype(F32) * o.astype(F32), axis=-1, keepdims=True)
            dog[t, hkv] = d_o.astype(BF16)
            dps[t, hkv] = lax.dot_general(dog[t, hkv], _kv_rows(views[t][3], views[t][4], hkv),
                                          (((1,), (1,)), ((), ())), preferred_element_type=F32)
        p16, ds16 = {}, {}
        for t, hkv in work:
            r0, r1 = hkv * GROUP * BLK, (hkv + 1) * GROUP * BLK
            p, p_sink = _attn_softmax(qk[t, hkv][2], views[t][5], sink_ref, hkv)
            ds = p * (dps[t, hkv] - deltas[t, hkv])
            dbias_ref[r0:r1, :] += ds
            dsink_ref[r0:r1, :] += -(p_sink * deltas[t, hkv])
            p16[t, hkv] = p.astype(BF16)
            ds16[t, hkv] = ds.astype(BF16)
        for t, hkv in work:
            rows = slice(t * BLK, (t + 1) * BLK)
            c0 = hkv * HEAD_DIM
            qg, kk, _ = qk[t, hkv]
            dqg = jnp.dot(ds16[t, hkv], kk, preferred_element_type=F32) * scale
            dkk = lax.dot_general(ds16[t, hkv], qg, dn_t, preferred_element_type=F32) * scale
            dvv = lax.dot_general(p16[t, hkv], dog[t, hkv], dn_t, preferred_element_type=F32)
            for g in range(GROUP):
                h = hkv * GROUP + g
                dq_ref[rows, h * HEAD_DIM:(h + 1) * HEAD_DIM] = (
                    dqg[g * BLK:(g + 1) * BLK, :].astype(dq_ref.dtype))
            dkp_ref[rows, c0:c0 + HEAD_DIM] = dkk[:BLK].astype(dkp_ref.dtype)
            dkc_ref[rows, c0:c0 + HEAD_DIM] = dkk[BLK:].astype(dkc_ref.dtype)
            dvp_ref[rows, c0:c0 + HEAD_DIM] = dvv[:BLK].astype(dvp_ref.dtype)
            dvc_ref[rows, c0:c0 + HEAD_DIM] = dvv[BLK:].astype(dvc_ref.dtype)

    wide = pl.BlockSpec((nq * BLK, ATTN_W), lambda n: (n, 0))
    kv_out = pl.BlockSpec((nq * BLK, KV_W), lambda n: (n, 0))
    kv_shape = jax.ShapeDtypeStruct((n_rows, KV_W), F32)
    acc_shape = bias2.shape[1:]
    return pl.pallas_call(
        body,
        grid=(steps,),
        in_specs=_attn_in_specs(bias2, nq) + [wide, wide, pl.BlockSpec(sinkcol.shape, lambda n: (0, 0))],
        out_specs=[
            wide, kv_out, kv_out, kv_out, kv_out,
            pl.BlockSpec(acc_shape, lambda n: (0, 0)),
            pl.BlockSpec(sinkcol.shape, lambda n: (0, 0)),
        ],
        out_shape=[
            jax.ShapeDtypeStruct((n_rows, ATTN_W), BF16),
            kv_shape, kv_shape, kv_shape, kv_shape,
            jax.ShapeDtypeStruct(acc_shape, F32),
            jax.ShapeDtypeStruct(sinkcol.shape, F32),
        ],
        compiler_params=_cparams(("arbitrary",)),
        name="attn_bwd",
    )(proj, proj, proj, proj, proj, bias2, attn, dattn, sinkcol)


def _bias_tables(rel_bias_t, onehot_t, band_first, band_rest):
    def body(rb_ref, oh_ref, mf_ref, mr_ref, out_ref):
        acc = jnp.zeros((N_Q_HEADS, BLK * 2 * BLK), F32)
        for part in _split3(rb_ref[...]):
            acc = acc + jnp.dot(part, oh_ref[...], preferred_element_type=F32)
        out_ref[0] = jnp.where(mf_ref[...] > 0.0, acc, NEG_INF)
        out_ref[1] = jnp.where(mr_ref[...] > 0.0, acc, NEG_INF)

    return pl.pallas_call(
        body,
        out_shape=jax.ShapeDtypeStruct((2, N_Q_HEADS, BLK * 2 * BLK), F32),
        compiler_params=pltpu.CompilerParams(vmem_limit_bytes=VMEM_LIMIT),
        name="bias_tables",
    )(rel_bias_t, onehot_t, band_first, band_rest)


def _split3(a):
    hi = a.astype(BF16)
    r1 = a - hi.astype(F32)
    mid = r1.astype(BF16)
    lo = (r1 - mid.astype(F32)).astype(BF16)
    return hi, mid, lo


def _bucket_reduce(dbias, dsink, onehot_t):
    def body(db_ref, ds_ref, oh_ref, ob_ref, os_ref):
        acc = jnp.zeros((N_Q_HEADS, 128), F32)
        for part in _split3(db_ref[...]):
            acc = acc + lax.dot_general(part, oh_ref[...], (((1,), (1,)), ((), ())),
                                        preferred_element_type=F32)
        ob_ref[...] = acc
        os_ref[...] = jnp.broadcast_to(jnp.sum(ds_ref[...], axis=-1, keepdims=True),
                                       os_ref.shape)

    return pl.pallas_call(
        body,
        out_shape=[jax.ShapeDtypeStruct((N_Q_HEADS, 128), F32),
                   jax.ShapeDtypeStruct((N_Q_HEADS, 128), F32)],
        compiler_params=pltpu.CompilerParams(vmem_limit_bytes=VMEM_LIMIT),
        name="bias_bucket_reduce",
    )(dbias, dsink, onehot_t)


def _disc(lr, li, ls, btr, bti):
    lam_re = jnp.minimum(lr, -1e-4)
    delta = jnp.exp(ls)
    mag = jnp.exp(lam_re * delta)
    ang = li * delta
    ar, ai = mag * jnp.cos(ang), mag * jnp.sin(ang)
    nr, ni = ar - 1.0, ai
    den = lam_re * lam_re + li * li
    fr = (nr * lam_re + ni * li) / den
    fi = (ni * lam_re - nr * li) / den
    bbr = fr * btr - fi * bti
    bbi = fr * bti + fi * btr
    return ar, ai, bbr, bbi


def _block_mask():
    row = lax.broadcasted_iota(jnp.int32, (SSM_W, SSM_H), 0)
    col = lax.broadcasted_iota(jnp.int32, (SSM_W, SSM_H), 1)
    return (row // SSM_P) == (col // SSM_N)


def _ssm_setup(lr, li, ls, btr, bti, ctr, cti):
    def body(lr_ref, li_ref, ls_ref, btr_ref, bti_ref, ctr_ref, cti_ref, a_ref, b_ref, c_ref):
        ar, ai, bbr, bbi = _disc(lr_ref[...], li_ref[...], ls_ref[...], btr_ref[...], bti_ref[...])
        a_ref[:, :SSM_H] = ar
        a_ref[:, SSM_H:] = ai
        mask = _block_mask()
        blk = lambda t: jnp.where(mask, jnp.tile(t, (SSM_G, 1)), 0.0)
        b_ref[:, :SSM_H] = blk(bbr).astype(BF16)
        b_ref[:, SSM_H:] = blk(bbi).astype(BF16)
        c_ref[:, :SSM_H] = blk(ctr_ref[...]).astype(BF16)
        c_ref[:, SSM_H:] = blk(-cti_ref[...]).astype(BF16)

    return pl.pallas_call(
        body,
        out_shape=[jax.ShapeDtypeStruct((1, 2 * SSM_H), F32),
                   jax.ShapeDtypeStruct((SSM_W, 2 * SSM_H), BF16),
                   jax.ShapeDtypeStruct((SSM_W, 2 * SSM_H), BF16)],
        compiler_params=pltpu.CompilerParams(vmem_limit_bytes=VMEM_LIMIT),
        name="ssm_setup",
    )(lr, li, ls, btr, bti, ctr, cti)


def _ssm_param_bwd(lr, li, ls, btr, bti, dacc, dbcat, dccat, gind):
    def body(lr_ref, li_ref, ls_ref, btr_ref, bti_ref, dacc_ref, db_ref, dc_ref, g_ref,
             dlr_ref, dli_ref, dls_ref, dbtr_ref, dbti_ref, dctr_ref, dcti_ref):
        dar = jnp.sum(dacc_ref[:, :SSM_H], axis=0, keepdims=True)
        dai = jnp.sum(dacc_ref[:, SSM_H:], axis=0, keepdims=True)
        col = lax.broadcasted_iota(jnp.int32, (SSM_P, 2 * SSM_H), 1)
        grp = (col % SSM_H) // SSM_N
        db = jnp.zeros((SSM_P, 2 * SSM_H), F32)
        dc = jnp.zeros((SSM_P, 2 * SSM_H), F32)
        half = SSM_G // 2
        for g in range(SSM_G):
            sel = grp == g
            r0 = (g % half) * SSM_P
            db = db + jnp.where(sel, db_ref[r0:r0 + SSM_P, :], 0.0)
            dc = dc + jnp.where(sel, dc_ref[r0:r0 + SSM_P, :], 0.0)
        dctr_ref[...] = dc[:, :SSM_H]
        dcti_ref[...] = -dc[:, SSM_H:]
        prim = (lr_ref[...], li_ref[...], ls_ref[...], btr_ref[...], bti_ref[...])
        _, vjp = jax.vjp(_disc, *prim)
        dlr, dli, dls, dbtr, dbti = vjp((dar, dai, db[:, :SSM_H], db[:, SSM_H:]))
        dlr_ref[...] = dlr
        dli_ref[...] = dli
        dbtr_ref[...] = dbtr
        dbti_ref[...] = dbti
        acc = jnp.zeros((8, 128), F32)
        for part in _split3(jnp.broadcast_to(dls, (8, SSM_H))):
            acc = acc + jnp.dot(part, g_ref[...], preferred_element_type=F32)
        dls_ref[...] = acc

    vec = jax.ShapeDtypeStruct((1, SSM_H), F32)
    mat = jax.ShapeDtypeStruct((SSM_P, SSM_H), F32)
    return pl.pallas_call(
        body,
        out_shape=[vec, vec, jax.ShapeDtypeStruct((8, 128), F32), mat, mat, mat, mat],
        compiler_params=pltpu.CompilerParams(vmem_limit_bytes=VMEM_LIMIT),
        name="ssm_param_bwd",
    )(lr, li, ls, btr, bti, dacc, dbcat, dccat, gind)


SCAN_TR = 256


def _cmul_add(vr, vi, pr, pi, sr, si):
    return vr + pr * sr - pi * si, vi + pr * si + pi * sr


def _bcast_row(v, row, which):
    return jnp.broadcast_to(v[which:which + 1, :], v.shape)


def _scan_tables(a_ref, tab_ref, reverse):
    H = SSM_H
    ar = jnp.broadcast_to(a_ref[:, :H], (8, H))
    ai = jnp.broadcast_to(a_ref[:, H:], (8, H))
    if reverse:
        ai = -ai
    row = lax.broadcasted_iota(jnp.int32, (8, H), 0)
    pw = [(ar, ai)]
    for _ in range(7):
        cr, ci = pw[-1]
        pw.append((cr * ar - ci * ai, cr * ai + ci * ar))
    pcr = jnp.zeros((8, H), F32)
    pci = jnp.zeros((8, H), F32)
    for e in range(8):
        sel = (row == (7 - e)) if reverse else (row == e)
        pcr = jnp.where(sel, pw[e][0], pcr)
        pci = jnp.where(sel, pw[e][1], pci)
    tab_ref[0, :, :H] = pcr
    tab_ref[0, :, H:] = pci
    for t, k in enumerate((1, 2, 4)):
        keep = (row < 8 - k) if reverse else (row >= k)
        tab_ref[1 + t, :, :H] = jnp.where(keep, pw[k - 1][0], 0.0)
        tab_ref[1 + t, :, H:] = jnp.where(keep, pw[k - 1][1], 0.0)


def _scan_group(vr, vi, cr, ci, tab_ref, reverse):
    H = SSM_H
    for t, k in enumerate((1, 2, 4)):
        sh = 8 - k if reverse else k
        vr, vi = _cmul_add(vr, vi, tab_ref[1 + t, :, :H], tab_ref[1 + t, :, H:],
                           pltpu.roll(vr, sh, 0), pltpu.roll(vi, sh, 0))
    return _cmul_add(vr, vi, tab_ref[0, :, :H], tab_ref[0, :, H:], cr, ci)


def _blockdiag_expand(x, w_ref, out_ref):
    hw, cb = SSM_W // 2, SSM_H // 2
    for j in range(4):
        h = j % 2
        out_ref[:, j * cb:(j + 1) * cb] = jnp.dot(
            x[:, h * hw:(h + 1) * hw], w_ref[h * hw:(h + 1) * hw, j * cb:(j + 1) * cb],
            preferred_element_type=F32)


def _blockdiag_contract(x_ref, w_ref):
    hw, cb = SSM_W // 2, SSM_H // 2
    nt = (((1,), (1,)), ((), ()))
    halves = []
    for h in range(2):
        acc = None
        for j in (h, 2 + h):
            part = lax.dot_general(x_ref[:, j * cb:(j + 1) * cb],
                                   w_ref[h * hw:(h + 1) * hw, j * cb:(j + 1) * cb], nt,
                                   preferred_element_type=F32)
            acc = part if acc is None else acc + part
        halves.append(acc)
    return jnp.concatenate(halves, axis=1)


def _scan_fwd(proj, u_blk, bcat, ccat, abar, n_rows):
    H = SSM_H
    nt = n_rows // SCAN_TR

    def body(u_ref, b_ref, c_ref, a_ref, xs_ref, xp_ref, yc_ref, bu_ref, tab_ref, carry_ref):
        @pl.when(pl.program_id(0) == 0)
        def _():
            _scan_tables(a_ref, tab_ref, False)
            carry_ref[...] = jnp.zeros(carry_ref.shape, F32)

        _blockdiag_expand(u_ref[...].astype(BF16), b_ref, bu_ref)
        row = lax.broadcasted_iota(jnp.int32, (8, H), 0)

        def group(j, carry):
            cr, ci = carry
            r0 = pl.multiple_of(j * 16, 16)
            xr, xi = [], []
            for half in range(2):
                rr = pl.multiple_of(r0 + 8 * half, 8)
                vr, vi = _scan_group(bu_ref[pl.ds(rr, 8), :H], bu_ref[pl.ds(rr, 8), H:],
                                     cr, ci, tab_ref, False)
                xp_ref[pl.ds(rr, 8), :H] = jnp.where(row == 0, cr, pltpu.roll(vr, 1, 0))
                xp_ref[pl.ds(rr, 8), H:] = jnp.where(row == 0, ci, pltpu.roll(vi, 1, 0))
                cr, ci = _bcast_row(vr, row, 7), _bcast_row(vi, row, 7)
                xr.append(vr)
                xi.append(vi)
            xs_ref[pl.ds(r0, 16), :H] = jnp.concatenate(xr, axis=0).astype(BF16)
            xs_ref[pl.ds(r0, 16), H:] = jnp.concatenate(xi, axis=0).astype(BF16)
            return cr, ci

        cr, ci = lax.fori_loop(0, SCAN_TR // 16, group,
                               (carry_ref[:, :H], carry_ref[:, H:]))
        carry_ref[:, :H] = cr
        carry_ref[:, H:] = ci
        yc_ref[...] = _blockdiag_contract(xs_ref, c_ref)

    tile = lambda w: pl.BlockSpec((SCAN_TR, w), lambda i: (i, 0))
    whole = lambda a: pl.BlockSpec(a.shape, lambda i: (0, 0))
    return pl.pallas_call(
        body,
        grid=(nt,),
        in_specs=[pl.BlockSpec((SCAN_TR, SSM_W), lambda i: (i, u_blk)),
                  whole(bcat), whole(ccat), whole(abar)],
        out_specs=[tile(2 * H), tile(2 * H), tile(SSM_W)],
        out_shape=[jax.ShapeDtypeStruct((n_rows, 2 * H), BF16),
                   jax.ShapeDtypeStruct((n_rows, 2 * H), F32),
                   jax.ShapeDtypeStruct((n_rows, SSM_W), F32)],
        scratch_shapes=[pltpu.VMEM((SCAN_TR, 2 * H), F32), pltpu.VMEM((4, 8, 2 * H), F32),
                        pltpu.VMEM((8, 2 * H), F32)],
        compiler_params=_cparams(("arbitrary",)),
        name="ssm_scan_fwd",
    )(proj, bcat, ccat, abar)


def _scan_bwd(dy, xprev, bcat, ccat, abar, n_rows):
    H = SSM_H
    nt = n_rows // SCAN_TR

    def body(dy_ref, xp_ref, b_ref, c_ref, a_ref, h_ref, da_ref, du_ref, g_ref, tab_ref, carry_ref):
        @pl.when(pl.program_id(0) == 0)
        def _():
            _scan_tables(a_ref, tab_ref, True)
            carry_ref[...] = jnp.zeros(carry_ref.shape, F32)
            da_ref[...] = jnp.zeros(da_ref.shape, F32)

        _blockdiag_expand(dy_ref[...], c_ref, g_ref)
        row = lax.broadcasted_iota(jnp.int32, (8, H), 0)
        n16 = SCAN_TR // 16

        def group(jj, carry):
            cr, ci = carry
            r0 = pl.multiple_of((n16 - 1 - jj) * 16, 16)
            hr, hi = [None, None], [None, None]
            for half in (1, 0):
                rr = pl.multiple_of(r0 + 8 * half, 8)
                vr, vi = _scan_group(g_ref[pl.ds(rr, 8), :H], g_ref[pl.ds(rr, 8), H:],
                                     cr, ci, tab_ref, True)
                pr, pi = xp_ref[pl.ds(rr, 8), :H], xp_ref[pl.ds(rr, 8), H:]
                da_ref[:, :H] += vr * pr + vi * pi
                da_ref[:, H:] += vi * pr - vr * pi
                cr, ci = _bcast_row(vr, row, 0), _bcast_row(vi, row, 0)
                hr[half], hi[half] = vr, vi
            h_ref[pl.ds(r0, 16), :H] = jnp.concatenate(hr, axis=0).astype(BF16)
            h_ref[pl.ds(r0, 16), H:] = jnp.concatenate(hi, axis=0).astype(BF16)
            return cr, ci

        cr, ci = lax.fori_loop(0, n16, group, (carry_ref[:, :H], carry_ref[:, H:]))
        carry_ref[:, :H] = cr
        carry_ref[:, H:] = ci
        du_ref[...] = _blockdiag_contract(h_ref, b_ref)

    rev = lambda i: (nt - 1 - i, 0)
    whole = lambda a: pl.BlockSpec(a.shape, lambda i: (0, 0))
    return pl.pallas_call(
        body,
        grid=(nt,),
        in_specs=[pl.BlockSpec((SCAN_TR, SSM_W), rev),
                  pl.BlockSpec((SCAN_TR, 2 * H), rev),
                  whole(bcat), whole(ccat), whole(abar)],
        out_specs=[pl.BlockSpec((SCAN_TR, 2 * H), rev),
                   pl.BlockSpec((8, 2 * H), lambda i: (0, 0)),
                   pl.BlockSpec((SCAN_TR, SSM_W), rev)],
        out_shape=[jax.ShapeDtypeStruct((n_rows, 2 * H), BF16),
                   jax.ShapeDtypeStruct((8, 2 * H), F32),
                   jax.ShapeDtypeStruct((n_rows, SSM_W), F32)],
        scratch_shapes=[pltpu.VMEM((SCAN_TR, 2 * H), F32), pltpu.VMEM((4, 8, 2 * H), F32),
                        pltpu.VMEM((8, 2 * H), F32)],
        compiler_params=_cparams(("arbitrary",)),
        name="ssm_scan_bwd",
    )(dy, xprev, bcat, ccat, abar)


def _adamw(parts, w, m, v, *, tr, ch, name, prefetch=None):
    n_rows, cols = w.shape
    n_parts = len(parts)
    c1 = 1.0 - ADAM_B1 ** ADAM_STEP
    c2 = 1.0 - ADAM_B2 ** ADAM_STEP

    def fn(rv, vv, i, nt):
        g = rv[0].astype(F32)
        for p in rv[1:n_parts]:
            g = g + p.astype(F32)
        wv, mv, vval = rv[n_parts:]
        nm = ADAM_B1 * mv + (1.0 - ADAM_B1) * g
        nv = ADAM_B2 * vval + (1.0 - ADAM_B2) * (g * g)
        delta = -ADAM_LR * ((nm / c1) / (jnp.sqrt(nv / c2) + ADAM_EPS) + ADAM_WD * wv)
        return [g, delta, nm, nv], []

    rows = [_row(arr, lead=lead) for (arr, lead) in parts] + [_row(w), _row(m), _row(v)]
    return _rowwise(fn, rows, [], [(cols, F32)] * 4, [], n_rows=n_rows, tr=tr, ch=ch, name=name,
                    prefetch=prefetch)


_PACK = [
    ("b_ada", 6), ("norm1_g", 1), ("b_in", 3), ("norm2_g", 1), ("final_g", 1),
    ("lambda_re", 1), ("lambda_im", 1), ("log_step", 1), ("attn_sinks", 1),
    ("rel_bias", 1), ("b_glu", 1), ("ssm_d", 1), ("loss", 1),
    ("ssm_b_re", 16), ("ssm_b_im", 16), ("ssm_c_re", 16), ("ssm_c_im", 16),
]
_PACK_OFF = {}
_off = 0
for _n, _r in _PACK:
    _PACK_OFF[_n] = (_off, _r)
    _off += _r
PACK_ROWS = -(-_off // 8) * 8


def _to_rows(a, rows):
    flat = a.reshape(-1).astype(F32)
    pad = rows * PACK_W - flat.shape[0]
    if pad:
        flat = jnp.pad(flat, (0, pad))
    return flat.reshape(rows, PACK_W)


def _b_to_rows(b):
    return jnp.transpose(b, (2, 0, 1)).reshape(SSM_P, SSM_H)


def _rows_to_b(r):
    return jnp.transpose(r.reshape(SSM_P, SSM_G, SSM_N), (1, 2, 0))


def _c_to_rows(cm):
    return jnp.transpose(cm, (1, 0, 2)).reshape(SSM_P, SSM_H)


def _rows_to_c(r):
    return jnp.transpose(r.reshape(SSM_P, SSM_G, SSM_N), (1, 0, 2))


def _pack(vals):
    out = jnp.zeros((PACK_ROWS, PACK_W), F32)
    for n, r in _PACK:
        if n in vals:
            pieces = vals[n] if isinstance(vals[n], list) else [vals[n]]
            rows_each = r // len(pieces)
            for i, piece in enumerate(pieces):
                out = lax.dynamic_update_slice(out, _to_rows(piece, rows_each),
                                               (_PACK_OFF[n][0] + i * rows_each, 0))
    return out


def _unpack(packed, name, shape):
    o, r = _PACK_OFF[name]
    n = int(np.prod(shape))
    return packed[o:o + r].reshape(-1)[:n].reshape(shape)


def _small_params_packed(p):
    return {
        "b_ada": p["b_ada"], "norm1_g": p["norm1_g"], "b_in": p["b_in"],
        "norm2_g": p["norm2_g"], "final_g": p["final_g"],
        "lambda_re": p["lambda_re"], "lambda_im": p["lambda_im"],
        "log_step": p["log_step"], "attn_sinks": p["attn_sinks"],
        "rel_bias": p["rel_bias"], "b_glu": p["b_glu"], "ssm_d": p["ssm_d"],
        "ssm_b_re": _b_to_rows(p["ssm_b_re"][0]), "ssm_b_im": _b_to_rows(p["ssm_b_im"][0]),
        "ssm_c_re": _c_to_rows(p["ssm_c_re"][0]), "ssm_c_im": _c_to_rows(p["ssm_c_im"][0]),
    }


_SMALL_SHAPES = {
    "b_ada": (1, N_MOD * D), "norm1_g": (1, D), "b_in": (1, IN_W), "norm2_g": (1, D),
    "final_g": (D,), "lambda_re": (1, SSM_G, SSM_N), "lambda_im": (1, SSM_G, SSM_N),
    "log_step": (1, SSM_G), "attn_sinks": (1, N_Q_HEADS), "rel_bias": (NUM_BUCKETS, N_Q_HEADS),
    "b_glu": (1, SSM_W), "ssm_d": (1, SSM_W),
}


def _unpack_small(packed, name):
    if name in ("ssm_b_re", "ssm_b_im"):
        o, r = _PACK_OFF[name]
        return _rows_to_b(packed[o:o + r])[None]
    if name in ("ssm_c_re", "ssm_c_im"):
        o, r = _PACK_OFF[name]
        return _rows_to_c(packed[o:o + r])[None]
    return _unpack(packed, name, _SMALL_SHAPES[name])


WEIGHT_ORDER = ['w_ada', 'b_ada', 'norm1_g', 'w_in', 'b_in', 'attn_sinks', 'rel_bias', 'lambda_re',
                'lambda_im', 'log_step', 'ssm_b_re', 'ssm_b_im', 'ssm_c_re', 'ssm_c_im', 'ssm_d',
                'w_glu', 'b_glu', 'w_attn_proj', 'w_ssm_proj', 'w_out', 'norm2_g', 'w_ff1', 'w_ff2',
                'final_g']
BIG = ['w_in', 'w_glu', 'w_attn_proj', 'w_ssm_proj', 'w_out', 'w_ff1', 'w_ff2']


ADAMW_TILE_ELEMS = 1 << 18


def _to_col_blocks(w):
    k, n = w.shape
    return jnp.transpose(w.reshape(k, N_DEV, n // N_DEV), (1, 0, 2))


def _adamw_rows(rows, cols):
    tr = rows
    while tr * cols > ADAMW_TILE_ELEMS and tr % 32 == 0:
        tr //= 2
    return tr


def _cast_to_slot(w, me1, name, dep=None):
    rows, cols = w.shape
    tr = min(rows, 256)
    n_dep = 0 if dep is None else 1

    def body(me_ref, w_ref, *rest):
        rest[-1][...] = w_ref[...].astype(BF16)

    return pl.pallas_call(
        body,
        grid_spec=pltpu.PrefetchScalarGridSpec(
            num_scalar_prefetch=1, grid=(rows // tr,),
            in_specs=[pl.BlockSpec((tr, cols), lambda i, me_ref: (i, 0))]
            + [pl.BlockSpec(memory_space=pl.ANY)] * n_dep,
            out_specs=pl.BlockSpec((None, tr, cols), lambda i, me_ref: (me_ref[0], i, 0))),
        out_shape=jax.ShapeDtypeStruct((N_DEV, rows, cols), BF16),
        compiler_params=_cparams(("arbitrary",)),
        name=name,
    )(me1, w, *([dep] if n_dep else []))


def kernel(x, c, w_ada, b_ada, norm1_g, w_in, b_in, attn_sinks, rel_bias, lambda_re, lambda_im, log_step, ssm_b_re, ssm_b_im, ssm_c_re, ssm_c_im, ssm_d, w_glu, b_glu, w_attn_proj, w_ssm_proj, w_out, norm2_g, w_ff1, w_ff2, final_g, loss_target, m_w_ada, m_b_ada, m_norm1_g, m_w_in, m_b_in, m_attn_sinks, m_rel_bias, m_lambda_re, m_lambda_im, m_log_step, m_ssm_b_re, m_ssm_b_im, m_ssm_c_re, m_ssm_c_im, m_ssm_d, m_w_glu, m_b_glu, m_w_attn_proj, m_w_ssm_proj, m_w_out, m_norm2_g, m_w_ff1, m_w_ff2, m_final_g, v_w_ada, v_b_ada, v_norm1_g, v_w_in, v_b_in, v_attn_sinks, v_rel_bias, v_lambda_re, v_lambda_im, v_log_step, v_ssm_b_re, v_ssm_b_im, v_ssm_c_re, v_ssm_c_im, v_ssm_d, v_w_glu, v_b_glu, v_w_attn_proj, v_w_ssm_proj, v_w_out, v_norm2_g, v_w_ff1, v_w_ff2, v_final_g):
    loc = dict(locals())
    W = {n: loc[n] for n in WEIGHT_ORDER}
    Mo = {n: loc["m_" + n] for n in WEIGHT_ORDER}
    Vo = {n: loc["v_" + n] for n in WEIGHT_ORDER}
    S = x.shape[1]
    TM = min(512, S)
    TS = min(1024, S)
    TR = min(256, S)
    TW = min(1024, S)
    TX = min(2048, S)
    me = 4 * lax.axis_index("x") + 2 * lax.axis_index("y") + lax.axis_index("c")
    x2d = x.reshape(S, D)
    tgt = loss_target.reshape(S, D)

    c_all = _small_allgather(c, "allgather_c").reshape(N_DEV, D)
    cs = _rowwise(lambda rv, vv, i, nt: ([rv[0] * _sigmoid(rv[0])], []), [_row(c_all)], [],
                  [(D, F32)], [], n_rows=N_DEV, tr=8, ch=8, name="silu_c")[0]
    n_ada = N_MOD * D // N_DEV
    b_ada_cols = lax.dynamic_slice(b_ada, (0, me * n_ada), (1, n_ada))
    mod_piece = _matmul(cs, w_ada[0], mode="nn", dims=(N_DEV, n_ada, D), tiles=(N_DEV, 512, D),
                        out_dtypes=[F32], name="ada_fwd", bias=b_ada_cols)
    mod_all = _small_allgather(mod_piece, "allgather_mod")
    mod_b = lax.dynamic_index_in_dim(mod_all, me, axis=1, keepdims=False).reshape(N_MOD, D)
    sh1, sc1, g1, sh2, sc2, g2 = [mod_b[i:i + 1] for i in range(N_MOD)]

    shard = {n: W[n][0] for n in BIG}
    me1 = jnp.reshape(me, (1,)).astype(jnp.int32)
    zone = {"w_in": _cast_to_slot(shard["w_in"], me1, "cast_w_in")}
    (in_flight,), tok_in = _relay_gather_start([zone["w_in"]], "w_in_start", mod_all)
    for n in BIG[1:]:
        zone[n] = _cast_to_slot(shard[n], me1, "cast_" + n, dep=tok_in)
    G = {}

    buckets = _t5_buckets_block()
    band = _band_mask()
    onehot_t = jnp.asarray(
        (np.arange(128)[:, None] == buckets.reshape(-1)[None, :]).astype(np.float32), BF16)
    band_first = band & (np.arange(2 * BLK)[None, :] >= BLK)
    rel_bias_t = jnp.pad(jnp.transpose(rel_bias), ((0, 0), (0, 128 - NUM_BUCKETS)))
    bias2 = _bias_tables(rel_bias_t, onehot_t,
                         jnp.asarray(band_first.reshape(1, -1).astype(np.float32)),
                         jnp.asarray(band.reshape(1, -1).astype(np.float32))
                         ).reshape(2, N_Q_HEADS * BLK, 2 * BLK)
    sinkcol = jnp.repeat(attn_sinks.reshape(N_Q_HEADS), BLK).reshape(N_Q_HEADS * BLK, 1)
    lam_re = lambda_re.reshape(1, SSM_H)
    lam_im = lambda_im.reshape(1, SSM_H)
    ls_x = jnp.repeat(log_step.reshape(SSM_G), SSM_N).reshape(1, SSM_H)
    btr, bti = _b_to_rows(ssm_b_re[0]), _b_to_rows(ssm_b_im[0])
    ctr, cti = _c_to_rows(ssm_c_re[0]), _c_to_rows(ssm_c_im[0])
    abar, bcat, ccat = _ssm_setup(lam_re, lam_im, ls_x, btr, bti, ctr, cti)
    wp, mp, vp = [_pack(_small_params_packed(p)) for p in (W, Mo, Vo)]

    def f_norm1(rv, vv, i, nt):
        xv, (g, sc, sh) = rv[0], vv
        return [(xv * _rms(xv) * g) * (1.0 + sc) + sh], []

    h = _rowwise(f_norm1, [_row(x2d)], [norm1_g, sc1, sh1], [(D, BF16)], [],
                 n_rows=S, tr=TR, ch=32, name="norm1_fwd", dep=zone["w_ff2"])[0]
    (zone_in,) = _relay_gather_arrive([in_flight], [h, bias2, bcat, zone["w_ff1"], wp, mp, vp], "w_in_arrive")
    (in_pass,), tok_p = _relay_pass_start([zone_in], "w_in_pass_start")
    mixer = ["w_attn_proj", "w_glu", "w_ssm_proj", "w_out"]
    later_flights, tok_w = _relay_gather_start(
        [zone[n] for n in mixer] + [zone["w_ff1"], zone["w_ff2"]], "weights_start", tok_p)
    mixer_flights, ff_flights = later_flights[:len(mixer)], later_flights[len(mixer):]
    (G["w_in"],) = _relay_pass_wait([in_pass], tok_w, "w_in_pass_wait")
    proj = _matmul(h, G["w_in"], mode="nn", dims=(S, IN_W, D), tiles=(TX, 768, D),
                   out_dtypes=[BF16], name="in_proj", b3=True, bias=b_in, dep=tok_w)

    attn = _attention_fwd(proj, bias2, sinkcol, S)
    mixer_zones = _relay_gather_arrive(mixer_flights, attn, "mixer_weights_arrive")
    mixer_pass, _ = _relay_pass_start(mixer_zones, "mixer_weights_pass_start")

    u_blk = (ATTN_W + 2 * KV_W) // SSM_W
    xs, xprev, yc = _scan_fwd(proj, u_blk, bcat, ccat, abar, S)

    def f_ssm_out(rv, vv, i, nt):
        y = rv[0] + vv[0] * rv[1]
        return [y, _gelu(y)], []

    y_ssm_pre, z = _rowwise(f_ssm_out, [_row(yc), _row(proj, u_blk, SSM_W)], [ssm_d],
                            [(SSM_W, F32), (SSM_W, BF16)], [], n_rows=S, tr=TM, ch=32, name="ssm_out")
    G.update(zip(mixer, _relay_pass_wait(mixer_pass, z, "mixer_weights_pass_wait")))
    w_glu_f = G["w_glu"].reshape(SSM_W, SSM_W)
    w_out_f = G["w_out"].reshape(D, D)
    w_ap_f = jnp.transpose(G["w_attn_proj"], (1, 0, 2)).reshape(ATTN_W, D)
    w_sp_f = jnp.transpose(G["w_ssm_proj"], (1, 0, 2)).reshape(SSM_W, D)
    y_attn = _matmul(attn, w_ap_f, mode="nn", dims=(S, D, ATTN_W), tiles=(TW, 1024, ATTN_W),
                     out_dtypes=[BF16], name="attn_proj")
    zg = _matmul(z, w_glu_f, mode="nn", dims=(S, SSM_W, SSM_W), tiles=(TM, SSM_W, SSM_W),
                 out_dtypes=[F32], name="glu_proj", bias=b_glu)
    z2 = _rowwise(lambda rv, vv, i, nt: ([rv[0].astype(F32) * _sigmoid(rv[1])], []),
                  [_row(z), _row(zg)], [], [(SSM_W, BF16)], [], n_rows=S, tr=TM, ch=32, name="glu_gate")[0]
    y_ssm = _matmul(z2, w_sp_f, mode="nn", dims=(S, D, SSM_W), tiles=(TW, 1024, SSM_W),
                    out_dtypes=[BF16], name="ssm_proj")

    ga_row = _row(proj, 1, D)
    gs_row = _row(proj, 2, D)

    def f_merge(rv, vv, i, nt):
        ga, gs, ya, ys = rv
        return [_sigmoid(ga) * ya + _sigmoid(gs) * ys], []

    merged = _rowwise(f_merge, [ga_row, gs_row, _row(y_attn), _row(y_ssm)], [], [(D, BF16)], [],
                      n_rows=S, tr=TR, ch=32, name="merge")[0]
    mo = _matmul(merged, w_out_f, mode="nn", dims=(S, D, D), tiles=(TW, 1024, D),
                 out_dtypes=[BF16], name="out_proj")

    ff_zones = _relay_gather_arrive(ff_flights, mo, "ff_weights_arrive")
    ff_pass, tok_fp = _relay_pass_start(ff_zones, "ff_weights_pass_start")

    def f_norm2(rv, vv, i, nt):
        xv, mv = rv
        g1v, g, sc, sh = vv
        x1v = xv + g1v * mv
        return [x1v, (x1v * _rms(x1v) * g) * (1.0 + sc) + sh], []

    x1, h2 = _rowwise(f_norm2, [_row(x2d), _row(mo)], [g1, norm2_g, sc2, sh2],
                      [(D, F32), (D, BF16)], [], n_rows=S, tr=TR, ch=32, name="norm2_fwd", dep=tok_fp)

    def relu_sq(acc):
        r = jnp.maximum(acc, 0.0)
        return r * r, r

    (G["w_ff1"],) = _relay_pass_wait(ff_pass[:1], h2, "w_ff1_pass_wait")
    act, relu = _matmul(h2, G["w_ff1"], mode="nn", dims=(S, D_FF, D), tiles=(TX, 1024, D),
                        out_dtypes=[BF16, BF16], name="ff1", b3=True, epilogue=relu_sq)
    w_ff2_f = _relay_pass_wait(ff_pass[1:], act, "w_ff2_pass_wait")[0].reshape(D_FF, D)
    ff = _matmul(act, w_ff2_f, mode="nn", dims=(S, D, D_FF), tiles=(TX, 1024, 2048),
                 out_dtypes=[BF16], name="ff2")

    def f_loss(rv, vv, i, nt):
        x1v, ffv, tv = rv
        g2v, gf = vv
        x2v = x1v + g2v * ffv
        r = _rms(x2v)
        xh = x2v * r
        diff = xh * gf - tv
        dy = diff * (1.0 / D)
        dxh = dy * gf
        dx2 = r * (dxh - xh * jnp.mean(dxh * xh, axis=-1, keepdims=True))
        return [dx2, dx2 * g2v], [_colsum(0.5 * diff * diff * (1.0 / D)), _colsum(dy * xh),
                                  _colsum(dx2 * ffv)]

    dx2, dff, loss_cols, d_final_g, dg2 = _rowwise(
        f_loss, [_row(x1), _row(ff), _row(tgt)], [g2, final_g.reshape(1, D)],
        [(D, F32), (D, BF16)], [(1, D)] * 3, n_rows=S, tr=TR, ch=32, name="loss_bwd")

    df1 = _matmul(dff, w_ff2_f, mode="nt", dims=(S, D_FF, D), tiles=(TX, 1024, D),
                  out_dtypes=[BF16], name="ff2_dgrad", extras=(relu,),
                  epilogue=lambda acc, r: (acc * (2.0 * r.astype(F32)),))
    gw_ff2 = _matmul(act, dff, mode="tn", dims=(D_FF, D, S), tiles=(2048, 1024, TS),
                     out_dtypes=[BF16], name="ff2_wgrad").reshape(N_DEV, D_FF // N_DEV, D)
    g_flight = {}
    (g_flight["w_ff2"],), tok = _exchange_start([gw_ff2], "scatter", "grads_start_ff2")
    dh2 = _matmul(df1, G["w_ff1"], mode="nt", dims=(S, D, D_FF), tiles=(TW, D, 2048),
                  out_dtypes=[BF16], name="ff1_dgrad", b3=True, dep=tok)
    gw_ff1 = _matmul(h2, df1, mode="tn", dims=(D, D_FF, S), tiles=(2048, 1024, TS),
                     out_dtypes=[BF16], name="ff1_wgrad", out3=True)
    (g_flight["w_ff1"],), tok = _exchange_start([gw_ff1], "scatter", "grads_start_ff1")

    def f_norm2_bwd(rv, vv, i, nt):
        x1v, dh, dx2v, mv = rv
        g, sc, g1v = vv
        r = _rms(x1v)
        xh = x1v * r
        t = xh * g
        dt = dh * (1.0 + sc)
        dxh = dt * g
        dx1 = dx2v + r * (dxh - xh * jnp.mean(dxh * xh, axis=-1, keepdims=True))
        return [dx1, dx1 * g1v], [_colsum(dh), _colsum(dh * t), _colsum(dt * xh), _colsum(dx1 * mv)]

    dx1, dmo, dsh2, dsc2, d_norm2_g, dg1 = _rowwise(
        f_norm2_bwd, [_row(x1), _row(dh2), _row(dx2), _row(mo)], [norm2_g, sc2, g1],
        [(D, F32), (D, BF16)], [(1, D)] * 4, n_rows=S, tr=TR, ch=16, name="norm2_bwd", dep=tok)

    dmerged = _matmul(dmo, w_out_f, mode="nt", dims=(S, D, D), tiles=(TW, 1024, D),
                      out_dtypes=[BF16], name="out_dgrad")
    gw_out = _matmul(merged, dmo, mode="tn", dims=(D, D, S), tiles=(2048, 1024, TS),
                     out_dtypes=[BF16], name="out_wgrad").reshape(N_DEV, D // N_DEV, D)
    (g_flight["w_out"],), tok = _exchange_start([gw_out], "scatter", "grads_start_out")

    def f_merge_bwd(rv, vv, i, nt):
        dm, ga, gs, ya, ys = rv
        sa, ss = _sigmoid(ga), _sigmoid(gs)
        return [dm * sa, dm * ss, dm * ya * sa * (1.0 - sa), dm * ys * ss * (1.0 - ss)], []

    dy_attn, dy_ssm, dga, dgs = _rowwise(
        f_merge_bwd, [_row(dmerged), ga_row, gs_row, _row(y_attn), _row(y_ssm)], [],
        [(D, BF16)] * 4, [], n_rows=S, tr=TR, ch=16, name="merge_bwd", dep=tok)

    dz2 = _matmul(dy_ssm, w_sp_f, mode="nt", dims=(S, SSM_W, D), tiles=(TW, SSM_W, D),
                  out_dtypes=[F32], name="ssm_proj_dgrad")
    gw_ssm_proj = _to_col_blocks(_matmul(z2, dy_ssm, mode="tn", dims=(SSM_W, D, S), tiles=(SSM_W, 1024, TS),
                                         out_dtypes=[BF16], name="ssm_proj_wgrad"))

    def f_glu_bwd(rv, vv, i, nt):
        dz2v, zv, zgv = rv
        sg = _sigmoid(zgv)
        dzg = dz2v * zv.astype(F32) * sg * (1.0 - sg)
        return [dzg, dz2v * sg], [_colsum(dzg)]

    dzg, dz_a, d_b_glu = _rowwise(f_glu_bwd, [_row(dz2), _row(z), _row(zg)], [],
                                  [(SSM_W, BF16), (SSM_W, F32)], [(1, SSM_W)],
                                  n_rows=S, tr=TM, ch=32, name="glu_bwd")
    dz_b = _matmul(dzg, w_glu_f, mode="nt", dims=(S, SSM_W, SSM_W), tiles=(TM, SSM_W, SSM_W),
                   out_dtypes=[F32], name="glu_dgrad")
    gw_glu = _matmul(z, dzg, mode="tn", dims=(SSM_W, SSM_W, S), tiles=(SSM_W, SSM_W, TS),
                     out_dtypes=[BF16], name="glu_wgrad").reshape(N_DEV, SSM_W // N_DEV, SSM_W)
    (g_flight["w_ssm_proj"], g_flight["w_glu"]), tok = _exchange_start(
        [gw_ssm_proj, gw_glu], "scatter", "grads_start_ssm")

    def f_ssm_out_bwd(rv, vv, i, nt):
        dza, dzb, yv, uv = rv
        dy = (dza + dzb) * _gelu_grad(yv)
        return [dy, dy * vv[0]], [_colsum(dy * uv)]

    dy_s, du_a, d_ssm_d = _rowwise(
        f_ssm_out_bwd, [_row(dz_a), _row(dz_b), _row(y_ssm_pre), _row(proj, u_blk, SSM_W)], [ssm_d],
        [(SSM_W, BF16), (SSM_W, F32)], [(1, SSM_W)], n_rows=S, tr=TM, ch=32, name="ssm_out_bwd", dep=tok)
    hw = SSM_W // 2
    u_half = (ATTN_W + 2 * KV_W) // hw
    dccat = _matmul(dy_s, xs, mode="tn", dims=(hw, 2 * SSM_H, S), tiles=(hw, 1024, TS),
                    out_dtypes=[F32], name="ssm_c_wgrad", a_index=lambda i, j, k: (k, j % 2))
    hs, dacc, du_b = _scan_bwd(dy_s, xprev, bcat, ccat, abar, S)
    dbcat = _matmul(proj, hs, mode="tn", dims=(hw, 2 * SSM_H, S), tiles=(hw, 1024, TS),
                    out_dtypes=[F32], name="ssm_b_wgrad", a_index=lambda i, j, k: (k, u_half + j % 2))
    grp = np.arange(SSM_H) // SSM_N
    gind = jnp.asarray((grp[:, None] == np.arange(128)[None, :]).astype(np.float32), BF16)
    d_lam_re, d_lam_im, d_ls, d_btr, d_bti, d_ctr, d_cti = _ssm_param_bwd(
        lam_re, lam_im, ls_x, btr, bti, dacc, dbcat, dccat, gind)

    dattn = _matmul(dy_attn, w_ap_f, mode="nt", dims=(S, ATTN_W, D), tiles=(TW, ATTN_W, D),
                    out_dtypes=[BF16], name="attn_proj_dgrad")
    gw_attn_proj = _to_col_blocks(_matmul(attn, dy_attn, mode="tn", dims=(ATTN_W, D, S), tiles=(ATTN_W, 1024, TS),
                                          out_dtypes=[BF16], name="attn_proj_wgrad"))
    (g_flight["w_attn_proj"],), tok = _exchange_start(
        [gw_attn_proj], "scatter", "grads_start_attn")
    dq, dkc, dkp, dvc, dvp, dbias, dsink = _attention_bwd(proj, attn, dattn, bias2, sinkcol, S)
    d_bias_b, d_sinks = _bucket_reduce(dbias.reshape(N_Q_HEADS, BLK * 2 * BLK),
                                       dsink.reshape(N_Q_HEADS, BLK), onehot_t)

    def f_dproj(rv, vv, i, nt):
        dqv, kc, kp, vc, vp, dua, dub, gav, gsv = rv
        keep = (i < nt - 1).astype(F32)
        dp = jnp.concatenate([dqv.astype(F32), kc + keep * kp, vc + keep * vp, dua + dub,
                              gav.astype(F32), gsv.astype(F32)], axis=-1)
        return [dp], [_colsum(dp)]

    dproj, d_b_in = _rowwise(
        f_dproj, [_row(dq), _row(dkc), _row(dkp, shift=1), _row(dvc), _row(dvp, shift=1),
                  _row(du_a), _row(du_b), _row(dga), _row(dgs)], [],
        [(IN_W, BF16)], [(1, IN_W)], n_rows=S, tr=BLK, ch=16, name="dproj", dep=tok)
    gw_in = _matmul(h, dproj, mode="tn", dims=(D, IN_W, S), tiles=(2048, 768, TS),
                    out_dtypes=[BF16], name="in_wgrad", out3=True)
    (g_flight["w_in"],), tok = _exchange_start([gw_in], "scatter", "grads_start_in")
    dh = _matmul(dproj, G["w_in"], mode="nt", dims=(S, D, IN_W), tiles=(TW, D, 1536),
                 out_dtypes=[BF16], name="in_dgrad", b3=True, dep=tok)

    def f_norm1_bwd(rv, vv, i, nt):
        xv, dhv, dx1v = rv
        g, sc = vv
        r = _rms(xv)
        xh = xv * r
        t = xh * g
        dt = dhv * (1.0 + sc)
        dxh = dt * g
        dxv = dx1v + r * (dxh - xh * jnp.mean(dxh * xh, axis=-1, keepdims=True))
        return [dxv], [_colsum(dhv), _colsum(dhv * t), _colsum(dt * xh)]

    grad_x, dsh1, dsc1, d_norm1_g = _rowwise(
        f_norm1_bwd, [_row(x2d), _row(dh), _row(dx1)], [norm1_g, sc1],
        [(D, F32)], [(1, D)] * 3, n_rows=S, tr=TR, ch=32, name="norm1_bwd")

    part = _pack({
        "b_ada": [dsh1, dsc1, dg1, dsh2, dsc2, dg2], "norm1_g": d_norm1_g, "b_in": d_b_in, "norm2_g": d_norm2_g,
        "final_g": d_final_g, "lambda_re": d_lam_re, "lambda_im": d_lam_im,
        "log_step": d_ls[0, :SSM_G], "attn_sinks": d_sinks[:, 0],
        "rel_bias": jnp.transpose(d_bias_b[:, :NUM_BUCKETS]), "b_glu": d_b_glu, "ssm_d": d_ssm_d,
        "loss": loss_cols, "ssm_b_re": d_btr, "ssm_b_im": d_bti, "ssm_c_re": d_ctr, "ssm_c_im": d_cti,
    })
    zone_small = lax.dynamic_update_slice(lax.empty((N_DEV, PACK_ROWS, PACK_W), F32), part[None], (me, 0, 0))
    (small_flight,), after = _exchange_start([zone_small], "gather", "small_grads_start")

    big_out = {}
    for n in ["w_ff2", "w_ff1", "w_out", "w_ssm_proj", "w_glu", "w_attn_proj", "w_in"]:
        own, recv = _exchange_wait([g_flight[n]], "scatter", after, "grads_wait_" + n[2:])[0]
        rows, cols = shard[n].shape
        parts = [(own, lambda m: m[0])] + [
            (recv, lambda m, j=j: jnp.where(j >= m[0], j + 1, j)) for j in range(N_DEV - 1)]
        big_out[n] = _adamw(parts, shard[n], Mo[n][0], Vo[n][0], tr=_adamw_rows(rows, cols), ch=16,
                            name="adamw_" + n, prefetch=me1)
        after = big_out[n][0]

    part_all = _exchange_wait([small_flight], "gather", after, "small_grads_wait")[0][0]
    sg, sdelta, sm, sv = _adamw([(part_all, d) for d in range(N_DEV)], wp, mp, vp,
                                tr=PACK_ROWS, ch=8, name="adamw_small")
    lo, _ = _PACK_OFF["loss"]
    loss = jnp.sum(sg[lo])

    o_ada, _ = _PACK_OFF["b_ada"]
    dmod_all = part_all[:, o_ada:o_ada + N_MOD, :].reshape(N_DEV, N_MOD * D)
    dmod_cols = lax.dynamic_slice(dmod_all, (0, me * n_ada), (N_DEV, n_ada))
    gw_ada = _matmul(cs, dmod_cols, mode="tn", dims=(D, n_ada, N_DEV), tiles=(D, 512, N_DEV),
                     out_dtypes=[F32], name="ada_wgrad")
    big_out["w_ada"] = _adamw([(gw_ada, 0)], w_ada[0], m_w_ada[0], v_w_ada[0],
                              tr=_adamw_rows(D, n_ada), ch=16, name="adamw_w_ada")

    def leaf(kind, n):
        if n in big_out:
            return big_out[n][kind][None]
        return _unpack_small((sg, sdelta, sm, sv)[kind], n)

    outs = [loss, grad_x.reshape(1, S, D)]
    for kind in range(4):
        outs.extend(leaf(kind, n) for n in WEIGHT_ORDER)
    return tuple(outs)
```

```python
import functools
import math

import numpy as np
import jax
import jax.numpy as jnp
from jax import lax
from jax.experimental import pallas as pl
from jax.experimental.pallas import tpu as pltpu

F32 = jnp.float32
BF16 = jnp.bfloat16
MESH = pl.DeviceIdType.MESH

N_DEV = 8
D = 2048
HEAD_DIM = 64
N_Q_HEADS = 16
N_KV_HEADS = 4
GROUP = N_Q_HEADS // N_KV_HEADS
ATTN_W = N_Q_HEADS * HEAD_DIM
KV_W = N_KV_HEADS * HEAD_DIM
BLK = 128
NUM_BUCKETS = 32
MAX_DISTANCE = 128
NEG_INF = -1e30
SSM_W = 512
SSM_P = 16
SSM_G = 32
SSM_N = 64
SSM_H = SSM_G * SSM_N
D_FF = 4 * D
IN_W = ATTN_W + 2 * KV_W + SSM_W + 2 * D
N_MOD = 6
EPS = 1e-6

ADAM_LR = 0.001
ADAM_B1 = 0.9
ADAM_B2 = 0.999
ADAM_EPS = 1e-08
ADAM_WD = 0.01
ADAM_STEP = 10

VMEM_LIMIT = 56 * 1024 * 1024
PACK_W = 2048


def _cparams(sem):
    return pltpu.CompilerParams(dimension_semantics=sem, vmem_limit_bytes=VMEM_LIMIT)


def _matmul(a, b, *, mode, dims, tiles, out_dtypes, name, a_off=0, b3=False,
            out3=False, bias=None, extras=(), epilogue=None, dep=None, a_index=None, b_index=None):
    M, N, K = dims
    tm, tn, tk = tiles
    assert M % tm == 0 and N % tn == 0 and K % tk == 0, (name, dims, tiles)
    gm, gn, gk = M // tm, N // tn, K // tk
    n_extra = len(extras)
    has_bias = bias is not None
    n_out = len(out_dtypes)

    if mode == "nn":
        a_spec = pl.BlockSpec((tm, tk), lambda i, j, k: (i, a_off + k))
        if b3:
            nb = (N // N_DEV) // tn
            assert nb * tn * N_DEV == N
            b_spec = pl.BlockSpec((None, tk, tn), lambda i, j, k: (j // nb, k, j % nb))
        else:
            b_spec = pl.BlockSpec((tk, tn), lambda i, j, k: (k, j))
        dn = (((1,), (0,)), ((), ()))
    elif mode == "nt":
        a_spec = pl.BlockSpec((tm, tk), lambda i, j, k: (i, a_off + k))
        if b3 and tk > K // N_DEV:
            shard_w = K // N_DEV
            shards_per_step = tk // shard_w
            assert shards_per_step * shard_w == tk
            b_spec = pl.BlockSpec((shards_per_step, tn, shard_w), lambda i, j, k: (k, j, 0))
        elif b3:
            nb = (K // N_DEV) // tk
            assert nb * tk * N_DEV == K
            b_spec = pl.BlockSpec((None, tn, tk), lambda i, j, k: (k // nb, j, k % nb))
        else:
            b_spec = pl.BlockSpec((tn, tk), lambda i, j, k: (j, k))
        dn = (((1,), (1,)), ((), ()))
    else:
        a_spec = pl.BlockSpec((tk, tm), lambda i, j, k: (k, a_off + i))
        b_spec = pl.BlockSpec((tk, tn), lambda i, j, k: (k, j))
        dn = (((0,), (0,)), ((), ()))
    if a_index is not None:
        a_spec = pl.BlockSpec(a_spec.block_shape, a_index)
    if b_index is not None:
        b_spec = pl.BlockSpec(b_spec.block_shape, b_index)

    if out3:
        nbo = (N // N_DEV) // tn
        assert nbo * tn * N_DEV == N
        o_spec = pl.BlockSpec((None, tm, tn), lambda i, j, k: (j // nbo, i, j % nbo))
        o_shape = (N_DEV, M, N // N_DEV)
    else:
        o_spec = pl.BlockSpec((tm, tn), lambda i, j, k: (i, j))
        o_shape = (M, N)

    in_specs = [a_spec, b_spec]
    args = [a, b]
    if has_bias:
        in_specs.append(pl.BlockSpec((1, tn), lambda i, j, k: (0, j)))
        args.append(bias)
    for e in extras:
        in_specs.append(pl.BlockSpec((tm, tn), lambda i, j, k: (i, j)))
        args.append(e)
    n_dep = 0 if dep is None else 1
    if n_dep:
        in_specs.append(pl.BlockSpec(memory_space=pl.ANY))
        args.append(dep)

    def body(*refs):
        a_ref, b_ref = refs[0], refs[1]
        pos = 2
        bias_ref = None
        if has_bias:
            bias_ref = refs[pos]
            pos += 1
        extra_refs = refs[pos:pos + n_extra]
        pos += n_extra + n_dep
        out_refs = refs[pos:pos + n_out]
        acc_ref = refs[pos + n_out] if gk > 1 else None

        if len(b_spec.block_shape) == 3 and b_spec.block_shape[0] is not None:
            sw = b_spec.block_shape[2]
            part = functools.reduce(lambda x, y: x + y, [
                lax.dot_general(a_ref[:, s * sw:(s + 1) * sw].astype(BF16), b_ref[s].astype(BF16), dn,
                                preferred_element_type=F32) for s in range(b_spec.block_shape[0])])
        else:
            part = lax.dot_general(a_ref[...].astype(BF16), b_ref[...].astype(BF16), dn,
                                   preferred_element_type=F32)

        def finish(acc):
            if has_bias:
                acc = acc + bias_ref[...]
            if epilogue is None:
                vals = (acc,)
            else:
                vals = epilogue(acc, *[e[...] for e in extra_refs])
            for o_ref, val in zip(out_refs, vals):
                o_ref[...] = val.astype(o_ref.dtype)

        if gk == 1:
            finish(part)
        else:
            k = pl.program_id(2)

            @pl.when(k == 0)
            def _():
                acc_ref[...] = part

            @pl.when(k > 0)
            def _():
                acc_ref[...] += part

            @pl.when(k == gk - 1)
            def _():
                finish(acc_ref[...])

    outs = pl.pallas_call(
        body,
        grid=(gm, gn, gk),
        in_specs=in_specs,
        out_specs=[o_spec] * n_out,
        out_shape=[jax.ShapeDtypeStruct(o_shape, dt) for dt in out_dtypes],
        scratch_shapes=([pltpu.VMEM((tm, tn), F32)] if gk > 1 else []),
        compiler_params=_cparams(("parallel", "parallel", "arbitrary")),
        name=name,
    )(*args)
    return outs[0] if n_out == 1 else outs


def _rowwise(fn, rows, vecs, row_outs, sum_outs, *, n_rows, tr, ch, name, dep=None, prefetch=None):
    assert n_rows % tr == 0 and tr % ch == 0
    nt = n_rows // tr
    nr, nv, nro, nso = len(rows), len(vecs), len(row_outs), len(sum_outs)
    in_specs, args = [], []
    n_pf = 0 if prefetch is None else 1
    for (arr, lead, cblk, w, shift) in rows:
        if shift:
            ridx = lambda i, shift=shift: jnp.minimum(i + shift, nt - 1)
        else:
            ridx = lambda i: i
        if arr.ndim == 3:
            def imap(i, *pf, lead=lead, cblk=cblk, ridx=ridx):
                return (lead(pf[0]) if callable(lead) else lead, ridx(i), cblk)
            in_specs.append(pl.BlockSpec((None, tr, w), imap))
        else:
            in_specs.append(pl.BlockSpec(
                (tr, w), lambda i, *pf, cblk=cblk, ridx=ridx: (ridx(i), cblk)))
        args.append(arr)
    for v in vecs:
        in_specs.append(pl.BlockSpec(v.shape, lambda i, *pf, nd=v.ndim: (0,) * nd))
        args.append(v)
    n_dep = 0 if dep is None else 1
    if n_dep:
        in_specs.append(pl.BlockSpec(memory_space=pl.ANY))
        args.append(dep)
    out_specs = [pl.BlockSpec((tr, w), lambda i, *pf: (i, 0)) for (w, _) in row_outs]
    out_shape = [jax.ShapeDtypeStruct((n_rows, w), dt) for (w, dt) in row_outs]
    for (r, w) in sum_outs:
        out_specs.append(pl.BlockSpec((r, w), lambda i, *pf: (0, 0)))
        out_shape.append(jax.ShapeDtypeStruct((r, w), F32))

    def body(*refs):
        refs = refs[n_pf:]
        i = pl.program_id(0)
        r_in = refs[:nr]
        v_in = refs[nr:nr + nv]
        r_out = refs[nr + nv + n_dep:nr + nv + n_dep + nro]
        s_out = refs[nr + nv + n_dep + nro:]
        s_out, s_acc = s_out[:nso], s_out[nso:]
        if nso:
            @pl.when(i == 0)
            def _():
                for s in s_acc:
                    s[...] = jnp.zeros(s.shape, F32)
        vvals = [v[...] for v in v_in]

        def chunk(ci, carry):
            r0 = pl.multiple_of(ci * ch, ch)
            rv = [r[pl.ds(r0, ch), :].astype(F32) for r in r_in]
            pieces = [fn([v[8 * k:8 * (k + 1)] for v in rv], vvals, i, nt) for k in range(ch // 8)]
            for j, ref in enumerate(r_out):
                val = jnp.concatenate([ro[j] for ro, _ in pieces], axis=0) if ch > 8 else pieces[0][0][j]
                ref[pl.ds(r0, ch), :] = val.astype(ref.dtype)
            for j, ref in enumerate(s_acc):
                ref[...] += functools.reduce(lambda a, b: a + b, [so[j] for _, so in pieces])
            return carry

        lax.fori_loop(0, tr // ch, chunk, 0)
        if nso:
            @pl.when(i == nt - 1)
            def _():
                for s, acc in zip(s_out, s_acc):
                    s[...] = jnp.sum(acc[...], axis=0, keepdims=True)

    outs = pl.pallas_call(
        body,
        grid_spec=pltpu.PrefetchScalarGridSpec(
            num_scalar_prefetch=n_pf, grid=(nt,), in_specs=in_specs, out_specs=out_specs,
            scratch_shapes=[pltpu.VMEM((8, w), F32) for (_, w) in sum_outs]),
        out_shape=out_shape,
        compiler_params=_cparams(("arbitrary",)),
        name=name,
    )(*([prefetch] if n_pf else []), *args)
    return outs


def _row(arr, cblk=0, w=None, lead=0, shift=0):
    return (arr, lead, cblk, arr.shape[-1] if w is None else w, shift)


def _colsum(v):
    parts = [v[8 * k:8 * (k + 1)] for k in range(v.shape[0] // 8)]
    return functools.reduce(lambda a, b: a + b, parts)


def _rms(x):
    return lax.rsqrt(jnp.mean(x * x, axis=-1, keepdims=True) + EPS)


def _sigmoid(x):
    return 1.0 / (1.0 + jnp.exp(-x))


_GELU_C = math.sqrt(2.0 / math.pi)


def _gelu(x):
    return 0.5 * x * (1.0 + jnp.tanh(_GELU_C * (x + 0.044715 * (x * x * x))))


def _gelu_grad(x):
    t = jnp.tanh(_GELU_C * (x + 0.044715 * (x * x * x)))
    return 0.5 * (1.0 + t) + 0.5 * x * (1.0 - t * t) * (_GELU_C * (1.0 + 3.0 * 0.044715 * (x * x)))


def _my_pos():
    return lax.axis_index("x"), lax.axis_index("y"), lax.axis_index("c")


def _flip(pos, k):
    x, y, c = pos
    return (1 - x if k & 4 else x, 1 - y if k & 2 else y, 1 - c if k & 1 else c)


def _dev_id(pos):
    return 4 * pos[0] + 2 * pos[1] + pos[2]


def _small_allgather(x, name):
    r, c = x.shape

    def body(x_ref, out_ref, send_sems, recv_sems):
        me = _my_pos()
        out_ref[_dev_id(me)] = x_ref[...]
        copies = []
        for k in range(1, N_DEV):
            cp = pltpu.make_async_remote_copy(
                src_ref=x_ref, dst_ref=out_ref.at[_dev_id(me)],
                send_sem=send_sems.at[k - 1], recv_sem=recv_sems.at[k - 1],
                device_id=_flip(me, k), device_id_type=MESH)
            cp.start()
            copies.append(cp)
        for k in range(1, N_DEV):
            peer = _flip(me, k)
            pltpu.make_async_remote_copy(
                src_ref=x_ref, dst_ref=out_ref.at[_dev_id(peer)],
                send_sem=send_sems.at[k - 1], recv_sem=recv_sems.at[k - 1],
                device_id=peer, device_id_type=MESH).wait_recv()
        for cp in copies:
            cp.wait_send()

    return pl.pallas_call(
        body,
        out_shape=jax.ShapeDtypeStruct((N_DEV, r, c), x.dtype),
        in_specs=[pl.BlockSpec(memory_space=pltpu.VMEM)],
        out_specs=pl.BlockSpec(memory_space=pltpu.VMEM),
        scratch_shapes=[pltpu.SemaphoreType.DMA((N_DEV - 1,)),
                        pltpu.SemaphoreType.DMA((N_DEV - 1,))],
        compiler_params=pltpu.CompilerParams(vmem_limit_bytes=VMEM_LIMIT),
        name=name,
    )(x)


_HBM = pl.BlockSpec(memory_space=pltpu.HBM)
_SEM = pl.BlockSpec(memory_space=pltpu.SEMAPHORE)
_EFFECT = pltpu.SideEffectType.DATAFLOW_SIDE_EFFECTING


def _relay_copy(zone, send_sems, recv_sems, k, block, to):
    slot = zone.at[_dev_id(block)]
    return pltpu.make_async_remote_copy(
        src_ref=slot, dst_ref=slot, send_sem=send_sems.at[k], recv_sem=recv_sems.at[k],
        device_id=to, device_id_type=MESH)


def _relay_peers():
    x, y, c = _my_pos()
    return (x, y, c), (x, y, 1 - c), [(1 - x, y), (x, 1 - y), (1 - x, 1 - y)]


def _relay_start_call(zones, n_sems, issue, name, after=None):
    n = len(zones)
    n_after = 0 if after is None else 1

    def body(*refs):
        refs = refs[:n] + refs[n + n_after:]
        send, recv, token = refs[n:2 * n], refs[2 * n:3 * n], refs[4 * n]
        for a in range(n):
            issue(refs[a], send[a], recv[a])
        token[...] = jnp.zeros(token.shape, token.dtype)

    sem = pltpu.SemaphoreType.DMA((n_sems,))
    outs = pl.pallas_call(
        body,
        name=name,
        out_shape=([sem] * (2 * n) + [pltpu.HBM(z.shape, z.dtype) for z in zones]
                   + [jax.ShapeDtypeStruct((8, 128), F32)]),
        in_specs=[_HBM] * n + [pl.BlockSpec(memory_space=pl.ANY)] * n_after,
        out_specs=[_SEM] * (2 * n) + [_HBM] * n + [pl.BlockSpec(memory_space=pltpu.VMEM)],
        input_output_aliases={a: 2 * n + a for a in range(n)},
        compiler_params=pltpu.CompilerParams(has_side_effects=_EFFECT),
    )(*[pltpu.with_memory_space_constraint(z, pltpu.HBM) for z in zones],
      *([after] if n_after else []))
    return [(outs[a], outs[n + a], outs[2 * n + a]) for a in range(n)], outs[3 * n]


def _relay_wait_call(flights, settle, after, name):
    n = len(flights)
    after = list(after) if isinstance(after, (list, tuple)) else [after]

    def body(*refs):
        send, recv = refs[n:2 * n], refs[2 * n:3 * n]
        for a in range(n):
            settle(refs[a], send[a], recv[a])

    outs = pl.pallas_call(
        body,
        name=name,
        out_shape=[pltpu.HBM(f[2].shape, f[2].dtype) for f in flights],
        in_specs=[_HBM] * n + [_SEM] * (2 * n) + [pl.BlockSpec(memory_space=pl.ANY)] * len(after),
        out_specs=[_HBM] * n,
        input_output_aliases={a: a for a in range(n)},
        compiler_params=pltpu.CompilerParams(has_side_effects=_EFFECT),
    )(*[f[2] for f in flights], *[f[0] for f in flights], *[f[1] for f in flights], *after)
    return list(outs)


def _relay_gather_start(zones, name, after=None):
    def issue(zone, send, recv):
        me, sib, chips = _relay_peers()
        _relay_copy(zone, send, recv, 0, me, sib).start()
        for j, chip in enumerate(chips):
            _relay_copy(zone, send, recv, 1 + j, me, (*chip, me[2])).start()
    return _relay_start_call(zones, 4, issue, name, after)


def _relay_gather_arrive(flights, after, name):
    def settle(zone, send, recv):
        me, sib, chips = _relay_peers()
        _relay_copy(zone, send, recv, 0, sib, me).wait_recv()
        _relay_copy(zone, send, recv, 0, me, sib).wait_send()
        for j, chip in enumerate(chips):
            _relay_copy(zone, send, recv, 1 + j, (*chip, me[2]), me).wait_recv()
            _relay_copy(zone, send, recv, 1 + j, me, (*chip, me[2])).wait_send()
    return _relay_wait_call(flights, settle, after, name)


def _relay_pass_start(zones, name, after=None):
    def issue(zone, send, recv):
        me, sib, chips = _relay_peers()
        for j, chip in enumerate(chips):
            _relay_copy(zone, send, recv, j, (*chip, me[2]), sib).start()
    return _relay_start_call(zones, 3, issue, name, after)


def _relay_pass_wait(flights, after, name):
    def settle(zone, send, recv):
        me, sib, chips = _relay_peers()
        for j, chip in enumerate(chips):
            _relay_copy(zone, send, recv, j, (*chip, sib[2]), me).wait_recv()
            _relay_copy(zone, send, recv, j, (*chip, me[2]), sib).wait_send()
    return _relay_wait_call(flights, settle, after, name)


def _exchange_copy(kind, bufs, send_sems, recv_sems, me, k, arriving):
    peer = _flip(me, k)
    my_id, peer_id = _dev_id(me), _dev_id(peer)
    if kind == "gather":
        slot = bufs[0].at[peer_id if arriving else my_id]
        src, dst = slot, slot
    else:
        src = bufs[0].at[my_id if arriving else peer_id]
        dst = bufs[1].at[peer_id if arriving else my_id]
    return pltpu.make_async_remote_copy(
        src_ref=src, dst_ref=dst, send_sem=send_sems.at[k - 1], recv_sem=recv_sems.at[k - 1],
        device_id=peer, device_id_type=MESH)


def _exchange_start(arrays, kind, name, after=None):
    n = len(arrays)
    n_after = 0 if after is None else 1
    if kind == "gather":
        bufs = [[a] for a in arrays]
    else:
        bufs = [[a, lax.empty(a.shape, a.dtype)] for a in arrays]
    nb = len(bufs[0])
    flat = [b for group in bufs for b in group]

    def body(*refs):
        outs_at = nb * n + n_after
        send = refs[outs_at:outs_at + n]
        recv = refs[outs_at + n:outs_at + 2 * n]
        token = refs[outs_at + 2 * n + nb * n]
        me = _my_pos()
        for a in range(n):
            for k in range(1, N_DEV):
                _exchange_copy(kind, refs[nb * a:nb * (a + 1)], send[a], recv[a], me, k, False).start()
        token[...] = jnp.zeros(token.shape, token.dtype)

    sem = pltpu.SemaphoreType.DMA((N_DEV - 1,))
    outs = pl.pallas_call(
        body,
        name=name,
        out_shape=([sem] * (2 * n) + [pltpu.HBM(b.shape, b.dtype) for b in flat]
                   + [jax.ShapeDtypeStruct((8, 128), F32)]),
        in_specs=[_HBM] * (nb * n) + [pl.BlockSpec(memory_space=pl.ANY)] * n_after,
        out_specs=[_SEM] * (2 * n) + [_HBM] * (nb * n) + [pl.BlockSpec(memory_space=pltpu.VMEM)],
        input_output_aliases={i: 2 * n + i for i in range(nb * n)},
        compiler_params=pltpu.CompilerParams(has_side_effects=_EFFECT),
    )(*[pltpu.with_memory_space_constraint(b, pltpu.HBM) for b in flat],
      *([after] if n_after else []))
    flights = [(outs[a], outs[n + a], list(outs[2 * n + nb * a:2 * n + nb * (a + 1)]))
               for a in range(n)]
    return flights, outs[2 * n + nb * n]


def _exchange_wait(flights, kind, after, name):
    n = len(flights)
    nb = len(flights[0][2])
    flat = [b for f in flights for b in f[2]]

    def body(*refs):
        send = refs[nb * n:nb * n + n]
        recv = refs[nb * n + n:nb * n + 2 * n]
        me = _my_pos()
        for a in range(n):
            for k in range(1, N_DEV):
                bufs = refs[nb * a:nb * (a + 1)]
                _exchange_copy(kind, bufs, send[a], recv[a], me, k, False).wait_send()
                _exchange_copy(kind, bufs, send[a], recv[a], me, k, True).wait_recv()

    outs = pl.pallas_call(
        body,
        name=name,
        out_shape=[pltpu.HBM(b.shape, b.dtype) for b in flat],
        in_specs=[_HBM] * (nb * n) + [_SEM] * (2 * n) + [pl.BlockSpec(memory_space=pl.ANY)],
        out_specs=[_HBM] * (nb * n),
        input_output_aliases={i: i for i in range(nb * n)},
        compiler_params=pltpu.CompilerParams(has_side_effects=_EFFECT),
    )(*flat, *[f[0] for f in flights], *[f[1] for f in flights], after)
    return [list(outs[nb * a:nb * (a + 1)]) for a in range(n)]


def _t5_buckets_block():
    qi = np.arange(BLK)[:, None]
    ki = np.arange(2 * BLK)[None, :]
    n = np.maximum(qi + BLK - ki, 0)
    max_exact = NUM_BUCKETS // 2
    large = max_exact + (np.log(np.maximum(n, 1) / max_exact)
                         / np.log(MAX_DISTANCE / max_exact)
                         * (NUM_BUCKETS - max_exact)).astype(np.int32)
    large = np.minimum(large, NUM_BUCKETS - 1)
    return np.where(n < max_exact, n, large).astype(np.int32)


def _band_mask():
    qi = np.arange(BLK)[:, None]
    ki = np.arange(2 * BLK)[None, :]
    dist = qi + BLK - ki
    return (dist >= 0) & (dist < BLK)


def _attn_scores(q_ref, kp_ref, kc_ref, hkv):
    c0 = hkv * HEAD_DIM
    kk = jnp.concatenate([kp_ref[:, c0:c0 + HEAD_DIM], kc_ref[:, c0:c0 + HEAD_DIM]],
                         axis=0).astype(BF16)
    qg = jnp.concatenate(
        [q_ref[:, (hkv * GROUP + g) * HEAD_DIM:(hkv * GROUP + g + 1) * HEAD_DIM]
         for g in range(GROUP)], axis=0).astype(BF16)
    s = lax.dot_general(qg, kk, (((1,), (1,)), ((), ())), preferred_element_type=F32)
    return qg, kk, s


def _attn_softmax(s, bias_ref, sink_ref, hkv):
    r0, r1 = hkv * GROUP * BLK, (hkv + 1) * GROUP * BLK
    s = s * (HEAD_DIM ** -0.5) + bias_ref[r0:r1, :]
    sink = sink_ref[r0:r1, :]
    m = jnp.maximum(jnp.max(s, axis=-1, keepdims=True), sink)
    p = jnp.exp(s - m)
    e_sink = jnp.exp(sink - m)
    inv = 1.0 / (jnp.sum(p, axis=-1, keepdims=True) + e_sink)
    return p * inv, e_sink * inv


def _kv_rows(p_ref, c_ref, hkv):
    c0 = hkv * HEAD_DIM
    return jnp.concatenate([p_ref[:, c0:c0 + HEAD_DIM], c_ref[:, c0:c0 + HEAD_DIM]],
                           axis=0).astype(BF16)


ATT_Q_FWD = 4
ATT_Q_BWD = 2


def _attn_in_specs(bias2, nq):
    prev = lambda n: jnp.maximum(nq * n - 1, 0)
    kcol = ATTN_W // KV_W
    return [
        pl.BlockSpec((nq * BLK, ATTN_W), lambda n: (n, 0)),
        pl.BlockSpec((BLK, KV_W), lambda n: (prev(n), kcol)),
        pl.BlockSpec((nq * BLK, KV_W), lambda n: (n, kcol)),
        pl.BlockSpec((BLK, KV_W), lambda n: (prev(n), kcol + 1)),
        pl.BlockSpec((nq * BLK, KV_W), lambda n: (n, kcol + 1)),
        pl.BlockSpec(bias2.shape, lambda n: (0, 0, 0)),
    ]


def _attn_views(t, q_ref, kp_ref, kc_ref, vp_ref, vc_ref, bias_ref):
    rows = pl.ds(t * BLK, BLK)
    before = pl.ds((t - 1) * BLK, BLK)
    table = jnp.minimum(pl.program_id(0), 1) if t == 0 else 1
    return (q_ref.at[rows, :],
            kp_ref if t == 0 else kc_ref.at[before, :], kc_ref.at[rows, :],
            vp_ref if t == 0 else vc_ref.at[before, :], vc_ref.at[rows, :],
            bias_ref.at[table])


def _attention_fwd(proj, bias2, sinkcol, n_rows):
    nq = min(ATT_Q_FWD, n_rows // BLK)
    steps = n_rows // (nq * BLK)

    def body(q_ref, kp_ref, kc_ref, vp_ref, vc_ref, bias_ref, sink_ref, o_ref):
        views = [_attn_views(t, q_ref, kp_ref, kc_ref, vp_ref, vc_ref, bias_ref) for t in range(nq)]
        work = [(t, hkv) for t in range(nq) for hkv in range(N_KV_HEADS)]
        scores = {w: _attn_scores(views[w[0]][0], views[w[0]][1], views[w[0]][2], w[1])[2] for w in work}
        probs = {w: _attn_softmax(scores[w], views[w[0]][5], sink_ref, w[1])[0] for w in work}
        outs = {w: jnp.dot(probs[w].astype(BF16), _kv_rows(views[w[0]][3], views[w[0]][4], w[1]),
                           preferred_element_type=F32) for w in work}
        for t, hkv in work:
            for g in range(GROUP):
                h = hkv * GROUP + g
                o_ref[t * BLK:(t + 1) * BLK, h * HEAD_DIM:(h + 1) * HEAD_DIM] = (
                    outs[t, hkv][g * BLK:(g + 1) * BLK, :].astype(o_ref.dtype))

    return pl.pallas_call(
        body,
        grid=(steps,),
        in_specs=_attn_in_specs(bias2, nq) + [pl.BlockSpec(sinkcol.shape, lambda n: (0, 0))],
        out_specs=pl.BlockSpec((nq * BLK, ATTN_W), lambda n: (n, 0)),
        out_shape=jax.ShapeDtypeStruct((n_rows, ATTN_W), BF16),
        compiler_params=_cparams(("parallel",)),
        name="attn_fwd",
    )(proj, proj, proj, proj, proj, bias2, sinkcol)


def _attention_bwd(proj, attn, dattn, bias2, sinkcol, n_rows):
    nq = min(ATT_Q_BWD, n_rows // BLK)
    steps = n_rows // (nq * BLK)
    scale = HEAD_DIM ** -0.5
    dn_t = (((0,), (0,)), ((), ()))

    def body(q_ref, kp_ref, kc_ref, vp_ref, vc_ref, bias_ref, o_ref, do_ref, sink_ref,
             dq_ref, dkc_ref, dkp_ref, dvc_ref, dvp_ref, dbias_ref, dsink_ref):
        @pl.when(pl.program_id(0) == 0)
        def _():
            dbias_ref[...] = jnp.zeros(dbias_ref.shape, F32)
            dsink_ref[...] = jnp.zeros(dsink_ref.shape, F32)

        views = [_attn_views(t, q_ref, kp_ref, kc_ref, vp_ref, vc_ref, bias_ref) for t in range(nq)]
        work = [(t, hkv) for t in range(nq) for hkv in range(N_KV_HEADS)]
        qk = {w: _attn_scores(views[w[0]][0], views[w[0]][1], views[w[0]][2], w[1]) for w in work}
        dog, dps, deltas = {}, {}, {}
        for t, hkv in work:
            rows = slice(t * BLK, (t + 1) * BLK)
            hs = [hkv * GROUP + g for g in range(GROUP)]
            d_o = jnp.concatenate([do_ref[rows, h * HEAD_DIM:(h + 1) * HEAD_DIM] for h in hs], axis=0)
            o = jnp.concatenate([o_ref[rows, h * HEAD_DIM:(h + 1) * HEAD_DIM] for h in hs], axis=0)
            deltas[t, hkv] = jnp.sum(d_o.astype(F32) * o.astype(F32), axis=-1, keepdims=True)
            dog[t, hkv] = d_o.astype(BF16)
            dps[t, hkv] = lax.dot_general(dog[t, hkv], _kv_rows(views[t][3], views[t][4], hkv),
                                          (((1,), (1,)), ((), ())), preferred_element_type=F32)
        p16, ds16 = {}, {}
        for t, hkv in work:
            r0, r1 = hkv * GROUP * BLK, (hkv + 1) * GROUP * BLK
            p, p_sink = _attn_softmax(qk[t, hkv][2], views[t][5], sink_ref, hkv)
            ds = p * (dps[t, hkv] - deltas[t, hkv])
            dbias_ref[r0:r1, :] += ds
            dsink_ref[r0:r1, :] += -(p_sink * deltas[t, hkv])
            p16[t, hkv] = p.astype(BF16)
            ds16[t, hkv] = ds.astype(BF16)
        for t, hkv in work:
            rows = slice(t * BLK, (t + 1) * BLK)
            c0 = hkv * HEAD_DIM
            qg, kk, _ = qk[t, hkv]
            dqg = jnp.dot(ds16[t, hkv], kk, preferred_element_type=F32) * scale
            dkk = lax.dot_general(ds16[t, hkv], qg, dn_t, preferred_element_type=F32) * scale
            dvv = lax.dot_general(p16[t, hkv], dog[t, hkv], dn_t, preferred_element_type=F32)
            for g in range(GROUP):
                h = hkv * GROUP + g
                dq_ref[rows, h * HEAD_DIM:(h + 1) * HEAD_DIM] = (
                    dqg[g * BLK:(g + 1) * BLK, :].astype(dq_ref.dtype))
            dkp_ref[rows, c0:c0 + HEAD_DIM] = dkk[:BLK].astype(dkp_ref.dtype)
            dkc_ref[rows, c0:c0 + HEAD_DIM] = dkk[BLK:].astype(dkc_ref.dtype)
            dvp_ref[rows, c0:c0 + HEAD_DIM] = dvv[:BLK].astype(dvp_ref.dtype)
            dvc_ref[rows, c0:c0 + HEAD_DIM] = dvv[BLK:].astype(dvc_ref.dtype)

    wide = pl.BlockSpec((nq * BLK, ATTN_W), lambda n: (n, 0))
    kv_out = pl.BlockSpec((nq * BLK, KV_W), lambda n: (n, 0))
    kv_shape = jax.ShapeDtypeStruct((n_rows, KV_W), F32)
    acc_shape = bias2.shape[1:]
    return pl.pallas_call(
        body,
        grid=(steps,),
        in_specs=_attn_in_specs(bias2, nq) + [wide, wide, pl.BlockSpec(sinkcol.shape, lambda n: (0, 0))],
        out_specs=[
            wide, kv_out, kv_out, kv_out, kv_out,
            pl.BlockSpec(acc_shape, lambda n: (0, 0)),
            pl.BlockSpec(sinkcol.shape, lambda n: (0, 0)),
        ],
        out_shape=[
            jax.ShapeDtypeStruct((n_rows, ATTN_W), BF16),
            kv_shape, kv_shape, kv_shape, kv_shape,
            jax.ShapeDtypeStruct(acc_shape, F32),
            jax.ShapeDtypeStruct(sinkcol.shape, F32),
        ],
        compiler_params=_cparams(("arbitrary",)),
        name="attn_bwd",
    )(proj, proj, proj, proj, proj, bias2, attn, dattn, sinkcol)


def _bias_tables(rel_bias_t, onehot_t, band_first, band_rest):
    def body(rb_ref, oh_ref, mf_ref, mr_ref, out_ref):
        acc = jnp.zeros((N_Q_HEADS, BLK * 2 * BLK), F32)
        for part in _split3(rb_ref[...]):
            acc = acc + jnp.dot(part, oh_ref[...], preferred_element_type=F32)
        out_ref[0] = jnp.where(mf_ref[...] > 0.0, acc, NEG_INF)
        out_ref[1] = jnp.where(mr_ref[...] > 0.0, acc, NEG_INF)

    return pl.pallas_call(
        body,
        out_shape=jax.ShapeDtypeStruct((2, N_Q_HEADS, BLK * 2 * BLK), F32),
        compiler_params=pltpu.CompilerParams(vmem_limit_bytes=VMEM_LIMIT),
        name="bias_tables",
    )(rel_bias_t, onehot_t, band_first, band_rest)


def _split3(a):
    hi = a.astype(BF16)
    r1 = a - hi.astype(F32)
    mid = r1.astype(BF16)
    lo = (r1 - mid.astype(F32)).astype(BF16)
    return hi, mid, lo


def _bucket_reduce(dbias, dsink, onehot_t):
    def body(db_ref, ds_ref, oh_ref, ob_ref, os_ref):
        acc = jnp.zeros((N_Q_HEADS, 128), F32)
        for part in _split3(db_ref[...]):
            acc = acc + lax.dot_general(part, oh_ref[...], (((1,), (1,)), ((), ())),
                                        preferred_element_type=F32)
        ob_ref[...] = acc
        os_ref[...] = jnp.broadcast_to(jnp.sum(ds_ref[...], axis=-1, keepdims=True),
                                       os_ref.shape)

    return pl.pallas_call(
        body,
        out_shape=[jax.ShapeDtypeStruct((N_Q_HEADS, 128), F32),
                   jax.ShapeDtypeStruct((N_Q_HEADS, 128), F32)],
        compiler_params=pltpu.CompilerParams(vmem_limit_bytes=VMEM_LIMIT),
        name="bias_bucket_reduce",
    )(dbias, dsink, onehot_t)


def _disc(lr, li, ls, btr, bti):
    lam_re = jnp.minimum(lr, -1e-4)
    delta = jnp.exp(ls)
    mag = jnp.exp(lam_re * delta)
    ang = li * delta
    ar, ai = mag * jnp.cos(ang), mag * jnp.sin(ang)
    nr, ni = ar - 1.0, ai
    den = lam_re * lam_re + li * li
    fr = (nr * lam_re + ni * li) / den
    fi = (ni * lam_re - nr * li) / den
    bbr = fr * btr - fi * bti
    bbi = fr * bti + fi * btr
    return ar, ai, bbr, bbi


def _block_mask():
    row = lax.broadcasted_iota(jnp.int32, (SSM_W, SSM_H), 0)
    col = lax.broadcasted_iota(jnp.int32, (SSM_W, SSM_H), 1)
    return (row // SSM_P) == (col // SSM_N)


def _ssm_setup(lr, li, ls, btr, bti, ctr, cti):
    def body(lr_ref, li_ref, ls_ref, btr_ref, bti_ref, ctr_ref, cti_ref, a_ref, b_ref, c_ref):
        ar, ai, bbr, bbi = _disc(lr_ref[...], li_ref[...], ls_ref[...], btr_ref[...], bti_ref[...])
        a_ref[:, :SSM_H] = ar
        a_ref[:, SSM_H:] = ai
        mask = _block_mask()
        blk = lambda t: jnp.where(mask, jnp.tile(t, (SSM_G, 1)), 0.0)
        b_ref[:, :SSM_H] = blk(bbr).astype(BF16)
        b_ref[:, SSM_H:] = blk(bbi).astype(BF16)
        c_ref[:, :SSM_H] = blk(ctr_ref[...]).astype(BF16)
        c_ref[:, SSM_H:] = blk(-cti_ref[...]).astype(BF16)

    return pl.pallas_call(
        body,
        out_shape=[jax.ShapeDtypeStruct((1, 2 * SSM_H), F32),
                   jax.ShapeDtypeStruct((SSM_W, 2 * SSM_H), BF16),
                   jax.ShapeDtypeStruct((SSM_W, 2 * SSM_H), BF16)],
        compiler_params=pltpu.CompilerParams(vmem_limit_bytes=VMEM_LIMIT),
        name="ssm_setup",
    )(lr, li, ls, btr, bti, ctr, cti)


def _ssm_param_bwd(lr, li, ls, btr, bti, dacc, dbcat, dccat, gind):
    def body(lr_ref, li_ref, ls_ref, btr_ref, bti_ref, dacc_ref, db_ref, dc_ref, g_ref,
             dlr_ref, dli_ref, dls_ref, dbtr_ref, dbti_ref, dctr_ref, dcti_ref):
        dar = jnp.sum(dacc_ref[:, :SSM_H], axis=0, keepdims=True)
        dai = jnp.sum(dacc_ref[:, SSM_H:], axis=0, keepdims=True)
        col = lax.broadcasted_iota(jnp.int32, (SSM_P, 2 * SSM_H), 1)
        grp = (col % SSM_H) // SSM_N
        db = jnp.zeros((SSM_P, 2 * SSM_H), F32)
        dc = jnp.zeros((SSM_P, 2 * SSM_H), F32)
        half = SSM_G // 2
        for g in range(SSM_G):
            sel = grp == g
            r0 = (g % half) * SSM_P
            db = db + jnp.where(sel, db_ref[r0:r0 + SSM_P, :], 0.0)
            dc = dc + jnp.where(sel, dc_ref[r0:r0 + SSM_P, :], 0.0)
        dctr_ref[...] = dc[:, :SSM_H]
        dcti_ref[...] = -dc[:, SSM_H:]
        prim = (lr_ref[...], li_ref[...], ls_ref[...], btr_ref[...], bti_ref[...])
        _, vjp = jax.vjp(_disc, *prim)
        dlr, dli, dls, dbtr, dbti = vjp((dar, dai, db[:, :SSM_H], db[:, SSM_H:]))
        dlr_ref[...] = dlr
        dli_ref[...] = dli
        dbtr_ref[...] = dbtr
        dbti_ref[...] = dbti
        acc = jnp.zeros((8, 128), F32)
        for part in _split3(jnp.broadcast_to(dls, (8, SSM_H))):
            acc = acc + jnp.dot(part, g_ref[...], preferred_element_type=F32)
        dls_ref[...] = acc

    vec = jax.ShapeDtypeStruct((1, SSM_H), F32)
    mat = jax.ShapeDtypeStruct((SSM_P, SSM_H), F32)
    return pl.pallas_call(
        body,
        out_shape=[vec, vec, jax.ShapeDtypeStruct((8, 128), F32), mat, mat, mat, mat],
        compiler_params=pltpu.CompilerParams(vmem_limit_bytes=VMEM_LIMIT),
        name="ssm_param_bwd",
    )(lr, li, ls, btr, bti, dacc, dbcat, dccat, gind)


SCAN_TR = 256


def _cmul_add(vr, vi, pr, pi, sr, si):
    return vr + pr * sr - pi * si, vi + pr * si + pi * sr


def _bcast_row(v, row, which):
    return jnp.broadcast_to(v[which:which + 1, :], v.shape)


def _scan_tables(a_ref, tab_ref, reverse):
    H = SSM_H
    ar = jnp.broadcast_to(a_ref[:, :H], (8, H))
    ai = jnp.broadcast_to(a_ref[:, H:], (8, H))
    if reverse:
        ai = -ai
    row = lax.broadcasted_iota(jnp.int32, (8, H), 0)
    pw = [(ar, ai)]
    for _ in range(7):
        cr, ci = pw[-1]
        pw.append((cr * ar - ci * ai, cr * ai + ci * ar))
    pcr = jnp.zeros((8, H), F32)
    pci = jnp.zeros((8, H), F32)
    for e in range(8):
        sel = (row == (7 - e)) if reverse else (row == e)
        pcr = jnp.where(sel, pw[e][0], pcr)
        pci = jnp.where(sel, pw[e][1], pci)
    tab_ref[0, :, :H] = pcr
    tab_ref[0, :, H:] = pci
    for t, k in enumerate((1, 2, 4)):
        keep = (row < 8 - k) if reverse else (row >= k)
        tab_ref[1 + t, :, :H] = jnp.where(keep, pw[k - 1][0], 0.0)
        tab_ref[1 + t, :, H:] = jnp.where(keep, pw[k - 1][1], 0.0)


def _scan_group(vr, vi, cr, ci, tab_ref, reverse):
    H = SSM_H
    for t, k in enumerate((1, 2, 4)):
        sh = 8 - k if reverse else k
        vr, vi = _cmul_add(vr, vi, tab_ref[1 + t, :, :H], tab_ref[1 + t, :, H:],
                           pltpu.roll(vr, sh, 0), pltpu.roll(vi, sh, 0))
    return _cmul_add(vr, vi, tab_ref[0, :, :H], tab_ref[0, :, H:], cr, ci)


def _blockdiag_expand(x, w_ref, out_ref):
    hw, cb = SSM_W // 2, SSM_H // 2
    for j in range(4):
        h = j % 2
        out_ref[:, j * cb:(j + 1) * cb] = jnp.dot(
            x[:, h * hw:(h + 1) * hw], w_ref[h * hw:(h + 1) * hw, j * cb:(j + 1) * cb],
            preferred_element_type=F32)


def _blockdiag_contract(x_ref, w_ref):
    hw, cb = SSM_W // 2, SSM_H // 2
    nt = (((1,), (1,)), ((), ()))
    halves = []
    for h in range(2):
        acc = None
        for j in (h, 2 + h):
            part = lax.dot_general(x_ref[:, j * cb:(j + 1) * cb],
                                   w_ref[h * hw:(h + 1) * hw, j * cb:(j + 1) * cb], nt,
                                   preferred_element_type=F32)
            acc = part if acc is None else acc + part
        halves.append(acc)
    return jnp.concatenate(halves, axis=1)


def _scan_fwd(proj, u_blk, bcat, ccat, abar, n_rows):
    H = SSM_H
    nt = n_rows // SCAN_TR

    def body(u_ref, b_ref, c_ref, a_ref, xs_ref, xp_ref, yc_ref, bu_ref, tab_ref, carry_ref):
        @pl.when(pl.program_id(0) == 0)
        def _():
            _scan_tables(a_ref, tab_ref, False)
            carry_ref[...] = jnp.zeros(carry_ref.shape, F32)

        _blockdiag_expand(u_ref[...].astype(BF16), b_ref, bu_ref)
        row = lax.broadcasted_iota(jnp.int32, (8, H), 0)

        def group(j, carry):
            cr, ci = carry
            r0 = pl.multiple_of(j * 16, 16)
            xr, xi = [], []
            for half in range(2):
                rr = pl.multiple_of(r0 + 8 * half, 8)
                vr, vi = _scan_group(bu_ref[pl.ds(rr, 8), :H], bu_ref[pl.ds(rr, 8), H:],
                                     cr, ci, tab_ref, False)
                xp_ref[pl.ds(rr, 8), :H] = jnp.where(row == 0, cr, pltpu.roll(vr, 1, 0))
                xp_ref[pl.ds(rr, 8), H:] = jnp.where(row == 0, ci, pltpu.roll(vi, 1, 0))
                cr, ci = _bcast_row(vr, row, 7), _bcast_row(vi, row, 7)
                xr.append(vr)
                xi.append(vi)
            xs_ref[pl.ds(r0, 16), :H] = jnp.concatenate(xr, axis=0).astype(BF16)
            xs_ref[pl.ds(r0, 16), H:] = jnp.concatenate(xi, axis=0).astype(BF16)
            return cr, ci

        cr, ci = lax.fori_loop(0, SCAN_TR // 16, group,
                               (carry_ref[:, :H], carry_ref[:, H:]))
        carry_ref[:, :H] = cr
        carry_ref[:, H:] = ci
        yc_ref[...] = _blockdiag_contract(xs_ref, c_ref)

    tile = lambda w: pl.BlockSpec((SCAN_TR, w), lambda i: (i, 0))
    whole = lambda a: pl.BlockSpec(a.shape, lambda i: (0, 0))
    return pl.pallas_call(
        body,
        grid=(nt,),
        in_specs=[pl.BlockSpec((SCAN_TR, SSM_W), lambda i: (i, u_blk)),
                  whole(bcat), whole(ccat), whole(abar)],
        out_specs=[tile(2 * H), tile(2 * H), tile(SSM_W)],
        out_shape=[jax.ShapeDtypeStruct((n_rows, 2 * H), BF16),
                   jax.ShapeDtypeStruct((n_rows, 2 * H), F32),
                   jax.ShapeDtypeStruct((n_rows, SSM_W), F32)],
        scratch_shapes=[pltpu.VMEM((SCAN_TR, 2 * H), F32), pltpu.VMEM((4, 8, 2 * H), F32),
                        pltpu.VMEM((8, 2 * H), F32)],
        compiler_params=_cparams(("arbitrary",)),
        name="ssm_scan_fwd",
    )(proj, bcat, ccat, abar)


def _scan_bwd(dy, xprev, bcat, ccat, abar, n_rows):
    H = SSM_H
    nt = n_rows // SCAN_TR

    def body(dy_ref, xp_ref, b_ref, c_ref, a_ref, h_ref, da_ref, du_ref, g_ref, tab_ref, carry_ref):
        @pl.when(pl.program_id(0) == 0)
        def _():
            _scan_tables(a_ref, tab_ref, True)
            carry_ref[...] = jnp.zeros(carry_ref.shape, F32)
            da_ref[...] = jnp.zeros(da_ref.shape, F32)

        _blockdiag_expand(dy_ref[...], c_ref, g_ref)
        row = lax.broadcasted_iota(jnp.int32, (8, H), 0)
        n16 = SCAN_TR // 16

        def group(jj, carry):
            cr, ci = carry
            r0 = pl.multiple_of((n16 - 1 - jj) * 16, 16)
            hr, hi = [None, None], [None, None]
            for half in (1, 0):
                rr = pl.multiple_of(r0 + 8 * half, 8)
                vr, vi = _scan_group(g_ref[pl.ds(rr, 8), :H], g_ref[pl.ds(rr, 8), H:],
                                     cr, ci, tab_ref, True)
                pr, pi = xp_ref[pl.ds(rr, 8), :H], xp_ref[pl.ds(rr, 8), H:]
                da_ref[:, :H] += vr * pr + vi * pi
                da_ref[:, H:] += vi * pr - vr * pi
                cr, ci = _bcast_row(vr, row, 0), _bcast_row(vi, row, 0)
                hr[half], hi[half] = vr, vi
            h_ref[pl.ds(r0, 16), :H] = jnp.concatenate(hr, axis=0).astype(BF16)
            h_ref[pl.ds(r0, 16), H:] = jnp.concatenate(hi, axis=0).astype(BF16)
            return cr, ci

        cr, ci = lax.fori_loop(0, n16, group, (carry_ref[:, :H], carry_ref[:, H:]))
        carry_ref[:, :H] = cr
        carry_ref[:, H:] = ci
        du_ref[...] = _blockdiag_contract(h_ref, b_ref)

    rev = lambda i: (nt - 1 - i, 0)
    whole = lambda a: pl.BlockSpec(a.shape, lambda i: (0, 0))
    return pl.pallas_call(
        body,
        grid=(nt,),
        in_specs=[pl.BlockSpec((SCAN_TR, SSM_W), rev),
                  pl.BlockSpec((SCAN_TR, 2 * H), rev),
                  whole(bcat), whole(ccat), whole(abar)],
        out_specs=[pl.BlockSpec((SCAN_TR, 2 * H), rev),
                   pl.BlockSpec((8, 2 * H), lambda i: (0, 0)),
                   pl.BlockSpec((SCAN_TR, SSM_W), rev)],
        out_shape=[jax.ShapeDtypeStruct((n_rows, 2 * H), BF16),
                   jax.ShapeDtypeStruct((8, 2 * H), F32),
                   jax.ShapeDtypeStruct((n_rows, SSM_W), F32)],
        scratch_shapes=[pltpu.VMEM((SCAN_TR, 2 * H), F32), pltpu.VMEM((4, 8, 2 * H), F32),
                        pltpu.VMEM((8, 2 * H), F32)],
        compiler_params=_cparams(("arbitrary",)),
        name="ssm_scan_bwd",
    )(dy, xprev, bcat, ccat, abar)


def _adamw(parts, w, m, v, *, tr, ch, name, prefetch=None):
    n_rows, cols = w.shape
    n_parts = len(parts)
    c1 = 1.0 - ADAM_B1 ** ADAM_STEP
    c2 = 1.0 - ADAM_B2 ** ADAM_STEP

    def fn(rv, vv, i, nt):
        g = rv[0].astype(F32)
        for p in rv[1:n_parts]:
            g = g + p.astype(F32)
        wv, mv, vval = rv[n_parts:]
        nm = ADAM_B1 * mv + (1.0 - ADAM_B1) * g
        nv = ADAM_B2 * vval + (1.0 - ADAM_B2) * (g * g)
        delta = -ADAM_LR * ((nm / c1) / (jnp.sqrt(nv / c2) + ADAM_EPS) + ADAM_WD * wv)
        return [g, delta, nm, nv], []

    rows = [_row(arr, lead=lead) for (arr, lead) in parts] + [_row(w), _row(m), _row(v)]
    return _rowwise(fn, rows, [], [(cols, F32)] * 4, [], n_rows=n_rows, tr=tr, ch=ch, name=name,
                    prefetch=prefetch)


_PACK = [
    ("b_ada", 6), ("norm1_g", 1), ("b_in", 3), ("norm2_g", 1), ("final_g", 1),
    ("lambda_re", 1), ("lambda_im", 1), ("log_step", 1), ("attn_sinks", 1),
    ("rel_bias", 1), ("b_glu", 1), ("ssm_d", 1), ("loss", 1),
    ("ssm_b_re", 16), ("ssm_b_im", 16), ("ssm_c_re", 16), ("ssm_c_im", 16),
]
_PACK_OFF = {}
_off = 0
for _n, _r in _PACK:
    _PACK_OFF[_n] = (_off, _r)
    _off += _r
PACK_ROWS = -(-_off // 8) * 8


def _to_rows(a, rows):
    flat = a.reshape(-1).astype(F32)
    pad = rows * PACK_W - flat.shape[0]
    if pad:
        flat = jnp.pad(flat, (0, pad))
    return flat.reshape(rows, PACK_W)


def _b_to_rows(b):
    return jnp.transpose(b, (2, 0, 1)).reshape(SSM_P, SSM_H)


def _rows_to_b(r):
    return jnp.transpose(r.reshape(SSM_P, SSM_G, SSM_N), (1, 2, 0))


def _c_to_rows(cm):
    return jnp.transpose(cm, (1, 0, 2)).reshape(SSM_P, SSM_H)


def _rows_to_c(r):
    return jnp.transpose(r.reshape(SSM_P, SSM_G, SSM_N), (1, 0, 2))


def _pack(vals):
    out = jnp.zeros((PACK_ROWS, PACK_W), F32)
    for n, r in _PACK:
        if n in vals:
            pieces = vals[n] if isinstance(vals[n], list) else [vals[n]]
            rows_each = r // len(pieces)
            for i, piece in enumerate(pieces):
                out = lax.dynamic_update_slice(out, _to_rows(piece, rows_each),
                                               (_PACK_OFF[n][0] + i * rows_each, 0))
    return out


def _unpack(packed, name, shape):
    o, r = _PACK_OFF[name]
    n = int(np.prod(shape))
    return packed[o:o + r].reshape(-1)[:n].reshape(shape)


def _small_params_packed(p):
    return {
        "b_ada": p["b_ada"], "norm1_g": p["norm1_g"], "b_in": p["b_in"],
        "norm2_g": p["norm2_g"], "final_g": p["final_g"],
        "lambda_re": p["lambda_re"], "lambda_im": p["lambda_im"],
        "log_step": p["log_step"], "attn_sinks": p["attn_sinks"],
        "rel_bias": p["rel_bias"], "b_glu": p["b_glu"], "ssm_d": p["ssm_d"],
        "ssm_b_re": _b_to_rows(p["ssm_b_re"][0]), "ssm_b_im": _b_to_rows(p["ssm_b_im"][0]),
        "ssm_c_re": _c_to_rows(p["ssm_c_re"][0]), "ssm_c_im": _c_to_rows(p["ssm_c_im"][0]),
    }


_SMALL_SHAPES = {
    "b_ada": (1, N_MOD * D), "norm1_g": (1, D), "b_in": (1, IN_W), "norm2_g": (1, D),
    "final_g": (D,), "lambda_re": (1, SSM_G, SSM_N), "lambda_im": (1, SSM_G, SSM_N),
    "log_step": (1, SSM_G), "attn_sinks": (1, N_Q_HEADS), "rel_bias": (NUM_BUCKETS, N_Q_HEADS),
    "b_glu": (1, SSM_W), "ssm_d": (1, SSM_W),
}


def _unpack_small(packed, name):
    if name in ("ssm_b_re", "ssm_b_im"):
        o, r = _PACK_OFF[name]
        return _rows_to_b(packed[o:o + r])[None]
    if name in ("ssm_c_re", "ssm_c_im"):
        o, r = _PACK_OFF[name]
        return _rows_to_c(packed[o:o + r])[None]
    return _unpack(packed, name, _SMALL_SHAPES[name])


WEIGHT_ORDER = ['w_ada', 'b_ada', 'norm1_g', 'w_in', 'b_in', 'attn_sinks', 'rel_bias', 'lambda_re',
                'lambda_im', 'log_step', 'ssm_b_re', 'ssm_b_im', 'ssm_c_re', 'ssm_c_im', 'ssm_d',
                'w_glu', 'b_glu', 'w_attn_proj', 'w_ssm_proj', 'w_out', 'norm2_g', 'w_ff1', 'w_ff2',
                'final_g']
BIG = ['w_in', 'w_glu', 'w_attn_proj', 'w_ssm_proj', 'w_out', 'w_ff1', 'w_ff2']


ADAMW_TILE_ELEMS = 1 << 18


def _to_col_blocks(w):
    k, n = w.shape
    return jnp.transpose(w.reshape(k, N_DEV, n // N_DEV), (1, 0, 2))


def _adamw_rows(rows, cols):
    tr = rows
    while tr * cols > ADAMW_TILE_ELEMS and tr % 32 == 0:
        tr //= 2
    return tr


def _cast_to_slot(w, me1, name, dep=None):
    rows, cols = w.shape
    tr = min(rows, 256)
    n_dep = 0 if dep is None else 1

    def body(me_ref, w_ref, *rest):
        rest[-1][...] = w_ref[...].astype(BF16)

    return pl.pallas_call(
        body,
        grid_spec=pltpu.PrefetchScalarGridSpec(
            num_scalar_prefetch=1, grid=(rows // tr,),
            in_specs=[pl.BlockSpec((tr, cols), lambda i, me_ref: (i, 0))]
            + [pl.BlockSpec(memory_space=pl.ANY)] * n_dep,
            out_specs=pl.BlockSpec((None, tr, cols), lambda i, me_ref: (me_ref[0], i, 0))),
        out_shape=jax.ShapeDtypeStruct((N_DEV, rows, cols), BF16),
        compiler_params=_cparams(("arbitrary",)),
        name=name,
    )(me1, w, *([dep] if n_dep else []))


def kernel(x, c, w_ada, b_ada, norm1_g, w_in, b_in, attn_sinks, rel_bias, lambda_re, lambda_im, log_step, ssm_b_re, ssm_b_im, ssm_c_re, ssm_c_im, ssm_d, w_glu, b_glu, w_attn_proj, w_ssm_proj, w_out, norm2_g, w_ff1, w_ff2, final_g, loss_target, m_w_ada, m_b_ada, m_norm1_g, m_w_in, m_b_in, m_attn_sinks, m_rel_bias, m_lambda_re, m_lambda_im, m_log_step, m_ssm_b_re, m_ssm_b_im, m_ssm_c_re, m_ssm_c_im, m_ssm_d, m_w_glu, m_b_glu, m_w_attn_proj, m_w_ssm_proj, m_w_out, m_norm2_g, m_w_ff1, m_w_ff2, m_final_g, v_w_ada, v_b_ada, v_norm1_g, v_w_in, v_b_in, v_attn_sinks, v_rel_bias, v_lambda_re, v_lambda_im, v_log_step, v_ssm_b_re, v_ssm_b_im, v_ssm_c_re, v_ssm_c_im, v_ssm_d, v_w_glu, v_b_glu, v_w_attn_proj, v_w_ssm_proj, v_w_out, v_norm2_g, v_w_ff1, v_w_ff2, v_final_g):
    loc = dict(locals())
    W = {n: loc[n] for n in WEIGHT_ORDER}
    Mo = {n: loc["m_" + n] for n in WEIGHT_ORDER}
    Vo = {n: loc["v_" + n] for n in WEIGHT_ORDER}
    S = x.shape[1]
    TM = min(512, S)
    TS = min(2048, S)
    TR = min(256, S)
    TW = min(1024, S)
    TX = min(2048, S)
    me = 4 * lax.axis_index("x") + 2 * lax.axis_index("y") + lax.axis_index("c")
    x2d = x.reshape(S, D)
    tgt = loss_target.reshape(S, D)

    c_all = _small_allgather(c, "allgather_c").reshape(N_DEV, D)
    cs = _rowwise(lambda rv, vv, i, nt: ([rv[0] * _sigmoid(rv[0])], []), [_row(c_all)], [],
                  [(D, F32)], [], n_rows=N_DEV, tr=8, ch=8, name="silu_c")[0]
    n_ada = N_MOD * D // N_DEV
    b_ada_cols = lax.dynamic_slice(b_ada, (0, me * n_ada), (1, n_ada))
    mod_piece = _matmul(cs, w_ada[0], mode="nn", dims=(N_DEV, n_ada, D), tiles=(N_DEV, 512, D),
                        out_dtypes=[F32], name="ada_fwd", bias=b_ada_cols)
    mod_all = _small_allgather(mod_piece, "allgather_mod")
    mod_b = lax.dynamic_index_in_dim(mod_all, me, axis=1, keepdims=False).reshape(N_MOD, D)
    sh1, sc1, g1, sh2, sc2, g2 = [mod_b[i:i + 1] for i in range(N_MOD)]

    shard = {n: W[n][0] for n in BIG}
    me1 = jnp.reshape(me, (1,)).astype(jnp.int32)
    zone = {"w_in": _cast_to_slot(shard["w_in"], me1, "cast_w_in")}
    (in_flight,), tok_in = _relay_gather_start([zone["w_in"]], "w_in_start", mod_all)
    for n in BIG[1:]:
        zone[n] = _cast_to_slot(shard[n], me1, "cast_" + n, dep=tok_in)
    G = {}

    buckets = _t5_buckets_block()
    band = _band_mask()
    onehot_t = jnp.asarray(
        (np.arange(128)[:, None] == buckets.reshape(-1)[None, :]).astype(np.float32), BF16)
    band_first = band & (np.arange(2 * BLK)[None, :] >= BLK)
    rel_bias_t = jnp.pad(jnp.transpose(rel_bias), ((0, 0), (0, 128 - NUM_BUCKETS)))
    bias2 = _bias_tables(rel_bias_t, onehot_t,
                         jnp.asarray(band_first.reshape(1, -1).astype(np.float32)),
                         jnp.asarray(band.reshape(1, -1).astype(np.float32))
                         ).reshape(2, N_Q_HEADS * BLK, 2 * BLK)
    sinkcol = jnp.repeat(attn_sinks.reshape(N_Q_HEADS), BLK).reshape(N_Q_HEADS * BLK, 1)
    lam_re = lambda_re.reshape(1, SSM_H)
    lam_im = lambda_im.reshape(1, SSM_H)
    ls_x = jnp.repeat(log_step.reshape(SSM_G), SSM_N).reshape(1, SSM_H)
    btr, bti = _b_to_rows(ssm_b_re[0]), _b_to_rows(ssm_b_im[0])
    ctr, cti = _c_to_rows(ssm_c_re[0]), _c_to_rows(ssm_c_im[0])
    abar, bcat, ccat = _ssm_setup(lam_re, lam_im, ls_x, btr, bti, ctr, cti)
    wp, mp, vp = [_pack(_small_params_packed(p)) for p in (W, Mo, Vo)]

    def f_norm1(rv, vv, i, nt):
        xv, (g, sc, sh) = rv[0], vv
        return [(xv * _rms(xv) * g) * (1.0 + sc) + sh], []

    h = _rowwise(f_norm1, [_row(x2d)], [norm1_g, sc1, sh1], [(D, BF16)], [],
                 n_rows=S, tr=TR, ch=32, name="norm1_fwd", dep=zone["w_ff2"])[0]
    (zone_in,) = _relay_gather_arrive([in_flight], [h, bias2, bcat, zone["w_ff1"], wp, mp, vp], "w_in_arrive")
    (in_pass,), tok_p = _relay_pass_start([zone_in], "w_in_pass_start")
    mixer = ["w_attn_proj", "w_glu", "w_ssm_proj", "w_out"]
    later_flights, tok_w = _relay_gather_start(
        [zone[n] for n in mixer] + [zone["w_ff1"], zone["w_ff2"]], "weights_start", tok_p)
    mixer_flights, ff_flights = later_flights[:len(mixer)], later_flights[len(mixer):]
    (G["w_in"],) = _relay_pass_wait([in_pass], tok_w, "w_in_pass_wait")
    proj = _matmul(h, G["w_in"], mode="nn", dims=(S, IN_W, D), tiles=(TX, 768, D),
                   out_dtypes=[BF16], name="in_proj", b3=True, bias=b_in, dep=tok_w)

    attn = _attention_fwd(proj, bias2, sinkcol, S)
    mixer_zones = _relay_gather_arrive(mixer_flights, attn, "mixer_weights_arrive")
    mixer_pass, _ = _relay_pass_start(mixer_zones, "mixer_weights_pass_start")

    u_blk = (ATTN_W + 2 * KV_W) // SSM_W
    xs, xprev, yc = _scan_fwd(proj, u_blk, bcat, ccat, abar, S)

    def f_ssm_out(rv, vv, i, nt):
        y = rv[0] + vv[0] * rv[1]
        return [y, _gelu(y)], []

    y_ssm_pre, z = _rowwise(f_ssm_out, [_row(yc), _row(proj, u_blk, SSM_W)], [ssm_d],
                            [(SSM_W, F32), (SSM_W, BF16)], [], n_rows=S, tr=TM, ch=32, name="ssm_out")
    G.update(zip(mixer, _relay_pass_wait(mixer_pass, z, "mixer_weights_pass_wait")))
    w_glu_f = G["w_glu"].reshape(SSM_W, SSM_W)
    w_out_f = G["w_out"].reshape(D, D)
    w_ap_f = jnp.transpose(G["w_attn_proj"], (1, 0, 2)).reshape(ATTN_W, D)
    w_sp_f = jnp.transpose(G["w_ssm_proj"], (1, 0, 2)).reshape(SSM_W, D)
    y_attn = _matmul(attn, w_ap_f, mode="nn", dims=(S, D, ATTN_W), tiles=(TW, 1024, ATTN_W),
                     out_dtypes=[BF16], name="attn_proj")
    zg = _matmul(z, w_glu_f, mode="nn", dims=(S, SSM_W, SSM_W), tiles=(TM, SSM_W, SSM_W),
                 out_dtypes=[F32], name="glu_proj", bias=b_glu)
    z2 = _rowwise(lambda rv, vv, i, nt: ([rv[0].astype(F32) * _sigmoid(rv[1])], []),
                  [_row(z), _row(zg)], [], [(SSM_W, BF16)], [], n_rows=S, tr=TM, ch=32, name="glu_gate")[0]
    y_ssm = _matmul(z2, w_sp_f, mode="nn", dims=(S, D, SSM_W), tiles=(TW, 1024, SSM_W),
                    out_dtypes=[BF16], name="ssm_proj")

    ga_row = _row(proj, 1, D)
    gs_row = _row(proj, 2, D)

    def f_merge(rv, vv, i, nt):
        ga, gs, ya, ys = rv
        return [_sigmoid(ga) * ya + _sigmoid(gs) * ys], []

    merged = _rowwise(f_merge, [ga_row, gs_row, _row(y_attn), _row(y_ssm)], [], [(D, BF16)], [],
                      n_rows=S, tr=TR, ch=32, name="merge")[0]
    mo = _matmul(merged, w_out_f, mode="nn", dims=(S, D, D), tiles=(TW, 1024, D),
                 out_dtypes=[BF16], name="out_proj")

    ff_zones = _relay_gather_arrive(ff_flights, mo, "ff_weights_arrive")
    ff_pass, tok_fp = _relay_pass_start(ff_zones, "ff_weights_pass_start")

    def f_norm2(rv, vv, i, nt):
        xv, mv = rv
        g1v, g, sc, sh = vv
        x1v = xv + g1v * mv
        return [x1v, (x1v * _rms(x1v) * g) * (1.0 + sc) + sh], []

    x1, h2 = _rowwise(f_norm2, [_row(x2d), _row(mo)], [g1, norm2_g, sc2, sh2],
                      [(D, F32), (D, BF16)], [], n_rows=S, tr=TR, ch=32, name="norm2_fwd", dep=tok_fp)

    def relu_sq(acc):
        r = jnp.maximum(acc, 0.0)
        return r * r, r

    (G["w_ff1"],) = _relay_pass_wait(ff_pass[:1], h2, "w_ff1_pass_wait")
    act, relu = _matmul(h2, G["w_ff1"], mode="nn", dims=(S, D_FF, D), tiles=(TX, 1024, D),
                        out_dtypes=[BF16, BF16], name="ff1", b3=True, epilogue=relu_sq)
    w_ff2_f = _relay_pass_wait(ff_pass[1:], act, "w_ff2_pass_wait")[0].reshape(D_FF, D)
    ff = _matmul(act, w_ff2_f, mode="nn", dims=(S, D, D_FF), tiles=(TW, 1024, 4096),
                 out_dtypes=[BF16], name="ff2")

    def f_loss(rv, vv, i, nt):
        x1v, ffv, tv = rv
        g2v, gf = vv
        x2v = x1v + g2v * ffv
        r = _rms(x2v)
        xh = x2v * r
        diff = xh * gf - tv
        dy = diff * (1.0 / D)
        dxh = dy * gf
        dx2 = r * (dxh - xh * jnp.mean(dxh * xh, axis=-1, keepdims=True))
        return [dx2, dx2 * g2v], [_colsum(0.5 * diff * diff * (1.0 / D)), _colsum(dy * xh),
                                  _colsum(dx2 * ffv)]

    dx2, dff, loss_cols, d_final_g, dg2 = _rowwise(
        f_loss, [_row(x1), _row(ff), _row(tgt)], [g2, final_g.reshape(1, D)],
        [(D, F32), (D, BF16)], [(1, D)] * 3, n_rows=S, tr=TR, ch=32, name="loss_bwd")

    df1 = _matmul(dff, w_ff2_f, mode="nt", dims=(S, D_FF, D), tiles=(TX, 1024, D),
                  out_dtypes=[BF16], name="ff2_dgrad", extras=(relu,),
                  epilogue=lambda acc, r: (acc * (2.0 * r.astype(F32)),))
    gw_ff2 = _matmul(act, dff, mode="tn", dims=(D_FF, D, S), tiles=(2048, 1024, TS),
                     out_dtypes=[BF16], name="ff2_wgrad").reshape(N_DEV, D_FF // N_DEV, D)
    g_flight = {}
    (g_flight["w_ff2"],), tok = _exchange_start([gw_ff2], "scatter", "grads_start_ff2")
    dh2 = _matmul(df1, G["w_ff1"], mode="nt", dims=(S, D, D_FF), tiles=(TW, D, 2048),
                  out_dtypes=[BF16], name="ff1_dgrad", b3=True, dep=tok)
    gw_ff1 = _matmul(h2, df1, mode="tn", dims=(D, D_FF, S), tiles=(2048, 1024, TS),
                     out_dtypes=[BF16], name="ff1_wgrad", out3=True)
    (g_flight["w_ff1"],), tok = _exchange_start([gw_ff1], "scatter", "grads_start_ff1")

    def f_norm2_bwd(rv, vv, i, nt):
        x1v, dh, dx2v, mv = rv
        g, sc, g1v = vv
        r = _rms(x1v)
        xh = x1v * r
        t = xh * g
        dt = dh * (1.0 + sc)
        dxh = dt * g
        dx1 = dx2v + r * (dxh - xh * jnp.mean(dxh * xh, axis=-1, keepdims=True))
        return [dx1, dx1 * g1v], [_colsum(dh), _colsum(dh * t), _colsum(dt * xh), _colsum(dx1 * mv)]

    dx1, dmo, dsh2, dsc2, d_norm2_g, dg1 = _rowwise(
        f_norm2_bwd, [_row(x1), _row(dh2), _row(dx2), _row(mo)], [norm2_g, sc2, g1],
        [(D, F32), (D, BF16)], [(1, D)] * 4, n_rows=S, tr=TR, ch=16, name="norm2_bwd", dep=tok)

    dmerged = _matmul(dmo, w_out_f, mode="nt", dims=(S, D, D), tiles=(TW, 1024, D),
                      out_dtypes=[BF16], name="out_dgrad")
    gw_out = _matmul(merged, dmo, mode="tn", dims=(D, D, S), tiles=(2048, 1024, TS),
                     out_dtypes=[BF16], name="out_wgrad").reshape(N_DEV, D // N_DEV, D)
    (g_flight["w_out"],), tok = _exchange_start([gw_out], "scatter", "grads_start_out")

    def f_merge_bwd(rv, vv, i, nt):
        dm, ga, gs, ya, ys = rv
        sa, ss = _sigmoid(ga), _sigmoid(gs)
        return [dm * sa, dm * ss, dm * ya * sa * (1.0 - sa), dm * ys * ss * (1.0 - ss)], []

    dy_attn, dy_ssm, dga, dgs = _rowwise(
        f_merge_bwd, [_row(dmerged), ga_row, gs_row, _row(y_attn), _row(y_ssm)], [],
        [(D, BF16)] * 4, [], n_rows=S, tr=TR, ch=16, name="merge_bwd", dep=tok)

    dz2 = _matmul(dy_ssm, w_sp_f, mode="nt", dims=(S, SSM_W, D), tiles=(TW, SSM_W, D),
                  out_dtypes=[F32], name="ssm_proj_dgrad")
    gw_ssm_proj = _to_col_blocks(_matmul(z2, dy_ssm, mode="tn", dims=(SSM_W, D, S), tiles=(SSM_W, 1024, TS),
                                         out_dtypes=[BF16], name="ssm_proj_wgrad"))

    def f_glu_bwd(rv, vv, i, nt):
        dz2v, zv, zgv = rv
        sg = _sigmoid(zgv)
        dzg = dz2v * zv.astype(F32) * sg * (1.0 - sg)
        return [dzg, dz2v * sg], [_colsum(dzg)]

    dzg, dz_a, d_b_glu = _rowwise(f_glu_bwd, [_row(dz2), _row(z), _row(zg)], [],
                                  [(SSM_W, BF16), (SSM_W, F32)], [(1, SSM_W)],
                                  n_rows=S, tr=TM, ch=32, name="glu_bwd")
    dz_b = _matmul(dzg, w_glu_f, mode="nt", dims=(S, SSM_W, SSM_W), tiles=(TM, SSM_W, SSM_W),
                   out_dtypes=[F32], name="glu_dgrad")
    gw_glu = _matmul(z, dzg, mode="tn", dims=(SSM_W, SSM_W, S), tiles=(SSM_W, SSM_W, TS),
                     out_dtypes=[BF16], name="glu_wgrad").reshape(N_DEV, SSM_W // N_DEV, SSM_W)
    (g_flight["w_ssm_proj"], g_flight["w_glu"]), tok = _exchange_start(
        [gw_ssm_proj, gw_glu], "scatter", "grads_start_ssm")

    def f_ssm_out_bwd(rv, vv, i, nt):
        dza, dzb, yv, uv = rv
        dy = (dza + dzb) * _gelu_grad(yv)
        return [dy, dy * vv[0]], [_colsum(dy * uv)]

    dy_s, du_a, d_ssm_d = _rowwise(
        f_ssm_out_bwd, [_row(dz_a), _row(dz_b), _row(y_ssm_pre), _row(proj, u_blk, SSM_W)], [ssm_d],
        [(SSM_W, BF16), (SSM_W, F32)], [(1, SSM_W)], n_rows=S, tr=TM, ch=32, name="ssm_out_bwd", dep=tok)
    hw = SSM_W // 2
    u_half = (ATTN_W + 2 * KV_W) // hw
    dccat = _matmul(dy_s, xs, mode="tn", dims=(hw, 2 * SSM_H, S), tiles=(hw, 1024, TS),
                    out_dtypes=[F32], name="ssm_c_wgrad", a_index=lambda i, j, k: (k, j % 2))
    hs, dacc, du_b = _scan_bwd(dy_s, xprev, bcat, ccat, abar, S)
    dbcat = _matmul(proj, hs, mode="tn", dims=(hw, 2 * SSM_H, S), tiles=(hw, 1024, TS),
                    out_dtypes=[F32], name="ssm_b_wgrad", a_index=lambda i, j, k: (k, u_half + j % 2))
    grp = np.arange(SSM_H) // SSM_N
    gind = jnp.asarray((grp[:, None] == np.arange(128)[None, :]).astype(np.float32), BF16)
    d_lam_re, d_lam_im, d_ls, d_btr, d_bti, d_ctr, d_cti = _ssm_param_bwd(
        lam_re, lam_im, ls_x, btr, bti, dacc, dbcat, dccat, gind)

    dattn = _matmul(dy_attn, w_ap_f, mode="nt", dims=(S, ATTN_W, D), tiles=(TW, ATTN_W, D),
                    out_dtypes=[BF16], name="attn_proj_dgrad")
    gw_attn_proj = _to_col_blocks(_matmul(attn, dy_attn, mode="tn", dims=(ATTN_W, D, S), tiles=(ATTN_W, 1024, TS),
                                          out_dtypes=[BF16], name="attn_proj_wgrad"))
    (g_flight["w_attn_proj"],), tok = _exchange_start(
        [gw_attn_proj], "scatter", "grads_start_attn")
    dq, dkc, dkp, dvc, dvp, dbias, dsink = _attention_bwd(proj, attn, dattn, bias2, sinkcol, S)
    d_bias_b, d_sinks = _bucket_reduce(dbias.reshape(N_Q_HEADS, BLK * 2 * BLK),
                                       dsink.reshape(N_Q_HEADS, BLK), onehot_t)

    def f_dproj(rv, vv, i, nt):
        dqv, kc, kp, vc, vp, dua, dub, gav, gsv = rv
        keep = (i < nt - 1).astype(F32)
        dp = jnp.concatenate([dqv.astype(F32), kc + keep * kp, vc + keep * vp, dua + dub,
                              gav.astype(F32), gsv.astype(F32)], axis=-1)
        return [dp], [_colsum(dp)]

    dproj, d_b_in = _rowwise(
        f_dproj, [_row(dq), _row(dkc), _row(dkp, shift=1), _row(dvc), _row(dvp, shift=1),
                  _row(du_a), _row(du_b), _row(dga), _row(dgs)], [],
        [(IN_W, BF16)], [(1, IN_W)], n_rows=S, tr=BLK, ch=16, name="dproj", dep=tok)
    gw_in = _matmul(h, dproj, mode="tn", dims=(D, IN_W, S), tiles=(2048, 768, TS),
                    out_dtypes=[BF16], name="in_wgrad", out3=True)
    (g_flight["w_in"],), tok = _exchange_start([gw_in], "scatter", "grads_start_in")
    dh = _matmul(dproj, G["w_in"], mode="nt", dims=(S, D, IN_W), tiles=(TW, D, 1536),
                 out_dtypes=[BF16], name="in_dgrad", b3=True, dep=tok)

    def f_norm1_bwd(rv, vv, i, nt):
        xv, dhv, dx1v = rv
        g, sc = vv
        r = _rms(xv)
        xh = xv * r
        t = xh * g
        dt = dhv * (1.0 + sc)
        dxh = dt * g
        dxv = dx1v + r * (dxh - xh * jnp.mean(dxh * xh, axis=-1, keepdims=True))
        return [dxv], [_colsum(dhv), _colsum(dhv * t), _colsum(dt * xh)]

    grad_x, dsh1, dsc1, d_norm1_g = _rowwise(
        f_norm1_bwd, [_row(x2d), _row(dh), _row(dx1)], [norm1_g, sc1],
        [(D, F32)], [(1, D)] * 3, n_rows=S, tr=TR, ch=32, name="norm1_bwd")

    part = _pack({
        "b_ada": [dsh1, dsc1, dg1, dsh2, dsc2, dg2], "norm1_g": d_norm1_g, "b_in": d_b_in, "norm2_g": d_norm2_g,
        "final_g": d_final_g, "lambda_re": d_lam_re, "lambda_im": d_lam_im,
        "log_step": d_ls[0, :SSM_G], "attn_sinks": d_sinks[:, 0],
        "rel_bias": jnp.transpose(d_bias_b[:, :NUM_BUCKETS]), "b_glu": d_b_glu, "ssm_d": d_ssm_d,
        "loss": loss_cols, "ssm_b_re": d_btr, "ssm_b_im": d_bti, "ssm_c_re": d_ctr, "ssm_c_im": d_cti,
    })
    zone_small = lax.dynamic_update_slice(lax.empty((N_DEV, PACK_ROWS, PACK_W), F32), part[None], (me, 0, 0))
    (small_flight,), after = _exchange_start([zone_small], "gather", "small_grads_start")

    big_out = {}
    for n in ["w_ff2", "w_ff1", "w_out", "w_ssm_proj", "w_glu", "w_attn_proj", "w_in"]:
        own, recv = _exchange_wait([g_flight[n]], "scatter", after, "grads_wait_" + n[2:])[0]
        rows, cols = shard[n].shape
        parts = [(own, lambda m: m[0])] + [
            (recv, lambda m, j=j: jnp.where(j >= m[0], j + 1, j)) for j in range(N_DEV - 1)]
        big_out[n] = _adamw(parts, shard[n], Mo[n][0], Vo[n][0], tr=_adamw_rows(rows, cols), ch=16,
                            name="adamw_" + n, prefetch=me1)
        after = big_out[n][0]

    part_all = _exchange_wait([small_flight], "gather", after, "small_grads_wait")[0][0]
    sg, sdelta, sm, sv = _adamw([(part_all, d) for d in range(N_DEV)], wp, mp, vp,
                                tr=PACK_ROWS, ch=8, name="adamw_small")
    lo, _ = _PACK_OFF["loss"]
    loss = jnp.sum(sg[lo])

    o_ada, _ = _PACK_OFF["b_ada"]
    dmod_all = part_all[:, o_ada:o_ada + N_MOD, :].reshape(N_DEV, N_MOD * D)
    dmod_cols = lax.dynamic_slice(dmod_all, (0, me * n_ada), (N_DEV, n_ada))
    gw_ada = _matmul(cs, dmod_cols, mode="tn", dims=(D, n_ada, N_DEV), tiles=(D, 512, N_DEV),
                     out_dtypes=[F32], name="ada_wgrad")
    big_out["w_ada"] = _adamw([(gw_ada, 0)], w_ada[0], m_w_ada[0], v_w_ada[0],
                              tr=_adamw_rows(D, n_ada), ch=16, name="adamw_w_ada")

    def leaf(kind, n):
        if n in big_out:
            return big_out[n][kind][None]
        return _unpack_small((sg, sdelta, sm, sv)[kind], n)

    outs = [loss, grad_x.reshape(1, S, D)]
    for kind in range(4):
        outs.extend(leaf(kind, n) for n in WEIGHT_ORDER)
    return tuple(outs)
```
